```python
import jax, jax.numpy as jnp
from jax import lax
import numpy as np

D_MODEL = 1024
BATCH = 8
SEQ = 8192
DEPTH = 4

N_MIXERS = 2
N_HGRN_LAYERS = (DEPTH + 1) // 2
N_MLA_LAYERS = DEPTH // 2
RMS_EPS = 1e-6

HGRN_EXPAND = 128
HGRN_HEADS = D_MODEL // HGRN_EXPAND
HGRN_VDIM = D_MODEL // HGRN_HEADS
HGRN_CHUNK = 64

MLA_HEADS = D_MODEL // 128
MLA_NOPE = 128
MLA_ROPE = 64
MLA_QK_HEAD = MLA_NOPE + MLA_ROPE
MLA_V_HEAD = 128
MLA_Q_LORA = D_MODEL // 4
MLA_KV_LORA = D_MODEL // 4
ROPE_THETA = 10000.0
Q_BLOCK = 128

D_FF = 2816
FFN_CONV = 3

kernel_name = "hybrid_hgrn2_mla_convffn_trunk"


def _rmsnorm(x, gain):
    xf = x.astype(jnp.float32)
    y = xf * lax.rsqrt(jnp.mean(xf * xf, axis=-1, keepdims=True) + RMS_EPS)
    return (y * gain.astype(jnp.float32)).astype(x.dtype)


def _chunk_gated_recurrence(q, k, v, log_f):
    B, S, H, Dk = q.shape
    Dv = v.shape[-1]
    C = HGRN_CHUNK
    N = S // C

    def to_chunks(t):
        return t.astype(jnp.float32).reshape(B, N, C, H, t.shape[-1]).transpose(1, 0, 3, 2, 4)

    causal = jnp.tril(jnp.ones((C, C), dtype=bool))[:, :, None]

    def step(state, inp):
        qc, kc, vc, gc = inp
        G = jnp.cumsum(gc, axis=2)
        o_inter = jnp.einsum("bhtd,bhde->bhte", qc * jnp.exp(G), state)
        diff = G[:, :, :, None, :] - G[:, :, None, :, :]
        decay = jnp.where(causal, jnp.exp(jnp.where(causal, diff, 0.0)), 0.0)
        scores = jnp.einsum("bhtd,bhsd,bhtsd->bhts", qc, kc, decay)
        o_intra = jnp.einsum("bhts,bhse->bhte", scores, vc)
        G_last = G[:, :, -1:, :]
        new_state = state * jnp.exp(G_last[:, :, 0, :, None]) + jnp.einsum(
            "bhsd,bhse->bhde", kc * jnp.exp(G_last - G), vc)
        return new_state, o_inter + o_intra

    state0 = jnp.zeros((B, H, Dk, Dv), jnp.float32)
    _, o = lax.scan(step, state0, (to_chunks(q), to_chunks(k), to_chunks(v), to_chunks(log_f)))
    return o.transpose(1, 0, 3, 2, 4).reshape(B, S, H, Dv)


def _hgrn2_mixer(h, w_in, lower_bound, out_norm, w_out):
    B, S, _ = h.shape
    q, f, i, g = jnp.split(h @ w_in, 4, axis=-1)

    def heads(t):
        return t.reshape(B, S, HGRN_HEADS, -1)

    lb = lower_bound.astype(jnp.float32)
    forget = lb + (1.0 - lb) * jax.nn.sigmoid(f.astype(jnp.float32))
    key_in = 1.0 - forget
    log_f = jnp.log(forget)
    o = _chunk_gated_recurrence(heads(jax.nn.silu(q)), heads(key_in), heads(i), heads(log_f))
    o = _rmsnorm(o, out_norm) * jax.nn.silu(heads(g.astype(jnp.float32)))
    return o.reshape(B, S, D_MODEL).astype(h.dtype) @ w_out


def _rope_tail(x, cos, sin):
    x_nope, x1, x2 = jnp.split(x, [MLA_NOPE, MLA_NOPE + MLA_ROPE // 2], axis=-1)
    xf1 = x1.astype(jnp.float32)
    xf2 = x2.astype(jnp.float32)
    rot = jnp.concatenate([xf1 * cos - xf2 * sin, xf2 * cos + xf1 * sin], axis=-1).astype(x.dtype)
    return jnp.concatenate([x_nope, rot], axis=-1)


def _causal_block_attention(q, k, v):
    B, S, H, Dqk = q.shape
    nb = S // Q_BLOCK
    scale = Dqk ** -0.5
    q_blocks = q.reshape(B, nb, Q_BLOCK, H, Dqk).transpose(1, 0, 3, 2, 4)
    k_t = k.transpose(0, 2, 1, 3)
    v_t = v.transpose(0, 2, 1, 3)
    key_pos = jnp.arange(S)

    def attend(args):
        q_blk, blk = args
        s = jnp.einsum("bhqd,bhkd->bhqk", q_blk, k_t).astype(jnp.float32) * scale
        q_pos = blk * Q_BLOCK + jnp.arange(Q_BLOCK)
        s = jnp.where(key_pos[None, :] <= q_pos[:, None], s, -jnp.inf)
        p = jax.nn.softmax(s, axis=-1).astype(v_t.dtype)
        return jnp.einsum("bhqk,bhkv->bhqv", p, v_t)

    o = lax.map(attend, (q_blocks, jnp.arange(nb)))
    return o.transpose(1, 0, 3, 2, 4).reshape(B, S, H, -1)


def _mla_mixer(h, cos, sin, w_in, q_a_norm, w_q_up, kv_a_norm, w_kv_up, q_norm, k_norm, w_out):
    B, S, _ = h.shape
    c_q, c_kv, k_rope = jnp.split(h @ w_in, [MLA_Q_LORA, MLA_Q_LORA + MLA_KV_LORA], axis=-1)
    q = (_rmsnorm(c_q, q_a_norm) @ w_q_up).reshape(B, S, MLA_HEADS, MLA_QK_HEAD)
    kv = (_rmsnorm(c_kv, kv_a_norm) @ w_kv_up).reshape(B, S, MLA_HEADS, MLA_NOPE + MLA_V_HEAD)
    k_nope, v = jnp.split(kv, [MLA_NOPE], axis=-1)
    k = jnp.concatenate(
        [k_nope, jnp.broadcast_to(k_rope[:, :, None, :], (B, S, MLA_HEADS, MLA_ROPE))], axis=-1)
    q = _rope_tail(_rmsnorm(q, q_norm), cos, sin)
    k = _rope_tail(_rmsnorm(k, k_norm), cos, sin)
    o = _causal_block_attention(q, k, v)
    return o.reshape(B, S, MLA_HEADS * MLA_V_HEAD) @ w_out


def _conv_ffn(h, w_up, conv_w, conv_b, w_down):
    S = h.shape[1]
    u = h @ w_up
    u_pad = jnp.pad(u, ((0, 0), (FFN_CONV - 1, 0), (0, 0)))
    y = conv_b.astype(u.dtype)
    for j in range(FFN_CONV):
        y = y + u_pad[:, j:j + S, :] * conv_w[j]
    gate, up = jnp.split(y, 2, axis=-1)
    return (jax.nn.silu(gate) * up) @ w_down


def _fwd_setup_inputs(seed: int = 0) -> dict:
    key = jax.random.key(seed)
    ks = iter(jax.random.split(key, 32))

    def nrm(shape, scale):
        return jax.random.normal(next(ks), shape, jnp.float32) * scale

    def gain(shape):
        return 1.0 + 0.02 * jax.random.normal(next(ks), shape, jnp.float32)

    D = D_MODEL
    x = jax.random.normal(next(ks), (BATCH, SEQ, D), jnp.float32)
    offsets = jax.random.randint(next(ks), (BATCH, 1), 0, 4096, dtype=jnp.int32)
    positions = offsets + jnp.arange(SEQ, dtype=jnp.int32)[None, :]
    return {
        "x": x,
        "positions": positions,
        "norm_mix": gain((DEPTH, D)),
        "norm_ffn": gain((DEPTH, D)),
        "hgrn_w_in": nrm((N_HGRN_LAYERS, D, 4 * D), D ** -0.5),
        "hgrn_lower_bounds": nrm((N_HGRN_LAYERS, HGRN_HEADS * HGRN_EXPAND), 0.1),
        "hgrn_out_norm": gain((N_HGRN_LAYERS, HGRN_VDIM)),
        "hgrn_w_out": nrm((N_HGRN_LAYERS, D, D), D ** -0.5),
        "mla_w_in": nrm((N_MLA_LAYERS, D, MLA_Q_LORA + MLA_KV_LORA + MLA_ROPE), D ** -0.5),
        "mla_q_a_norm": gain((N_MLA_LAYERS, MLA_Q_LORA)),
        "mla_w_q_up": nrm((N_MLA_LAYERS, MLA_Q_LORA, MLA_HEADS * MLA_QK_HEAD), MLA_Q_LORA ** -0.5),
        "mla_kv_a_norm": gain((N_MLA_LAYERS, MLA_KV_LORA)),
        "mla_w_kv_up": nrm((N_MLA_LAYERS, MLA_KV_LORA, MLA_HEADS * (MLA_NOPE + MLA_V_HEAD)), MLA_KV_LORA ** -0.5),
        "mla_q_norm": gain((N_MLA_LAYERS, MLA_QK_HEAD)),
        "mla_k_norm": gain((N_MLA_LAYERS, MLA_QK_HEAD)),
        "mla_w_out": nrm((N_MLA_LAYERS, MLA_HEADS * MLA_V_HEAD, D), (MLA_HEADS * MLA_V_HEAD) ** -0.5),
        "ffn_w_up": nrm((DEPTH, D, 2 * D_FF), D ** -0.5),
        "ffn_conv_w": nrm((DEPTH, FFN_CONV, 2 * D_FF), FFN_CONV ** -0.5),
        "ffn_conv_b": nrm((DEPTH, 2 * D_FF), 0.01),
        "ffn_w_down": nrm((DEPTH, D_FF, D), D_FF ** -0.5),
    }


def _fwd_reference(x, positions, norm_mix, norm_ffn, hgrn_w_in, hgrn_lower_bounds, hgrn_out_norm, hgrn_w_out,
              mla_w_in, mla_q_a_norm, mla_w_q_up, mla_kv_a_norm, mla_w_kv_up, mla_q_norm, mla_k_norm,
              mla_w_out, ffn_w_up, ffn_conv_w, ffn_conv_b, ffn_w_down):
    lb_soft = jax.nn.softmax(hgrn_lower_bounds.astype(jnp.float32), axis=0)
    lower_bounds = jnp.cumsum(lb_soft, axis=0) - lb_soft[0:1]
    inv_freq = ROPE_THETA ** (-jnp.arange(0, MLA_ROPE, 2, dtype=jnp.float32) / MLA_ROPE)
    ang = positions.astype(jnp.float32)[..., None] * inv_freq
    cos = jnp.cos(ang)[:, :, None, :]
    sin = jnp.sin(ang)[:, :, None, :]

    for layer in range(DEPTH):
        h = _rmsnorm(x, norm_mix[layer])
        j = layer // N_MIXERS
        if layer % N_MIXERS == 0:
            y = _hgrn2_mixer(h, hgrn_w_in[j], lower_bounds[j], hgrn_out_norm[j], hgrn_w_out[j])
        else:
            y = _mla_mixer(h, cos, sin, mla_w_in[j], mla_q_a_norm[j], mla_w_q_up[j], mla_kv_a_norm[j],
                           mla_w_kv_up[j], mla_q_norm[j], mla_k_norm[j], mla_w_out[j])
        x = x + y
        h = _rmsnorm(x, norm_ffn[layer])
        x = x + _conv_ffn(h, ffn_w_up[layer], ffn_conv_w[layer], ffn_conv_b[layer], ffn_w_down[layer])
    return x


import jax as _jax
import jax.numpy as _jnp

TWIN_FORMAT = 'train_step'
FWD_PARAMS = ['x', 'positions', 'norm_mix', 'norm_ffn', 'hgrn_w_in', 'hgrn_lower_bounds', 'hgrn_out_norm', 'hgrn_w_out', 'mla_w_in', 'mla_q_a_norm', 'mla_w_q_up', 'mla_kv_a_norm', 'mla_w_kv_up', 'mla_q_norm', 'mla_k_norm', 'mla_w_out', 'ffn_w_up', 'ffn_conv_w', 'ffn_conv_b', 'ffn_w_down']
TWIN_WEIGHTS = ['norm_mix', 'norm_ffn', 'hgrn_w_in', 'hgrn_lower_bounds', 'hgrn_out_norm', 'hgrn_w_out', 'mla_w_in', 'mla_q_a_norm', 'mla_w_q_up', 'mla_kv_a_norm', 'mla_w_kv_up', 'mla_q_norm', 'mla_k_norm', 'mla_w_out', 'ffn_w_up', 'ffn_conv_w', 'ffn_conv_b', 'ffn_w_down']
TWIN_DIFF_INPUT = 'x'
TWIN_INPUTS = ['x', 'positions', 'norm_mix', 'norm_ffn', 'hgrn_w_in', 'hgrn_lower_bounds', 'hgrn_out_norm', 'hgrn_w_out', 'mla_w_in', 'mla_q_a_norm', 'mla_w_q_up', 'mla_kv_a_norm', 'mla_w_kv_up', 'mla_q_norm', 'mla_k_norm', 'mla_w_out', 'ffn_w_up', 'ffn_conv_w', 'ffn_conv_b', 'ffn_w_down', 'loss_target', 'm_norm_mix', 'm_norm_ffn', 'm_hgrn_w_in', 'm_hgrn_lower_bounds', 'm_hgrn_out_norm', 'm_hgrn_w_out', 'm_mla_w_in', 'm_mla_q_a_norm', 'm_mla_w_q_up', 'm_mla_kv_a_norm', 'm_mla_w_kv_up', 'm_mla_q_norm', 'm_mla_k_norm', 'm_mla_w_out', 'm_ffn_w_up', 'm_ffn_conv_w', 'm_ffn_conv_b', 'm_ffn_w_down', 'v_norm_mix', 'v_norm_ffn', 'v_hgrn_w_in', 'v_hgrn_lower_bounds', 'v_hgrn_out_norm', 'v_hgrn_w_out', 'v_mla_w_in', 'v_mla_q_a_norm', 'v_mla_w_q_up', 'v_mla_kv_a_norm', 'v_mla_w_kv_up', 'v_mla_q_norm', 'v_mla_k_norm', 'v_mla_w_out', 'v_ffn_w_up', 'v_ffn_conv_w', 'v_ffn_conv_b', 'v_ffn_w_down']
TWIN_OUTPUTS = ['loss', 'grad_x', 'grad_norm_mix', 'grad_norm_ffn', 'grad_hgrn_w_in', 'grad_hgrn_lower_bounds', 'grad_hgrn_out_norm', 'grad_hgrn_w_out', 'grad_mla_w_in', 'grad_mla_q_a_norm', 'grad_mla_w_q_up', 'grad_mla_kv_a_norm', 'grad_mla_w_kv_up', 'grad_mla_q_norm', 'grad_mla_k_norm', 'grad_mla_w_out', 'grad_ffn_w_up', 'grad_ffn_conv_w', 'grad_ffn_conv_b', 'grad_ffn_w_down', 'delta_norm_mix', 'delta_norm_ffn', 'delta_hgrn_w_in', 'delta_hgrn_lower_bounds', 'delta_hgrn_out_norm', 'delta_hgrn_w_out', 'delta_mla_w_in', 'delta_mla_q_a_norm', 'delta_mla_w_q_up', 'delta_mla_kv_a_norm', 'delta_mla_w_kv_up', 'delta_mla_q_norm', 'delta_mla_k_norm', 'delta_mla_w_out', 'delta_ffn_w_up', 'delta_ffn_conv_w', 'delta_ffn_conv_b', 'delta_ffn_w_down', 'new_m_norm_mix', 'new_m_norm_ffn', 'new_m_hgrn_w_in', 'new_m_hgrn_lower_bounds', 'new_m_hgrn_out_norm', 'new_m_hgrn_w_out', 'new_m_mla_w_in', 'new_m_mla_q_a_norm', 'new_m_mla_w_q_up', 'new_m_mla_kv_a_norm', 'new_m_mla_w_kv_up', 'new_m_mla_q_norm', 'new_m_mla_k_norm', 'new_m_mla_w_out', 'new_m_ffn_w_up', 'new_m_ffn_conv_w', 'new_m_ffn_conv_b', 'new_m_ffn_w_down', 'new_v_norm_mix', 'new_v_norm_ffn', 'new_v_hgrn_w_in', 'new_v_hgrn_lower_bounds', 'new_v_hgrn_out_norm', 'new_v_hgrn_w_out', 'new_v_mla_w_in', 'new_v_mla_q_a_norm', 'new_v_mla_w_q_up', 'new_v_mla_kv_a_norm', 'new_v_mla_w_kv_up', 'new_v_mla_q_norm', 'new_v_mla_k_norm', 'new_v_mla_w_out', 'new_v_ffn_w_up', 'new_v_ffn_conv_w', 'new_v_ffn_conv_b', 'new_v_ffn_w_down']
TWIN_LEAF_KINDS = {'loss': 'loss', 'grad_x': 'grad_x', 'grad_norm_mix': 'grad_w', 'grad_norm_ffn': 'grad_w', 'grad_hgrn_w_in': 'grad_w', 'grad_hgrn_lower_bounds': 'grad_w', 'grad_hgrn_out_norm': 'grad_w', 'grad_hgrn_w_out': 'grad_w', 'grad_mla_w_in': 'grad_w', 'grad_mla_q_a_norm': 'grad_w', 'grad_mla_w_q_up': 'grad_w', 'grad_mla_kv_a_norm': 'grad_w', 'grad_mla_w_kv_up': 'grad_w', 'grad_mla_q_norm': 'grad_w', 'grad_mla_k_norm': 'grad_w', 'grad_mla_w_out': 'grad_w', 'grad_ffn_w_up': 'grad_w', 'grad_ffn_conv_w': 'grad_w', 'grad_ffn_conv_b': 'grad_w', 'grad_ffn_w_down': 'grad_w', 'delta_norm_mix': 'delta_w', 'delta_norm_ffn': 'delta_w', 'delta_hgrn_w_in': 'delta_w', 'delta_hgrn_lower_bounds': 'delta_w', 'delta_hgrn_out_norm': 'delta_w', 'delta_hgrn_w_out': 'delta_w', 'delta_mla_w_in': 'delta_w', 'delta_mla_q_a_norm': 'delta_w', 'delta_mla_w_q_up': 'delta_w', 'delta_mla_kv_a_norm': 'delta_w', 'delta_mla_w_kv_up': 'delta_w', 'delta_mla_q_norm': 'delta_w', 'delta_mla_k_norm': 'delta_w', 'delta_mla_w_out': 'delta_w', 'delta_ffn_w_up': 'delta_w', 'delta_ffn_conv_w': 'delta_w', 'delta_ffn_conv_b': 'delta_w', 'delta_ffn_w_down': 'delta_w', 'new_m_norm_mix': 'new_m', 'new_m_norm_ffn': 'new_m', 'new_m_hgrn_w_in': 'new_m', 'new_m_hgrn_lower_bounds': 'new_m', 'new_m_hgrn_out_norm': 'new_m', 'new_m_hgrn_w_out': 'new_m', 'new_m_mla_w_in': 'new_m', 'new_m_mla_q_a_norm': 'new_m', 'new_m_mla_w_q_up': 'new_m', 'new_m_mla_kv_a_norm': 'new_m', 'new_m_mla_w_kv_up': 'new_m', 'new_m_mla_q_norm': 'new_m', 'new_m_mla_k_norm': 'new_m', 'new_m_mla_w_out': 'new_m', 'new_m_ffn_w_up': 'new_m', 'new_m_ffn_conv_w': 'new_m', 'new_m_ffn_conv_b': 'new_m', 'new_m_ffn_w_down': 'new_m', 'new_v_norm_mix': 'new_v', 'new_v_norm_ffn': 'new_v', 'new_v_hgrn_w_in': 'new_v', 'new_v_hgrn_lower_bounds': 'new_v', 'new_v_hgrn_out_norm': 'new_v', 'new_v_hgrn_w_out': 'new_v', 'new_v_mla_w_in': 'new_v', 'new_v_mla_q_a_norm': 'new_v', 'new_v_mla_w_q_up': 'new_v', 'new_v_mla_kv_a_norm': 'new_v', 'new_v_mla_w_kv_up': 'new_v', 'new_v_mla_q_norm': 'new_v', 'new_v_mla_k_norm': 'new_v', 'new_v_mla_w_out': 'new_v', 'new_v_ffn_w_up': 'new_v', 'new_v_ffn_conv_w': 'new_v', 'new_v_ffn_conv_b': 'new_v', 'new_v_ffn_w_down': 'new_v'}


def _forward(args):
    return _fwd_reference(*[args[k] for k in FWD_PARAMS])


def _output_shape():
    def fwd():
        inp = _fwd_setup_inputs(0)
        return _fwd_reference(*[inp[k] for k in FWD_PARAMS])
    out = _jax.eval_shape(fwd)
    return out.shape, out.dtype

N_MICROBATCH = 1
ADAM_LR = 0.001
ADAM_B1 = 0.9
ADAM_B2 = 0.999
ADAM_EPS = 1e-08
ADAM_WD = 0.01
ADAM_STEP = 10
PER_EXAMPLE_BATCH_AXIS = {'x': 0, 'positions': 0, 'loss_target': 0}
SHARED_INPUTS = []
_WEIGHT_DTYPES = {'norm_mix': _jnp.float32, 'norm_ffn': _jnp.float32, 'hgrn_w_in': _jnp.float32, 'hgrn_lower_bounds': _jnp.float32, 'hgrn_out_norm': _jnp.float32, 'hgrn_w_out': _jnp.float32, 'mla_w_in': _jnp.float32, 'mla_q_a_norm': _jnp.float32, 'mla_w_q_up': _jnp.float32, 'mla_kv_a_norm': _jnp.float32, 'mla_w_kv_up': _jnp.float32, 'mla_q_norm': _jnp.float32, 'mla_k_norm': _jnp.float32, 'mla_w_out': _jnp.float32, 'ffn_w_up': _jnp.float32, 'ffn_conv_w': _jnp.float32, 'ffn_conv_b': _jnp.float32, 'ffn_w_down': _jnp.float32}
MOMENT_SCALE = {'norm_mix': 1.857465e+01, 'norm_ffn': 5.097605e+01, 'hgrn_w_in': 6.057896e-01, 'hgrn_lower_bounds': 5.028317e-02, 'hgrn_out_norm': 1.825783e+02, 'hgrn_w_out': 8.774967e-01, 'mla_w_in': 6.330767e-01, 'mla_q_a_norm': 3.866155e-01, 'mla_w_q_up': 1.525135e-01, 'mla_kv_a_norm': 3.022089e+00, 'mla_w_kv_up': 2.879369e-01, 'mla_q_norm': 1.719975e+00, 'mla_k_norm': 1.719827e+00, 'mla_w_out': 3.546351e-01, 'ffn_w_up': 4.836707e-01, 'ffn_conv_w': 6.968575e+00, 'ffn_conv_b': 6.449862e+00, 'ffn_w_down': 7.383392e-01}


def _to_microbatches(a, axis):
    t = _jnp.moveaxis(a, axis, 0)
    t = t.reshape((N_MICROBATCH, t.shape[0] // N_MICROBATCH) + t.shape[1:])
    return _jnp.moveaxis(t, 1, axis + 1)


def setup_inputs(seed: int = 0) -> dict:
    inp = _fwd_setup_inputs(seed)
    key = _jax.random.fold_in(_jax.random.key(seed), 7919)
    shape, _ = _output_shape()
    out = dict(inp)
    out["loss_target"] = _jax.random.normal(_jax.random.fold_in(key, 0), shape, _jnp.float32)
    for i, name in enumerate(TWIN_WEIGHTS):
        w = inp[name].astype(_jnp.float32)
        if MOMENT_SCALE is None:
            s = _jnp.sqrt(_jnp.mean(_jnp.square(w)) + 1e-30)
        else:
            s = MOMENT_SCALE[name]
        km, kv = _jax.random.split(_jax.random.fold_in(key, i + 1))
        out[name] = w
        out["m_" + name] = s * _jax.random.normal(km, w.shape, _jnp.float32)
        out["v_" + name] = (s * s) * _jax.random.uniform(kv, w.shape, _jnp.float32, 0.5, 1.5)
    if N_MICROBATCH > 1:
        for name, axis in PER_EXAMPLE_BATCH_AXIS.items():
            out[name] = _to_microbatches(out[name], axis)
    return {'x': out['x'], 'positions': out['positions'], 'norm_mix': out['norm_mix'], 'norm_ffn': out['norm_ffn'], 'hgrn_w_in': out['hgrn_w_in'], 'hgrn_lower_bounds': out['hgrn_lower_bounds'], 'hgrn_out_norm': out['hgrn_out_norm'], 'hgrn_w_out': out['hgrn_w_out'], 'mla_w_in': out['mla_w_in'], 'mla_q_a_norm': out['mla_q_a_norm'], 'mla_w_q_up': out['mla_w_q_up'], 'mla_kv_a_norm': out['mla_kv_a_norm'], 'mla_w_kv_up': out['mla_w_kv_up'], 'mla_q_norm': out['mla_q_norm'], 'mla_k_norm': out['mla_k_norm'], 'mla_w_out': out['mla_w_out'], 'ffn_w_up': out['ffn_w_up'], 'ffn_conv_w': out['ffn_conv_w'], 'ffn_conv_b': out['ffn_conv_b'], 'ffn_w_down': out['ffn_w_down'], 'loss_target': out['loss_target'], 'm_norm_mix': out['m_norm_mix'], 'm_norm_ffn': out['m_norm_ffn'], 'm_hgrn_w_in': out['m_hgrn_w_in'], 'm_hgrn_lower_bounds': out['m_hgrn_lower_bounds'], 'm_hgrn_out_norm': out['m_hgrn_out_norm'], 'm_hgrn_w_out': out['m_hgrn_w_out'], 'm_mla_w_in': out['m_mla_w_in'], 'm_mla_q_a_norm': out['m_mla_q_a_norm'], 'm_mla_w_q_up': out['m_mla_w_q_up'], 'm_mla_kv_a_norm': out['m_mla_kv_a_norm'], 'm_mla_w_kv_up': out['m_mla_w_kv_up'], 'm_mla_q_norm': out['m_mla_q_norm'], 'm_mla_k_norm': out['m_mla_k_norm'], 'm_mla_w_out': out['m_mla_w_out'], 'm_ffn_w_up': out['m_ffn_w_up'], 'm_ffn_conv_w': out['m_ffn_conv_w'], 'm_ffn_conv_b': out['m_ffn_conv_b'], 'm_ffn_w_down': out['m_ffn_w_down'], 'v_norm_mix': out['v_norm_mix'], 'v_norm_ffn': out['v_norm_ffn'], 'v_hgrn_w_in': out['v_hgrn_w_in'], 'v_hgrn_lower_bounds': out['v_hgrn_lower_bounds'], 'v_hgrn_out_norm': out['v_hgrn_out_norm'], 'v_hgrn_w_out': out['v_hgrn_w_out'], 'v_mla_w_in': out['v_mla_w_in'], 'v_mla_q_a_norm': out['v_mla_q_a_norm'], 'v_mla_w_q_up': out['v_mla_w_q_up'], 'v_mla_kv_a_norm': out['v_mla_kv_a_norm'], 'v_mla_w_kv_up': out['v_mla_w_kv_up'], 'v_mla_q_norm': out['v_mla_q_norm'], 'v_mla_k_norm': out['v_mla_k_norm'], 'v_mla_w_out': out['v_mla_w_out'], 'v_ffn_w_up': out['v_ffn_w_up'], 'v_ffn_conv_w': out['v_ffn_conv_w'], 'v_ffn_conv_b': out['v_ffn_conv_b'], 'v_ffn_w_down': out['v_ffn_w_down']}


def _loss(weights, diff, rest, loss_target):
    with _jax.named_scope("forward"):
        args = {**rest, TWIN_DIFF_INPUT: diff, **{k: w.astype(_WEIGHT_DTYPES[k]) for k, w in weights.items()}}
        y = _forward(args)
    with _jax.named_scope("loss_head"):
        err = _jnp.square(y.astype(_jnp.float32) - loss_target)
        return 0.5 * _jnp.sum(_jnp.mean(err, axis=-1)) if err.ndim else 0.5 * err


def _adamw(w, g, m, v):
    m = ADAM_B1 * m + (1.0 - ADAM_B1) * g
    v = ADAM_B2 * v + (1.0 - ADAM_B2) * _jnp.square(g)
    m_hat = m / (1.0 - ADAM_B1 ** ADAM_STEP)
    v_hat = v / (1.0 - ADAM_B2 ** ADAM_STEP)
    delta = -ADAM_LR * (m_hat / (_jnp.sqrt(v_hat) + ADAM_EPS) + ADAM_WD * w)
    return delta, m, v


def reference(x, positions, norm_mix, norm_ffn, hgrn_w_in, hgrn_lower_bounds, hgrn_out_norm, hgrn_w_out, mla_w_in, mla_q_a_norm, mla_w_q_up, mla_kv_a_norm, mla_w_kv_up, mla_q_norm, mla_k_norm, mla_w_out, ffn_w_up, ffn_conv_w, ffn_conv_b, ffn_w_down, loss_target, m_norm_mix, m_norm_ffn, m_hgrn_w_in, m_hgrn_lower_bounds, m_hgrn_out_norm, m_hgrn_w_out, m_mla_w_in, m_mla_q_a_norm, m_mla_w_q_up, m_mla_kv_a_norm, m_mla_w_kv_up, m_mla_q_norm, m_mla_k_norm, m_mla_w_out, m_ffn_w_up, m_ffn_conv_w, m_ffn_conv_b, m_ffn_w_down, v_norm_mix, v_norm_ffn, v_hgrn_w_in, v_hgrn_lower_bounds, v_hgrn_out_norm, v_hgrn_w_out, v_mla_w_in, v_mla_q_a_norm, v_mla_w_q_up, v_mla_kv_a_norm, v_mla_w_kv_up, v_mla_q_norm, v_mla_k_norm, v_mla_w_out, v_ffn_w_up, v_ffn_conv_w, v_ffn_conv_b, v_ffn_w_down):
    given = dict(x=x, positions=positions, norm_mix=norm_mix, norm_ffn=norm_ffn, hgrn_w_in=hgrn_w_in, hgrn_lower_bounds=hgrn_lower_bounds, hgrn_out_norm=hgrn_out_norm, hgrn_w_out=hgrn_w_out, mla_w_in=mla_w_in, mla_q_a_norm=mla_q_a_norm, mla_w_q_up=mla_w_q_up, mla_kv_a_norm=mla_kv_a_norm, mla_w_kv_up=mla_w_kv_up, mla_q_norm=mla_q_norm, mla_k_norm=mla_k_norm, mla_w_out=mla_w_out, ffn_w_up=ffn_w_up, ffn_conv_w=ffn_conv_w, ffn_conv_b=ffn_conv_b, ffn_w_down=ffn_w_down, loss_target=loss_target, m_norm_mix=m_norm_mix, m_norm_ffn=m_norm_ffn, m_hgrn_w_in=m_hgrn_w_in, m_hgrn_lower_bounds=m_hgrn_lower_bounds, m_hgrn_out_norm=m_hgrn_out_norm, m_hgrn_w_out=m_hgrn_w_out, m_mla_w_in=m_mla_w_in, m_mla_q_a_norm=m_mla_q_a_norm, m_mla_w_q_up=m_mla_w_q_up, m_mla_kv_a_norm=m_mla_kv_a_norm, m_mla_w_kv_up=m_mla_w_kv_up, m_mla_q_norm=m_mla_q_norm, m_mla_k_norm=m_mla_k_norm, m_mla_w_out=m_mla_w_out, m_ffn_w_up=m_ffn_w_up, m_ffn_conv_w=m_ffn_conv_w, m_ffn_conv_b=m_ffn_conv_b, m_ffn_w_down=m_ffn_w_down, v_norm_mix=v_norm_mix, v_norm_ffn=v_norm_ffn, v_hgrn_w_in=v_hgrn_w_in, v_hgrn_lower_bounds=v_hgrn_lower_bounds, v_hgrn_out_norm=v_hgrn_out_norm, v_hgrn_w_out=v_hgrn_w_out, v_mla_w_in=v_mla_w_in, v_mla_q_a_norm=v_mla_q_a_norm, v_mla_w_q_up=v_mla_w_q_up, v_mla_kv_a_norm=v_mla_kv_a_norm, v_mla_w_kv_up=v_mla_w_kv_up, v_mla_q_norm=v_mla_q_norm, v_mla_k_norm=v_mla_k_norm, v_mla_w_out=v_mla_w_out, v_ffn_w_up=v_ffn_w_up, v_ffn_conv_w=v_ffn_conv_w, v_ffn_conv_b=v_ffn_conv_b, v_ffn_w_down=v_ffn_w_down)
    weights = {n: given[n] for n in TWIN_WEIGHTS}
    shared = {n: given[n] for n in SHARED_INPUTS}
    per_example = {n: given[n] for n in ['x', 'positions']}
    grad_fn = _jax.value_and_grad(_loss, argnums=(0, 1))

    def one_microbatch(ex, loss_target):
        ex = dict(ex)
        diff = ex.pop(TWIN_DIFF_INPUT)
        return grad_fn(weights, diff, {**shared, **ex}, loss_target)

    if N_MICROBATCH == 1:
        loss, (grad_w, grad_x) = one_microbatch(per_example, given["loss_target"])
    else:
        def body(carry, xs):
            loss_sum, grad_sum = carry
            l_k, (gw_k, gx_k) = one_microbatch(xs[0], xs[1])
            with _jax.named_scope("update"):
                return (loss_sum + l_k, _jax.tree.map(_jnp.add, grad_sum, gw_k)), gx_k

        init = (_jnp.zeros((), _jnp.float32), _jax.tree.map(_jnp.zeros_like, weights))
        (loss, grad_w), grad_x = _jax.lax.scan(body, init, (per_example, given["loss_target"]))
    with _jax.named_scope("update"):
        delta_w, new_m, new_v = {}, {}, {}
        for n in TWIN_WEIGHTS:
            delta_w[n], new_m[n], new_v[n] = _adamw(weights[n], grad_w[n], given["m_" + n], given["v_" + n])
    return (loss, grad_x, *[grad_w[n] for n in TWIN_WEIGHTS], *[delta_w[n] for n in TWIN_WEIGHTS],
            *[new_m[n] for n in TWIN_WEIGHTS], *[new_v[n] for n in TWIN_WEIGHTS])
```

```python
import functools

import numpy as np
import jax
import jax.numpy as jnp
from jax import lax
from jax.experimental import pallas as pl
from jax.experimental.pallas import tpu as pltpu

F32 = jnp.float32
BF16 = jnp.bfloat16
_MXU = BF16

RMS_EPS = 1e-6
D_MODEL = 1024
HEADS = 8
HEAD_DIM = 128
HGRN_CHUNK = 64
MLA_NOPE = 128
MLA_ROPE = 64
MLA_QK = MLA_NOPE + MLA_ROPE
MLA_SLOT = 256
MLA_LORA = 256
ROPE_THETA = 10000.0
D_FF = 2816
FF_BLOCK = 1408
LANES = 128
SUBLANES = 8

ADAM_LR = 0.001
ADAM_B1 = 0.9
ADAM_B2 = 0.999
ADAM_EPS = 1e-08
ADAM_WD = 0.01
ADAM_STEP = 10

VMEM_LIMIT = 56 * 1024 * 1024
MESH_AXES = ("x", "y", "c")

_NN = ((1,), (0,))
_NT = ((1,), (1,))
_TN = ((0,), (0,))


def _dg(a, b, dims):
    return lax.dot_general(a.astype(_MXU), b.astype(_MXU), (dims, ((), ())), preferred_element_type=F32)


@jax.custom_vjp
def kdot(a, b):
    return _dg(a, b, _NN)


kdot.defvjp(lambda a, b: (_dg(a, b, _NN), (a, b)), lambda r, g: (_dg(g, r[1], _NT), _dg(r[0], g, _TN)))


@jax.custom_vjp
def kdot_nt(a, b):
    return _dg(a, b, _NT)


kdot_nt.defvjp(lambda a, b: (_dg(a, b, _NT), (a, b)), lambda r, g: (_dg(g, r[1], _NN), _dg(g, r[0], _TN)))


@jax.custom_vjp
def kdot_tn(a, b):
    return _dg(a, b, _TN)


kdot_tn.defvjp(lambda a, b: (_dg(a, b, _TN), (a, b)), lambda r, g: (_dg(r[1], g, _NT), _dg(r[0], g, _NN)))


def _pick(d, prefs):
    for p in prefs:
        if d >= p and d % p == 0:
            return p
    return d


def _params(sem):
    return pltpu.CompilerParams(dimension_semantics=sem, vmem_limit_bytes=VMEM_LIMIT)


def mm(a, b, mode, name, add=None, out_dtype=F32):
    if mode == "nn":
        (M, K), (K2, N) = a.shape, b.shape
    elif mode == "nt":
        (M, K), (N, K2) = a.shape, b.shape
    else:
        (K, M), (K2, N) = a.shape, b.shape
    assert K == K2, (name, a.shape, b.shape)
    tm = M if M <= 1024 else _pick(M, (1024, 1408, 512, 256, 128))
    tn = N if N <= 1024 else _pick(N, (1024, 1408, 512, 256, 128))
    tk = K if K <= 512 else _pick(K, (512, 256, 128))
    nk = K // tk
    dims = {"nn": _NN, "nt": _NT, "tn": _TN}[mode]
    a_spec = {"nn": pl.BlockSpec((tm, tk), lambda i, j, k: (i, k)),
              "nt": pl.BlockSpec((tm, tk), lambda i, j, k: (i, k)),
              "tn": pl.BlockSpec((tk, tm), lambda i, j, k: (k, i))}[mode]
    b_spec = {"nn": pl.BlockSpec((tk, tn), lambda i, j, k: (k, j)),
              "nt": pl.BlockSpec((tn, tk), lambda i, j, k: (j, k)),
              "tn": pl.BlockSpec((tk, tn), lambda i, j, k: (k, j))}[mode]
    o_spec = pl.BlockSpec((tm, tn), lambda i, j, k: (i, j))
    has_add = add is not None

    def kern(*refs):
        a_ref, b_ref = refs[0], refs[1]
        add_ref = refs[2] if has_add else None
        o_ref, acc = refs[-2], refs[-1]
        k = pl.program_id(2)

        @pl.when(k == 0)
        def _():
            acc[...] = jnp.zeros_like(acc)

        acc[...] += _dg(a_ref[...], b_ref[...], dims)

        @pl.when(k == nk - 1)
        def _():
            r = acc[...]
            if has_add:
                r = r + add_ref[...].astype(F32)
            o_ref[...] = r.astype(o_ref.dtype)

    ins = [a, b] + ([add] if has_add else [])
    specs = [a_spec, b_spec] + ([o_spec] if has_add else [])
    return pl.pallas_call(
        kern, name=name, grid=(M // tm, N // tn, nk), in_specs=specs, out_specs=o_spec,
        out_shape=jax.ShapeDtypeStruct((M, N), out_dtype), scratch_shapes=[pltpu.VMEM((tm, tn), F32)],
        compiler_params=_params(("parallel", "parallel", "arbitrary")))(*ins)


def _store(ref, val, first):
    if first is None:
        ref[...] = val.astype(ref.dtype)
        return

    @pl.when(first)
    def _():
        ref[...] = val.astype(ref.dtype)

    @pl.when(jnp.logical_not(first))
    def _():
        ref[...] += val.astype(ref.dtype)


def tilecall(body, name, grid, ins, outs, sem):
    n_in = len(ins)

    def kern(*refs):
        vals = body(*refs[:n_in])
        for ref, val, (_, _, first) in zip(refs[n_in:], vals, outs):
            _store(ref, val, None if first is None else first())

    return pl.pallas_call(
        kern, name=name, grid=grid, in_specs=[s for _, s in ins], out_specs=[s for _, s, _ in outs],
        out_shape=[sh for sh, _, _ in outs], compiler_params=_params(sem))(*[a for a, _ in ins])


def _rms(x, g, n):
    ms = jnp.sum(x * x, axis=-1, keepdims=True) / n
    return x * lax.rsqrt(ms + RMS_EPS) * g


def _sds(shape, dtype=F32):
    return jax.ShapeDtypeStruct(shape, dtype)


def _row_tile(S, w):
    return min(S, 512 if w <= 1024 else 256)


def rms_fwd(x, g, name, col=0, w=None):
    S = x.shape[0]
    w = w or x.shape[1]
    ts = _row_tile(S, w)
    return tilecall(
        lambda x_ref, g_ref: (_rms(x_ref[...], g_ref[...], w),), name, (S // ts,),
        [(x, pl.BlockSpec((ts, w), lambda i: (i, col))), (g, pl.BlockSpec((1, w), lambda i: (0, 0)))],
        [(_sds((S, w), _MXU), pl.BlockSpec((ts, w), lambda i: (i, 0)), None)], ("parallel",))[0]


def rms_bwd(x, g, dh, res, name, w=None):
    S = x.shape[0]
    w = w or x.shape[1]
    ts = _row_tile(S, w)

    def body(x_ref, g_ref, dh_ref, *rest):
        _, vjp = jax.vjp(lambda xv, gv: _rms(xv, gv, w), x_ref[...], g_ref[...])
        dx, dg = vjp(dh_ref[...].astype(F32))
        if rest:
            dx = dx + rest[0][...]
        return dx, dg

    row = pl.BlockSpec((ts, w), lambda i: (i, 0))
    vec = pl.BlockSpec((1, w), lambda i: (0, 0))
    ins = [(x, row), (g, vec), (dh, row)] + ([(res, row)] if res is not None else [])
    return tilecall(body, name, (S // ts,), ins,
                    [(_sds((S, w)), row, None), (_sds((1, w)), vec, lambda: pl.program_id(0) == 0)], ("arbitrary",))


def _shifted(u, halo_ref, is_first):
    rid = lax.broadcasted_iota(jnp.int32, (u.shape[0], 1), 0)
    h7 = jnp.where(is_first, 0.0, halo_ref[7:8, :])
    h6 = jnp.where(is_first, 0.0, halo_ref[6:7, :])
    u1 = jnp.where(rid == 0, h7, pltpu.roll(u, 1, 0))
    u2 = jnp.where(rid == 0, h6, jnp.where(rid == 1, h7, pltpu.roll(u, 2, 0)))
    return u1, u2


def _conv(u, u1, u2, cw_ref, cb_ref):
    return ((cb_ref[...] + u2 * cw_ref[0:1, :]) + u1 * cw_ref[1:2, :]) + u * cw_ref[2:3, :]


def _ffn_specs(S, ts, jmap):
    hb = ts // SUBLANES
    return (pl.BlockSpec((ts, FF_BLOCK), lambda j, i: (i, jmap(j))),
            pl.BlockSpec((SUBLANES, FF_BLOCK), lambda j, i: (jnp.maximum(i * hb - 1, 0), jmap(j))),
            pl.BlockSpec((SUBLANES, FF_BLOCK), lambda j, i: (0, jmap(j))),
            pl.BlockSpec((1, FF_BLOCK), lambda j, i: (0, jmap(j))))


def ffn_act_fwd(u, cw8, cb, name):
    S = u.shape[0]
    ts = _row_tile(S, 2 * D_FF)
    nb = D_FF // FF_BLOCK

    def body(ug, hg, cwg, cbg, uu, hu, cwu, cbu):
        first = pl.program_id(1) == 0
        g = ug[...]
        g1, g2 = _shifted(g, hg, first)
        yg = _conv(g, g1, g2, cwg, cbg)
        v = uu[...]
        v1, v2 = _shifted(v, hu, first)
        yu = _conv(v, v1, v2, cwu, cbu)
        return (yg * jax.nn.sigmoid(yg) * yu,)

    sg = _ffn_specs(S, ts, lambda j: j)
    su = _ffn_specs(S, ts, lambda j: j + nb)
    ins = [(u, sg[0]), (u, sg[1]), (cw8, sg[2]), (cb, sg[3]), (u, su[0]), (u, su[1]), (cw8, su[2]), (cb, su[3])]
    return tilecall(body, name, (nb, S // ts), ins,
                    [(_sds((S, D_FF), _MXU), pl.BlockSpec((ts, FF_BLOCK), lambda j, i: (i, j)), None)],
                    ("parallel", "parallel"))[0]


def ffn_act_bwd(u, cw8, cb, da, name):
    S = u.shape[0]
    ts = _row_tile(S, 2 * D_FF)
    nb = D_FF // FF_BLOCK

    def body(um, hm, cwm, cbm, up, hp, cwp, cbp, da_ref):
        j = pl.program_id(0)
        first = pl.program_id(1) == 0
        a = um[...]
        a1, a2 = _shifted(a, hm, first)
        ym = _conv(a, a1, a2, cwm, cbm)
        b = up[...]
        b1, b2 = _shifted(b, hp, first)
        yp = _conv(b, b1, b2, cwp, cbp)
        d = da_ref[...]
        sm = jax.nn.sigmoid(ym)
        d_gate = d * yp * (sm * (1.0 + ym * (1.0 - sm)))
        d_up = d * (yp * jax.nn.sigmoid(yp))
        dy = jnp.where(j < nb, d_gate, d_up)
        dcw = jnp.concatenate(
            [jnp.sum(dy * a2, axis=0, keepdims=True), jnp.sum(dy * a1, axis=0, keepdims=True),
             jnp.sum(dy * a, axis=0, keepdims=True), jnp.zeros((SUBLANES - 3, dy.shape[1]), F32)], axis=0)
        return dy, dcw, jnp.sum(dy, axis=0, keepdims=True)

    sm_ = _ffn_specs(S, ts, lambda j: j)
    sp_ = _ffn_specs(S, ts, lambda j: (j + nb) % (2 * nb))
    ins = [(u, sm_[0]), (u, sm_[1]), (cw8, sm_[2]), (cb, sm_[3]), (u, sp_[0]), (u, sp_[1]), (cw8, sp_[2]), (cb, sp_[3]),
           (da, pl.BlockSpec((ts, FF_BLOCK), lambda j, i: (i, j % nb)))]
    first_row = lambda: pl.program_id(1) == 0
    return tilecall(body, name, (2 * nb, S // ts), ins,
                    [(_sds((S, 2 * D_FF)), sm_[0], None), (_sds((SUBLANES, 2 * D_FF)), sm_[2], first_row),
                     (_sds((1, 2 * D_FF)), sm_[3], first_row)], ("parallel", "arbitrary"))


def ffn_conv_bwd(dy, cw8, name):
    S = dy.shape[0]
    ts = _row_tile(S, 2 * D_FF)
    hb = ts // SUBLANES
    nrow = S // ts
    ncol = 2 * D_FF // FF_BLOCK

    def body(dy_ref, halo_ref, cw_ref):
        last = pl.program_id(1) == nrow - 1
        d = dy_ref[...]
        rid = lax.broadcasted_iota(jnp.int32, (ts, 1), 0)
        n0 = jnp.where(last, 0.0, halo_ref[0:1, :])
        n1 = jnp.where(last, 0.0, halo_ref[1:2, :])
        d1 = jnp.where(rid == ts - 1, n0, pltpu.roll(d, ts - 1, 0))
        d2 = jnp.where(rid == ts - 1, n1, jnp.where(rid == ts - 2, n0, pltpu.roll(d, ts - 2, 0)))
        return (d * cw_ref[2:3, :] + d1 * cw_ref[1:2, :] + d2 * cw_ref[0:1, :],)

    row = pl.BlockSpec((ts, FF_BLOCK), lambda j, i: (i, j))
    ins = [(dy, row), (dy, pl.BlockSpec((SUBLANES, FF_BLOCK), lambda j, i: (jnp.minimum((i + 1) * hb, S // SUBLANES - 1), j))),
           (cw8, pl.BlockSpec((SUBLANES, FF_BLOCK), lambda j, i: (0, j)))]
    return tilecall(body, name, (ncol, nrow), ins, [(_sds((S, 2 * D_FF), _MXU), row, None)], ("parallel", "parallel"))[0]


def _hgrn_levels(C):
    out, m = [], C // 2
    while m >= 1:
        out.append(m)
        m //= 2
    return out


def _hgrn_sum_matrix(C):
    t = np.arange(C)[:, None]
    u = np.arange(C)[None, :]
    blocks = [u <= t, u > t]
    for m in _hgrn_levels(C):
        r = (t // (2 * m)) * (2 * m) + m
        right = (t % (2 * m)) >= m
        blocks.append(right & (u > r) & (u <= t))
        blocks.append((~right) & (u > t) & (u <= r))
    return np.concatenate(blocks, axis=0).astype(np.float32)


def _make_partial_sums(nb, C):
    @jax.custom_vjp
    def sums(mall, lf):
        hi = lf.astype(_MXU)
        r1 = lf - hi.astype(F32)
        mid = r1.astype(_MXU)
        lo = (r1 - mid.astype(F32)).astype(_MXU)
        e = _dg(mall, hi, _NN) + _dg(mall, mid, _NN) + _dg(mall, lo, _NN)
        return tuple(e[b * C:(b + 1) * C] for b in range(nb))

    def fwd(mall, lf):
        return sums(mall, lf), mall

    def bwd(mall, gs):
        return jnp.zeros_like(mall), _dg(mall, jnp.concatenate(gs, axis=0), _TN)

    sums.defvjp(fwd, bwd)
    return sums


def _hgrn_chunk(zq, zf, v, lb, st, mall, C):
    levels = _hgrn_levels(C)
    qs = zq * jax.nn.sigmoid(zq)
    fg = lb + (1.0 - lb) * jax.nn.sigmoid(zf)
    k = 1.0 - fg
    e = _make_partial_sums(2 + 2 * len(levels), C)(mall, jnp.log(fg))
    g_incl, g_after = e[0], e[1]
    rid = lax.broadcasted_iota(jnp.int32, (C, 1), 0)
    tt = lax.broadcasted_iota(jnp.int32, (C, C), 0)
    ss = lax.broadcasted_iota(jnp.int32, (C, C), 1)
    o = kdot_nt(qs * jnp.exp(g_incl), st)
    o = o + jnp.sum(qs * k, axis=-1, keepdims=True) * v
    scores = jnp.zeros((C, C), F32)
    for li, m in enumerate(levels):
        sh = int(np.log2(m))
        right = ((rid >> sh) & 1) == 1
        qt = jnp.where(right, qs * jnp.exp(e[2 + 2 * li]), 0.0)
        kt = jnp.where(right, 0.0, k * jnp.exp(e[3 + 2 * li]))
        pair = ((tt >> (sh + 1)) == (ss >> (sh + 1))) & (((tt >> sh) & 1) == 1) & (((ss >> sh) & 1) == 0)
        scores = scores + jnp.where(pair, kdot_nt(qt, kt), 0.0)
    o = o + kdot(scores, v)
    g_last = jnp.sum(jnp.where(rid == C - 1, g_incl, 0.0), axis=0, keepdims=True)
    st_new = st * jnp.exp(g_last) + kdot_tn(v, k * jnp.exp(g_after))
    return o, st_new


def _hgrn_in_specs(C, nc, rev):
    cm = (lambda c: nc - 1 - c) if rev else (lambda c: c)
    blk = lambda: pl.BlockSpec((C, HEAD_DIM), lambda h, c: (cm(c), h))
    return cm, [blk(), blk(), blk(), pl.BlockSpec((1, HEAD_DIM), lambda h, c: (0, h))]


def hgrn_fwd(zq, zf, zi, lb, mall, name):
    S = zq.shape[0]
    C = min(HGRN_CHUNK, S)
    nc = S // C

    def kern(zq_ref, zf_ref, zi_ref, lb_ref, mall_ref, o_ref, st_ref, st):
        @pl.when(pl.program_id(1) == 0)
        def _():
            st[...] = jnp.zeros_like(st)

        s_in = st[...]
        st_ref[...] = s_in
        o, s_new = _hgrn_chunk(zq_ref[...], zf_ref[...], zi_ref[...], lb_ref[...], s_in, mall_ref[...], C)
        o_ref[...] = o
        st[...] = s_new

    _, specs = _hgrn_in_specs(C, nc, False)
    return pl.pallas_call(
        kern, name=name, grid=(HEADS, nc),
        in_specs=specs + [pl.BlockSpec(mall.shape, lambda h, c: (0, 0))],
        out_specs=[pl.BlockSpec((C, HEAD_DIM), lambda h, c: (c, h)),
                   pl.BlockSpec((None, None, HEAD_DIM, HEAD_DIM), lambda h, c: (h, c, 0, 0))],
        out_shape=[_sds((S, D_MODEL)), _sds((HEADS, nc, HEAD_DIM, HEAD_DIM))],
        scratch_shapes=[pltpu.VMEM((HEAD_DIM, HEAD_DIM), F32)],
        compiler_params=_params(("parallel", "arbitrary")))(zq, zf, zi, lb, mall)


def hgrn_bwd(zq, zf, zi, lb, mall, states, do, name):
    S = zq.shape[0]
    C = min(HGRN_CHUNK, S)
    nc = S // C

    def kern(zq_ref, zf_ref, zi_ref, lb_ref, mall_ref, st_ref, do_ref, dq_ref, df_ref, di_ref, dlb_ref, dst):
        first = pl.program_id(1) == 0

        @pl.when(first)
        def _():
            dst[...] = jnp.zeros_like(dst)

        mall_v = mall_ref[...]
        _, vjp = jax.vjp(lambda a, b, c, d, e: _hgrn_chunk(a, b, c, d, e, mall_v, C),
                         zq_ref[...], zf_ref[...], zi_ref[...], lb_ref[...], st_ref[...])
        ga, gb, gv, gl, gs = vjp((do_ref[...], dst[...]))
        dq_ref[...] = ga.astype(dq_ref.dtype)
        df_ref[...] = gb.astype(df_ref.dtype)
        di_ref[...] = gv.astype(di_ref.dtype)
        _store(dlb_ref, gl, first)
        dst[...] = gs

    cm, specs = _hgrn_in_specs(C, nc, True)
    row = lambda: pl.BlockSpec((C, HEAD_DIM), lambda h, c: (cm(c), h))
    return pl.pallas_call(
        kern, name=name, grid=(HEADS, nc),
        in_specs=specs + [pl.BlockSpec(mall.shape, lambda h, c: (0, 0)),
                          pl.BlockSpec((None, None, HEAD_DIM, HEAD_DIM), lambda h, c: (h, cm(c), 0, 0)), row()],
        out_specs=[row(), row(), row(), pl.BlockSpec((1, HEAD_DIM), lambda h, c: (0, h))],
        out_shape=[_sds((S, D_MODEL), _MXU), _sds((S, D_MODEL), _MXU), _sds((S, D_MODEL), _MXU), _sds((1, D_MODEL))],
        scratch_shapes=[pltpu.VMEM((HEAD_DIM, HEAD_DIM), F32)],
        compiler_params=_params(("parallel", "arbitrary")))(zq, zf, zi, lb, mall, states, do)


def _hgrn_out(o, g, gain):
    return _rms(o, gain, HEAD_DIM) * (g * jax.nn.sigmoid(g))


def hgrn_out_fwd(o, zg, gain, name):
    S = o.shape[0]
    ts = _row_tile(S, D_MODEL)
    blk = pl.BlockSpec((ts, HEAD_DIM), lambda i, h: (i, h))
    return tilecall(lambda o_ref, g_ref, w_ref: (_hgrn_out(o_ref[...], g_ref[...], w_ref[...]),), name, (S // ts, HEADS),
                    [(o, blk), (zg, blk), (gain, pl.BlockSpec((1, HEAD_DIM), lambda i, h: (0, 0)))],
                    [(_sds((S, D_MODEL), _MXU), blk, None)], ("parallel", "parallel"))[0]


def hgrn_out_bwd(o, zg, gain, don, name):
    S = o.shape[0]
    ts = _row_tile(S, D_MODEL)
    blk = pl.BlockSpec((ts, HEAD_DIM), lambda i, h: (i, h))
    vec = pl.BlockSpec((1, HEAD_DIM), lambda i, h: (0, 0))

    def body(o_ref, g_ref, w_ref, d_ref):
        _, vjp = jax.vjp(_hgrn_out, o_ref[...], g_ref[...], w_ref[...])
        return vjp(d_ref[...])

    return tilecall(body, name, (S // ts, HEADS), [(o, blk), (zg, blk), (gain, vec), (don, blk)],
                    [(_sds((S, D_MODEL)), blk, None), (_sds((S, D_MODEL), _MXU), blk, None),
                     (_sds((1, HEAD_DIM)), vec, lambda: (pl.program_id(0) == 0) & (pl.program_id(1) == 0))],
                    ("arbitrary", "arbitrary"))


def lower_bound_fwd(p, name):
    assert p.shape[0] == 2

    def body(p_ref):
        s = _lb_soft(p_ref[0:1, :], p_ref[1:2, :])
        return (jnp.concatenate([jnp.zeros_like(s), s] + [jnp.zeros_like(s)] * (SUBLANES - 2), axis=0),)

    spec8 = pl.BlockSpec((SUBLANES, p.shape[1]), lambda: (0, 0))
    return tilecall(body, name, (), [(p, pl.BlockSpec(p.shape, lambda: (0, 0)))], [(_sds((SUBLANES, p.shape[1])), spec8, None)], ())[0]


def _lb_soft(p0, p1):
    mx = jnp.maximum(p0, p1)
    e0, e1 = jnp.exp(p0 - mx), jnp.exp(p1 - mx)
    s0, s1 = e0 / (e0 + e1), e1 / (e0 + e1)
    return (s0 + s1) - s0


def lower_bound_bwd(p, dlb1, name):
    def body(p_ref, d_ref):
        _, vjp = jax.vjp(_lb_soft, p_ref[0:1, :], p_ref[1:2, :])
        g0, g1 = vjp(d_ref[...])
        return (jnp.concatenate([g0, g1] + [jnp.zeros_like(g0)] * (SUBLANES - 2), axis=0),)

    spec8 = pl.BlockSpec((SUBLANES, p.shape[1]), lambda: (0, 0))
    return tilecall(body, name, (), [(p, pl.BlockSpec(p.shape, lambda: (0, 0))), (dlb1, pl.BlockSpec(dlb1.shape, lambda: (0, 0)))],
                    [(_sds((SUBLANES, p.shape[1])), spec8, None)], ())[0]


@jax.custom_vjp
def _swap_rope_halves(x):
    lane = lax.broadcasted_iota(jnp.int32, x.shape, 1)
    lo = (lane >= MLA_NOPE) & (lane < MLA_NOPE + MLA_ROPE // 2)
    hi = (lane >= MLA_NOPE + MLA_ROPE // 2) & (lane < MLA_QK)
    return jnp.where(lo, pltpu.roll(x, MLA_SLOT - MLA_ROPE // 2, 1), jnp.where(hi, pltpu.roll(x, MLA_ROPE // 2, 1), 0.0))


_swap_rope_halves.defvjp(lambda x: (_swap_rope_halves(x), None), lambda _, g: (_swap_rope_halves(g),))


def _norm_rope(x, gain, cos_t, sin_t):
    y = _rms(x, gain, MLA_QK)
    return y * cos_t + _swap_rope_halves(y) * sin_t


def _qk_heads(qs, kn, kr, qn, kn_gain, cos_t, sin_t):
    q = _norm_rope(qs, qn, cos_t, sin_t)
    k = _norm_rope(jnp.concatenate([kn, kr], axis=1), kn_gain, cos_t, sin_t)
    return q, k


def _qk_specs(ts):
    slot = pl.BlockSpec((ts, MLA_SLOT), lambda i, h: (i, h))
    nope = pl.BlockSpec((ts, HEAD_DIM), lambda i, h: (i, 2 * h))
    shared = pl.BlockSpec((ts, HEAD_DIM), lambda i, h: (i, 0))
    gain = pl.BlockSpec((1, MLA_SLOT), lambda i, h: (0, 0))
    table = pl.BlockSpec((ts, MLA_SLOT), lambda i, h: (i, 0))
    return slot, nope, shared, gain, table


def qk_fwd(qslots, kv, krope, qn, kn, cos_t, sin_t, name):
    S = qslots.shape[0]
    ts = _row_tile(S, D_MODEL)
    slot, nope, shared, gain, table = _qk_specs(ts)

    def body(q_ref, kn_ref, kr_ref, qn_ref, kg_ref, c_ref, s_ref):
        return _qk_heads(q_ref[...], kn_ref[...], kr_ref[...], qn_ref[...], kg_ref[...], c_ref[...], s_ref[...])

    out = _sds((S, HEADS * MLA_SLOT), _MXU)
    return tilecall(body, name, (S // ts, HEADS),
                    [(qslots, slot), (kv, nope), (krope, shared), (qn, gain), (kn, gain), (cos_t, table), (sin_t, table)],
                    [(out, slot, None), (out, slot, None)], ("parallel", "parallel"))


def qk_bwd(qslots, kv, krope, qn, kn, cos_t, sin_t, dq, dk, dv, name):
    S = qslots.shape[0]
    ts = _row_tile(S, D_MODEL)
    slot, nope, shared, gain, table = _qk_specs(ts)
    vblk = pl.BlockSpec((ts, HEAD_DIM), lambda i, h: (i, h))

    def body(q_ref, kn_ref, kr_ref, qn_ref, kg_ref, c_ref, s_ref, dq_ref, dk_ref, dv_ref):
        c, s = c_ref[...], s_ref[...]
        _, vjp = jax.vjp(lambda a, b, r, g1, g2: _qk_heads(a, b, r, g1, g2, c, s),
                         q_ref[...], kn_ref[...], kr_ref[...], qn_ref[...], kg_ref[...])
        ga, gb, gr, g1, g2 = vjp((dq_ref[...], dk_ref[...]))
        return ga, jnp.concatenate([gb, dv_ref[...]], axis=1), gr, g1, g2

    first_head = lambda: pl.program_id(1) == 0
    first = lambda: (pl.program_id(0) == 0) & (pl.program_id(1) == 0)
    wide = _sds((S, HEADS * MLA_SLOT), _MXU)
    return tilecall(body, name, (S // ts, HEADS),
                    [(qslots, slot), (kv, nope), (krope, shared), (qn, gain), (kn, gain), (cos_t, table), (sin_t, table),
                     (dq, slot), (dk, slot), (dv, vblk)],
                    [(wide, slot, None), (wide, slot, None), (_sds((S, HEAD_DIM)), shared, first_head),
                     (_sds((1, MLA_SLOT)), gain, first), (_sds((1, MLA_SLOT)), gain, first)], ("arbitrary", "arbitrary"))


_ATTN_SCALE = MLA_QK ** -0.5


def _attn_probs(q_ref, k_ref, lse, i, j, tq):
    s = _dg(q_ref[...], k_ref[...], _NT) * _ATTN_SCALE
    rows = i * tq + lax.broadcasted_iota(jnp.int32, (tq, tq), 0)
    cols = j * tq + lax.broadcasted_iota(jnp.int32, (tq, tq), 1)
    return jnp.exp(jnp.where(rows >= cols, s - lse, -jnp.inf))


def attn_fwd(qr, kr, kv, name):
    S = qr.shape[0]
    tq = min(S, 512)
    nq = S // tq

    def kern(q_ref, k_ref, v_ref, o_ref, lse_ref, m_s, l_s, acc):
        i, j = pl.program_id(1), pl.program_id(2)

        @pl.when(j == 0)
        def _():
            m_s[...] = jnp.full_like(m_s, -jnp.inf)
            l_s[...] = jnp.zeros_like(l_s)
            acc[...] = jnp.zeros_like(acc)

        @pl.when(j <= i)
        def _():
            s = _dg(q_ref[...], k_ref[...], _NT) * _ATTN_SCALE
            rows = i * tq + lax.broadcasted_iota(jnp.int32, (tq, tq), 0)
            cols = j * tq + lax.broadcasted_iota(jnp.int32, (tq, tq), 1)
            s = jnp.where(rows >= cols, s, -jnp.inf)
            m_prev = m_s[...]
            m_new = jnp.maximum(m_prev, jnp.max(s, axis=-1, keepdims=True))
            alpha = jnp.exp(m_prev - m_new)
            p = jnp.exp(s - m_new)
            l_s[...] = alpha * l_s[...] + jnp.sum(p, axis=-1, keepdims=True)
            acc[...] = alpha * acc[...] + _dg(p, v_ref[...], _NN)
            m_s[...] = m_new

        @pl.when(j == i)
        def _():
            l = l_s[...]
            o_ref[...] = acc[...] / l
            lse_ref[...] = jnp.broadcast_to(m_s[...] + jnp.log(l), lse_ref.shape)

    kclamp = lambda i, j: jnp.minimum(j, i)
    out = _sds((S, HEADS * HEAD_DIM))
    oblk = pl.BlockSpec((tq, HEAD_DIM), lambda h, i, j: (i, h))
    return pl.pallas_call(
        kern, name=name, grid=(HEADS, nq, nq),
        in_specs=[pl.BlockSpec((tq, MLA_SLOT), lambda h, i, j: (i, h)),
                  pl.BlockSpec((tq, MLA_SLOT), lambda h, i, j: (kclamp(i, j), h)),
                  pl.BlockSpec((tq, HEAD_DIM), lambda h, i, j: (kclamp(i, j), 2 * h + 1))],
        out_specs=[oblk, oblk], out_shape=[out, out],
        scratch_shapes=[pltpu.VMEM((tq, 1), F32), pltpu.VMEM((tq, 1), F32), pltpu.VMEM((tq, HEAD_DIM), F32)],
        compiler_params=_params(("parallel", "parallel", "arbitrary")))(qr, kr, kv)


def attn_bwd_q(qr, kr, kv, o, lse, do, name):
    S = qr.shape[0]
    tq = min(S, 512)
    nq = S // tq

    def kern(q_ref, k_ref, v_ref, o_ref, lse_ref, do_ref, dq_ref, acc, delta):
        i, j = pl.program_id(1), pl.program_id(2)

        @pl.when(j == 0)
        def _():
            acc[...] = jnp.zeros_like(acc)
            delta[...] = jnp.sum(do_ref[...] * o_ref[...], axis=-1, keepdims=True)

        @pl.when(j <= i)
        def _():
            p = _attn_probs(q_ref, k_ref, lse_ref[:, 0:1], i, j, tq)
            dp = _dg(do_ref[...], v_ref[...], _NT)
            ds = p * (dp - delta[...]) * _ATTN_SCALE
            acc[...] += _dg(ds, k_ref[...], _NN)

        @pl.when(j == i)
        def _():
            dq_ref[...] = acc[...]

    kclamp = lambda i, j: jnp.minimum(j, i)
    qblk = pl.BlockSpec((tq, MLA_SLOT), lambda h, i, j: (i, h))
    oblk = pl.BlockSpec((tq, HEAD_DIM), lambda h, i, j: (i, h))
    return pl.pallas_call(
        kern, name=name, grid=(HEADS, nq, nq),
        in_specs=[qblk, pl.BlockSpec((tq, MLA_SLOT), lambda h, i, j: (kclamp(i, j), h)),
                  pl.BlockSpec((tq, HEAD_DIM), lambda h, i, j: (kclamp(i, j), 2 * h + 1)), oblk, oblk, oblk],
        out_specs=qblk, out_shape=_sds((S, HEADS * MLA_SLOT)),
        scratch_shapes=[pltpu.VMEM((tq, MLA_SLOT), F32), pltpu.VMEM((tq, 1), F32)],
        compiler_params=_params(("parallel", "parallel", "arbitrary")))(qr, kr, kv, o, lse, do)


def attn_bwd_kv(qr, kr, kv, o, lse, do, name):
    S = qr.shape[0]
    tq = min(S, 512)
    nq = S // tq

    def kern(q_ref, k_ref, v_ref, o_ref, lse_ref, do_ref, dk_ref, dv_ref, dk_acc, dv_acc):
        j, i = pl.program_id(1), pl.program_id(2)

        @pl.when(i == 0)
        def _():
            dk_acc[...] = jnp.zeros_like(dk_acc)
            dv_acc[...] = jnp.zeros_like(dv_acc)

        @pl.when(i >= j)
        def _():
            p = _attn_probs(q_ref, k_ref, lse_ref[:, 0:1], i, j, tq)
            g = do_ref[...]
            delta = jnp.sum(g * o_ref[...], axis=-1, keepdims=True)
            dv_acc[...] += _dg(p, g, _TN)
            ds = p * (_dg(g, v_ref[...], _NT) - delta) * _ATTN_SCALE
            dk_acc[...] += _dg(ds, q_ref[...], _TN)

        @pl.when(i == nq - 1)
        def _():
            dk_ref[...] = dk_acc[...]
            dv_ref[...] = dv_acc[...]

    qclamp = lambda j, i: jnp.maximum(i, j)
    qblk = pl.BlockSpec((tq, MLA_SLOT), lambda h, j, i: (qclamp(j, i), h))
    oblk = pl.BlockSpec((tq, HEAD_DIM), lambda h, j, i: (qclamp(j, i), h))
    kblk = pl.BlockSpec((tq, MLA_SLOT), lambda h, j, i: (j, h))
    return pl.pallas_call(
        kern, name=name, grid=(HEADS, nq, nq),
        in_specs=[qblk, kblk, pl.BlockSpec((tq, HEAD_DIM), lambda h, j, i: (j, 2 * h + 1)), oblk, oblk, oblk],
        out_specs=[kblk, pl.BlockSpec((tq, HEAD_DIM), lambda h, j, i: (j, h))],
        out_shape=[_sds((S, HEADS * MLA_SLOT)), _sds((S, HEADS * HEAD_DIM))],
        scratch_shapes=[pltpu.VMEM((tq, MLA_SLOT), F32), pltpu.VMEM((tq, HEAD_DIM), F32)],
        compiler_params=_params(("parallel", "parallel", "arbitrary")))(qr, kr, kv, o, lse, do)


def loss_head(y, target, name):
    S, Dm = y.shape
    ts = _row_tile(S, Dm)

    def body(y_ref, t_ref):
        e = y_ref[...] - t_ref[...]
        tot = jnp.sum(jnp.sum(e * e, axis=-1, keepdims=True) / Dm, axis=0, keepdims=True)
        return e / Dm, jnp.broadcast_to(0.5 * tot, (SUBLANES, LANES))

    row = pl.BlockSpec((ts, Dm), lambda i: (i, 0))
    return tilecall(body, name, (S // ts,), [(y, row), (target, row)],
                    [(_sds((S, Dm)), row, None),
                     (_sds((SUBLANES, LANES)), pl.BlockSpec((SUBLANES, LANES), lambda i: (0, 0)), lambda: pl.program_id(0) == 0)],
                    ("arbitrary",))


_PEER_FLIPS = {
    "chips": ((1, 0, 0), (0, 1, 0), (1, 1, 0)),
    "sibling": ((0, 0, 1),),
    "all": tuple((a, b, c) for a in (0, 1) for b in (0, 1) for c in (0, 1))[1:],
}
_SLOT_WEIGHTS = {"chips": (2, 1, 0), "sibling": (0, 0, 1), "all": (4, 2, 1)}


def exchange(arrs, group, scatter, name):
    flips = _PEER_FLIPS[group]
    wx, wy, wc = _SLOT_WEIGHTS[group]
    n_slots = len(flips) + 1
    n = len(arrs)

    def kern(*refs):
        srcs, outs = refs[:n], refs[n:2 * n]
        send_sems, recv_sems, local_sems = refs[2 * n:]
        me = (lax.axis_index("x"), lax.axis_index("y"), lax.axis_index("c"))
        my_slot = wx * me[0] + wy * me[1] + wc * me[2]
        copies = []
        for a in range(n):
            own = srcs[a].at[my_slot] if scatter else srcs[a]
            cp = pltpu.make_async_copy(own, outs[a].at[my_slot], local_sems.at[a])
            cp.start()
            copies.append(cp)
        for f, flip in enumerate(flips):
            peer = tuple(m ^ b if b else m for m, b in zip(me, flip))
            peer_slot = wx * peer[0] + wy * peer[1] + wc * peer[2]
            for a in range(n):
                cp = pltpu.make_async_remote_copy(
                    src_ref=srcs[a].at[peer_slot] if scatter else srcs[a], dst_ref=outs[a].at[my_slot],
                    send_sem=send_sems.at[f, a], recv_sem=recv_sems.at[f, a],
                    device_id=peer, device_id_type=pl.DeviceIdType.MESH)
                cp.start()
                copies.append(cp)
        for cp in copies:
            cp.wait()

    hbm = pl.BlockSpec(memory_space=pltpu.HBM)
    out_shape = [_sds((n_slots,) + (a.shape[1:] if scatter else a.shape), a.dtype) for a in arrs]
    return pl.pallas_call(
        kern, name=name, in_specs=[hbm] * n, out_specs=[hbm] * n, out_shape=out_shape,
        scratch_shapes=[pltpu.SemaphoreType.DMA((len(flips), n)), pltpu.SemaphoreType.DMA((len(flips), n)),
                        pltpu.SemaphoreType.DMA((n,))])(*arrs)


PACK_ROWS = 2048


def _pack(arrs, dtype, row_multiple):
    flat = jnp.concatenate([a.reshape(-1).astype(dtype) for a in arrs])
    rows = -(-flat.shape[0] // LANES)
    rows = -(-rows // row_multiple) * row_multiple
    return jnp.pad(flat, (0, rows * LANES - flat.shape[0])).reshape(rows, LANES)


def _unpack(buf, shapes):
    flat = buf.reshape(-1)
    out, off = [], 0
    for s in shapes:
        n = int(np.prod(s))
        out.append(flat[off:off + n].reshape(s))
        off += n
    return out


def _pack_rows(rows):
    return min(rows, PACK_ROWS)


def sum_slots(buf, name):
    P, R, _ = buf.shape
    tr = _pack_rows(R)

    def body(b_ref):
        acc = b_ref[0]
        for p in range(1, P):
            acc = acc + b_ref[p]
        return (acc,)

    return tilecall(body, name, (R // tr,), [(buf, pl.BlockSpec((P, tr, LANES), lambda i: (0, i, 0)))],
                    [(_sds((R, LANES)), pl.BlockSpec((tr, LANES), lambda i: (i, 0)), None)], ("parallel",))[0]


def adamw(gparts, w, m, v, name):
    P, R, _ = gparts.shape
    tr = _pack_rows(R)

    def body(g_ref, w_ref, m_ref, v_ref):
        g = g_ref[0]
        for p in range(1, P):
            g = g + g_ref[p]
        m_new = ADAM_B1 * m_ref[...] + (1.0 - ADAM_B1) * g
        v_new = ADAM_B2 * v_ref[...] + (1.0 - ADAM_B2) * jnp.square(g)
        m_hat = m_new / (1.0 - ADAM_B1 ** ADAM_STEP)
        v_hat = v_new / (1.0 - ADAM_B2 ** ADAM_STEP)
        delta = -ADAM_LR * (m_hat / (jnp.sqrt(v_hat) + ADAM_EPS) + ADAM_WD * w_ref[...])
        return g, delta, m_new, v_new

    row = pl.BlockSpec((tr, LANES), lambda i: (i, 0))
    out = (_sds((R, LANES)), row, None)
    return tilecall(body, name, (R // tr,),
                    [(gparts, pl.BlockSpec((P, tr, LANES), lambda i: (0, i, 0))), (w, row), (m, row), (v, row)],
                    [out, out, out, out], ("parallel",))


def _ffn_fwd(x, gain, w_up, cw8, cb, w_down):
    h = rms_fwd(x, gain, "rms_fwd")
    u = mm(h, w_up, "nn", "mm_nn")
    a = ffn_act_fwd(u, cw8, cb, "ffn_act_fwd")
    y = mm(a, w_down, "nn", "mm_nn_add", add=x)
    return y, (x, h, u, a)


def _ffn_bwd(dy, saved, gain, w_up, cw8, cb, w_down):
    x, h, u, a = saved
    d_w_down = mm(a, dy, "tn", "mm_tn")
    da = mm(dy, w_down, "nt", "mm_nt")
    dpre, d_cw8, d_cb = ffn_act_bwd(u, cw8, cb, da, "ffn_act_bwd")
    du = ffn_conv_bwd(dpre, cw8, "ffn_conv_bwd")
    d_w_up = mm(h, du, "tn", "mm_tn")
    dh = mm(du, w_up, "nt", "mm_nt")
    dx, d_gain = rms_bwd(x, gain, dh, dy, "rms_bwd")
    return dx, dict(gain=d_gain, w_up=d_w_up, conv_w=d_cw8[0:3], conv_b=d_cb, w_down=d_w_down)


def _hgrn_layer_fwd(x, gain, w_in4, lb, out_gain, w_out, mall):
    h = rms_fwd(x, gain, "rms_fwd")
    z = [mm(h, w, "nn", "mm_nn") for w in w_in4]
    o, states = hgrn_fwd(z[0], z[1], z[2], lb, mall, "hgrn_fwd")
    on = hgrn_out_fwd(o, z[3], out_gain, "hgrn_out_fwd")
    y = mm(on, w_out, "nn", "mm_nn_add", add=x)
    return y, (x, h, z, o, states, on)


def _hgrn_layer_bwd(dy, saved, gain, w_in4, lb, out_gain, w_out, mall):
    x, h, z, o, states, on = saved
    d_w_out = mm(on, dy, "tn", "mm_tn")
    don = mm(dy, w_out, "nt", "mm_nt")
    do, dzg, d_out_gain = hgrn_out_bwd(o, z[3], out_gain, don, "hgrn_out_bwd")
    dzq, dzf, dzi, dlb = hgrn_bwd(z[0], z[1], z[2], lb, mall, states, do, "hgrn_bwd")
    dz = [dzq, dzf, dzi, dzg]
    d_w_in4 = [mm(h, d, "tn", "mm_tn") for d in dz]
    dh = mm(dz[0], w_in4[0], "nt", "mm_nt")
    for d, w in zip(dz[1:], w_in4[1:]):
        dh = mm(d, w, "nt", "mm_nt_add", add=dh)
    dx, d_gain = rms_bwd(x, gain, dh, dy, "rms_bwd")
    return dx, dict(gain=d_gain, w_in4=d_w_in4, lb=dlb, out_gain=d_out_gain, w_out=d_w_out)


def _mla_layer_fwd(x, gain, w_cq, w_ckv, w_kr, qa_gain, kva_gain, w_q_up, w_kv_up, qn, kn, w_out, cos_t, sin_t):
    h = rms_fwd(x, gain, "rms_fwd")
    cq = mm(h, w_cq, "nn", "mm_nn")
    ckv = mm(h, w_ckv, "nn", "mm_nn")
    kr = mm(h, w_kr, "nn", "mm_nn")
    cqn = rms_fwd(cq, qa_gain, "rms_fwd")
    ckvn = rms_fwd(ckv, kva_gain, "rms_fwd")
    qslots = mm(cqn, w_q_up, "nn", "mm_nn")
    kv = mm(ckvn, w_kv_up, "nn", "mm_nn")
    qr, krot = qk_fwd(qslots, kv, kr, qn, kn, cos_t, sin_t, "qk_fwd")
    o, lse = attn_fwd(qr, krot, kv, "attn_fwd")
    y = mm(o, w_out, "nn", "mm_nn_add", add=x)
    return y, (x, h, cq, ckv, kr, cqn, ckvn, qslots, kv, qr, krot, o, lse)


def _mla_layer_bwd(dy, saved, gain, w_cq, w_ckv, w_kr, qa_gain, kva_gain, w_q_up, w_kv_up, qn, kn, w_out, cos_t, sin_t):
    x, h, cq, ckv, kr, cqn, ckvn, qslots, kv, qr, krot, o, lse = saved
    d_w_out = mm(o, dy, "tn", "mm_tn")
    do = mm(dy, w_out, "nt", "mm_nt")
    dq = attn_bwd_q(qr, krot, kv, o, lse, do, "attn_bwd_q")
    dk, dv = attn_bwd_kv(qr, krot, kv, o, lse, do, "attn_bwd_kv")
    dqslots, dkv, dkr, d_qn, d_kn = qk_bwd(qslots, kv, kr, qn, kn, cos_t, sin_t, dq, dk, dv, "qk_bwd")
    d_w_q_up = mm(cqn, dqslots, "tn", "mm_tn")
    dcqn = mm(dqslots, w_q_up, "nt", "mm_nt")
    d_w_kv_up = mm(ckvn, dkv, "tn", "mm_tn")
    dckvn = mm(dkv, w_kv_up, "nt", "mm_nt")
    dcq, d_qa = rms_bwd(cq, qa_gain, dcqn, None, "rms_bwd")
    dckv, d_kva = rms_bwd(ckv, kva_gain, dckvn, None, "rms_bwd")
    d_w_cq = mm(h, dcq, "tn", "mm_tn")
    d_w_ckv = mm(h, dckv, "tn", "mm_tn")
    d_w_kr = mm(h, dkr, "tn", "mm_tn")
    dh = mm(dcq, w_cq, "nt", "mm_nt")
    dh = mm(dckv, w_ckv, "nt", "mm_nt_add", add=dh)
    dh = mm(dkr, w_kr, "nt", "mm_nt_add", add=dh)
    dx, d_gain = rms_bwd(x, gain, dh, dy, "rms_bwd")
    return dx, dict(gain=d_gain, w_cq=d_w_cq, w_ckv=d_w_ckv, w_kr=d_w_kr, qa=d_qa, kva=d_kva, w_q_up=d_w_q_up,
                    w_kv_up=d_w_kv_up, qn=d_qn, kn=d_kn, w_out=d_w_out)


SHARDED = (("hgrn_w_in", 2), ("hgrn_w_out", 1), ("mla_w_in", 1), ("mla_q_a_norm", 1), ("mla_w_q_up", 2),
           ("mla_kv_a_norm", 1), ("mla_w_kv_up", 2), ("mla_w_out", 1), ("ffn_w_up", 2), ("ffn_conv_w", 2), ("ffn_w_down", 1))
GATHER_F32 = ("mla_q_a_norm", "mla_kv_a_norm", "ffn_conv_w")
REPLICATED = ("norm_mix", "norm_ffn", "hgrn_lower_bounds", "hgrn_out_norm", "mla_q_norm", "mla_k_norm", "ffn_conv_b")
WEIGHTS = ("norm_mix", "norm_ffn", "hgrn_w_in", "hgrn_lower_bounds", "hgrn_out_norm", "hgrn_w_out", "mla_w_in",
           "mla_q_a_norm", "mla_w_q_up", "mla_kv_a_norm", "mla_w_kv_up", "mla_q_norm", "mla_k_norm", "mla_w_out",
           "ffn_w_up", "ffn_conv_w", "ffn_conv_b", "ffn_w_down")
N_CHIPS = 4


def _gather_weights(w):
    big = [n for n, _ in SHARDED if n not in GATHER_F32]
    small = [n for n, _ in SHARDED if n in GATHER_F32]
    axis = dict(SHARDED)
    got_big, got_small = exchange([_pack([w[n] for n in big], _MXU, 16), _pack([w[n] for n in small], F32, SUBLANES)],
                                  "chips", False, "gather_weights")
    full = {}
    for names, got in ((big, got_big), (small, got_small)):
        per_chip = [_unpack(got[p], [w[n].shape for n in names]) for p in range(N_CHIPS)]
        for k, n in enumerate(names):
            full[n] = jnp.concatenate([per_chip[p][k] for p in range(N_CHIPS)], axis=axis[n])
    return full


def _pad_cols(a, width):
    return jnp.pad(a, [(0, 0)] * (a.ndim - 1) + [(0, width - a.shape[-1])])


def _head_slots(w):
    lead = w.shape[:-1]
    return _pad_cols(w.reshape(lead + (HEADS, MLA_QK)), MLA_SLOT).reshape(lead + (HEADS * MLA_SLOT,))


def _head_unslots(w):
    lead = w.shape[:-1]
    return w.reshape(lead + (HEADS, MLA_SLOT))[..., :MLA_QK].reshape(lead + (HEADS * MLA_QK,))


def _rope_tables(positions):
    inv_freq = ROPE_THETA ** (-jnp.arange(0, MLA_ROPE, 2, dtype=F32) / MLA_ROPE)
    ang = positions.astype(F32)[:, None] * inv_freq
    cos, sin = jnp.cos(ang), jnp.sin(ang)
    S = positions.shape[0]
    ones, zeros = jnp.ones((S, MLA_NOPE), F32), jnp.zeros((S, MLA_SLOT - MLA_QK), F32)
    return (jnp.concatenate([ones, cos, cos, zeros], axis=1),
            jnp.concatenate([jnp.zeros((S, MLA_NOPE), F32), -sin, sin, zeros], axis=1))


def kernel(x, positions, norm_mix, norm_ffn, hgrn_w_in, hgrn_lower_bounds, hgrn_out_norm, hgrn_w_out, mla_w_in, mla_q_a_norm, mla_w_q_up, mla_kv_a_norm, mla_w_kv_up, mla_q_norm, mla_k_norm, mla_w_out, ffn_w_up, ffn_conv_w, ffn_conv_b, ffn_w_down, loss_target, m_norm_mix, m_norm_ffn, m_hgrn_w_in, m_hgrn_lower_bounds, m_hgrn_out_norm, m_hgrn_w_out, m_mla_w_in, m_mla_q_a_norm, m_mla_w_q_up, m_mla_kv_a_norm, m_mla_w_kv_up, m_mla_q_norm, m_mla_k_norm, m_mla_w_out, m_ffn_w_up, m_ffn_conv_w, m_ffn_conv_b, m_ffn_w_down, v_norm_mix, v_norm_ffn, v_hgrn_w_in, v_hgrn_lower_bounds, v_hgrn_out_norm, v_hgrn_w_out, v_mla_w_in, v_mla_q_a_norm, v_mla_w_q_up, v_mla_kv_a_norm, v_mla_w_kv_up, v_mla_q_norm, v_mla_k_norm, v_mla_w_out, v_ffn_w_up, v_ffn_conv_w, v_ffn_conv_b, v_ffn_w_down):
    args = dict(locals())
    w = {n: args[n] for n in WEIGHTS}
    m = {n: args["m_" + n] for n in WEIGHTS}
    v = {n: args["v_" + n] for n in WEIGHTS}
    depth = norm_mix.shape[0]
    x0 = x[0]
    S = x0.shape[0]

    full = _gather_weights(w)
    cos_t, sin_t = _rope_tables(positions[0])
    lbs = lower_bound_fwd(hgrn_lower_bounds, "lower_bound_fwd")
    mall = jnp.asarray(_hgrn_sum_matrix(min(HGRN_CHUNK, S)), _MXU)
    qn = _pad_cols(mla_q_norm, MLA_SLOT)
    kn = _pad_cols(mla_k_norm, MLA_SLOT)
    cb = ffn_conv_b
    cw8 = jnp.pad(full["ffn_conv_w"], ((0, 0), (0, SUBLANES - 3), (0, 0)))

    def mixer_weights(layer):
        j = layer // 2
        if layer % 2 == 0:
            w_in = full["hgrn_w_in"][j]
            return (norm_mix[layer:layer + 1], [w_in[:, k * D_MODEL:(k + 1) * D_MODEL] for k in range(4)],
                    lbs[j:j + 1], hgrn_out_norm[j:j + 1], full["hgrn_w_out"][j], mall)
        w_in = full["mla_w_in"][j]
        return (norm_mix[layer:layer + 1], w_in[:, :MLA_LORA], w_in[:, MLA_LORA:2 * MLA_LORA],
                _pad_cols(w_in[:, 2 * MLA_LORA:], HEAD_DIM), full["mla_q_a_norm"][j:j + 1], full["mla_kv_a_norm"][j:j + 1],
                _head_slots(full["mla_w_q_up"][j]), full["mla_w_kv_up"][j], qn[j:j + 1], kn[j:j + 1], full["mla_w_out"][j],
                cos_t, sin_t)

    def ffn_weights(layer):
        return (norm_ffn[layer:layer + 1], full["ffn_w_up"][layer], cw8[layer], cb[layer:layer + 1], full["ffn_w_down"][layer])

    h = x0
    saved = []
    for layer in range(depth):
        fwd = _hgrn_layer_fwd if layer % 2 == 0 else _mla_layer_fwd
        h, s_mix = fwd(h, *mixer_weights(layer))
        h, s_ffn = _ffn_fwd(h, *ffn_weights(layer))
        saved.append((s_mix, s_ffn))

    dh, loss_blk = loss_head(h, loss_target[0], "loss_head")
    loss = lax.psum(loss_blk[0, 0], MESH_AXES)

    g_mix, g_ffn = [None] * depth, [None] * depth
    for layer in reversed(range(depth)):
        s_mix, s_ffn = saved[layer]
        dh, g_ffn[layer] = _ffn_bwd(dh, s_ffn, *ffn_weights(layer))
        bwd = _hgrn_layer_bwd if layer % 2 == 0 else _mla_layer_bwd
        dh, g_mix[layer] = bwd(dh, s_mix, *mixer_weights(layer))
    hg = [g_mix[l] for l in range(0, depth, 2)]
    mg = [g_mix[l] for l in range(1, depth, 2)]
    d_p = lower_bound_bwd(hgrn_lower_bounds, hg[1]["lb"], "lower_bound_bwd")

    grads = {
        "norm_mix": jnp.concatenate([g["gain"] for g in g_mix], axis=0),
        "norm_ffn": jnp.concatenate([g["gain"] for g in g_ffn], axis=0),
        "hgrn_w_in": jnp.stack([jnp.concatenate(g["w_in4"], axis=1) for g in hg]),
        "hgrn_lower_bounds": d_p[0:2],
        "hgrn_out_norm": jnp.concatenate([g["out_gain"] for g in hg], axis=0),
        "hgrn_w_out": jnp.stack([g["w_out"] for g in hg]),
        "mla_w_in": jnp.stack([jnp.concatenate([g["w_cq"], g["w_ckv"], g["w_kr"][:, :MLA_ROPE]], axis=1) for g in mg]),
        "mla_q_a_norm": jnp.concatenate([g["qa"] for g in mg], axis=0),
        "mla_w_q_up": jnp.stack([_head_unslots(g["w_q_up"]) for g in mg]),
        "mla_kv_a_norm": jnp.concatenate([g["kva"] for g in mg], axis=0),
        "mla_w_kv_up": jnp.stack([g["w_kv_up"] for g in mg]),
        "mla_q_norm": jnp.concatenate([g["qn"][:, :MLA_QK] for g in mg], axis=0),
        "mla_k_norm": jnp.concatenate([g["kn"][:, :MLA_QK] for g in mg], axis=0),
        "mla_w_out": jnp.stack([g["w_out"] for g in mg]),
        "ffn_w_up": jnp.stack([g["w_up"] for g in g_ffn]),
        "ffn_conv_w": jnp.stack([g["conv_w"] for g in g_ffn]),
        "ffn_conv_b": jnp.concatenate([g["conv_b"] for g in g_ffn], axis=0),
        "ffn_w_down": jnp.stack([g["w_down"] for g in g_ffn]),
    }

    names = [n for n, _ in SHARDED]
    shapes = [w[n].shape for n in names]

    def chip_slab(p):
        parts = []
        for (n, ax), shp in zip(SHARDED, shapes):
            parts.append(lax.slice_in_dim(grads[n], p * shp[ax], (p + 1) * shp[ax], axis=ax))
        return _pack(parts, F32, PACK_ROWS)

    send = jnp.stack([chip_slab(p) for p in range(N_CHIPS)])
    (from_chips,) = exchange([send], "chips", True, "reduce_chips")
    partial_sum = sum_slots(from_chips, "sum_slots")
    (both_cores,) = exchange([partial_sum], "sibling", False, "reduce_cores")
    packed = adamw(both_cores, *[_pack([t[n] for n in names], F32, PACK_ROWS) for t in (w, m, v)], "adamw")
    sharded_out = [dict(zip(names, _unpack(buf, shapes))) for buf in packed]

    rshapes = [w[n].shape for n in REPLICATED]
    (from_all,) = exchange([_pack([grads[n] for n in REPLICATED], F32, SUBLANES)], "all", False, "reduce_all")
    rpacked = adamw(from_all, *[_pack([t[n] for n in REPLICATED], F32, SUBLANES) for t in (w, m, v)], "adamw_small")
    repl_out = [dict(zip(REPLICATED, _unpack(buf, rshapes))) for buf in rpacked]

    result = [loss, dh[None]]
    for k in range(4):
        result += [sharded_out[k][n] if n in sharded_out[k] else repl_out[k][n] for n in WEIGHTS]
    return tuple(result)
```

```python
import functools

import numpy as np
import jax
import jax.numpy as jnp
from jax import lax
from jax.experimental import pallas as pl
from jax.experimental.pallas import tpu as pltpu

F32 = jnp.float32
BF16 = jnp.bfloat16
_MXU = BF16

RMS_EPS = 1e-6
D_MODEL = 1024
HEADS = 8
HEAD_DIM = 128
HGRN_CHUNK = 64
MLA_NOPE = 128
MLA_ROPE = 64
MLA_QK = MLA_NOPE + MLA_ROPE
MLA_SLOT = 256
MLA_LORA = 256
ROPE_THETA = 10000.0
D_FF = 2816
FF_BLOCK = 1408
LANES = 128
SUBLANES = 8

ADAM_LR = 0.001
ADAM_B1 = 0.9
ADAM_B2 = 0.999
ADAM_EPS = 1e-08
ADAM_WD = 0.01
ADAM_STEP = 10

VMEM_LIMIT = 56 * 1024 * 1024
MESH_AXES = ("x", "y", "c")

_NN = ((1,), (0,))
_NT = ((1,), (1,))
_TN = ((0,), (0,))


def _dg(a, b, dims):
    return lax.dot_general(a.astype(_MXU), b.astype(_MXU), (dims, ((), ())), preferred_element_type=F32)


@jax.custom_vjp
def kdot(a, b):
    return _dg(a, b, _NN)


kdot.defvjp(lambda a, b: (_dg(a, b, _NN), (a, b)), lambda r, g: (_dg(g, r[1], _NT), _dg(r[0], g, _TN)))


@jax.custom_vjp
def kdot_nt(a, b):
    return _dg(a, b, _NT)


kdot_nt.defvjp(lambda a, b: (_dg(a, b, _NT), (a, b)), lambda r, g: (_dg(g, r[1], _NN), _dg(g, r[0], _TN)))


@jax.custom_vjp
def kdot_tn(a, b):
    return _dg(a, b, _TN)


kdot_tn.defvjp(lambda a, b: (_dg(a, b, _TN), (a, b)), lambda r, g: (_dg(r[1], g, _NT), _dg(r[0], g, _NN)))


def _pick(d, prefs):
    for p in prefs:
        if d >= p and d % p == 0:
            return p
    return d


def _params(sem):
    return pltpu.CompilerParams(dimension_semantics=sem, vmem_limit_bytes=VMEM_LIMIT)


def mm(a, b, mode, name, add=None, out_dtype=F32):
    if mode == "nn":
        (M, K), (K2, N) = a.shape, b.shape
    elif mode == "nt":
        (M, K), (N, K2) = a.shape, b.shape
    else:
        (K, M), (K2, N) = a.shape, b.shape
    assert K == K2, (name, a.shape, b.shape)
    tm = M if M <= 1024 else _pick(M, (1024, 1408, 512, 256, 128))
    tn = N if N <= 1024 else _pick(N, (1024, 1408, 512, 256, 128))
    tk = K if K <= 512 else _pick(K, (512, 256, 128))
    nk = K // tk
    dims = {"nn": _NN, "nt": _NT, "tn": _TN}[mode]
    a_spec = {"nn": pl.BlockSpec((tm, tk), lambda i, j, k: (i, k)),
              "nt": pl.BlockSpec((tm, tk), lambda i, j, k: (i, k)),
              "tn": pl.BlockSpec((tk, tm), lambda i, j, k: (k, i))}[mode]
    b_spec = {"nn": pl.BlockSpec((tk, tn), lambda i, j, k: (k, j)),
              "nt": pl.BlockSpec((tn, tk), lambda i, j, k: (j, k)),
              "tn": pl.BlockSpec((tk, tn), lambda i, j, k: (k, j))}[mode]
    o_spec = pl.BlockSpec((tm, tn), lambda i, j, k: (i, j))
    has_add = add is not None

    def kern(*refs):
        a_ref, b_ref = refs[0], refs[1]
        add_ref = refs[2] if has_add else None
        o_ref, acc = refs[-2], refs[-1]
        k = pl.program_id(2)

        @pl.when(k == 0)
        def _():
            acc[...] = jnp.zeros_like(acc)

        acc[...] += _dg(a_ref[...], b_ref[...], dims)

        @pl.when(k == nk - 1)
        def _():
            r = acc[...]
            if has_add:
                r = r + add_ref[...].astype(F32)
            o_ref[...] = r.astype(o_ref.dtype)

    ins = [a, b] + ([add] if has_add else [])
    specs = [a_spec, b_spec] + ([o_spec] if has_add else [])
    return pl.pallas_call(
        kern, name=name, grid=(M // tm, N // tn, nk), in_specs=specs, out_specs=o_spec,
        out_shape=jax.ShapeDtypeStruct((M, N), out_dtype), scratch_shapes=[pltpu.VMEM((tm, tn), F32)],
        compiler_params=_params(("parallel", "parallel", "arbitrary")))(*ins)


def _store(ref, val, first):
    if first is None:
        ref[...] = val.astype(ref.dtype)
        return

    @pl.when(first)
    def _():
        ref[...] = val.astype(ref.dtype)

    @pl.when(jnp.logical_not(first))
    def _():
        ref[...] += val.astype(ref.dtype)


def tilecall(body, name, grid, ins, outs, sem):
    n_in = len(ins)

    def kern(*refs):
        vals = body(*refs[:n_in])
        for ref, val, (_, _, first) in zip(refs[n_in:], vals, outs):
            _store(ref, val, None if first is None else first())

    return pl.pallas_call(
        kern, name=name, grid=grid, in_specs=[s for _, s in ins], out_specs=[s for _, s, _ in outs],
        out_shape=[sh for sh, _, _ in outs], compiler_params=_params(sem))(*[a for a, _ in ins])


def _rms(x, g, n):
    ms = jnp.sum(x * x, axis=-1, keepdims=True) / n
    return x * lax.rsqrt(ms + RMS_EPS) * g


def _sds(shape, dtype=F32):
    return jax.ShapeDtypeStruct(shape, dtype)


def _row_tile(S, w):
    return min(S, 512 if w <= 1024 else 256)


def rms_fwd(x, g, name, col=0, w=None):
    S = x.shape[0]
    w = w or x.shape[1]
    ts = _row_tile(S, w)
    return tilecall(
        lambda x_ref, g_ref: (_rms(x_ref[...], g_ref[...], w),), name, (S // ts,),
        [(x, pl.BlockSpec((ts, w), lambda i: (i, col))), (g, pl.BlockSpec((1, w), lambda i: (0, 0)))],
        [(_sds((S, w), _MXU), pl.BlockSpec((ts, w), lambda i: (i, 0)), None)], ("parallel",))[0]


def rms_bwd(x, g, dh, res, name, w=None):
    S = x.shape[0]
    w = w or x.shape[1]
    ts = _row_tile(S, w)

    def body(x_ref, g_ref, dh_ref, *rest):
        _, vjp = jax.vjp(lambda xv, gv: _rms(xv, gv, w), x_ref[...], g_ref[...])
        dx, dg = vjp(dh_ref[...].astype(F32))
        if rest:
            dx = dx + rest[0][...]
        return dx, dg

    row = pl.BlockSpec((ts, w), lambda i: (i, 0))
    vec = pl.BlockSpec((1, w), lambda i: (0, 0))
    ins = [(x, row), (g, vec), (dh, row)] + ([(res, row)] if res is not None else [])
    return tilecall(body, name, (S // ts,), ins,
                    [(_sds((S, w)), row, None), (_sds((1, w)), vec, lambda: pl.program_id(0) == 0)], ("arbitrary",))


def _shifted(u, halo_ref, is_first):
    rid = lax.broadcasted_iota(jnp.int32, (u.shape[0], 1), 0)
    h7 = jnp.where(is_first, 0.0, halo_ref[7:8, :])
    h6 = jnp.where(is_first, 0.0, halo_ref[6:7, :])
    u1 = jnp.where(rid == 0, h7, pltpu.roll(u, 1, 0))
    u2 = jnp.where(rid == 0, h6, jnp.where(rid == 1, h7, pltpu.roll(u, 2, 0)))
    return u1, u2


def _conv(u, u1, u2, cw_ref, cb_ref):
    return ((cb_ref[...] + u2 * cw_ref[0:1, :]) + u1 * cw_ref[1:2, :]) + u * cw_ref[2:3, :]


def _ffn_specs(S, ts, jmap):
    hb = ts // SUBLANES
    return (pl.BlockSpec((ts, FF_BLOCK), lambda j, i: (i, jmap(j))),
            pl.BlockSpec((SUBLANES, FF_BLOCK), lambda j, i: (jnp.maximum(i * hb - 1, 0), jmap(j))),
            pl.BlockSpec((SUBLANES, FF_BLOCK), lambda j, i: (0, jmap(j))),
            pl.BlockSpec((1, FF_BLOCK), lambda j, i: (0, jmap(j))))


def ffn_act_fwd(u, cw8, cb, name):
    S = u.shape[0]
    ts = _row_tile(S, 2 * D_FF)
    nb = D_FF // FF_BLOCK

    def body(ug, hg, cwg, cbg, uu, hu, cwu, cbu):
        first = pl.program_id(1) == 0
        g = ug[...]
        g1, g2 = _shifted(g, hg, first)
        yg = _conv(g, g1, g2, cwg, cbg)
        v = uu[...]
        v1, v2 = _shifted(v, hu, first)
        yu = _conv(v, v1, v2, cwu, cbu)
        return (yg * jax.nn.sigmoid(yg) * yu,)

    sg = _ffn_specs(S, ts, lambda j: j)
    su = _ffn_specs(S, ts, lambda j: j + nb)
    ins = [(u, sg[0]), (u, sg[1]), (cw8, sg[2]), (cb, sg[3]), (u, su[0]), (u, su[1]), (cw8, su[2]), (cb, su[3])]
    return tilecall(body, name, (nb, S // ts), ins,
                    [(_sds((S, D_FF), _MXU), pl.BlockSpec((ts, FF_BLOCK), lambda j, i: (i, j)), None)],
                    ("parallel", "parallel"))[0]


def ffn_act_bwd(u, cw8, cb, da, name):
    S = u.shape[0]
    ts = _row_tile(S, 2 * D_FF)
    nb = D_FF // FF_BLOCK

    def body(um, hm, cwm, cbm, up, hp, cwp, cbp, da_ref):
        j = pl.program_id(0)
        first = pl.program_id(1) == 0
        a = um[...]
        a1, a2 = _shifted(a, hm, first)
        ym = _conv(a, a1, a2, cwm, cbm)
        b = up[...]
        b1, b2 = _shifted(b, hp, first)
        yp = _conv(b, b1, b2, cwp, cbp)
        d = da_ref[...]
        sm = jax.nn.sigmoid(ym)
        d_gate = d * yp * (sm * (1.0 + ym * (1.0 - sm)))
        d_up = d * (yp * jax.nn.sigmoid(yp))
        dy = jnp.where(j < nb, d_gate, d_up)
        dcw = jnp.concatenate(
            [jnp.sum(dy * a2, axis=0, keepdims=True), jnp.sum(dy * a1, axis=0, keepdims=True),
             jnp.sum(dy * a, axis=0, keepdims=True), jnp.zeros((SUBLANES - 3, dy.shape[1]), F32)], axis=0)
        return dy, dcw, jnp.sum(dy, axis=0, keepdims=True)

    sm_ = _ffn_specs(S, ts, lambda j: j)
    sp_ = _ffn_specs(S, ts, lambda j: (j + nb) % (2 * nb))
    ins = [(u, sm_[0]), (u, sm_[1]), (cw8, sm_[2]), (cb, sm_[3]), (u, sp_[0]), (u, sp_[1]), (cw8, sp_[2]), (cb, sp_[3]),
           (da, pl.BlockSpec((ts, FF_BLOCK), lambda j, i: (i, j % nb)))]
    first_row = lambda: pl.program_id(1) == 0
    return tilecall(body, name, (2 * nb, S // ts), ins,
                    [(_sds((S, 2 * D_FF)), sm_[0], None), (_sds((SUBLANES, 2 * D_FF)), sm_[2], first_row),
                     (_sds((1, 2 * D_FF)), sm_[3], first_row)], ("parallel", "arbitrary"))


def ffn_conv_bwd(dy, cw8, name):
    S = dy.shape[0]
    ts = _row_tile(S, 2 * D_FF)
    hb = ts // SUBLANES
    nrow = S // ts
    ncol = 2 * D_FF // FF_BLOCK

    def body(dy_ref, halo_ref, cw_ref):
        last = pl.program_id(1) == nrow - 1
        d = dy_ref[...]
        rid = lax.broadcasted_iota(jnp.int32, (ts, 1), 0)
        n0 = jnp.where(last, 0.0, halo_ref[0:1, :])
        n1 = jnp.where(last, 0.0, halo_ref[1:2, :])
        d1 = jnp.where(rid == ts - 1, n0, pltpu.roll(d, ts - 1, 0))
        d2 = jnp.where(rid == ts - 1, n1, jnp.where(rid == ts - 2, n0, pltpu.roll(d, ts - 2, 0)))
        return (d * cw_ref[2:3, :] + d1 * cw_ref[1:2, :] + d2 * cw_ref[0:1, :],)

    row = pl.BlockSpec((ts, FF_BLOCK), lambda j, i: (i, j))
    ins = [(dy, row), (dy, pl.BlockSpec((SUBLANES, FF_BLOCK), lambda j, i: (jnp.minimum((i + 1) * hb, S // SUBLANES - 1), j))),
           (cw8, pl.BlockSpec((SUBLANES, FF_BLOCK), lambda j, i: (0, j)))]
    return tilecall(body, name, (ncol, nrow), ins, [(_sds((S, 2 * D_FF), _MXU), row, None)], ("parallel", "parallel"))[0]


def _hgrn_levels(C):
    out, m = [], C // 2
    while m >= 1:
        out.append(m)
        m //= 2
    return out


def _hgrn_sum_matrix(C):
    t = np.arange(C)[:, None]
    u = np.arange(C)[None, :]
    blocks = [u <= t, u > t]
    for m in _hgrn_levels(C):
        r = (t // (2 * m)) * (2 * m) + m
        right = (t % (2 * m)) >= m
        blocks.append(right & (u > r) & (u <= t))
        blocks.append((~right) & (u > t) & (u <= r))
    return np.concatenate(blocks, axis=0).astype(np.float32)


def _make_partial_sums(nb, C):
    @jax.custom_vjp
    def sums(mall, lf):
        hi = lf.astype(_MXU)
        r1 = lf - hi.astype(F32)
        mid = r1.astype(_MXU)
        lo = (r1 - mid.astype(F32)).astype(_MXU)
        e = _dg(mall, hi, _NN) + _dg(mall, mid, _NN) + _dg(mall, lo, _NN)
        return tuple(e[b * C:(b + 1) * C] for b in range(nb))

    def fwd(mall, lf):
        return sums(mall, lf), mall

    def bwd(mall, gs):
        return jnp.zeros_like(mall), _dg(mall, jnp.concatenate(gs, axis=0), _TN)

    sums.defvjp(fwd, bwd)
    return sums


def _hgrn_chunk(zq, zf, v, lb, st, mall, C):
    levels = _hgrn_levels(C)
    qs = zq * jax.nn.sigmoid(zq)
    fg = lb + (1.0 - lb) * jax.nn.sigmoid(zf)
    k = 1.0 - fg
    e = _make_partial_sums(2 + 2 * len(levels), C)(mall, jnp.log(fg))
    g_incl, g_after = e[0], e[1]
    rid = lax.broadcasted_iota(jnp.int32, (C, 1), 0)
    tt = lax.broadcasted_iota(jnp.int32, (C, C), 0)
    ss = lax.broadcasted_iota(jnp.int32, (C, C), 1)
    o = kdot_nt(qs * jnp.exp(g_incl), st)
    o = o + jnp.sum(qs * k, axis=-1, keepdims=True) * v
    scores = jnp.zeros((C, C), F32)
    for li, m in enumerate(levels):
        sh = int(np.log2(m))
        right = ((rid >> sh) & 1) == 1
        qt = jnp.where(right, qs * jnp.exp(e[2 + 2 * li]), 0.0)
        kt = jnp.where(right, 0.0, k * jnp.exp(e[3 + 2 * li]))
        pair = ((tt >> (sh + 1)) == (ss >> (sh + 1))) & (((tt >> sh) & 1) == 1) & (((ss >> sh) & 1) == 0)
        scores = scores + jnp.where(pair, kdot_nt(qt, kt), 0.0)
    o = o + kdot(scores, v)
    g_last = jnp.sum(jnp.where(rid == C - 1, g_incl, 0.0), axis=0, keepdims=True)
    st_new = st * jnp.exp(g_last) + kdot_tn(v, k * jnp.exp(g_after))
    return o, st_new


def _hgrn_in_specs(C, nc, rev):
    cm = (lambda c: nc - 1 - c) if rev else (lambda c: c)
    blk = lambda: pl.BlockSpec((C, HEAD_DIM), lambda h, c: (cm(c), h))
    return cm, [blk(), blk(), blk(), pl.BlockSpec((1, HEAD_DIM), lambda h, c: (0, h))]


def hgrn_fwd(zq, zf, zi, lb, mall, name):
    S = zq.shape[0]
    C = min(HGRN_CHUNK, S)
    nc = S // C

    def kern(zq_ref, zf_ref, zi_ref, lb_ref, mall_ref, o_ref, st_ref, st):
        @pl.when(pl.program_id(1) == 0)
        def _():
            st[...] = jnp.zeros_like(st)

        s_in = st[...]
        st_ref[...] = s_in
        o, s_new = _hgrn_chunk(zq_ref[...], zf_ref[...], zi_ref[...], lb_ref[...], s_in, mall_ref[...], C)
        o_ref[...] = o
        st[...] = s_new

    _, specs = _hgrn_in_specs(C, nc, False)
    return pl.pallas_call(
        kern, name=name, grid=(HEADS, nc),
        in_specs=specs + [pl.BlockSpec(mall.shape, lambda h, c: (0, 0))],
        out_specs=[pl.BlockSpec((C, HEAD_DIM), lambda h, c: (c, h)),
                   pl.BlockSpec((None, None, HEAD_DIM, HEAD_DIM), lambda h, c: (h, c, 0, 0))],
        out_shape=[_sds((S, D_MODEL)), _sds((HEADS, nc, HEAD_DIM, HEAD_DIM))],
        scratch_shapes=[pltpu.VMEM((HEAD_DIM, HEAD_DIM), F32)],
        compiler_params=_params(("parallel", "arbitrary")))(zq, zf, zi, lb, mall)


def hgrn_bwd(zq, zf, zi, lb, mall, states, do, name):
    S = zq.shape[0]
    C = min(HGRN_CHUNK, S)
    nc = S // C

    def kern(zq_ref, zf_ref, zi_ref, lb_ref, mall_ref, st_ref, do_ref, dq_ref, df_ref, di_ref, dlb_ref, dst):
        first = pl.program_id(1) == 0

        @pl.when(first)
        def _():
            dst[...] = jnp.zeros_like(dst)

        mall_v = mall_ref[...]
        _, vjp = jax.vjp(lambda a, b, c, d, e: _hgrn_chunk(a, b, c, d, e, mall_v, C),
                         zq_ref[...], zf_ref[...], zi_ref[...], lb_ref[...], st_ref[...])
        ga, gb, gv, gl, gs = vjp((do_ref[...], dst[...]))
        dq_ref[...] = ga.astype(dq_ref.dtype)
        df_ref[...] = gb.astype(df_ref.dtype)
        di_ref[...] = gv.astype(di_ref.dtype)
        _store(dlb_ref, gl, first)
        dst[...] = gs

    cm, specs = _hgrn_in_specs(C, nc, True)
    row = lambda: pl.BlockSpec((C, HEAD_DIM), lambda h, c: (cm(c), h))
    return pl.pallas_call(
        kern, name=name, grid=(HEADS, nc),
        in_specs=specs + [pl.BlockSpec(mall.shape, lambda h, c: (0, 0)),
                          pl.BlockSpec((None, None, HEAD_DIM, HEAD_DIM), lambda h, c: (h, cm(c), 0, 0)), row()],
        out_specs=[row(), row(), row(), pl.BlockSpec((1, HEAD_DIM), lambda h, c: (0, h))],
        out_shape=[_sds((S, D_MODEL), _MXU), _sds((S, D_MODEL), _MXU), _sds((S, D_MODEL), _MXU), _sds((1, D_MODEL))],
        scratch_shapes=[pltpu.VMEM((HEAD_DIM, HEAD_DIM), F32)],
        compiler_params=_params(("parallel", "arbitrary")))(zq, zf, zi, lb, mall, states, do)


def _hgrn_out(o, g, gain):
    return _rms(o, gain, HEAD_DIM) * (g * jax.nn.sigmoid(g))


def hgrn_out_fwd(o, zg, gain, name):
    S = o.shape[0]
    ts = _row_tile(S, D_MODEL)
    blk = pl.BlockSpec((ts, HEAD_DIM), lambda i, h: (i, h))
    return tilecall(lambda o_ref, g_ref, w_ref: (_hgrn_out(o_ref[...], g_ref[...], w_ref[...]),), name, (S // ts, HEADS),
                    [(o, blk), (zg, blk), (gain, pl.BlockSpec((1, HEAD_DIM), lambda i, h: (0, 0)))],
                    [(_sds((S, D_MODEL), _MXU), blk, None)], ("parallel", "parallel"))[0]


def hgrn_out_bwd(o, zg, gain, don, name):
    S = o.shape[0]
    ts = _row_tile(S, D_MODEL)
    blk = pl.BlockSpec((ts, HEAD_DIM), lambda i, h: (i, h))
    vec = pl.BlockSpec((1, HEAD_DIM), lambda i, h: (0, 0))

    def body(o_ref, g_ref, w_ref, d_ref):
        _, vjp = jax.vjp(_hgrn_out, o_ref[...], g_ref[...], w_ref[...])
        return vjp(d_ref[...])

    return tilecall(body, name, (S // ts, HEADS), [(o, blk), (zg, blk), (gain, vec), (don, blk)],
                    [(_sds((S, D_MODEL)), blk, None), (_sds((S, D_MODEL), _MXU), blk, None),
                     (_sds((1, HEAD_DIM)), vec, lambda: (pl.program_id(0) == 0) & (pl.program_id(1) == 0))],
                    ("arbitrary", "arbitrary"))


def lower_bound_fwd(p, name):
    assert p.shape[0] == 2

    def body(p_ref):
        s = _lb_soft(p_ref[0:1, :], p_ref[1:2, :])
        return (jnp.concatenate([jnp.zeros_like(s), s] + [jnp.zeros_like(s)] * (SUBLANES - 2), axis=0),)

    spec8 = pl.BlockSpec((SUBLANES, p.shape[1]), lambda: (0, 0))
    return tilecall(body, name, (), [(p, pl.BlockSpec(p.shape, lambda: (0, 0)))], [(_sds((SUBLANES, p.shape[1])), spec8, None)], ())[0]


def _lb_soft(p0, p1):
    mx = jnp.maximum(p0, p1)
    e0, e1 = jnp.exp(p0 - mx), jnp.exp(p1 - mx)
    s0, s1 = e0 / (e0 + e1), e1 / (e0 + e1)
    return (s0 + s1) - s0


def lower_bound_bwd(p, dlb1, name):
    def body(p_ref, d_ref):
        _, vjp = jax.vjp(_lb_soft, p_ref[0:1, :], p_ref[1:2, :])
        g0, g1 = vjp(d_ref[...])
        return (jnp.concatenate([g0, g1] + [jnp.zeros_like(g0)] * (SUBLANES - 2), axis=0),)

    spec8 = pl.BlockSpec((SUBLANES, p.shape[1]), lambda: (0, 0))
    return tilecall(body, name, (), [(p, pl.BlockSpec(p.shape, lambda: (0, 0))), (dlb1, pl.BlockSpec(dlb1.shape, lambda: (0, 0)))],
                    [(_sds((SUBLANES, p.shape[1])), spec8, None)], ())[0]


@jax.custom_vjp
def _swap_rope_halves(x):
    lane = lax.broadcasted_iota(jnp.int32, x.shape, 1)
    lo = (lane >= MLA_NOPE) & (lane < MLA_NOPE + MLA_ROPE // 2)
    hi = (lane >= MLA_NOPE + MLA_ROPE // 2) & (lane < MLA_QK)
    return jnp.where(lo, pltpu.roll(x, MLA_SLOT - MLA_ROPE // 2, 1), jnp.where(hi, pltpu.roll(x, MLA_ROPE // 2, 1), 0.0))


_swap_rope_halves.defvjp(lambda x: (_swap_rope_halves(x), None), lambda _, g: (_swap_rope_halves(g),))


def _norm_rope(x, gain, cos_t, sin_t):
    y = _rms(x, gain, MLA_QK)
    return y * cos_t + _swap_rope_halves(y) * sin_t


def _qk_heads(qs, kn, kr, qn, kn_gain, cos_t, sin_t):
    q = _norm_rope(qs, qn, cos_t, sin_t)
    k = _norm_rope(jnp.concatenate([kn, kr], axis=1), kn_gain, cos_t, sin_t)
    return q, k


def _qk_specs(ts):
    slot = pl.BlockSpec((ts, MLA_SLOT), lambda i, h: (i, h))
    nope = pl.BlockSpec((ts, HEAD_DIM), lambda i, h: (i, 2 * h))
    shared = pl.BlockSpec((ts, HEAD_DIM), lambda i, h: (i, 0))
    gain = pl.BlockSpec((1, MLA_SLOT), lambda i, h: (0, 0))
    table = pl.BlockSpec((ts, MLA_SLOT), lambda i, h: (i, 0))
    return slot, nope, shared, gain, table


def qk_fwd(qslots, kv, krope, qn, kn, cos_t, sin_t, name):
    S = qslots.shape[0]
    ts = _row_tile(S, D_MODEL)
    slot, nope, shared, gain, table = _qk_specs(ts)

    def body(q_ref, kn_ref, kr_ref, qn_ref, kg_ref, c_ref, s_ref):
        return _qk_heads(q_ref[...], kn_ref[...], kr_ref[...], qn_ref[...], kg_ref[...], c_ref[...], s_ref[...])

    out = _sds((S, HEADS * MLA_SLOT), _MXU)
    return tilecall(body, name, (S // ts, HEADS),
                    [(qslots, slot), (kv, nope), (krope, shared), (qn, gain), (kn, gain), (cos_t, table), (sin_t, table)],
                    [(out, slot, None), (out, slot, None)], ("parallel", "parallel"))


def qk_bwd(qslots, kv, krope, qn, kn, cos_t, sin_t, dq, dk, dv, name):
    S = qslots.shape[0]
    ts = _row_tile(S, D_MODEL)
    slot, nope, shared, gain, table = _qk_specs(ts)
    vblk = pl.BlockSpec((ts, HEAD_DIM), lambda i, h: (i, h))

    def body(q_ref, kn_ref, kr_ref, qn_ref, kg_ref, c_ref, s_ref, dq_ref, dk_ref, dv_ref):
        c, s = c_ref[...], s_ref[...]
        _, vjp = jax.vjp(lambda a, b, r, g1, g2: _qk_heads(a, b, r, g1, g2, c, s),
                         q_ref[...], kn_ref[...], kr_ref[...], qn_ref[...], kg_ref[...])
        ga, gb, gr, g1, g2 = vjp((dq_ref[...], dk_ref[...]))
        return ga, jnp.concatenate([gb, dv_ref[...]], axis=1), gr, g1, g2

    first_head = lambda: pl.program_id(1) == 0
    first = lambda: (pl.program_id(0) == 0) & (pl.program_id(1) == 0)
    wide = _sds((S, HEADS * MLA_SLOT), _MXU)
    return tilecall(body, name, (S // ts, HEADS),
                    [(qslots, slot), (kv, nope), (krope, shared), (qn, gain), (kn, gain), (cos_t, table), (sin_t, table),
                     (dq, slot), (dk, slot), (dv, vblk)],
                    [(wide, slot, None), (wide, slot, None), (_sds((S, HEAD_DIM)), shared, first_head),
                     (_sds((1, MLA_SLOT)), gain, first), (_sds((1, MLA_SLOT)), gain, first)], ("arbitrary", "arbitrary"))


_ATTN_SCALE = MLA_QK ** -0.5


ATTN_TILE = 512


def _causal_pairs(nq, by_row):
    pairs = [(i, j) for i in range(nq) for j in range(i + 1)] if by_row else [(i, j) for j in range(nq) for i in range(j, nq)]
    return jnp.asarray([p[0] for p in pairs], jnp.int32), jnp.asarray([p[1] for p in pairs], jnp.int32)


def _diag_mask(s, tq):
    rows = lax.broadcasted_iota(jnp.int32, (tq, tq), 0)
    cols = lax.broadcasted_iota(jnp.int32, (tq, tq), 1)
    return jnp.where(rows >= cols, s, -jnp.inf)


def attn_fwd(qr, kr, kv, name):
    S = qr.shape[0]
    tq = min(S, ATTN_TILE)
    nq = S // tq
    i_tab, j_tab = _causal_pairs(nq, True)

    def kern(it, jt, q_ref, k_ref, v_ref, o_ref, lse_ref, m_s, l_s, acc):
        n = pl.program_id(1)
        i, j = it[n], jt[n]

        @pl.when(j == 0)
        def _():
            m_s[...] = jnp.full_like(m_s, -jnp.inf)
            l_s[...] = jnp.zeros_like(l_s)
            acc[...] = jnp.zeros_like(acc)

        def step(diagonal):
            s = _dg(q_ref[...], k_ref[...], _NT) * _ATTN_SCALE
            if diagonal:
                s = _diag_mask(s, tq)
            m_prev = m_s[...]
            m_new = jnp.maximum(m_prev, jnp.max(s, axis=-1, keepdims=True))
            alpha = jnp.exp(m_prev - m_new)
            p = jnp.exp(s - m_new)
            l_s[...] = alpha * l_s[...] + jnp.sum(p, axis=-1, keepdims=True)
            acc[...] = alpha * acc[...] + _dg(p, v_ref[...], _NN)
            m_s[...] = m_new

        @pl.when(j < i)
        def _():
            step(False)

        @pl.when(j == i)
        def _():
            step(True)
            l = l_s[...]
            o_ref[...] = acc[...] / l
            lse_ref[...] = jnp.broadcast_to(m_s[...] + jnp.log(l), lse_ref.shape)

    out = _sds((S, HEADS * HEAD_DIM))
    oblk = pl.BlockSpec((tq, HEAD_DIM), lambda h, n, it, jt: (it[n], h))
    return pl.pallas_call(
        kern, name=name,
        grid_spec=pltpu.PrefetchScalarGridSpec(
            num_scalar_prefetch=2, grid=(HEADS, i_tab.shape[0]),
            in_specs=[pl.BlockSpec((tq, MLA_SLOT), lambda h, n, it, jt: (it[n], h)),
                      pl.BlockSpec((tq, MLA_SLOT), lambda h, n, it, jt: (jt[n], h)),
                      pl.BlockSpec((tq, HEAD_DIM), lambda h, n, it, jt: (jt[n], 2 * h + 1))],
            out_specs=[oblk, oblk],
            scratch_shapes=[pltpu.VMEM((tq, 1), F32), pltpu.VMEM((tq, 1), F32), pltpu.VMEM((tq, HEAD_DIM), F32)]),
        out_shape=[out, out], compiler_params=_params(("parallel", "arbitrary")))(i_tab, j_tab, qr, kr, kv)


def attn_bwd(qr, kr, kv, o, lse, do, name):
    S = qr.shape[0]
    tq = min(S, ATTN_TILE)
    nq = S // tq
    i_tab, j_tab = _causal_pairs(nq, False)

    def kern(it, jt, q_ref, k_ref, v_ref, o_ref, lse_ref, do_ref, dq_ref, dk_ref, dv_ref, dk_acc, dv_acc):
        n = pl.program_id(1)
        i, j = it[n], jt[n]

        @pl.when(n == 0)
        def _():
            dq_ref[...] = jnp.zeros_like(dq_ref)

        @pl.when(i == j)
        def _():
            dk_acc[...] = jnp.zeros_like(dk_acc)
            dv_acc[...] = jnp.zeros_like(dv_acc)

        def step(diagonal):
            s = _dg(q_ref[...], k_ref[...], _NT) * _ATTN_SCALE - lse_ref[:, 0:1]
            if diagonal:
                s = _diag_mask(s, tq)
            p = jnp.exp(s)
            g = do_ref[...]
            delta = jnp.sum(g * o_ref[...], axis=-1, keepdims=True)
            dv_acc[...] += _dg(p, g, _TN)
            ds = p * (_dg(g, v_ref[...], _NT) - delta) * _ATTN_SCALE
            dk_acc[...] += _dg(ds, q_ref[...], _TN)
            rows = pl.ds(pl.multiple_of(i * tq, tq), tq)
            dq_ref[rows, :] += _dg(ds, k_ref[...], _NN)

        @pl.when(i > j)
        def _():
            step(False)

        @pl.when(i == j)
        def _():
            step(True)

        @pl.when(i == nq - 1)
        def _():
            dk_ref[...] = dk_acc[...]
            dv_ref[...] = dv_acc[...]

    qblk = pl.BlockSpec((tq, MLA_SLOT), lambda h, n, it, jt: (it[n], h))
    oblk = pl.BlockSpec((tq, HEAD_DIM), lambda h, n, it, jt: (it[n], h))
    kblk = pl.BlockSpec((tq, MLA_SLOT), lambda h, n, it, jt: (jt[n], h))
    return pl.pallas_call(
        kern, name=name,
        grid_spec=pltpu.PrefetchScalarGridSpec(
            num_scalar_prefetch=2, grid=(HEADS, i_tab.shape[0]),
            in_specs=[qblk, kblk, pl.BlockSpec((tq, HEAD_DIM), lambda h, n, it, jt: (jt[n], 2 * h + 1)), oblk, oblk, oblk],
            out_specs=[pl.BlockSpec((S, MLA_SLOT), lambda h, n, it, jt: (0, h)), kblk,
                       pl.BlockSpec((tq, HEAD_DIM), lambda h, n, it, jt: (jt[n], h))],
            scratch_shapes=[pltpu.VMEM((tq, MLA_SLOT), F32), pltpu.VMEM((tq, HEAD_DIM), F32)]),
        out_shape=[_sds((S, HEADS * MLA_SLOT)), _sds((S, HEADS * MLA_SLOT)), _sds((S, HEADS * HEAD_DIM))],
        compiler_params=_params(("parallel", "arbitrary")))(i_tab, j_tab, qr, kr, kv, o, lse, do)


def loss_head(y, target, name):
    S, Dm = y.shape
    ts = _row_tile(S, Dm)

    def body(y_ref, t_ref):
        e = y_ref[...] - t_ref[...]
        tot = jnp.sum(jnp.sum(e * e, axis=-1, keepdims=True) / Dm, axis=0, keepdims=True)
        return e / Dm, jnp.broadcast_to(0.5 * tot, (SUBLANES, LANES))

    row = pl.BlockSpec((ts, Dm), lambda i: (i, 0))
    return tilecall(body, name, (S // ts,), [(y, row), (target, row)],
                    [(_sds((S, Dm)), row, None),
                     (_sds((SUBLANES, LANES)), pl.BlockSpec((SUBLANES, LANES), lambda i: (0, 0)), lambda: pl.program_id(0) == 0)],
                    ("arbitrary",))


_PEER_FLIPS = {
    "chips": ((1, 0, 0), (0, 1, 0), (1, 1, 0)),
    "sibling": ((0, 0, 1),),
    "all": tuple((a, b, c) for a in (0, 1) for b in (0, 1) for c in (0, 1))[1:],
}
_SLOT_WEIGHTS = {"chips": (2, 1, 0), "sibling": (0, 0, 1), "all": (4, 2, 1)}


def exchange(arrs, group, scatter, name):
    flips = _PEER_FLIPS[group]
    wx, wy, wc = _SLOT_WEIGHTS[group]
    n_slots = len(flips) + 1
    n = len(arrs)

    def kern(*refs):
        srcs, outs = refs[:n], refs[n:2 * n]
        send_sems, recv_sems, local_sems = refs[2 * n:]
        me = (lax.axis_index("x"), lax.axis_index("y"), lax.axis_index("c"))
        my_slot = wx * me[0] + wy * me[1] + wc * me[2]
        copies = []
        for a in range(n):
            own = srcs[a].at[my_slot] if scatter else srcs[a]
            cp = pltpu.make_async_copy(own, outs[a].at[my_slot], local_sems.at[a])
            cp.start()
            copies.append(cp)
        for f, flip in enumerate(flips):
            peer = tuple(m ^ b if b else m for m, b in zip(me, flip))
            peer_slot = wx * peer[0] + wy * peer[1] + wc * peer[2]
            for a in range(n):
                cp = pltpu.make_async_remote_copy(
                    src_ref=srcs[a].at[peer_slot] if scatter else srcs[a], dst_ref=outs[a].at[my_slot],
                    send_sem=send_sems.at[f, a], recv_sem=recv_sems.at[f, a],
                    device_id=peer, device_id_type=pl.DeviceIdType.MESH)
                cp.start()
                copies.append(cp)
        for cp in copies:
            cp.wait()

    hbm = pl.BlockSpec(memory_space=pltpu.HBM)
    out_shape = [_sds((n_slots,) + (a.shape[1:] if scatter else a.shape), a.dtype) for a in arrs]
    return pl.pallas_call(
        kern, name=name, in_specs=[hbm] * n, out_specs=[hbm] * n, out_shape=out_shape,
        scratch_shapes=[pltpu.SemaphoreType.DMA((len(flips), n)), pltpu.SemaphoreType.DMA((len(flips), n)),
                        pltpu.SemaphoreType.DMA((n,))])(*arrs)


PACK_ROWS = 2048


def _pack(arrs, dtype, row_multiple):
    flat = jnp.concatenate([a.reshape(-1).astype(dtype) for a in arrs])
    rows = -(-flat.shape[0] // LANES)
    rows = -(-rows // row_multiple) * row_multiple
    return jnp.pad(flat, (0, rows * LANES - flat.shape[0])).reshape(rows, LANES)


def _unpack(buf, shapes):
    flat = buf.reshape(-1)
    out, off = [], 0
    for s in shapes:
        n = int(np.prod(s))
        out.append(flat[off:off + n].reshape(s))
        off += n
    return out


def _pack_rows(rows):
    return min(rows, PACK_ROWS)


def sum_slots(buf, name):
    P, R, _ = buf.shape
    tr = _pack_rows(R)

    def body(b_ref):
        acc = b_ref[0].astype(F32)
        for p in range(1, P):
            acc = acc + b_ref[p].astype(F32)
        return (acc,)

    return tilecall(body, name, (R // tr,), [(buf, pl.BlockSpec((P, tr, LANES), lambda i: (0, i, 0)))],
                    [(_sds((R, LANES)), pl.BlockSpec((tr, LANES), lambda i: (i, 0)), None)], ("parallel",))[0]


def adamw(gparts, w, m, v, name):
    P, R, _ = gparts.shape
    tr = _pack_rows(R)

    def body(g_ref, w_ref, m_ref, v_ref):
        g = g_ref[0]
        for p in range(1, P):
            g = g + g_ref[p]
        m_new = ADAM_B1 * m_ref[...] + (1.0 - ADAM_B1) * g
        v_new = ADAM_B2 * v_ref[...] + (1.0 - ADAM_B2) * jnp.square(g)
        m_hat = m_new / (1.0 - ADAM_B1 ** ADAM_STEP)
        v_hat = v_new / (1.0 - ADAM_B2 ** ADAM_STEP)
        delta = -ADAM_LR * (m_hat / (jnp.sqrt(v_hat) + ADAM_EPS) + ADAM_WD * w_ref[...])
        return g, delta, m_new, v_new

    row = pl.BlockSpec((tr, LANES), lambda i: (i, 0))
    out = (_sds((R, LANES)), row, None)
    return tilecall(body, name, (R // tr,),
                    [(gparts, pl.BlockSpec((P, tr, LANES), lambda i: (0, i, 0))), (w, row), (m, row), (v, row)],
                    [out, out, out, out], ("parallel",))


def _ffn_fwd(x, gain, w_up, cw8, cb, w_down):
    h = rms_fwd(x, gain, "rms_fwd")
    u = mm(h, w_up, "nn", "mm_nn")
    a = ffn_act_fwd(u, cw8, cb, "ffn_act_fwd")
    y = mm(a, w_down, "nn", "mm_nn_add", add=x)
    return y, (x, h, u, a)


def _ffn_bwd(dy, saved, gain, w_up, cw8, cb, w_down):
    x, h, u, a = saved
    d_w_down = mm(a, dy, "tn", "mm_tn")
    da = mm(dy, w_down, "nt", "mm_nt")
    dpre, d_cw8, d_cb = ffn_act_bwd(u, cw8, cb, da, "ffn_act_bwd")
    du = ffn_conv_bwd(dpre, cw8, "ffn_conv_bwd")
    d_w_up = mm(h, du, "tn", "mm_tn")
    dh = mm(du, w_up, "nt", "mm_nt")
    dx, d_gain = rms_bwd(x, gain, dh, dy, "rms_bwd")
    return dx, dict(gain=d_gain, w_up=d_w_up, conv_w=d_cw8[0:3], conv_b=d_cb, w_down=d_w_down)


def _hgrn_layer_fwd(x, gain, w_in4, lb, out_gain, w_out, mall):
    h = rms_fwd(x, gain, "rms_fwd")
    z = [mm(h, w, "nn", "mm_nn") for w in w_in4]
    o, states = hgrn_fwd(z[0], z[1], z[2], lb, mall, "hgrn_fwd")
    on = hgrn_out_fwd(o, z[3], out_gain, "hgrn_out_fwd")
    y = mm(on, w_out, "nn", "mm_nn_add", add=x)
    return y, (x, h, z, o, states, on)


def _hgrn_layer_bwd(dy, saved, gain, w_in4, lb, out_gain, w_out, mall):
    x, h, z, o, states, on = saved
    d_w_out = mm(on, dy, "tn", "mm_tn")
    don = mm(dy, w_out, "nt", "mm_nt")
    do, dzg, d_out_gain = hgrn_out_bwd(o, z[3], out_gain, don, "hgrn_out_bwd")
    dzq, dzf, dzi, dlb = hgrn_bwd(z[0], z[1], z[2], lb, mall, states, do, "hgrn_bwd")
    dz = [dzq, dzf, dzi, dzg]
    d_w_in4 = [mm(h, d, "tn", "mm_tn") for d in dz]
    dh = mm(dz[0], w_in4[0], "nt", "mm_nt")
    for d, w in zip(dz[1:], w_in4[1:]):
        dh = mm(d, w, "nt", "mm_nt_add", add=dh)
    dx, d_gain = rms_bwd(x, gain, dh, dy, "rms_bwd")
    return dx, dict(gain=d_gain, w_in4=d_w_in4, lb=dlb, out_gain=d_out_gain, w_out=d_w_out)


def _mla_layer_fwd(x, gain, w_cq, w_ckv, w_kr, qa_gain, kva_gain, w_q_up, w_kv_up, qn, kn, w_out, cos_t, sin_t):
    h = rms_fwd(x, gain, "rms_fwd")
    cq = mm(h, w_cq, "nn", "mm_nn")
    ckv = mm(h, w_ckv, "nn", "mm_nn")
    kr = mm(h, w_kr, "nn", "mm_nn")
    cqn = rms_fwd(cq, qa_gain, "rms_fwd")
    ckvn = rms_fwd(ckv, kva_gain, "rms_fwd")
    qslots = mm(cqn, w_q_up, "nn", "mm_nn")
    kv = mm(ckvn, w_kv_up, "nn", "mm_nn")
    qr, krot = qk_fwd(qslots, kv, kr, qn, kn, cos_t, sin_t, "qk_fwd")
    o, lse = attn_fwd(qr, krot, kv, "attn_fwd")
    y = mm(o, w_out, "nn", "mm_nn_add", add=x)
    return y, (x, h, cq, ckv, kr, cqn, ckvn, qslots, kv, qr, krot, o, lse)


def _mla_layer_bwd(dy, saved, gain, w_cq, w_ckv, w_kr, qa_gain, kva_gain, w_q_up, w_kv_up, qn, kn, w_out, cos_t, sin_t):
    x, h, cq, ckv, kr, cqn, ckvn, qslots, kv, qr, krot, o, lse = saved
    d_w_out = mm(o, dy, "tn", "mm_tn")
    do = mm(dy, w_out, "nt", "mm_nt")
    dq, dk, dv = attn_bwd(qr, krot, kv, o, lse, do, "attn_bwd")
    dqslots, dkv, dkr, d_qn, d_kn = qk_bwd(qslots, kv, kr, qn, kn, cos_t, sin_t, dq, dk, dv, "qk_bwd")
    d_w_q_up = mm(cqn, dqslots, "tn", "mm_tn")
    dcqn = mm(dqslots, w_q_up, "nt", "mm_nt")
    d_w_kv_up = mm(ckvn, dkv, "tn", "mm_tn")
    dckvn = mm(dkv, w_kv_up, "nt", "mm_nt")
    dcq, d_qa = rms_bwd(cq, qa_gain, dcqn, None, "rms_bwd")
    dckv, d_kva = rms_bwd(ckv, kva_gain, dckvn, None, "rms_bwd")
    d_w_cq = mm(h, dcq, "tn", "mm_tn")
    d_w_ckv = mm(h, dckv, "tn", "mm_tn")
    d_w_kr = mm(h, dkr, "tn", "mm_tn")
    dh = mm(dcq, w_cq, "nt", "mm_nt")
    dh = mm(dckv, w_ckv, "nt", "mm_nt_add", add=dh)
    dh = mm(dkr, w_kr, "nt", "mm_nt_add", add=dh)
    dx, d_gain = rms_bwd(x, gain, dh, dy, "rms_bwd")
    return dx, dict(gain=d_gain, w_cq=d_w_cq, w_ckv=d_w_ckv, w_kr=d_w_kr, qa=d_qa, kva=d_kva, w_q_up=d_w_q_up,
                    w_kv_up=d_w_kv_up, qn=d_qn, kn=d_kn, w_out=d_w_out)


SHARDED = (("hgrn_w_in", 2), ("hgrn_w_out", 1), ("mla_w_in", 1), ("mla_q_a_norm", 1), ("mla_w_q_up", 2),
           ("mla_kv_a_norm", 1), ("mla_w_kv_up", 2), ("mla_w_out", 1), ("ffn_w_up", 2), ("ffn_conv_w", 2), ("ffn_w_down", 1))
GATHER_F32 = ("mla_q_a_norm", "mla_kv_a_norm", "ffn_conv_w")
REPLICATED = ("norm_mix", "norm_ffn", "hgrn_lower_bounds", "hgrn_out_norm", "mla_q_norm", "mla_k_norm", "ffn_conv_b")
WEIGHTS = ("norm_mix", "norm_ffn", "hgrn_w_in", "hgrn_lower_bounds", "hgrn_out_norm", "hgrn_w_out", "mla_w_in",
           "mla_q_a_norm", "mla_w_q_up", "mla_kv_a_norm", "mla_w_kv_up", "mla_q_norm", "mla_k_norm", "mla_w_out",
           "ffn_w_up", "ffn_conv_w", "ffn_conv_b", "ffn_w_down")
N_CHIPS = 4


def _gather_weights(w):
    big = [n for n, _ in SHARDED if n not in GATHER_F32]
    small = [n for n, _ in SHARDED if n in GATHER_F32]
    axis = dict(SHARDED)
    got_big, got_small = exchange([_pack([w[n] for n in big], _MXU, 16), _pack([w[n] for n in small], F32, SUBLANES)],
                                  "chips", False, "gather_weights")
    full = {}
    for names, got in ((big, got_big), (small, got_small)):
        per_chip = [_unpack(got[p], [w[n].shape for n in names]) for p in range(N_CHIPS)]
        for k, n in enumerate(names):
            full[n] = jnp.concatenate([per_chip[p][k] for p in range(N_CHIPS)], axis=axis[n])
    return full


def _pad_cols(a, width):
    return jnp.pad(a, [(0, 0)] * (a.ndim - 1) + [(0, width - a.shape[-1])])


def _head_slots(w):
    lead = w.shape[:-1]
    return _pad_cols(w.reshape(lead + (HEADS, MLA_QK)), MLA_SLOT).reshape(lead + (HEADS * MLA_SLOT,))


def _head_unslots(w):
    lead = w.shape[:-1]
    return w.reshape(lead + (HEADS, MLA_SLOT))[..., :MLA_QK].reshape(lead + (HEADS * MLA_QK,))


def _rope_tables(positions):
    inv_freq = ROPE_THETA ** (-jnp.arange(0, MLA_ROPE, 2, dtype=F32) / MLA_ROPE)
    ang = positions.astype(F32)[:, None] * inv_freq
    cos, sin = jnp.cos(ang), jnp.sin(ang)
    S = positions.shape[0]
    ones, zeros = jnp.ones((S, MLA_NOPE), F32), jnp.zeros((S, MLA_SLOT - MLA_QK), F32)
    return (jnp.concatenate([ones, cos, cos, zeros], axis=1),
            jnp.concatenate([jnp.zeros((S, MLA_NOPE), F32), -sin, sin, zeros], axis=1))


def kernel(x, positions, norm_mix, norm_ffn, hgrn_w_in, hgrn_lower_bounds, hgrn_out_norm, hgrn_w_out, mla_w_in, mla_q_a_norm, mla_w_q_up, mla_kv_a_norm, mla_w_kv_up, mla_q_norm, mla_k_norm, mla_w_out, ffn_w_up, ffn_conv_w, ffn_conv_b, ffn_w_down, loss_target, m_norm_mix, m_norm_ffn, m_hgrn_w_in, m_hgrn_lower_bounds, m_hgrn_out_norm, m_hgrn_w_out, m_mla_w_in, m_mla_q_a_norm, m_mla_w_q_up, m_mla_kv_a_norm, m_mla_w_kv_up, m_mla_q_norm, m_mla_k_norm, m_mla_w_out, m_ffn_w_up, m_ffn_conv_w, m_ffn_conv_b, m_ffn_w_down, v_norm_mix, v_norm_ffn, v_hgrn_w_in, v_hgrn_lower_bounds, v_hgrn_out_norm, v_hgrn_w_out, v_mla_w_in, v_mla_q_a_norm, v_mla_w_q_up, v_mla_kv_a_norm, v_mla_w_kv_up, v_mla_q_norm, v_mla_k_norm, v_mla_w_out, v_ffn_w_up, v_ffn_conv_w, v_ffn_conv_b, v_ffn_w_down):
    args = dict(locals())
    w = {n: args[n] for n in WEIGHTS}
    m = {n: args["m_" + n] for n in WEIGHTS}
    v = {n: args["v_" + n] for n in WEIGHTS}
    depth = norm_mix.shape[0]
    x0 = x[0]
    S = x0.shape[0]

    full = _gather_weights(w)
    cos_t, sin_t = _rope_tables(positions[0])
    lbs = lower_bound_fwd(hgrn_lower_bounds, "lower_bound_fwd")
    mall = jnp.asarray(_hgrn_sum_matrix(min(HGRN_CHUNK, S)), _MXU)
    qn = _pad_cols(mla_q_norm, MLA_SLOT)
    kn = _pad_cols(mla_k_norm, MLA_SLOT)
    cb = ffn_conv_b
    cw8 = jnp.pad(full["ffn_conv_w"], ((0, 0), (0, SUBLANES - 3), (0, 0)))

    def mixer_weights(layer):
        j = layer // 2
        if layer % 2 == 0:
            w_in = full["hgrn_w_in"][j]
            return (norm_mix[layer:layer + 1], [w_in[:, k * D_MODEL:(k + 1) * D_MODEL] for k in range(4)],
                    lbs[j:j + 1], hgrn_out_norm[j:j + 1], full["hgrn_w_out"][j], mall)
        w_in = full["mla_w_in"][j]
        return (norm_mix[layer:layer + 1], w_in[:, :MLA_LORA], w_in[:, MLA_LORA:2 * MLA_LORA],
                _pad_cols(w_in[:, 2 * MLA_LORA:], HEAD_DIM), full["mla_q_a_norm"][j:j + 1], full["mla_kv_a_norm"][j:j + 1],
                _head_slots(full["mla_w_q_up"][j]), full["mla_w_kv_up"][j], qn[j:j + 1], kn[j:j + 1], full["mla_w_out"][j],
                cos_t, sin_t)

    def ffn_weights(layer):
        return (norm_ffn[layer:layer + 1], full["ffn_w_up"][layer], cw8[layer], cb[layer:layer + 1], full["ffn_w_down"][layer])

    h = x0
    saved = []
    for layer in range(depth):
        fwd = _hgrn_layer_fwd if layer % 2 == 0 else _mla_layer_fwd
        h, s_mix = fwd(h, *mixer_weights(layer))
        h, s_ffn = _ffn_fwd(h, *ffn_weights(layer))
        saved.append((s_mix, s_ffn))

    dh, loss_blk = loss_head(h, loss_target[0], "loss_head")
    loss = lax.psum(loss_blk[0, 0], MESH_AXES)

    g_mix, g_ffn = [None] * depth, [None] * depth
    for layer in reversed(range(depth)):
        s_mix, s_ffn = saved[layer]
        dh, g_ffn[layer] = _ffn_bwd(dh, s_ffn, *ffn_weights(layer))
        bwd = _hgrn_layer_bwd if layer % 2 == 0 else _mla_layer_bwd
        dh, g_mix[layer] = bwd(dh, s_mix, *mixer_weights(layer))
    hg = [g_mix[l] for l in range(0, depth, 2)]
    mg = [g_mix[l] for l in range(1, depth, 2)]
    d_p = lower_bound_bwd(hgrn_lower_bounds, hg[1]["lb"], "lower_bound_bwd")

    grads = {
        "norm_mix": jnp.concatenate([g["gain"] for g in g_mix], axis=0),
        "norm_ffn": jnp.concatenate([g["gain"] for g in g_ffn], axis=0),
        "hgrn_w_in": jnp.stack([jnp.concatenate(g["w_in4"], axis=1) for g in hg]),
        "hgrn_lower_bounds": d_p[0:2],
        "hgrn_out_norm": jnp.concatenate([g["out_gain"] for g in hg], axis=0),
        "hgrn_w_out": jnp.stack([g["w_out"] for g in hg]),
        "mla_w_in": jnp.stack([jnp.concatenate([g["w_cq"], g["w_ckv"], g["w_kr"][:, :MLA_ROPE]], axis=1) for g in mg]),
        "mla_q_a_norm": jnp.concatenate([g["qa"] for g in mg], axis=0),
        "mla_w_q_up": jnp.stack([_head_unslots(g["w_q_up"]) for g in mg]),
        "mla_kv_a_norm": jnp.concatenate([g["kva"] for g in mg], axis=0),
        "mla_w_kv_up": jnp.stack([g["w_kv_up"] for g in mg]),
        "mla_q_norm": jnp.concatenate([g["qn"][:, :MLA_QK] for g in mg], axis=0),
        "mla_k_norm": jnp.concatenate([g["kn"][:, :MLA_QK] for g in mg], axis=0),
        "mla_w_out": jnp.stack([g["w_out"] for g in mg]),
        "ffn_w_up": jnp.stack([g["w_up"] for g in g_ffn]),
        "ffn_conv_w": jnp.stack([g["conv_w"] for g in g_ffn]),
        "ffn_conv_b": jnp.concatenate([g["conv_b"] for g in g_ffn], axis=0),
        "ffn_w_down": jnp.stack([g["w_down"] for g in g_ffn]),
    }

    names = [n for n, _ in SHARDED]
    shapes = [w[n].shape for n in names]

    def chip_slab(p):
        parts = []
        for (n, ax), shp in zip(SHARDED, shapes):
            parts.append(lax.slice_in_dim(grads[n], p * shp[ax], (p + 1) * shp[ax], axis=ax))
        return _pack(parts, _MXU, PACK_ROWS)

    send = jnp.stack([chip_slab(p) for p in range(N_CHIPS)])
    (from_chips,) = exchange([send], "chips", True, "reduce_chips")
    partial_sum = sum_slots(from_chips, "sum_slots")
    (both_cores,) = exchange([partial_sum], "sibling", False, "reduce_cores")
    packed = adamw(both_cores, *[_pack([t[n] for n in names], F32, PACK_ROWS) for t in (w, m, v)], "adamw")
    sharded_out = [dict(zip(names, _unpack(buf, shapes))) for buf in packed]

    rshapes = [w[n].shape for n in REPLICATED]
    (from_all,) = exchange([_pack([grads[n] for n in REPLICATED], F32, SUBLANES)], "all", False, "reduce_all")
    rpacked = adamw(from_all, *[_pack([t[n] for n in REPLICATED], F32, SUBLANES) for t in (w, m, v)], "adamw_small")
    repl_out = [dict(zip(REPLICATED, _unpack(buf, rshapes))) for buf in rpacked]

    result = [loss, dh[None]]
    for k in range(4):
        result += [sharded_out[k][n] if n in sharded_out[k] else repl_out[k][n] for n in WEIGHTS]
    return tuple(result)
```

```python
import functools

import numpy as np
import jax
import jax.numpy as jnp
from jax import lax
from jax.experimental import pallas as pl
from jax.experimental.pallas import tpu as pltpu

F32 = jnp.float32
BF16 = jnp.bfloat16
_MXU = BF16

RMS_EPS = 1e-6
D_MODEL = 1024
HEADS = 8
HEAD_DIM = 128
HGRN_CHUNK = 64
MLA_NOPE = 128
MLA_ROPE = 64
MLA_QK = MLA_NOPE + MLA_ROPE
MLA_SLOT = 256
MLA_LORA = 256
ROPE_THETA = 10000.0
D_FF = 2816
FF_BLOCK = 1408
LANES = 128
SUBLANES = 8

ADAM_LR = 0.001
ADAM_B1 = 0.9
ADAM_B2 = 0.999
ADAM_EPS = 1e-08
ADAM_WD = 0.01
ADAM_STEP = 10

VMEM_LIMIT = 56 * 1024 * 1024
MESH_AXES = ("x", "y", "c")

_NN = ((1,), (0,))
_NT = ((1,), (1,))
_TN = ((0,), (0,))


def _dg(a, b, dims):
    return lax.dot_general(a.astype(_MXU), b.astype(_MXU), (dims, ((), ())), preferred_element_type=F32)


@jax.custom_vjp
def kdot(a, b):
    return _dg(a, b, _NN)


kdot.defvjp(lambda a, b: (_dg(a, b, _NN), (a, b)), lambda r, g: (_dg(g, r[1], _NT), _dg(r[0], g, _TN)))


@jax.custom_vjp
def kdot_nt(a, b):
    return _dg(a, b, _NT)


kdot_nt.defvjp(lambda a, b: (_dg(a, b, _NT), (a, b)), lambda r, g: (_dg(g, r[1], _NN), _dg(g, r[0], _TN)))


@jax.custom_vjp
def kdot_tn(a, b):
    return _dg(a, b, _TN)


kdot_tn.defvjp(lambda a, b: (_dg(a, b, _TN), (a, b)), lambda r, g: (_dg(r[1], g, _NT), _dg(r[0], g, _NN)))


def _pick(d, prefs):
    for p in prefs:
        if d >= p and d % p == 0:
            return p
    return d


def _params(sem):
    return pltpu.CompilerParams(dimension_semantics=sem, vmem_limit_bytes=VMEM_LIMIT)


def mm(a, b, mode, name, add=None, out_dtype=F32):
    if mode == "nn":
        (M, K), (K2, N) = a.shape, b.shape
    elif mode == "nt":
        (M, K), (N, K2) = a.shape, b.shape
    else:
        (K, M), (K2, N) = a.shape, b.shape
    assert K == K2, (name, a.shape, b.shape)
    tm = M if M <= 1024 else _pick(M, (1024, 1408, 512, 256, 128))
    tn = N if N <= 1024 else _pick(N, (1024, 1408, 512, 256, 128))
    tk = K if K <= 512 else _pick(K, (512, 256, 128))
    nk = K // tk
    dims = {"nn": _NN, "nt": _NT, "tn": _TN}[mode]
    a_spec = {"nn": pl.BlockSpec((tm, tk), lambda i, j, k: (i, k)),
              "nt": pl.BlockSpec((tm, tk), lambda i, j, k: (i, k)),
              "tn": pl.BlockSpec((tk, tm), lambda i, j, k: (k, i))}[mode]
    b_spec = {"nn": pl.BlockSpec((tk, tn), lambda i, j, k: (k, j)),
              "nt": pl.BlockSpec((tn, tk), lambda i, j, k: (j, k)),
              "tn": pl.BlockSpec((tk, tn), lambda i, j, k: (k, j))}[mode]
    o_spec = pl.BlockSpec((tm, tn), lambda i, j, k: (i, j))
    has_add = add is not None

    def kern(*refs):
        a_ref, b_ref = refs[0], refs[1]
        add_ref = refs[2] if has_add else None
        o_ref, acc = refs[-2], refs[-1]
        k = pl.program_id(2)

        @pl.when(k == 0)
        def _():
            acc[...] = jnp.zeros_like(acc)

        acc[...] += _dg(a_ref[...], b_ref[...], dims)

        @pl.when(k == nk - 1)
        def _():
            r = acc[...]
            if has_add:
                r = r + add_ref[...].astype(F32)
            o_ref[...] = r.astype(o_ref.dtype)

    ins = [a, b] + ([add] if has_add else [])
    specs = [a_spec, b_spec] + ([o_spec] if has_add else [])
    return pl.pallas_call(
        kern, name=name, grid=(M // tm, N // tn, nk), in_specs=specs, out_specs=o_spec,
        out_shape=jax.ShapeDtypeStruct((M, N), out_dtype), scratch_shapes=[pltpu.VMEM((tm, tn), F32)],
        compiler_params=_params(("parallel", "parallel", "arbitrary")))(*ins)


def _store(ref, val, first):
    if first is None:
        ref[...] = val.astype(ref.dtype)
        return

    @pl.when(first)
    def _():
        ref[...] = val.astype(ref.dtype)

    @pl.when(jnp.logical_not(first))
    def _():
        ref[...] += val.astype(ref.dtype)


def tilecall(body, name, grid, ins, outs, sem):
    n_in = len(ins)

    def kern(*refs):
        vals = body(*refs[:n_in])
        for ref, val, (_, _, first) in zip(refs[n_in:], vals, outs):
            _store(ref, val, None if first is None else first())

    return pl.pallas_call(
        kern, name=name, grid=grid, in_specs=[s for _, s in ins], out_specs=[s for _, s, _ in outs],
        out_shape=[sh for sh, _, _ in outs], compiler_params=_params(sem))(*[a for a, _ in ins])


def _rms(x, g, n):
    ms = jnp.sum(x * x, axis=-1, keepdims=True) / n
    return x * lax.rsqrt(ms + RMS_EPS) * g


def _sds(shape, dtype=F32):
    return jax.ShapeDtypeStruct(shape, dtype)


def _row_tile(S, w):
    return min(S, 512 if w <= 1024 else 256)


def rms_fwd(x, g, name, col=0, w=None):
    S = x.shape[0]
    w = w or x.shape[1]
    ts = _row_tile(S, w)
    return tilecall(
        lambda x_ref, g_ref: (_rms(x_ref[...], g_ref[...], w),), name, (S // ts,),
        [(x, pl.BlockSpec((ts, w), lambda i: (i, col))), (g, pl.BlockSpec((1, w), lambda i: (0, 0)))],
        [(_sds((S, w), _MXU), pl.BlockSpec((ts, w), lambda i: (i, 0)), None)], ("parallel",))[0]


def rms_bwd(x, g, dh, res, name, w=None):
    S = x.shape[0]
    w = w or x.shape[1]
    ts = _row_tile(S, w)

    def body(x_ref, g_ref, dh_ref, *rest):
        _, vjp = jax.vjp(lambda xv, gv: _rms(xv, gv, w), x_ref[...], g_ref[...])
        dx, dg = vjp(dh_ref[...].astype(F32))
        if rest:
            dx = dx + rest[0][...]
        return dx, dg

    row = pl.BlockSpec((ts, w), lambda i: (i, 0))
    vec = pl.BlockSpec((1, w), lambda i: (0, 0))
    ins = [(x, row), (g, vec), (dh, row)] + ([(res, row)] if res is not None else [])
    return tilecall(body, name, (S // ts,), ins,
                    [(_sds((S, w)), row, None), (_sds((1, w)), vec, lambda: pl.program_id(0) == 0)], ("arbitrary",))


def _shifted(u, halo_ref, is_first):
    rid = lax.broadcasted_iota(jnp.int32, (u.shape[0], 1), 0)
    h7 = jnp.where(is_first, 0.0, halo_ref[7:8, :])
    h6 = jnp.where(is_first, 0.0, halo_ref[6:7, :])
    u1 = jnp.where(rid == 0, h7, pltpu.roll(u, 1, 0))
    u2 = jnp.where(rid == 0, h6, jnp.where(rid == 1, h7, pltpu.roll(u, 2, 0)))
    return u1, u2


def _conv(u, u1, u2, cw_ref, cb_ref):
    return ((cb_ref[...] + u2 * cw_ref[0:1, :]) + u1 * cw_ref[1:2, :]) + u * cw_ref[2:3, :]


def _ffn_specs(S, ts, jmap):
    hb = ts // SUBLANES
    return (pl.BlockSpec((ts, FF_BLOCK), lambda j, i: (i, jmap(j))),
            pl.BlockSpec((SUBLANES, FF_BLOCK), lambda j, i: (jnp.maximum(i * hb - 1, 0), jmap(j))),
            pl.BlockSpec((SUBLANES, FF_BLOCK), lambda j, i: (0, jmap(j))),
            pl.BlockSpec((1, FF_BLOCK), lambda j, i: (0, jmap(j))))


def ffn_act_fwd(u, cw8, cb, name):
    S = u.shape[0]
    ts = _row_tile(S, 2 * D_FF)
    nb = D_FF // FF_BLOCK

    def body(ug, hg, cwg, cbg, uu, hu, cwu, cbu):
        first = pl.program_id(1) == 0
        g = ug[...]
        g1, g2 = _shifted(g, hg, first)
        yg = _conv(g, g1, g2, cwg, cbg)
        v = uu[...]
        v1, v2 = _shifted(v, hu, first)
        yu = _conv(v, v1, v2, cwu, cbu)
        return (yg * jax.nn.sigmoid(yg) * yu,)

    sg = _ffn_specs(S, ts, lambda j: j)
    su = _ffn_specs(S, ts, lambda j: j + nb)
    ins = [(u, sg[0]), (u, sg[1]), (cw8, sg[2]), (cb, sg[3]), (u, su[0]), (u, su[1]), (cw8, su[2]), (cb, su[3])]
    return tilecall(body, name, (nb, S // ts), ins,
                    [(_sds((S, D_FF), _MXU), pl.BlockSpec((ts, FF_BLOCK), lambda j, i: (i, j)), None)],
                    ("parallel", "parallel"))[0]


def ffn_act_bwd(u, cw8, cb, da, name):
    S = u.shape[0]
    ts = _row_tile(S, 2 * D_FF)
    nb = D_FF // FF_BLOCK

    def body(um, hm, cwm, cbm, up, hp, cwp, cbp, da_ref):
        j = pl.program_id(0)
        first = pl.program_id(1) == 0
        a = um[...]
        a1, a2 = _shifted(a, hm, first)
        ym = _conv(a, a1, a2, cwm, cbm)
        b = up[...]
        b1, b2 = _shifted(b, hp, first)
        yp = _conv(b, b1, b2, cwp, cbp)
        d = da_ref[...]
        sm = jax.nn.sigmoid(ym)
        d_gate = d * yp * (sm * (1.0 + ym * (1.0 - sm)))
        d_up = d * (yp * jax.nn.sigmoid(yp))
        dy = jnp.where(j < nb, d_gate, d_up)
        dcw = jnp.concatenate(
            [jnp.sum(dy * a2, axis=0, keepdims=True), jnp.sum(dy * a1, axis=0, keepdims=True),
             jnp.sum(dy * a, axis=0, keepdims=True), jnp.zeros((SUBLANES - 3, dy.shape[1]), F32)], axis=0)
        return dy, dcw, jnp.sum(dy, axis=0, keepdims=True)

    sm_ = _ffn_specs(S, ts, lambda j: j)
    sp_ = _ffn_specs(S, ts, lambda j: (j + nb) % (2 * nb))
    ins = [(u, sm_[0]), (u, sm_[1]), (cw8, sm_[2]), (cb, sm_[3]), (u, sp_[0]), (u, sp_[1]), (cw8, sp_[2]), (cb, sp_[3]),
           (da, pl.BlockSpec((ts, FF_BLOCK), lambda j, i: (i, j % nb)))]
    first_row = lambda: pl.program_id(1) == 0
    return tilecall(body, name, (2 * nb, S // ts), ins,
                    [(_sds((S, 2 * D_FF)), sm_[0], None), (_sds((SUBLANES, 2 * D_FF)), sm_[2], first_row),
                     (_sds((1, 2 * D_FF)), sm_[3], first_row)], ("parallel", "arbitrary"))


def ffn_conv_bwd(dy, cw8, name):
    S = dy.shape[0]
    ts = _row_tile(S, 2 * D_FF)
    hb = ts // SUBLANES
    nrow = S // ts
    ncol = 2 * D_FF // FF_BLOCK

    def body(dy_ref, halo_ref, cw_ref):
        last = pl.program_id(1) == nrow - 1
        d = dy_ref[...]
        rid = lax.broadcasted_iota(jnp.int32, (ts, 1), 0)
        n0 = jnp.where(last, 0.0, halo_ref[0:1, :])
        n1 = jnp.where(last, 0.0, halo_ref[1:2, :])
        d1 = jnp.where(rid == ts - 1, n0, pltpu.roll(d, ts - 1, 0))
        d2 = jnp.where(rid == ts - 1, n1, jnp.where(rid == ts - 2, n0, pltpu.roll(d, ts - 2, 0)))
        return (d * cw_ref[2:3, :] + d1 * cw_ref[1:2, :] + d2 * cw_ref[0:1, :],)

    row = pl.BlockSpec((ts, FF_BLOCK), lambda j, i: (i, j))
    ins = [(dy, row), (dy, pl.BlockSpec((SUBLANES, FF_BLOCK), lambda j, i: (jnp.minimum((i + 1) * hb, S // SUBLANES - 1), j))),
           (cw8, pl.BlockSpec((SUBLANES, FF_BLOCK), lambda j, i: (0, j)))]
    return tilecall(body, name, (ncol, nrow), ins, [(_sds((S, 2 * D_FF), _MXU), row, None)], ("parallel", "parallel"))[0]


def _hgrn_levels(C):
    out, m = [], C // 2
    while m >= 1:
        out.append(m)
        m //= 2
    return out


def _hgrn_sum_matrix(C):
    t = np.arange(C)[:, None]
    u = np.arange(C)[None, :]
    blocks = [u <= t, u > t]
    for m in _hgrn_levels(C):
        r = (t // (2 * m)) * (2 * m) + m
        right = (t % (2 * m)) >= m
        blocks.append(right & (u > r) & (u <= t))
        blocks.append((~right) & (u > t) & (u <= r))
    return np.concatenate(blocks, axis=0).astype(np.float32)


def _make_partial_sums(nb, C):
    @jax.custom_vjp
    def sums(mall, lf):
        hi = lf.astype(_MXU)
        r1 = lf - hi.astype(F32)
        mid = r1.astype(_MXU)
        lo = (r1 - mid.astype(F32)).astype(_MXU)
        e = _dg(mall, hi, _NN) + _dg(mall, mid, _NN) + _dg(mall, lo, _NN)
        return tuple(e[b * C:(b + 1) * C] for b in range(nb))

    def fwd(mall, lf):
        return sums(mall, lf), mall

    def bwd(mall, gs):
        return jnp.zeros_like(mall), _dg(mall, jnp.concatenate(gs, axis=0), _TN)

    sums.defvjp(fwd, bwd)
    return sums


def _hgrn_chunk(zq, zf, v, lb, st, mall, C):
    levels = _hgrn_levels(C)
    qs = zq * jax.nn.sigmoid(zq)
    fg = lb + (1.0 - lb) * jax.nn.sigmoid(zf)
    k = 1.0 - fg
    e = _make_partial_sums(2 + 2 * len(levels), C)(mall, jnp.log(fg))
    g_incl, g_after = e[0], e[1]
    rid = lax.broadcasted_iota(jnp.int32, (C, 1), 0)
    tt = lax.broadcasted_iota(jnp.int32, (C, C), 0)
    ss = lax.broadcasted_iota(jnp.int32, (C, C), 1)
    o = kdot_nt(qs * jnp.exp(g_incl), st)
    o = o + jnp.sum(qs * k, axis=-1, keepdims=True) * v
    scores = jnp.zeros((C, C), F32)
    for li, m in enumerate(levels):
        sh = int(np.log2(m))
        right = ((rid >> sh) & 1) == 1
        qt = jnp.where(right, qs * jnp.exp(e[2 + 2 * li]), 0.0)
        kt = jnp.where(right, 0.0, k * jnp.exp(e[3 + 2 * li]))
        pair = ((tt >> (sh + 1)) == (ss >> (sh + 1))) & (((tt >> sh) & 1) == 1) & (((ss >> sh) & 1) == 0)
        scores = scores + jnp.where(pair, kdot_nt(qt, kt), 0.0)
    o = o + kdot(scores, v)
    g_last = jnp.sum(jnp.where(rid == C - 1, g_incl, 0.0), axis=0, keepdims=True)
    st_new = st * jnp.exp(g_last) + kdot_tn(v, k * jnp.exp(g_after))
    return o, st_new


def _hgrn_in_specs(C, nc, rev):
    cm = (lambda c: nc - 1 - c) if rev else (lambda c: c)
    blk = lambda: pl.BlockSpec((C, HEAD_DIM), lambda h, c: (cm(c), h))
    return cm, [blk(), blk(), blk(), pl.BlockSpec((1, HEAD_DIM), lambda h, c: (0, h))]


def hgrn_fwd(zq, zf, zi, lb, mall, name):
    S = zq.shape[0]
    C = min(HGRN_CHUNK, S)
    nc = S // C

    def kern(zq_ref, zf_ref, zi_ref, lb_ref, mall_ref, o_ref, st_ref, st):
        @pl.when(pl.program_id(1) == 0)
        def _():
            st[...] = jnp.zeros_like(st)

        s_in = st[...]
        st_ref[...] = s_in
        o, s_new = _hgrn_chunk(zq_ref[...], zf_ref[...], zi_ref[...], lb_ref[...], s_in, mall_ref[...], C)
        o_ref[...] = o
        st[...] = s_new

    _, specs = _hgrn_in_specs(C, nc, False)
    return pl.pallas_call(
        kern, name=name, grid=(HEADS, nc),
        in_specs=specs + [pl.BlockSpec(mall.shape, lambda h, c: (0, 0))],
        out_specs=[pl.BlockSpec((C, HEAD_DIM), lambda h, c: (c, h)),
                   pl.BlockSpec((None, None, HEAD_DIM, HEAD_DIM), lambda h, c: (h, c, 0, 0))],
        out_shape=[_sds((S, D_MODEL)), _sds((HEADS, nc, HEAD_DIM, HEAD_DIM))],
        scratch_shapes=[pltpu.VMEM((HEAD_DIM, HEAD_DIM), F32)],
        compiler_params=_params(("parallel", "arbitrary")))(zq, zf, zi, lb, mall)


def hgrn_bwd(zq, zf, zi, lb, mall, states, do, name):
    S = zq.shape[0]
    C = min(HGRN_CHUNK, S)
    nc = S // C

    def kern(zq_ref, zf_ref, zi_ref, lb_ref, mall_ref, st_ref, do_ref, dq_ref, df_ref, di_ref, dlb_ref, dst):
        first = pl.program_id(1) == 0

        @pl.when(first)
        def _():
            dst[...] = jnp.zeros_like(dst)

        mall_v = mall_ref[...]
        _, vjp = jax.vjp(lambda a, b, c, d, e: _hgrn_chunk(a, b, c, d, e, mall_v, C),
                         zq_ref[...], zf_ref[...], zi_ref[...], lb_ref[...], st_ref[...])
        ga, gb, gv, gl, gs = vjp((do_ref[...], dst[...]))
        dq_ref[...] = ga.astype(dq_ref.dtype)
        df_ref[...] = gb.astype(df_ref.dtype)
        di_ref[...] = gv.astype(di_ref.dtype)
        _store(dlb_ref, gl, first)
        dst[...] = gs

    cm, specs = _hgrn_in_specs(C, nc, True)
    row = lambda: pl.BlockSpec((C, HEAD_DIM), lambda h, c: (cm(c), h))
    return pl.pallas_call(
        kern, name=name, grid=(HEADS, nc),
        in_specs=specs + [pl.BlockSpec(mall.shape, lambda h, c: (0, 0)),
                          pl.BlockSpec((None, None, HEAD_DIM, HEAD_DIM), lambda h, c: (h, cm(c), 0, 0)), row()],
        out_specs=[row(), row(), row(), pl.BlockSpec((1, HEAD_DIM), lambda h, c: (0, h))],
        out_shape=[_sds((S, D_MODEL), _MXU), _sds((S, D_MODEL), _MXU), _sds((S, D_MODEL), _MXU), _sds((1, D_MODEL))],
        scratch_shapes=[pltpu.VMEM((HEAD_DIM, HEAD_DIM), F32)],
        compiler_params=_params(("parallel", "arbitrary")))(zq, zf, zi, lb, mall, states, do)


def _hgrn_out(o, g, gain):
    return _rms(o, gain, HEAD_DIM) * (g * jax.nn.sigmoid(g))


def hgrn_out_fwd(o, zg, gain, name):
    S = o.shape[0]
    ts = _row_tile(S, D_MODEL)
    blk = pl.BlockSpec((ts, HEAD_DIM), lambda i, h: (i, h))
    return tilecall(lambda o_ref, g_ref, w_ref: (_hgrn_out(o_ref[...], g_ref[...], w_ref[...]),), name, (S // ts, HEADS),
                    [(o, blk), (zg, blk), (gain, pl.BlockSpec((1, HEAD_DIM), lambda i, h: (0, 0)))],
                    [(_sds((S, D_MODEL), _MXU), blk, None)], ("parallel", "parallel"))[0]


def hgrn_out_bwd(o, zg, gain, don, name):
    S = o.shape[0]
    ts = _row_tile(S, D_MODEL)
    blk = pl.BlockSpec((ts, HEAD_DIM), lambda i, h: (i, h))
    vec = pl.BlockSpec((1, HEAD_DIM), lambda i, h: (0, 0))

    def body(o_ref, g_ref, w_ref, d_ref):
        _, vjp = jax.vjp(_hgrn_out, o_ref[...], g_ref[...], w_ref[...])
        return vjp(d_ref[...])

    return tilecall(body, name, (S // ts, HEADS), [(o, blk), (zg, blk), (gain, vec), (don, blk)],
                    [(_sds((S, D_MODEL)), blk, None), (_sds((S, D_MODEL), _MXU), blk, None),
                     (_sds((1, HEAD_DIM)), vec, lambda: (pl.program_id(0) == 0) & (pl.program_id(1) == 0))],
                    ("arbitrary", "arbitrary"))


def lower_bound_fwd(p, name):
    assert p.shape[0] == 2

    def body(p_ref):
        s = _lb_soft(p_ref[0:1, :], p_ref[1:2, :])
        return (jnp.concatenate([jnp.zeros_like(s), s] + [jnp.zeros_like(s)] * (SUBLANES - 2), axis=0),)

    spec8 = pl.BlockSpec((SUBLANES, p.shape[1]), lambda: (0, 0))
    return tilecall(body, name, (), [(p, pl.BlockSpec(p.shape, lambda: (0, 0)))], [(_sds((SUBLANES, p.shape[1])), spec8, None)], ())[0]


def _lb_soft(p0, p1):
    mx = jnp.maximum(p0, p1)
    e0, e1 = jnp.exp(p0 - mx), jnp.exp(p1 - mx)
    s0, s1 = e0 / (e0 + e1), e1 / (e0 + e1)
    return (s0 + s1) - s0


def lower_bound_bwd(p, dlb1, name):
    def body(p_ref, d_ref):
        _, vjp = jax.vjp(_lb_soft, p_ref[0:1, :], p_ref[1:2, :])
        g0, g1 = vjp(d_ref[...])
        return (jnp.concatenate([g0, g1] + [jnp.zeros_like(g0)] * (SUBLANES - 2), axis=0),)

    spec8 = pl.BlockSpec((SUBLANES, p.shape[1]), lambda: (0, 0))
    return tilecall(body, name, (), [(p, pl.BlockSpec(p.shape, lambda: (0, 0))), (dlb1, pl.BlockSpec(dlb1.shape, lambda: (0, 0)))],
                    [(_sds((SUBLANES, p.shape[1])), spec8, None)], ())[0]


@jax.custom_vjp
def _swap_rope_halves(x):
    lane = lax.broadcasted_iota(jnp.int32, x.shape, 1)
    lo = (lane >= MLA_NOPE) & (lane < MLA_NOPE + MLA_ROPE // 2)
    hi = (lane >= MLA_NOPE + MLA_ROPE // 2) & (lane < MLA_QK)
    return jnp.where(lo, pltpu.roll(x, MLA_SLOT - MLA_ROPE // 2, 1), jnp.where(hi, pltpu.roll(x, MLA_ROPE // 2, 1), 0.0))


_swap_rope_halves.defvjp(lambda x: (_swap_rope_halves(x), None), lambda _, g: (_swap_rope_halves(g),))


def _norm_rope(x, gain, cos_t, sin_t):
    y = _rms(x, gain, MLA_QK)
    return y * cos_t + _swap_rope_halves(y) * sin_t


def _qk_heads(qs, kn, kr, qn, kn_gain, cos_t, sin_t):
    q = _norm_rope(qs, qn, cos_t, sin_t)
    k = _norm_rope(jnp.concatenate([kn, kr], axis=1), kn_gain, cos_t, sin_t)
    return q, k


def _qk_specs(ts):
    slot = pl.BlockSpec((ts, MLA_SLOT), lambda i, h: (i, h))
    nope = pl.BlockSpec((ts, HEAD_DIM), lambda i, h: (i, 2 * h))
    shared = pl.BlockSpec((ts, HEAD_DIM), lambda i, h: (i, 0))
    gain = pl.BlockSpec((1, MLA_SLOT), lambda i, h: (0, 0))
    table = pl.BlockSpec((ts, MLA_SLOT), lambda i, h: (i, 0))
    return slot, nope, shared, gain, table


def qk_fwd(qslots, kv, krope, qn, kn, cos_t, sin_t, name):
    S = qslots.shape[0]
    ts = _row_tile(S, D_MODEL)
    slot, nope, shared, gain, table = _qk_specs(ts)

    def body(q_ref, kn_ref, kr_ref, qn_ref, kg_ref, c_ref, s_ref):
        return _qk_heads(q_ref[...], kn_ref[...], kr_ref[...], qn_ref[...], kg_ref[...], c_ref[...], s_ref[...])

    out = _sds((S, HEADS * MLA_SLOT), _MXU)
    return tilecall(body, name, (S // ts, HEADS),
                    [(qslots, slot), (kv, nope), (krope, shared), (qn, gain), (kn, gain), (cos_t, table), (sin_t, table)],
                    [(out, slot, None), (out, slot, None)], ("parallel", "parallel"))


def qk_bwd(qslots, kv, krope, qn, kn, cos_t, sin_t, dq, dk, dv, name):
    S = qslots.shape[0]
    ts = _row_tile(S, D_MODEL)
    slot, nope, shared, gain, table = _qk_specs(ts)
    vblk = pl.BlockSpec((ts, HEAD_DIM), lambda i, h: (i, h))

    def body(q_ref, kn_ref, kr_ref, qn_ref, kg_ref, c_ref, s_ref, dq_ref, dk_ref, dv_ref):
        c, s = c_ref[...], s_ref[...]
        _, vjp = jax.vjp(lambda a, b, r, g1, g2: _qk_heads(a, b, r, g1, g2, c, s),
                         q_ref[...], kn_ref[...], kr_ref[...], qn_ref[...], kg_ref[...])
        ga, gb, gr, g1, g2 = vjp((dq_ref[...], dk_ref[...]))
        return ga, jnp.concatenate([gb, dv_ref[...]], axis=1), gr, g1, g2

    first_head = lambda: pl.program_id(1) == 0
    first = lambda: (pl.program_id(0) == 0) & (pl.program_id(1) == 0)
    wide = _sds((S, HEADS * MLA_SLOT), _MXU)
    return tilecall(body, name, (S // ts, HEADS),
                    [(qslots, slot), (kv, nope), (krope, shared), (qn, gain), (kn, gain), (cos_t, table), (sin_t, table),
                     (dq, slot), (dk, slot), (dv, vblk)],
                    [(wide, slot, None), (wide, slot, None), (_sds((S, HEAD_DIM)), shared, first_head),
                     (_sds((1, MLA_SLOT)), gain, first), (_sds((1, MLA_SLOT)), gain, first)], ("arbitrary", "arbitrary"))


_ATTN_SCALE = MLA_QK ** -0.5


ATTN_TILE = 512


def _causal_pairs(nq, by_row):
    pairs = [(i, j) for i in range(nq) for j in range(i + 1)] if by_row else [(i, j) for j in range(nq) for i in range(j, nq)]
    return jnp.asarray([p[0] for p in pairs], jnp.int32), jnp.asarray([p[1] for p in pairs], jnp.int32)


def _diag_mask(s, tq):
    rows = lax.broadcasted_iota(jnp.int32, (tq, tq), 0)
    cols = lax.broadcasted_iota(jnp.int32, (tq, tq), 1)
    return jnp.where(rows >= cols, s, -jnp.inf)


def attn_fwd(qr, kr, kv, name):
    S = qr.shape[0]
    tq = min(S, ATTN_TILE)
    nq = S // tq
    i_tab, j_tab = _causal_pairs(nq, True)

    def kern(it, jt, q_ref, k_ref, v_ref, o_ref, lse_ref, m_s, l_s, acc):
        n = pl.program_id(1)
        i, j = it[n], jt[n]

        @pl.when(j == 0)
        def _():
            m_s[...] = jnp.full_like(m_s, -jnp.inf)
            l_s[...] = jnp.zeros_like(l_s)
            acc[...] = jnp.zeros_like(acc)

        def step(diagonal):
            s = _dg(q_ref[...], k_ref[...], _NT) * _ATTN_SCALE
            if diagonal:
                s = _diag_mask(s, tq)
            m_prev = m_s[...]
            m_new = jnp.maximum(m_prev, jnp.max(s, axis=-1, keepdims=True))
            alpha = jnp.exp(m_prev - m_new)
            p = jnp.exp(s - m_new)
            l_s[...] = alpha * l_s[...] + jnp.sum(p, axis=-1, keepdims=True)
            acc[...] = alpha * acc[...] + _dg(p, v_ref[...], _NN)
            m_s[...] = m_new

        @pl.when(j < i)
        def _():
            step(False)

        @pl.when(j == i)
        def _():
            step(True)
            l = l_s[...]
            o_ref[...] = acc[...] / l
            lse_ref[...] = jnp.broadcast_to(m_s[...] + jnp.log(l), lse_ref.shape)

    out = _sds((S, HEADS * HEAD_DIM))
    oblk = pl.BlockSpec((tq, HEAD_DIM), lambda h, n, it, jt: (it[n], h))
    return pl.pallas_call(
        kern, name=name,
        grid_spec=pltpu.PrefetchScalarGridSpec(
            num_scalar_prefetch=2, grid=(HEADS, i_tab.shape[0]),
            in_specs=[pl.BlockSpec((tq, MLA_SLOT), lambda h, n, it, jt: (it[n], h)),
                      pl.BlockSpec((tq, MLA_SLOT), lambda h, n, it, jt: (jt[n], h)),
                      pl.BlockSpec((tq, HEAD_DIM), lambda h, n, it, jt: (jt[n], 2 * h + 1))],
            out_specs=[oblk, oblk],
            scratch_shapes=[pltpu.VMEM((tq, 1), F32), pltpu.VMEM((tq, 1), F32), pltpu.VMEM((tq, HEAD_DIM), F32)]),
        out_shape=[out, out], compiler_params=_params(("parallel", "arbitrary")))(i_tab, j_tab, qr, kr, kv)


def attn_bwd(qr, kr, kv, o, lse, do, name):
    S = qr.shape[0]
    tq = min(S, ATTN_TILE)
    nq = S // tq
    i_tab, j_tab = _causal_pairs(nq, False)

    def kern(it, jt, q_ref, k_ref, v_ref, o_ref, lse_ref, do_ref, dq_ref, dk_ref, dv_ref, dk_acc, dv_acc):
        n = pl.program_id(1)
        i, j = it[n], jt[n]

        @pl.when(n == 0)
        def _():
            dq_ref[...] = jnp.zeros_like(dq_ref)

        @pl.when(i == j)
        def _():
            dk_acc[...] = jnp.zeros_like(dk_acc)
            dv_acc[...] = jnp.zeros_like(dv_acc)

        def step(diagonal):
            s = _dg(q_ref[...], k_ref[...], _NT) * _ATTN_SCALE - lse_ref[:, 0:1]
            if diagonal:
                s = _diag_mask(s, tq)
            p = jnp.exp(s)
            g = do_ref[...]
            delta = jnp.sum(g * o_ref[...], axis=-1, keepdims=True)
            dv_acc[...] += _dg(p, g, _TN)
            ds = p * (_dg(g, v_ref[...], _NT) - delta) * _ATTN_SCALE
            dk_acc[...] += _dg(ds, q_ref[...], _TN)
            rows = pl.ds(pl.multiple_of(i * tq, tq), tq)
            dq_ref[rows, :] += _dg(ds, k_ref[...], _NN)

        @pl.when(i > j)
        def _():
            step(False)

        @pl.when(i == j)
        def _():
            step(True)

        @pl.when(i == nq - 1)
        def _():
            dk_ref[...] = dk_acc[...]
            dv_ref[...] = dv_acc[...]

    qblk = pl.BlockSpec((tq, MLA_SLOT), lambda h, n, it, jt: (it[n], h))
    oblk = pl.BlockSpec((tq, HEAD_DIM), lambda h, n, it, jt: (it[n], h))
    kblk = pl.BlockSpec((tq, MLA_SLOT), lambda h, n, it, jt: (jt[n], h))
    return pl.pallas_call(
        kern, name=name,
        grid_spec=pltpu.PrefetchScalarGridSpec(
            num_scalar_prefetch=2, grid=(HEADS, i_tab.shape[0]),
            in_specs=[qblk, kblk, pl.BlockSpec((tq, HEAD_DIM), lambda h, n, it, jt: (jt[n], 2 * h + 1)), oblk, oblk, oblk],
            out_specs=[pl.BlockSpec((S, MLA_SLOT), lambda h, n, it, jt: (0, h)), kblk,
                       pl.BlockSpec((tq, HEAD_DIM), lambda h, n, it, jt: (jt[n], h))],
            scratch_shapes=[pltpu.VMEM((tq, MLA_SLOT), F32), pltpu.VMEM((tq, HEAD_DIM), F32)]),
        out_shape=[_sds((S, HEADS * MLA_SLOT)), _sds((S, HEADS * MLA_SLOT)), _sds((S, HEADS * HEAD_DIM))],
        compiler_params=_params(("parallel", "arbitrary")))(i_tab, j_tab, qr, kr, kv, o, lse, do)


def loss_head(y, target, name):
    S, Dm = y.shape
    ts = _row_tile(S, Dm)

    def body(y_ref, t_ref):
        e = y_ref[...] - t_ref[...]
        tot = jnp.sum(jnp.sum(e * e, axis=-1, keepdims=True) / Dm, axis=0, keepdims=True)
        return e / Dm, jnp.broadcast_to(0.5 * tot, (SUBLANES, LANES))

    row = pl.BlockSpec((ts, Dm), lambda i: (i, 0))
    return tilecall(body, name, (S // ts,), [(y, row), (target, row)],
                    [(_sds((S, Dm)), row, None),
                     (_sds((SUBLANES, LANES)), pl.BlockSpec((SUBLANES, LANES), lambda i: (0, 0)), lambda: pl.program_id(0) == 0)],
                    ("arbitrary",))


_PEER_FLIPS = {
    "chips": ((1, 0, 0), (0, 1, 0), (1, 1, 0)),
    "sibling": ((0, 0, 1),),
    "all": tuple((a, b, c) for a in (0, 1) for b in (0, 1) for c in (0, 1))[1:],
}
_SLOT_WEIGHTS = {"chips": (2, 1, 0), "sibling": (0, 0, 1), "all": (4, 2, 1)}


def exchange(arrs, group, scatter, name, keep_own=True):
    flips = _PEER_FLIPS[group]
    wx, wy, wc = _SLOT_WEIGHTS[group]
    n_slots = len(flips) + (1 if keep_own else 0)
    n = len(arrs)

    def kern(*refs):
        srcs, outs = refs[:n], refs[n:2 * n]
        send_sems, recv_sems = refs[2 * n:2 * n + 2]
        me = (lax.axis_index("x"), lax.axis_index("y"), lax.axis_index("c"))
        my_slot = wx * me[0] + wy * me[1] + wc * me[2]
        copies = []
        if keep_own:
            local_sems = refs[2 * n + 2]
            for a in range(n):
                own = srcs[a].at[my_slot] if scatter else srcs[a]
                cp = pltpu.make_async_copy(own, outs[a].at[my_slot], local_sems.at[a])
                cp.start()
                copies.append(cp)
        for f, flip in enumerate(flips):
            peer = tuple(m ^ b if b else m for m, b in zip(me, flip))
            peer_slot = wx * peer[0] + wy * peer[1] + wc * peer[2]
            for a in range(n):
                cp = pltpu.make_async_remote_copy(
                    src_ref=srcs[a].at[peer_slot] if scatter else srcs[a],
                    dst_ref=outs[a].at[my_slot if keep_own else f],
                    send_sem=send_sems.at[f, a], recv_sem=recv_sems.at[f, a],
                    device_id=peer, device_id_type=pl.DeviceIdType.MESH)
                cp.start()
                copies.append(cp)
        for cp in copies:
            cp.wait()

    hbm = pl.BlockSpec(memory_space=pltpu.HBM)
    out_shape = [_sds((n_slots,) + (a.shape[1:] if scatter else a.shape), a.dtype) for a in arrs]
    sems = [pltpu.SemaphoreType.DMA((len(flips), n)), pltpu.SemaphoreType.DMA((len(flips), n))]
    return pl.pallas_call(
        kern, name=name, in_specs=[hbm] * n, out_specs=[hbm] * n, out_shape=out_shape,
        scratch_shapes=sems + ([pltpu.SemaphoreType.DMA((n,))] if keep_own else []))(*arrs)


PACK_ROWS = 2048


def _pack(arrs, dtype, row_multiple):
    flat = jnp.concatenate([a.reshape(-1).astype(dtype) for a in arrs])
    rows = -(-flat.shape[0] // LANES)
    rows = -(-rows // row_multiple) * row_multiple
    return jnp.pad(flat, (0, rows * LANES - flat.shape[0])).reshape(rows, LANES)


def _unpack(buf, shapes):
    flat = buf.reshape(-1)
    out, off = [], 0
    for s in shapes:
        n = int(np.prod(s))
        out.append(flat[off:off + n].reshape(s))
        off += n
    return out


def _pack_rows(rows):
    return min(rows, PACK_ROWS)


def sum_slots(buf, name):
    P, R, _ = buf.shape
    tr = _pack_rows(R)

    def body(b_ref):
        acc = b_ref[0].astype(F32)
        for p in range(1, P):
            acc = acc + b_ref[p].astype(F32)
        return (acc,)

    return tilecall(body, name, (R // tr,), [(buf, pl.BlockSpec((P, tr, LANES), lambda i: (0, i, 0)))],
                    [(_sds((R, LANES)), pl.BlockSpec((tr, LANES), lambda i: (i, 0)), None)], ("parallel",))[0]


def adamw(gparts, w, m, v, name, g_extra=None):
    P, R, _ = gparts.shape
    tr = _pack_rows(R)

    def body(g_ref, w_ref, m_ref, v_ref, *more):
        g = g_ref[0]
        for p in range(1, P):
            g = g + g_ref[p]
        if more:
            g = g + more[0][...]
        m_new = ADAM_B1 * m_ref[...] + (1.0 - ADAM_B1) * g
        v_new = ADAM_B2 * v_ref[...] + (1.0 - ADAM_B2) * jnp.square(g)
        m_hat = m_new / (1.0 - ADAM_B1 ** ADAM_STEP)
        v_hat = v_new / (1.0 - ADAM_B2 ** ADAM_STEP)
        delta = -ADAM_LR * (m_hat / (jnp.sqrt(v_hat) + ADAM_EPS) + ADAM_WD * w_ref[...])
        return g, delta, m_new, v_new

    row = pl.BlockSpec((tr, LANES), lambda i: (i, 0))
    out = (_sds((R, LANES)), row, None)
    extra = [(g_extra, row)] if g_extra is not None else []
    return tilecall(body, name, (R // tr,),
                    [(gparts, pl.BlockSpec((P, tr, LANES), lambda i: (0, i, 0))), (w, row), (m, row), (v, row)] + extra,
                    [out, out, out, out], ("parallel",))


def _ffn_fwd(x, gain, w_up, cw8, cb, w_down):
    h = rms_fwd(x, gain, "rms_fwd")
    u = mm(h, w_up, "nn", "mm_nn")
    a = ffn_act_fwd(u, cw8, cb, "ffn_act_fwd")
    y = mm(a, w_down, "nn", "mm_nn_add", add=x)
    return y, (x, h, u, a)


def _ffn_bwd(dy, saved, gain, w_up, cw8, cb, w_down):
    x, h, u, a = saved
    d_w_down = mm(a, dy, "tn", "mm_tn")
    da = mm(dy, w_down, "nt", "mm_nt")
    dpre, d_cw8, d_cb = ffn_act_bwd(u, cw8, cb, da, "ffn_act_bwd")
    du = ffn_conv_bwd(dpre, cw8, "ffn_conv_bwd")
    d_w_up = mm(h, du, "tn", "mm_tn")
    dh = mm(du, w_up, "nt", "mm_nt")
    dx, d_gain = rms_bwd(x, gain, dh, dy, "rms_bwd")
    return dx, dict(gain=d_gain, w_up=d_w_up, conv_w=d_cw8[0:3], conv_b=d_cb, w_down=d_w_down)


def _hgrn_layer_fwd(x, gain, w_in4, lb, out_gain, w_out, mall):
    h = rms_fwd(x, gain, "rms_fwd")
    z = [mm(h, w, "nn", "mm_nn") for w in w_in4]
    o, states = hgrn_fwd(z[0], z[1], z[2], lb, mall, "hgrn_fwd")
    on = hgrn_out_fwd(o, z[3], out_gain, "hgrn_out_fwd")
    y = mm(on, w_out, "nn", "mm_nn_add", add=x)
    return y, (x, h, z, o, states, on)


def _hgrn_layer_bwd(dy, saved, gain, w_in4, lb, out_gain, w_out, mall):
    x, h, z, o, states, on = saved
    d_w_out = mm(on, dy, "tn", "mm_tn")
    don = mm(dy, w_out, "nt", "mm_nt")
    do, dzg, d_out_gain = hgrn_out_bwd(o, z[3], out_gain, don, "hgrn_out_bwd")
    dzq, dzf, dzi, dlb = hgrn_bwd(z[0], z[1], z[2], lb, mall, states, do, "hgrn_bwd")
    dz = [dzq, dzf, dzi, dzg]
    d_w_in4 = [mm(h, d, "tn", "mm_tn") for d in dz]
    dh = mm(dz[0], w_in4[0], "nt", "mm_nt")
    for d, w in zip(dz[1:], w_in4[1:]):
        dh = mm(d, w, "nt", "mm_nt_add", add=dh)
    dx, d_gain = rms_bwd(x, gain, dh, dy, "rms_bwd")
    return dx, dict(gain=d_gain, w_in4=d_w_in4, lb=dlb, out_gain=d_out_gain, w_out=d_w_out)


def _mla_layer_fwd(x, gain, w_cq, w_ckv, w_kr, qa_gain, kva_gain, w_q_up, w_kv_up, qn, kn, w_out, cos_t, sin_t):
    h = rms_fwd(x, gain, "rms_fwd")
    cq = mm(h, w_cq, "nn", "mm_nn")
    ckv = mm(h, w_ckv, "nn", "mm_nn")
    kr = mm(h, w_kr, "nn", "mm_nn")
    cqn = rms_fwd(cq, qa_gain, "rms_fwd")
    ckvn = rms_fwd(ckv, kva_gain, "rms_fwd")
    qslots = mm(cqn, w_q_up, "nn", "mm_nn")
    kv = mm(ckvn, w_kv_up, "nn", "mm_nn")
    qr, krot = qk_fwd(qslots, kv, kr, qn, kn, cos_t, sin_t, "qk_fwd")
    o, lse = attn_fwd(qr, krot, kv, "attn_fwd")
    y = mm(o, w_out, "nn", "mm_nn_add", add=x)
    return y, (x, h, cq, ckv, kr, cqn, ckvn, qslots, kv, qr, krot, o, lse)


def _mla_layer_bwd(dy, saved, gain, w_cq, w_ckv, w_kr, qa_gain, kva_gain, w_q_up, w_kv_up, qn, kn, w_out, cos_t, sin_t):
    x, h, cq, ckv, kr, cqn, ckvn, qslots, kv, qr, krot, o, lse = saved
    d_w_out = mm(o, dy, "tn", "mm_tn")
    do = mm(dy, w_out, "nt", "mm_nt")
    dq, dk, dv = attn_bwd(qr, krot, kv, o, lse, do, "attn_bwd")
    dqslots, dkv, dkr, d_qn, d_kn = qk_bwd(qslots, kv, kr, qn, kn, cos_t, sin_t, dq, dk, dv, "qk_bwd")
    d_w_q_up = mm(cqn, dqslots, "tn", "mm_tn")
    dcqn = mm(dqslots, w_q_up, "nt", "mm_nt")
    d_w_kv_up = mm(ckvn, dkv, "tn", "mm_tn")
    dckvn = mm(dkv, w_kv_up, "nt", "mm_nt")
    dcq, d_qa = rms_bwd(cq, qa_gain, dcqn, None, "rms_bwd")
    dckv, d_kva = rms_bwd(ckv, kva_gain, dckvn, None, "rms_bwd")
    d_w_cq = mm(h, dcq, "tn", "mm_tn")
    d_w_ckv = mm(h, dckv, "tn", "mm_tn")
    d_w_kr = mm(h, dkr, "tn", "mm_tn")
    dh = mm(dcq, w_cq, "nt", "mm_nt")
    dh = mm(dckv, w_ckv, "nt", "mm_nt_add", add=dh)
    dh = mm(dkr, w_kr, "nt", "mm_nt_add", add=dh)
    dx, d_gain = rms_bwd(x, gain, dh, dy, "rms_bwd")
    return dx, dict(gain=d_gain, w_cq=d_w_cq, w_ckv=d_w_ckv, w_kr=d_w_kr, qa=d_qa, kva=d_kva, w_q_up=d_w_q_up,
                    w_kv_up=d_w_kv_up, qn=d_qn, kn=d_kn, w_out=d_w_out)


SHARDED = (("hgrn_w_in", 2), ("hgrn_w_out", 1), ("mla_w_in", 1), ("mla_q_a_norm", 1), ("mla_w_q_up", 2),
           ("mla_kv_a_norm", 1), ("mla_w_kv_up", 2), ("mla_w_out", 1), ("ffn_w_up", 2), ("ffn_conv_w", 2), ("ffn_w_down", 1))
GATHER_F32 = ("mla_q_a_norm", "mla_kv_a_norm", "ffn_conv_w")
REPLICATED = ("norm_mix", "norm_ffn", "hgrn_lower_bounds", "hgrn_out_norm", "mla_q_norm", "mla_k_norm", "ffn_conv_b")
WEIGHTS = ("norm_mix", "norm_ffn", "hgrn_w_in", "hgrn_lower_bounds", "hgrn_out_norm", "hgrn_w_out", "mla_w_in",
           "mla_q_a_norm", "mla_w_q_up", "mla_kv_a_norm", "mla_w_kv_up", "mla_q_norm", "mla_k_norm", "mla_w_out",
           "ffn_w_up", "ffn_conv_w", "ffn_conv_b", "ffn_w_down")
N_CHIPS = 4


def _gather_weights(w):
    big = [n for n, _ in SHARDED if n not in GATHER_F32]
    small = [n for n, _ in SHARDED if n in GATHER_F32]
    axis = dict(SHARDED)
    got_big, got_small = exchange([_pack([w[n] for n in big], _MXU, 16), _pack([w[n] for n in small], F32, SUBLANES)],
                                  "chips", False, "gather_weights")
    full = {}
    for names, got in ((big, got_big), (small, got_small)):
        per_chip = [_unpack(got[p], [w[n].shape for n in names]) for p in range(N_CHIPS)]
        for k, n in enumerate(names):
            full[n] = jnp.concatenate([per_chip[p][k] for p in range(N_CHIPS)], axis=axis[n])
    return full


def _pad_cols(a, width):
    return jnp.pad(a, [(0, 0)] * (a.ndim - 1) + [(0, width - a.shape[-1])])


def _head_slots(w):
    lead = w.shape[:-1]
    return _pad_cols(w.reshape(lead + (HEADS, MLA_QK)), MLA_SLOT).reshape(lead + (HEADS * MLA_SLOT,))


def _head_unslots(w):
    lead = w.shape[:-1]
    return w.reshape(lead + (HEADS, MLA_SLOT))[..., :MLA_QK].reshape(lead + (HEADS * MLA_QK,))


def _rope_tables(positions):
    inv_freq = ROPE_THETA ** (-jnp.arange(0, MLA_ROPE, 2, dtype=F32) / MLA_ROPE)
    ang = positions.astype(F32)[:, None] * inv_freq
    cos, sin = jnp.cos(ang), jnp.sin(ang)
    S = positions.shape[0]
    ones, zeros = jnp.ones((S, MLA_NOPE), F32), jnp.zeros((S, MLA_SLOT - MLA_QK), F32)
    return (jnp.concatenate([ones, cos, cos, zeros], axis=1),
            jnp.concatenate([jnp.zeros((S, MLA_NOPE), F32), -sin, sin, zeros], axis=1))


def kernel(x, positions, norm_mix, norm_ffn, hgrn_w_in, hgrn_lower_bounds, hgrn_out_norm, hgrn_w_out, mla_w_in, mla_q_a_norm, mla_w_q_up, mla_kv_a_norm, mla_w_kv_up, mla_q_norm, mla_k_norm, mla_w_out, ffn_w_up, ffn_conv_w, ffn_conv_b, ffn_w_down, loss_target, m_norm_mix, m_norm_ffn, m_hgrn_w_in, m_hgrn_lower_bounds, m_hgrn_out_norm, m_hgrn_w_out, m_mla_w_in, m_mla_q_a_norm, m_mla_w_q_up, m_mla_kv_a_norm, m_mla_w_kv_up, m_mla_q_norm, m_mla_k_norm, m_mla_w_out, m_ffn_w_up, m_ffn_conv_w, m_ffn_conv_b, m_ffn_w_down, v_norm_mix, v_norm_ffn, v_hgrn_w_in, v_hgrn_lower_bounds, v_hgrn_out_norm, v_hgrn_w_out, v_mla_w_in, v_mla_q_a_norm, v_mla_w_q_up, v_mla_kv_a_norm, v_mla_w_kv_up, v_mla_q_norm, v_mla_k_norm, v_mla_w_out, v_ffn_w_up, v_ffn_conv_w, v_ffn_conv_b, v_ffn_w_down):
    args = dict(locals())
    w = {n: args[n] for n in WEIGHTS}
    m = {n: args["m_" + n] for n in WEIGHTS}
    v = {n: args["v_" + n] for n in WEIGHTS}
    depth = norm_mix.shape[0]
    x0 = x[0]
    S = x0.shape[0]

    full = _gather_weights(w)
    cos_t, sin_t = _rope_tables(positions[0])
    lbs = lower_bound_fwd(hgrn_lower_bounds, "lower_bound_fwd")
    mall = jnp.asarray(_hgrn_sum_matrix(min(HGRN_CHUNK, S)), _MXU)
    qn = _pad_cols(mla_q_norm, MLA_SLOT)
    kn = _pad_cols(mla_k_norm, MLA_SLOT)
    cb = ffn_conv_b
    cw8 = jnp.pad(full["ffn_conv_w"], ((0, 0), (0, SUBLANES - 3), (0, 0)))

    def mixer_weights(layer):
        j = layer // 2
        if layer % 2 == 0:
            w_in = full["hgrn_w_in"][j]
            return (norm_mix[layer:layer + 1], [w_in[:, k * D_MODEL:(k + 1) * D_MODEL] for k in range(4)],
                    lbs[j:j + 1], hgrn_out_norm[j:j + 1], full["hgrn_w_out"][j], mall)
        w_in = full["mla_w_in"][j]
        return (norm_mix[layer:layer + 1], w_in[:, :MLA_LORA], w_in[:, MLA_LORA:2 * MLA_LORA],
                _pad_cols(w_in[:, 2 * MLA_LORA:], HEAD_DIM), full["mla_q_a_norm"][j:j + 1], full["mla_kv_a_norm"][j:j + 1],
                _head_slots(full["mla_w_q_up"][j]), full["mla_w_kv_up"][j], qn[j:j + 1], kn[j:j + 1], full["mla_w_out"][j],
                cos_t, sin_t)

    def ffn_weights(layer):
        return (norm_ffn[layer:layer + 1], full["ffn_w_up"][layer], cw8[layer], cb[layer:layer + 1], full["ffn_w_down"][layer])

    h = x0
    saved = []
    for layer in range(depth):
        fwd = _hgrn_layer_fwd if layer % 2 == 0 else _mla_layer_fwd
        h, s_mix = fwd(h, *mixer_weights(layer))
        h, s_ffn = _ffn_fwd(h, *ffn_weights(layer))
        saved.append((s_mix, s_ffn))

    dh, loss_blk = loss_head(h, loss_target[0], "loss_head")
    loss = lax.psum(loss_blk[0, 0], MESH_AXES)

    g_mix, g_ffn = [None] * depth, [None] * depth
    for layer in reversed(range(depth)):
        s_mix, s_ffn = saved[layer]
        dh, g_ffn[layer] = _ffn_bwd(dh, s_ffn, *ffn_weights(layer))
        bwd = _hgrn_layer_bwd if layer % 2 == 0 else _mla_layer_bwd
        dh, g_mix[layer] = bwd(dh, s_mix, *mixer_weights(layer))
    hg = [g_mix[l] for l in range(0, depth, 2)]
    mg = [g_mix[l] for l in range(1, depth, 2)]
    d_p = lower_bound_bwd(hgrn_lower_bounds, hg[1]["lb"], "lower_bound_bwd")

    grads = {
        "norm_mix": jnp.concatenate([g["gain"] for g in g_mix], axis=0),
        "norm_ffn": jnp.concatenate([g["gain"] for g in g_ffn], axis=0),
        "hgrn_w_in": jnp.stack([jnp.concatenate(g["w_in4"], axis=1) for g in hg]),
        "hgrn_lower_bounds": d_p[0:2],
        "hgrn_out_norm": jnp.concatenate([g["out_gain"] for g in hg], axis=0),
        "hgrn_w_out": jnp.stack([g["w_out"] for g in hg]),
        "mla_w_in": jnp.stack([jnp.concatenate([g["w_cq"], g["w_ckv"], g["w_kr"][:, :MLA_ROPE]], axis=1) for g in mg]),
        "mla_q_a_norm": jnp.concatenate([g["qa"] for g in mg], axis=0),
        "mla_w_q_up": jnp.stack([_head_unslots(g["w_q_up"]) for g in mg]),
        "mla_kv_a_norm": jnp.concatenate([g["kva"] for g in mg], axis=0),
        "mla_w_kv_up": jnp.stack([g["w_kv_up"] for g in mg]),
        "mla_q_norm": jnp.concatenate([g["qn"][:, :MLA_QK] for g in mg], axis=0),
        "mla_k_norm": jnp.concatenate([g["kn"][:, :MLA_QK] for g in mg], axis=0),
        "mla_w_out": jnp.stack([g["w_out"] for g in mg]),
        "ffn_w_up": jnp.stack([g["w_up"] for g in g_ffn]),
        "ffn_conv_w": jnp.stack([g["conv_w"] for g in g_ffn]),
        "ffn_conv_b": jnp.concatenate([g["conv_b"] for g in g_ffn], axis=0),
        "ffn_w_down": jnp.stack([g["w_down"] for g in g_ffn]),
    }

    names = [n for n, _ in SHARDED]
    shapes = [w[n].shape for n in names]

    def chip_slab(p):
        parts = []
        for (n, ax), shp in zip(SHARDED, shapes):
            parts.append(lax.slice_in_dim(grads[n], p * shp[ax], (p + 1) * shp[ax], axis=ax))
        return _pack(parts, _MXU, PACK_ROWS)

    send = jnp.stack([chip_slab(p) for p in range(N_CHIPS)])
    (from_chips,) = exchange([send], "chips", True, "reduce_chips")
    partial_sum = sum_slots(from_chips, "sum_slots")
    (other_core,) = exchange([partial_sum], "sibling", False, "reduce_cores", keep_own=False)
    packed = adamw(other_core, *[_pack([t[n] for n in names], F32, PACK_ROWS) for t in (w, m, v)], "adamw",
                   g_extra=partial_sum)
    sharded_out = [dict(zip(names, _unpack(buf, shapes))) for buf in packed]

    rshapes = [w[n].shape for n in REPLICATED]
    (from_all,) = exchange([_pack([grads[n] for n in REPLICATED], F32, SUBLANES)], "all", False, "reduce_all")
    rpacked = adamw(from_all, *[_pack([t[n] for n in REPLICATED], F32, SUBLANES) for t in (w, m, v)], "adamw_small")
    repl_out = [dict(zip(REPLICATED, _unpack(buf, rshapes))) for buf in rpacked]

    result = [loss, dh[None]]
    for k in range(4):
        result += [sharded_out[k][n] if n in sharded_out[k] else repl_out[k][n] for n in WEIGHTS]
    return tuple(result)
```

```python
import numpy as np
import jax
import jax.numpy as jnp
from jax import lax
from jax.experimental import pallas as pl
from jax.experimental.pallas import tpu as pltpu

F32 = jnp.float32
BF16 = jnp.bfloat16
_MXU = BF16

RMS_EPS = 1e-6
D_MODEL = 1024
HEADS = 8
HEAD_DIM = 128
HGRN_CHUNK = 64
MLA_NOPE = 128
MLA_ROPE = 64
MLA_QK = MLA_NOPE + MLA_ROPE
MLA_SLOT = 256
MLA_LORA = 256
MLA_IN_COLS = 2 * MLA_LORA + HEAD_DIM
ROPE_THETA = 10000.0
D_FF = 2816
FF_BLOCK = 1408
LANES = 128
SUBLANES = 8

ADAM_LR = 0.001
ADAM_B1 = 0.9
ADAM_B2 = 0.999
ADAM_EPS = 1e-08
ADAM_WD = 0.01
ADAM_STEP = 10

VMEM_LIMIT = 56 * 1024 * 1024
MESH_AXES = ("x", "y", "c")
N_CHIPS = 4

_NN = ((1,), (0,))
_NT = ((1,), (1,))
_TN = ((0,), (0,))


def _dg(a, b, dims):
    return lax.dot_general(a.astype(_MXU), b.astype(_MXU), (dims, ((), ())), preferred_element_type=F32)


@jax.custom_vjp
def kdot(a, b):
    return _dg(a, b, _NN)


kdot.defvjp(lambda a, b: (_dg(a, b, _NN), (a, b)), lambda r, g: (_dg(g, r[1], _NT), _dg(r[0], g, _TN)))


@jax.custom_vjp
def kdot_nt(a, b):
    return _dg(a, b, _NT)


kdot_nt.defvjp(lambda a, b: (_dg(a, b, _NT), (a, b)), lambda r, g: (_dg(g, r[1], _NN), _dg(g, r[0], _TN)))


@jax.custom_vjp
def kdot_tn(a, b):
    return _dg(a, b, _TN)


kdot_tn.defvjp(lambda a, b: (_dg(a, b, _TN), (a, b)), lambda r, g: (_dg(r[1], g, _NT), _dg(r[0], g, _NN)))


def _pick(d, prefs):
    for p in prefs:
        if d >= p and d % p == 0:
            return p
    return d


def _params(sem):
    return pltpu.CompilerParams(dimension_semantics=sem, vmem_limit_bytes=VMEM_LIMIT)


def _sds(shape, dtype=F32):
    return jax.ShapeDtypeStruct(shape, dtype)


def win(arr, layer, row_off=0, col_off=0, rows=None, cols=None):
    return (arr, layer, row_off, col_off, rows or arr.shape[1] - row_off, cols or arr.shape[2] - col_off)


def mm(a, b, mode, name, add=None, out_dtype=F32, into=None):
    if isinstance(b, tuple):
        b_arr, b_layer, b_r0, b_c0, b_rows, b_cols = b
    else:
        b_arr, b_layer, b_r0, b_c0, (b_rows, b_cols) = b, None, 0, 0, b.shape
    if mode == "nn":
        (M, K), (K2, N) = a.shape, (b_rows, b_cols)
    elif mode == "nt":
        (M, K), (N, K2) = a.shape, (b_rows, b_cols)
    else:
        (K, M), (K2, N) = a.shape, (b_rows, b_cols)
    assert K == K2, (name, a.shape, b_rows, b_cols)
    tm = M if M <= 1024 else _pick(M, (1024, 1408, 512, 256, 128))
    tn = N if N <= 1024 else _pick(N, (1024, 1408, 512, 256, 128))
    tk = K if K <= 512 else _pick(K, (512, 256, 128))
    nk = K // tk
    dims = {"nn": _NN, "nt": _NT, "tn": _TN}[mode]
    a_spec = pl.BlockSpec((tk, tm), lambda i, j, k: (k, i)) if mode == "tn" else pl.BlockSpec((tm, tk), lambda i, j, k: (i, k))
    b_blk = (tn, tk) if mode == "nt" else (tk, tn)
    assert b_r0 % b_blk[0] == 0 and b_c0 % b_blk[1] == 0, (name, b_r0, b_c0, b_blk)
    br, bc = b_r0 // b_blk[0], b_c0 // b_blk[1]
    if mode == "nt":
        b_idx = lambda i, j, k: (br + j, bc + k)
    else:
        b_idx = lambda i, j, k: (br + k, bc + j)
    if b_layer is None:
        b_spec = pl.BlockSpec(b_blk, b_idx)
    else:
        b_spec = pl.BlockSpec((None,) + b_blk, lambda i, j, k: (b_layer,) + b_idx(i, j, k))
    plain = pl.BlockSpec((tm, tn), lambda i, j, k: (i, j))
    has_add = add is not None
    ins = [a, b_arr] + ([add] if has_add else [])
    specs = [a_spec, b_spec] + ([plain] if has_add else [])
    aliases = {}
    if into is None:
        o_spec, out_shape = plain, _sds((M, N), out_dtype)
    else:
        buf, o_layer, o_r0, o_c0, o_rows, o_cols = into
        assert (o_rows, o_cols) == (M, N) and o_r0 % tm == 0 and o_c0 % tn == 0, (name, into[1:], M, N, tm, tn)
        orow, ocol = o_r0 // tm, o_c0 // tn
        o_spec = pl.BlockSpec((None, tm, tn), lambda i, j, k: (o_layer, orow + i, ocol + j))
        out_shape = _sds(buf.shape, buf.dtype)
        aliases = {len(ins): 0}
        ins.append(buf)
        specs.append(pl.BlockSpec(memory_space=pl.ANY))

    def kern(*refs):
        a_ref, b_ref = refs[0], refs[1]
        add_ref = refs[2] if has_add else None
        o_ref, acc = refs[-2], refs[-1]
        k = pl.program_id(2)

        @pl.when(k == 0)
        def _():
            acc[...] = jnp.zeros_like(acc)

        acc[...] += _dg(a_ref[...], b_ref[...], dims)

        @pl.when(k == nk - 1)
        def _():
            r = acc[...]
            if has_add:
                r = r + add_ref[...].astype(F32)
            o_ref[...] = r.astype(o_ref.dtype)

    return pl.pallas_call(
        kern, name=name, grid=(M // tm, N // tn, nk), in_specs=specs, out_specs=o_spec, out_shape=out_shape,
        scratch_shapes=[pltpu.VMEM((tm, tn), F32)], input_output_aliases=aliases,
        compiler_params=_params(("parallel", "parallel", "arbitrary")))(*ins)


def _store(ref, val, first):
    if first is None:
        ref[...] = val.astype(ref.dtype)
        return

    @pl.when(first)
    def _():
        ref[...] = val.astype(ref.dtype)

    @pl.when(jnp.logical_not(first))
    def _():
        ref[...] += val.astype(ref.dtype)


def tilecall(body, name, grid, ins, outs, sem, prefetch=(), aliases=None):
    n_pre, n_in = len(prefetch), len(ins)

    def kern(*refs):
        vals = body(*refs[:n_pre + n_in])
        for ref, val, (_, _, first) in zip(refs[n_pre + n_in:], vals, outs):
            _store(ref, val, None if first is None else first())

    in_specs, out_specs = [s for _, s in ins], [s for _, s, _ in outs]
    kwargs = dict(name=name, out_shape=[sh for sh, _, _ in outs], compiler_params=_params(sem),
                  input_output_aliases={n_pre + k: v for k, v in (aliases or {}).items()})
    if n_pre:
        kwargs["grid_spec"] = pltpu.PrefetchScalarGridSpec(num_scalar_prefetch=n_pre, grid=grid, in_specs=in_specs,
                                                           out_specs=out_specs)
    else:
        kwargs.update(grid=grid, in_specs=in_specs, out_specs=out_specs)
    return pl.pallas_call(kern, **kwargs)(*prefetch, *[a for a, _ in ins])


def _rms(x, g, n):
    ms = jnp.sum(x * x, axis=-1, keepdims=True) / n
    return x * lax.rsqrt(ms + RMS_EPS) * g


def _row_tile(S, w):
    return min(S, 512 if w <= 1024 else 256)


def rms_fwd(x, g, name, col=0, w=None):
    S = x.shape[0]
    w = w or x.shape[1]
    ts = _row_tile(S, w)
    return tilecall(
        lambda x_ref, g_ref: (_rms(x_ref[...], g_ref[...], w),), name, (S // ts,),
        [(x, pl.BlockSpec((ts, w), lambda i: (i, col))), (g, pl.BlockSpec((1, w), lambda i: (0, 0)))],
        [(_sds((S, w), _MXU), pl.BlockSpec((ts, w), lambda i: (i, 0)), None)], ("parallel",))[0]


def rms_bwd(x, g, dh, res, name, w=None):
    S = x.shape[0]
    w = w or x.shape[1]
    ts = _row_tile(S, w)

    def body(x_ref, g_ref, dh_ref, *rest):
        _, vjp = jax.vjp(lambda xv, gv: _rms(xv, gv, w), x_ref[...], g_ref[...])
        dx, dg = vjp(dh_ref[...].astype(F32))
        if rest:
            dx = dx + rest[0][...]
        return dx, dg

    row = pl.BlockSpec((ts, w), lambda i: (i, 0))
    vec = pl.BlockSpec((1, w), lambda i: (0, 0))
    ins = [(x, row), (g, vec), (dh, row)] + ([(res, row)] if res is not None else [])
    return tilecall(body, name, (S // ts,), ins,
                    [(_sds((S, w)), row, None), (_sds((1, w)), vec, lambda: pl.program_id(0) == 0)], ("arbitrary",))


def _shifted(u, halo_ref, is_first):
    rid = lax.broadcasted_iota(jnp.int32, (u.shape[0], 1), 0)
    h7 = jnp.where(is_first, 0.0, halo_ref[7:8, :])
    h6 = jnp.where(is_first, 0.0, halo_ref[6:7, :])
    u1 = jnp.where(rid == 0, h7, pltpu.roll(u, 1, 0))
    u2 = jnp.where(rid == 0, h6, jnp.where(rid == 1, h7, pltpu.roll(u, 2, 0)))
    return u1, u2


def _conv(u, u1, u2, cw_ref, cb_ref):
    return ((cb_ref[...] + u2 * cw_ref[0:1, :]) + u1 * cw_ref[1:2, :]) + u * cw_ref[2:3, :]


def _ffn_specs(S, ts, jmap):
    hb = ts // SUBLANES
    return (pl.BlockSpec((ts, FF_BLOCK), lambda j, i: (i, jmap(j))),
            pl.BlockSpec((SUBLANES, FF_BLOCK), lambda j, i: (jnp.maximum(i * hb - 1, 0), jmap(j))),
            pl.BlockSpec((SUBLANES, FF_BLOCK), lambda j, i: (0, jmap(j))),
            pl.BlockSpec((1, FF_BLOCK), lambda j, i: (0, jmap(j))))


def ffn_act_fwd(u, cw8, cb, name):
    S = u.shape[0]
    ts = _row_tile(S, 2 * D_FF)
    nb = D_FF // FF_BLOCK

    def body(ug, hg, cwg, cbg, uu, hu, cwu, cbu):
        first = pl.program_id(1) == 0
        g = ug[...]
        g1, g2 = _shifted(g, hg, first)
        yg = _conv(g, g1, g2, cwg, cbg)
        v = uu[...]
        v1, v2 = _shifted(v, hu, first)
        yu = _conv(v, v1, v2, cwu, cbu)
        return (yg * jax.nn.sigmoid(yg) * yu,)

    sg = _ffn_specs(S, ts, lambda j: j)
    su = _ffn_specs(S, ts, lambda j: j + nb)
    ins = [(u, sg[0]), (u, sg[1]), (cw8, sg[2]), (cb, sg[3]), (u, su[0]), (u, su[1]), (cw8, su[2]), (cb, su[3])]
    return tilecall(body, name, (nb, S // ts), ins,
                    [(_sds((S, D_FF), _MXU), pl.BlockSpec((ts, FF_BLOCK), lambda j, i: (i, j)), None)],
                    ("parallel", "parallel"))[0]


def ffn_act_bwd(u, cw8, cb, da, name):
    S = u.shape[0]
    ts = _row_tile(S, 2 * D_FF)
    nb = D_FF // FF_BLOCK

    def body(um, hm, cwm, cbm, up, hp, cwp, cbp, da_ref):
        j = pl.program_id(0)
        first = pl.program_id(1) == 0
        a = um[...]
        a1, a2 = _shifted(a, hm, first)
        ym = _conv(a, a1, a2, cwm, cbm)
        b = up[...]
        b1, b2 = _shifted(b, hp, first)
        yp = _conv(b, b1, b2, cwp, cbp)
        d = da_ref[...]
        sm = jax.nn.sigmoid(ym)
        d_gate = d * yp * (sm * (1.0 + ym * (1.0 - sm)))
        d_up = d * (yp * jax.nn.sigmoid(yp))
        dy = jnp.where(j < nb, d_gate, d_up)
        dcw = jnp.concatenate(
            [jnp.sum(dy * a2, axis=0, keepdims=True), jnp.sum(dy * a1, axis=0, keepdims=True),
             jnp.sum(dy * a, axis=0, keepdims=True), jnp.zeros((SUBLANES - 3, dy.shape[1]), F32)], axis=0)
        return dy, dcw, jnp.sum(dy, axis=0, keepdims=True)

    sm_ = _ffn_specs(S, ts, lambda j: j)
    sp_ = _ffn_specs(S, ts, lambda j: (j + nb) % (2 * nb))
    ins = [(u, sm_[0]), (u, sm_[1]), (cw8, sm_[2]), (cb, sm_[3]), (u, sp_[0]), (u, sp_[1]), (cw8, sp_[2]), (cb, sp_[3]),
           (da, pl.BlockSpec((ts, FF_BLOCK), lambda j, i: (i, j % nb)))]
    first_row = lambda: pl.program_id(1) == 0
    return tilecall(body, name, (2 * nb, S // ts), ins,
                    [(_sds((S, 2 * D_FF)), sm_[0], None), (_sds((SUBLANES, 2 * D_FF)), sm_[2], first_row),
                     (_sds((1, 2 * D_FF)), sm_[3], first_row)], ("parallel", "arbitrary"))


def ffn_conv_bwd(dy, cw8, name):
    S = dy.shape[0]
    ts = _row_tile(S, 2 * D_FF)
    hb = ts // SUBLANES
    nrow = S // ts
    ncol = 2 * D_FF // FF_BLOCK

    def body(dy_ref, halo_ref, cw_ref):
        last = pl.program_id(1) == nrow - 1
        d = dy_ref[...]
        rid = lax.broadcasted_iota(jnp.int32, (ts, 1), 0)
        n0 = jnp.where(last, 0.0, halo_ref[0:1, :])
        n1 = jnp.where(last, 0.0, halo_ref[1:2, :])
        d1 = jnp.where(rid == ts - 1, n0, pltpu.roll(d, ts - 1, 0))
        d2 = jnp.where(rid == ts - 1, n1, jnp.where(rid == ts - 2, n0, pltpu.roll(d, ts - 2, 0)))
        return (d * cw_ref[2:3, :] + d1 * cw_ref[1:2, :] + d2 * cw_ref[0:1, :],)

    row = pl.BlockSpec((ts, FF_BLOCK), lambda j, i: (i, j))
    ins = [(dy, row), (dy, pl.BlockSpec((SUBLANES, FF_BLOCK), lambda j, i: (jnp.minimum((i + 1) * hb, S // SUBLANES - 1), j))),
           (cw8, pl.BlockSpec((SUBLANES, FF_BLOCK), lambda j, i: (0, j)))]
    return tilecall(body, name, (ncol, nrow), ins, [(_sds((S, 2 * D_FF), _MXU), row, None)], ("parallel", "parallel"))[0]


def _hgrn_levels(C):
    out, m = [], C // 2
    while m >= 1:
        out.append(m)
        m //= 2
    return out


def _hgrn_sum_matrix(C):
    t = np.arange(C)[:, None]
    u = np.arange(C)[None, :]
    blocks = [u <= t, u > t]
    for m in _hgrn_levels(C):
        r = (t // (2 * m)) * (2 * m) + m
        right = (t % (2 * m)) >= m
        blocks.append(right & (u > r) & (u <= t))
        blocks.append((~right) & (u > t) & (u <= r))
    return np.concatenate(blocks, axis=0).astype(np.float32)


def _make_partial_sums(nb, C):
    @jax.custom_vjp
    def sums(mall, lf):
        hi = lf.astype(_MXU)
        r1 = lf - hi.astype(F32)
        mid = r1.astype(_MXU)
        lo = (r1 - mid.astype(F32)).astype(_MXU)
        e = _dg(mall, hi, _NN) + _dg(mall, mid, _NN) + _dg(mall, lo, _NN)
        return tuple(e[b * C:(b + 1) * C] for b in range(nb))

    def fwd(mall, lf):
        return sums(mall, lf), mall

    def bwd(mall, gs):
        return jnp.zeros_like(mall), _dg(mall, jnp.concatenate(gs, axis=0), _TN)

    sums.defvjp(fwd, bwd)
    return sums


def _hgrn_chunk(zq, zf, v, lb, st, mall, C):
    levels = _hgrn_levels(C)
    qs = zq * jax.nn.sigmoid(zq)
    fg = lb + (1.0 - lb) * jax.nn.sigmoid(zf)
    k = 1.0 - fg
    e = _make_partial_sums(2 + 2 * len(levels), C)(mall, jnp.log(fg))
    g_incl, g_after = e[0], e[1]
    rid = lax.broadcasted_iota(jnp.int32, (C, 1), 0)
    tt = lax.broadcasted_iota(jnp.int32, (C, C), 0)
    ss = lax.broadcasted_iota(jnp.int32, (C, C), 1)
    o = kdot_nt(qs * jnp.exp(g_incl), st)
    o = o + jnp.sum(qs * k, axis=-1, keepdims=True) * v
    scores = jnp.zeros((C, C), F32)
    for li, m in enumerate(levels):
        sh = int(np.log2(m))
        right = ((rid >> sh) & 1) == 1
        qt = jnp.where(right, qs * jnp.exp(e[2 + 2 * li]), 0.0)
        kt = jnp.where(right, 0.0, k * jnp.exp(e[3 + 2 * li]))
        pair = ((tt >> (sh + 1)) == (ss >> (sh + 1))) & (((tt >> sh) & 1) == 1) & (((ss >> sh) & 1) == 0)
        scores = scores + jnp.where(pair, kdot_nt(qt, kt), 0.0)
    o = o + kdot(scores, v)
    g_last = jnp.sum(jnp.where(rid == C - 1, g_incl, 0.0), axis=0, keepdims=True)
    st_new = st * jnp.exp(g_last) + kdot_tn(v, k * jnp.exp(g_after))
    return o, st_new


def _hgrn_in_specs(C, nc, rev):
    cm = (lambda c: nc - 1 - c) if rev else (lambda c: c)
    blk = lambda: pl.BlockSpec((C, HEAD_DIM), lambda h, c: (cm(c), h))
    return cm, [blk(), blk(), blk(), pl.BlockSpec((1, HEAD_DIM), lambda h, c: (0, h))]


def hgrn_fwd(zq, zf, zi, lb, mall, name):
    S = zq.shape[0]
    C = min(HGRN_CHUNK, S)
    nc = S // C

    def kern(zq_ref, zf_ref, zi_ref, lb_ref, mall_ref, o_ref, st_ref, st):
        @pl.when(pl.program_id(1) == 0)
        def _():
            st[...] = jnp.zeros_like(st)

        s_in = st[...]
        st_ref[...] = s_in
        o, s_new = _hgrn_chunk(zq_ref[...], zf_ref[...], zi_ref[...], lb_ref[...], s_in, mall_ref[...], C)
        o_ref[...] = o
        st[...] = s_new

    _, specs = _hgrn_in_specs(C, nc, False)
    return pl.pallas_call(
        kern, name=name, grid=(HEADS, nc),
        in_specs=specs + [pl.BlockSpec(mall.shape, lambda h, c: (0, 0))],
        out_specs=[pl.BlockSpec((C, HEAD_DIM), lambda h, c: (c, h)),
                   pl.BlockSpec((None, None, HEAD_DIM, HEAD_DIM), lambda h, c: (h, c, 0, 0))],
        out_shape=[_sds((S, D_MODEL)), _sds((HEADS, nc, HEAD_DIM, HEAD_DIM))],
        scratch_shapes=[pltpu.VMEM((HEAD_DIM, HEAD_DIM), F32)],
        compiler_params=_params(("parallel", "arbitrary")))(zq, zf, zi, lb, mall)


def hgrn_bwd(zq, zf, zi, lb, mall, states, do, name):
    S = zq.shape[0]
    C = min(HGRN_CHUNK, S)
    nc = S // C

    def kern(zq_ref, zf_ref, zi_ref, lb_ref, mall_ref, st_ref, do_ref, dq_ref, df_ref, di_ref, dlb_ref, dst):
        first = pl.program_id(1) == 0

        @pl.when(first)
        def _():
            dst[...] = jnp.zeros_like(dst)

        mall_v = mall_ref[...]
        _, vjp = jax.vjp(lambda a, b, c, d, e: _hgrn_chunk(a, b, c, d, e, mall_v, C),
                         zq_ref[...], zf_ref[...], zi_ref[...], lb_ref[...], st_ref[...])
        ga, gb, gv, gl, gs = vjp((do_ref[...], dst[...]))
        dq_ref[...] = ga.astype(dq_ref.dtype)
        df_ref[...] = gb.astype(df_ref.dtype)
        di_ref[...] = gv.astype(di_ref.dtype)
        _store(dlb_ref, gl, first)
        dst[...] = gs

    cm, specs = _hgrn_in_specs(C, nc, True)
    row = lambda: pl.BlockSpec((C, HEAD_DIM), lambda h, c: (cm(c), h))
    return pl.pallas_call(
        kern, name=name, grid=(HEADS, nc),
        in_specs=specs + [pl.BlockSpec(mall.shape, lambda h, c: (0, 0)),
                          pl.BlockSpec((None, None, HEAD_DIM, HEAD_DIM), lambda h, c: (h, cm(c), 0, 0)), row()],
        out_specs=[row(), row(), row(), pl.BlockSpec((1, HEAD_DIM), lambda h, c: (0, h))],
        out_shape=[_sds((S, D_MODEL), _MXU), _sds((S, D_MODEL), _MXU), _sds((S, D_MODEL), _MXU), _sds((1, D_MODEL))],
        scratch_shapes=[pltpu.VMEM((HEAD_DIM, HEAD_DIM), F32)],
        compiler_params=_params(("parallel", "arbitrary")))(zq, zf, zi, lb, mall, states, do)


def _hgrn_out(o, g, gain):
    return _rms(o, gain, HEAD_DIM) * (g * jax.nn.sigmoid(g))


def hgrn_out_fwd(o, zg, gain, name):
    S = o.shape[0]
    ts = _row_tile(S, D_MODEL)
    blk = pl.BlockSpec((ts, HEAD_DIM), lambda i, h: (i, h))
    return tilecall(lambda o_ref, g_ref, w_ref: (_hgrn_out(o_ref[...], g_ref[...], w_ref[...]),), name, (S // ts, HEADS),
                    [(o, blk), (zg, blk), (gain, pl.BlockSpec((1, HEAD_DIM), lambda i, h: (0, 0)))],
                    [(_sds((S, D_MODEL), _MXU), blk, None)], ("parallel", "parallel"))[0]


def hgrn_out_bwd(o, zg, gain, don, name):
    S = o.shape[0]
    ts = _row_tile(S, D_MODEL)
    blk = pl.BlockSpec((ts, HEAD_DIM), lambda i, h: (i, h))
    vec = pl.BlockSpec((1, HEAD_DIM), lambda i, h: (0, 0))

    def body(o_ref, g_ref, w_ref, d_ref):
        _, vjp = jax.vjp(_hgrn_out, o_ref[...], g_ref[...], w_ref[...])
        return vjp(d_ref[...])

    return tilecall(body, name, (S // ts, HEADS), [(o, blk), (zg, blk), (gain, vec), (don, blk)],
                    [(_sds((S, D_MODEL)), blk, None), (_sds((S, D_MODEL), _MXU), blk, None),
                     (_sds((1, HEAD_DIM)), vec, lambda: (pl.program_id(0) == 0) & (pl.program_id(1) == 0))],
                    ("arbitrary", "arbitrary"))


def _lb_soft(p0, p1):
    mx = jnp.maximum(p0, p1)
    e0, e1 = jnp.exp(p0 - mx), jnp.exp(p1 - mx)
    s0, s1 = e0 / (e0 + e1), e1 / (e0 + e1)
    return (s0 + s1) - s0


def lower_bound_fwd(p, name):
    assert p.shape[0] == 2

    def body(p_ref):
        s = _lb_soft(p_ref[0:1, :], p_ref[1:2, :])
        return (jnp.concatenate([jnp.zeros_like(s), s] + [jnp.zeros_like(s)] * (SUBLANES - 2), axis=0),)

    spec8 = pl.BlockSpec((SUBLANES, p.shape[1]), lambda: (0, 0))
    return tilecall(body, name, (), [(p, pl.BlockSpec(p.shape, lambda: (0, 0)))], [(_sds((SUBLANES, p.shape[1])), spec8, None)], ())[0]


def lower_bound_bwd(p, dlb1, name):
    def body(p_ref, d_ref):
        _, vjp = jax.vjp(_lb_soft, p_ref[0:1, :], p_ref[1:2, :])
        g0, g1 = vjp(d_ref[...])
        return (jnp.concatenate([g0, g1] + [jnp.zeros_like(g0)] * (SUBLANES - 2), axis=0),)

    spec8 = pl.BlockSpec((SUBLANES, p.shape[1]), lambda: (0, 0))
    return tilecall(body, name, (), [(p, pl.BlockSpec(p.shape, lambda: (0, 0))), (dlb1, pl.BlockSpec(dlb1.shape, lambda: (0, 0)))],
                    [(_sds((SUBLANES, p.shape[1])), spec8, None)], ())[0]


@jax.custom_vjp
def _swap_rope_halves(x):
    lane = lax.broadcasted_iota(jnp.int32, x.shape, 1)
    lo = (lane >= MLA_NOPE) & (lane < MLA_NOPE + MLA_ROPE // 2)
    hi = (lane >= MLA_NOPE + MLA_ROPE // 2) & (lane < MLA_QK)
    return jnp.where(lo, pltpu.roll(x, MLA_SLOT - MLA_ROPE // 2, 1), jnp.where(hi, pltpu.roll(x, MLA_ROPE // 2, 1), 0.0))


_swap_rope_halves.defvjp(lambda x: (_swap_rope_halves(x), None), lambda _, g: (_swap_rope_halves(g),))


def _norm_rope(x, gain, cos_t, sin_t):
    y = _rms(x, gain, MLA_QK)
    return y * cos_t + _swap_rope_halves(y) * sin_t


def _qk_heads(qs, kn, kr, qn, kn_gain, cos_t, sin_t):
    q = _norm_rope(qs, qn, cos_t, sin_t)
    k = _norm_rope(jnp.concatenate([kn, kr], axis=1), kn_gain, cos_t, sin_t)
    return q, k


def _qk_specs(ts):
    slot = pl.BlockSpec((ts, MLA_SLOT), lambda i, h: (i, h))
    nope = pl.BlockSpec((ts, HEAD_DIM), lambda i, h: (i, 2 * h))
    shared = pl.BlockSpec((ts, HEAD_DIM), lambda i, h: (i, 0))
    gain = pl.BlockSpec((1, MLA_SLOT), lambda i, h: (0, 0))
    table = pl.BlockSpec((ts, MLA_SLOT), lambda i, h: (i, 0))
    return slot, nope, shared, gain, table


def qk_fwd(qslots, kv, krope, qn, kn, cos_t, sin_t, name):
    S = qslots.shape[0]
    ts = _row_tile(S, D_MODEL)
    slot, nope, shared, gain, table = _qk_specs(ts)

    def body(q_ref, kn_ref, kr_ref, qn_ref, kg_ref, c_ref, s_ref):
        return _qk_heads(q_ref[...], kn_ref[...], kr_ref[...], qn_ref[...], kg_ref[...], c_ref[...], s_ref[...])

    out = _sds((S, HEADS * MLA_SLOT), _MXU)
    return tilecall(body, name, (S // ts, HEADS),
                    [(qslots, slot), (kv, nope), (krope, shared), (qn, gain), (kn, gain), (cos_t, table), (sin_t, table)],
                    [(out, slot, None), (out, slot, None)], ("parallel", "parallel"))


def qk_bwd(qslots, kv, krope, qn, kn, cos_t, sin_t, dq, dk, dv, name):
    S = qslots.shape[0]
    ts = _row_tile(S, D_MODEL)
    slot, nope, shared, gain, table = _qk_specs(ts)
    vblk = pl.BlockSpec((ts, HEAD_DIM), lambda i, h: (i, h))

    def body(q_ref, kn_ref, kr_ref, qn_ref, kg_ref, c_ref, s_ref, dq_ref, dk_ref, dv_ref):
        c, s = c_ref[...], s_ref[...]
        _, vjp = jax.vjp(lambda a, b, r, g1, g2: _qk_heads(a, b, r, g1, g2, c, s),
                         q_ref[...], kn_ref[...], kr_ref[...], qn_ref[...], kg_ref[...])
        ga, gb, gr, g1, g2 = vjp((dq_ref[...], dk_ref[...]))
        return ga, jnp.concatenate([gb, dv_ref[...]], axis=1), gr, g1, g2

    first_head = lambda: pl.program_id(1) == 0
    first = lambda: (pl.program_id(0) == 0) & (pl.program_id(1) == 0)
    wide = _sds((S, HEADS * MLA_SLOT), _MXU)
    return tilecall(body, name, (S // ts, HEADS),
                    [(qslots, slot), (kv, nope), (krope, shared), (qn, gain), (kn, gain), (cos_t, table), (sin_t, table),
                     (dq, slot), (dk, slot), (dv, vblk)],
                    [(wide, slot, None), (wide, slot, None), (_sds((S, HEAD_DIM)), shared, first_head),
                     (_sds((1, MLA_SLOT)), gain, first), (_sds((1, MLA_SLOT)), gain, first)], ("arbitrary", "arbitrary"))


_ATTN_SCALE = MLA_QK ** -0.5
ATTN_TILE = 512


def _causal_pairs(nq, by_row):
    pairs = [(i, j) for i in range(nq) for j in range(i + 1)] if by_row else [(i, j) for j in range(nq) for i in range(j, nq)]
    return jnp.asarray([p[0] for p in pairs], jnp.int32), jnp.asarray([p[1] for p in pairs], jnp.int32)


def _diag_mask(s, tq):
    rows = lax.broadcasted_iota(jnp.int32, (tq, tq), 0)
    cols = lax.broadcasted_iota(jnp.int32, (tq, tq), 1)
    return jnp.where(rows >= cols, s, -jnp.inf)


def attn_fwd(qr, kr, kv, name):
    S = qr.shape[0]
    tq = min(S, ATTN_TILE)
    nq = S // tq
    i_tab, j_tab = _causal_pairs(nq, True)

    def kern(it, jt, q_ref, k_ref, v_ref, o_ref, lse_ref, m_s, l_s, acc):
        n = pl.program_id(1)
        i, j = it[n], jt[n]

        @pl.when(j == 0)
        def _():
            m_s[...] = jnp.full_like(m_s, -jnp.inf)
            l_s[...] = jnp.zeros_like(l_s)
            acc[...] = jnp.zeros_like(acc)

        def step(diagonal):
            s = _dg(q_ref[...], k_ref[...], _NT) * _ATTN_SCALE
            if diagonal:
                s = _diag_mask(s, tq)
            m_prev = m_s[...]
            m_new = jnp.maximum(m_prev, jnp.max(s, axis=-1, keepdims=True))
            alpha = jnp.exp(m_prev - m_new)
            p = jnp.exp(s - m_new)
            l_s[...] = alpha * l_s[...] + jnp.sum(p, axis=-1, keepdims=True)
            acc[...] = alpha * acc[...] + _dg(p, v_ref[...], _NN)
            m_s[...] = m_new

        @pl.when(j < i)
        def _():
            step(False)

        @pl.when(j == i)
        def _():
            step(True)
            l = l_s[...]
            o_ref[...] = acc[...] / l
            lse_ref[...] = jnp.broadcast_to(m_s[...] + jnp.log(l), lse_ref.shape)

    out = _sds((S, HEADS * HEAD_DIM))
    oblk = pl.BlockSpec((tq, HEAD_DIM), lambda h, n, it, jt: (it[n], h))
    return pl.pallas_call(
        kern, name=name,
        grid_spec=pltpu.PrefetchScalarGridSpec(
            num_scalar_prefetch=2, grid=(HEADS, i_tab.shape[0]),
            in_specs=[pl.BlockSpec((tq, MLA_SLOT), lambda h, n, it, jt: (it[n], h)),
                      pl.BlockSpec((tq, MLA_SLOT), lambda h, n, it, jt: (jt[n], h)),
                      pl.BlockSpec((tq, HEAD_DIM), lambda h, n, it, jt: (jt[n], 2 * h + 1))],
            out_specs=[oblk, oblk],
            scratch_shapes=[pltpu.VMEM((tq, 1), F32), pltpu.VMEM((tq, 1), F32), pltpu.VMEM((tq, HEAD_DIM), F32)]),
        out_shape=[out, out], compiler_params=_params(("parallel", "arbitrary")))(i_tab, j_tab, qr, kr, kv)


def attn_bwd(qr, kr, kv, o, lse, do, name):
    S = qr.shape[0]
    tq = min(S, ATTN_TILE)
    nq = S // tq
    i_tab, j_tab = _causal_pairs(nq, False)

    def kern(it, jt, q_ref, k_ref, v_ref, o_ref, lse_ref, do_ref, dq_ref, dk_ref, dv_ref, dk_acc, dv_acc):
        n = pl.program_id(1)
        i, j = it[n], jt[n]

        @pl.when(n == 0)
        def _():
            dq_ref[...] = jnp.zeros_like(dq_ref)

        @pl.when(i == j)
        def _():
            dk_acc[...] = jnp.zeros_like(dk_acc)
            dv_acc[...] = jnp.zeros_like(dv_acc)

        def step(diagonal):
            s = _dg(q_ref[...], k_ref[...], _NT) * _ATTN_SCALE - lse_ref[:, 0:1]
            if diagonal:
                s = _diag_mask(s, tq)
            p = jnp.exp(s)
            g = do_ref[...]
            delta = jnp.sum(g * o_ref[...], axis=-1, keepdims=True)
            dv_acc[...] += _dg(p, g, _TN)
            ds = p * (_dg(g, v_ref[...], _NT) - delta) * _ATTN_SCALE
            dk_acc[...] += _dg(ds, q_ref[...], _TN)
            rows = pl.ds(pl.multiple_of(i * tq, tq), tq)
            dq_ref[rows, :] += _dg(ds, k_ref[...], _NN)

        @pl.when(i > j)
        def _():
            step(False)

        @pl.when(i == j)
        def _():
            step(True)

        @pl.when(i == nq - 1)
        def _():
            dk_ref[...] = dk_acc[...]
            dv_ref[...] = dv_acc[...]

    qblk = pl.BlockSpec((tq, MLA_SLOT), lambda h, n, it, jt: (it[n], h))
    oblk = pl.BlockSpec((tq, HEAD_DIM), lambda h, n, it, jt: (it[n], h))
    kblk = pl.BlockSpec((tq, MLA_SLOT), lambda h, n, it, jt: (jt[n], h))
    return pl.pallas_call(
        kern, name=name,
        grid_spec=pltpu.PrefetchScalarGridSpec(
            num_scalar_prefetch=2, grid=(HEADS, i_tab.shape[0]),
            in_specs=[qblk, kblk, pl.BlockSpec((tq, HEAD_DIM), lambda h, n, it, jt: (jt[n], 2 * h + 1)), oblk, oblk, oblk],
            out_specs=[pl.BlockSpec((S, MLA_SLOT), lambda h, n, it, jt: (0, h)), kblk,
                       pl.BlockSpec((tq, HEAD_DIM), lambda h, n, it, jt: (jt[n], h))],
            scratch_shapes=[pltpu.VMEM((tq, MLA_SLOT), F32), pltpu.VMEM((tq, HEAD_DIM), F32)]),
        out_shape=[_sds((S, HEADS * MLA_SLOT)), _sds((S, HEADS * MLA_SLOT)), _sds((S, HEADS * HEAD_DIM))],
        compiler_params=_params(("parallel", "arbitrary")))(i_tab, j_tab, qr, kr, kv, o, lse, do)


def loss_head(y, target, name):
    S, Dm = y.shape
    ts = _row_tile(S, Dm)

    def body(y_ref, t_ref):
        e = y_ref[...] - t_ref[...]
        tot = jnp.sum(jnp.sum(e * e, axis=-1, keepdims=True) / Dm, axis=0, keepdims=True)
        return e / Dm, jnp.broadcast_to(0.5 * tot, (SUBLANES, LANES))

    row = pl.BlockSpec((ts, Dm), lambda i: (i, 0))
    return tilecall(body, name, (S // ts,), [(y, row), (target, row)],
                    [(_sds((S, Dm)), row, None),
                     (_sds((SUBLANES, LANES)), pl.BlockSpec((SUBLANES, LANES), lambda i: (0, 0)), lambda: pl.program_id(0) == 0)],
                    ("arbitrary",))


_CHIP_FLIPS = ((1, 0), (0, 1), (1, 1))
_PEER_FLIPS = {
    "chips": ((1, 0, 0), (0, 1, 0), (1, 1, 0)),
    "sibling": ((0, 0, 1),),
    "all": tuple((a, b, c) for a in (0, 1) for b in (0, 1) for c in (0, 1))[1:],
}
_SLOT_WEIGHTS = {"chips": (2, 1, 0), "sibling": (0, 0, 1), "all": (4, 2, 1)}
_HBM = pl.BlockSpec(memory_space=pltpu.HBM)


def _me():
    return lax.axis_index("x"), lax.axis_index("y"), lax.axis_index("c")


def _remote(src, dst, send_sem, recv_sem, peer):
    return pltpu.make_async_remote_copy(src_ref=src, dst_ref=dst, send_sem=send_sem, recv_sem=recv_sem,
                                        device_id=peer, device_id_type=pl.DeviceIdType.MESH)


def exchange(arrs, group, slab_weights, name, keep_own=True):
    flips = _PEER_FLIPS[group]
    wx, wy, wc = _SLOT_WEIGHTS[group]
    n_slots = len(flips) + (1 if keep_own else 0)
    n = len(arrs)

    def slab(ref, a, pos):
        w = slab_weights[a]
        return ref if w is None else ref.at[w[0] * pos[0] + w[1] * pos[1] + w[2] * pos[2]]

    def kern(*refs):
        srcs, outs = refs[:n], refs[n:2 * n]
        send_sems, recv_sems = refs[2 * n:2 * n + 2]
        me = _me()
        my_slot = wx * me[0] + wy * me[1] + wc * me[2]
        copies = []
        if keep_own:
            local_sems = refs[2 * n + 2]
            for a in range(n):
                cp = pltpu.make_async_copy(slab(srcs[a], a, me), outs[a].at[my_slot], local_sems.at[a])
                cp.start()
                copies.append(cp)
        for f, flip in enumerate(flips):
            peer = tuple(m ^ b if b else m for m, b in zip(me, flip))
            for a in range(n):
                cp = _remote(slab(srcs[a], a, peer), outs[a].at[my_slot if keep_own else f],
                             send_sems.at[f, a], recv_sems.at[f, a], peer)
                cp.start()
                copies.append(cp)
        for cp in copies:
            cp.wait()

    out_shape = [_sds((n_slots,) + (a.shape if slab_weights[k] is None else a.shape[1:]), a.dtype) for k, a in enumerate(arrs)]
    sems = [pltpu.SemaphoreType.DMA((len(flips), n)), pltpu.SemaphoreType.DMA((len(flips), n))]
    return pl.pallas_call(
        kern, name=name, in_specs=[_HBM] * n, out_specs=[_HBM] * n, out_shape=out_shape,
        scratch_shapes=sems + ([pltpu.SemaphoreType.DMA((n,))] if keep_own else []))(*arrs)


def _chip_window(ref, kind, size, chip, layers):
    if kind == "rows":
        return ref.at[layers, pl.ds(chip * size, size), :]
    return ref.at[layers, :, pl.ds(pl.multiple_of(chip * size, LANES), size)]


def gather_big(shards, kinds, name):
    n = len(shards)
    fulls = []
    for s, kind in zip(shards, kinds):
        L, r, c = s.shape
        fulls.append(_sds((L, N_CHIPS * r, c) if kind == "rows" else (L, r, N_CHIPS * c), s.dtype))

    def kern(*refs):
        srcs, outs = refs[:n], refs[n:2 * n]
        ici_s, ici_r, d2d_s, d2d_r = refs[2 * n:]
        x, y, c = _me()
        my_chip = 2 * x + y
        ici, fwd = [], []
        for a in range(n):
            L, r, cc = shards[a].shape
            size = r if kinds[a] == "rows" else cc
            mine = pl.ds(c * (L // 2), L // 2)
            for f, (fx, fy) in enumerate(_CHIP_FLIPS):
                cp = _remote(srcs[a].at[mine], _chip_window(outs[a], kinds[a], size, my_chip, mine),
                             ici_s.at[a, f], ici_r.at[a, f], (x ^ fx, y ^ fy, c))
                cp.start()
                ici.append(cp)
        for a in range(n):
            L, r, cc = shards[a].shape
            size = r if kinds[a] == "rows" else cc
            mine = pl.ds(c * (L // 2), L // 2)
            for f, (fx, fy) in enumerate(_CHIP_FLIPS):
                ici[a * len(_CHIP_FLIPS) + f].wait_recv()
                landed = _chip_window(outs[a], kinds[a], size, 2 * (x ^ fx) + (y ^ fy), mine)
                cp = _remote(landed, landed, d2d_s.at[a, f], d2d_r.at[a, f], (x, y, 1 - c))
                cp.start()
                fwd.append(cp)
        for cp in ici:
            cp.wait_send()
        for cp in fwd:
            cp.wait()

    sem = pltpu.SemaphoreType.DMA((n, len(_CHIP_FLIPS)))
    return pl.pallas_call(kern, name=name, in_specs=[_HBM] * n, out_specs=[_HBM] * n, out_shape=fulls,
                          scratch_shapes=[sem, sem, sem, sem])(*shards)


def send_other_half(arrs, name):
    n = len(arrs)

    def kern(*refs):
        srcs, outs = refs[:n], refs[n:2 * n]
        send_sems, recv_sems = refs[2 * n:]
        x, y, c = _me()
        copies = []
        for a in range(n):
            hl = arrs[a].shape[0] // 2
            cp = _remote(srcs[a].at[pl.ds((1 - c) * hl, hl)], outs[a], send_sems.at[a], recv_sems.at[a], (x, y, 1 - c))
            cp.start()
            copies.append(cp)
        for cp in copies:
            cp.wait()

    return pl.pallas_call(
        kern, name=name, in_specs=[_HBM] * n, out_specs=[_HBM] * n,
        out_shape=[_sds((a.shape[0] // 2,) + a.shape[1:], a.dtype) for a in arrs],
        scratch_shapes=[pltpu.SemaphoreType.DMA((n,)), pltpu.SemaphoreType.DMA((n,))])(*arrs)


def scatter_to_chips(arrs, kinds, name):
    n = len(arrs)
    shapes = []
    for a, kind in zip(arrs, kinds):
        l, R, C = a.shape
        shapes.append((l, R // N_CHIPS, C) if kind == "rows" else (l, R, C // N_CHIPS))

    def kern(*refs):
        srcs, outs = refs[:n], refs[n:2 * n]
        send_sems, recv_sems = refs[2 * n:]
        x, y, c = _me()
        copies = []
        for a in range(n):
            size = shapes[a][1] if kinds[a] == "rows" else shapes[a][2]
            for f, (fx, fy) in enumerate(_CHIP_FLIPS):
                window = _chip_window(srcs[a], kinds[a], size, 2 * (x ^ fx) + (y ^ fy), slice(None))
                cp = _remote(window, outs[a].at[f], send_sems.at[a, f], recv_sems.at[a, f], (x ^ fx, y ^ fy, c))
                cp.start()
                copies.append(cp)
        for cp in copies:
            cp.wait()

    sem = pltpu.SemaphoreType.DMA((n, len(_CHIP_FLIPS)))
    return pl.pallas_call(
        kern, name=name, in_specs=[_HBM] * n, out_specs=[_HBM] * n,
        out_shape=[_sds((len(_CHIP_FLIPS),) + s, a.dtype) for s, a in zip(shapes, arrs)],
        scratch_shapes=[sem, sem])(*arrs)


def _stack_tile(r, c):
    for t in (1024, 704, 512, 352, 256, 128, 64, 32, 16):
        if r % t == 0 and t * c * 4 <= 3 * 512 * 1024:
            return t
    return r


def _window_map(kind, r, tr):
    nrt = r // tr
    if kind == "rows":
        return lambda l, i, chip: (l, chip[0] * nrt + i, 0)
    return lambda l, i, chip: (l, i, chip[0])


def place(full, shard, kind, chip, name):
    L, r, c = shard.shape
    tr = _stack_tile(r, c)
    wmap = _window_map(kind, r, tr)
    return tilecall(lambda chip_ref, s_ref, f_ref: (s_ref[...],), name, (L, r // tr),
                    [(shard, pl.BlockSpec((None, tr, c), lambda l, i, chip: (l, i, 0))), (full, pl.BlockSpec(memory_space=pl.ANY))],
                    [(_sds(full.shape, full.dtype), pl.BlockSpec((None, tr, c), lambda l, i, chip: wmap(l, i, chip)), None)],
                    ("parallel", "parallel"), prefetch=(chip,), aliases={1: 0})[0]


def add_cores(g, other, core, name):
    L, R, C = g.shape
    hl = L // 2
    tr = _stack_tile(R, C)
    blk = (None, tr, C)
    return tilecall(lambda core_ref, a_ref, b_ref: (a_ref[...] + b_ref[...],), name, (hl, R // tr),
                    [(g, pl.BlockSpec(blk, lambda l, i, core: (core[0] * hl + l, i, 0))),
                     (other, pl.BlockSpec(blk, lambda l, i, core: (l, i, 0)))],
                    [(_sds((hl, R, C), _MXU), pl.BlockSpec(blk, lambda l, i, core: (l, i, 0)), None)],
                    ("parallel", "parallel"), prefetch=(core,))[0]


def add_chips(own, got, kind, chip, name):
    nf, l, r, c = got.shape
    tr = _stack_tile(r, c)
    wmap = _window_map(kind, r, tr)

    def body(chip_ref, own_ref, *got_refs):
        acc = own_ref[...].astype(F32)
        for ref in got_refs:
            acc = acc + ref[...].astype(F32)
        return (acc,)

    return tilecall(body, name, (l, r // tr),
                    [(own, pl.BlockSpec((None, tr, c), lambda ll, i, chip: wmap(ll, i, chip)))] +
                    [(got, pl.BlockSpec((None, None, tr, c), lambda ll, i, chip, f=f: (f, ll, i, 0))) for f in range(nf)],
                    [(_sds((l, r, c)), pl.BlockSpec((None, tr, c), lambda ll, i, chip: (ll, i, 0)), None)],
                    ("parallel", "parallel"), prefetch=(chip,))[0]


def _adam_update(g, w, m, v):
    m_new = ADAM_B1 * m + (1.0 - ADAM_B1) * g
    v_new = ADAM_B2 * v + (1.0 - ADAM_B2) * jnp.square(g)
    m_hat = m_new / (1.0 - ADAM_B1 ** ADAM_STEP)
    v_hat = v_new / (1.0 - ADAM_B2 ** ADAM_STEP)
    delta = -ADAM_LR * (m_hat / (jnp.sqrt(v_hat) + ADAM_EPS) + ADAM_WD * w)
    return g, delta, m_new, v_new


def adamw_stacked(mine, theirs, w, m, v, core, name):
    L, r, c = w.shape
    hl = L // 2
    tr = _stack_tile(r, c)

    def body(core_ref, a_ref, b_ref, w_ref, m_ref, v_ref):
        is_mine = (pl.program_id(0) // hl) == core_ref[0]
        g = jnp.where(is_mine, a_ref[...], b_ref[...])
        return _adam_update(g, w_ref[...], m_ref[...], v_ref[...])

    full = pl.BlockSpec((None, tr, c), lambda l, i, core: (l, i, 0))
    out = (_sds((L, r, c)), full, None)
    return tilecall(body, name, (L, r // tr),
                    [(mine, pl.BlockSpec((None, tr, c), lambda l, i, core: (l % hl, i, 0))),
                     (theirs, pl.BlockSpec((None, None, tr, c), lambda l, i, core: (0, l % hl, i, 0))),
                     (w, full), (m, full), (v, full)],
                    [out, out, out, out], ("parallel", "parallel"), prefetch=(core,))


def _pack(arrs, dtype, row_multiple):
    flat = jnp.concatenate([a.reshape(-1).astype(dtype) for a in arrs])
    rows = -(-flat.shape[0] // LANES)
    rows = -(-rows // row_multiple) * row_multiple
    return jnp.pad(flat, (0, rows * LANES - flat.shape[0])).reshape(rows, LANES)


def _unpack(buf, shapes):
    flat = buf.reshape(-1)
    out, off = [], 0
    for s in shapes:
        n = int(np.prod(s))
        out.append(flat[off:off + n].reshape(s))
        off += n
    return out


def adamw_packed(gparts, w, m, v, name):
    P, R, _ = gparts.shape

    def body(g_ref, w_ref, m_ref, v_ref):
        g = g_ref[0]
        for p in range(1, P):
            g = g + g_ref[p]
        return _adam_update(g, w_ref[...], m_ref[...], v_ref[...])

    whole = pl.BlockSpec((R, LANES), lambda: (0, 0))
    out = (_sds((R, LANES)), whole, None)
    return tilecall(body, name, (), [(gparts, pl.BlockSpec((P, R, LANES), lambda: (0, 0, 0))), (w, whole), (m, whole), (v, whole)],
                    [out, out, out, out], ())


def _ffn_fwd(x, layer, gain, wts, cw8, cb):
    h = rms_fwd(x, gain, "rms_fwd")
    u = mm(h, win(wts["ffn_w_up"], layer), "nn", "mm_nn")
    a = ffn_act_fwd(u, cw8, cb, "ffn_act_fwd")
    y = mm(a, win(wts["ffn_w_down"], layer), "nn", "mm_nn_add", add=x)
    return y, (x, h, u, a)


def _ffn_bwd(dy, saved, layer, gain, wts, cw8, cb, grads):
    x, h, u, a = saved
    grads["ffn_w_down"] = mm(a, dy, "tn", "mm_tn_into", into=win(grads["ffn_w_down"], layer))
    da = mm(dy, win(wts["ffn_w_down"], layer), "nt", "mm_nt")
    dpre, d_cw8, d_cb = ffn_act_bwd(u, cw8, cb, da, "ffn_act_bwd")
    du = ffn_conv_bwd(dpre, cw8, "ffn_conv_bwd")
    grads["ffn_w_up"] = mm(h, du, "tn", "mm_tn_into", into=win(grads["ffn_w_up"], layer))
    dh = mm(du, win(wts["ffn_w_up"], layer), "nt", "mm_nt")
    dx, d_gain = rms_bwd(x, gain, dh, dy, "rms_bwd")
    return dx, dict(gain=d_gain, conv_w=d_cw8[0:3], conv_b=d_cb)


def _hgrn_w_in(wts, j, k):
    return win(wts["hgrn_w_in"], j, row_off=k * D_MODEL, rows=D_MODEL)


def _hgrn_layer_fwd(x, j, gain, wts, lb, out_gain, mall):
    h = rms_fwd(x, gain, "rms_fwd")
    z = [mm(h, _hgrn_w_in(wts, j, k), "nn", "mm_nn") for k in range(4)]
    o, states = hgrn_fwd(z[0], z[1], z[2], lb, mall, "hgrn_fwd")
    on = hgrn_out_fwd(o, z[3], out_gain, "hgrn_out_fwd")
    y = mm(on, win(wts["hgrn_w_out"], j), "nn", "mm_nn_add", add=x)
    return y, (x, h, z, o, states, on)


def _hgrn_layer_bwd(dy, saved, j, gain, wts, lb, out_gain, mall, grads):
    x, h, z, o, states, on = saved
    grads["hgrn_w_out"] = mm(on, dy, "tn", "mm_tn_into", into=win(grads["hgrn_w_out"], j))
    don = mm(dy, win(wts["hgrn_w_out"], j), "nt", "mm_nt")
    do, dzg, d_out_gain = hgrn_out_bwd(o, z[3], out_gain, don, "hgrn_out_bwd")
    dzq, dzf, dzi, dlb = hgrn_bwd(z[0], z[1], z[2], lb, mall, states, do, "hgrn_bwd")
    dz = [dzq, dzf, dzi, dzg]
    dh = None
    for k, d in enumerate(dz):
        grads["hgrn_w_in"] = mm(h, d, "tn", "mm_tn_into", into=win(grads["hgrn_w_in"], j, row_off=k * D_MODEL, rows=D_MODEL))
        dh = mm(d, _hgrn_w_in(wts, j, k), "nt", "mm_nt" if dh is None else "mm_nt_add", add=dh)
    dx, d_gain = rms_bwd(x, gain, dh, dy, "rms_bwd")
    return dx, dict(gain=d_gain, lb=dlb, out_gain=d_out_gain)


_MLA_IN_WINDOWS = ((0, MLA_LORA), (MLA_LORA, MLA_LORA), (2 * MLA_LORA, HEAD_DIM))


def _mla_layer_fwd(x, j, gain, wts, qa_gain, kva_gain, qn, kn, cos_t, sin_t):
    h = rms_fwd(x, gain, "rms_fwd")
    cq, ckv, kr = [mm(h, win(wts["mla_w_in"], j, col_off=c0, cols=n), "nn", "mm_nn") for c0, n in _MLA_IN_WINDOWS]
    cqn = rms_fwd(cq, qa_gain, "rms_fwd")
    ckvn = rms_fwd(ckv, kva_gain, "rms_fwd")
    qslots = mm(cqn, win(wts["mla_w_q_up"], j), "nn", "mm_nn")
    kv = mm(ckvn, win(wts["mla_w_kv_up"], j), "nn", "mm_nn")
    qr, krot = qk_fwd(qslots, kv, kr, qn, kn, cos_t, sin_t, "qk_fwd")
    o, lse = attn_fwd(qr, krot, kv, "attn_fwd")
    y = mm(o, win(wts["mla_w_out"], j), "nn", "mm_nn_add", add=x)
    return y, (x, h, cq, ckv, kr, cqn, ckvn, qslots, kv, qr, krot, o, lse)


def _mla_layer_bwd(dy, saved, j, gain, wts, qa_gain, kva_gain, qn, kn, cos_t, sin_t, grads):
    x, h, cq, ckv, kr, cqn, ckvn, qslots, kv, qr, krot, o, lse = saved
    grads["mla_w_out"] = mm(o, dy, "tn", "mm_tn_into", into=win(grads["mla_w_out"], j))
    do = mm(dy, win(wts["mla_w_out"], j), "nt", "mm_nt")
    dq, dk, dv = attn_bwd(qr, krot, kv, o, lse, do, "attn_bwd")
    dqslots, dkv, dkr, d_qn, d_kn = qk_bwd(qslots, kv, kr, qn, kn, cos_t, sin_t, dq, dk, dv, "qk_bwd")
    grads["mla_w_q_up"] = mm(cqn, dqslots, "tn", "mm_tn_into", into=win(grads["mla_w_q_up"], j))
    dcqn = mm(dqslots, win(wts["mla_w_q_up"], j), "nt", "mm_nt")
    grads["mla_w_kv_up"] = mm(ckvn, dkv, "tn", "mm_tn_into", into=win(grads["mla_w_kv_up"], j))
    dckvn = mm(dkv, win(wts["mla_w_kv_up"], j), "nt", "mm_nt")
    dcq, d_qa = rms_bwd(cq, qa_gain, dcqn, None, "rms_bwd")
    dckv, d_kva = rms_bwd(ckv, kva_gain, dckvn, None, "rms_bwd")
    dh = None
    for d, (c0, n) in zip((dcq, dckv, dkr), _MLA_IN_WINDOWS):
        grads["mla_w_in"] = mm(h, d, "tn", "mm_tn_into", into=win(grads["mla_w_in"], j, col_off=c0, cols=n))
        dh = mm(d, win(wts["mla_w_in"], j, col_off=c0, cols=n), "nt", "mm_nt" if dh is None else "mm_nt_add", add=dh)
    dx, d_gain = rms_bwd(x, gain, dh, dy, "rms_bwd")
    return dx, dict(gain=d_gain, qa=d_qa, kva=d_kva, qn=d_qn, kn=d_kn)


BIG = (("hgrn_w_in", "rows"), ("hgrn_w_out", "rows"), ("mla_w_in", "rows"), ("mla_w_q_up", "cols"),
       ("mla_w_kv_up", "cols"), ("mla_w_out", "rows"), ("ffn_w_up", "cols"), ("ffn_w_down", "rows"))
SMALL_SHARDED = (("mla_q_a_norm", 1), ("mla_kv_a_norm", 1), ("ffn_conv_w", 2))
REPLICATED = ("norm_mix", "norm_ffn", "hgrn_lower_bounds", "hgrn_out_norm", "mla_q_norm", "mla_k_norm", "ffn_conv_b")
WEIGHTS = ("norm_mix", "norm_ffn", "hgrn_w_in", "hgrn_lower_bounds", "hgrn_out_norm", "hgrn_w_out", "mla_w_in",
           "mla_q_a_norm", "mla_w_q_up", "mla_kv_a_norm", "mla_w_kv_up", "mla_q_norm", "mla_k_norm", "mla_w_out",
           "ffn_w_up", "ffn_conv_w", "ffn_conv_b", "ffn_w_down")


def _pad_cols(a, width):
    return jnp.pad(a, [(0, 0)] * (a.ndim - 1) + [(0, width - a.shape[-1])])


def _head_slots(w):
    lead, n = w.shape[:-1], w.shape[-1] // MLA_QK
    return _pad_cols(w.reshape(lead + (n, MLA_QK)), MLA_SLOT).reshape(lead + (n * MLA_SLOT,))


def _head_unslots(w):
    lead, n = w.shape[:-1], w.shape[-1] // MLA_SLOT
    return w.reshape(lead + (n, MLA_SLOT))[..., :MLA_QK].reshape(lead + (n * MLA_QK,))


def _to_stack_layout(name, a):
    if name == "hgrn_w_in":
        return a
    if name == "mla_w_in":
        return _pad_cols(a, MLA_IN_COLS)
    if name == "mla_w_q_up":
        return _head_slots(a)
    return a


def _from_stack_layout(name, a):
    if name == "mla_w_in":
        return a[..., :2 * MLA_LORA + MLA_ROPE]
    if name == "mla_w_q_up":
        return _head_unslots(a)
    return a


def _rope_tables(positions):
    inv_freq = ROPE_THETA ** (-jnp.arange(0, MLA_ROPE, 2, dtype=F32) / MLA_ROPE)
    ang = positions.astype(F32)[:, None] * inv_freq
    cos, sin = jnp.cos(ang), jnp.sin(ang)
    S = positions.shape[0]
    ones, zeros = jnp.ones((S, MLA_NOPE), F32), jnp.zeros((S, MLA_SLOT - MLA_QK), F32)
    return (jnp.concatenate([ones, cos, cos, zeros], axis=1),
            jnp.concatenate([jnp.zeros((S, MLA_NOPE), F32), -sin, sin, zeros], axis=1))


def kernel(x, positions, norm_mix, norm_ffn, hgrn_w_in, hgrn_lower_bounds, hgrn_out_norm, hgrn_w_out, mla_w_in, mla_q_a_norm, mla_w_q_up, mla_kv_a_norm, mla_w_kv_up, mla_q_norm, mla_k_norm, mla_w_out, ffn_w_up, ffn_conv_w, ffn_conv_b, ffn_w_down, loss_target, m_norm_mix, m_norm_ffn, m_hgrn_w_in, m_hgrn_lower_bounds, m_hgrn_out_norm, m_hgrn_w_out, m_mla_w_in, m_mla_q_a_norm, m_mla_w_q_up, m_mla_kv_a_norm, m_mla_w_kv_up, m_mla_q_norm, m_mla_k_norm, m_mla_w_out, m_ffn_w_up, m_ffn_conv_w, m_ffn_conv_b, m_ffn_w_down, v_norm_mix, v_norm_ffn, v_hgrn_w_in, v_hgrn_lower_bounds, v_hgrn_out_norm, v_hgrn_w_out, v_mla_w_in, v_mla_q_a_norm, v_mla_w_q_up, v_mla_kv_a_norm, v_mla_w_kv_up, v_mla_q_norm, v_mla_k_norm, v_mla_w_out, v_ffn_w_up, v_ffn_conv_w, v_ffn_conv_b, v_ffn_w_down):
    args = dict(locals())
    w = {n: args[n] for n in WEIGHTS}
    m = {n: args["m_" + n] for n in WEIGHTS}
    v = {n: args["v_" + n] for n in WEIGHTS}
    depth = norm_mix.shape[0]
    x0 = x[0]
    S = x0.shape[0]
    chip = (2 * lax.axis_index("x") + lax.axis_index("y")).astype(jnp.int32).reshape(1)
    core = lax.axis_index("c").astype(jnp.int32).reshape(1)
    big_names = [n for n, _ in BIG]
    kinds = [k for _, k in BIG]
    small_names = [n for n, _ in SMALL_SHARDED]
    small_axis = dict(SMALL_SHARDED)

    local = {n: _to_stack_layout(n, w[n]) for n in big_names}
    gathered = gather_big([local[n].astype(_MXU) for n in big_names], kinds, "gather_weights")
    wts = {n: place(g, local[n], k, chip, "place") for n, k, g in zip(big_names, kinds, gathered)}
    (got_small,) = exchange([_pack([w[n] for n in small_names], F32, SUBLANES)], "chips", [None], "gather_small")
    per_chip = [_unpack(got_small[p], [w[n].shape for n in small_names]) for p in range(N_CHIPS)]
    small = {n: jnp.concatenate([per_chip[p][k] for p in range(N_CHIPS)], axis=small_axis[n]) for k, n in enumerate(small_names)}

    cos_t, sin_t = _rope_tables(positions[0])
    lbs = lower_bound_fwd(hgrn_lower_bounds, "lower_bound_fwd")
    mall = jnp.asarray(_hgrn_sum_matrix(min(HGRN_CHUNK, S)), _MXU)
    qn = _pad_cols(mla_q_norm, MLA_SLOT)
    kn = _pad_cols(mla_k_norm, MLA_SLOT)
    cw8 = jnp.pad(small["ffn_conv_w"], ((0, 0), (0, SUBLANES - 3), (0, 0)))

    def mixer_args(layer):
        j = layer // 2
        if layer % 2 == 0:
            return (j, norm_mix[layer:layer + 1], wts, lbs[j:j + 1], hgrn_out_norm[j:j + 1], mall)
        return (j, norm_mix[layer:layer + 1], wts, small["mla_q_a_norm"][j:j + 1], small["mla_kv_a_norm"][j:j + 1],
                qn[j:j + 1], kn[j:j + 1], cos_t, sin_t)

    def ffn_args(layer):
        return (layer, norm_ffn[layer:layer + 1], wts, cw8[layer], ffn_conv_b[layer:layer + 1])

    h = x0
    saved = []
    for layer in range(depth):
        fwd = _hgrn_layer_fwd if layer % 2 == 0 else _mla_layer_fwd
        h, s_mix = fwd(h, *mixer_args(layer))
        h, s_ffn = _ffn_fwd(h, *ffn_args(layer))
        saved.append((s_mix, s_ffn))

    dh, loss_blk = loss_head(h, loss_target[0], "loss_head")
    loss = lax.psum(loss_blk[0, 0], MESH_AXES)

    grads = {n: jnp.zeros(g.shape, F32) for n, g in zip(big_names, gathered)}
    g_mix, g_ffn = [None] * depth, [None] * depth
    for layer in reversed(range(depth)):
        s_mix, s_ffn = saved[layer]
        dh, g_ffn[layer] = _ffn_bwd(dh, s_ffn, *ffn_args(layer), grads)
        bwd = _hgrn_layer_bwd if layer % 2 == 0 else _mla_layer_bwd
        dh, g_mix[layer] = bwd(dh, s_mix, *mixer_args(layer), grads)
    hg = [g_mix[l] for l in range(0, depth, 2)]
    mg = [g_mix[l] for l in range(1, depth, 2)]
    d_p = lower_bound_bwd(hgrn_lower_bounds, hg[1]["lb"], "lower_bound_bwd")

    g_list = [grads[n] for n in big_names]
    from_core = send_other_half(g_list, "reduce_cores_in")
    chip_sums = [add_cores(g, o, core, "add_cores") for g, o in zip(g_list, from_core)]
    from_chips = scatter_to_chips(chip_sums, kinds, "reduce_chips")
    reduced = [add_chips(own, got, k, chip, "add_chips") for own, got, k in zip(chip_sums, from_chips, kinds)]
    other_half = exchange(reduced, "sibling", [None] * len(reduced), "reduce_cores_out", keep_own=False)
    big_out = {}
    for n, mine, theirs in zip(big_names, reduced, other_half):
        outs = adamw_stacked(mine, theirs, local[n], _to_stack_layout(n, m[n]), _to_stack_layout(n, v[n]), core, "adamw")
        big_out[n] = [_from_stack_layout(n, o) for o in outs]

    small_grads = {
        "norm_mix": jnp.concatenate([g["gain"] for g in g_mix], axis=0),
        "norm_ffn": jnp.concatenate([g["gain"] for g in g_ffn], axis=0),
        "hgrn_lower_bounds": d_p[0:2],
        "hgrn_out_norm": jnp.concatenate([g["out_gain"] for g in hg], axis=0),
        "mla_q_a_norm": jnp.concatenate([g["qa"] for g in mg], axis=0),
        "mla_kv_a_norm": jnp.concatenate([g["kva"] for g in mg], axis=0),
        "mla_q_norm": jnp.concatenate([g["qn"][:, :MLA_QK] for g in mg], axis=0),
        "mla_k_norm": jnp.concatenate([g["kn"][:, :MLA_QK] for g in mg], axis=0),
        "ffn_conv_w": jnp.stack([g["conv_w"] for g in g_ffn]),
        "ffn_conv_b": jnp.concatenate([g["conv_b"] for g in g_ffn], axis=0),
    }

    def chip_part(n, p):
        size = w[n].shape[small_axis[n]]
        return lax.slice_in_dim(small_grads[n], p * size, (p + 1) * size, axis=small_axis[n])

    to_chips = jnp.stack([_pack([chip_part(n, p) for n in small_names], F32, SUBLANES) for p in range(N_CHIPS)])
    rep_g, shard_g = exchange([_pack([small_grads[n] for n in REPLICATED], F32, SUBLANES), to_chips], "all",
                              [None, (2, 1, 0)], "reduce_small")
    small_out = {}
    for names, gparts in ((REPLICATED, rep_g), (small_names, shard_g)):
        packed = adamw_packed(gparts, *[_pack([t[n] for n in names], F32, SUBLANES) for t in (w, m, v)], "adamw_small")
        unpacked = [_unpack(buf, [w[n].shape for n in names]) for buf in packed]
        for k, n in enumerate(names):
            small_out[n] = [u[k] for u in unpacked]

    result = [loss, dh[None]]
    for k in range(4):
        result += [(big_out[n] if n in big_out else small_out[n])[k] for n in WEIGHTS]
    return tuple(result)
```

```python
import numpy as np
import jax
import jax.numpy as jnp
from jax import lax
from jax.experimental import pallas as pl
from jax.experimental.pallas import tpu as pltpu

F32 = jnp.float32
BF16 = jnp.bfloat16
_MXU = BF16

RMS_EPS = 1e-6
D_MODEL = 1024
HEADS = 8
HEAD_DIM = 128
HGRN_CHUNK = 64
MLA_NOPE = 128
MLA_ROPE = 64
MLA_QK = MLA_NOPE + MLA_ROPE
MLA_SLOT = 256
MLA_LORA = 256
MLA_IN_COLS = 2 * MLA_LORA + HEAD_DIM
ROPE_THETA = 10000.0
D_FF = 2816
FF_BLOCK = 1408
LANES = 128
SUBLANES = 8

ADAM_LR = 0.001
ADAM_B1 = 0.9
ADAM_B2 = 0.999
ADAM_EPS = 1e-08
ADAM_WD = 0.01
ADAM_STEP = 10

VMEM_LIMIT = 56 * 1024 * 1024
MESH_AXES = ("x", "y", "c")
N_CHIPS = 4

_NN = ((1,), (0,))
_NT = ((1,), (1,))
_TN = ((0,), (0,))


def _dg(a, b, dims):
    return lax.dot_general(a.astype(_MXU), b.astype(_MXU), (dims, ((), ())), preferred_element_type=F32)


@jax.custom_vjp
def kdot(a, b):
    return _dg(a, b, _NN)


kdot.defvjp(lambda a, b: (_dg(a, b, _NN), (a, b)), lambda r, g: (_dg(g, r[1], _NT), _dg(r[0], g, _TN)))


@jax.custom_vjp
def kdot_nt(a, b):
    return _dg(a, b, _NT)


kdot_nt.defvjp(lambda a, b: (_dg(a, b, _NT), (a, b)), lambda r, g: (_dg(g, r[1], _NN), _dg(g, r[0], _TN)))


@jax.custom_vjp
def kdot_tn(a, b):
    return _dg(a, b, _TN)


kdot_tn.defvjp(lambda a, b: (_dg(a, b, _TN), (a, b)), lambda r, g: (_dg(r[1], g, _NT), _dg(r[0], g, _NN)))


def _pick(d, prefs):
    for p in prefs:
        if d >= p and d % p == 0:
            return p
    return d


def _params(sem):
    return pltpu.CompilerParams(dimension_semantics=sem, vmem_limit_bytes=VMEM_LIMIT)


def _sds(shape, dtype=F32):
    return jax.ShapeDtypeStruct(shape, dtype)


def win(arr, layer, row_off=0, col_off=0, rows=None, cols=None):
    return (arr, layer, row_off, col_off, rows or arr.shape[1] - row_off, cols or arr.shape[2] - col_off)


def mm(a, b, mode, name, add=None, out_dtype=F32, into=None):
    if isinstance(b, tuple):
        b_arr, b_layer, b_r0, b_c0, b_rows, b_cols = b
    else:
        b_arr, b_layer, b_r0, b_c0, (b_rows, b_cols) = b, None, 0, 0, b.shape
    if mode == "nn":
        (M, K), (K2, N) = a.shape, (b_rows, b_cols)
    elif mode == "nt":
        (M, K), (N, K2) = a.shape, (b_rows, b_cols)
    else:
        (K, M), (K2, N) = a.shape, (b_rows, b_cols)
    assert K == K2, (name, a.shape, b_rows, b_cols)
    tm = M if M <= 1024 else _pick(M, (1024, 1408, 512, 256, 128))
    tn = N if N <= 1024 else _pick(N, (1024, 1408, 512, 256, 128))
    tk = K if K <= 2048 else _pick(K, (2048, 2816, 1024, 512, 256, 128))
    nk = K // tk
    dims = {"nn": _NN, "nt": _NT, "tn": _TN}[mode]
    a_spec = pl.BlockSpec((tk, tm), lambda i, j, k: (k, i)) if mode == "tn" else pl.BlockSpec((tm, tk), lambda i, j, k: (i, k))
    b_blk = (tn, tk) if mode == "nt" else (tk, tn)
    assert b_r0 % b_blk[0] == 0 and b_c0 % b_blk[1] == 0, (name, b_r0, b_c0, b_blk)
    br, bc = b_r0 // b_blk[0], b_c0 // b_blk[1]
    if mode == "nt":
        b_idx = lambda i, j, k: (br + j, bc + k)
    else:
        b_idx = lambda i, j, k: (br + k, bc + j)
    if b_layer is None:
        b_spec = pl.BlockSpec(b_blk, b_idx)
    else:
        b_spec = pl.BlockSpec((None,) + b_blk, lambda i, j, k: (b_layer,) + b_idx(i, j, k))
    plain = pl.BlockSpec((tm, tn), lambda i, j, k: (i, j))
    has_add = add is not None
    ins = [a, b_arr] + ([add] if has_add else [])
    specs = [a_spec, b_spec] + ([plain] if has_add else [])
    aliases = {}
    if into is None:
        o_spec, out_shape = plain, _sds((M, N), out_dtype)
    else:
        buf, o_layer, o_r0, o_c0, o_rows, o_cols = into
        assert (o_rows, o_cols) == (M, N) and o_r0 % tm == 0 and o_c0 % tn == 0, (name, into[1:], M, N, tm, tn)
        orow, ocol = o_r0 // tm, o_c0 // tn
        o_spec = pl.BlockSpec((None, tm, tn), lambda i, j, k: (o_layer, orow + i, ocol + j))
        out_shape = _sds(buf.shape, buf.dtype)
        aliases = {len(ins): 0}
        ins.append(buf)
        specs.append(pl.BlockSpec(memory_space=pl.ANY))

    def kern(*refs):
        a_ref, b_ref = refs[0], refs[1]
        add_ref = refs[2] if has_add else None

        def finish(r, o_ref):
            if has_add:
                r = r + add_ref[...].astype(F32)
            o_ref[...] = r.astype(o_ref.dtype)

        if nk == 1:
            finish(_dg(a_ref[...], b_ref[...], dims), refs[-1])
            return
        o_ref, acc = refs[-2], refs[-1]
        k = pl.program_id(2)

        @pl.when(k == 0)
        def _():
            acc[...] = jnp.zeros_like(acc)

        acc[...] += _dg(a_ref[...], b_ref[...], dims)

        @pl.when(k == nk - 1)
        def _():
            finish(acc[...], o_ref)

    return pl.pallas_call(
        kern, name=name, grid=(M // tm, N // tn, nk), in_specs=specs, out_specs=o_spec, out_shape=out_shape,
        scratch_shapes=[pltpu.VMEM((tm, tn), F32)] if nk > 1 else [], input_output_aliases=aliases,
        compiler_params=_params(("parallel", "parallel", "arbitrary")))(*ins)


def _store(ref, val, first):
    if first is None:
        ref[...] = val.astype(ref.dtype)
        return

    @pl.when(first)
    def _():
        ref[...] = val.astype(ref.dtype)

    @pl.when(jnp.logical_not(first))
    def _():
        ref[...] += val.astype(ref.dtype)


def tilecall(body, name, grid, ins, outs, sem, prefetch=(), aliases=None):
    n_pre, n_in = len(prefetch), len(ins)

    def kern(*refs):
        vals = body(*refs[:n_pre + n_in])
        for ref, val, (_, _, first) in zip(refs[n_pre + n_in:], vals, outs):
            _store(ref, val, None if first is None else first())

    in_specs, out_specs = [s for _, s in ins], [s for _, s, _ in outs]
    kwargs = dict(name=name, out_shape=[sh for sh, _, _ in outs], compiler_params=_params(sem),
                  input_output_aliases={n_pre + k: v for k, v in (aliases or {}).items()})
    if n_pre:
        kwargs["grid_spec"] = pltpu.PrefetchScalarGridSpec(num_scalar_prefetch=n_pre, grid=grid, in_specs=in_specs,
                                                           out_specs=out_specs)
    else:
        kwargs.update(grid=grid, in_specs=in_specs, out_specs=out_specs)
    return pl.pallas_call(kern, **kwargs)(*prefetch, *[a for a, _ in ins])


def _rms(x, g, n):
    ms = jnp.sum(x * x, axis=-1, keepdims=True) / n
    return x * lax.rsqrt(ms + RMS_EPS) * g


def _row_tile(S, w):
    return min(S, 512 if w <= 1024 else 256)


def rms_fwd(x, g, name, col=0, w=None):
    S = x.shape[0]
    w = w or x.shape[1]
    ts = _row_tile(S, w)
    return tilecall(
        lambda x_ref, g_ref: (_rms(x_ref[...], g_ref[...], w),), name, (S // ts,),
        [(x, pl.BlockSpec((ts, w), lambda i: (i, col))), (g, pl.BlockSpec((1, w), lambda i: (0, 0)))],
        [(_sds((S, w), _MXU), pl.BlockSpec((ts, w), lambda i: (i, 0)), None)], ("parallel",))[0]


def rms_bwd(x, g, dh, res, name, w=None):
    S = x.shape[0]
    w = w or x.shape[1]
    ts = _row_tile(S, w)

    def body(x_ref, g_ref, dh_ref, *rest):
        _, vjp = jax.vjp(lambda xv, gv: _rms(xv, gv, w), x_ref[...], g_ref[...])
        dx, dg = vjp(dh_ref[...].astype(F32))
        if rest:
            dx = dx + rest[0][...]
        return dx, dg

    row = pl.BlockSpec((ts, w), lambda i: (i, 0))
    vec = pl.BlockSpec((1, w), lambda i: (0, 0))
    ins = [(x, row), (g, vec), (dh, row)] + ([(res, row)] if res is not None else [])
    return tilecall(body, name, (S // ts,), ins,
                    [(_sds((S, w)), row, None), (_sds((1, w)), vec, lambda: pl.program_id(0) == 0)], ("arbitrary",))


def _shifted(u, halo_ref, is_first):
    rid = lax.broadcasted_iota(jnp.int32, (u.shape[0], 1), 0)
    h7 = jnp.where(is_first, 0.0, halo_ref[7:8, :])
    h6 = jnp.where(is_first, 0.0, halo_ref[6:7, :])
    u1 = jnp.where(rid == 0, h7, pltpu.roll(u, 1, 0))
    u2 = jnp.where(rid == 0, h6, jnp.where(rid == 1, h7, pltpu.roll(u, 2, 0)))
    return u1, u2


def _conv(u, u1, u2, cw_ref, cb_ref):
    return ((cb_ref[...] + u2 * cw_ref[0:1, :]) + u1 * cw_ref[1:2, :]) + u * cw_ref[2:3, :]


def _ffn_specs(S, ts, jmap):
    hb = ts // SUBLANES
    return (pl.BlockSpec((ts, FF_BLOCK), lambda j, i: (i, jmap(j))),
            pl.BlockSpec((SUBLANES, FF_BLOCK), lambda j, i: (jnp.maximum(i * hb - 1, 0), jmap(j))),
            pl.BlockSpec((SUBLANES, FF_BLOCK), lambda j, i: (0, jmap(j))),
            pl.BlockSpec((1, FF_BLOCK), lambda j, i: (0, jmap(j))))


def ffn_act_fwd(u, cw8, cb, name):
    S = u.shape[0]
    ts = _row_tile(S, 2 * D_FF)
    nb = D_FF // FF_BLOCK

    def body(ug, hg, cwg, cbg, uu, hu, cwu, cbu):
        first = pl.program_id(1) == 0
        g = ug[...]
        g1, g2 = _shifted(g, hg, first)
        yg = _conv(g, g1, g2, cwg, cbg)
        v = uu[...]
        v1, v2 = _shifted(v, hu, first)
        yu = _conv(v, v1, v2, cwu, cbu)
        return (yg * jax.nn.sigmoid(yg) * yu,)

    sg = _ffn_specs(S, ts, lambda j: j)
    su = _ffn_specs(S, ts, lambda j: j + nb)
    ins = [(u, sg[0]), (u, sg[1]), (cw8, sg[2]), (cb, sg[3]), (u, su[0]), (u, su[1]), (cw8, su[2]), (cb, su[3])]
    return tilecall(body, name, (nb, S // ts), ins,
                    [(_sds((S, D_FF), _MXU), pl.BlockSpec((ts, FF_BLOCK), lambda j, i: (i, j)), None)],
                    ("parallel", "parallel"))[0]


def ffn_act_bwd(u, cw8, cb, da, name):
    S = u.shape[0]
    ts = _row_tile(S, 2 * D_FF)
    nb = D_FF // FF_BLOCK

    def body(um, hm, cwm, cbm, up, hp, cwp, cbp, da_ref):
        j = pl.program_id(0)
        first = pl.program_id(1) == 0
        a = um[...]
        a1, a2 = _shifted(a, hm, first)
        ym = _conv(a, a1, a2, cwm, cbm)
        b = up[...]
        b1, b2 = _shifted(b, hp, first)
        yp = _conv(b, b1, b2, cwp, cbp)
        d = da_ref[...]
        sm = jax.nn.sigmoid(ym)
        d_gate = d * yp * (sm * (1.0 + ym * (1.0 - sm)))
        d_up = d * (yp * jax.nn.sigmoid(yp))
        dy = jnp.where(j < nb, d_gate, d_up)
        dcw = jnp.concatenate(
            [jnp.sum(dy * a2, axis=0, keepdims=True), jnp.sum(dy * a1, axis=0, keepdims=True),
             jnp.sum(dy * a, axis=0, keepdims=True), jnp.zeros((SUBLANES - 3, dy.shape[1]), F32)], axis=0)
        return dy, dcw, jnp.sum(dy, axis=0, keepdims=True)

    sm_ = _ffn_specs(S, ts, lambda j: j)
    sp_ = _ffn_specs(S, ts, lambda j: (j + nb) % (2 * nb))
    ins = [(u, sm_[0]), (u, sm_[1]), (cw8, sm_[2]), (cb, sm_[3]), (u, sp_[0]), (u, sp_[1]), (cw8, sp_[2]), (cb, sp_[3]),
           (da, pl.BlockSpec((ts, FF_BLOCK), lambda j, i: (i, j % nb)))]
    first_row = lambda: pl.program_id(1) == 0
    return tilecall(body, name, (2 * nb, S // ts), ins,
                    [(_sds((S, 2 * D_FF)), sm_[0], None), (_sds((SUBLANES, 2 * D_FF)), sm_[2], first_row),
                     (_sds((1, 2 * D_FF)), sm_[3], first_row)], ("parallel", "arbitrary"))


def ffn_conv_bwd(dy, cw8, name):
    S = dy.shape[0]
    ts = _row_tile(S, 2 * D_FF)
    hb = ts // SUBLANES
    nrow = S // ts
    ncol = 2 * D_FF // FF_BLOCK

    def body(dy_ref, halo_ref, cw_ref):
        last = pl.program_id(1) == nrow - 1
        d = dy_ref[...]
        rid = lax.broadcasted_iota(jnp.int32, (ts, 1), 0)
        n0 = jnp.where(last, 0.0, halo_ref[0:1, :])
        n1 = jnp.where(last, 0.0, halo_ref[1:2, :])
        d1 = jnp.where(rid == ts - 1, n0, pltpu.roll(d, ts - 1, 0))
        d2 = jnp.where(rid == ts - 1, n1, jnp.where(rid == ts - 2, n0, pltpu.roll(d, ts - 2, 0)))
        return (d * cw_ref[2:3, :] + d1 * cw_ref[1:2, :] + d2 * cw_ref[0:1, :],)

    row = pl.BlockSpec((ts, FF_BLOCK), lambda j, i: (i, j))
    ins = [(dy, row), (dy, pl.BlockSpec((SUBLANES, FF_BLOCK), lambda j, i: (jnp.minimum((i + 1) * hb, S // SUBLANES - 1), j))),
           (cw8, pl.BlockSpec((SUBLANES, FF_BLOCK), lambda j, i: (0, j)))]
    return tilecall(body, name, (ncol, nrow), ins, [(_sds((S, 2 * D_FF), _MXU), row, None)], ("parallel", "parallel"))[0]


def _hgrn_levels(C):
    out, m = [], C // 2
    while m >= 1:
        out.append(m)
        m //= 2
    return out


def _hgrn_sum_matrix(C):
    t = np.arange(C)[:, None]
    u = np.arange(C)[None, :]
    blocks = [u <= t, u > t]
    for m in _hgrn_levels(C):
        r = (t // (2 * m)) * (2 * m) + m
        right = (t % (2 * m)) >= m
        blocks.append(right & (u > r) & (u <= t))
        blocks.append((~right) & (u > t) & (u <= r))
    return np.concatenate(blocks, axis=0).astype(np.float32)


def _make_partial_sums(nb, C):
    @jax.custom_vjp
    def sums(mall, lf):
        hi = lf.astype(_MXU)
        r1 = lf - hi.astype(F32)
        mid = r1.astype(_MXU)
        lo = (r1 - mid.astype(F32)).astype(_MXU)
        e = _dg(mall, hi, _NN) + _dg(mall, mid, _NN) + _dg(mall, lo, _NN)
        return tuple(e[b * C:(b + 1) * C] for b in range(nb))

    def fwd(mall, lf):
        return sums(mall, lf), mall

    def bwd(mall, gs):
        return jnp.zeros_like(mall), _dg(mall, jnp.concatenate(gs, axis=0), _TN)

    sums.defvjp(fwd, bwd)
    return sums


def _hgrn_chunk(zq, zf, v, lb, st, mall, C):
    levels = _hgrn_levels(C)
    qs = zq * jax.nn.sigmoid(zq)
    fg = lb + (1.0 - lb) * jax.nn.sigmoid(zf)
    k = 1.0 - fg
    e = _make_partial_sums(2 + 2 * len(levels), C)(mall, jnp.log(fg))
    g_incl, g_after = e[0], e[1]
    rid = lax.broadcasted_iota(jnp.int32, (C, 1), 0)
    tt = lax.broadcasted_iota(jnp.int32, (C, C), 0)
    ss = lax.broadcasted_iota(jnp.int32, (C, C), 1)
    o = kdot_nt(qs * jnp.exp(g_incl), st)
    o = o + jnp.sum(qs * k, axis=-1, keepdims=True) * v
    scores = jnp.zeros((C, C), F32)
    for li, m in enumerate(levels):
        sh = int(np.log2(m))
        right = ((rid >> sh) & 1) == 1
        qt = jnp.where(right, qs * jnp.exp(e[2 + 2 * li]), 0.0)
        kt = jnp.where(right, 0.0, k * jnp.exp(e[3 + 2 * li]))
        pair = ((tt >> (sh + 1)) == (ss >> (sh + 1))) & (((tt >> sh) & 1) == 1) & (((ss >> sh) & 1) == 0)
        scores = scores + jnp.where(pair, kdot_nt(qt, kt), 0.0)
    o = o + kdot(scores, v)
    g_last = jnp.sum(jnp.where(rid == C - 1, g_incl, 0.0), axis=0, keepdims=True)
    st_new = st * jnp.exp(g_last) + kdot_tn(v, k * jnp.exp(g_after))
    return o, st_new


HGRN_HEADS_PER_STEP = 2
_HGRN_LANES = HGRN_HEADS_PER_STEP * HEAD_DIM


def _hgrn_in_specs(C, nc, rev):
    cm = (lambda c: nc - 1 - c) if rev else (lambda c: c)
    blk = lambda: pl.BlockSpec((C, _HGRN_LANES), lambda h, c: (cm(c), h))
    return cm, [blk(), blk(), blk(), pl.BlockSpec((1, _HGRN_LANES), lambda h, c: (0, h))]


def _hgrn_state_spec(cm):
    return pl.BlockSpec((HGRN_HEADS_PER_STEP, None, HEAD_DIM, HEAD_DIM), lambda h, c: (h, cm(c), 0, 0))


def hgrn_fwd(zq, zf, zi, lb, mall, name):
    S = zq.shape[0]
    C = min(HGRN_CHUNK, S)
    nc = S // C

    def kern(zq_ref, zf_ref, zi_ref, lb_ref, mall_ref, o_ref, st_ref, st):
        @pl.when(pl.program_id(1) == 0)
        def _():
            st[...] = jnp.zeros_like(st)

        mall_v = mall_ref[...]
        for g in range(HGRN_HEADS_PER_STEP):
            lanes = slice(g * HEAD_DIM, (g + 1) * HEAD_DIM)
            s_in = st[g]
            st_ref[g] = s_in
            o, s_new = _hgrn_chunk(zq_ref[:, lanes], zf_ref[:, lanes], zi_ref[:, lanes], lb_ref[:, lanes], s_in, mall_v, C)
            o_ref[:, lanes] = o
            st[g] = s_new

    cm, specs = _hgrn_in_specs(C, nc, False)
    return pl.pallas_call(
        kern, name=name, grid=(HEADS // HGRN_HEADS_PER_STEP, nc),
        in_specs=specs + [pl.BlockSpec(mall.shape, lambda h, c: (0, 0))],
        out_specs=[pl.BlockSpec((C, _HGRN_LANES), lambda h, c: (c, h)), _hgrn_state_spec(cm)],
        out_shape=[_sds((S, D_MODEL)), _sds((HEADS, nc, HEAD_DIM, HEAD_DIM))],
        scratch_shapes=[pltpu.VMEM((HGRN_HEADS_PER_STEP, HEAD_DIM, HEAD_DIM), F32)],
        compiler_params=_params(("parallel", "arbitrary")))(zq, zf, zi, lb, mall)


def hgrn_bwd(zq, zf, zi, lb, mall, states, do, name):
    S = zq.shape[0]
    C = min(HGRN_CHUNK, S)
    nc = S // C

    def kern(zq_ref, zf_ref, zi_ref, lb_ref, mall_ref, st_ref, do_ref, dq_ref, df_ref, di_ref, dlb_ref, dst):
        first = pl.program_id(1) == 0

        @pl.when(first)
        def _():
            dst[...] = jnp.zeros_like(dst)

        mall_v = mall_ref[...]
        gls = []
        for g in range(HGRN_HEADS_PER_STEP):
            lanes = slice(g * HEAD_DIM, (g + 1) * HEAD_DIM)
            _, vjp = jax.vjp(lambda a, b, c, d, e: _hgrn_chunk(a, b, c, d, e, mall_v, C),
                             zq_ref[:, lanes], zf_ref[:, lanes], zi_ref[:, lanes], lb_ref[:, lanes], st_ref[g])
            ga, gb, gv, gl, gs = vjp((do_ref[:, lanes], dst[g]))
            dq_ref[:, lanes] = ga.astype(dq_ref.dtype)
            df_ref[:, lanes] = gb.astype(df_ref.dtype)
            di_ref[:, lanes] = gv.astype(di_ref.dtype)
            gls.append(gl)
            dst[g] = gs
        _store(dlb_ref, jnp.concatenate(gls, axis=1), first)

    cm, specs = _hgrn_in_specs(C, nc, True)
    row = lambda: pl.BlockSpec((C, _HGRN_LANES), lambda h, c: (cm(c), h))
    return pl.pallas_call(
        kern, name=name, grid=(HEADS // HGRN_HEADS_PER_STEP, nc),
        in_specs=specs + [pl.BlockSpec(mall.shape, lambda h, c: (0, 0)), _hgrn_state_spec(cm), row()],
        out_specs=[row(), row(), row(), pl.BlockSpec((1, _HGRN_LANES), lambda h, c: (0, h))],
        out_shape=[_sds((S, D_MODEL), _MXU), _sds((S, D_MODEL), _MXU), _sds((S, D_MODEL), _MXU), _sds((1, D_MODEL))],
        scratch_shapes=[pltpu.VMEM((HGRN_HEADS_PER_STEP, HEAD_DIM, HEAD_DIM), F32)],
        compiler_params=_params(("parallel", "arbitrary")))(zq, zf, zi, lb, mall, states, do)


def _hgrn_out(o, g, gain):
    return _rms(o, gain, HEAD_DIM) * (g * jax.nn.sigmoid(g))


def hgrn_out_fwd(o, zg, gain, name):
    S = o.shape[0]
    ts = _row_tile(S, D_MODEL)
    blk = pl.BlockSpec((ts, HEAD_DIM), lambda i, h: (i, h))
    return tilecall(lambda o_ref, g_ref, w_ref: (_hgrn_out(o_ref[...], g_ref[...], w_ref[...]),), name, (S // ts, HEADS),
                    [(o, blk), (zg, blk), (gain, pl.BlockSpec((1, HEAD_DIM), lambda i, h: (0, 0)))],
                    [(_sds((S, D_MODEL), _MXU), blk, None)], ("parallel", "parallel"))[0]


def hgrn_out_bwd(o, zg, gain, don, name):
    S = o.shape[0]
    ts = _row_tile(S, D_MODEL)
    blk = pl.BlockSpec((ts, HEAD_DIM), lambda i, h: (i, h))
    vec = pl.BlockSpec((1, HEAD_DIM), lambda i, h: (0, 0))

    def body(o_ref, g_ref, w_ref, d_ref):
        _, vjp = jax.vjp(_hgrn_out, o_ref[...], g_ref[...], w_ref[...])
        return vjp(d_ref[...])

    return tilecall(body, name, (S // ts, HEADS), [(o, blk), (zg, blk), (gain, vec), (don, blk)],
                    [(_sds((S, D_MODEL)), blk, None), (_sds((S, D_MODEL), _MXU), blk, None),
                     (_sds((1, HEAD_DIM)), vec, lambda: (pl.program_id(0) == 0) & (pl.program_id(1) == 0))],
                    ("arbitrary", "arbitrary"))


def _lb_soft(p0, p1):
    mx = jnp.maximum(p0, p1)
    e0, e1 = jnp.exp(p0 - mx), jnp.exp(p1 - mx)
    s0, s1 = e0 / (e0 + e1), e1 / (e0 + e1)
    return (s0 + s1) - s0


def lower_bound_fwd(p, name):
    assert p.shape[0] == 2

    def body(p_ref):
        s = _lb_soft(p_ref[0:1, :], p_ref[1:2, :])
        return (jnp.concatenate([jnp.zeros_like(s), s] + [jnp.zeros_like(s)] * (SUBLANES - 2), axis=0),)

    spec8 = pl.BlockSpec((SUBLANES, p.shape[1]), lambda: (0, 0))
    return tilecall(body, name, (), [(p, pl.BlockSpec(p.shape, lambda: (0, 0)))], [(_sds((SUBLANES, p.shape[1])), spec8, None)], ())[0]


def lower_bound_bwd(p, dlb1, name):
    def body(p_ref, d_ref):
        _, vjp = jax.vjp(_lb_soft, p_ref[0:1, :], p_ref[1:2, :])
        g0, g1 = vjp(d_ref[...])
        return (jnp.concatenate([g0, g1] + [jnp.zeros_like(g0)] * (SUBLANES - 2), axis=0),)

    spec8 = pl.BlockSpec((SUBLANES, p.shape[1]), lambda: (0, 0))
    return tilecall(body, name, (), [(p, pl.BlockSpec(p.shape, lambda: (0, 0))), (dlb1, pl.BlockSpec(dlb1.shape, lambda: (0, 0)))],
                    [(_sds((SUBLANES, p.shape[1])), spec8, None)], ())[0]


@jax.custom_vjp
def _swap_rope_halves(x):
    lane = lax.broadcasted_iota(jnp.int32, x.shape, 1)
    lo = (lane >= MLA_NOPE) & (lane < MLA_NOPE + MLA_ROPE // 2)
    hi = (lane >= MLA_NOPE + MLA_ROPE // 2) & (lane < MLA_QK)
    return jnp.where(lo, pltpu.roll(x, MLA_SLOT - MLA_ROPE // 2, 1), jnp.where(hi, pltpu.roll(x, MLA_ROPE // 2, 1), 0.0))


_swap_rope_halves.defvjp(lambda x: (_swap_rope_halves(x), None), lambda _, g: (_swap_rope_halves(g),))


def _norm_rope(x, gain, cos_t, sin_t):
    y = _rms(x, gain, MLA_QK)
    return y * cos_t + _swap_rope_halves(y) * sin_t


def _qk_heads(qs, kn, kr, qn, kn_gain, cos_t, sin_t):
    q = _norm_rope(qs, qn, cos_t, sin_t)
    k = _norm_rope(jnp.concatenate([kn, kr], axis=1), kn_gain, cos_t, sin_t)
    return q, k


def _qk_specs(ts):
    slot = pl.BlockSpec((ts, MLA_SLOT), lambda i, h: (i, h))
    nope = pl.BlockSpec((ts, HEAD_DIM), lambda i, h: (i, 2 * h))
    shared = pl.BlockSpec((ts, HEAD_DIM), lambda i, h: (i, 0))
    gain = pl.BlockSpec((1, MLA_SLOT), lambda i, h: (0, 0))
    table = pl.BlockSpec((ts, MLA_SLOT), lambda i, h: (i, 0))
    return slot, nope, shared, gain, table


def qk_fwd(qslots, kv, krope, qn, kn, cos_t, sin_t, name):
    S = qslots.shape[0]
    ts = _row_tile(S, D_MODEL)
    slot, nope, shared, gain, table = _qk_specs(ts)

    def body(q_ref, kn_ref, kr_ref, qn_ref, kg_ref, c_ref, s_ref):
        return _qk_heads(q_ref[...], kn_ref[...], kr_ref[...], qn_ref[...], kg_ref[...], c_ref[...], s_ref[...])

    out = _sds((S, HEADS * MLA_SLOT), _MXU)
    return tilecall(body, name, (S // ts, HEADS),
                    [(qslots, slot), (kv, nope), (krope, shared), (qn, gain), (kn, gain), (cos_t, table), (sin_t, table)],
                    [(out, slot, None), (out, slot, None)], ("parallel", "parallel"))


def qk_bwd(qslots, kv, krope, qn, kn, cos_t, sin_t, dq, dk, dv, name):
    S = qslots.shape[0]
    ts = _row_tile(S, D_MODEL)
    slot, nope, shared, gain, table = _qk_specs(ts)
    vblk = pl.BlockSpec((ts, HEAD_DIM), lambda i, h: (i, h))

    def body(q_ref, kn_ref, kr_ref, qn_ref, kg_ref, c_ref, s_ref, dq_ref, dk_ref, dv_ref):
        c, s = c_ref[...], s_ref[...]
        _, vjp = jax.vjp(lambda a, b, r, g1, g2: _qk_heads(a, b, r, g1, g2, c, s),
                         q_ref[...], kn_ref[...], kr_ref[...], qn_ref[...], kg_ref[...])
        ga, gb, gr, g1, g2 = vjp((dq_ref[...], dk_ref[...]))
        return ga, jnp.concatenate([gb, dv_ref[...]], axis=1), gr, g1, g2

    first_head = lambda: pl.program_id(1) == 0
    first = lambda: (pl.program_id(0) == 0) & (pl.program_id(1) == 0)
    wide = _sds((S, HEADS * MLA_SLOT), _MXU)
    return tilecall(body, name, (S // ts, HEADS),
                    [(qslots, slot), (kv, nope), (krope, shared), (qn, gain), (kn, gain), (cos_t, table), (sin_t, table),
                     (dq, slot), (dk, slot), (dv, vblk)],
                    [(wide, slot, None), (wide, slot, None), (_sds((S, HEAD_DIM)), shared, first_head),
                     (_sds((1, MLA_SLOT)), gain, first), (_sds((1, MLA_SLOT)), gain, first)], ("arbitrary", "arbitrary"))


_ATTN_SCALE = MLA_QK ** -0.5
ATTN_TILE = 512
ATTN_HEADS_PER_STEP = 2


def _causal_pairs(nq, by_row):
    pairs = [(i, j) for i in range(nq) for j in range(i + 1)] if by_row else [(i, j) for j in range(nq) for i in range(j, nq)]
    return jnp.asarray([p[0] for p in pairs], jnp.int32), jnp.asarray([p[1] for p in pairs], jnp.int32)


def _diag_mask(s, tq):
    rows = lax.broadcasted_iota(jnp.int32, (tq, tq), 0)
    cols = lax.broadcasted_iota(jnp.int32, (tq, tq), 1)
    return jnp.where(rows >= cols, s, -jnp.inf)


def attn_fwd(qr, kr, kv, name):
    S = qr.shape[0]
    tq = min(S, ATTN_TILE)
    nq = S // tq
    i_tab, j_tab = _causal_pairs(nq, True)

    G = ATTN_HEADS_PER_STEP
    reps = tq // LANES

    def kern(it, jt, q_ref, k_ref, kv_ref, o_ref, lse_ref, m_s, l_s, acc):
        n = pl.program_id(1)
        i, j = it[n], jt[n]

        @pl.when(j == 0)
        def _():
            m_s[...] = jnp.full_like(m_s, -jnp.inf)
            l_s[...] = jnp.zeros_like(l_s)
            acc[...] = jnp.zeros_like(acc)

        def step(diagonal):
            for g in range(G):
                slot = slice(g * MLA_SLOT, (g + 1) * MLA_SLOT)
                s = _dg(q_ref[:, slot], k_ref[:, slot], _NT) * _ATTN_SCALE
                if diagonal:
                    s = _diag_mask(s, tq)
                m_prev = m_s[g]
                m_new = jnp.maximum(m_prev, jnp.max(s, axis=-1, keepdims=True))
                alpha = jnp.exp(m_prev - m_new)
                p = jnp.exp(s - jnp.tile(m_new, (1, reps)))
                l_s[g] = alpha * l_s[g] + jnp.sum(p, axis=-1, keepdims=True)
                acc[g] = alpha * acc[g] + _dg(p, kv_ref[:, g * MLA_SLOT + HEAD_DIM:(g + 1) * MLA_SLOT], _NN)
                m_s[g] = m_new

        @pl.when(j < i)
        def _():
            step(False)

        @pl.when(j == i)
        def _():
            step(True)
            for g in range(G):
                l = l_s[g]
                lanes = slice(g * HEAD_DIM, (g + 1) * HEAD_DIM)
                o_ref[:, lanes] = acc[g] / l
                lse_ref[:, lanes] = m_s[g] + jnp.log(l)

    out = _sds((S, HEADS * HEAD_DIM))
    oblk = pl.BlockSpec((tq, G * HEAD_DIM), lambda h, n, it, jt: (it[n], h))
    stat = pltpu.VMEM((G, tq, HEAD_DIM), F32)
    return pl.pallas_call(
        kern, name=name,
        grid_spec=pltpu.PrefetchScalarGridSpec(
            num_scalar_prefetch=2, grid=(HEADS // G, i_tab.shape[0]),
            in_specs=[pl.BlockSpec((tq, G * MLA_SLOT), lambda h, n, it, jt: (it[n], h)),
                      pl.BlockSpec((tq, G * MLA_SLOT), lambda h, n, it, jt: (jt[n], h)),
                      pl.BlockSpec((tq, G * MLA_SLOT), lambda h, n, it, jt: (jt[n], h))],
            out_specs=[oblk, oblk], scratch_shapes=[stat, stat, stat]),
        out_shape=[out, out], compiler_params=_params(("parallel", "arbitrary")))(i_tab, j_tab, qr, kr, kv)


def attn_bwd(qr, kr, kv, o, lse, do, name):
    S = qr.shape[0]
    tq = min(S, ATTN_TILE)
    nq = S // tq
    i_tab, j_tab = _causal_pairs(nq, False)

    def kern(it, jt, q_ref, k_ref, v_ref, o_ref, lse_ref, do_ref, dq_ref, dk_ref, dv_ref, dk_acc, dv_acc):
        n = pl.program_id(1)
        i, j = it[n], jt[n]

        @pl.when(n == 0)
        def _():
            dq_ref[...] = jnp.zeros_like(dq_ref)

        @pl.when(i == j)
        def _():
            dk_acc[...] = jnp.zeros_like(dk_acc)
            dv_acc[...] = jnp.zeros_like(dv_acc)

        def step(diagonal):
            s = _dg(q_ref[...], k_ref[...], _NT) * _ATTN_SCALE - jnp.tile(lse_ref[...], (1, tq // LANES))
            if diagonal:
                s = _diag_mask(s, tq)
            p = jnp.exp(s)
            g = do_ref[...]
            delta = jnp.sum(g * o_ref[...], axis=-1, keepdims=True)
            dv_acc[...] += _dg(p, g, _TN)
            ds = p * (_dg(g, v_ref[...], _NT) - delta) * _ATTN_SCALE
            dk_acc[...] += _dg(ds, q_ref[...], _TN)
            rows = pl.ds(pl.multiple_of(i * tq, tq), tq)
            dq_ref[rows, :] += _dg(ds, k_ref[...], _NN)

        @pl.when(i > j)
        def _():
            step(False)

        @pl.when(i == j)
        def _():
            step(True)

        @pl.when(i == nq - 1)
        def _():
            dk_ref[...] = dk_acc[...]
            dv_ref[...] = dv_acc[...]

    qblk = pl.BlockSpec((tq, MLA_SLOT), lambda h, n, it, jt: (it[n], h))
    oblk = pl.BlockSpec((tq, HEAD_DIM), lambda h, n, it, jt: (it[n], h))
    kblk = pl.BlockSpec((tq, MLA_SLOT), lambda h, n, it, jt: (jt[n], h))
    return pl.pallas_call(
        kern, name=name,
        grid_spec=pltpu.PrefetchScalarGridSpec(
            num_scalar_prefetch=2, grid=(HEADS, i_tab.shape[0]),
            in_specs=[qblk, kblk, pl.BlockSpec((tq, HEAD_DIM), lambda h, n, it, jt: (jt[n], 2 * h + 1)), oblk, oblk, oblk],
            out_specs=[pl.BlockSpec((S, MLA_SLOT), lambda h, n, it, jt: (0, h)), kblk,
                       pl.BlockSpec((tq, HEAD_DIM), lambda h, n, it, jt: (jt[n], h))],
            scratch_shapes=[pltpu.VMEM((tq, MLA_SLOT), F32), pltpu.VMEM((tq, HEAD_DIM), F32)]),
        out_shape=[_sds((S, HEADS * MLA_SLOT)), _sds((S, HEADS * MLA_SLOT)), _sds((S, HEADS * HEAD_DIM))],
        compiler_params=_params(("parallel", "arbitrary")))(i_tab, j_tab, qr, kr, kv, o, lse, do)


def loss_head(y, target, name):
    S, Dm = y.shape
    ts = _row_tile(S, Dm)

    def body(y_ref, t_ref):
        e = y_ref[...] - t_ref[...]
        tot = jnp.sum(jnp.sum(e * e, axis=-1, keepdims=True) / Dm, axis=0, keepdims=True)
        return e / Dm, jnp.broadcast_to(0.5 * tot, (SUBLANES, LANES))

    row = pl.BlockSpec((ts, Dm), lambda i: (i, 0))
    return tilecall(body, name, (S // ts,), [(y, row), (target, row)],
                    [(_sds((S, Dm)), row, None),
                     (_sds((SUBLANES, LANES)), pl.BlockSpec((SUBLANES, LANES), lambda i: (0, 0)), lambda: pl.program_id(0) == 0)],
                    ("arbitrary",))


_CHIP_FLIPS = ((1, 0), (0, 1), (1, 1))
_PEER_FLIPS = {
    "chips": ((1, 0, 0), (0, 1, 0), (1, 1, 0)),
    "sibling": ((0, 0, 1),),
    "all": tuple((a, b, c) for a in (0, 1) for b in (0, 1) for c in (0, 1))[1:],
}
_SLOT_WEIGHTS = {"chips": (2, 1, 0), "sibling": (0, 0, 1), "all": (4, 2, 1)}
_HBM = pl.BlockSpec(memory_space=pltpu.HBM)


def _me():
    return lax.axis_index("x"), lax.axis_index("y"), lax.axis_index("c")


def _remote(src, dst, send_sem, recv_sem, peer):
    return pltpu.make_async_remote_copy(src_ref=src, dst_ref=dst, send_sem=send_sem, recv_sem=recv_sem,
                                        device_id=peer, device_id_type=pl.DeviceIdType.MESH)


def exchange(arrs, group, slab_weights, name, keep_own=True):
    flips = _PEER_FLIPS[group]
    wx, wy, wc = _SLOT_WEIGHTS[group]
    n_slots = len(flips) + (1 if keep_own else 0)
    n = len(arrs)

    def slab(ref, a, pos):
        w = slab_weights[a]
        return ref if w is None else ref.at[w[0] * pos[0] + w[1] * pos[1] + w[2] * pos[2]]

    def kern(*refs):
        srcs, outs = refs[:n], refs[n:2 * n]
        send_sems, recv_sems = refs[2 * n:2 * n + 2]
        me = _me()
        my_slot = wx * me[0] + wy * me[1] + wc * me[2]
        copies = []
        if keep_own:
            local_sems = refs[2 * n + 2]
            for a in range(n):
                cp = pltpu.make_async_copy(slab(srcs[a], a, me), outs[a].at[my_slot], local_sems.at[a])
                cp.start()
                copies.append(cp)
        for f, flip in enumerate(flips):
            peer = tuple(m ^ b if b else m for m, b in zip(me, flip))
            for a in range(n):
                cp = _remote(slab(srcs[a], a, peer), outs[a].at[my_slot if keep_own else f],
                             send_sems.at[f, a], recv_sems.at[f, a], peer)
                cp.start()
                copies.append(cp)
        for cp in copies:
            cp.wait()

    out_shape = [_sds((n_slots,) + (a.shape if slab_weights[k] is None else a.shape[1:]), a.dtype) for k, a in enumerate(arrs)]
    sems = [pltpu.SemaphoreType.DMA((len(flips), n)), pltpu.SemaphoreType.DMA((len(flips), n))]
    return pl.pallas_call(
        kern, name=name, in_specs=[_HBM] * n, out_specs=[_HBM] * n, out_shape=out_shape,
        scratch_shapes=sems + ([pltpu.SemaphoreType.DMA((n,))] if keep_own else []))(*arrs)


def _chip_window(ref, kind, size, chip, layers):
    if kind == "rows":
        return ref.at[layers, pl.ds(chip * size, size), :]
    return ref.at[layers, :, pl.ds(pl.multiple_of(chip * size, LANES), size)]


def gather_big(shards, kinds, name):
    n = len(shards)
    fulls = []
    for s, kind in zip(shards, kinds):
        L, r, c = s.shape
        fulls.append(_sds((L, N_CHIPS * r, c) if kind == "rows" else (L, r, N_CHIPS * c), s.dtype))

    def kern(*refs):
        srcs, outs = refs[:n], refs[n:2 * n]
        ici_s, ici_r, d2d_s, d2d_r = refs[2 * n:]
        x, y, c = _me()
        my_chip = 2 * x + y
        ici, fwd = [], []
        for a in range(n):
            L, r, cc = shards[a].shape
            size = r if kinds[a] == "rows" else cc
            mine = pl.ds(c * (L // 2), L // 2)
            for f, (fx, fy) in enumerate(_CHIP_FLIPS):
                cp = _remote(srcs[a].at[mine], _chip_window(outs[a], kinds[a], size, my_chip, mine),
                             ici_s.at[a, f], ici_r.at[a, f], (x ^ fx, y ^ fy, c))
                cp.start()
                ici.append(cp)
        for a in range(n):
            L, r, cc = shards[a].shape
            size = r if kinds[a] == "rows" else cc
            mine = pl.ds(c * (L // 2), L // 2)
            for f, (fx, fy) in enumerate(_CHIP_FLIPS):
                ici[a * len(_CHIP_FLIPS) + f].wait_recv()
                landed = _chip_window(outs[a], kinds[a], size, 2 * (x ^ fx) + (y ^ fy), mine)
                cp = _remote(landed, landed, d2d_s.at[a, f], d2d_r.at[a, f], (x, y, 1 - c))
                cp.start()
                fwd.append(cp)
        for cp in ici:
            cp.wait_send()
        for cp in fwd:
            cp.wait()

    sem = pltpu.SemaphoreType.DMA((n, len(_CHIP_FLIPS)))
    return pl.pallas_call(kern, name=name, in_specs=[_HBM] * n, out_specs=[_HBM] * n, out_shape=fulls,
                          scratch_shapes=[sem, sem, sem, sem])(*shards)


def send_other_half(arrs, name):
    n = len(arrs)

    def kern(*refs):
        srcs, outs = refs[:n], refs[n:2 * n]
        send_sems, recv_sems = refs[2 * n:]
        x, y, c = _me()
        copies = []
        for a in range(n):
            hl = arrs[a].shape[0] // 2
            cp = _remote(srcs[a].at[pl.ds((1 - c) * hl, hl)], outs[a], send_sems.at[a], recv_sems.at[a], (x, y, 1 - c))
            cp.start()
            copies.append(cp)
        for cp in copies:
            cp.wait()

    return pl.pallas_call(
        kern, name=name, in_specs=[_HBM] * n, out_specs=[_HBM] * n,
        out_shape=[_sds((a.shape[0] // 2,) + a.shape[1:], a.dtype) for a in arrs],
        scratch_shapes=[pltpu.SemaphoreType.DMA((n,)), pltpu.SemaphoreType.DMA((n,))])(*arrs)


def scatter_to_chips(arrs, kinds, name):
    n = len(arrs)
    shapes = []
    for a, kind in zip(arrs, kinds):
        l, R, C = a.shape
        shapes.append((l, R // N_CHIPS, C) if kind == "rows" else (l, R, C // N_CHIPS))

    def kern(*refs):
        srcs, outs = refs[:n], refs[n:2 * n]
        send_sems, recv_sems = refs[2 * n:]
        x, y, c = _me()
        copies = []
        for a in range(n):
            size = shapes[a][1] if kinds[a] == "rows" else shapes[a][2]
            for f, (fx, fy) in enumerate(_CHIP_FLIPS):
                window = _chip_window(srcs[a], kinds[a], size, 2 * (x ^ fx) + (y ^ fy), slice(None))
                cp = _remote(window, outs[a].at[f], send_sems.at[a, f], recv_sems.at[a, f], (x ^ fx, y ^ fy, c))
                cp.start()
                copies.append(cp)
        for cp in copies:
            cp.wait()

    sem = pltpu.SemaphoreType.DMA((n, len(_CHIP_FLIPS)))
    return pl.pallas_call(
        kern, name=name, in_specs=[_HBM] * n, out_specs=[_HBM] * n,
        out_shape=[_sds((len(_CHIP_FLIPS),) + s, a.dtype) for s, a in zip(shapes, arrs)],
        scratch_shapes=[sem, sem])(*arrs)


def _stack_tile(r, c):
    for t in (1024, 704, 512, 352, 256, 128, 64, 32, 16):
        if r % t == 0 and t * c * 4 <= 3 * 512 * 1024:
            return t
    return r


def _window_map(kind, r, tr):
    nrt = r // tr
    if kind == "rows":
        return lambda l, i, chip: (l, chip[0] * nrt + i, 0)
    return lambda l, i, chip: (l, i, chip[0])


def place(full, shard, kind, chip, name):
    L, r, c = shard.shape
    tr = _stack_tile(r, c)
    wmap = _window_map(kind, r, tr)
    return tilecall(lambda chip_ref, s_ref, f_ref: (s_ref[...],), name, (L, r // tr),
                    [(shard, pl.BlockSpec((None, tr, c), lambda l, i, chip: (l, i, 0))), (full, pl.BlockSpec(memory_space=pl.ANY))],
                    [(_sds(full.shape, full.dtype), pl.BlockSpec((None, tr, c), lambda l, i, chip: wmap(l, i, chip)), None)],
                    ("parallel", "parallel"), prefetch=(chip,), aliases={1: 0})[0]


def add_cores(g, other, core, name):
    L, R, C = g.shape
    hl = L // 2
    tr = _stack_tile(R, C)
    blk = (None, tr, C)
    return tilecall(lambda core_ref, a_ref, b_ref: (a_ref[...] + b_ref[...],), name, (hl, R // tr),
                    [(g, pl.BlockSpec(blk, lambda l, i, core: (core[0] * hl + l, i, 0))),
                     (other, pl.BlockSpec(blk, lambda l, i, core: (l, i, 0)))],
                    [(_sds((hl, R, C), _MXU), pl.BlockSpec(blk, lambda l, i, core: (l, i, 0)), None)],
                    ("parallel", "parallel"), prefetch=(core,))[0]


def add_chips(own, got, kind, chip, name):
    nf, l, r, c = got.shape
    tr = _stack_tile(r, c)
    wmap = _window_map(kind, r, tr)

    def body(chip_ref, own_ref, *got_refs):
        acc = own_ref[...].astype(F32)
        for ref in got_refs:
            acc = acc + ref[...].astype(F32)
        return (acc,)

    return tilecall(body, name, (l, r // tr),
                    [(own, pl.BlockSpec((None, tr, c), lambda ll, i, chip: wmap(ll, i, chip)))] +
                    [(got, pl.BlockSpec((None, None, tr, c), lambda ll, i, chip, f=f: (f, ll, i, 0))) for f in range(nf)],
                    [(_sds((l, r, c)), pl.BlockSpec((None, tr, c), lambda ll, i, chip: (ll, i, 0)), None)],
                    ("parallel", "parallel"), prefetch=(chip,))[0]


def _adam_update(g, w, m, v):
    m_new = ADAM_B1 * m + (1.0 - ADAM_B1) * g
    v_new = ADAM_B2 * v + (1.0 - ADAM_B2) * jnp.square(g)
    m_hat = m_new / (1.0 - ADAM_B1 ** ADAM_STEP)
    v_hat = v_new / (1.0 - ADAM_B2 ** ADAM_STEP)
    delta = -ADAM_LR * (m_hat / (jnp.sqrt(v_hat) + ADAM_EPS) + ADAM_WD * w)
    return g, delta, m_new, v_new


def adamw_stacked(mine, theirs, w, m, v, core, name):
    L, r, c = w.shape
    hl = L // 2
    tr = _stack_tile(r, c)

    def body(core_ref, a_ref, b_ref, w_ref, m_ref, v_ref):
        is_mine = (pl.program_id(0) // hl) == core_ref[0]
        g = jnp.where(is_mine, a_ref[...], b_ref[...])
        return _adam_update(g, w_ref[...], m_ref[...], v_ref[...])

    full = pl.BlockSpec((None, tr, c), lambda l, i, core: (l, i, 0))
    out = (_sds((L, r, c)), full, None)
    return tilecall(body, name, (L, r // tr),
                    [(mine, pl.BlockSpec((None, tr, c), lambda l, i, core: (l % hl, i, 0))),
                     (theirs, pl.BlockSpec((None, None, tr, c), lambda l, i, core: (0, l % hl, i, 0))),
                     (w, full), (m, full), (v, full)],
                    [out, out, out, out], ("parallel", "parallel"), prefetch=(core,))


def _pack(arrs, dtype, row_multiple):
    flat = jnp.concatenate([a.reshape(-1).astype(dtype) for a in arrs])
    rows = -(-flat.shape[0] // LANES)
    rows = -(-rows // row_multiple) * row_multiple
    return jnp.pad(flat, (0, rows * LANES - flat.shape[0])).reshape(rows, LANES)


def _unpack(buf, shapes):
    flat = buf.reshape(-1)
    out, off = [], 0
    for s in shapes:
        n = int(np.prod(s))
        out.append(flat[off:off + n].reshape(s))
        off += n
    return out


def adamw_packed(gparts, w, m, v, name):
    P, R, _ = gparts.shape

    def body(g_ref, w_ref, m_ref, v_ref):
        g = g_ref[0]
        for p in range(1, P):
            g = g + g_ref[p]
        return _adam_update(g, w_ref[...], m_ref[...], v_ref[...])

    whole = pl.BlockSpec((R, LANES), lambda: (0, 0))
    out = (_sds((R, LANES)), whole, None)
    return tilecall(body, name, (), [(gparts, pl.BlockSpec((P, R, LANES), lambda: (0, 0, 0))), (w, whole), (m, whole), (v, whole)],
                    [out, out, out, out], ())


def _ffn_fwd(x, layer, gain, wts, cw8, cb):
    h = rms_fwd(x, gain, "rms_fwd")
    u = mm(h, win(wts["ffn_w_up"], layer), "nn", "mm_nn")
    a = ffn_act_fwd(u, cw8, cb, "ffn_act_fwd")
    y = mm(a, win(wts["ffn_w_down"], layer), "nn", "mm_nn_add", add=x)
    return y, (x, h, u, a)


def _ffn_bwd(dy, saved, layer, gain, wts, cw8, cb, grads):
    x, h, u, a = saved
    grads["ffn_w_down"] = mm(a, dy, "tn", "mm_tn_into", into=win(grads["ffn_w_down"], layer))
    da = mm(dy, win(wts["ffn_w_down"], layer), "nt", "mm_nt")
    dpre, d_cw8, d_cb = ffn_act_bwd(u, cw8, cb, da, "ffn_act_bwd")
    du = ffn_conv_bwd(dpre, cw8, "ffn_conv_bwd")
    grads["ffn_w_up"] = mm(h, du, "tn", "mm_tn_into", into=win(grads["ffn_w_up"], layer))
    dh = mm(du, win(wts["ffn_w_up"], layer), "nt", "mm_nt")
    dx, d_gain = rms_bwd(x, gain, dh, dy, "rms_bwd")
    return dx, dict(gain=d_gain, conv_w=d_cw8[0:3], conv_b=d_cb)


def _hgrn_w_in(wts, j, k):
    return win(wts["hgrn_w_in"], j, row_off=k * D_MODEL, rows=D_MODEL)


def _hgrn_layer_fwd(x, j, gain, wts, lb, out_gain, mall):
    h = rms_fwd(x, gain, "rms_fwd")
    z = [mm(h, _hgrn_w_in(wts, j, k), "nn", "mm_nn") for k in range(4)]
    o, states = hgrn_fwd(z[0], z[1], z[2], lb, mall, "hgrn_fwd")
    on = hgrn_out_fwd(o, z[3], out_gain, "hgrn_out_fwd")
    y = mm(on, win(wts["hgrn_w_out"], j), "nn", "mm_nn_add", add=x)
    return y, (x, h, z, o, states, on)


def _hgrn_layer_bwd(dy, saved, j, gain, wts, lb, out_gain, mall, grads):
    x, h, z, o, states, on = saved
    grads["hgrn_w_out"] = mm(on, dy, "tn", "mm_tn_into", into=win(grads["hgrn_w_out"], j))
    don = mm(dy, win(wts["hgrn_w_out"], j), "nt", "mm_nt")
    do, dzg, d_out_gain = hgrn_out_bwd(o, z[3], out_gain, don, "hgrn_out_bwd")
    dzq, dzf, dzi, dlb = hgrn_bwd(z[0], z[1], z[2], lb, mall, states, do, "hgrn_bwd")
    dz = [dzq, dzf, dzi, dzg]
    dh = None
    for k, d in enumerate(dz):
        grads["hgrn_w_in"] = mm(h, d, "tn", "mm_tn_into", into=win(grads["hgrn_w_in"], j, row_off=k * D_MODEL, rows=D_MODEL))
        dh = mm(d, _hgrn_w_in(wts, j, k), "nt", "mm_nt" if dh is None else "mm_nt_add", add=dh)
    dx, d_gain = rms_bwd(x, gain, dh, dy, "rms_bwd")
    return dx, dict(gain=d_gain, lb=dlb, out_gain=d_out_gain)


_MLA_IN_WINDOWS = ((0, MLA_LORA), (MLA_LORA, MLA_LORA), (2 * MLA_LORA, HEAD_DIM))


def _mla_layer_fwd(x, j, gain, wts, qa_gain, kva_gain, qn, kn, cos_t, sin_t):
    h = rms_fwd(x, gain, "rms_fwd")
    cq, ckv, kr = [mm(h, win(wts["mla_w_in"], j, col_off=c0, cols=n), "nn", "mm_nn") for c0, n in _MLA_IN_WINDOWS]
    cqn = rms_fwd(cq, qa_gain, "rms_fwd")
    ckvn = rms_fwd(ckv, kva_gain, "rms_fwd")
    qslots = mm(cqn, win(wts["mla_w_q_up"], j), "nn", "mm_nn")
    kv = mm(ckvn, win(wts["mla_w_kv_up"], j), "nn", "mm_nn")
    qr, krot = qk_fwd(qslots, kv, kr, qn, kn, cos_t, sin_t, "qk_fwd")
    o, lse = attn_fwd(qr, krot, kv, "attn_fwd")
    y = mm(o, win(wts["mla_w_out"], j), "nn", "mm_nn_add", add=x)
    return y, (x, h, cq, ckv, kr, cqn, ckvn, qslots, kv, qr, krot, o, lse)


def _mla_layer_bwd(dy, saved, j, gain, wts, qa_gain, kva_gain, qn, kn, cos_t, sin_t, grads):
    x, h, cq, ckv, kr, cqn, ckvn, qslots, kv, qr, krot, o, lse = saved
    grads["mla_w_out"] = mm(o, dy, "tn", "mm_tn_into", into=win(grads["mla_w_out"], j))
    do = mm(dy, win(wts["mla_w_out"], j), "nt", "mm_nt")
    dq, dk, dv = attn_bwd(qr, krot, kv, o, lse, do, "attn_bwd")
    dqslots, dkv, dkr, d_qn, d_kn = qk_bwd(qslots, kv, kr, qn, kn, cos_t, sin_t, dq, dk, dv, "qk_bwd")
    grads["mla_w_q_up"] = mm(cqn, dqslots, "tn", "mm_tn_into", into=win(grads["mla_w_q_up"], j))
    dcqn = mm(dqslots, win(wts["mla_w_q_up"], j), "nt", "mm_nt")
    grads["mla_w_kv_up"] = mm(ckvn, dkv, "tn", "mm_tn_into", into=win(grads["mla_w_kv_up"], j))
    dckvn = mm(dkv, win(wts["mla_w_kv_up"], j), "nt", "mm_nt")
    dcq, d_qa = rms_bwd(cq, qa_gain, dcqn, None, "rms_bwd")
    dckv, d_kva = rms_bwd(ckv, kva_gain, dckvn, None, "rms_bwd")
    dh = None
    for d, (c0, n) in zip((dcq, dckv, dkr), _MLA_IN_WINDOWS):
        grads["mla_w_in"] = mm(h, d, "tn", "mm_tn_into", into=win(grads["mla_w_in"], j, col_off=c0, cols=n))
        dh = mm(d, win(wts["mla_w_in"], j, col_off=c0, cols=n), "nt", "mm_nt" if dh is None else "mm_nt_add", add=dh)
    dx, d_gain = rms_bwd(x, gain, dh, dy, "rms_bwd")
    return dx, dict(gain=d_gain, qa=d_qa, kva=d_kva, qn=d_qn, kn=d_kn)


BIG = (("hgrn_w_in", "rows"), ("hgrn_w_out", "rows"), ("mla_w_in", "rows"), ("mla_w_q_up", "cols"),
       ("mla_w_kv_up", "cols"), ("mla_w_out", "rows"), ("ffn_w_up", "cols"), ("ffn_w_down", "rows"))
SMALL_SHARDED = (("mla_q_a_norm", 1), ("mla_kv_a_norm", 1), ("ffn_conv_w", 2))
REPLICATED = ("norm_mix", "norm_ffn", "hgrn_lower_bounds", "hgrn_out_norm", "mla_q_norm", "mla_k_norm", "ffn_conv_b")
WEIGHTS = ("norm_mix", "norm_ffn", "hgrn_w_in", "hgrn_lower_bounds", "hgrn_out_norm", "hgrn_w_out", "mla_w_in",
           "mla_q_a_norm", "mla_w_q_up", "mla_kv_a_norm", "mla_w_kv_up", "mla_q_norm", "mla_k_norm", "mla_w_out",
           "ffn_w_up", "ffn_conv_w", "ffn_conv_b", "ffn_w_down")


def _pad_cols(a, width):
    return jnp.pad(a, [(0, 0)] * (a.ndim - 1) + [(0, width - a.shape[-1])])


def _head_slots(w):
    lead, n = w.shape[:-1], w.shape[-1] // MLA_QK
    return _pad_cols(w.reshape(lead + (n, MLA_QK)), MLA_SLOT).reshape(lead + (n * MLA_SLOT,))


def _head_unslots(w):
    lead, n = w.shape[:-1], w.shape[-1] // MLA_SLOT
    return w.reshape(lead + (n, MLA_SLOT))[..., :MLA_QK].reshape(lead + (n * MLA_QK,))


def _to_stack_layout(name, a):
    if name == "hgrn_w_in":
        return a
    if name == "mla_w_in":
        return _pad_cols(a, MLA_IN_COLS)
    if name == "mla_w_q_up":
        return _head_slots(a)
    return a


def _from_stack_layout(name, a):
    if name == "mla_w_in":
        return a[..., :2 * MLA_LORA + MLA_ROPE]
    if name == "mla_w_q_up":
        return _head_unslots(a)
    return a


def _rope_tables(positions):
    inv_freq = ROPE_THETA ** (-jnp.arange(0, MLA_ROPE, 2, dtype=F32) / MLA_ROPE)
    ang = positions.astype(F32)[:, None] * inv_freq
    cos, sin = jnp.cos(ang), jnp.sin(ang)
    S = positions.shape[0]
    ones, zeros = jnp.ones((S, MLA_NOPE), F32), jnp.zeros((S, MLA_SLOT - MLA_QK), F32)
    return (jnp.concatenate([ones, cos, cos, zeros], axis=1),
            jnp.concatenate([jnp.zeros((S, MLA_NOPE), F32), -sin, sin, zeros], axis=1))


def kernel(x, positions, norm_mix, norm_ffn, hgrn_w_in, hgrn_lower_bounds, hgrn_out_norm, hgrn_w_out, mla_w_in, mla_q_a_norm, mla_w_q_up, mla_kv_a_norm, mla_w_kv_up, mla_q_norm, mla_k_norm, mla_w_out, ffn_w_up, ffn_conv_w, ffn_conv_b, ffn_w_down, loss_target, m_norm_mix, m_norm_ffn, m_hgrn_w_in, m_hgrn_lower_bounds, m_hgrn_out_norm, m_hgrn_w_out, m_mla_w_in, m_mla_q_a_norm, m_mla_w_q_up, m_mla_kv_a_norm, m_mla_w_kv_up, m_mla_q_norm, m_mla_k_norm, m_mla_w_out, m_ffn_w_up, m_ffn_conv_w, m_ffn_conv_b, m_ffn_w_down, v_norm_mix, v_norm_ffn, v_hgrn_w_in, v_hgrn_lower_bounds, v_hgrn_out_norm, v_hgrn_w_out, v_mla_w_in, v_mla_q_a_norm, v_mla_w_q_up, v_mla_kv_a_norm, v_mla_w_kv_up, v_mla_q_norm, v_mla_k_norm, v_mla_w_out, v_ffn_w_up, v_ffn_conv_w, v_ffn_conv_b, v_ffn_w_down):
    args = dict(locals())
    w = {n: args[n] for n in WEIGHTS}
    m = {n: args["m_" + n] for n in WEIGHTS}
    v = {n: args["v_" + n] for n in WEIGHTS}
    depth = norm_mix.shape[0]
    x0 = x[0]
    S = x0.shape[0]
    chip = (2 * lax.axis_index("x") + lax.axis_index("y")).astype(jnp.int32).reshape(1)
    core = lax.axis_index("c").astype(jnp.int32).reshape(1)
    big_names = [n for n, _ in BIG]
    kinds = [k for _, k in BIG]
    small_names = [n for n, _ in SMALL_SHARDED]
    small_axis = dict(SMALL_SHARDED)

    local = {n: _to_stack_layout(n, w[n]) for n in big_names}
    gathered = gather_big([local[n].astype(_MXU) for n in big_names], kinds, "gather_weights")
    wts = {n: place(g, local[n], k, chip, "place") for n, k, g in zip(big_names, kinds, gathered)}
    (got_small,) = exchange([_pack([w[n] for n in small_names], F32, SUBLANES)], "chips", [None], "gather_small")
    per_chip = [_unpack(got_small[p], [w[n].shape for n in small_names]) for p in range(N_CHIPS)]
    small = {n: jnp.concatenate([per_chip[p][k] for p in range(N_CHIPS)], axis=small_axis[n]) for k, n in enumerate(small_names)}

    cos_t, sin_t = _rope_tables(positions[0])
    lbs = lower_bound_fwd(hgrn_lower_bounds, "lower_bound_fwd")
    mall = jnp.asarray(_hgrn_sum_matrix(min(HGRN_CHUNK, S)), _MXU)
    qn = _pad_cols(mla_q_norm, MLA_SLOT)
    kn = _pad_cols(mla_k_norm, MLA_SLOT)
    cw8 = jnp.pad(small["ffn_conv_w"], ((0, 0), (0, SUBLANES - 3), (0, 0)))

    def mixer_args(layer):
        j = layer // 2
        if layer % 2 == 0:
            return (j, norm_mix[layer:layer + 1], wts, lbs[j:j + 1], hgrn_out_norm[j:j + 1], mall)
        return (j, norm_mix[layer:layer + 1], wts, small["mla_q_a_norm"][j:j + 1], small["mla_kv_a_norm"][j:j + 1],
                qn[j:j + 1], kn[j:j + 1], cos_t, sin_t)

    def ffn_args(layer):
        return (layer, norm_ffn[layer:layer + 1], wts, cw8[layer], ffn_conv_b[layer:layer + 1])

    h = x0
    saved = []
    for layer in range(depth):
        fwd = _hgrn_layer_fwd if layer % 2 == 0 else _mla_layer_fwd
        h, s_mix = fwd(h, *mixer_args(layer))
        h, s_ffn = _ffn_fwd(h, *ffn_args(layer))
        saved.append((s_mix, s_ffn))

    dh, loss_blk = loss_head(h, loss_target[0], "loss_head")
    loss = lax.psum(loss_blk[0, 0], MESH_AXES)

    grads = {n: jnp.zeros(g.shape, F32) for n, g in zip(big_names, gathered)}
    g_mix, g_ffn = [None] * depth, [None] * depth
    for layer in reversed(range(depth)):
        s_mix, s_ffn = saved[layer]
        dh, g_ffn[layer] = _ffn_bwd(dh, s_ffn, *ffn_args(layer), grads)
        bwd = _hgrn_layer_bwd if layer % 2 == 0 else _mla_layer_bwd
        dh, g_mix[layer] = bwd(dh, s_mix, *mixer_args(layer), grads)
    hg = [g_mix[l] for l in range(0, depth, 2)]
    mg = [g_mix[l] for l in range(1, depth, 2)]
    d_p = lower_bound_bwd(hgrn_lower_bounds, hg[1]["lb"], "lower_bound_bwd")

    g_list = [grads[n] for n in big_names]
    from_core = send_other_half(g_list, "reduce_cores_in")
    chip_sums = [add_cores(g, o, core, "add_cores") for g, o in zip(g_list, from_core)]
    from_chips = scatter_to_chips(chip_sums, kinds, "reduce_chips")
    reduced = [add_chips(own, got, k, chip, "add_chips") for own, got, k in zip(chip_sums, from_chips, kinds)]
    other_half = exchange(reduced, "sibling", [None] * len(reduced), "reduce_cores_out", keep_own=False)
    big_out = {}
    for n, mine, theirs in zip(big_names, reduced, other_half):
        outs = adamw_stacked(mine, theirs, local[n], _to_stack_layout(n, m[n]), _to_stack_layout(n, v[n]), core, "adamw")
        big_out[n] = [_from_stack_layout(n, o) for o in outs]

    small_grads = {
        "norm_mix": jnp.concatenate([g["gain"] for g in g_mix], axis=0),
        "norm_ffn": jnp.concatenate([g["gain"] for g in g_ffn], axis=0),
        "hgrn_lower_bounds": d_p[0:2],
        "hgrn_out_norm": jnp.concatenate([g["out_gain"] for g in hg], axis=0),
        "mla_q_a_norm": jnp.concatenate([g["qa"] for g in mg], axis=0),
        "mla_kv_a_norm": jnp.concatenate([g["kva"] for g in mg], axis=0),
        "mla_q_norm": jnp.concatenate([g["qn"][:, :MLA_QK] for g in mg], axis=0),
        "mla_k_norm": jnp.concatenate([g["kn"][:, :MLA_QK] for g in mg], axis=0),
        "ffn_conv_w": jnp.stack([g["conv_w"] for g in g_ffn]),
        "ffn_conv_b": jnp.concatenate([g["conv_b"] for g in g_ffn], axis=0),
    }

    def chip_part(n, p):
        size = w[n].shape[small_axis[n]]
        return lax.slice_in_dim(small_grads[n], p * size, (p + 1) * size, axis=small_axis[n])

    to_chips = jnp.stack([_pack([chip_part(n, p) for n in small_names], F32, SUBLANES) for p in range(N_CHIPS)])
    rep_g, shard_g = exchange([_pack([small_grads[n] for n in REPLICATED], F32, SUBLANES), to_chips], "all",
                              [None, (2, 1, 0)], "reduce_small")
    small_out = {}
    for names, gparts in ((REPLICATED, rep_g), (small_names, shard_g)):
        packed = adamw_packed(gparts, *[_pack([t[n] for n in names], F32, SUBLANES) for t in (w, m, v)], "adamw_small")
        unpacked = [_unpack(buf, [w[n].shape for n in names]) for buf in packed]
        for k, n in enumerate(names):
            small_out[n] = [u[k] for u in unpacked]

    result = [loss, dh[None]]
    for k in range(4):
        result += [(big_out[n] if n in big_out else small_out[n])[k] for n in WEIGHTS]
    return tuple(result)
```

```python
import numpy as np
import jax
import jax.numpy as jnp
from jax import lax
from jax.experimental import pallas as pl
from jax.experimental.pallas import tpu as pltpu

F32 = jnp.float32
BF16 = jnp.bfloat16
_MXU = BF16

RMS_EPS = 1e-6
D_MODEL = 1024
HEADS = 8
HEAD_DIM = 128
HGRN_CHUNK = 64
MLA_NOPE = 128
MLA_ROPE = 64
MLA_QK = MLA_NOPE + MLA_ROPE
MLA_SLOT = 256
MLA_LORA = 256
MLA_IN_COLS = 2 * MLA_LORA + HEAD_DIM
ROPE_THETA = 10000.0
D_FF = 2816
FF_BLOCK = 1408
LANES = 128
SUBLANES = 8

ADAM_LR = 0.001
ADAM_B1 = 0.9
ADAM_B2 = 0.999
ADAM_EPS = 1e-08
ADAM_WD = 0.01
ADAM_STEP = 10

VMEM_LIMIT = 56 * 1024 * 1024
MESH_AXES = ("x", "y", "c")
N_CHIPS = 4

_NN = ((1,), (0,))
_NT = ((1,), (1,))
_TN = ((0,), (0,))


def _dg(a, b, dims):
    return lax.dot_general(a.astype(_MXU), b.astype(_MXU), (dims, ((), ())), preferred_element_type=F32)


@jax.custom_vjp
def kdot(a, b):
    return _dg(a, b, _NN)


kdot.defvjp(lambda a, b: (_dg(a, b, _NN), (a, b)), lambda r, g: (_dg(g, r[1], _NT), _dg(r[0], g, _TN)))


@jax.custom_vjp
def kdot_nt(a, b):
    return _dg(a, b, _NT)


kdot_nt.defvjp(lambda a, b: (_dg(a, b, _NT), (a, b)), lambda r, g: (_dg(g, r[1], _NN), _dg(g, r[0], _TN)))


@jax.custom_vjp
def kdot_tn(a, b):
    return _dg(a, b, _TN)


kdot_tn.defvjp(lambda a, b: (_dg(a, b, _TN), (a, b)), lambda r, g: (_dg(r[1], g, _NT), _dg(r[0], g, _NN)))


def _pick(d, prefs):
    for p in prefs:
        if d >= p and d % p == 0:
            return p
    return d


def _params(sem):
    return pltpu.CompilerParams(dimension_semantics=sem, vmem_limit_bytes=VMEM_LIMIT)


def _sds(shape, dtype=F32):
    return jax.ShapeDtypeStruct(shape, dtype)


def win(arr, layer, row_off=0, col_off=0, rows=None, cols=None):
    return (arr, layer, row_off, col_off, rows or arr.shape[1] - row_off, cols or arr.shape[2] - col_off)


def mm(a, b, mode, name, add=None, out_dtype=F32, into=None):
    if isinstance(b, tuple):
        b_arr, b_layer, b_r0, b_c0, b_rows, b_cols = b
    else:
        b_arr, b_layer, b_r0, b_c0, (b_rows, b_cols) = b, None, 0, 0, b.shape
    if mode == "nn":
        (M, K), (K2, N) = a.shape, (b_rows, b_cols)
    elif mode == "nt":
        (M, K), (N, K2) = a.shape, (b_rows, b_cols)
    else:
        (K, M), (K2, N) = a.shape, (b_rows, b_cols)
    assert K == K2, (name, a.shape, b_rows, b_cols)
    tm = M if M <= 1024 else _pick(M, (1024, 1408, 512, 256, 128))
    tn = N if N <= 1024 else _pick(N, (1024, 1408, 512, 256, 128))
    tk = K if K <= 2048 else _pick(K, (2048, 2816, 1024, 512, 256, 128))
    nk = K // tk
    dims = {"nn": _NN, "nt": _NT, "tn": _TN}[mode]
    a_spec = pl.BlockSpec((tk, tm), lambda i, j, k: (k, i)) if mode == "tn" else pl.BlockSpec((tm, tk), lambda i, j, k: (i, k))
    b_blk = (tn, tk) if mode == "nt" else (tk, tn)
    assert b_r0 % b_blk[0] == 0 and b_c0 % b_blk[1] == 0, (name, b_r0, b_c0, b_blk)
    br, bc = b_r0 // b_blk[0], b_c0 // b_blk[1]
    if mode == "nt":
        b_idx = lambda i, j, k: (br + j, bc + k)
    else:
        b_idx = lambda i, j, k: (br + k, bc + j)
    if b_layer is None:
        b_spec = pl.BlockSpec(b_blk, b_idx)
    else:
        b_spec = pl.BlockSpec((None,) + b_blk, lambda i, j, k: (b_layer,) + b_idx(i, j, k))
    plain = pl.BlockSpec((tm, tn), lambda i, j, k: (i, j))
    has_add = add is not None
    ins = [a, b_arr] + ([add] if has_add else [])
    specs = [a_spec, b_spec] + ([plain] if has_add else [])
    aliases = {}
    if into is None:
        o_spec, out_shape = plain, _sds((M, N), out_dtype)
    else:
        buf, o_layer, o_r0, o_c0, o_rows, o_cols = into
        assert (o_rows, o_cols) == (M, N) and o_r0 % tm == 0 and o_c0 % tn == 0, (name, into[1:], M, N, tm, tn)
        orow, ocol = o_r0 // tm, o_c0 // tn
        o_spec = pl.BlockSpec((None, tm, tn), lambda i, j, k: (o_layer, orow + i, ocol + j))
        out_shape = _sds(buf.shape, buf.dtype)
        aliases = {len(ins): 0}
        ins.append(buf)
        specs.append(pl.BlockSpec(memory_space=pl.ANY))

    def kern(*refs):
        a_ref, b_ref = refs[0], refs[1]
        add_ref = refs[2] if has_add else None

        def finish(r, o_ref):
            if has_add:
                r = r + add_ref[...].astype(F32)
            o_ref[...] = r.astype(o_ref.dtype)

        if nk == 1:
            finish(_dg(a_ref[...], b_ref[...], dims), refs[-1])
            return
        o_ref, acc = refs[-2], refs[-1]
        k = pl.program_id(2)

        @pl.when(k == 0)
        def _():
            acc[...] = jnp.zeros_like(acc)

        acc[...] += _dg(a_ref[...], b_ref[...], dims)

        @pl.when(k == nk - 1)
        def _():
            finish(acc[...], o_ref)

    return pl.pallas_call(
        kern, name=name, grid=(M // tm, N // tn, nk), in_specs=specs, out_specs=o_spec, out_shape=out_shape,
        scratch_shapes=[pltpu.VMEM((tm, tn), F32)] if nk > 1 else [], input_output_aliases=aliases,
        compiler_params=_params(("parallel", "parallel", "arbitrary")))(*ins)


def _store(ref, val, first):
    if first is None:
        ref[...] = val.astype(ref.dtype)
        return

    @pl.when(first)
    def _():
        ref[...] = val.astype(ref.dtype)

    @pl.when(jnp.logical_not(first))
    def _():
        ref[...] += val.astype(ref.dtype)


def tilecall(body, name, grid, ins, outs, sem, prefetch=(), aliases=None):
    n_pre, n_in = len(prefetch), len(ins)

    def kern(*refs):
        vals = body(*refs[:n_pre + n_in])
        for ref, val, (_, _, first) in zip(refs[n_pre + n_in:], vals, outs):
            _store(ref, val, None if first is None else first())

    in_specs, out_specs = [s for _, s in ins], [s for _, s, _ in outs]
    kwargs = dict(name=name, out_shape=[sh for sh, _, _ in outs], compiler_params=_params(sem),
                  input_output_aliases={n_pre + k: v for k, v in (aliases or {}).items()})
    if n_pre:
        kwargs["grid_spec"] = pltpu.PrefetchScalarGridSpec(num_scalar_prefetch=n_pre, grid=grid, in_specs=in_specs,
                                                           out_specs=out_specs)
    else:
        kwargs.update(grid=grid, in_specs=in_specs, out_specs=out_specs)
    return pl.pallas_call(kern, **kwargs)(*prefetch, *[a for a, _ in ins])


def _rms(x, g, n):
    ms = jnp.sum(x * x, axis=-1, keepdims=True) / n
    return x * lax.rsqrt(ms + RMS_EPS) * g


def _row_tile(S, w):
    return min(S, 512 if w <= 1024 else 256)


def rms_fwd(x, g, name, col=0, w=None):
    S = x.shape[0]
    w = w or x.shape[1]
    ts = _row_tile(S, w)
    return tilecall(
        lambda x_ref, g_ref: (_rms(x_ref[...], g_ref[...], w),), name, (S // ts,),
        [(x, pl.BlockSpec((ts, w), lambda i: (i, col))), (g, pl.BlockSpec((1, w), lambda i: (0, 0)))],
        [(_sds((S, w), _MXU), pl.BlockSpec((ts, w), lambda i: (i, 0)), None)], ("parallel",))[0]


def rms_bwd(x, g, dh, res, name, w=None):
    S = x.shape[0]
    w = w or x.shape[1]
    ts = _row_tile(S, w)

    def body(x_ref, g_ref, dh_ref, *rest):
        _, vjp = jax.vjp(lambda xv, gv: _rms(xv, gv, w), x_ref[...], g_ref[...])
        dx, dg = vjp(dh_ref[...].astype(F32))
        if rest:
            dx = dx + rest[0][...]
        return dx, dg

    row = pl.BlockSpec((ts, w), lambda i: (i, 0))
    vec = pl.BlockSpec((1, w), lambda i: (0, 0))
    ins = [(x, row), (g, vec), (dh, row)] + ([(res, row)] if res is not None else [])
    return tilecall(body, name, (S // ts,), ins,
                    [(_sds((S, w)), row, None), (_sds((1, w)), vec, lambda: pl.program_id(0) == 0)], ("arbitrary",))


def _shifted(u, halo_ref, is_first):
    rid = lax.broadcasted_iota(jnp.int32, (u.shape[0], 1), 0)
    h7 = jnp.where(is_first, 0.0, halo_ref[7:8, :])
    h6 = jnp.where(is_first, 0.0, halo_ref[6:7, :])
    u1 = jnp.where(rid == 0, h7, pltpu.roll(u, 1, 0))
    u2 = jnp.where(rid == 0, h6, jnp.where(rid == 1, h7, pltpu.roll(u, 2, 0)))
    return u1, u2


def _conv(u, u1, u2, cw_ref, cb_ref):
    return ((cb_ref[...] + u2 * cw_ref[0:1, :]) + u1 * cw_ref[1:2, :]) + u * cw_ref[2:3, :]


def _ffn_specs(S, ts, jmap):
    hb = ts // SUBLANES
    return (pl.BlockSpec((ts, FF_BLOCK), lambda j, i: (i, jmap(j))),
            pl.BlockSpec((SUBLANES, FF_BLOCK), lambda j, i: (jnp.maximum(i * hb - 1, 0), jmap(j))),
            pl.BlockSpec((SUBLANES, FF_BLOCK), lambda j, i: (0, jmap(j))),
            pl.BlockSpec((1, FF_BLOCK), lambda j, i: (0, jmap(j))))


def ffn_act_fwd(u, cw8, cb, name):
    S = u.shape[0]
    ts = _row_tile(S, 2 * D_FF)
    nb = D_FF // FF_BLOCK

    def body(ug, hg, cwg, cbg, uu, hu, cwu, cbu):
        first = pl.program_id(1) == 0
        g = ug[...]
        g1, g2 = _shifted(g, hg, first)
        yg = _conv(g, g1, g2, cwg, cbg)
        v = uu[...]
        v1, v2 = _shifted(v, hu, first)
        yu = _conv(v, v1, v2, cwu, cbu)
        return (yg * jax.nn.sigmoid(yg) * yu,)

    sg = _ffn_specs(S, ts, lambda j: j)
    su = _ffn_specs(S, ts, lambda j: j + nb)
    ins = [(u, sg[0]), (u, sg[1]), (cw8, sg[2]), (cb, sg[3]), (u, su[0]), (u, su[1]), (cw8, su[2]), (cb, su[3])]
    return tilecall(body, name, (nb, S // ts), ins,
                    [(_sds((S, D_FF), _MXU), pl.BlockSpec((ts, FF_BLOCK), lambda j, i: (i, j)), None)],
                    ("parallel", "parallel"))[0]


def ffn_act_bwd(u, cw8, cb, da, name):
    S = u.shape[0]
    ts = _row_tile(S, 2 * D_FF)
    nb = D_FF // FF_BLOCK

    def taps(dy, x, x1, x2):
        return jnp.concatenate(
            [jnp.sum(dy * x2, axis=0, keepdims=True), jnp.sum(dy * x1, axis=0, keepdims=True),
             jnp.sum(dy * x, axis=0, keepdims=True), jnp.zeros((SUBLANES - 3, dy.shape[1]), F32)], axis=0)

    def body(ug, hg, cwg, cbg, uu, hu, cwu, cbu, da_ref):
        first = pl.program_id(1) == 0
        g = ug[...]
        g1, g2 = _shifted(g, hg, first)
        yg = _conv(g, g1, g2, cwg, cbg)
        v = uu[...]
        v1, v2 = _shifted(v, hu, first)
        yu = _conv(v, v1, v2, cwu, cbu)
        d = da_ref[...]
        sg = jax.nn.sigmoid(yg)
        dyg = d * yu * (sg * (1.0 + yg * (1.0 - sg)))
        dyu = d * (yg * sg)
        return (dyg, dyu, taps(dyg, g, g1, g2), taps(dyu, v, v1, v2),
                jnp.sum(dyg, axis=0, keepdims=True), jnp.sum(dyu, axis=0, keepdims=True))

    sg_ = _ffn_specs(S, ts, lambda j: j)
    su_ = _ffn_specs(S, ts, lambda j: j + nb)
    row = pl.BlockSpec((ts, FF_BLOCK), lambda j, i: (i, j))
    ins = [(u, sg_[0]), (u, sg_[1]), (cw8, sg_[2]), (cb, sg_[3]), (u, su_[0]), (u, su_[1]), (cw8, su_[2]), (cb, su_[3]), (da, row)]
    first_row = lambda: pl.program_id(1) == 0
    dy, dcw, dcb = (_sds((S, D_FF)), row, None), (_sds((SUBLANES, D_FF)), sg_[2], first_row), (_sds((1, D_FF)), sg_[3], first_row)
    return tilecall(body, name, (nb, S // ts), ins, [dy, dy, dcw, dcw, dcb, dcb], ("parallel", "arbitrary"))


def ffn_conv_bwd(dyg, dyu, cw8, name):
    S = dyg.shape[0]
    ts = _row_tile(S, 2 * D_FF)
    hb = ts // SUBLANES
    nrow = S // ts
    nb = D_FF // FF_BLOCK

    def back(dy_ref, halo_ref, cw_ref):
        last = pl.program_id(1) == nrow - 1
        d = dy_ref[...]
        rid = lax.broadcasted_iota(jnp.int32, (ts, 1), 0)
        n0 = jnp.where(last, 0.0, halo_ref[0:1, :])
        n1 = jnp.where(last, 0.0, halo_ref[1:2, :])
        d1 = jnp.where(rid == ts - 1, n0, pltpu.roll(d, ts - 1, 0))
        d2 = jnp.where(rid == ts - 1, n1, jnp.where(rid == ts - 2, n0, pltpu.roll(d, ts - 2, 0)))
        return d * cw_ref[2:3, :] + d1 * cw_ref[1:2, :] + d2 * cw_ref[0:1, :]

    row = pl.BlockSpec((ts, FF_BLOCK), lambda j, i: (i, j))
    halo = pl.BlockSpec((SUBLANES, FF_BLOCK), lambda j, i: (jnp.minimum((i + 1) * hb, S // SUBLANES - 1), j))
    ins = [(dyg, row), (dyg, halo), (cw8, pl.BlockSpec((SUBLANES, FF_BLOCK), lambda j, i: (0, j))),
           (dyu, row), (dyu, halo), (cw8, pl.BlockSpec((SUBLANES, FF_BLOCK), lambda j, i: (0, j + nb)))]
    out = (_sds((S, D_FF), _MXU), row, None)
    return tilecall(lambda a, b, c, d, e, f: (back(a, b, c), back(d, e, f)), name, (nb, nrow), ins, [out, out],
                    ("parallel", "parallel"))


def _hgrn_levels(C):
    out, m = [], C // 2
    while m >= 1:
        out.append(m)
        m //= 2
    return out


def _hgrn_sum_matrix(C):
    t = np.arange(C)[:, None]
    u = np.arange(C)[None, :]
    blocks = [u <= t, u > t]
    for m in _hgrn_levels(C):
        r = (t // (2 * m)) * (2 * m) + m
        right = (t % (2 * m)) >= m
        blocks.append(right & (u > r) & (u <= t))
        blocks.append((~right) & (u > t) & (u <= r))
    return np.concatenate(blocks, axis=0).astype(np.float32)


def _make_partial_sums(nb, C):
    @jax.custom_vjp
    def sums(mall, lf):
        hi = lf.astype(_MXU)
        mid = (lf - hi.astype(F32)).astype(_MXU)
        e2 = _dg(mall, jnp.concatenate([hi, mid], axis=1), _NN)
        e = e2[:, :HEAD_DIM] + e2[:, HEAD_DIM:]
        return tuple(e[b * C:(b + 1) * C] for b in range(nb))

    def fwd(mall, lf):
        return sums(mall, lf), mall

    def bwd(mall, gs):
        return jnp.zeros_like(mall), _dg(mall, jnp.concatenate(gs, axis=0), _TN)

    sums.defvjp(fwd, bwd)
    return sums


def _hgrn_chunk(zq, zf, v, lb, st, mall, C):
    levels = _hgrn_levels(C)
    qs = zq * jax.nn.sigmoid(zq)
    fg = lb + (1.0 - lb) * jax.nn.sigmoid(zf)
    k = 1.0 - fg
    e = _make_partial_sums(2 + 2 * len(levels), C)(mall, jnp.log(fg))
    g_incl, g_after = e[0], e[1]
    rid = lax.broadcasted_iota(jnp.int32, (C, 1), 0)
    tt = lax.broadcasted_iota(jnp.int32, (C, C), 0)
    ss = lax.broadcasted_iota(jnp.int32, (C, C), 1)
    o = kdot_nt(qs * jnp.exp(g_incl), st)
    o = o + jnp.sum(qs * k, axis=-1, keepdims=True) * v
    scores = jnp.zeros((C, C), F32)
    for li, m in enumerate(levels):
        sh = int(np.log2(m))
        right = ((rid >> sh) & 1) == 1
        qt = jnp.where(right, qs * jnp.exp(e[2 + 2 * li]), 0.0)
        kt = jnp.where(right, 0.0, k * jnp.exp(e[3 + 2 * li]))
        pair = ((tt >> (sh + 1)) == (ss >> (sh + 1))) & (((tt >> sh) & 1) == 1) & (((ss >> sh) & 1) == 0)
        scores = scores + jnp.where(pair, kdot_nt(qt, kt), 0.0)
    o = o + kdot(scores, v)
    g_last = jnp.sum(jnp.where(rid == C - 1, g_incl, 0.0), axis=0, keepdims=True)
    st_new = st * jnp.exp(g_last) + kdot_tn(v, k * jnp.exp(g_after))
    return o, st_new


HGRN_HEADS_PER_STEP = 2
_HGRN_LANES = HGRN_HEADS_PER_STEP * HEAD_DIM


def _hgrn_in_specs(C, nc, rev):
    cm = (lambda c: nc - 1 - c) if rev else (lambda c: c)
    blk = lambda: pl.BlockSpec((C, _HGRN_LANES), lambda h, c: (cm(c), h))
    return cm, [blk(), blk(), blk(), pl.BlockSpec((1, _HGRN_LANES), lambda h, c: (0, h))]


def _hgrn_state_spec(cm):
    return pl.BlockSpec((HGRN_HEADS_PER_STEP, None, HEAD_DIM, HEAD_DIM), lambda h, c: (h, cm(c), 0, 0))


def hgrn_fwd(zq, zf, zi, lb, mall, name):
    S = zq.shape[0]
    C = min(HGRN_CHUNK, S)
    nc = S // C

    def kern(zq_ref, zf_ref, zi_ref, lb_ref, mall_ref, o_ref, st_ref, st):
        @pl.when(pl.program_id(1) == 0)
        def _():
            st[...] = jnp.zeros_like(st)

        mall_v = mall_ref[...]
        for g in range(HGRN_HEADS_PER_STEP):
            lanes = slice(g * HEAD_DIM, (g + 1) * HEAD_DIM)
            s_in = st[g]
            st_ref[g] = s_in
            o, s_new = _hgrn_chunk(zq_ref[:, lanes], zf_ref[:, lanes], zi_ref[:, lanes], lb_ref[:, lanes], s_in, mall_v, C)
            o_ref[:, lanes] = o
            st[g] = s_new

    cm, specs = _hgrn_in_specs(C, nc, False)
    return pl.pallas_call(
        kern, name=name, grid=(HEADS // HGRN_HEADS_PER_STEP, nc),
        in_specs=specs + [pl.BlockSpec(mall.shape, lambda h, c: (0, 0))],
        out_specs=[pl.BlockSpec((C, _HGRN_LANES), lambda h, c: (c, h)), _hgrn_state_spec(cm)],
        out_shape=[_sds((S, D_MODEL)), _sds((HEADS, nc, HEAD_DIM, HEAD_DIM))],
        scratch_shapes=[pltpu.VMEM((HGRN_HEADS_PER_STEP, HEAD_DIM, HEAD_DIM), F32)],
        compiler_params=_params(("parallel", "arbitrary")))(zq, zf, zi, lb, mall)


def hgrn_bwd(zq, zf, zi, lb, mall, states, do, name):
    S = zq.shape[0]
    C = min(HGRN_CHUNK, S)
    nc = S // C

    def kern(zq_ref, zf_ref, zi_ref, lb_ref, mall_ref, st_ref, do_ref, dq_ref, df_ref, di_ref, dlb_ref, dst):
        first = pl.program_id(1) == 0

        @pl.when(first)
        def _():
            dst[...] = jnp.zeros_like(dst)

        mall_v = mall_ref[...]
        gls = []
        for g in range(HGRN_HEADS_PER_STEP):
            lanes = slice(g * HEAD_DIM, (g + 1) * HEAD_DIM)
            _, vjp = jax.vjp(lambda a, b, c, d, e: _hgrn_chunk(a, b, c, d, e, mall_v, C),
                             zq_ref[:, lanes], zf_ref[:, lanes], zi_ref[:, lanes], lb_ref[:, lanes], st_ref[g])
            ga, gb, gv, gl, gs = vjp((do_ref[:, lanes], dst[g]))
            dq_ref[:, lanes] = ga.astype(dq_ref.dtype)
            df_ref[:, lanes] = gb.astype(df_ref.dtype)
            di_ref[:, lanes] = gv.astype(di_ref.dtype)
            gls.append(gl)
            dst[g] = gs
        _store(dlb_ref, jnp.concatenate(gls, axis=1), first)

    cm, specs = _hgrn_in_specs(C, nc, True)
    row = lambda: pl.BlockSpec((C, _HGRN_LANES), lambda h, c: (cm(c), h))
    return pl.pallas_call(
        kern, name=name, grid=(HEADS // HGRN_HEADS_PER_STEP, nc),
        in_specs=specs + [pl.BlockSpec(mall.shape, lambda h, c: (0, 0)), _hgrn_state_spec(cm), row()],
        out_specs=[row(), row(), row(), pl.BlockSpec((1, _HGRN_LANES), lambda h, c: (0, h))],
        out_shape=[_sds((S, D_MODEL), _MXU), _sds((S, D_MODEL), _MXU), _sds((S, D_MODEL), _MXU), _sds((1, D_MODEL))],
        scratch_shapes=[pltpu.VMEM((HGRN_HEADS_PER_STEP, HEAD_DIM, HEAD_DIM), F32)],
        compiler_params=_params(("parallel", "arbitrary")))(zq, zf, zi, lb, mall, states, do)


def _hgrn_out(o, g, gain):
    return _rms(o, gain, HEAD_DIM) * (g * jax.nn.sigmoid(g))


def hgrn_out_fwd(o, zg, gain, name):
    S = o.shape[0]
    ts = _row_tile(S, D_MODEL)
    blk = pl.BlockSpec((ts, HEAD_DIM), lambda i, h: (i, h))
    return tilecall(lambda o_ref, g_ref, w_ref: (_hgrn_out(o_ref[...], g_ref[...], w_ref[...]),), name, (S // ts, HEADS),
                    [(o, blk), (zg, blk), (gain, pl.BlockSpec((1, HEAD_DIM), lambda i, h: (0, 0)))],
                    [(_sds((S, D_MODEL), _MXU), blk, None)], ("parallel", "parallel"))[0]


def hgrn_out_bwd(o, zg, gain, don, name):
    S = o.shape[0]
    ts = _row_tile(S, D_MODEL)
    blk = pl.BlockSpec((ts, HEAD_DIM), lambda i, h: (i, h))
    vec = pl.BlockSpec((1, HEAD_DIM), lambda i, h: (0, 0))

    def body(o_ref, g_ref, w_ref, d_ref):
        _, vjp = jax.vjp(_hgrn_out, o_ref[...], g_ref[...], w_ref[...])
        return vjp(d_ref[...])

    return tilecall(body, name, (S // ts, HEADS), [(o, blk), (zg, blk), (gain, vec), (don, blk)],
                    [(_sds((S, D_MODEL)), blk, None), (_sds((S, D_MODEL), _MXU), blk, None),
                     (_sds((1, HEAD_DIM)), vec, lambda: (pl.program_id(0) == 0) & (pl.program_id(1) == 0))],
                    ("arbitrary", "arbitrary"))


def _lb_soft(p0, p1):
    mx = jnp.maximum(p0, p1)
    e0, e1 = jnp.exp(p0 - mx), jnp.exp(p1 - mx)
    s0, s1 = e0 / (e0 + e1), e1 / (e0 + e1)
    return (s0 + s1) - s0


def lower_bound_fwd(p, name):
    assert p.shape[0] == 2

    def body(p_ref):
        s = _lb_soft(p_ref[0:1, :], p_ref[1:2, :])
        return (jnp.concatenate([jnp.zeros_like(s), s] + [jnp.zeros_like(s)] * (SUBLANES - 2), axis=0),)

    spec8 = pl.BlockSpec((SUBLANES, p.shape[1]), lambda: (0, 0))
    return tilecall(body, name, (), [(p, pl.BlockSpec(p.shape, lambda: (0, 0)))], [(_sds((SUBLANES, p.shape[1])), spec8, None)], ())[0]


def lower_bound_bwd(p, dlb1, name):
    def body(p_ref, d_ref):
        _, vjp = jax.vjp(_lb_soft, p_ref[0:1, :], p_ref[1:2, :])
        g0, g1 = vjp(d_ref[...])
        return (jnp.concatenate([g0, g1] + [jnp.zeros_like(g0)] * (SUBLANES - 2), axis=0),)

    spec8 = pl.BlockSpec((SUBLANES, p.shape[1]), lambda: (0, 0))
    return tilecall(body, name, (), [(p, pl.BlockSpec(p.shape, lambda: (0, 0))), (dlb1, pl.BlockSpec(dlb1.shape, lambda: (0, 0)))],
                    [(_sds((SUBLANES, p.shape[1])), spec8, None)], ())[0]


@jax.custom_vjp
def _swap_rope_halves(x):
    lane = lax.broadcasted_iota(jnp.int32, x.shape, 1)
    lo = (lane >= MLA_NOPE) & (lane < MLA_NOPE + MLA_ROPE // 2)
    hi = (lane >= MLA_NOPE + MLA_ROPE // 2) & (lane < MLA_QK)
    return jnp.where(lo, pltpu.roll(x, MLA_SLOT - MLA_ROPE // 2, 1), jnp.where(hi, pltpu.roll(x, MLA_ROPE // 2, 1), 0.0))


_swap_rope_halves.defvjp(lambda x: (_swap_rope_halves(x), None), lambda _, g: (_swap_rope_halves(g),))


def _norm_rope(x, gain, cos_t, sin_t):
    y = _rms(x, gain, MLA_QK)
    return y * cos_t + _swap_rope_halves(y) * sin_t


def _qk_heads(qs, kn, kr, qn, kn_gain, cos_t, sin_t):
    q = _norm_rope(qs, qn, cos_t, sin_t)
    k = _norm_rope(jnp.concatenate([kn, kr], axis=1), kn_gain, cos_t, sin_t)
    return q, k


def _qk_specs(ts):
    slot = pl.BlockSpec((ts, MLA_SLOT), lambda i, h: (i, h))
    nope = pl.BlockSpec((ts, HEAD_DIM), lambda i, h: (i, 2 * h))
    shared = pl.BlockSpec((ts, HEAD_DIM), lambda i, h: (i, 0))
    gain = pl.BlockSpec((1, MLA_SLOT), lambda i, h: (0, 0))
    table = pl.BlockSpec((ts, MLA_SLOT), lambda i, h: (i, 0))
    return slot, nope, shared, gain, table


def qk_fwd(qslots, kv, krope, qn, kn, cos_t, sin_t, name):
    S = qslots.shape[0]
    ts = _row_tile(S, D_MODEL)
    slot, nope, shared, gain, table = _qk_specs(ts)

    def body(q_ref, kn_ref, kr_ref, qn_ref, kg_ref, c_ref, s_ref):
        return _qk_heads(q_ref[...], kn_ref[...], kr_ref[...], qn_ref[...], kg_ref[...], c_ref[...], s_ref[...])

    out = _sds((S, HEADS * MLA_SLOT), _MXU)
    return tilecall(body, name, (S // ts, HEADS),
                    [(qslots, slot), (kv, nope), (krope, shared), (qn, gain), (kn, gain), (cos_t, table), (sin_t, table)],
                    [(out, slot, None), (out, slot, None)], ("parallel", "parallel"))


def qk_bwd(qslots, kv, krope, qn, kn, cos_t, sin_t, dq, dk, dv, name):
    S = qslots.shape[0]
    ts = _row_tile(S, D_MODEL)
    slot, nope, shared, gain, table = _qk_specs(ts)
    vblk = pl.BlockSpec((ts, HEAD_DIM), lambda i, h: (i, h))

    def body(q_ref, kn_ref, kr_ref, qn_ref, kg_ref, c_ref, s_ref, dq_ref, dk_ref, dv_ref):
        c, s = c_ref[...], s_ref[...]
        _, vjp = jax.vjp(lambda a, b, r, g1, g2: _qk_heads(a, b, r, g1, g2, c, s),
                         q_ref[...], kn_ref[...], kr_ref[...], qn_ref[...], kg_ref[...])
        ga, gb, gr, g1, g2 = vjp((dq_ref[...], dk_ref[...]))
        return ga, jnp.concatenate([gb, dv_ref[...]], axis=1), gr, g1, g2

    first_head = lambda: pl.program_id(1) == 0
    first = lambda: (pl.program_id(0) == 0) & (pl.program_id(1) == 0)
    wide = _sds((S, HEADS * MLA_SLOT), _MXU)
    return tilecall(body, name, (S // ts, HEADS),
                    [(qslots, slot), (kv, nope), (krope, shared), (qn, gain), (kn, gain), (cos_t, table), (sin_t, table),
                     (dq, slot), (dk, slot), (dv, vblk)],
                    [(wide, slot, None), (wide, slot, None), (_sds((S, HEAD_DIM)), shared, first_head),
                     (_sds((1, MLA_SLOT)), gain, first), (_sds((1, MLA_SLOT)), gain, first)], ("arbitrary", "arbitrary"))


_ATTN_SCALE = MLA_QK ** -0.5
ATTN_TILE = 512
ATTN_HEADS_PER_STEP = 2


def _causal_pairs(nq, by_row):
    pairs = [(i, j) for i in range(nq) for j in range(i + 1)] if by_row else [(i, j) for j in range(nq) for i in range(j, nq)]
    return jnp.asarray([p[0] for p in pairs], jnp.int32), jnp.asarray([p[1] for p in pairs], jnp.int32)


def _diag_mask(s, tq):
    rows = lax.broadcasted_iota(jnp.int32, (tq, tq), 0)
    cols = lax.broadcasted_iota(jnp.int32, (tq, tq), 1)
    return jnp.where(rows >= cols, s, -jnp.inf)


def attn_fwd(qr, kr, kv, name):
    S = qr.shape[0]
    tq = min(S, ATTN_TILE)
    nq = S // tq
    i_tab, j_tab = _causal_pairs(nq, True)

    G = ATTN_HEADS_PER_STEP
    reps = tq // LANES

    def kern(it, jt, q_ref, k_ref, kv_ref, o_ref, lse_ref, m_s, l_s, acc):
        n = pl.program_id(1)
        i, j = it[n], jt[n]

        @pl.when(j == 0)
        def _():
            m_s[...] = jnp.full_like(m_s, -jnp.inf)
            l_s[...] = jnp.zeros_like(l_s)
            acc[...] = jnp.zeros_like(acc)

        def step(diagonal):
            for g in range(G):
                slot = slice(g * MLA_SLOT, (g + 1) * MLA_SLOT)
                s = _dg(q_ref[:, slot], k_ref[:, slot], _NT) * _ATTN_SCALE
                if diagonal:
                    s = _diag_mask(s, tq)
                m_prev = m_s[g]
                m_new = jnp.maximum(m_prev, jnp.max(s, axis=-1, keepdims=True))
                alpha = jnp.exp(m_prev - m_new)
                p = jnp.exp(s - jnp.tile(m_new, (1, reps)))
                l_s[g] = alpha * l_s[g] + jnp.sum(p, axis=-1, keepdims=True)
                acc[g] = alpha * acc[g] + _dg(p, kv_ref[:, g * MLA_SLOT + HEAD_DIM:(g + 1) * MLA_SLOT], _NN)
                m_s[g] = m_new

        @pl.when(j < i)
        def _():
            step(False)

        @pl.when(j == i)
        def _():
            step(True)
            for g in range(G):
                l = l_s[g]
                lanes = slice(g * HEAD_DIM, (g + 1) * HEAD_DIM)
                o_ref[:, lanes] = acc[g] / l
                lse_ref[:, lanes] = m_s[g] + jnp.log(l)

    out = _sds((S, HEADS * HEAD_DIM))
    oblk = pl.BlockSpec((tq, G * HEAD_DIM), lambda h, n, it, jt: (it[n], h))
    stat = pltpu.VMEM((G, tq, HEAD_DIM), F32)
    return pl.pallas_call(
        kern, name=name,
        grid_spec=pltpu.PrefetchScalarGridSpec(
            num_scalar_prefetch=2, grid=(HEADS // G, i_tab.shape[0]),
            in_specs=[pl.BlockSpec((tq, G * MLA_SLOT), lambda h, n, it, jt: (it[n], h)),
                      pl.BlockSpec((tq, G * MLA_SLOT), lambda h, n, it, jt: (jt[n], h)),
                      pl.BlockSpec((tq, G * MLA_SLOT), lambda h, n, it, jt: (jt[n], h))],
            out_specs=[oblk, oblk], scratch_shapes=[stat, stat, stat]),
        out_shape=[out, out], compiler_params=_params(("parallel", "arbitrary")))(i_tab, j_tab, qr, kr, kv)


def attn_bwd(qr, kr, kv, o, lse, do, name):
    S = qr.shape[0]
    tq = min(S, ATTN_TILE)
    nq = S // tq
    i_tab, j_tab = _causal_pairs(nq, False)

    G = ATTN_HEADS_PER_STEP

    def kern(it, jt, q_ref, k_ref, kv_ref, o_ref, lse_ref, do_ref, dq_ref, dk_ref, dv_ref, dk_acc, dv_acc):
        n = pl.program_id(1)
        i, j = it[n], jt[n]

        @pl.when(n == 0)
        def _():
            dq_ref[...] = jnp.zeros_like(dq_ref)

        @pl.when(i == j)
        def _():
            dk_acc[...] = jnp.zeros_like(dk_acc)
            dv_acc[...] = jnp.zeros_like(dv_acc)

        def step(diagonal):
            rows = pl.ds(pl.multiple_of(i * tq, tq), tq)
            for g in range(G):
                slot = slice(g * MLA_SLOT, (g + 1) * MLA_SLOT)
                lanes = slice(g * HEAD_DIM, (g + 1) * HEAD_DIM)
                q, k = q_ref[:, slot], k_ref[:, slot]
                s = _dg(q, k, _NT) * _ATTN_SCALE - jnp.tile(lse_ref[:, lanes], (1, tq // LANES))
                if diagonal:
                    s = _diag_mask(s, tq)
                p = jnp.exp(s)
                d = do_ref[:, lanes]
                delta = jnp.sum(d * o_ref[:, lanes], axis=-1, keepdims=True)
                dv_acc[:, lanes] += _dg(p, d, _TN)
                ds = p * (_dg(d, kv_ref[:, g * MLA_SLOT + HEAD_DIM:(g + 1) * MLA_SLOT], _NT) - delta) * _ATTN_SCALE
                dk_acc[:, slot] += _dg(ds, q, _TN)
                dq_ref[rows, slot] += _dg(ds, k, _NN)

        @pl.when(i > j)
        def _():
            step(False)

        @pl.when(i == j)
        def _():
            step(True)

        @pl.when(i == nq - 1)
        def _():
            dk_ref[...] = dk_acc[...]
            dv_ref[...] = dv_acc[...]

    qblk = pl.BlockSpec((tq, G * MLA_SLOT), lambda h, n, it, jt: (it[n], h))
    oblk = pl.BlockSpec((tq, G * HEAD_DIM), lambda h, n, it, jt: (it[n], h))
    kblk = pl.BlockSpec((tq, G * MLA_SLOT), lambda h, n, it, jt: (jt[n], h))
    return pl.pallas_call(
        kern, name=name,
        grid_spec=pltpu.PrefetchScalarGridSpec(
            num_scalar_prefetch=2, grid=(HEADS // G, i_tab.shape[0]),
            in_specs=[qblk, kblk, kblk, oblk, oblk, oblk],
            out_specs=[pl.BlockSpec((S, G * MLA_SLOT), lambda h, n, it, jt: (0, h)), kblk,
                       pl.BlockSpec((tq, G * HEAD_DIM), lambda h, n, it, jt: (jt[n], h))],
            scratch_shapes=[pltpu.VMEM((tq, G * MLA_SLOT), F32), pltpu.VMEM((tq, G * HEAD_DIM), F32)]),
        out_shape=[_sds((S, HEADS * MLA_SLOT)), _sds((S, HEADS * MLA_SLOT)), _sds((S, HEADS * HEAD_DIM))],
        compiler_params=_params(("parallel", "arbitrary")))(i_tab, j_tab, qr, kr, kv, o, lse, do)


def loss_head(y, target, name):
    S, Dm = y.shape
    ts = _row_tile(S, Dm)

    def body(y_ref, t_ref):
        e = y_ref[...] - t_ref[...]
        tot = jnp.sum(jnp.sum(e * e, axis=-1, keepdims=True) / Dm, axis=0, keepdims=True)
        return e / Dm, jnp.broadcast_to(0.5 * tot, (SUBLANES, LANES))

    row = pl.BlockSpec((ts, Dm), lambda i: (i, 0))
    return tilecall(body, name, (S // ts,), [(y, row), (target, row)],
                    [(_sds((S, Dm)), row, None),
                     (_sds((SUBLANES, LANES)), pl.BlockSpec((SUBLANES, LANES), lambda i: (0, 0)), lambda: pl.program_id(0) == 0)],
                    ("arbitrary",))


_CHIP_FLIPS = ((1, 0), (0, 1), (1, 1))
_PEER_FLIPS = {
    "chips": ((1, 0, 0), (0, 1, 0), (1, 1, 0)),
    "sibling": ((0, 0, 1),),
    "all": tuple((a, b, c) for a in (0, 1) for b in (0, 1) for c in (0, 1))[1:],
}
_SLOT_WEIGHTS = {"chips": (2, 1, 0), "sibling": (0, 0, 1), "all": (4, 2, 1)}
_HBM = pl.BlockSpec(memory_space=pltpu.HBM)


def _me():
    return lax.axis_index("x"), lax.axis_index("y"), lax.axis_index("c")


def _remote(src, dst, send_sem, recv_sem, peer):
    return pltpu.make_async_remote_copy(src_ref=src, dst_ref=dst, send_sem=send_sem, recv_sem=recv_sem,
                                        device_id=peer, device_id_type=pl.DeviceIdType.MESH)


def exchange(arrs, group, slab_weights, name, keep_own=True):
    flips = _PEER_FLIPS[group]
    wx, wy, wc = _SLOT_WEIGHTS[group]
    n_slots = len(flips) + (1 if keep_own else 0)
    n = len(arrs)

    def slab(ref, a, pos):
        w = slab_weights[a]
        return ref if w is None else ref.at[w[0] * pos[0] + w[1] * pos[1] + w[2] * pos[2]]

    def kern(*refs):
        srcs, outs = refs[:n], refs[n:2 * n]
        send_sems, recv_sems = refs[2 * n:2 * n + 2]
        me = _me()
        my_slot = wx * me[0] + wy * me[1] + wc * me[2]
        copies = []
        if keep_own:
            local_sems = refs[2 * n + 2]
            for a in range(n):
                cp = pltpu.make_async_copy(slab(srcs[a], a, me), outs[a].at[my_slot], local_sems.at[a])
                cp.start()
                copies.append(cp)
        for f, flip in enumerate(flips):
            peer = tuple(m ^ b if b else m for m, b in zip(me, flip))
            for a in range(n):
                cp = _remote(slab(srcs[a], a, peer), outs[a].at[my_slot if keep_own else f],
                             send_sems.at[f, a], recv_sems.at[f, a], peer)
                cp.start()
                copies.append(cp)
        for cp in copies:
            cp.wait()

    out_shape = [_sds((n_slots,) + (a.shape if slab_weights[k] is None else a.shape[1:]), a.dtype) for k, a in enumerate(arrs)]
    sems = [pltpu.SemaphoreType.DMA((len(flips), n)), pltpu.SemaphoreType.DMA((len(flips), n))]
    return pl.pallas_call(
        kern, name=name, in_specs=[_HBM] * n, out_specs=[_HBM] * n, out_shape=out_shape,
        scratch_shapes=sems + ([pltpu.SemaphoreType.DMA((n,))] if keep_own else []))(*arrs)


def _chip_window(ref, kind, size, chip, layers):
    if kind == "rows":
        return ref.at[layers, pl.ds(chip * size, size), :]
    return ref.at[layers, :, pl.ds(pl.multiple_of(chip * size, LANES), size)]


def gather_big(shards, kinds, name):
    n = len(shards)
    fulls = []
    for s, kind in zip(shards, kinds):
        L, r, c = s.shape
        fulls.append(_sds((L, N_CHIPS * r, c) if kind == "rows" else (L, r, N_CHIPS * c), s.dtype))

    def kern(*refs):
        srcs, outs = refs[:n], refs[n:2 * n]
        ici_s, ici_r, d2d_s, d2d_r = refs[2 * n:]
        x, y, c = _me()
        my_chip = 2 * x + y
        ici, fwd = [], []
        for a in range(n):
            L, r, cc = shards[a].shape
            size = r if kinds[a] == "rows" else cc
            mine = pl.ds(c * (L // 2), L // 2)
            for f, (fx, fy) in enumerate(_CHIP_FLIPS):
                cp = _remote(srcs[a].at[mine], _chip_window(outs[a], kinds[a], size, my_chip, mine),
                             ici_s.at[a, f], ici_r.at[a, f], (x ^ fx, y ^ fy, c))
                cp.start()
                ici.append(cp)
        for a in range(n):
            L, r, cc = shards[a].shape
            size = r if kinds[a] == "rows" else cc
            mine = pl.ds(c * (L // 2), L // 2)
            for f, (fx, fy) in enumerate(_CHIP_FLIPS):
                ici[a * len(_CHIP_FLIPS) + f].wait_recv()
                landed = _chip_window(outs[a], kinds[a], size, 2 * (x ^ fx) + (y ^ fy), mine)
                cp = _remote(landed, landed, d2d_s.at[a, f], d2d_r.at[a, f], (x, y, 1 - c))
                cp.start()
                fwd.append(cp)
        for cp in ici:
            cp.wait_send()
        for cp in fwd:
            cp.wait()

    sem = pltpu.SemaphoreType.DMA((n, len(_CHIP_FLIPS)))
    return pl.pallas_call(kern, name=name, in_specs=[_HBM] * n, out_specs=[_HBM] * n, out_shape=fulls,
                          scratch_shapes=[sem, sem, sem, sem])(*shards)


def send_other_half(arrs, name):
    n = len(arrs)

    def kern(*refs):
        srcs, outs = refs[:n], refs[n:2 * n]
        send_sems, recv_sems = refs[2 * n:]
        x, y, c = _me()
        copies = []
        for a in range(n):
            hl = arrs[a].shape[0] // 2
            cp = _remote(srcs[a].at[pl.ds((1 - c) * hl, hl)], outs[a], send_sems.at[a], recv_sems.at[a], (x, y, 1 - c))
            cp.start()
            copies.append(cp)
        for cp in copies:
            cp.wait()

    return pl.pallas_call(
        kern, name=name, in_specs=[_HBM] * n, out_specs=[_HBM] * n,
        out_shape=[_sds((a.shape[0] // 2,) + a.shape[1:], a.dtype) for a in arrs],
        scratch_shapes=[pltpu.SemaphoreType.DMA((n,)), pltpu.SemaphoreType.DMA((n,))])(*arrs)


def scatter_to_chips(arrs, kinds, name):
    n = len(arrs)
    shapes = []
    for a, kind in zip(arrs, kinds):
        l, R, C = a.shape
        shapes.append((l, R // N_CHIPS, C) if kind == "rows" else (l, R, C // N_CHIPS))

    def kern(*refs):
        srcs, outs = refs[:n], refs[n:2 * n]
        send_sems, recv_sems = refs[2 * n:]
        x, y, c = _me()
        copies = []
        for a in range(n):
            size = shapes[a][1] if kinds[a] == "rows" else shapes[a][2]
            for f, (fx, fy) in enumerate(_CHIP_FLIPS):
                window = _chip_window(srcs[a], kinds[a], size, 2 * (x ^ fx) + (y ^ fy), slice(None))
                cp = _remote(window, outs[a].at[f], send_sems.at[a, f], recv_sems.at[a, f], (x ^ fx, y ^ fy, c))
                cp.start()
                copies.append(cp)
        for cp in copies:
            cp.wait()

    sem = pltpu.SemaphoreType.DMA((n, len(_CHIP_FLIPS)))
    return pl.pallas_call(
        kern, name=name, in_specs=[_HBM] * n, out_specs=[_HBM] * n,
        out_shape=[_sds((len(_CHIP_FLIPS),) + s, a.dtype) for s, a in zip(shapes, arrs)],
        scratch_shapes=[sem, sem])(*arrs)


def _stack_tile(r, c):
    for t in (1024, 704, 512, 352, 256, 128, 64, 32, 16):
        if r % t == 0 and t * c * 4 <= 3 * 512 * 1024:
            return t
    return r


def _window_map(kind, r, tr):
    nrt = r // tr
    if kind == "rows":
        return lambda l, i, chip: (l, chip[0] * nrt + i, 0)
    return lambda l, i, chip: (l, i, chip[0])


def place(full, shard, kind, chip, name):
    L, r, c = shard.shape
    tr = _stack_tile(r, c)
    wmap = _window_map(kind, r, tr)
    return tilecall(lambda chip_ref, s_ref, f_ref: (s_ref[...],), name, (L, r // tr),
                    [(shard, pl.BlockSpec((None, tr, c), lambda l, i, chip: (l, i, 0))), (full, pl.BlockSpec(memory_space=pl.ANY))],
                    [(_sds(full.shape, full.dtype), pl.BlockSpec((None, tr, c), lambda l, i, chip: wmap(l, i, chip)), None)],
                    ("parallel", "parallel"), prefetch=(chip,), aliases={1: 0})[0]


def add_cores(g, other, core, name):
    L, R, C = g.shape
    hl = L // 2
    tr = _stack_tile(R, C)
    blk = (None, tr, C)
    return tilecall(lambda core_ref, a_ref, b_ref: (a_ref[...] + b_ref[...],), name, (hl, R // tr),
                    [(g, pl.BlockSpec(blk, lambda l, i, core: (core[0] * hl + l, i, 0))),
                     (other, pl.BlockSpec(blk, lambda l, i, core: (l, i, 0)))],
                    [(_sds((hl, R, C), _MXU), pl.BlockSpec(blk, lambda l, i, core: (l, i, 0)), None)],
                    ("parallel", "parallel"), prefetch=(core,))[0]


def add_chips(own, got, kind, chip, name):
    nf, l, r, c = got.shape
    tr = _stack_tile(r, c)
    wmap = _window_map(kind, r, tr)

    def body(chip_ref, own_ref, *got_refs):
        acc = own_ref[...].astype(F32)
        for ref in got_refs:
            acc = acc + ref[...].astype(F32)
        return (acc,)

    return tilecall(body, name, (l, r // tr),
                    [(own, pl.BlockSpec((None, tr, c), lambda ll, i, chip: wmap(ll, i, chip)))] +
                    [(got, pl.BlockSpec((None, None, tr, c), lambda ll, i, chip, f=f: (f, ll, i, 0))) for f in range(nf)],
                    [(_sds((l, r, c)), pl.BlockSpec((None, tr, c), lambda ll, i, chip: (ll, i, 0)), None)],
                    ("parallel", "parallel"), prefetch=(chip,))[0]


def _adam_update(g, w, m, v):
    m_new = ADAM_B1 * m + (1.0 - ADAM_B1) * g
    v_new = ADAM_B2 * v + (1.0 - ADAM_B2) * jnp.square(g)
    m_hat = m_new / (1.0 - ADAM_B1 ** ADAM_STEP)
    v_hat = v_new / (1.0 - ADAM_B2 ** ADAM_STEP)
    delta = -ADAM_LR * (m_hat / (jnp.sqrt(v_hat) + ADAM_EPS) + ADAM_WD * w)
    return g, delta, m_new, v_new


def adamw_stacked(mine, theirs, w, m, v, core, name):
    L, r, c = w.shape
    hl = L // 2
    tr = _stack_tile(r, c)

    def body(core_ref, a_ref, b_ref, w_ref, m_ref, v_ref):
        is_mine = (pl.program_id(0) // hl) == core_ref[0]
        g = jnp.where(is_mine, a_ref[...], b_ref[...])
        return _adam_update(g, w_ref[...], m_ref[...], v_ref[...])

    full = pl.BlockSpec((None, tr, c), lambda l, i, core: (l, i, 0))
    out = (_sds((L, r, c)), full, None)
    return tilecall(body, name, (L, r // tr),
                    [(mine, pl.BlockSpec((None, tr, c), lambda l, i, core: (l % hl, i, 0))),
                     (theirs, pl.BlockSpec((None, None, tr, c), lambda l, i, core: (0, l % hl, i, 0))),
                     (w, full), (m, full), (v, full)],
                    [out, out, out, out], ("parallel", "parallel"), prefetch=(core,))


def _pack(arrs, dtype, row_multiple):
    flat = jnp.concatenate([a.reshape(-1).astype(dtype) for a in arrs])
    rows = -(-flat.shape[0] // LANES)
    rows = -(-rows // row_multiple) * row_multiple
    return jnp.pad(flat, (0, rows * LANES - flat.shape[0])).reshape(rows, LANES)


def _unpack(buf, shapes):
    flat = buf.reshape(-1)
    out, off = [], 0
    for s in shapes:
        n = int(np.prod(s))
        out.append(flat[off:off + n].reshape(s))
        off += n
    return out


def adamw_packed(gparts, w, m, v, name):
    P, R, _ = gparts.shape

    def body(g_ref, w_ref, m_ref, v_ref):
        g = g_ref[0]
        for p in range(1, P):
            g = g + g_ref[p]
        return _adam_update(g, w_ref[...], m_ref[...], v_ref[...])

    whole = pl.BlockSpec((R, LANES), lambda: (0, 0))
    out = (_sds((R, LANES)), whole, None)
    return tilecall(body, name, (), [(gparts, pl.BlockSpec((P, R, LANES), lambda: (0, 0, 0))), (w, whole), (m, whole), (v, whole)],
                    [out, out, out, out], ())


def _ffn_fwd(x, layer, gain, wts, cw8, cb):
    h = rms_fwd(x, gain, "rms_fwd")
    u = mm(h, win(wts["ffn_w_up"], layer), "nn", "mm_nn")
    a = ffn_act_fwd(u, cw8, cb, "ffn_act_fwd")
    y = mm(a, win(wts["ffn_w_down"], layer), "nn", "mm_nn_add", add=x)
    return y, (x, h, u, a)


def _ffn_bwd(dy, saved, layer, gain, wts, cw8, cb, grads):
    x, h, u, a = saved
    grads["ffn_w_down"] = mm(a, dy, "tn", "mm_tn_into", into=win(grads["ffn_w_down"], layer))
    da = mm(dy, win(wts["ffn_w_down"], layer), "nt", "mm_nt")
    dyg, dyu, dcw_g, dcw_u, dcb_g, dcb_u = ffn_act_bwd(u, cw8, cb, da, "ffn_act_bwd")
    dh = None
    for half, du in enumerate(ffn_conv_bwd(dyg, dyu, cw8, "ffn_conv_bwd")):
        cols = dict(col_off=half * D_FF, cols=D_FF)
        grads["ffn_w_up"] = mm(h, du, "tn", "mm_tn_into", into=win(grads["ffn_w_up"], layer, **cols))
        dh = mm(du, win(wts["ffn_w_up"], layer, **cols), "nt", "mm_nt" if dh is None else "mm_nt_add", add=dh)
    dx, d_gain = rms_bwd(x, gain, dh, dy, "rms_bwd")
    return dx, dict(gain=d_gain, conv_w=jnp.concatenate([dcw_g[0:3], dcw_u[0:3]], axis=1),
                    conv_b=jnp.concatenate([dcb_g, dcb_u], axis=1))


def _hgrn_w_in(wts, j, k):
    return win(wts["hgrn_w_in"], j, row_off=k * D_MODEL, rows=D_MODEL)


def _hgrn_layer_fwd(x, j, gain, wts, lb, out_gain, mall):
    h = rms_fwd(x, gain, "rms_fwd")
    z = [mm(h, _hgrn_w_in(wts, j, k), "nn", "mm_nn") for k in range(4)]
    o, states = hgrn_fwd(z[0], z[1], z[2], lb, mall, "hgrn_fwd")
    on = hgrn_out_fwd(o, z[3], out_gain, "hgrn_out_fwd")
    y = mm(on, win(wts["hgrn_w_out"], j), "nn", "mm_nn_add", add=x)
    return y, (x, h, z, o, states, on)


def _hgrn_layer_bwd(dy, saved, j, gain, wts, lb, out_gain, mall, grads):
    x, h, z, o, states, on = saved
    grads["hgrn_w_out"] = mm(on, dy, "tn", "mm_tn_into", into=win(grads["hgrn_w_out"], j))
    don = mm(dy, win(wts["hgrn_w_out"], j), "nt", "mm_nt")
    do, dzg, d_out_gain = hgrn_out_bwd(o, z[3], out_gain, don, "hgrn_out_bwd")
    dzq, dzf, dzi, dlb = hgrn_bwd(z[0], z[1], z[2], lb, mall, states, do, "hgrn_bwd")
    dz = [dzq, dzf, dzi, dzg]
    dh = None
    for k, d in enumerate(dz):
        grads["hgrn_w_in"] = mm(h, d, "tn", "mm_tn_into", into=win(grads["hgrn_w_in"], j, row_off=k * D_MODEL, rows=D_MODEL))
        dh = mm(d, _hgrn_w_in(wts, j, k), "nt", "mm_nt" if dh is None else "mm_nt_add", add=dh)
    dx, d_gain = rms_bwd(x, gain, dh, dy, "rms_bwd")
    return dx, dict(gain=d_gain, lb=dlb, out_gain=d_out_gain)


_MLA_IN_WINDOWS = ((0, MLA_LORA), (MLA_LORA, MLA_LORA), (2 * MLA_LORA, HEAD_DIM))


def _mla_layer_fwd(x, j, gain, wts, qa_gain, kva_gain, qn, kn, cos_t, sin_t):
    h = rms_fwd(x, gain, "rms_fwd")
    cq, ckv, kr = [mm(h, win(wts["mla_w_in"], j, col_off=c0, cols=n), "nn", "mm_nn") for c0, n in _MLA_IN_WINDOWS]
    cqn = rms_fwd(cq, qa_gain, "rms_fwd")
    ckvn = rms_fwd(ckv, kva_gain, "rms_fwd")
    qslots = mm(cqn, win(wts["mla_w_q_up"], j), "nn", "mm_nn")
    kv = mm(ckvn, win(wts["mla_w_kv_up"], j), "nn", "mm_nn")
    qr, krot = qk_fwd(qslots, kv, kr, qn, kn, cos_t, sin_t, "qk_fwd")
    o, lse = attn_fwd(qr, krot, kv, "attn_fwd")
    y = mm(o, win(wts["mla_w_out"], j), "nn", "mm_nn_add", add=x)
    return y, (x, h, cq, ckv, kr, cqn, ckvn, qslots, kv, qr, krot, o, lse)


def _mla_layer_bwd(dy, saved, j, gain, wts, qa_gain, kva_gain, qn, kn, cos_t, sin_t, grads):
    x, h, cq, ckv, kr, cqn, ckvn, qslots, kv, qr, krot, o, lse = saved
    grads["mla_w_out"] = mm(o, dy, "tn", "mm_tn_into", into=win(grads["mla_w_out"], j))
    do = mm(dy, win(wts["mla_w_out"], j), "nt", "mm_nt")
    dq, dk, dv = attn_bwd(qr, krot, kv, o, lse, do, "attn_bwd")
    dqslots, dkv, dkr, d_qn, d_kn = qk_bwd(qslots, kv, kr, qn, kn, cos_t, sin_t, dq, dk, dv, "qk_bwd")
    grads["mla_w_q_up"] = mm(cqn, dqslots, "tn", "mm_tn_into", into=win(grads["mla_w_q_up"], j))
    dcqn = mm(dqslots, win(wts["mla_w_q_up"], j), "nt", "mm_nt")
    grads["mla_w_kv_up"] = mm(ckvn, dkv, "tn", "mm_tn_into", into=win(grads["mla_w_kv_up"], j))
    dckvn = mm(dkv, win(wts["mla_w_kv_up"], j), "nt", "mm_nt")
    dcq, d_qa = rms_bwd(cq, qa_gain, dcqn, None, "rms_bwd")
    dckv, d_kva = rms_bwd(ckv, kva_gain, dckvn, None, "rms_bwd")
    dh = None
    for d, (c0, n) in zip((dcq, dckv, dkr), _MLA_IN_WINDOWS):
        grads["mla_w_in"] = mm(h, d, "tn", "mm_tn_into", into=win(grads["mla_w_in"], j, col_off=c0, cols=n))
        dh = mm(d, win(wts["mla_w_in"], j, col_off=c0, cols=n), "nt", "mm_nt" if dh is None else "mm_nt_add", add=dh)
    dx, d_gain = rms_bwd(x, gain, dh, dy, "rms_bwd")
    return dx, dict(gain=d_gain, qa=d_qa, kva=d_kva, qn=d_qn, kn=d_kn)


BIG = (("hgrn_w_in", "rows"), ("hgrn_w_out", "rows"), ("mla_w_in", "rows"), ("mla_w_q_up", "cols"),
       ("mla_w_kv_up", "cols"), ("mla_w_out", "rows"), ("ffn_w_up", "cols"), ("ffn_w_down", "rows"))
SMALL_SHARDED = (("mla_q_a_norm", 1), ("mla_kv_a_norm", 1), ("ffn_conv_w", 2))
REPLICATED = ("norm_mix", "norm_ffn", "hgrn_lower_bounds", "hgrn_out_norm", "mla_q_norm", "mla_k_norm", "ffn_conv_b")
WEIGHTS = ("norm_mix", "norm_ffn", "hgrn_w_in", "hgrn_lower_bounds", "hgrn_out_norm", "hgrn_w_out", "mla_w_in",
           "mla_q_a_norm", "mla_w_q_up", "mla_kv_a_norm", "mla_w_kv_up", "mla_q_norm", "mla_k_norm", "mla_w_out",
           "ffn_w_up", "ffn_conv_w", "ffn_conv_b", "ffn_w_down")


def _pad_cols(a, width):
    return jnp.pad(a, [(0, 0)] * (a.ndim - 1) + [(0, width - a.shape[-1])])


def _head_slots(w):
    lead, n = w.shape[:-1], w.shape[-1] // MLA_QK
    return _pad_cols(w.reshape(lead + (n, MLA_QK)), MLA_SLOT).reshape(lead + (n * MLA_SLOT,))


def _head_unslots(w):
    lead, n = w.shape[:-1], w.shape[-1] // MLA_SLOT
    return w.reshape(lead + (n, MLA_SLOT))[..., :MLA_QK].reshape(lead + (n * MLA_QK,))


def _to_stack_layout(name, a):
    if name == "hgrn_w_in":
        return a
    if name == "mla_w_in":
        return _pad_cols(a, MLA_IN_COLS)
    if name == "mla_w_q_up":
        return _head_slots(a)
    return a


def _from_stack_layout(name, a):
    if name == "mla_w_in":
        return a[..., :2 * MLA_LORA + MLA_ROPE]
    if name == "mla_w_q_up":
        return _head_unslots(a)
    return a


def _rope_tables(positions):
    inv_freq = ROPE_THETA ** (-jnp.arange(0, MLA_ROPE, 2, dtype=F32) / MLA_ROPE)
    ang = positions.astype(F32)[:, None] * inv_freq
    cos, sin = jnp.cos(ang), jnp.sin(ang)
    S = positions.shape[0]
    ones, zeros = jnp.ones((S, MLA_NOPE), F32), jnp.zeros((S, MLA_SLOT - MLA_QK), F32)
    return (jnp.concatenate([ones, cos, cos, zeros], axis=1),
            jnp.concatenate([jnp.zeros((S, MLA_NOPE), F32), -sin, sin, zeros], axis=1))


def kernel(x, positions, norm_mix, norm_ffn, hgrn_w_in, hgrn_lower_bounds, hgrn_out_norm, hgrn_w_out, mla_w_in, mla_q_a_norm, mla_w_q_up, mla_kv_a_norm, mla_w_kv_up, mla_q_norm, mla_k_norm, mla_w_out, ffn_w_up, ffn_conv_w, ffn_conv_b, ffn_w_down, loss_target, m_norm_mix, m_norm_ffn, m_hgrn_w_in, m_hgrn_lower_bounds, m_hgrn_out_norm, m_hgrn_w_out, m_mla_w_in, m_mla_q_a_norm, m_mla_w_q_up, m_mla_kv_a_norm, m_mla_w_kv_up, m_mla_q_norm, m_mla_k_norm, m_mla_w_out, m_ffn_w_up, m_ffn_conv_w, m_ffn_conv_b, m_ffn_w_down, v_norm_mix, v_norm_ffn, v_hgrn_w_in, v_hgrn_lower_bounds, v_hgrn_out_norm, v_hgrn_w_out, v_mla_w_in, v_mla_q_a_norm, v_mla_w_q_up, v_mla_kv_a_norm, v_mla_w_kv_up, v_mla_q_norm, v_mla_k_norm, v_mla_w_out, v_ffn_w_up, v_ffn_conv_w, v_ffn_conv_b, v_ffn_w_down):
    args = dict(locals())
    w = {n: args[n] for n in WEIGHTS}
    m = {n: args["m_" + n] for n in WEIGHTS}
    v = {n: args["v_" + n] for n in WEIGHTS}
    depth = norm_mix.shape[0]
    x0 = x[0]
    S = x0.shape[0]
    chip = (2 * lax.axis_index("x") + lax.axis_index("y")).astype(jnp.int32).reshape(1)
    core = lax.axis_index("c").astype(jnp.int32).reshape(1)
    big_names = [n for n, _ in BIG]
    kinds = [k for _, k in BIG]
    small_names = [n for n, _ in SMALL_SHARDED]
    small_axis = dict(SMALL_SHARDED)

    local = {n: _to_stack_layout(n, w[n]) for n in big_names}
    gathered = gather_big([local[n].astype(_MXU) for n in big_names], kinds, "gather_weights")
    wts = {n: place(g, local[n], k, chip, "place") for n, k, g in zip(big_names, kinds, gathered)}
    (got_small,) = exchange([_pack([w[n] for n in small_names], F32, SUBLANES)], "chips", [None], "gather_small")
    per_chip = [_unpack(got_small[p], [w[n].shape for n in small_names]) for p in range(N_CHIPS)]
    small = {n: jnp.concatenate([per_chip[p][k] for p in range(N_CHIPS)], axis=small_axis[n]) for k, n in enumerate(small_names)}

    cos_t, sin_t = _rope_tables(positions[0])
    lbs = lower_bound_fwd(hgrn_lower_bounds, "lower_bound_fwd")
    mall = jnp.asarray(_hgrn_sum_matrix(min(HGRN_CHUNK, S)), _MXU)
    qn = _pad_cols(mla_q_norm, MLA_SLOT)
    kn = _pad_cols(mla_k_norm, MLA_SLOT)
    cw8 = jnp.pad(small["ffn_conv_w"], ((0, 0), (0, SUBLANES - 3), (0, 0)))

    def mixer_args(layer):
        j = layer // 2
        if layer % 2 == 0:
            return (j, norm_mix[layer:layer + 1], wts, lbs[j:j + 1], hgrn_out_norm[j:j + 1], mall)
        return (j, norm_mix[layer:layer + 1], wts, small["mla_q_a_norm"][j:j + 1], small["mla_kv_a_norm"][j:j + 1],
                qn[j:j + 1], kn[j:j + 1], cos_t, sin_t)

    def ffn_args(layer):
        return (layer, norm_ffn[layer:layer + 1], wts, cw8[layer], ffn_conv_b[layer:layer + 1])

    h = x0
    saved = []
    for layer in range(depth):
        fwd = _hgrn_layer_fwd if layer % 2 == 0 else _mla_layer_fwd
        h, s_mix = fwd(h, *mixer_args(layer))
        h, s_ffn = _ffn_fwd(h, *ffn_args(layer))
        saved.append((s_mix, s_ffn))

    dh, loss_blk = loss_head(h, loss_target[0], "loss_head")
    loss = lax.psum(loss_blk[0, 0], MESH_AXES)

    grads = {n: jnp.zeros(g.shape, F32) for n, g in zip(big_names, gathered)}
    g_mix, g_ffn = [None] * depth, [None] * depth
    for layer in reversed(range(depth)):
        s_mix, s_ffn = saved[layer]
        dh, g_ffn[layer] = _ffn_bwd(dh, s_ffn, *ffn_args(layer), grads)
        bwd = _hgrn_layer_bwd if layer % 2 == 0 else _mla_layer_bwd
        dh, g_mix[layer] = bwd(dh, s_mix, *mixer_args(layer), grads)
    hg = [g_mix[l] for l in range(0, depth, 2)]
    mg = [g_mix[l] for l in range(1, depth, 2)]
    d_p = lower_bound_bwd(hgrn_lower_bounds, hg[1]["lb"], "lower_bound_bwd")

    g_list = [grads[n] for n in big_names]
    from_core = send_other_half(g_list, "reduce_cores_in")
    chip_sums = [add_cores(g, o, core, "add_cores") for g, o in zip(g_list, from_core)]
    from_chips = scatter_to_chips(chip_sums, kinds, "reduce_chips")
    reduced = [add_chips(own, got, k, chip, "add_chips") for own, got, k in zip(chip_sums, from_chips, kinds)]
    other_half = exchange(reduced, "sibling", [None] * len(reduced), "reduce_cores_out", keep_own=False)
    big_out = {}
    for n, mine, theirs in zip(big_names, reduced, other_half):
        outs = adamw_stacked(mine, theirs, local[n], _to_stack_layout(n, m[n]), _to_stack_layout(n, v[n]), core, "adamw")
        big_out[n] = [_from_stack_layout(n, o) for o in outs]

    small_grads = {
        "norm_mix": jnp.concatenate([g["gain"] for g in g_mix], axis=0),
        "norm_ffn": jnp.concatenate([g["gain"] for g in g_ffn], axis=0),
        "hgrn_lower_bounds": d_p[0:2],
        "hgrn_out_norm": jnp.concatenate([g["out_gain"] for g in hg], axis=0),
        "mla_q_a_norm": jnp.concatenate([g["qa"] for g in mg], axis=0),
        "mla_kv_a_norm": jnp.concatenate([g["kva"] for g in mg], axis=0),
        "mla_q_norm": jnp.concatenate([g["qn"][:, :MLA_QK] for g in mg], axis=0),
        "mla_k_norm": jnp.concatenate([g["kn"][:, :MLA_QK] for g in mg], axis=0),
        "ffn_conv_w": jnp.stack([g["conv_w"] for g in g_ffn]),
        "ffn_conv_b": jnp.concatenate([g["conv_b"] for g in g_ffn], axis=0),
    }

    def chip_part(n, p):
        size = w[n].shape[small_axis[n]]
        return lax.slice_in_dim(small_grads[n], p * size, (p + 1) * size, axis=small_axis[n])

    to_chips = jnp.stack([_pack([chip_part(n, p) for n in small_names], F32, SUBLANES) for p in range(N_CHIPS)])
    rep_g, shard_g = exchange([_pack([small_grads[n] for n in REPLICATED], F32, SUBLANES), to_chips], "all",
                              [None, (2, 1, 0)], "reduce_small")
    small_out = {}
    for names, gparts in ((REPLICATED, rep_g), (small_names, shard_g)):
        packed = adamw_packed(gparts, *[_pack([t[n] for n in names], F32, SUBLANES) for t in (w, m, v)], "adamw_small")
        unpacked = [_unpack(buf, [w[n].shape for n in names]) for buf in packed]
        for k, n in enumerate(names):
            small_out[n] = [u[k] for u in unpacked]

    result = [loss, dh[None]]
    for k in range(4):
        result += [(big_out[n] if n in big_out else small_out[n])[k] for n in WEIGHTS]
    return tuple(result)
```

```python
import numpy as np
import jax
import jax.numpy as jnp
from jax import lax
from jax.experimental import pallas as pl
from jax.experimental.pallas import tpu as pltpu

F32 = jnp.float32
BF16 = jnp.bfloat16
_MXU = BF16

RMS_EPS = 1e-6
D_MODEL = 1024
HEADS = 8
HEAD_DIM = 128
HGRN_CHUNK = 64
MLA_NOPE = 128
MLA_ROPE = 64
MLA_QK = MLA_NOPE + MLA_ROPE
MLA_SLOT = 256
MLA_LORA = 256
MLA_IN_COLS = 2 * MLA_LORA + HEAD_DIM
ROPE_THETA = 10000.0
D_FF = 2816
FF_BLOCK = 1408
LANES = 128
SUBLANES = 8

ADAM_LR = 0.001
ADAM_B1 = 0.9
ADAM_B2 = 0.999
ADAM_EPS = 1e-08
ADAM_WD = 0.01
ADAM_STEP = 10

VMEM_LIMIT = 56 * 1024 * 1024
MESH_AXES = ("x", "y", "c")
N_CHIPS = 4

_NN = ((1,), (0,))
_NT = ((1,), (1,))
_TN = ((0,), (0,))


def _dg(a, b, dims):
    return lax.dot_general(a.astype(_MXU), b.astype(_MXU), (dims, ((), ())), preferred_element_type=F32)


@jax.custom_vjp
def kdot(a, b):
    return _dg(a, b, _NN)


kdot.defvjp(lambda a, b: (_dg(a, b, _NN), (a, b)), lambda r, g: (_dg(g, r[1], _NT), _dg(r[0], g, _TN)))


@jax.custom_vjp
def kdot_nt(a, b):
    return _dg(a, b, _NT)


kdot_nt.defvjp(lambda a, b: (_dg(a, b, _NT), (a, b)), lambda r, g: (_dg(g, r[1], _NN), _dg(g, r[0], _TN)))


@jax.custom_vjp
def kdot_tn(a, b):
    return _dg(a, b, _TN)


kdot_tn.defvjp(lambda a, b: (_dg(a, b, _TN), (a, b)), lambda r, g: (_dg(r[1], g, _NT), _dg(r[0], g, _NN)))


def _pick(d, prefs):
    for p in prefs:
        if d >= p and d % p == 0:
            return p
    return d


def _params(sem):
    return pltpu.CompilerParams(dimension_semantics=sem, vmem_limit_bytes=VMEM_LIMIT)


def _sds(shape, dtype=F32):
    return jax.ShapeDtypeStruct(shape, dtype)


def win(arr, layer, row_off=0, col_off=0, rows=None, cols=None):
    return (arr, layer, row_off, col_off, rows or arr.shape[1] - row_off, cols or arr.shape[2] - col_off)


def mm(a, b, mode, name, add=None, out_dtype=F32, into=None):
    if isinstance(b, tuple):
        b_arr, b_layer, b_r0, b_c0, b_rows, b_cols = b
    else:
        b_arr, b_layer, b_r0, b_c0, (b_rows, b_cols) = b, None, 0, 0, b.shape
    if mode == "nn":
        (M, K), (K2, N) = a.shape, (b_rows, b_cols)
    elif mode == "nt":
        (M, K), (N, K2) = a.shape, (b_rows, b_cols)
    else:
        (K, M), (K2, N) = a.shape, (b_rows, b_cols)
    assert K == K2, (name, a.shape, b_rows, b_cols)
    tm = M if M <= 1024 else _pick(M, (1024, 1408, 512, 256, 128))
    tn = N if N <= 1024 else _pick(N, (1024, 1408, 512, 256, 128))
    tk = K if K <= 2048 else _pick(K, (2048, 2816, 1024, 512, 256, 128))
    nk = K // tk
    dims = {"nn": _NN, "nt": _NT, "tn": _TN}[mode]
    a_spec = pl.BlockSpec((tk, tm), lambda i, j, k: (k, i)) if mode == "tn" else pl.BlockSpec((tm, tk), lambda i, j, k: (i, k))
    b_blk = (tn, tk) if mode == "nt" else (tk, tn)
    assert b_r0 % b_blk[0] == 0 and b_c0 % b_blk[1] == 0, (name, b_r0, b_c0, b_blk)
    br, bc = b_r0 // b_blk[0], b_c0 // b_blk[1]
    if mode == "nt":
        b_idx = lambda i, j, k: (br + j, bc + k)
    else:
        b_idx = lambda i, j, k: (br + k, bc + j)
    if b_layer is None:
        b_spec = pl.BlockSpec(b_blk, b_idx)
    else:
        b_spec = pl.BlockSpec((None,) + b_blk, lambda i, j, k: (b_layer,) + b_idx(i, j, k))
    plain = pl.BlockSpec((tm, tn), lambda i, j, k: (i, j))
    has_add = add is not None
    ins = [a, b_arr] + ([add] if has_add else [])
    specs = [a_spec, b_spec] + ([plain] if has_add else [])
    aliases = {}
    if into is None:
        o_spec, out_shape = plain, _sds((M, N), out_dtype)
    else:
        buf, o_layer, o_r0, o_c0, o_rows, o_cols = into
        assert (o_rows, o_cols) == (M, N) and o_r0 % tm == 0 and o_c0 % tn == 0, (name, into[1:], M, N, tm, tn)
        orow, ocol = o_r0 // tm, o_c0 // tn
        o_spec = pl.BlockSpec((None, tm, tn), lambda i, j, k: (o_layer, orow + i, ocol + j))
        out_shape = _sds(buf.shape, buf.dtype)
        aliases = {len(ins): 0}
        ins.append(buf)
        specs.append(pl.BlockSpec(memory_space=pl.ANY))

    def kern(*refs):
        a_ref, b_ref = refs[0], refs[1]
        add_ref = refs[2] if has_add else None

        def finish(r, o_ref):
            if has_add:
                r = r + add_ref[...].astype(F32)
            o_ref[...] = r.astype(o_ref.dtype)

        if nk == 1:
            finish(_dg(a_ref[...], b_ref[...], dims), refs[-1])
            return
        o_ref, acc = refs[-2], refs[-1]
        k = pl.program_id(2)

        @pl.when(k == 0)
        def _():
            acc[...] = jnp.zeros_like(acc)

        acc[...] += _dg(a_ref[...], b_ref[...], dims)

        @pl.when(k == nk - 1)
        def _():
            finish(acc[...], o_ref)

    return pl.pallas_call(
        kern, name=name, grid=(M // tm, N // tn, nk), in_specs=specs, out_specs=o_spec, out_shape=out_shape,
        scratch_shapes=[pltpu.VMEM((tm, tn), F32)] if nk > 1 else [], input_output_aliases=aliases,
        compiler_params=_params(("parallel", "parallel", "arbitrary")))(*ins)


def _store(ref, val, first):
    if first is None:
        ref[...] = val.astype(ref.dtype)
        return

    @pl.when(first)
    def _():
        ref[...] = val.astype(ref.dtype)

    @pl.when(jnp.logical_not(first))
    def _():
        ref[...] += val.astype(ref.dtype)


def tilecall(body, name, grid, ins, outs, sem, prefetch=(), aliases=None):
    n_pre, n_in = len(prefetch), len(ins)

    def kern(*refs):
        vals = body(*refs[:n_pre + n_in])
        for ref, val, (_, _, first) in zip(refs[n_pre + n_in:], vals, outs):
            _store(ref, val, None if first is None else first())

    in_specs, out_specs = [s for _, s in ins], [s for _, s, _ in outs]
    kwargs = dict(name=name, out_shape=[sh for sh, _, _ in outs], compiler_params=_params(sem),
                  input_output_aliases={n_pre + k: v for k, v in (aliases or {}).items()})
    if n_pre:
        kwargs["grid_spec"] = pltpu.PrefetchScalarGridSpec(num_scalar_prefetch=n_pre, grid=grid, in_specs=in_specs,
                                                           out_specs=out_specs)
    else:
        kwargs.update(grid=grid, in_specs=in_specs, out_specs=out_specs)
    return pl.pallas_call(kern, **kwargs)(*prefetch, *[a for a, _ in ins])


def _rms(x, g, n):
    ms = jnp.sum(x * x, axis=-1, keepdims=True) / n
    return x * lax.rsqrt(ms + RMS_EPS) * g


def _row_tile(S, w):
    return min(S, 512 if w <= 1024 else 256)


def rms_fwd(x, g, name, col=0, w=None):
    S = x.shape[0]
    w = w or x.shape[1]
    ts = _row_tile(S, w)
    return tilecall(
        lambda x_ref, g_ref: (_rms(x_ref[...], g_ref[...], w),), name, (S // ts,),
        [(x, pl.BlockSpec((ts, w), lambda i: (i, col))), (g, pl.BlockSpec((1, w), lambda i: (0, 0)))],
        [(_sds((S, w), _MXU), pl.BlockSpec((ts, w), lambda i: (i, 0)), None)], ("parallel",))[0]


def rms_bwd(x, g, dh, res, name, w=None):
    S = x.shape[0]
    w = w or x.shape[1]
    ts = _row_tile(S, w)

    def body(x_ref, g_ref, dh_ref, *rest):
        _, vjp = jax.vjp(lambda xv, gv: _rms(xv, gv, w), x_ref[...], g_ref[...])
        dx, dg = vjp(dh_ref[...].astype(F32))
        if rest:
            dx = dx + rest[0][...]
        return dx, dg

    row = pl.BlockSpec((ts, w), lambda i: (i, 0))
    vec = pl.BlockSpec((1, w), lambda i: (0, 0))
    ins = [(x, row), (g, vec), (dh, row)] + ([(res, row)] if res is not None else [])
    return tilecall(body, name, (S // ts,), ins,
                    [(_sds((S, w)), row, None), (_sds((1, w)), vec, lambda: pl.program_id(0) == 0)], ("arbitrary",))


def _shifted(u, halo_ref, is_first):
    rid = lax.broadcasted_iota(jnp.int32, (u.shape[0], 1), 0)
    h7 = jnp.where(is_first, 0.0, halo_ref[7:8, :])
    h6 = jnp.where(is_first, 0.0, halo_ref[6:7, :])
    u1 = jnp.where(rid == 0, h7, pltpu.roll(u, 1, 0))
    u2 = jnp.where(rid == 0, h6, jnp.where(rid == 1, h7, pltpu.roll(u, 2, 0)))
    return u1, u2


def _conv(u, u1, u2, cw_ref, cb_ref):
    return ((cb_ref[...] + u2 * cw_ref[0:1, :]) + u1 * cw_ref[1:2, :]) + u * cw_ref[2:3, :]


def _ffn_specs(S, ts, jmap):
    hb = ts // SUBLANES
    return (pl.BlockSpec((ts, FF_BLOCK), lambda j, i: (i, jmap(j))),
            pl.BlockSpec((SUBLANES, FF_BLOCK), lambda j, i: (jnp.maximum(i * hb - 1, 0), jmap(j))),
            pl.BlockSpec((SUBLANES, FF_BLOCK), lambda j, i: (0, jmap(j))),
            pl.BlockSpec((1, FF_BLOCK), lambda j, i: (0, jmap(j))))


def ffn_act_fwd(u, cw8, cb, name):
    S = u.shape[0]
    ts = _row_tile(S, 2 * D_FF)
    nb = D_FF // FF_BLOCK

    def body(ug, hg, cwg, cbg, uu, hu, cwu, cbu):
        first = pl.program_id(1) == 0
        g = ug[...]
        g1, g2 = _shifted(g, hg, first)
        yg = _conv(g, g1, g2, cwg, cbg)
        v = uu[...]
        v1, v2 = _shifted(v, hu, first)
        yu = _conv(v, v1, v2, cwu, cbu)
        return (yg * jax.nn.sigmoid(yg) * yu,)

    sg = _ffn_specs(S, ts, lambda j: j)
    su = _ffn_specs(S, ts, lambda j: j + nb)
    ins = [(u, sg[0]), (u, sg[1]), (cw8, sg[2]), (cb, sg[3]), (u, su[0]), (u, su[1]), (cw8, su[2]), (cb, su[3])]
    return tilecall(body, name, (nb, S // ts), ins,
                    [(_sds((S, D_FF), _MXU), pl.BlockSpec((ts, FF_BLOCK), lambda j, i: (i, j)), None)],
                    ("parallel", "parallel"))[0]


def ffn_act_bwd(u, cw8, cb, da, name):
    S = u.shape[0]
    ts = _row_tile(S, 2 * D_FF)
    nb = D_FF // FF_BLOCK

    def taps(dy, x, x1, x2):
        return jnp.concatenate(
            [jnp.sum(dy * x2, axis=0, keepdims=True), jnp.sum(dy * x1, axis=0, keepdims=True),
             jnp.sum(dy * x, axis=0, keepdims=True), jnp.zeros((SUBLANES - 3, dy.shape[1]), F32)], axis=0)

    def body(ug, hg, cwg, cbg, uu, hu, cwu, cbu, da_ref):
        first = pl.program_id(1) == 0
        g = ug[...]
        g1, g2 = _shifted(g, hg, first)
        yg = _conv(g, g1, g2, cwg, cbg)
        v = uu[...]
        v1, v2 = _shifted(v, hu, first)
        yu = _conv(v, v1, v2, cwu, cbu)
        d = da_ref[...]
        sg = jax.nn.sigmoid(yg)
        dyg = d * yu * (sg * (1.0 + yg * (1.0 - sg)))
        dyu = d * (yg * sg)
        return (dyg, dyu, taps(dyg, g, g1, g2), taps(dyu, v, v1, v2),
                jnp.sum(dyg, axis=0, keepdims=True), jnp.sum(dyu, axis=0, keepdims=True))

    sg_ = _ffn_specs(S, ts, lambda j: j)
    su_ = _ffn_specs(S, ts, lambda j: j + nb)
    row = pl.BlockSpec((ts, FF_BLOCK), lambda j, i: (i, j))
    ins = [(u, sg_[0]), (u, sg_[1]), (cw8, sg_[2]), (cb, sg_[3]), (u, su_[0]), (u, su_[1]), (cw8, su_[2]), (cb, su_[3]), (da, row)]
    first_row = lambda: pl.program_id(1) == 0
    dy, dcw, dcb = (_sds((S, D_FF)), row, None), (_sds((SUBLANES, D_FF)), sg_[2], first_row), (_sds((1, D_FF)), sg_[3], first_row)
    return tilecall(body, name, (nb, S // ts), ins, [dy, dy, dcw, dcw, dcb, dcb], ("parallel", "arbitrary"))


def ffn_conv_bwd(dyg, dyu, cw8, name):
    S = dyg.shape[0]
    ts = _row_tile(S, 2 * D_FF)
    hb = ts // SUBLANES
    nrow = S // ts
    nb = D_FF // FF_BLOCK

    def back(dy_ref, halo_ref, cw_ref):
        last = pl.program_id(1) == nrow - 1
        d = dy_ref[...]
        rid = lax.broadcasted_iota(jnp.int32, (ts, 1), 0)
        n0 = jnp.where(last, 0.0, halo_ref[0:1, :])
        n1 = jnp.where(last, 0.0, halo_ref[1:2, :])
        d1 = jnp.where(rid == ts - 1, n0, pltpu.roll(d, ts - 1, 0))
        d2 = jnp.where(rid == ts - 1, n1, jnp.where(rid == ts - 2, n0, pltpu.roll(d, ts - 2, 0)))
        return d * cw_ref[2:3, :] + d1 * cw_ref[1:2, :] + d2 * cw_ref[0:1, :]

    row = pl.BlockSpec((ts, FF_BLOCK), lambda j, i: (i, j))
    halo = pl.BlockSpec((SUBLANES, FF_BLOCK), lambda j, i: (jnp.minimum((i + 1) * hb, S // SUBLANES - 1), j))
    ins = [(dyg, row), (dyg, halo), (cw8, pl.BlockSpec((SUBLANES, FF_BLOCK), lambda j, i: (0, j))),
           (dyu, row), (dyu, halo), (cw8, pl.BlockSpec((SUBLANES, FF_BLOCK), lambda j, i: (0, j + nb)))]
    out = (_sds((S, D_FF), _MXU), row, None)
    return tilecall(lambda a, b, c, d, e, f: (back(a, b, c), back(d, e, f)), name, (nb, nrow), ins, [out, out],
                    ("parallel", "parallel"))


def _hgrn_levels(C):
    out, m = [], C // 2
    while m >= 1:
        out.append(m)
        m //= 2
    return out


def _hgrn_sum_matrix(C):
    t = np.arange(C)[:, None]
    u = np.arange(C)[None, :]
    blocks = [u <= t, u > t]
    for m in _hgrn_levels(C):
        r = (t // (2 * m)) * (2 * m) + m
        right = (t % (2 * m)) >= m
        blocks.append((right & (u > r) & (u <= t)) | ((~right) & (u > t) & (u <= r)))
    return np.concatenate(blocks, axis=0).astype(np.float32)


def _make_partial_sums(nb, C):
    @jax.custom_vjp
    def sums(mall, lf):
        hi = lf.astype(_MXU)
        mid = (lf - hi.astype(F32)).astype(_MXU)
        e2 = _dg(mall, jnp.concatenate([hi, mid], axis=1), _NN)
        e = e2[:, :HEAD_DIM] + e2[:, HEAD_DIM:]
        return tuple(e[b * C:(b + 1) * C] for b in range(nb))

    def fwd(mall, lf):
        return sums(mall, lf), mall

    def bwd(mall, gs):
        return jnp.zeros_like(mall), _dg(mall, jnp.concatenate(gs, axis=0), _TN)

    sums.defvjp(fwd, bwd)
    return sums


def _hgrn_chunk(zq, zf, v, lb, st, mall, C):
    levels = _hgrn_levels(C)
    qs = zq * jax.nn.sigmoid(zq)
    fg = lb + (1.0 - lb) * jax.nn.sigmoid(zf)
    k = 1.0 - fg
    e = _make_partial_sums(2 + len(levels), C)(mall, jnp.log(fg))
    g_incl, g_after = e[0], e[1]
    rid = lax.broadcasted_iota(jnp.int32, (C, 1), 0)
    tt = lax.broadcasted_iota(jnp.int32, (C, C), 0)
    ss = lax.broadcasted_iota(jnp.int32, (C, C), 1)
    o = kdot_nt(qs * jnp.exp(g_incl), st)
    o = o + jnp.sum(qs * k, axis=-1, keepdims=True) * v
    scores = jnp.zeros((C, C), F32)
    for li, m in enumerate(levels):
        sh = int(np.log2(m))
        right = ((rid >> sh) & 1) == 1
        both = jnp.where(right, qs, k) * jnp.exp(e[2 + li])
        pair = ((tt >> (sh + 1)) == (ss >> (sh + 1))) & (((tt >> sh) & 1) == 1) & (((ss >> sh) & 1) == 0)
        scores = scores + jnp.where(pair, kdot_nt(both, both), 0.0)
    o = o + kdot(scores, v)
    g_last = jnp.sum(jnp.where(rid == C - 1, g_incl, 0.0), axis=0, keepdims=True)
    st_new = st * jnp.exp(g_last) + kdot_tn(v, k * jnp.exp(g_after))
    return o, st_new


HGRN_HEADS_PER_STEP = 4
_HGRN_LANES = HGRN_HEADS_PER_STEP * HEAD_DIM


def _hgrn_in_specs(C, nc, rev):
    cm = (lambda c: nc - 1 - c) if rev else (lambda c: c)
    blk = lambda: pl.BlockSpec((C, _HGRN_LANES), lambda h, c: (cm(c), h))
    return cm, [blk(), blk(), blk(), pl.BlockSpec((1, _HGRN_LANES), lambda h, c: (0, h))]


def _hgrn_state_spec(cm):
    return pl.BlockSpec((HGRN_HEADS_PER_STEP, None, HEAD_DIM, HEAD_DIM), lambda h, c: (h, cm(c), 0, 0))


def hgrn_fwd(zq, zf, zi, lb, mall, name):
    S = zq.shape[0]
    C = min(HGRN_CHUNK, S)
    nc = S // C

    def kern(zq_ref, zf_ref, zi_ref, lb_ref, mall_ref, o_ref, st_ref, st):
        @pl.when(pl.program_id(1) == 0)
        def _():
            st[...] = jnp.zeros_like(st)

        mall_v = mall_ref[...]
        for g in range(HGRN_HEADS_PER_STEP):
            lanes = slice(g * HEAD_DIM, (g + 1) * HEAD_DIM)
            s_in = st[g]
            st_ref[g] = s_in
            o, s_new = _hgrn_chunk(zq_ref[:, lanes], zf_ref[:, lanes], zi_ref[:, lanes], lb_ref[:, lanes], s_in, mall_v, C)
            o_ref[:, lanes] = o
            st[g] = s_new

    cm, specs = _hgrn_in_specs(C, nc, False)
    return pl.pallas_call(
        kern, name=name, grid=(HEADS // HGRN_HEADS_PER_STEP, nc),
        in_specs=specs + [pl.BlockSpec(mall.shape, lambda h, c: (0, 0))],
        out_specs=[pl.BlockSpec((C, _HGRN_LANES), lambda h, c: (c, h)), _hgrn_state_spec(cm)],
        out_shape=[_sds((S, D_MODEL)), _sds((HEADS, nc, HEAD_DIM, HEAD_DIM))],
        scratch_shapes=[pltpu.VMEM((HGRN_HEADS_PER_STEP, HEAD_DIM, HEAD_DIM), F32)],
        compiler_params=_params(("parallel", "arbitrary")))(zq, zf, zi, lb, mall)


def hgrn_bwd(zq, zf, zi, lb, mall, states, do, name):
    S = zq.shape[0]
    C = min(HGRN_CHUNK, S)
    nc = S // C

    def kern(zq_ref, zf_ref, zi_ref, lb_ref, mall_ref, st_ref, do_ref, dq_ref, df_ref, di_ref, dlb_ref, dst):
        first = pl.program_id(1) == 0

        @pl.when(first)
        def _():
            dst[...] = jnp.zeros_like(dst)

        mall_v = mall_ref[...]
        gls = []
        for g in range(HGRN_HEADS_PER_STEP):
            lanes = slice(g * HEAD_DIM, (g + 1) * HEAD_DIM)
            _, vjp = jax.vjp(lambda a, b, c, d, e: _hgrn_chunk(a, b, c, d, e, mall_v, C),
                             zq_ref[:, lanes], zf_ref[:, lanes], zi_ref[:, lanes], lb_ref[:, lanes], st_ref[g])
            ga, gb, gv, gl, gs = vjp((do_ref[:, lanes], dst[g]))
            dq_ref[:, lanes] = ga.astype(dq_ref.dtype)
            df_ref[:, lanes] = gb.astype(df_ref.dtype)
            di_ref[:, lanes] = gv.astype(di_ref.dtype)
            gls.append(gl)
            dst[g] = gs
        _store(dlb_ref, jnp.concatenate(gls, axis=1), first)

    cm, specs = _hgrn_in_specs(C, nc, True)
    row = lambda: pl.BlockSpec((C, _HGRN_LANES), lambda h, c: (cm(c), h))
    return pl.pallas_call(
        kern, name=name, grid=(HEADS // HGRN_HEADS_PER_STEP, nc),
        in_specs=specs + [pl.BlockSpec(mall.shape, lambda h, c: (0, 0)), _hgrn_state_spec(cm), row()],
        out_specs=[row(), row(), row(), pl.BlockSpec((1, _HGRN_LANES), lambda h, c: (0, h))],
        out_shape=[_sds((S, D_MODEL), _MXU), _sds((S, D_MODEL), _MXU), _sds((S, D_MODEL), _MXU), _sds((1, D_MODEL))],
        scratch_shapes=[pltpu.VMEM((HGRN_HEADS_PER_STEP, HEAD_DIM, HEAD_DIM), F32)],
        compiler_params=_params(("parallel", "arbitrary")))(zq, zf, zi, lb, mall, states, do)


def _hgrn_out(o, g, gain):
    return _rms(o, gain, HEAD_DIM) * (g * jax.nn.sigmoid(g))


def hgrn_out_fwd(o, zg, gain, name):
    S = o.shape[0]
    ts = _row_tile(S, D_MODEL)
    blk = pl.BlockSpec((ts, HEAD_DIM), lambda i, h: (i, h))
    return tilecall(lambda o_ref, g_ref, w_ref: (_hgrn_out(o_ref[...], g_ref[...], w_ref[...]),), name, (S // ts, HEADS),
                    [(o, blk), (zg, blk), (gain, pl.BlockSpec((1, HEAD_DIM), lambda i, h: (0, 0)))],
                    [(_sds((S, D_MODEL), _MXU), blk, None)], ("parallel", "parallel"))[0]


def hgrn_out_bwd(o, zg, gain, don, name):
    S = o.shape[0]
    ts = _row_tile(S, D_MODEL)
    blk = pl.BlockSpec((ts, HEAD_DIM), lambda i, h: (i, h))
    vec = pl.BlockSpec((1, HEAD_DIM), lambda i, h: (0, 0))

    def body(o_ref, g_ref, w_ref, d_ref):
        _, vjp = jax.vjp(_hgrn_out, o_ref[...], g_ref[...], w_ref[...])
        return vjp(d_ref[...])

    return tilecall(body, name, (S // ts, HEADS), [(o, blk), (zg, blk), (gain, vec), (don, blk)],
                    [(_sds((S, D_MODEL)), blk, None), (_sds((S, D_MODEL), _MXU), blk, None),
                     (_sds((1, HEAD_DIM)), vec, lambda: (pl.program_id(0) == 0) & (pl.program_id(1) == 0))],
                    ("arbitrary", "arbitrary"))


def _lb_soft(p0, p1):
    mx = jnp.maximum(p0, p1)
    e0, e1 = jnp.exp(p0 - mx), jnp.exp(p1 - mx)
    s0, s1 = e0 / (e0 + e1), e1 / (e0 + e1)
    return (s0 + s1) - s0


def lower_bound_fwd(p, name):
    assert p.shape[0] == 2

    def body(p_ref):
        s = _lb_soft(p_ref[0:1, :], p_ref[1:2, :])
        return (jnp.concatenate([jnp.zeros_like(s), s] + [jnp.zeros_like(s)] * (SUBLANES - 2), axis=0),)

    spec8 = pl.BlockSpec((SUBLANES, p.shape[1]), lambda: (0, 0))
    return tilecall(body, name, (), [(p, pl.BlockSpec(p.shape, lambda: (0, 0)))], [(_sds((SUBLANES, p.shape[1])), spec8, None)], ())[0]


def lower_bound_bwd(p, dlb1, name):
    def body(p_ref, d_ref):
        _, vjp = jax.vjp(_lb_soft, p_ref[0:1, :], p_ref[1:2, :])
        g0, g1 = vjp(d_ref[...])
        return (jnp.concatenate([g0, g1] + [jnp.zeros_like(g0)] * (SUBLANES - 2), axis=0),)

    spec8 = pl.BlockSpec((SUBLANES, p.shape[1]), lambda: (0, 0))
    return tilecall(body, name, (), [(p, pl.BlockSpec(p.shape, lambda: (0, 0))), (dlb1, pl.BlockSpec(dlb1.shape, lambda: (0, 0)))],
                    [(_sds((SUBLANES, p.shape[1])), spec8, None)], ())[0]


@jax.custom_vjp
def _swap_rope_halves(x):
    lane = lax.broadcasted_iota(jnp.int32, x.shape, 1)
    lo = (lane >= MLA_NOPE) & (lane < MLA_NOPE + MLA_ROPE // 2)
    hi = (lane >= MLA_NOPE + MLA_ROPE // 2) & (lane < MLA_QK)
    return jnp.where(lo, pltpu.roll(x, MLA_SLOT - MLA_ROPE // 2, 1), jnp.where(hi, pltpu.roll(x, MLA_ROPE // 2, 1), 0.0))


_swap_rope_halves.defvjp(lambda x: (_swap_rope_halves(x), None), lambda _, g: (_swap_rope_halves(g),))


def _norm_rope(x, gain, cos_t, sin_t):
    y = _rms(x, gain, MLA_QK)
    return y * cos_t + _swap_rope_halves(y) * sin_t


_ATTN_SCALE = MLA_QK ** -0.5


def _qk_heads(qs, kn, kr, qn, kn_gain, cos_t, sin_t):
    q = _norm_rope(qs, qn, cos_t, sin_t) * _ATTN_SCALE
    k = _norm_rope(jnp.concatenate([kn, kr], axis=1), kn_gain, cos_t, sin_t)
    return q, k


def _qk_specs(ts):
    slot = pl.BlockSpec((ts, MLA_SLOT), lambda i, h: (i, h))
    nope = pl.BlockSpec((ts, HEAD_DIM), lambda i, h: (i, 2 * h))
    shared = pl.BlockSpec((ts, HEAD_DIM), lambda i, h: (i, 0))
    gain = pl.BlockSpec((1, MLA_SLOT), lambda i, h: (0, 0))
    table = pl.BlockSpec((ts, MLA_SLOT), lambda i, h: (i, 0))
    return slot, nope, shared, gain, table


def qk_fwd(qslots, kv, krope, qn, kn, cos_t, sin_t, name):
    S = qslots.shape[0]
    ts = _row_tile(S, D_MODEL)
    slot, nope, shared, gain, table = _qk_specs(ts)

    def body(q_ref, kn_ref, kr_ref, qn_ref, kg_ref, c_ref, s_ref):
        return _qk_heads(q_ref[...], kn_ref[...], kr_ref[...], qn_ref[...], kg_ref[...], c_ref[...], s_ref[...])

    out = _sds((S, HEADS * MLA_SLOT), _MXU)
    return tilecall(body, name, (S // ts, HEADS),
                    [(qslots, slot), (kv, nope), (krope, shared), (qn, gain), (kn, gain), (cos_t, table), (sin_t, table)],
                    [(out, slot, None), (out, slot, None)], ("parallel", "parallel"))


def qk_bwd(qslots, kv, krope, qn, kn, cos_t, sin_t, dq, dk, dv, name):
    S = qslots.shape[0]
    ts = _row_tile(S, D_MODEL)
    slot, nope, shared, gain, table = _qk_specs(ts)
    vblk = pl.BlockSpec((ts, HEAD_DIM), lambda i, h: (i, h))

    def body(q_ref, kn_ref, kr_ref, qn_ref, kg_ref, c_ref, s_ref, dq_ref, dk_ref, dv_ref):
        c, s = c_ref[...], s_ref[...]
        _, vjp = jax.vjp(lambda a, b, r, g1, g2: _qk_heads(a, b, r, g1, g2, c, s),
                         q_ref[...], kn_ref[...], kr_ref[...], qn_ref[...], kg_ref[...])
        ga, gb, gr, g1, g2 = vjp((dq_ref[...], dk_ref[...]))
        return ga, jnp.concatenate([gb, dv_ref[...]], axis=1), gr, g1, g2

    first_head = lambda: pl.program_id(1) == 0
    first = lambda: (pl.program_id(0) == 0) & (pl.program_id(1) == 0)
    wide = _sds((S, HEADS * MLA_SLOT), _MXU)
    return tilecall(body, name, (S // ts, HEADS),
                    [(qslots, slot), (kv, nope), (krope, shared), (qn, gain), (kn, gain), (cos_t, table), (sin_t, table),
                     (dq, slot), (dk, slot), (dv, vblk)],
                    [(wide, slot, None), (wide, slot, None), (_sds((S, HEAD_DIM)), shared, first_head),
                     (_sds((1, MLA_SLOT)), gain, first), (_sds((1, MLA_SLOT)), gain, first)], ("arbitrary", "arbitrary"))


ATTN_TILE = 512
ATTN_HEADS_PER_STEP = 2


def _causal_pairs(nq, by_row):
    pairs = [(i, j) for i in range(nq) for j in range(i + 1)] if by_row else [(i, j) for j in range(nq) for i in range(j, nq)]
    return jnp.asarray([p[0] for p in pairs], jnp.int32), jnp.asarray([p[1] for p in pairs], jnp.int32)


def _diag_mask(s, tq):
    rows = lax.broadcasted_iota(jnp.int32, (tq, tq), 0)
    cols = lax.broadcasted_iota(jnp.int32, (tq, tq), 1)
    return jnp.where(rows >= cols, s, -jnp.inf)


def attn_fwd(qr, kr, kv, name):
    S = qr.shape[0]
    tq = min(S, ATTN_TILE)
    nq = S // tq
    i_tab, j_tab = _causal_pairs(nq, True)

    G = ATTN_HEADS_PER_STEP
    reps = tq // LANES

    def kern(it, jt, q_ref, k_ref, kv_ref, o_ref, lse_ref, m_s, l_s, acc):
        n = pl.program_id(1)
        i, j = it[n], jt[n]

        @pl.when(j == 0)
        def _():
            m_s[...] = jnp.full_like(m_s, -jnp.inf)
            l_s[...] = jnp.zeros_like(l_s)
            acc[...] = jnp.zeros_like(acc)

        def step(diagonal):
            for g in range(G):
                slot = slice(g * MLA_SLOT, (g + 1) * MLA_SLOT)
                s = _dg(q_ref[:, slot], k_ref[:, slot], _NT)
                if diagonal:
                    s = _diag_mask(s, tq)
                m_prev = m_s[g]
                m_new = jnp.maximum(m_prev, jnp.max(s, axis=-1, keepdims=True))
                alpha = jnp.exp(m_prev - m_new)
                p = jnp.exp(s - jnp.tile(m_new, (1, reps)))
                l_s[g] = alpha * l_s[g] + jnp.sum(p, axis=-1, keepdims=True)
                acc[g] = alpha * acc[g] + _dg(p, kv_ref[:, g * MLA_SLOT + HEAD_DIM:(g + 1) * MLA_SLOT], _NN)
                m_s[g] = m_new

        @pl.when(j < i)
        def _():
            step(False)

        @pl.when(j == i)
        def _():
            step(True)
            for g in range(G):
                l = l_s[g]
                lanes = slice(g * HEAD_DIM, (g + 1) * HEAD_DIM)
                o_ref[:, lanes] = acc[g] / l
                lse_ref[:, lanes] = m_s[g] + jnp.log(l)

    out = _sds((S, HEADS * HEAD_DIM))
    oblk = pl.BlockSpec((tq, G * HEAD_DIM), lambda h, n, it, jt: (it[n], h))
    stat = pltpu.VMEM((G, tq, HEAD_DIM), F32)
    return pl.pallas_call(
        kern, name=name,
        grid_spec=pltpu.PrefetchScalarGridSpec(
            num_scalar_prefetch=2, grid=(HEADS // G, i_tab.shape[0]),
            in_specs=[pl.BlockSpec((tq, G * MLA_SLOT), lambda h, n, it, jt: (it[n], h)),
                      pl.BlockSpec((tq, G * MLA_SLOT), lambda h, n, it, jt: (jt[n], h)),
                      pl.BlockSpec((tq, G * MLA_SLOT), lambda h, n, it, jt: (jt[n], h))],
            out_specs=[oblk, oblk], scratch_shapes=[stat, stat, stat]),
        out_shape=[out, out], compiler_params=_params(("parallel", "arbitrary")))(i_tab, j_tab, qr, kr, kv)


def attn_bwd(qr, kr, kv, o, lse, do, name):
    S = qr.shape[0]
    tq = min(S, ATTN_TILE)
    nq = S // tq
    i_tab, j_tab = _causal_pairs(nq, False)

    G = ATTN_HEADS_PER_STEP

    def kern(it, jt, q_ref, k_ref, kv_ref, o_ref, lse_ref, do_ref, dq_ref, dk_ref, dv_ref, dk_acc, dv_acc):
        n = pl.program_id(1)
        i, j = it[n], jt[n]

        @pl.when(n == 0)
        def _():
            dq_ref[...] = jnp.zeros_like(dq_ref)

        @pl.when(i == j)
        def _():
            dk_acc[...] = jnp.zeros_like(dk_acc)
            dv_acc[...] = jnp.zeros_like(dv_acc)

        def step(diagonal):
            rows = pl.ds(pl.multiple_of(i * tq, tq), tq)
            for g in range(G):
                slot = slice(g * MLA_SLOT, (g + 1) * MLA_SLOT)
                lanes = slice(g * HEAD_DIM, (g + 1) * HEAD_DIM)
                q, k = q_ref[:, slot], k_ref[:, slot]
                s = _dg(q, k, _NT) - jnp.tile(lse_ref[:, lanes], (1, tq // LANES))
                if diagonal:
                    s = _diag_mask(s, tq)
                p = jnp.exp(s)
                d = do_ref[:, lanes]
                delta = jnp.sum(d * o_ref[:, lanes], axis=-1, keepdims=True)
                dv_acc[:, lanes] += _dg(p, d, _TN)
                ds = p * (_dg(d, kv_ref[:, g * MLA_SLOT + HEAD_DIM:(g + 1) * MLA_SLOT], _NT) - delta)
                dk_acc[:, slot] += _dg(ds, q, _TN)
                dq_ref[rows, slot] += _dg(ds, k, _NN)

        @pl.when(i > j)
        def _():
            step(False)

        @pl.when(i == j)
        def _():
            step(True)

        @pl.when(i == nq - 1)
        def _():
            dk_ref[...] = dk_acc[...]
            dv_ref[...] = dv_acc[...]

    qblk = pl.BlockSpec((tq, G * MLA_SLOT), lambda h, n, it, jt: (it[n], h))
    oblk = pl.BlockSpec((tq, G * HEAD_DIM), lambda h, n, it, jt: (it[n], h))
    kblk = pl.BlockSpec((tq, G * MLA_SLOT), lambda h, n, it, jt: (jt[n], h))
    return pl.pallas_call(
        kern, name=name,
        grid_spec=pltpu.PrefetchScalarGridSpec(
            num_scalar_prefetch=2, grid=(HEADS // G, i_tab.shape[0]),
            in_specs=[qblk, kblk, kblk, oblk, oblk, oblk],
            out_specs=[pl.BlockSpec((S, G * MLA_SLOT), lambda h, n, it, jt: (0, h)), kblk,
                       pl.BlockSpec((tq, G * HEAD_DIM), lambda h, n, it, jt: (jt[n], h))],
            scratch_shapes=[pltpu.VMEM((tq, G * MLA_SLOT), F32), pltpu.VMEM((tq, G * HEAD_DIM), F32)]),
        out_shape=[_sds((S, HEADS * MLA_SLOT)), _sds((S, HEADS * MLA_SLOT)), _sds((S, HEADS * HEAD_DIM))],
        compiler_params=_params(("parallel", "arbitrary")))(i_tab, j_tab, qr, kr, kv, o, lse, do)


def loss_head(y, target, name):
    S, Dm = y.shape
    ts = _row_tile(S, Dm)

    def body(y_ref, t_ref):
        e = y_ref[...] - t_ref[...]
        tot = jnp.sum(jnp.sum(e * e, axis=-1, keepdims=True) / Dm, axis=0, keepdims=True)
        return e / Dm, jnp.broadcast_to(0.5 * tot, (SUBLANES, LANES))

    row = pl.BlockSpec((ts, Dm), lambda i: (i, 0))
    return tilecall(body, name, (S // ts,), [(y, row), (target, row)],
                    [(_sds((S, Dm)), row, None),
                     (_sds((SUBLANES, LANES)), pl.BlockSpec((SUBLANES, LANES), lambda i: (0, 0)), lambda: pl.program_id(0) == 0)],
                    ("arbitrary",))


_CHIP_FLIPS = ((1, 0), (0, 1), (1, 1))
_PEER_FLIPS = {
    "chips": ((1, 0, 0), (0, 1, 0), (1, 1, 0)),
    "sibling": ((0, 0, 1),),
    "all": tuple((a, b, c) for a in (0, 1) for b in (0, 1) for c in (0, 1))[1:],
}
_SLOT_WEIGHTS = {"chips": (2, 1, 0), "sibling": (0, 0, 1), "all": (4, 2, 1)}
_HBM = pl.BlockSpec(memory_space=pltpu.HBM)


def _me():
    return lax.axis_index("x"), lax.axis_index("y"), lax.axis_index("c")


def _remote(src, dst, send_sem, recv_sem, peer):
    return pltpu.make_async_remote_copy(src_ref=src, dst_ref=dst, send_sem=send_sem, recv_sem=recv_sem,
                                        device_id=peer, device_id_type=pl.DeviceIdType.MESH)


def exchange(arrs, group, slab_weights, name, keep_own=True):
    flips = _PEER_FLIPS[group]
    wx, wy, wc = _SLOT_WEIGHTS[group]
    n_slots = len(flips) + (1 if keep_own else 0)
    n = len(arrs)

    def slab(ref, a, pos):
        w = slab_weights[a]
        return ref if w is None else ref.at[w[0] * pos[0] + w[1] * pos[1] + w[2] * pos[2]]

    def kern(*refs):
        srcs, outs = refs[:n], refs[n:2 * n]
        send_sems, recv_sems = refs[2 * n:2 * n + 2]
        me = _me()
        my_slot = wx * me[0] + wy * me[1] + wc * me[2]
        copies = []
        if keep_own:
            local_sems = refs[2 * n + 2]
            for a in range(n):
                cp = pltpu.make_async_copy(slab(srcs[a], a, me), outs[a].at[my_slot], local_sems.at[a])
                cp.start()
                copies.append(cp)
        for f, flip in enumerate(flips):
            peer = tuple(m ^ b if b else m for m, b in zip(me, flip))
            for a in range(n):
                cp = _remote(slab(srcs[a], a, peer), outs[a].at[my_slot if keep_own else f],
                             send_sems.at[f, a], recv_sems.at[f, a], peer)
                cp.start()
                copies.append(cp)
        for cp in copies:
            cp.wait()

    out_shape = [_sds((n_slots,) + (a.shape if slab_weights[k] is None else a.shape[1:]), a.dtype) for k, a in enumerate(arrs)]
    sems = [pltpu.SemaphoreType.DMA((len(flips), n)), pltpu.SemaphoreType.DMA((len(flips), n))]
    return pl.pallas_call(
        kern, name=name, in_specs=[_HBM] * n, out_specs=[_HBM] * n, out_shape=out_shape,
        scratch_shapes=sems + ([pltpu.SemaphoreType.DMA((n,))] if keep_own else []))(*arrs)


def _chip_window(ref, kind, size, chip, layers):
    if kind == "rows":
        return ref.at[layers, pl.ds(chip * size, size), :]
    return ref.at[layers, :, pl.ds(pl.multiple_of(chip * size, LANES), size)]


def gather_big(shards, kinds, name):
    n = len(shards)
    fulls = []
    for s, kind in zip(shards, kinds):
        L, r, c = s.shape
        fulls.append(_sds((L, N_CHIPS * r, c) if kind == "rows" else (L, r, N_CHIPS * c), s.dtype))

    def kern(*refs):
        srcs, outs = refs[:n], refs[n:2 * n]
        ici_s, ici_r, d2d_s, d2d_r = refs[2 * n:]
        x, y, c = _me()
        my_chip = 2 * x + y
        ici, fwd = [], []
        for a in range(n):
            L, r, cc = shards[a].shape
            size = r if kinds[a] == "rows" else cc
            mine = pl.ds(c * (L // 2), L // 2)
            for f, (fx, fy) in enumerate(_CHIP_FLIPS):
                cp = _remote(srcs[a].at[mine], _chip_window(outs[a], kinds[a], size, my_chip, mine),
                             ici_s.at[a, f], ici_r.at[a, f], (x ^ fx, y ^ fy, c))
                cp.start()
                ici.append(cp)
        for a in range(n):
            L, r, cc = shards[a].shape
            size = r if kinds[a] == "rows" else cc
            mine = pl.ds(c * (L // 2), L // 2)
            for f, (fx, fy) in enumerate(_CHIP_FLIPS):
                ici[a * len(_CHIP_FLIPS) + f].wait_recv()
                landed = _chip_window(outs[a], kinds[a], size, 2 * (x ^ fx) + (y ^ fy), mine)
                cp = _remote(landed, landed, d2d_s.at[a, f], d2d_r.at[a, f], (x, y, 1 - c))
                cp.start()
                fwd.append(cp)
        for cp in ici:
            cp.wait_send()
        for cp in fwd:
            cp.wait()

    sem = pltpu.SemaphoreType.DMA((n, len(_CHIP_FLIPS)))
    return pl.pallas_call(kern, name=name, in_specs=[_HBM] * n, out_specs=[_HBM] * n, out_shape=fulls,
                          scratch_shapes=[sem, sem, sem, sem])(*shards)


def send_other_half(arrs, name):
    n = len(arrs)

    def kern(*refs):
        srcs, outs = refs[:n], refs[n:2 * n]
        send_sems, recv_sems = refs[2 * n:]
        x, y, c = _me()
        copies = []
        for a in range(n):
            hl = arrs[a].shape[0] // 2
            cp = _remote(srcs[a].at[pl.ds((1 - c) * hl, hl)], outs[a], send_sems.at[a], recv_sems.at[a], (x, y, 1 - c))
            cp.start()
            copies.append(cp)
        for cp in copies:
            cp.wait()

    return pl.pallas_call(
        kern, name=name, in_specs=[_HBM] * n, out_specs=[_HBM] * n,
        out_shape=[_sds((a.shape[0] // 2,) + a.shape[1:], a.dtype) for a in arrs],
        scratch_shapes=[pltpu.SemaphoreType.DMA((n,)), pltpu.SemaphoreType.DMA((n,))])(*arrs)


def scatter_to_chips(arrs, kinds, name):
    n = len(arrs)
    shapes = []
    for a, kind in zip(arrs, kinds):
        l, R, C = a.shape
        shapes.append((l, R // N_CHIPS, C) if kind == "rows" else (l, R, C // N_CHIPS))

    def kern(*refs):
        srcs, outs = refs[:n], refs[n:2 * n]
        send_sems, recv_sems = refs[2 * n:]
        x, y, c = _me()
        copies = []
        for a in range(n):
            size = shapes[a][1] if kinds[a] == "rows" else shapes[a][2]
            for f, (fx, fy) in enumerate(_CHIP_FLIPS):
                window = _chip_window(srcs[a], kinds[a], size, 2 * (x ^ fx) + (y ^ fy), slice(None))
                cp = _remote(window, outs[a].at[f], send_sems.at[a, f], recv_sems.at[a, f], (x ^ fx, y ^ fy, c))
                cp.start()
                copies.append(cp)
        for cp in copies:
            cp.wait()

    sem = pltpu.SemaphoreType.DMA((n, len(_CHIP_FLIPS)))
    return pl.pallas_call(
        kern, name=name, in_specs=[_HBM] * n, out_specs=[_HBM] * n,
        out_shape=[_sds((len(_CHIP_FLIPS),) + s, a.dtype) for s, a in zip(shapes, arrs)],
        scratch_shapes=[sem, sem])(*arrs)


def _stack_tile(r, c):
    for t in (1024, 704, 512, 352, 256, 128, 64, 32, 16):
        if r % t == 0 and t * c * 4 <= 3 * 512 * 1024:
            return t
    return r


def _window_map(kind, r, tr):
    nrt = r // tr
    if kind == "rows":
        return lambda l, i, chip: (l, chip[0] * nrt + i, 0)
    return lambda l, i, chip: (l, i, chip[0])


def place(full, shard, kind, chip, name):
    L, r, c = shard.shape
    tr = _stack_tile(r, c)
    wmap = _window_map(kind, r, tr)
    return tilecall(lambda chip_ref, s_ref, f_ref: (s_ref[...],), name, (L, r // tr),
                    [(shard, pl.BlockSpec((None, tr, c), lambda l, i, chip: (l, i, 0))), (full, pl.BlockSpec(memory_space=pl.ANY))],
                    [(_sds(full.shape, full.dtype), pl.BlockSpec((None, tr, c), lambda l, i, chip: wmap(l, i, chip)), None)],
                    ("parallel", "parallel"), prefetch=(chip,), aliases={1: 0})[0]


def add_cores(g, other, core, name):
    L, R, C = g.shape
    hl = L // 2
    tr = _stack_tile(R, C)
    blk = (None, tr, C)
    return tilecall(lambda core_ref, a_ref, b_ref: (a_ref[...] + b_ref[...],), name, (hl, R // tr),
                    [(g, pl.BlockSpec(blk, lambda l, i, core: (core[0] * hl + l, i, 0))),
                     (other, pl.BlockSpec(blk, lambda l, i, core: (l, i, 0)))],
                    [(_sds((hl, R, C), _MXU), pl.BlockSpec(blk, lambda l, i, core: (l, i, 0)), None)],
                    ("parallel", "parallel"), prefetch=(core,))[0]


def add_chips(own, got, kind, chip, name):
    nf, l, r, c = got.shape
    tr = _stack_tile(r, c)
    wmap = _window_map(kind, r, tr)

    def body(chip_ref, own_ref, *got_refs):
        acc = own_ref[...].astype(F32)
        for ref in got_refs:
            acc = acc + ref[...].astype(F32)
        return (acc,)

    return tilecall(body, name, (l, r // tr),
                    [(own, pl.BlockSpec((None, tr, c), lambda ll, i, chip: wmap(ll, i, chip)))] +
                    [(got, pl.BlockSpec((None, None, tr, c), lambda ll, i, chip, f=f: (f, ll, i, 0))) for f in range(nf)],
                    [(_sds((l, r, c)), pl.BlockSpec((None, tr, c), lambda ll, i, chip: (ll, i, 0)), None)],
                    ("parallel", "parallel"), prefetch=(chip,))[0]


def _adam_update(g, w, m, v):
    m_new = ADAM_B1 * m + (1.0 - ADAM_B1) * g
    v_new = ADAM_B2 * v + (1.0 - ADAM_B2) * jnp.square(g)
    m_hat = m_new / (1.0 - ADAM_B1 ** ADAM_STEP)
    v_hat = v_new / (1.0 - ADAM_B2 ** ADAM_STEP)
    delta = -ADAM_LR * (m_hat / (jnp.sqrt(v_hat) + ADAM_EPS) + ADAM_WD * w)
    return g, delta, m_new, v_new


def adamw_stacked(mine, theirs, w, m, v, core, name):
    L, r, c = w.shape
    hl = L // 2
    tr = _stack_tile(r, c)

    def body(core_ref, a_ref, b_ref, w_ref, m_ref, v_ref):
        is_mine = (pl.program_id(0) // hl) == core_ref[0]
        g = jnp.where(is_mine, a_ref[...], b_ref[...])
        return _adam_update(g, w_ref[...], m_ref[...], v_ref[...])

    full = pl.BlockSpec((None, tr, c), lambda l, i, core: (l, i, 0))
    out = (_sds((L, r, c)), full, None)
    return tilecall(body, name, (L, r // tr),
                    [(mine, pl.BlockSpec((None, tr, c), lambda l, i, core: (l % hl, i, 0))),
                     (theirs, pl.BlockSpec((None, None, tr, c), lambda l, i, core: (0, l % hl, i, 0))),
                     (w, full), (m, full), (v, full)],
                    [out, out, out, out], ("parallel", "parallel"), prefetch=(core,))


def _pack(arrs, dtype, row_multiple):
    flat = jnp.concatenate([a.reshape(-1).astype(dtype) for a in arrs])
    rows = -(-flat.shape[0] // LANES)
    rows = -(-rows // row_multiple) * row_multiple
    return jnp.pad(flat, (0, rows * LANES - flat.shape[0])).reshape(rows, LANES)


def _unpack(buf, shapes):
    flat = buf.reshape(-1)
    out, off = [], 0
    for s in shapes:
        n = int(np.prod(s))
        out.append(flat[off:off + n].reshape(s))
        off += n
    return out


def adamw_packed(gparts, w, m, v, name):
    P, R, _ = gparts.shape

    def body(g_ref, w_ref, m_ref, v_ref):
        g = g_ref[0]
        for p in range(1, P):
            g = g + g_ref[p]
        return _adam_update(g, w_ref[...], m_ref[...], v_ref[...])

    whole = pl.BlockSpec((R, LANES), lambda: (0, 0))
    out = (_sds((R, LANES)), whole, None)
    return tilecall(body, name, (), [(gparts, pl.BlockSpec((P, R, LANES), lambda: (0, 0, 0))), (w, whole), (m, whole), (v, whole)],
                    [out, out, out, out], ())


def _ffn_fwd(x, layer, gain, wts, cw8, cb):
    h = rms_fwd(x, gain, "rms_fwd")
    u = mm(h, win(wts["ffn_w_up"], layer), "nn", "mm_nn")
    a = ffn_act_fwd(u, cw8, cb, "ffn_act_fwd")
    y = mm(a, win(wts["ffn_w_down"], layer), "nn", "mm_nn_add", add=x)
    return y, (x, h, u, a)


def _ffn_bwd(dy, saved, layer, gain, wts, cw8, cb, grads):
    x, h, u, a = saved
    grads["ffn_w_down"] = mm(a, dy, "tn", "mm_tn_into", into=win(grads["ffn_w_down"], layer))
    da = mm(dy, win(wts["ffn_w_down"], layer), "nt", "mm_nt")
    dyg, dyu, dcw_g, dcw_u, dcb_g, dcb_u = ffn_act_bwd(u, cw8, cb, da, "ffn_act_bwd")
    dh = None
    for half, du in enumerate(ffn_conv_bwd(dyg, dyu, cw8, "ffn_conv_bwd")):
        cols = dict(col_off=half * D_FF, cols=D_FF)
        grads["ffn_w_up"] = mm(h, du, "tn", "mm_tn_into", into=win(grads["ffn_w_up"], layer, **cols))
        dh = mm(du, win(wts["ffn_w_up"], layer, **cols), "nt", "mm_nt" if dh is None else "mm_nt_add", add=dh)
    dx, d_gain = rms_bwd(x, gain, dh, dy, "rms_bwd")
    return dx, dict(gain=d_gain, conv_w=jnp.concatenate([dcw_g[0:3], dcw_u[0:3]], axis=1),
                    conv_b=jnp.concatenate([dcb_g, dcb_u], axis=1))


def _hgrn_w_in(wts, j, k):
    return win(wts["hgrn_w_in"], j, row_off=k * D_MODEL, rows=D_MODEL)


def _hgrn_layer_fwd(x, j, gain, wts, lb, out_gain, mall):
    h = rms_fwd(x, gain, "rms_fwd")
    z = [mm(h, _hgrn_w_in(wts, j, k), "nn", "mm_nn") for k in range(4)]
    o, states = hgrn_fwd(z[0], z[1], z[2], lb, mall, "hgrn_fwd")
    on = hgrn_out_fwd(o, z[3], out_gain, "hgrn_out_fwd")
    y = mm(on, win(wts["hgrn_w_out"], j), "nn", "mm_nn_add", add=x)
    return y, (x, h, z, o, states, on)


def _hgrn_layer_bwd(dy, saved, j, gain, wts, lb, out_gain, mall, grads):
    x, h, z, o, states, on = saved
    grads["hgrn_w_out"] = mm(on, dy, "tn", "mm_tn_into", into=win(grads["hgrn_w_out"], j))
    don = mm(dy, win(wts["hgrn_w_out"], j), "nt", "mm_nt")
    do, dzg, d_out_gain = hgrn_out_bwd(o, z[3], out_gain, don, "hgrn_out_bwd")
    dzq, dzf, dzi, dlb = hgrn_bwd(z[0], z[1], z[2], lb, mall, states, do, "hgrn_bwd")
    dz = [dzq, dzf, dzi, dzg]
    dh = None
    for k, d in enumerate(dz):
        grads["hgrn_w_in"] = mm(h, d, "tn", "mm_tn_into", into=win(grads["hgrn_w_in"], j, row_off=k * D_MODEL, rows=D_MODEL))
        dh = mm(d, _hgrn_w_in(wts, j, k), "nt", "mm_nt" if dh is None else "mm_nt_add", add=dh)
    dx, d_gain = rms_bwd(x, gain, dh, dy, "rms_bwd")
    return dx, dict(gain=d_gain, lb=dlb, out_gain=d_out_gain)


_MLA_IN_WINDOWS = ((0, MLA_LORA), (MLA_LORA, MLA_LORA), (2 * MLA_LORA, HEAD_DIM))


def _mla_layer_fwd(x, j, gain, wts, qa_gain, kva_gain, qn, kn, cos_t, sin_t):
    h = rms_fwd(x, gain, "rms_fwd")
    cq, ckv, kr = [mm(h, win(wts["mla_w_in"], j, col_off=c0, cols=n), "nn", "mm_nn") for c0, n in _MLA_IN_WINDOWS]
    cqn = rms_fwd(cq, qa_gain, "rms_fwd")
    ckvn = rms_fwd(ckv, kva_gain, "rms_fwd")
    qslots = mm(cqn, win(wts["mla_w_q_up"], j), "nn", "mm_nn")
    kv = mm(ckvn, win(wts["mla_w_kv_up"], j), "nn", "mm_nn")
    qr, krot = qk_fwd(qslots, kv, kr, qn, kn, cos_t, sin_t, "qk_fwd")
    o, lse = attn_fwd(qr, krot, kv, "attn_fwd")
    y = mm(o, win(wts["mla_w_out"], j), "nn", "mm_nn_add", add=x)
    return y, (x, h, cq, ckv, kr, cqn, ckvn, qslots, kv, qr, krot, o, lse)


def _mla_layer_bwd(dy, saved, j, gain, wts, qa_gain, kva_gain, qn, kn, cos_t, sin_t, grads):
    x, h, cq, ckv, kr, cqn, ckvn, qslots, kv, qr, krot, o, lse = saved
    grads["mla_w_out"] = mm(o, dy, "tn", "mm_tn_into", into=win(grads["mla_w_out"], j))
    do = mm(dy, win(wts["mla_w_out"], j), "nt", "mm_nt")
    dq, dk, dv = attn_bwd(qr, krot, kv, o, lse, do, "attn_bwd")
    dqslots, dkv, dkr, d_qn, d_kn = qk_bwd(qslots, kv, kr, qn, kn, cos_t, sin_t, dq, dk, dv, "qk_bwd")
    grads["mla_w_q_up"] = mm(cqn, dqslots, "tn", "mm_tn_into", into=win(grads["mla_w_q_up"], j))
    dcqn = mm(dqslots, win(wts["mla_w_q_up"], j), "nt", "mm_nt")
    grads["mla_w_kv_up"] = mm(ckvn, dkv, "tn", "mm_tn_into", into=win(grads["mla_w_kv_up"], j))
    dckvn = mm(dkv, win(wts["mla_w_kv_up"], j), "nt", "mm_nt")
    dcq, d_qa = rms_bwd(cq, qa_gain, dcqn, None, "rms_bwd")
    dckv, d_kva = rms_bwd(ckv, kva_gain, dckvn, None, "rms_bwd")
    dh = None
    for d, (c0, n) in zip((dcq, dckv, dkr), _MLA_IN_WINDOWS):
        grads["mla_w_in"] = mm(h, d, "tn", "mm_tn_into", into=win(grads["mla_w_in"], j, col_off=c0, cols=n))
        dh = mm(d, win(wts["mla_w_in"], j, col_off=c0, cols=n), "nt", "mm_nt" if dh is None else "mm_nt_add", add=dh)
    dx, d_gain = rms_bwd(x, gain, dh, dy, "rms_bwd")
    return dx, dict(gain=d_gain, qa=d_qa, kva=d_kva, qn=d_qn, kn=d_kn)


BIG = (("hgrn_w_in", "rows"), ("hgrn_w_out", "rows"), ("mla_w_in", "rows"), ("mla_w_q_up", "cols"),
       ("mla_w_kv_up", "cols"), ("mla_w_out", "rows"), ("ffn_w_up", "cols"), ("ffn_w_down", "rows"))
SMALL_SHARDED = (("mla_q_a_norm", 1), ("mla_kv_a_norm", 1), ("ffn_conv_w", 2))
REPLICATED = ("norm_mix", "norm_ffn", "hgrn_lower_bounds", "hgrn_out_norm", "mla_q_norm", "mla_k_norm", "ffn_conv_b")
WEIGHTS = ("norm_mix", "norm_ffn", "hgrn_w_in", "hgrn_lower_bounds", "hgrn_out_norm", "hgrn_w_out", "mla_w_in",
           "mla_q_a_norm", "mla_w_q_up", "mla_kv_a_norm", "mla_w_kv_up", "mla_q_norm", "mla_k_norm", "mla_w_out",
           "ffn_w_up", "ffn_conv_w", "ffn_conv_b", "ffn_w_down")


def _pad_cols(a, width):
    return jnp.pad(a, [(0, 0)] * (a.ndim - 1) + [(0, width - a.shape[-1])])


def _head_slots(w):
    lead, n = w.shape[:-1], w.shape[-1] // MLA_QK
    return _pad_cols(w.reshape(lead + (n, MLA_QK)), MLA_SLOT).reshape(lead + (n * MLA_SLOT,))


def _head_unslots(w):
    lead, n = w.shape[:-1], w.shape[-1] // MLA_SLOT
    return w.reshape(lead + (n, MLA_SLOT))[..., :MLA_QK].reshape(lead + (n * MLA_QK,))


def _to_stack_layout(name, a):
    if name == "hgrn_w_in":
        return a
    if name == "mla_w_in":
        return _pad_cols(a, MLA_IN_COLS)
    if name == "mla_w_q_up":
        return _head_slots(a)
    return a


def _from_stack_layout(name, a):
    if name == "mla_w_in":
        return a[..., :2 * MLA_LORA + MLA_ROPE]
    if name == "mla_w_q_up":
        return _head_unslots(a)
    return a


def _rope_tables(positions):
    inv_freq = ROPE_THETA ** (-jnp.arange(0, MLA_ROPE, 2, dtype=F32) / MLA_ROPE)
    ang = positions.astype(F32)[:, None] * inv_freq
    cos, sin = jnp.cos(ang), jnp.sin(ang)
    S = positions.shape[0]
    ones, zeros = jnp.ones((S, MLA_NOPE), F32), jnp.zeros((S, MLA_SLOT - MLA_QK), F32)
    return (jnp.concatenate([ones, cos, cos, zeros], axis=1),
            jnp.concatenate([jnp.zeros((S, MLA_NOPE), F32), -sin, sin, zeros], axis=1))


def kernel(x, positions, norm_mix, norm_ffn, hgrn_w_in, hgrn_lower_bounds, hgrn_out_norm, hgrn_w_out, mla_w_in, mla_q_a_norm, mla_w_q_up, mla_kv_a_norm, mla_w_kv_up, mla_q_norm, mla_k_norm, mla_w_out, ffn_w_up, ffn_conv_w, ffn_conv_b, ffn_w_down, loss_target, m_norm_mix, m_norm_ffn, m_hgrn_w_in, m_hgrn_lower_bounds, m_hgrn_out_norm, m_hgrn_w_out, m_mla_w_in, m_mla_q_a_norm, m_mla_w_q_up, m_mla_kv_a_norm, m_mla_w_kv_up, m_mla_q_norm, m_mla_k_norm, m_mla_w_out, m_ffn_w_up, m_ffn_conv_w, m_ffn_conv_b, m_ffn_w_down, v_norm_mix, v_norm_ffn, v_hgrn_w_in, v_hgrn_lower_bounds, v_hgrn_out_norm, v_hgrn_w_out, v_mla_w_in, v_mla_q_a_norm, v_mla_w_q_up, v_mla_kv_a_norm, v_mla_w_kv_up, v_mla_q_norm, v_mla_k_norm, v_mla_w_out, v_ffn_w_up, v_ffn_conv_w, v_ffn_conv_b, v_ffn_w_down):
    args = dict(locals())
    w = {n: args[n] for n in WEIGHTS}
    m = {n: args["m_" + n] for n in WEIGHTS}
    v = {n: args["v_" + n] for n in WEIGHTS}
    depth = norm_mix.shape[0]
    x0 = x[0]
    S = x0.shape[0]
    chip = (2 * lax.axis_index("x") + lax.axis_index("y")).astype(jnp.int32).reshape(1)
    core = lax.axis_index("c").astype(jnp.int32).reshape(1)
    big_names = [n for n, _ in BIG]
    kinds = [k for _, k in BIG]
    small_names = [n for n, _ in SMALL_SHARDED]
    small_axis = dict(SMALL_SHARDED)

    local = {n: _to_stack_layout(n, w[n]) for n in big_names}
    gathered = gather_big([local[n].astype(_MXU) for n in big_names], kinds, "gather_weights")
    wts = {n: place(g, local[n], k, chip, "place") for n, k, g in zip(big_names, kinds, gathered)}
    (got_small,) = exchange([_pack([w[n] for n in small_names], F32, SUBLANES)], "chips", [None], "gather_small")
    per_chip = [_unpack(got_small[p], [w[n].shape for n in small_names]) for p in range(N_CHIPS)]
    small = {n: jnp.concatenate([per_chip[p][k] for p in range(N_CHIPS)], axis=small_axis[n]) for k, n in enumerate(small_names)}

    cos_t, sin_t = _rope_tables(positions[0])
    lbs = lower_bound_fwd(hgrn_lower_bounds, "lower_bound_fwd")
    mall = jnp.asarray(_hgrn_sum_matrix(min(HGRN_CHUNK, S)), _MXU)
    qn = _pad_cols(mla_q_norm, MLA_SLOT)
    kn = _pad_cols(mla_k_norm, MLA_SLOT)
    cw8 = jnp.pad(small["ffn_conv_w"], ((0, 0), (0, SUBLANES - 3), (0, 0)))

    def mixer_args(layer):
        j = layer // 2
        if layer % 2 == 0:
            return (j, norm_mix[layer:layer + 1], wts, lbs[j:j + 1], hgrn_out_norm[j:j + 1], mall)
        return (j, norm_mix[layer:layer + 1], wts, small["mla_q_a_norm"][j:j + 1], small["mla_kv_a_norm"][j:j + 1],
                qn[j:j + 1], kn[j:j + 1], cos_t, sin_t)

    def ffn_args(layer):
        return (layer, norm_ffn[layer:layer + 1], wts, cw8[layer], ffn_conv_b[layer:layer + 1])

    h = x0
    saved = []
    for layer in range(depth):
        fwd = _hgrn_layer_fwd if layer % 2 == 0 else _mla_layer_fwd
        h, s_mix = fwd(h, *mixer_args(layer))
        h, s_ffn = _ffn_fwd(h, *ffn_args(layer))
        saved.append((s_mix, s_ffn))

    dh, loss_blk = loss_head(h, loss_target[0], "loss_head")
    loss = lax.psum(loss_blk[0, 0], MESH_AXES)

    grads = {n: jnp.zeros(g.shape, F32) for n, g in zip(big_names, gathered)}
    g_mix, g_ffn = [None] * depth, [None] * depth
    for layer in reversed(range(depth)):
        s_mix, s_ffn = saved[layer]
        dh, g_ffn[layer] = _ffn_bwd(dh, s_ffn, *ffn_args(layer), grads)
        bwd = _hgrn_layer_bwd if layer % 2 == 0 else _mla_layer_bwd
        dh, g_mix[layer] = bwd(dh, s_mix, *mixer_args(layer), grads)
    hg = [g_mix[l] for l in range(0, depth, 2)]
    mg = [g_mix[l] for l in range(1, depth, 2)]
    d_p = lower_bound_bwd(hgrn_lower_bounds, hg[1]["lb"], "lower_bound_bwd")

    g_list = [grads[n] for n in big_names]
    from_core = send_other_half(g_list, "reduce_cores_in")
    chip_sums = [add_cores(g, o, core, "add_cores") for g, o in zip(g_list, from_core)]
    from_chips = scatter_to_chips(chip_sums, kinds, "reduce_chips")
    reduced = [add_chips(own, got, k, chip, "add_chips") for own, got, k in zip(chip_sums, from_chips, kinds)]
    other_half = exchange(reduced, "sibling", [None] * len(reduced), "reduce_cores_out", keep_own=False)
    big_out = {}
    for n, mine, theirs in zip(big_names, reduced, other_half):
        outs = adamw_stacked(mine, theirs, local[n], _to_stack_layout(n, m[n]), _to_stack_layout(n, v[n]), core, "adamw")
        big_out[n] = [_from_stack_layout(n, o) for o in outs]

    small_grads = {
        "norm_mix": jnp.concatenate([g["gain"] for g in g_mix], axis=0),
        "norm_ffn": jnp.concatenate([g["gain"] for g in g_ffn], axis=0),
        "hgrn_lower_bounds": d_p[0:2],
        "hgrn_out_norm": jnp.concatenate([g["out_gain"] for g in hg], axis=0),
        "mla_q_a_norm": jnp.concatenate([g["qa"] for g in mg], axis=0),
        "mla_kv_a_norm": jnp.concatenate([g["kva"] for g in mg], axis=0),
        "mla_q_norm": jnp.concatenate([g["qn"][:, :MLA_QK] for g in mg], axis=0),
        "mla_k_norm": jnp.concatenate([g["kn"][:, :MLA_QK] for g in mg], axis=0),
        "ffn_conv_w": jnp.stack([g["conv_w"] for g in g_ffn]),
        "ffn_conv_b": jnp.concatenate([g["conv_b"] for g in g_ffn], axis=0),
    }

    def chip_part(n, p):
        size = w[n].shape[small_axis[n]]
        return lax.slice_in_dim(small_grads[n], p * size, (p + 1) * size, axis=small_axis[n])

    to_chips = jnp.stack([_pack([chip_part(n, p) for n in small_names], F32, SUBLANES) for p in range(N_CHIPS)])
    rep_g, shard_g = exchange([_pack([small_grads[n] for n in REPLICATED], F32, SUBLANES), to_chips], "all",
                              [None, (2, 1, 0)], "reduce_small")
    small_out = {}
    for names, gparts in ((REPLICATED, rep_g), (small_names, shard_g)):
        packed = adamw_packed(gparts, *[_pack([t[n] for n in names], F32, SUBLANES) for t in (w, m, v)], "adamw_small")
        unpacked = [_unpack(buf, [w[n].shape for n in names]) for buf in packed]
        for k, n in enumerate(names):
            small_out[n] = [u[k] for u in unpacked]

    result = [loss, dh[None]]
    for k in range(4):
        result += [(big_out[n] if n in big_out else small_out[n])[k] for n in WEIGHTS]
    return tuple(result)
```

```python
import numpy as np
import jax
import jax.numpy as jnp
from jax import lax
from jax.experimental import pallas as pl
from jax.experimental.pallas import tpu as pltpu

F32 = jnp.float32
BF16 = jnp.bfloat16
_MXU = BF16

RMS_EPS = 1e-6
D_MODEL = 1024
HEADS = 8
HEAD_DIM = 128
HGRN_CHUNK = 128
MLA_NOPE = 128
MLA_ROPE = 64
MLA_QK = MLA_NOPE + MLA_ROPE
MLA_SLOT = 256
MLA_LORA = 256
MLA_IN_COLS = 2 * MLA_LORA + HEAD_DIM
ROPE_THETA = 10000.0
D_FF = 2816
FF_BLOCK = 1408
LANES = 128
SUBLANES = 8

ADAM_LR = 0.001
ADAM_B1 = 0.9
ADAM_B2 = 0.999
ADAM_EPS = 1e-08
ADAM_WD = 0.01
ADAM_STEP = 10

VMEM_LIMIT = 56 * 1024 * 1024
MESH_AXES = ("x", "y", "c")
N_CHIPS = 4

_NN = ((1,), (0,))
_NT = ((1,), (1,))
_TN = ((0,), (0,))


def _dg(a, b, dims):
    return lax.dot_general(a.astype(_MXU), b.astype(_MXU), (dims, ((), ())), preferred_element_type=F32)


@jax.custom_vjp
def kdot(a, b):
    return _dg(a, b, _NN)


kdot.defvjp(lambda a, b: (_dg(a, b, _NN), (a, b)), lambda r, g: (_dg(g, r[1], _NT), _dg(r[0], g, _TN)))


@jax.custom_vjp
def kdot_nt(a, b):
    return _dg(a, b, _NT)


kdot_nt.defvjp(lambda a, b: (_dg(a, b, _NT), (a, b)), lambda r, g: (_dg(g, r[1], _NN), _dg(g, r[0], _TN)))


@jax.custom_vjp
def kdot_tn(a, b):
    return _dg(a, b, _TN)


kdot_tn.defvjp(lambda a, b: (_dg(a, b, _TN), (a, b)), lambda r, g: (_dg(r[1], g, _NT), _dg(r[0], g, _NN)))


def _pick(d, prefs):
    for p in prefs:
        if d >= p and d % p == 0:
            return p
    return d


def _params(sem):
    return pltpu.CompilerParams(dimension_semantics=sem, vmem_limit_bytes=VMEM_LIMIT)


def _sds(shape, dtype=F32):
    return jax.ShapeDtypeStruct(shape, dtype)


def win(arr, layer, row_off=0, col_off=0, rows=None, cols=None):
    return (arr, layer, row_off, col_off, rows or arr.shape[1] - row_off, cols or arr.shape[2] - col_off)


def mm(a, b, mode, name, add=None, out_dtype=F32, into=None):
    if isinstance(b, tuple):
        b_arr, b_layer, b_r0, b_c0, b_rows, b_cols = b
    else:
        b_arr, b_layer, b_r0, b_c0, (b_rows, b_cols) = b, None, 0, 0, b.shape
    if mode == "nn":
        (M, K), (K2, N) = a.shape, (b_rows, b_cols)
    elif mode == "nt":
        (M, K), (N, K2) = a.shape, (b_rows, b_cols)
    else:
        (K, M), (K2, N) = a.shape, (b_rows, b_cols)
    assert K == K2, (name, a.shape, b_rows, b_cols)
    tm = M if M <= 1024 else _pick(M, (1024, 1408, 512, 256, 128))
    tn = N if N <= 1024 else _pick(N, (1024, 1408, 512, 256, 128))
    tk = K if K <= 2048 else _pick(K, (2048, 2816, 1024, 512, 256, 128))
    nk = K // tk
    dims = {"nn": _NN, "nt": _NT, "tn": _TN}[mode]
    a_spec = pl.BlockSpec((tk, tm), lambda i, j, k: (k, i)) if mode == "tn" else pl.BlockSpec((tm, tk), lambda i, j, k: (i, k))
    b_blk = (tn, tk) if mode == "nt" else (tk, tn)
    assert b_r0 % b_blk[0] == 0 and b_c0 % b_blk[1] == 0, (name, b_r0, b_c0, b_blk)
    br, bc = b_r0 // b_blk[0], b_c0 // b_blk[1]
    if mode == "nt":
        b_idx = lambda i, j, k: (br + j, bc + k)
    else:
        b_idx = lambda i, j, k: (br + k, bc + j)
    if b_layer is None:
        b_spec = pl.BlockSpec(b_blk, b_idx)
    else:
        b_spec = pl.BlockSpec((None,) + b_blk, lambda i, j, k: (b_layer,) + b_idx(i, j, k))
    plain = pl.BlockSpec((tm, tn), lambda i, j, k: (i, j))
    has_add = add is not None
    ins = [a, b_arr] + ([add] if has_add else [])
    specs = [a_spec, b_spec] + ([plain] if has_add else [])
    aliases = {}
    if into is None:
        o_spec, out_shape = plain, _sds((M, N), out_dtype)
    else:
        buf, o_layer, o_r0, o_c0, o_rows, o_cols = into
        assert (o_rows, o_cols) == (M, N) and o_r0 % tm == 0 and o_c0 % tn == 0, (name, into[1:], M, N, tm, tn)
        orow, ocol = o_r0 // tm, o_c0 // tn
        o_spec = pl.BlockSpec((None, tm, tn), lambda i, j, k: (o_layer, orow + i, ocol + j))
        out_shape = _sds(buf.shape, buf.dtype)
        aliases = {len(ins): 0}
        ins.append(buf)
        specs.append(pl.BlockSpec(memory_space=pl.ANY))

    def kern(*refs):
        a_ref, b_ref = refs[0], refs[1]
        add_ref = refs[2] if has_add else None

        def finish(r, o_ref):
            if has_add:
                r = r + add_ref[...].astype(F32)
            o_ref[...] = r.astype(o_ref.dtype)

        if nk == 1:
            finish(_dg(a_ref[...], b_ref[...], dims), refs[-1])
            return
        o_ref, acc = refs[-2], refs[-1]
        k = pl.program_id(2)

        @pl.when(k == 0)
        def _():
            acc[...] = jnp.zeros_like(acc)

        acc[...] += _dg(a_ref[...], b_ref[...], dims)

        @pl.when(k == nk - 1)
        def _():
            finish(acc[...], o_ref)

    return pl.pallas_call(
        kern, name=name, grid=(M // tm, N // tn, nk), in_specs=specs, out_specs=o_spec, out_shape=out_shape,
        scratch_shapes=[pltpu.VMEM((tm, tn), F32)] if nk > 1 else [], input_output_aliases=aliases,
        compiler_params=_params(("parallel", "parallel", "arbitrary")))(*ins)


def _store(ref, val, first):
    if first is None:
        ref[...] = val.astype(ref.dtype)
        return

    @pl.when(first)
    def _():
        ref[...] = val.astype(ref.dtype)

    @pl.when(jnp.logical_not(first))
    def _():
        ref[...] += val.astype(ref.dtype)


def tilecall(body, name, grid, ins, outs, sem, prefetch=(), aliases=None):
    n_pre, n_in = len(prefetch), len(ins)

    def kern(*refs):
        vals = body(*refs[:n_pre + n_in])
        for ref, val, (_, _, first) in zip(refs[n_pre + n_in:], vals, outs):
            _store(ref, val, None if first is None else first())

    in_specs, out_specs = [s for _, s in ins], [s for _, s, _ in outs]
    kwargs = dict(name=name, out_shape=[sh for sh, _, _ in outs], compiler_params=_params(sem),
                  input_output_aliases={n_pre + k: v for k, v in (aliases or {}).items()})
    if n_pre:
        kwargs["grid_spec"] = pltpu.PrefetchScalarGridSpec(num_scalar_prefetch=n_pre, grid=grid, in_specs=in_specs,
                                                           out_specs=out_specs)
    else:
        kwargs.update(grid=grid, in_specs=in_specs, out_specs=out_specs)
    return pl.pallas_call(kern, **kwargs)(*prefetch, *[a for a, _ in ins])


def _rms(x, g, n):
    ms = jnp.sum(x * x, axis=-1, keepdims=True) / n
    return x * lax.rsqrt(ms + RMS_EPS) * g


def _row_tile(S, w):
    return min(S, 512 if w <= 1024 else 256)


def rms_fwd(x, g, name, col=0, w=None):
    S = x.shape[0]
    w = w or x.shape[1]
    ts = _row_tile(S, w)
    return tilecall(
        lambda x_ref, g_ref: (_rms(x_ref[...], g_ref[...], w),), name, (S // ts,),
        [(x, pl.BlockSpec((ts, w), lambda i: (i, col))), (g, pl.BlockSpec((1, w), lambda i: (0, 0)))],
        [(_sds((S, w), _MXU), pl.BlockSpec((ts, w), lambda i: (i, 0)), None)], ("parallel",))[0]


def rms_bwd(x, g, dh, res, name, w=None):
    S = x.shape[0]
    w = w or x.shape[1]
    ts = _row_tile(S, w)

    def body(x_ref, g_ref, dh_ref, *rest):
        _, vjp = jax.vjp(lambda xv, gv: _rms(xv, gv, w), x_ref[...], g_ref[...])
        dx, dg = vjp(dh_ref[...].astype(F32))
        if rest:
            dx = dx + rest[0][...]
        return dx, dg

    row = pl.BlockSpec((ts, w), lambda i: (i, 0))
    vec = pl.BlockSpec((1, w), lambda i: (0, 0))
    ins = [(x, row), (g, vec), (dh, row)] + ([(res, row)] if res is not None else [])
    return tilecall(body, name, (S // ts,), ins,
                    [(_sds((S, w)), row, None), (_sds((1, w)), vec, lambda: pl.program_id(0) == 0)], ("arbitrary",))


def _shifted(u, halo_ref, is_first):
    rid = lax.broadcasted_iota(jnp.int32, (SUBLANES, 1), 0)
    h7 = jnp.where(is_first, 0.0, halo_ref[7:8, :])
    h6 = jnp.where(is_first, 0.0, halo_ref[6:7, :])
    r1, r2 = pltpu.roll(u, 1, 0), pltpu.roll(u, 2, 0)
    top1 = jnp.where(rid == 0, h7, r1[:SUBLANES])
    top2 = jnp.where(rid == 0, h6, jnp.where(rid == 1, h7, r2[:SUBLANES]))
    return jnp.concatenate([top1, r1[SUBLANES:]], axis=0), jnp.concatenate([top2, r2[SUBLANES:]], axis=0)


def _conv(u, u1, u2, cw_ref, cb_ref):
    return ((cb_ref[...] + u2 * cw_ref[0:1, :]) + u1 * cw_ref[1:2, :]) + u * cw_ref[2:3, :]


def _ffn_specs(S, ts, jmap):
    hb = ts // SUBLANES
    return (pl.BlockSpec((ts, FF_BLOCK), lambda j, i: (i, jmap(j))),
            pl.BlockSpec((SUBLANES, FF_BLOCK), lambda j, i: (jnp.maximum(i * hb - 1, 0), jmap(j))),
            pl.BlockSpec((SUBLANES, FF_BLOCK), lambda j, i: (0, jmap(j))),
            pl.BlockSpec((1, FF_BLOCK), lambda j, i: (0, jmap(j))))


def ffn_act_fwd(u, cw8, cb, name):
    S = u.shape[0]
    ts = _row_tile(S, 2 * D_FF)
    nb = D_FF // FF_BLOCK

    def body(ug, hg, cwg, cbg, uu, hu, cwu, cbu):
        first = pl.program_id(1) == 0
        g = ug[...]
        g1, g2 = _shifted(g, hg, first)
        yg = _conv(g, g1, g2, cwg, cbg)
        v = uu[...]
        v1, v2 = _shifted(v, hu, first)
        yu = _conv(v, v1, v2, cwu, cbu)
        return (yg * jax.nn.sigmoid(yg) * yu,)

    sg = _ffn_specs(S, ts, lambda j: j)
    su = _ffn_specs(S, ts, lambda j: j + nb)
    ins = [(u, sg[0]), (u, sg[1]), (cw8, sg[2]), (cb, sg[3]), (u, su[0]), (u, su[1]), (cw8, su[2]), (cb, su[3])]
    return tilecall(body, name, (nb, S // ts), ins,
                    [(_sds((S, D_FF), _MXU), pl.BlockSpec((ts, FF_BLOCK), lambda j, i: (i, j)), None)],
                    ("parallel", "parallel"))[0]


def ffn_act_bwd(u, cw8, cb, da, name):
    S = u.shape[0]
    ts = _row_tile(S, 2 * D_FF)
    nb = D_FF // FF_BLOCK

    def taps(dy, x, x1, x2):
        return jnp.concatenate(
            [jnp.sum(dy * x2, axis=0, keepdims=True), jnp.sum(dy * x1, axis=0, keepdims=True),
             jnp.sum(dy * x, axis=0, keepdims=True), jnp.zeros((SUBLANES - 3, dy.shape[1]), F32)], axis=0)

    def body(ug, hg, cwg, cbg, uu, hu, cwu, cbu, da_ref):
        first = pl.program_id(1) == 0
        g = ug[...]
        g1, g2 = _shifted(g, hg, first)
        yg = _conv(g, g1, g2, cwg, cbg)
        v = uu[...]
        v1, v2 = _shifted(v, hu, first)
        yu = _conv(v, v1, v2, cwu, cbu)
        d = da_ref[...]
        sg = jax.nn.sigmoid(yg)
        dyg = d * yu * (sg * (1.0 + yg * (1.0 - sg)))
        dyu = d * (yg * sg)
        return (dyg, dyu, taps(dyg, g, g1, g2), taps(dyu, v, v1, v2),
                jnp.sum(dyg, axis=0, keepdims=True), jnp.sum(dyu, axis=0, keepdims=True))

    sg_ = _ffn_specs(S, ts, lambda j: j)
    su_ = _ffn_specs(S, ts, lambda j: j + nb)
    row = pl.BlockSpec((ts, FF_BLOCK), lambda j, i: (i, j))
    ins = [(u, sg_[0]), (u, sg_[1]), (cw8, sg_[2]), (cb, sg_[3]), (u, su_[0]), (u, su_[1]), (cw8, su_[2]), (cb, su_[3]), (da, row)]
    first_row = lambda: pl.program_id(1) == 0
    dy, dcw, dcb = (_sds((S, D_FF)), row, None), (_sds((SUBLANES, D_FF)), sg_[2], first_row), (_sds((1, D_FF)), sg_[3], first_row)
    return tilecall(body, name, (nb, S // ts), ins, [dy, dy, dcw, dcw, dcb, dcb], ("parallel", "arbitrary"))


def ffn_conv_bwd(dyg, dyu, cw8, name):
    S = dyg.shape[0]
    ts = _row_tile(S, 2 * D_FF)
    hb = ts // SUBLANES
    nrow = S // ts
    nb = D_FF // FF_BLOCK

    def back(dy_ref, halo_ref, cw_ref):
        last = pl.program_id(1) == nrow - 1
        d = dy_ref[...]
        rid = lax.broadcasted_iota(jnp.int32, (SUBLANES, 1), 0)
        n0 = jnp.where(last, 0.0, halo_ref[0:1, :])
        n1 = jnp.where(last, 0.0, halo_ref[1:2, :])
        r1, r2 = pltpu.roll(d, ts - 1, 0), pltpu.roll(d, ts - 2, 0)
        end1 = jnp.where(rid == SUBLANES - 1, n0, r1[ts - SUBLANES:])
        end2 = jnp.where(rid == SUBLANES - 1, n1, jnp.where(rid == SUBLANES - 2, n0, r2[ts - SUBLANES:]))
        d1 = jnp.concatenate([r1[:ts - SUBLANES], end1], axis=0)
        d2 = jnp.concatenate([r2[:ts - SUBLANES], end2], axis=0)
        return d * cw_ref[2:3, :] + d1 * cw_ref[1:2, :] + d2 * cw_ref[0:1, :]

    row = pl.BlockSpec((ts, FF_BLOCK), lambda j, i: (i, j))
    halo = pl.BlockSpec((SUBLANES, FF_BLOCK), lambda j, i: (jnp.minimum((i + 1) * hb, S // SUBLANES - 1), j))
    ins = [(dyg, row), (dyg, halo), (cw8, pl.BlockSpec((SUBLANES, FF_BLOCK), lambda j, i: (0, j))),
           (dyu, row), (dyu, halo), (cw8, pl.BlockSpec((SUBLANES, FF_BLOCK), lambda j, i: (0, j + nb)))]
    out = (_sds((S, D_FF), _MXU), row, None)
    return tilecall(lambda a, b, c, d, e, f: (back(a, b, c), back(d, e, f)), name, (nb, nrow), ins, [out, out],
                    ("parallel", "parallel"))


def _hgrn_levels(C):
    out, m = [], C // 2
    while m >= 1:
        out.append(m)
        m //= 2
    return out


def _hgrn_sum_matrix(C):
    t = np.arange(C)[:, None]
    u = np.arange(C)[None, :]
    blocks = [u <= t, u > t]
    for m in _hgrn_levels(C):
        r = (t // (2 * m)) * (2 * m) + m
        right = (t % (2 * m)) >= m
        blocks.append((right & (u > r) & (u <= t)) | ((~right) & (u > t) & (u <= r)))
    return np.concatenate(blocks, axis=0).astype(np.float32)


def _make_partial_sums(nb, C):
    @jax.custom_vjp
    def sums(mall, lf):
        hi = lf.astype(_MXU)
        mid = (lf - hi.astype(F32)).astype(_MXU)
        e2 = _dg(mall, jnp.concatenate([hi, mid], axis=1), _NN)
        e = e2[:, :HEAD_DIM] + e2[:, HEAD_DIM:]
        return tuple(e[b * C:(b + 1) * C] for b in range(nb))

    def fwd(mall, lf):
        return sums(mall, lf), mall

    def bwd(mall, gs):
        return jnp.zeros_like(mall), _dg(mall, jnp.concatenate(gs, axis=0), _TN)

    sums.defvjp(fwd, bwd)
    return sums


def _hgrn_chunk(zq, zf, v, lb, st, mall, C):
    levels = _hgrn_levels(C)
    qs = zq * jax.nn.sigmoid(zq)
    fg = lb + (1.0 - lb) * jax.nn.sigmoid(zf)
    k = 1.0 - fg
    e = _make_partial_sums(2 + len(levels), C)(mall, jnp.log(fg))
    g_incl, g_after = e[0], e[1]
    rid = lax.broadcasted_iota(jnp.int32, (C, 1), 0)
    tt = lax.broadcasted_iota(jnp.int32, (C, C), 0)
    ss = lax.broadcasted_iota(jnp.int32, (C, C), 1)
    o = kdot_nt(qs * jnp.exp(g_incl), st)
    o = o + jnp.sum(qs * k, axis=-1, keepdims=True) * v
    scores = jnp.zeros((C, C), F32)
    for li, m in enumerate(levels):
        sh = int(np.log2(m))
        right = ((rid >> sh) & 1) == 1
        both = jnp.where(right, qs, k) * jnp.exp(e[2 + li])
        pair = ((tt >> (sh + 1)) == (ss >> (sh + 1))) & (((tt >> sh) & 1) == 1) & (((ss >> sh) & 1) == 0)
        scores = scores + jnp.where(pair, kdot_nt(both, both), 0.0)
    o = o + kdot(scores, v)
    g_last = jnp.sum(jnp.where(rid == C - 1, g_incl, 0.0), axis=0, keepdims=True)
    st_new = st * jnp.exp(g_last) + kdot_tn(v, k * jnp.exp(g_after))
    return o, st_new


HGRN_HEADS_PER_STEP = 4
_HGRN_LANES = HGRN_HEADS_PER_STEP * HEAD_DIM


def _hgrn_in_specs(C, nc, rev):
    cm = (lambda c: nc - 1 - c) if rev else (lambda c: c)
    blk = lambda: pl.BlockSpec((C, _HGRN_LANES), lambda h, c: (cm(c), h))
    return cm, [blk(), blk(), blk(), pl.BlockSpec((1, _HGRN_LANES), lambda h, c: (0, h))]


def _hgrn_state_spec(cm):
    return pl.BlockSpec((HGRN_HEADS_PER_STEP, None, HEAD_DIM, HEAD_DIM), lambda h, c: (h, cm(c), 0, 0))


def hgrn_fwd(zq, zf, zi, lb, mall, name):
    S = zq.shape[0]
    C = min(HGRN_CHUNK, S)
    nc = S // C

    def kern(zq_ref, zf_ref, zi_ref, lb_ref, mall_ref, o_ref, st_ref, st):
        @pl.when(pl.program_id(1) == 0)
        def _():
            st[...] = jnp.zeros_like(st)

        mall_v = mall_ref[...]
        for g in range(HGRN_HEADS_PER_STEP):
            lanes = slice(g * HEAD_DIM, (g + 1) * HEAD_DIM)
            s_in = st[g]
            st_ref[g] = s_in
            o, s_new = _hgrn_chunk(zq_ref[:, lanes], zf_ref[:, lanes], zi_ref[:, lanes], lb_ref[:, lanes], s_in, mall_v, C)
            o_ref[:, lanes] = o
            st[g] = s_new

    cm, specs = _hgrn_in_specs(C, nc, False)
    return pl.pallas_call(
        kern, name=name, grid=(HEADS // HGRN_HEADS_PER_STEP, nc),
        in_specs=specs + [pl.BlockSpec(mall.shape, lambda h, c: (0, 0))],
        out_specs=[pl.BlockSpec((C, _HGRN_LANES), lambda h, c: (c, h)), _hgrn_state_spec(cm)],
        out_shape=[_sds((S, D_MODEL)), _sds((HEADS, nc, HEAD_DIM, HEAD_DIM))],
        scratch_shapes=[pltpu.VMEM((HGRN_HEADS_PER_STEP, HEAD_DIM, HEAD_DIM), F32)],
        compiler_params=_params(("parallel", "arbitrary")))(zq, zf, zi, lb, mall)


def hgrn_bwd(zq, zf, zi, lb, mall, states, do, name):
    S = zq.shape[0]
    C = min(HGRN_CHUNK, S)
    nc = S // C

    def kern(zq_ref, zf_ref, zi_ref, lb_ref, mall_ref, st_ref, do_ref, dq_ref, df_ref, di_ref, dlb_ref, dst):
        first = pl.program_id(1) == 0

        @pl.when(first)
        def _():
            dst[...] = jnp.zeros_like(dst)

        mall_v = mall_ref[...]
        gls = []
        for g in range(HGRN_HEADS_PER_STEP):
            lanes = slice(g * HEAD_DIM, (g + 1) * HEAD_DIM)
            _, vjp = jax.vjp(lambda a, b, c, d, e: _hgrn_chunk(a, b, c, d, e, mall_v, C),
                             zq_ref[:, lanes], zf_ref[:, lanes], zi_ref[:, lanes], lb_ref[:, lanes], st_ref[g])
            ga, gb, gv, gl, gs = vjp((do_ref[:, lanes], dst[g]))
            dq_ref[:, lanes] = ga.astype(dq_ref.dtype)
            df_ref[:, lanes] = gb.astype(df_ref.dtype)
            di_ref[:, lanes] = gv.astype(di_ref.dtype)
            gls.append(gl)
            dst[g] = gs
        _store(dlb_ref, jnp.concatenate(gls, axis=1), first)

    cm, specs = _hgrn_in_specs(C, nc, True)
    row = lambda: pl.BlockSpec((C, _HGRN_LANES), lambda h, c: (cm(c), h))
    return pl.pallas_call(
        kern, name=name, grid=(HEADS // HGRN_HEADS_PER_STEP, nc),
        in_specs=specs + [pl.BlockSpec(mall.shape, lambda h, c: (0, 0)), _hgrn_state_spec(cm), row()],
        out_specs=[row(), row(), row(), pl.BlockSpec((1, _HGRN_LANES), lambda h, c: (0, h))],
        out_shape=[_sds((S, D_MODEL), _MXU), _sds((S, D_MODEL), _MXU), _sds((S, D_MODEL), _MXU), _sds((1, D_MODEL))],
        scratch_shapes=[pltpu.VMEM((HGRN_HEADS_PER_STEP, HEAD_DIM, HEAD_DIM), F32)],
        compiler_params=_params(("parallel", "arbitrary")))(zq, zf, zi, lb, mall, states, do)


def _hgrn_out(o, g, gain):
    return _rms(o, gain, HEAD_DIM) * (g * jax.nn.sigmoid(g))


def hgrn_out_fwd(o, zg, gain, name):
    S = o.shape[0]
    ts = _row_tile(S, D_MODEL)
    blk = pl.BlockSpec((ts, HEAD_DIM), lambda i, h: (i, h))
    return tilecall(lambda o_ref, g_ref, w_ref: (_hgrn_out(o_ref[...], g_ref[...], w_ref[...]),), name, (S // ts, HEADS),
                    [(o, blk), (zg, blk), (gain, pl.BlockSpec((1, HEAD_DIM), lambda i, h: (0, 0)))],
                    [(_sds((S, D_MODEL), _MXU), blk, None)], ("parallel", "parallel"))[0]


def hgrn_out_bwd(o, zg, gain, don, name):
    S = o.shape[0]
    ts = _row_tile(S, D_MODEL)
    blk = pl.BlockSpec((ts, HEAD_DIM), lambda i, h: (i, h))
    vec = pl.BlockSpec((1, HEAD_DIM), lambda i, h: (0, 0))

    def body(o_ref, g_ref, w_ref, d_ref):
        _, vjp = jax.vjp(_hgrn_out, o_ref[...], g_ref[...], w_ref[...])
        return vjp(d_ref[...])

    return tilecall(body, name, (S // ts, HEADS), [(o, blk), (zg, blk), (gain, vec), (don, blk)],
                    [(_sds((S, D_MODEL)), blk, None), (_sds((S, D_MODEL), _MXU), blk, None),
                     (_sds((1, HEAD_DIM)), vec, lambda: (pl.program_id(0) == 0) & (pl.program_id(1) == 0))],
                    ("arbitrary", "arbitrary"))


def _lb_soft(p0, p1):
    mx = jnp.maximum(p0, p1)
    e0, e1 = jnp.exp(p0 - mx), jnp.exp(p1 - mx)
    s0, s1 = e0 / (e0 + e1), e1 / (e0 + e1)
    return (s0 + s1) - s0


def lower_bound_fwd(p, name):
    assert p.shape[0] == 2

    def body(p_ref):
        s = _lb_soft(p_ref[0:1, :], p_ref[1:2, :])
        return (jnp.concatenate([jnp.zeros_like(s), s] + [jnp.zeros_like(s)] * (SUBLANES - 2), axis=0),)

    spec8 = pl.BlockSpec((SUBLANES, p.shape[1]), lambda: (0, 0))
    return tilecall(body, name, (), [(p, pl.BlockSpec(p.shape, lambda: (0, 0)))], [(_sds((SUBLANES, p.shape[1])), spec8, None)], ())[0]


def lower_bound_bwd(p, dlb1, name):
    def body(p_ref, d_ref):
        _, vjp = jax.vjp(_lb_soft, p_ref[0:1, :], p_ref[1:2, :])
        g0, g1 = vjp(d_ref[...])
        return (jnp.concatenate([g0, g1] + [jnp.zeros_like(g0)] * (SUBLANES - 2), axis=0),)

    spec8 = pl.BlockSpec((SUBLANES, p.shape[1]), lambda: (0, 0))
    return tilecall(body, name, (), [(p, pl.BlockSpec(p.shape, lambda: (0, 0))), (dlb1, pl.BlockSpec(dlb1.shape, lambda: (0, 0)))],
                    [(_sds((SUBLANES, p.shape[1])), spec8, None)], ())[0]


@jax.custom_vjp
def _swap_rope_halves(x):
    lane = lax.broadcasted_iota(jnp.int32, x.shape, 1)
    lo = (lane >= MLA_NOPE) & (lane < MLA_NOPE + MLA_ROPE // 2)
    hi = (lane >= MLA_NOPE + MLA_ROPE // 2) & (lane < MLA_QK)
    return jnp.where(lo, pltpu.roll(x, MLA_SLOT - MLA_ROPE // 2, 1), jnp.where(hi, pltpu.roll(x, MLA_ROPE // 2, 1), 0.0))


_swap_rope_halves.defvjp(lambda x: (_swap_rope_halves(x), None), lambda _, g: (_swap_rope_halves(g),))


def _norm_rope(x, gain, cos_t, sin_t):
    y = _rms(x, gain, MLA_QK)
    return y * cos_t + _swap_rope_halves(y) * sin_t


_ATTN_SCALE = MLA_QK ** -0.5


def _qk_heads(qs, kn, kr, qn, kn_gain, cos_t, sin_t):
    q = _norm_rope(qs, qn, cos_t, sin_t) * _ATTN_SCALE
    k = _norm_rope(jnp.concatenate([kn, kr], axis=1), kn_gain, cos_t, sin_t)
    return q, k


def _qk_specs(ts):
    slot = pl.BlockSpec((ts, MLA_SLOT), lambda i, h: (i, h))
    nope = pl.BlockSpec((ts, HEAD_DIM), lambda i, h: (i, 2 * h))
    shared = pl.BlockSpec((ts, HEAD_DIM), lambda i, h: (i, 0))
    gain = pl.BlockSpec((1, MLA_SLOT), lambda i, h: (0, 0))
    table = pl.BlockSpec((ts, MLA_SLOT), lambda i, h: (i, 0))
    return slot, nope, shared, gain, table


def qk_fwd(qslots, kv, krope, qn, kn, cos_t, sin_t, name):
    S = qslots.shape[0]
    ts = _row_tile(S, D_MODEL)
    slot, nope, shared, gain, table = _qk_specs(ts)

    def body(q_ref, kn_ref, kr_ref, qn_ref, kg_ref, c_ref, s_ref):
        return _qk_heads(q_ref[...], kn_ref[...], kr_ref[...], qn_ref[...], kg_ref[...], c_ref[...], s_ref[...])

    out = _sds((S, HEADS * MLA_SLOT), _MXU)
    return tilecall(body, name, (S // ts, HEADS),
                    [(qslots, slot), (kv, nope), (krope, shared), (qn, gain), (kn, gain), (cos_t, table), (sin_t, table)],
                    [(out, slot, None), (out, slot, None)], ("parallel", "parallel"))


def qk_bwd(qslots, kv, krope, qn, kn, cos_t, sin_t, dq, dk, dv, name):
    S = qslots.shape[0]
    ts = _row_tile(S, D_MODEL)
    slot, nope, shared, gain, table = _qk_specs(ts)
    vblk = pl.BlockSpec((ts, HEAD_DIM), lambda i, h: (i, h))

    def body(q_ref, kn_ref, kr_ref, qn_ref, kg_ref, c_ref, s_ref, dq_ref, dk_ref, dv_ref):
        c, s = c_ref[...], s_ref[...]
        _, vjp = jax.vjp(lambda a, b, r, g1, g2: _qk_heads(a, b, r, g1, g2, c, s),
                         q_ref[...], kn_ref[...], kr_ref[...], qn_ref[...], kg_ref[...])
        ga, gb, gr, g1, g2 = vjp((dq_ref[...], dk_ref[...]))
        return ga, jnp.concatenate([gb, dv_ref[...]], axis=1), gr, g1, g2

    first_head = lambda: pl.program_id(1) == 0
    first = lambda: (pl.program_id(0) == 0) & (pl.program_id(1) == 0)
    wide = _sds((S, HEADS * MLA_SLOT), _MXU)
    return tilecall(body, name, (S // ts, HEADS),
                    [(qslots, slot), (kv, nope), (krope, shared), (qn, gain), (kn, gain), (cos_t, table), (sin_t, table),
                     (dq, slot), (dk, slot), (dv, vblk)],
                    [(wide, slot, None), (wide, slot, None), (_sds((S, HEAD_DIM)), shared, first_head),
                     (_sds((1, MLA_SLOT)), gain, first), (_sds((1, MLA_SLOT)), gain, first)], ("arbitrary", "arbitrary"))


ATTN_TILE = 512
ATTN_HEADS_PER_STEP = 2


def _causal_pairs(nq, by_row):
    pairs = [(i, j) for i in range(nq) for j in range(i + 1)] if by_row else [(i, j) for j in range(nq) for i in range(j, nq)]
    return jnp.asarray([p[0] for p in pairs], jnp.int32), jnp.asarray([p[1] for p in pairs], jnp.int32)


def _diag_mask(s, tq):
    rows = lax.broadcasted_iota(jnp.int32, (tq, tq), 0)
    cols = lax.broadcasted_iota(jnp.int32, (tq, tq), 1)
    return jnp.where(rows >= cols, s, -jnp.inf)


def attn_fwd(qr, kr, kv, name):
    S = qr.shape[0]
    tq = min(S, ATTN_TILE)
    nq = S // tq
    i_tab, j_tab = _causal_pairs(nq, True)

    G = ATTN_HEADS_PER_STEP
    reps = tq // LANES

    def kern(it, jt, q_ref, k_ref, kv_ref, o_ref, lse_ref, m_s, l_s, acc):
        n = pl.program_id(1)
        i, j = it[n], jt[n]

        @pl.when(j == 0)
        def _():
            m_s[...] = jnp.full_like(m_s, -jnp.inf)
            l_s[...] = jnp.zeros_like(l_s)
            acc[...] = jnp.zeros_like(acc)

        def step(diagonal):
            for g in range(G):
                slot = slice(g * MLA_SLOT, (g + 1) * MLA_SLOT)
                s = _dg(q_ref[:, slot], k_ref[:, slot], _NT)
                if diagonal:
                    s = _diag_mask(s, tq)
                m_prev = m_s[g]
                m_new = jnp.maximum(m_prev, jnp.max(s, axis=-1, keepdims=True))
                alpha = jnp.exp(m_prev - m_new)
                p = jnp.exp(s - jnp.tile(m_new, (1, reps)))
                l_s[g] = alpha * l_s[g] + jnp.sum(p, axis=-1, keepdims=True)
                acc[g] = alpha * acc[g] + _dg(p, kv_ref[:, g * MLA_SLOT + HEAD_DIM:(g + 1) * MLA_SLOT], _NN)
                m_s[g] = m_new

        @pl.when(j < i)
        def _():
            step(False)

        @pl.when(j == i)
        def _():
            step(True)
            for g in range(G):
                l = l_s[g]
                lanes = slice(g * HEAD_DIM, (g + 1) * HEAD_DIM)
                o_ref[:, lanes] = acc[g] / l
                lse_ref[:, lanes] = m_s[g] + jnp.log(l)

    out = _sds((S, HEADS * HEAD_DIM))
    oblk = pl.BlockSpec((tq, G * HEAD_DIM), lambda h, n, it, jt: (it[n], h))
    stat = pltpu.VMEM((G, tq, HEAD_DIM), F32)
    return pl.pallas_call(
        kern, name=name,
        grid_spec=pltpu.PrefetchScalarGridSpec(
            num_scalar_prefetch=2, grid=(HEADS // G, i_tab.shape[0]),
            in_specs=[pl.BlockSpec((tq, G * MLA_SLOT), lambda h, n, it, jt: (it[n], h)),
                      pl.BlockSpec((tq, G * MLA_SLOT), lambda h, n, it, jt: (jt[n], h)),
                      pl.BlockSpec((tq, G * MLA_SLOT), lambda h, n, it, jt: (jt[n], h))],
            out_specs=[oblk, oblk], scratch_shapes=[stat, stat, stat]),
        out_shape=[out, out], compiler_params=_params(("parallel", "arbitrary")))(i_tab, j_tab, qr, kr, kv)


def attn_bwd(qr, kr, kv, o, lse, do, name):
    S = qr.shape[0]
    tq = min(S, ATTN_TILE)
    nq = S // tq
    i_tab, j_tab = _causal_pairs(nq, False)

    G = ATTN_HEADS_PER_STEP

    def kern(it, jt, q_ref, k_ref, kv_ref, o_ref, lse_ref, do_ref, dq_ref, dk_ref, dv_ref, dk_acc, dv_acc):
        n = pl.program_id(1)
        i, j = it[n], jt[n]

        @pl.when(n == 0)
        def _():
            dq_ref[...] = jnp.zeros_like(dq_ref)

        @pl.when(i == j)
        def _():
            dk_acc[...] = jnp.zeros_like(dk_acc)
            dv_acc[...] = jnp.zeros_like(dv_acc)

        def step(diagonal):
            rows = pl.ds(pl.multiple_of(i * tq, tq), tq)
            for g in range(G):
                slot = slice(g * MLA_SLOT, (g + 1) * MLA_SLOT)
                lanes = slice(g * HEAD_DIM, (g + 1) * HEAD_DIM)
                q, k = q_ref[:, slot], k_ref[:, slot]
                s = _dg(q, k, _NT) - jnp.tile(lse_ref[:, lanes], (1, tq // LANES))
                if diagonal:
                    s = _diag_mask(s, tq)
                p = jnp.exp(s)
                d = do_ref[:, lanes]
                delta = jnp.sum(d * o_ref[:, lanes], axis=-1, keepdims=True)
                dv_acc[:, lanes] += _dg(p, d, _TN)
                ds = p * (_dg(d, kv_ref[:, g * MLA_SLOT + HEAD_DIM:(g + 1) * MLA_SLOT], _NT) - delta)
                dk_acc[:, slot] += _dg(ds, q, _TN)
                dq_ref[rows, slot] += _dg(ds, k, _NN)

        @pl.when(i > j)
        def _():
            step(False)

        @pl.when(i == j)
        def _():
            step(True)

        @pl.when(i == nq - 1)
        def _():
            dk_ref[...] = dk_acc[...]
            dv_ref[...] = dv_acc[...]

    qblk = pl.BlockSpec((tq, G * MLA_SLOT), lambda h, n, it, jt: (it[n], h))
    oblk = pl.BlockSpec((tq, G * HEAD_DIM), lambda h, n, it, jt: (it[n], h))
    kblk = pl.BlockSpec((tq, G * MLA_SLOT), lambda h, n, it, jt: (jt[n], h))
    return pl.pallas_call(
        kern, name=name,
        grid_spec=pltpu.PrefetchScalarGridSpec(
            num_scalar_prefetch=2, grid=(HEADS // G, i_tab.shape[0]),
            in_specs=[qblk, kblk, kblk, oblk, oblk, oblk],
            out_specs=[pl.BlockSpec((S, G * MLA_SLOT), lambda h, n, it, jt: (0, h)), kblk,
                       pl.BlockSpec((tq, G * HEAD_DIM), lambda h, n, it, jt: (jt[n], h))],
            scratch_shapes=[pltpu.VMEM((tq, G * MLA_SLOT), F32), pltpu.VMEM((tq, G * HEAD_DIM), F32)]),
        out_shape=[_sds((S, HEADS * MLA_SLOT)), _sds((S, HEADS * MLA_SLOT)), _sds((S, HEADS * HEAD_DIM))],
        compiler_params=_params(("parallel", "arbitrary")))(i_tab, j_tab, qr, kr, kv, o, lse, do)


def loss_head(y, target, name):
    S, Dm = y.shape
    ts = _row_tile(S, Dm)

    def body(y_ref, t_ref):
        e = y_ref[...] - t_ref[...]
        tot = jnp.sum(jnp.sum(e * e, axis=-1, keepdims=True) / Dm, axis=0, keepdims=True)
        return e / Dm, jnp.broadcast_to(0.5 * tot, (SUBLANES, LANES))

    row = pl.BlockSpec((ts, Dm), lambda i: (i, 0))
    return tilecall(body, name, (S // ts,), [(y, row), (target, row)],
                    [(_sds((S, Dm)), row, None),
                     (_sds((SUBLANES, LANES)), pl.BlockSpec((SUBLANES, LANES), lambda i: (0, 0)), lambda: pl.program_id(0) == 0)],
                    ("arbitrary",))


_CHIP_FLIPS = ((1, 0), (0, 1), (1, 1))
_PEER_FLIPS = {
    "chips": ((1, 0, 0), (0, 1, 0), (1, 1, 0)),
    "sibling": ((0, 0, 1),),
    "all": tuple((a, b, c) for a in (0, 1) for b in (0, 1) for c in (0, 1))[1:],
}
_SLOT_WEIGHTS = {"chips": (2, 1, 0), "sibling": (0, 0, 1), "all": (4, 2, 1)}
_HBM = pl.BlockSpec(memory_space=pltpu.HBM)


def _me():
    return lax.axis_index("x"), lax.axis_index("y"), lax.axis_index("c")


def _remote(src, dst, send_sem, recv_sem, peer):
    return pltpu.make_async_remote_copy(src_ref=src, dst_ref=dst, send_sem=send_sem, recv_sem=recv_sem,
                                        device_id=peer, device_id_type=pl.DeviceIdType.MESH)


def exchange(arrs, group, slab_weights, name, keep_own=True):
    flips = _PEER_FLIPS[group]
    wx, wy, wc = _SLOT_WEIGHTS[group]
    n_slots = len(flips) + (1 if keep_own else 0)
    n = len(arrs)

    def slab(ref, a, pos):
        w = slab_weights[a]
        return ref if w is None else ref.at[w[0] * pos[0] + w[1] * pos[1] + w[2] * pos[2]]

    def kern(*refs):
        srcs, outs = refs[:n], refs[n:2 * n]
        send_sems, recv_sems = refs[2 * n:2 * n + 2]
        me = _me()
        my_slot = wx * me[0] + wy * me[1] + wc * me[2]
        copies = []
        if keep_own:
            local_sems = refs[2 * n + 2]
            for a in range(n):
                cp = pltpu.make_async_copy(slab(srcs[a], a, me), outs[a].at[my_slot], local_sems.at[a])
                cp.start()
                copies.append(cp)
        for f, flip in enumerate(flips):
            peer = tuple(m ^ b if b else m for m, b in zip(me, flip))
            for a in range(n):
                cp = _remote(slab(srcs[a], a, peer), outs[a].at[my_slot if keep_own else f],
                             send_sems.at[f, a], recv_sems.at[f, a], peer)
                cp.start()
                copies.append(cp)
        for cp in copies:
            cp.wait()

    out_shape = [_sds((n_slots,) + (a.shape if slab_weights[k] is None else a.shape[1:]), a.dtype) for k, a in enumerate(arrs)]
    sems = [pltpu.SemaphoreType.DMA((len(flips), n)), pltpu.SemaphoreType.DMA((len(flips), n))]
    return pl.pallas_call(
        kern, name=name, in_specs=[_HBM] * n, out_specs=[_HBM] * n, out_shape=out_shape,
        scratch_shapes=sems + ([pltpu.SemaphoreType.DMA((n,))] if keep_own else []))(*arrs)


def _chip_window(ref, kind, size, chip, layers):
    if kind == "rows":
        return ref.at[layers, pl.ds(chip * size, size), :]
    return ref.at[layers, :, pl.ds(pl.multiple_of(chip * size, LANES), size)]


def gather_big(shards, kinds, name):
    n = len(shards)
    fulls = []
    for s, kind in zip(shards, kinds):
        L, r, c = s.shape
        fulls.append(_sds((L, N_CHIPS * r, c) if kind == "rows" else (L, r, N_CHIPS * c), s.dtype))

    def kern(*refs):
        srcs, outs = refs[:n], refs[n:2 * n]
        ici_s, ici_r, d2d_s, d2d_r = refs[2 * n:]
        x, y, c = _me()
        my_chip = 2 * x + y
        ici, fwd = [], []
        for a in range(n):
            L, r, cc = shards[a].shape
            size = r if kinds[a] == "rows" else cc
            mine = pl.ds(c * (L // 2), L // 2)
            for f, (fx, fy) in enumerate(_CHIP_FLIPS):
                cp = _remote(srcs[a].at[mine], _chip_window(outs[a], kinds[a], size, my_chip, mine),
                             ici_s.at[a, f], ici_r.at[a, f], (x ^ fx, y ^ fy, c))
                cp.start()
                ici.append(cp)
        for a in range(n):
            L, r, cc = shards[a].shape
            size = r if kinds[a] == "rows" else cc
            mine = pl.ds(c * (L // 2), L // 2)
            for f, (fx, fy) in enumerate(_CHIP_FLIPS):
                ici[a * len(_CHIP_FLIPS) + f].wait_recv()
                landed = _chip_window(outs[a], kinds[a], size, 2 * (x ^ fx) + (y ^ fy), mine)
                cp = _remote(landed, landed, d2d_s.at[a, f], d2d_r.at[a, f], (x, y, 1 - c))
                cp.start()
                fwd.append(cp)
        for cp in ici:
            cp.wait_send()
        for cp in fwd:
            cp.wait()

    sem = pltpu.SemaphoreType.DMA((n, len(_CHIP_FLIPS)))
    return pl.pallas_call(kern, name=name, in_specs=[_HBM] * n, out_specs=[_HBM] * n, out_shape=fulls,
                          scratch_shapes=[sem, sem, sem, sem])(*shards)


def send_other_half(arrs, name):
    n = len(arrs)

    def kern(*refs):
        srcs, outs = refs[:n], refs[n:2 * n]
        send_sems, recv_sems = refs[2 * n:]
        x, y, c = _me()
        copies = []
        for a in range(n):
            hl = arrs[a].shape[0] // 2
            cp = _remote(srcs[a].at[pl.ds((1 - c) * hl, hl)], outs[a], send_sems.at[a], recv_sems.at[a], (x, y, 1 - c))
            cp.start()
            copies.append(cp)
        for cp in copies:
            cp.wait()

    return pl.pallas_call(
        kern, name=name, in_specs=[_HBM] * n, out_specs=[_HBM] * n,
        out_shape=[_sds((a.shape[0] // 2,) + a.shape[1:], a.dtype) for a in arrs],
        scratch_shapes=[pltpu.SemaphoreType.DMA((n,)), pltpu.SemaphoreType.DMA((n,))])(*arrs)


def scatter_to_chips(arrs, kinds, name):
    n = len(arrs)
    shapes = []
    for a, kind in zip(arrs, kinds):
        l, R, C = a.shape
        shapes.append((l, R // N_CHIPS, C) if kind == "rows" else (l, R, C // N_CHIPS))

    def kern(*refs):
        srcs, outs = refs[:n], refs[n:2 * n]
        send_sems, recv_sems = refs[2 * n:]
        x, y, c = _me()
        copies = []
        for a in range(n):
            size = shapes[a][1] if kinds[a] == "rows" else shapes[a][2]
            for f, (fx, fy) in enumerate(_CHIP_FLIPS):
                window = _chip_window(srcs[a], kinds[a], size, 2 * (x ^ fx) + (y ^ fy), slice(None))
                cp = _remote(window, outs[a].at[f], send_sems.at[a, f], recv_sems.at[a, f], (x ^ fx, y ^ fy, c))
                cp.start()
                copies.append(cp)
        for cp in copies:
            cp.wait()

    sem = pltpu.SemaphoreType.DMA((n, len(_CHIP_FLIPS)))
    return pl.pallas_call(
        kern, name=name, in_specs=[_HBM] * n, out_specs=[_HBM] * n,
        out_shape=[_sds((len(_CHIP_FLIPS),) + s, a.dtype) for s, a in zip(shapes, arrs)],
        scratch_shapes=[sem, sem])(*arrs)


def _stack_tile(r, c):
    for t in (1024, 704, 512, 352, 256, 128, 64, 32, 16):
        if r % t == 0 and t * c * 4 <= 3 * 512 * 1024:
            return t
    return r


def _window_map(kind, r, tr):
    nrt = r // tr
    if kind == "rows":
        return lambda l, i, chip: (l, chip[0] * nrt + i, 0)
    return lambda l, i, chip: (l, i, chip[0])


def place(full, shard, kind, chip, name):
    L, r, c = shard.shape
    tr = _stack_tile(r, c)
    wmap = _window_map(kind, r, tr)
    return tilecall(lambda chip_ref, s_ref, f_ref: (s_ref[...],), name, (L, r // tr),
                    [(shard, pl.BlockSpec((None, tr, c), lambda l, i, chip: (l, i, 0))), (full, pl.BlockSpec(memory_space=pl.ANY))],
                    [(_sds(full.shape, full.dtype), pl.BlockSpec((None, tr, c), lambda l, i, chip: wmap(l, i, chip)), None)],
                    ("parallel", "parallel"), prefetch=(chip,), aliases={1: 0})[0]


def add_cores(g, other, core, name):
    L, R, C = g.shape
    hl = L // 2
    tr = _stack_tile(R, C)
    blk = (None, tr, C)
    return tilecall(lambda core_ref, a_ref, b_ref: (a_ref[...] + b_ref[...],), name, (hl, R // tr),
                    [(g, pl.BlockSpec(blk, lambda l, i, core: (core[0] * hl + l, i, 0))),
                     (other, pl.BlockSpec(blk, lambda l, i, core: (l, i, 0)))],
                    [(_sds((hl, R, C), _MXU), pl.BlockSpec(blk, lambda l, i, core: (l, i, 0)), None)],
                    ("parallel", "parallel"), prefetch=(core,))[0]


def add_chips(own, got, kind, chip, name):
    nf, l, r, c = got.shape
    tr = _stack_tile(r, c)
    wmap = _window_map(kind, r, tr)

    def body(chip_ref, own_ref, *got_refs):
        acc = own_ref[...].astype(F32)
        for ref in got_refs:
            acc = acc + ref[...].astype(F32)
        return (acc,)

    return tilecall(body, name, (l, r // tr),
                    [(own, pl.BlockSpec((None, tr, c), lambda ll, i, chip: wmap(ll, i, chip)))] +
                    [(got, pl.BlockSpec((None, None, tr, c), lambda ll, i, chip, f=f: (f, ll, i, 0))) for f in range(nf)],
                    [(_sds((l, r, c)), pl.BlockSpec((None, tr, c), lambda ll, i, chip: (ll, i, 0)), None)],
                    ("parallel", "parallel"), prefetch=(chip,))[0]


def _adam_update(g, w, m, v):
    m_new = ADAM_B1 * m + (1.0 - ADAM_B1) * g
    v_new = ADAM_B2 * v + (1.0 - ADAM_B2) * jnp.square(g)
    m_hat = m_new / (1.0 - ADAM_B1 ** ADAM_STEP)
    v_hat = v_new / (1.0 - ADAM_B2 ** ADAM_STEP)
    delta = -ADAM_LR * (m_hat / (jnp.sqrt(v_hat) + ADAM_EPS) + ADAM_WD * w)
    return g, delta, m_new, v_new


def adamw_stacked(mine, theirs, w, m, v, core, name):
    L, r, c = w.shape
    hl = L // 2
    tr = _stack_tile(r, c)

    def body(core_ref, a_ref, b_ref, w_ref, m_ref, v_ref):
        is_mine = (pl.program_id(0) // hl) == core_ref[0]
        g = jnp.where(is_mine, a_ref[...], b_ref[...])
        return _adam_update(g, w_ref[...], m_ref[...], v_ref[...])

    full = pl.BlockSpec((None, tr, c), lambda l, i, core: (l, i, 0))
    out = (_sds((L, r, c)), full, None)
    return tilecall(body, name, (L, r // tr),
                    [(mine, pl.BlockSpec((None, tr, c), lambda l, i, core: (l % hl, i, 0))),
                     (theirs, pl.BlockSpec((None, None, tr, c), lambda l, i, core: (0, l % hl, i, 0))),
                     (w, full), (m, full), (v, full)],
                    [out, out, out, out], ("parallel", "parallel"), prefetch=(core,))


def _pack(arrs, dtype, row_multiple):
    flat = jnp.concatenate([a.reshape(-1).astype(dtype) for a in arrs])
    rows = -(-flat.shape[0] // LANES)
    rows = -(-rows // row_multiple) * row_multiple
    return jnp.pad(flat, (0, rows * LANES - flat.shape[0])).reshape(rows, LANES)


def _unpack(buf, shapes):
    flat = buf.reshape(-1)
    out, off = [], 0
    for s in shapes:
        n = int(np.prod(s))
        out.append(flat[off:off + n].reshape(s))
        off += n
    return out


def adamw_packed(gparts, w, m, v, name):
    P, R, _ = gparts.shape

    def body(g_ref, w_ref, m_ref, v_ref):
        g = g_ref[0]
        for p in range(1, P):
            g = g + g_ref[p]
        return _adam_update(g, w_ref[...], m_ref[...], v_ref[...])

    whole = pl.BlockSpec((R, LANES), lambda: (0, 0))
    out = (_sds((R, LANES)), whole, None)
    return tilecall(body, name, (), [(gparts, pl.BlockSpec((P, R, LANES), lambda: (0, 0, 0))), (w, whole), (m, whole), (v, whole)],
                    [out, out, out, out], ())


def _ffn_fwd(x, layer, gain, wts, cw8, cb):
    h = rms_fwd(x, gain, "rms_fwd")
    u = mm(h, win(wts["ffn_w_up"], layer), "nn", "mm_nn")
    a = ffn_act_fwd(u, cw8, cb, "ffn_act_fwd")
    y = mm(a, win(wts["ffn_w_down"], layer), "nn", "mm_nn_add", add=x)
    return y, (x, h, u, a)


def _ffn_bwd(dy, saved, layer, gain, wts, cw8, cb, grads):
    x, h, u, a = saved
    grads["ffn_w_down"] = mm(a, dy, "tn", "mm_tn_into", into=win(grads["ffn_w_down"], layer))
    da = mm(dy, win(wts["ffn_w_down"], layer), "nt", "mm_nt")
    dyg, dyu, dcw_g, dcw_u, dcb_g, dcb_u = ffn_act_bwd(u, cw8, cb, da, "ffn_act_bwd")
    dh = None
    for half, du in enumerate(ffn_conv_bwd(dyg, dyu, cw8, "ffn_conv_bwd")):
        cols = dict(col_off=half * D_FF, cols=D_FF)
        grads["ffn_w_up"] = mm(h, du, "tn", "mm_tn_into", into=win(grads["ffn_w_up"], layer, **cols))
        dh = mm(du, win(wts["ffn_w_up"], layer, **cols), "nt", "mm_nt" if dh is None else "mm_nt_add", add=dh)
    dx, d_gain = rms_bwd(x, gain, dh, dy, "rms_bwd")
    return dx, dict(gain=d_gain, conv_w=jnp.concatenate([dcw_g[0:3], dcw_u[0:3]], axis=1),
                    conv_b=jnp.concatenate([dcb_g, dcb_u], axis=1))


def _hgrn_w_in(wts, j, k):
    return win(wts["hgrn_w_in"], j, row_off=k * D_MODEL, rows=D_MODEL)


def _hgrn_layer_fwd(x, j, gain, wts, lb, out_gain, mall):
    h = rms_fwd(x, gain, "rms_fwd")
    z = [mm(h, _hgrn_w_in(wts, j, k), "nn", "mm_nn") for k in range(4)]
    o, states = hgrn_fwd(z[0], z[1], z[2], lb, mall, "hgrn_fwd")
    on = hgrn_out_fwd(o, z[3], out_gain, "hgrn_out_fwd")
    y = mm(on, win(wts["hgrn_w_out"], j), "nn", "mm_nn_add", add=x)
    return y, (x, h, z, o, states, on)


def _hgrn_layer_bwd(dy, saved, j, gain, wts, lb, out_gain, mall, grads):
    x, h, z, o, states, on = saved
    grads["hgrn_w_out"] = mm(on, dy, "tn", "mm_tn_into", into=win(grads["hgrn_w_out"], j))
    don = mm(dy, win(wts["hgrn_w_out"], j), "nt", "mm_nt")
    do, dzg, d_out_gain = hgrn_out_bwd(o, z[3], out_gain, don, "hgrn_out_bwd")
    dzq, dzf, dzi, dlb = hgrn_bwd(z[0], z[1], z[2], lb, mall, states, do, "hgrn_bwd")
    dz = [dzq, dzf, dzi, dzg]
    dh = None
    for k, d in enumerate(dz):
        grads["hgrn_w_in"] = mm(h, d, "tn", "mm_tn_into", into=win(grads["hgrn_w_in"], j, row_off=k * D_MODEL, rows=D_MODEL))
        dh = mm(d, _hgrn_w_in(wts, j, k), "nt", "mm_nt" if dh is None else "mm_nt_add", add=dh)
    dx, d_gain = rms_bwd(x, gain, dh, dy, "rms_bwd")
    return dx, dict(gain=d_gain, lb=dlb, out_gain=d_out_gain)


_MLA_IN_WINDOWS = ((0, MLA_LORA), (MLA_LORA, MLA_LORA), (2 * MLA_LORA, HEAD_DIM))


def _mla_layer_fwd(x, j, gain, wts, qa_gain, kva_gain, qn, kn, cos_t, sin_t):
    h = rms_fwd(x, gain, "rms_fwd")
    cq, ckv, kr = [mm(h, win(wts["mla_w_in"], j, col_off=c0, cols=n), "nn", "mm_nn") for c0, n in _MLA_IN_WINDOWS]
    cqn = rms_fwd(cq, qa_gain, "rms_fwd")
    ckvn = rms_fwd(ckv, kva_gain, "rms_fwd")
    qslots = mm(cqn, win(wts["mla_w_q_up"], j), "nn", "mm_nn")
    kv = mm(ckvn, win(wts["mla_w_kv_up"], j), "nn", "mm_nn")
    qr, krot = qk_fwd(qslots, kv, kr, qn, kn, cos_t, sin_t, "qk_fwd")
    o, lse = attn_fwd(qr, krot, kv, "attn_fwd")
    y = mm(o, win(wts["mla_w_out"], j), "nn", "mm_nn_add", add=x)
    return y, (x, h, cq, ckv, kr, cqn, ckvn, qslots, kv, qr, krot, o, lse)


def _mla_layer_bwd(dy, saved, j, gain, wts, qa_gain, kva_gain, qn, kn, cos_t, sin_t, grads):
    x, h, cq, ckv, kr, cqn, ckvn, qslots, kv, qr, krot, o, lse = saved
    grads["mla_w_out"] = mm(o, dy, "tn", "mm_tn_into", into=win(grads["mla_w_out"], j))
    do = mm(dy, win(wts["mla_w_out"], j), "nt", "mm_nt")
    dq, dk, dv = attn_bwd(qr, krot, kv, o, lse, do, "attn_bwd")
    dqslots, dkv, dkr, d_qn, d_kn = qk_bwd(qslots, kv, kr, qn, kn, cos_t, sin_t, dq, dk, dv, "qk_bwd")
    grads["mla_w_q_up"] = mm(cqn, dqslots, "tn", "mm_tn_into", into=win(grads["mla_w_q_up"], j))
    dcqn = mm(dqslots, win(wts["mla_w_q_up"], j), "nt", "mm_nt")
    grads["mla_w_kv_up"] = mm(ckvn, dkv, "tn", "mm_tn_into", into=win(grads["mla_w_kv_up"], j))
    dckvn = mm(dkv, win(wts["mla_w_kv_up"], j), "nt", "mm_nt")
    dcq, d_qa = rms_bwd(cq, qa_gain, dcqn, None, "rms_bwd")
    dckv, d_kva = rms_bwd(ckv, kva_gain, dckvn, None, "rms_bwd")
    dh = None
    for d, (c0, n) in zip((dcq, dckv, dkr), _MLA_IN_WINDOWS):
        grads["mla_w_in"] = mm(h, d, "tn", "mm_tn_into", into=win(grads["mla_w_in"], j, col_off=c0, cols=n))
        dh = mm(d, win(wts["mla_w_in"], j, col_off=c0, cols=n), "nt", "mm_nt" if dh is None else "mm_nt_add", add=dh)
    dx, d_gain = rms_bwd(x, gain, dh, dy, "rms_bwd")
    return dx, dict(gain=d_gain, qa=d_qa, kva=d_kva, qn=d_qn, kn=d_kn)


BIG = (("hgrn_w_in", "rows"), ("hgrn_w_out", "rows"), ("mla_w_in", "rows"), ("mla_w_q_up", "cols"),
       ("mla_w_kv_up", "cols"), ("mla_w_out", "rows"), ("ffn_w_up", "cols"), ("ffn_w_down", "rows"))
SMALL_SHARDED = (("mla_q_a_norm", 1), ("mla_kv_a_norm", 1), ("ffn_conv_w", 2))
REPLICATED = ("norm_mix", "norm_ffn", "hgrn_lower_bounds", "hgrn_out_norm", "mla_q_norm", "mla_k_norm", "ffn_conv_b")
WEIGHTS = ("norm_mix", "norm_ffn", "hgrn_w_in", "hgrn_lower_bounds", "hgrn_out_norm", "hgrn_w_out", "mla_w_in",
           "mla_q_a_norm", "mla_w_q_up", "mla_kv_a_norm", "mla_w_kv_up", "mla_q_norm", "mla_k_norm", "mla_w_out",
           "ffn_w_up", "ffn_conv_w", "ffn_conv_b", "ffn_w_down")


def _pad_cols(a, width):
    return jnp.pad(a, [(0, 0)] * (a.ndim - 1) + [(0, width - a.shape[-1])])


def _head_slots(w):
    lead, n = w.shape[:-1], w.shape[-1] // MLA_QK
    return _pad_cols(w.reshape(lead + (n, MLA_QK)), MLA_SLOT).reshape(lead + (n * MLA_SLOT,))


def _head_unslots(w):
    lead, n = w.shape[:-1], w.shape[-1] // MLA_SLOT
    return w.reshape(lead + (n, MLA_SLOT))[..., :MLA_QK].reshape(lead + (n * MLA_QK,))


def _to_stack_layout(name, a):
    if name == "hgrn_w_in":
        return a
    if name == "mla_w_in":
        return _pad_cols(a, MLA_IN_COLS)
    if name == "mla_w_q_up":
        return _head_slots(a)
    return a


def _from_stack_layout(name, a):
    if name == "mla_w_in":
        return a[..., :2 * MLA_LORA + MLA_ROPE]
    if name == "mla_w_q_up":
        return _head_unslots(a)
    return a


def _rope_tables(positions):
    inv_freq = ROPE_THETA ** (-jnp.arange(0, MLA_ROPE, 2, dtype=F32) / MLA_ROPE)
    ang = positions.astype(F32)[:, None] * inv_freq
    cos, sin = jnp.cos(ang), jnp.sin(ang)
    S = positions.shape[0]
    ones, zeros = jnp.ones((S, MLA_NOPE), F32), jnp.zeros((S, MLA_SLOT - MLA_QK), F32)
    return (jnp.concatenate([ones, cos, cos, zeros], axis=1),
            jnp.concatenate([jnp.zeros((S, MLA_NOPE), F32), -sin, sin, zeros], axis=1))


def kernel(x, positions, norm_mix, norm_ffn, hgrn_w_in, hgrn_lower_bounds, hgrn_out_norm, hgrn_w_out, mla_w_in, mla_q_a_norm, mla_w_q_up, mla_kv_a_norm, mla_w_kv_up, mla_q_norm, mla_k_norm, mla_w_out, ffn_w_up, ffn_conv_w, ffn_conv_b, ffn_w_down, loss_target, m_norm_mix, m_norm_ffn, m_hgrn_w_in, m_hgrn_lower_bounds, m_hgrn_out_norm, m_hgrn_w_out, m_mla_w_in, m_mla_q_a_norm, m_mla_w_q_up, m_mla_kv_a_norm, m_mla_w_kv_up, m_mla_q_norm, m_mla_k_norm, m_mla_w_out, m_ffn_w_up, m_ffn_conv_w, m_ffn_conv_b, m_ffn_w_down, v_norm_mix, v_norm_ffn, v_hgrn_w_in, v_hgrn_lower_bounds, v_hgrn_out_norm, v_hgrn_w_out, v_mla_w_in, v_mla_q_a_norm, v_mla_w_q_up, v_mla_kv_a_norm, v_mla_w_kv_up, v_mla_q_norm, v_mla_k_norm, v_mla_w_out, v_ffn_w_up, v_ffn_conv_w, v_ffn_conv_b, v_ffn_w_down):
    args = dict(locals())
    w = {n: args[n] for n in WEIGHTS}
    m = {n: args["m_" + n] for n in WEIGHTS}
    v = {n: args["v_" + n] for n in WEIGHTS}
    depth = norm_mix.shape[0]
    x0 = x[0]
    S = x0.shape[0]
    chip = (2 * lax.axis_index("x") + lax.axis_index("y")).astype(jnp.int32).reshape(1)
    core = lax.axis_index("c").astype(jnp.int32).reshape(1)
    big_names = [n for n, _ in BIG]
    kinds = [k for _, k in BIG]
    small_names = [n for n, _ in SMALL_SHARDED]
    small_axis = dict(SMALL_SHARDED)

    local = {n: _to_stack_layout(n, w[n]) for n in big_names}
    gathered = gather_big([local[n].astype(_MXU) for n in big_names], kinds, "gather_weights")
    wts = {n: place(g, local[n], k, chip, "place") for n, k, g in zip(big_names, kinds, gathered)}
    (got_small,) = exchange([_pack([w[n] for n in small_names], F32, SUBLANES)], "chips", [None], "gather_small")
    per_chip = [_unpack(got_small[p], [w[n].shape for n in small_names]) for p in range(N_CHIPS)]
    small = {n: jnp.concatenate([per_chip[p][k] for p in range(N_CHIPS)], axis=small_axis[n]) for k, n in enumerate(small_names)}

    cos_t, sin_t = _rope_tables(positions[0])
    lbs = lower_bound_fwd(hgrn_lower_bounds, "lower_bound_fwd")
    mall = jnp.asarray(_hgrn_sum_matrix(min(HGRN_CHUNK, S)), _MXU)
    qn = _pad_cols(mla_q_norm, MLA_SLOT)
    kn = _pad_cols(mla_k_norm, MLA_SLOT)
    cw8 = jnp.pad(small["ffn_conv_w"], ((0, 0), (0, SUBLANES - 3), (0, 0)))

    def mixer_args(layer):
        j = layer // 2
        if layer % 2 == 0:
            return (j, norm_mix[layer:layer + 1], wts, lbs[j:j + 1], hgrn_out_norm[j:j + 1], mall)
        return (j, norm_mix[layer:layer + 1], wts, small["mla_q_a_norm"][j:j + 1], small["mla_kv_a_norm"][j:j + 1],
                qn[j:j + 1], kn[j:j + 1], cos_t, sin_t)

    def ffn_args(layer):
        return (layer, norm_ffn[layer:layer + 1], wts, cw8[layer], ffn_conv_b[layer:layer + 1])

    h = x0
    saved = []
    for layer in range(depth):
        fwd = _hgrn_layer_fwd if layer % 2 == 0 else _mla_layer_fwd
        h, s_mix = fwd(h, *mixer_args(layer))
        h, s_ffn = _ffn_fwd(h, *ffn_args(layer))
        saved.append((s_mix, s_ffn))

    dh, loss_blk = loss_head(h, loss_target[0], "loss_head")
    loss = lax.psum(loss_blk[0, 0], MESH_AXES)

    grads = {n: jnp.zeros(g.shape, F32) for n, g in zip(big_names, gathered)}
    g_mix, g_ffn = [None] * depth, [None] * depth
    for layer in reversed(range(depth)):
        s_mix, s_ffn = saved[layer]
        dh, g_ffn[layer] = _ffn_bwd(dh, s_ffn, *ffn_args(layer), grads)
        bwd = _hgrn_layer_bwd if layer % 2 == 0 else _mla_layer_bwd
        dh, g_mix[layer] = bwd(dh, s_mix, *mixer_args(layer), grads)
    hg = [g_mix[l] for l in range(0, depth, 2)]
    mg = [g_mix[l] for l in range(1, depth, 2)]
    d_p = lower_bound_bwd(hgrn_lower_bounds, hg[1]["lb"], "lower_bound_bwd")

    g_list = [grads[n] for n in big_names]
    from_core = send_other_half(g_list, "reduce_cores_in")
    chip_sums = [add_cores(g, o, core, "add_cores") for g, o in zip(g_list, from_core)]
    from_chips = scatter_to_chips(chip_sums, kinds, "reduce_chips")
    reduced = [add_chips(own, got, k, chip, "add_chips") for own, got, k in zip(chip_sums, from_chips, kinds)]
    other_half = exchange(reduced, "sibling", [None] * len(reduced), "reduce_cores_out", keep_own=False)
    big_out = {}
    for n, mine, theirs in zip(big_names, reduced, other_half):
        outs = adamw_stacked(mine, theirs, local[n], _to_stack_layout(n, m[n]), _to_stack_layout(n, v[n]), core, "adamw")
        big_out[n] = [_from_stack_layout(n, o) for o in outs]

    small_grads = {
        "norm_mix": jnp.concatenate([g["gain"] for g in g_mix], axis=0),
        "norm_ffn": jnp.concatenate([g["gain"] for g in g_ffn], axis=0),
        "hgrn_lower_bounds": d_p[0:2],
        "hgrn_out_norm": jnp.concatenate([g["out_gain"] for g in hg], axis=0),
        "mla_q_a_norm": jnp.concatenate([g["qa"] for g in mg], axis=0),
        "mla_kv_a_norm": jnp.concatenate([g["kva"] for g in mg], axis=0),
        "mla_q_norm": jnp.concatenate([g["qn"][:, :MLA_QK] for g in mg], axis=0),
        "mla_k_norm": jnp.concatenate([g["kn"][:, :MLA_QK] for g in mg], axis=0),
        "ffn_conv_w": jnp.stack([g["conv_w"] for g in g_ffn]),
        "ffn_conv_b": jnp.concatenate([g["conv_b"] for g in g_ffn], axis=0),
    }

    def chip_part(n, p):
        size = w[n].shape[small_axis[n]]
        return lax.slice_in_dim(small_grads[n], p * size, (p + 1) * size, axis=small_axis[n])

    to_chips = jnp.stack([_pack([chip_part(n, p) for n in small_names], F32, SUBLANES) for p in range(N_CHIPS)])
    rep_g, shard_g = exchange([_pack([small_grads[n] for n in REPLICATED], F32, SUBLANES), to_chips], "all",
                              [None, (2, 1, 0)], "reduce_small")
    small_out = {}
    for names, gparts in ((REPLICATED, rep_g), (small_names, shard_g)):
        packed = adamw_packed(gparts, *[_pack([t[n] for n in names], F32, SUBLANES) for t in (w, m, v)], "adamw_small")
        unpacked = [_unpack(buf, [w[n].shape for n in names]) for buf in packed]
        for k, n in enumerate(names):
            small_out[n] = [u[k] for u in unpacked]

    result = [loss, dh[None]]
    for k in range(4):
        result += [(big_out[n] if n in big_out else small_out[n])[k] for n in WEIGHTS]
    return tuple(result)
```

```python
import numpy as np
import jax
import jax.numpy as jnp
from jax import lax
from jax.experimental import pallas as pl
from jax.experimental.pallas import tpu as pltpu

F32 = jnp.float32
BF16 = jnp.bfloat16
_MXU = BF16

RMS_EPS = 1e-6
D_MODEL = 1024
HEADS = 8
HEAD_DIM = 128
HGRN_CHUNK = 128
MLA_NOPE = 128
MLA_ROPE = 64
MLA_QK = MLA_NOPE + MLA_ROPE
MLA_SLOT = 256
MLA_LORA = 256
MLA_IN_COLS = 2 * MLA_LORA + HEAD_DIM
ROPE_THETA = 10000.0
D_FF = 2816
FF_BLOCK = 1408
LANES = 128
SUBLANES = 8

ADAM_LR = 0.001
ADAM_B1 = 0.9
ADAM_B2 = 0.999
ADAM_EPS = 1e-08
ADAM_WD = 0.01
ADAM_STEP = 10

VMEM_LIMIT = 56 * 1024 * 1024
MESH_AXES = ("x", "y", "c")
N_CHIPS = 4

_NN = ((1,), (0,))
_NT = ((1,), (1,))
_TN = ((0,), (0,))


def _dg(a, b, dims):
    return lax.dot_general(a.astype(_MXU), b.astype(_MXU), (dims, ((), ())), preferred_element_type=F32)


@jax.custom_vjp
def kdot(a, b):
    return _dg(a, b, _NN)


kdot.defvjp(lambda a, b: (_dg(a, b, _NN), (a, b)), lambda r, g: (_dg(g, r[1], _NT), _dg(r[0], g, _TN)))


@jax.custom_vjp
def kdot_nt(a, b):
    return _dg(a, b, _NT)


kdot_nt.defvjp(lambda a, b: (_dg(a, b, _NT), (a, b)), lambda r, g: (_dg(g, r[1], _NN), _dg(g, r[0], _TN)))


@jax.custom_vjp
def kdot_tn(a, b):
    return _dg(a, b, _TN)


kdot_tn.defvjp(lambda a, b: (_dg(a, b, _TN), (a, b)), lambda r, g: (_dg(r[1], g, _NT), _dg(r[0], g, _NN)))


def _pick(d, prefs):
    for p in prefs:
        if d >= p and d % p == 0:
            return p
    return d


def _params(sem):
    return pltpu.CompilerParams(dimension_semantics=sem, vmem_limit_bytes=VMEM_LIMIT)


def _sds(shape, dtype=F32):
    return jax.ShapeDtypeStruct(shape, dtype)


def win(arr, layer, row_off=0, col_off=0, rows=None, cols=None):
    return (arr, layer, row_off, col_off, rows or arr.shape[1] - row_off, cols or arr.shape[2] - col_off)


def mm(a, b, mode, name, add=None, out_dtype=F32, into=None, rms_gain=None):
    if isinstance(b, tuple):
        b_arr, b_layer, b_r0, b_c0, b_rows, b_cols = b
    else:
        b_arr, b_layer, b_r0, b_c0, (b_rows, b_cols) = b, None, 0, 0, b.shape
    if mode == "nn":
        (M, K), (K2, N) = a.shape, (b_rows, b_cols)
    elif mode == "nt":
        (M, K), (N, K2) = a.shape, (b_rows, b_cols)
    else:
        (K, M), (K2, N) = a.shape, (b_rows, b_cols)
    assert K == K2, (name, a.shape, b_rows, b_cols)
    tm = M if M <= 1024 else _pick(M, (1024, 1408, 512, 256, 128))
    tn = N if N <= 1024 else _pick(N, (1024, 1408, 512, 256, 128))
    tk = K if K <= 2048 else _pick(K, (2048, 2816, 1024, 512, 256, 128))
    nk = K // tk
    dims = {"nn": _NN, "nt": _NT, "tn": _TN}[mode]
    a_spec = pl.BlockSpec((tk, tm), lambda i, j, k: (k, i)) if mode == "tn" else pl.BlockSpec((tm, tk), lambda i, j, k: (i, k))
    b_blk = (tn, tk) if mode == "nt" else (tk, tn)
    assert b_r0 % b_blk[0] == 0 and b_c0 % b_blk[1] == 0, (name, b_r0, b_c0, b_blk)
    br, bc = b_r0 // b_blk[0], b_c0 // b_blk[1]
    if mode == "nt":
        b_idx = lambda i, j, k: (br + j, bc + k)
    else:
        b_idx = lambda i, j, k: (br + k, bc + j)
    if b_layer is None:
        b_spec = pl.BlockSpec(b_blk, b_idx)
    else:
        b_spec = pl.BlockSpec((None,) + b_blk, lambda i, j, k: (b_layer,) + b_idx(i, j, k))
    plain = pl.BlockSpec((tm, tn), lambda i, j, k: (i, j))
    has_add, has_rms = add is not None, rms_gain is not None
    assert not has_rms or (tn == N and into is None), name
    ins = [a, b_arr] + ([add] if has_add else []) + ([rms_gain] if has_rms else [])
    specs = [a_spec, b_spec] + ([plain] if has_add else []) + ([pl.BlockSpec((1, tn), lambda i, j, k: (0, j))] if has_rms else [])
    n_in = len(ins)
    aliases = {}
    if into is None:
        o_spec, out_shape = plain, _sds((M, N), out_dtype)
    else:
        buf, o_layer, o_r0, o_c0, o_rows, o_cols = into
        assert (o_rows, o_cols) == (M, N) and o_r0 % tm == 0 and o_c0 % tn == 0, (name, into[1:], M, N, tm, tn)
        orow, ocol = o_r0 // tm, o_c0 // tn
        o_spec = pl.BlockSpec((None, tm, tn), lambda i, j, k: (o_layer, orow + i, ocol + j))
        out_shape = _sds(buf.shape, buf.dtype)
        aliases = {len(ins): 0}
        ins.append(buf)
        specs.append(pl.BlockSpec(memory_space=pl.ANY))

    n_all_in = len(ins)

    def kern(*refs):
        a_ref, b_ref = refs[0], refs[1]
        add_ref = refs[2] if has_add else None
        o_ref = refs[n_all_in]

        def finish(r):
            if has_add:
                r = r + add_ref[...].astype(F32)
            o_ref[...] = r.astype(o_ref.dtype)
            if has_rms:
                refs[n_all_in + 1][...] = _rms(r, refs[n_in - 1][...], N).astype(_MXU)

        if nk == 1:
            finish(_dg(a_ref[...], b_ref[...], dims))
            return
        acc = refs[-1]
        k = pl.program_id(2)

        @pl.when(k == 0)
        def _():
            acc[...] = jnp.zeros_like(acc)

        acc[...] += _dg(a_ref[...], b_ref[...], dims)

        @pl.when(k == nk - 1)
        def _():
            finish(acc[...])

    if has_rms:
        o_spec, out_shape = [o_spec, plain], [out_shape, _sds((M, N), _MXU)]
    return pl.pallas_call(
        kern, name=name, grid=(M // tm, N // tn, nk), in_specs=specs, out_specs=o_spec, out_shape=out_shape,
        scratch_shapes=[pltpu.VMEM((tm, tn), F32)] if nk > 1 else [], input_output_aliases=aliases,
        compiler_params=_params(("parallel", "parallel", "arbitrary")))(*ins)


def _store(ref, val, first):
    if first is None:
        ref[...] = val.astype(ref.dtype)
        return

    @pl.when(first)
    def _():
        ref[...] = val.astype(ref.dtype)

    @pl.when(jnp.logical_not(first))
    def _():
        ref[...] += val.astype(ref.dtype)


def tilecall(body, name, grid, ins, outs, sem, prefetch=(), aliases=None):
    n_pre, n_in = len(prefetch), len(ins)

    def kern(*refs):
        vals = body(*refs[:n_pre + n_in])
        for ref, val, (_, _, first) in zip(refs[n_pre + n_in:], vals, outs):
            _store(ref, val, None if first is None else first())

    in_specs, out_specs = [s for _, s in ins], [s for _, s, _ in outs]
    kwargs = dict(name=name, out_shape=[sh for sh, _, _ in outs], compiler_params=_params(sem),
                  input_output_aliases={n_pre + k: v for k, v in (aliases or {}).items()})
    if n_pre:
        kwargs["grid_spec"] = pltpu.PrefetchScalarGridSpec(num_scalar_prefetch=n_pre, grid=grid, in_specs=in_specs,
                                                           out_specs=out_specs)
    else:
        kwargs.update(grid=grid, in_specs=in_specs, out_specs=out_specs)
    return pl.pallas_call(kern, **kwargs)(*prefetch, *[a for a, _ in ins])


def _rms(x, g, n):
    ms = jnp.sum(x * x, axis=-1, keepdims=True) / n
    return x * lax.rsqrt(ms + RMS_EPS) * g


def _row_tile(S, w):
    return min(S, 512 if w <= 1024 else 256)


def rms_fwd(x, g, name, col=0, w=None):
    S = x.shape[0]
    w = w or x.shape[1]
    ts = _row_tile(S, w)
    return tilecall(
        lambda x_ref, g_ref: (_rms(x_ref[...], g_ref[...], w),), name, (S // ts,),
        [(x, pl.BlockSpec((ts, w), lambda i: (i, col))), (g, pl.BlockSpec((1, w), lambda i: (0, 0)))],
        [(_sds((S, w), _MXU), pl.BlockSpec((ts, w), lambda i: (i, 0)), None)], ("parallel",))[0]


def rms_bwd(x, g, dh, res, name, w=None):
    S = x.shape[0]
    w = w or x.shape[1]
    ts = _row_tile(S, w)

    def body(x_ref, g_ref, dh_ref, *rest):
        _, vjp = jax.vjp(lambda xv, gv: _rms(xv, gv, w), x_ref[...], g_ref[...])
        dx, dg = vjp(dh_ref[...].astype(F32))
        if rest:
            dx = dx + rest[0][...]
        return dx, dg

    row = pl.BlockSpec((ts, w), lambda i: (i, 0))
    vec = pl.BlockSpec((1, w), lambda i: (0, 0))
    ins = [(x, row), (g, vec), (dh, row)] + ([(res, row)] if res is not None else [])
    return tilecall(body, name, (S // ts,), ins,
                    [(_sds((S, w)), row, None), (_sds((1, w)), vec, lambda: pl.program_id(0) == 0)], ("arbitrary",))


def _shifted(u, halo_ref, is_first):
    rid = lax.broadcasted_iota(jnp.int32, (SUBLANES, 1), 0)
    h7 = jnp.where(is_first, 0.0, halo_ref[7:8, :])
    h6 = jnp.where(is_first, 0.0, halo_ref[6:7, :])
    r1, r2 = pltpu.roll(u, 1, 0), pltpu.roll(u, 2, 0)
    top1 = jnp.where(rid == 0, h7, r1[:SUBLANES])
    top2 = jnp.where(rid == 0, h6, jnp.where(rid == 1, h7, r2[:SUBLANES]))
    return jnp.concatenate([top1, r1[SUBLANES:]], axis=0), jnp.concatenate([top2, r2[SUBLANES:]], axis=0)


def _conv(u, u1, u2, cw_ref, cb_ref):
    return ((cb_ref[...] + u2 * cw_ref[0:1, :]) + u1 * cw_ref[1:2, :]) + u * cw_ref[2:3, :]


def _ffn_specs(S, ts, jmap):
    hb = ts // SUBLANES
    return (pl.BlockSpec((ts, FF_BLOCK), lambda j, i: (i, jmap(j))),
            pl.BlockSpec((SUBLANES, FF_BLOCK), lambda j, i: (jnp.maximum(i * hb - 1, 0), jmap(j))),
            pl.BlockSpec((SUBLANES, FF_BLOCK), lambda j, i: (0, jmap(j))),
            pl.BlockSpec((1, FF_BLOCK), lambda j, i: (0, jmap(j))))


def ffn_act_fwd(u, cw8, cb, name):
    S = u.shape[0]
    ts = _row_tile(S, 2 * D_FF)
    nb = D_FF // FF_BLOCK

    def body(ug, hg, cwg, cbg, uu, hu, cwu, cbu):
        first = pl.program_id(1) == 0
        g = ug[...]
        g1, g2 = _shifted(g, hg, first)
        yg = _conv(g, g1, g2, cwg, cbg)
        v = uu[...]
        v1, v2 = _shifted(v, hu, first)
        yu = _conv(v, v1, v2, cwu, cbu)
        return (yg * jax.nn.sigmoid(yg) * yu,)

    sg = _ffn_specs(S, ts, lambda j: j)
    su = _ffn_specs(S, ts, lambda j: j + nb)
    ins = [(u, sg[0]), (u, sg[1]), (cw8, sg[2]), (cb, sg[3]), (u, su[0]), (u, su[1]), (cw8, su[2]), (cb, su[3])]
    return tilecall(body, name, (nb, S // ts), ins,
                    [(_sds((S, D_FF), _MXU), pl.BlockSpec((ts, FF_BLOCK), lambda j, i: (i, j)), None)],
                    ("parallel", "parallel"))[0]


def ffn_act_bwd(u, cw8, cb, da, name):
    S = u.shape[0]
    ts = _row_tile(S, 2 * D_FF)
    nb = D_FF // FF_BLOCK

    def taps(dy, x, x1, x2):
        return jnp.concatenate(
            [jnp.sum(dy * x2, axis=0, keepdims=True), jnp.sum(dy * x1, axis=0, keepdims=True),
             jnp.sum(dy * x, axis=0, keepdims=True), jnp.zeros((SUBLANES - 3, dy.shape[1]), F32)], axis=0)

    def body(ug, hg, cwg, cbg, uu, hu, cwu, cbu, da_ref):
        first = pl.program_id(1) == 0
        g = ug[...]
        g1, g2 = _shifted(g, hg, first)
        yg = _conv(g, g1, g2, cwg, cbg)
        v = uu[...]
        v1, v2 = _shifted(v, hu, first)
        yu = _conv(v, v1, v2, cwu, cbu)
        d = da_ref[...]
        sg = jax.nn.sigmoid(yg)
        dyg = d * yu * (sg * (1.0 + yg * (1.0 - sg)))
        dyu = d * (yg * sg)
        return (dyg, dyu, taps(dyg, g, g1, g2), taps(dyu, v, v1, v2),
                jnp.sum(dyg, axis=0, keepdims=True), jnp.sum(dyu, axis=0, keepdims=True))

    sg_ = _ffn_specs(S, ts, lambda j: j)
    su_ = _ffn_specs(S, ts, lambda j: j + nb)
    row = pl.BlockSpec((ts, FF_BLOCK), lambda j, i: (i, j))
    ins = [(u, sg_[0]), (u, sg_[1]), (cw8, sg_[2]), (cb, sg_[3]), (u, su_[0]), (u, su_[1]), (cw8, su_[2]), (cb, su_[3]), (da, row)]
    first_row = lambda: pl.program_id(1) == 0
    dy, dcw, dcb = (_sds((S, D_FF)), row, None), (_sds((SUBLANES, D_FF)), sg_[2], first_row), (_sds((1, D_FF)), sg_[3], first_row)
    return tilecall(body, name, (nb, S // ts), ins, [dy, dy, dcw, dcw, dcb, dcb], ("parallel", "arbitrary"))


def ffn_conv_bwd(dyg, dyu, cw8, name):
    S = dyg.shape[0]
    ts = _row_tile(S, 2 * D_FF)
    hb = ts // SUBLANES
    nrow = S // ts
    nb = D_FF // FF_BLOCK

    def back(dy_ref, halo_ref, cw_ref):
        last = pl.program_id(1) == nrow - 1
        d = dy_ref[...]
        rid = lax.broadcasted_iota(jnp.int32, (SUBLANES, 1), 0)
        n0 = jnp.where(last, 0.0, halo_ref[0:1, :])
        n1 = jnp.where(last, 0.0, halo_ref[1:2, :])
        r1, r2 = pltpu.roll(d, ts - 1, 0), pltpu.roll(d, ts - 2, 0)
        end1 = jnp.where(rid == SUBLANES - 1, n0, r1[ts - SUBLANES:])
        end2 = jnp.where(rid == SUBLANES - 1, n1, jnp.where(rid == SUBLANES - 2, n0, r2[ts - SUBLANES:]))
        d1 = jnp.concatenate([r1[:ts - SUBLANES], end1], axis=0)
        d2 = jnp.concatenate([r2[:ts - SUBLANES], end2], axis=0)
        return d * cw_ref[2:3, :] + d1 * cw_ref[1:2, :] + d2 * cw_ref[0:1, :]

    row = pl.BlockSpec((ts, FF_BLOCK), lambda j, i: (i, j))
    halo = pl.BlockSpec((SUBLANES, FF_BLOCK), lambda j, i: (jnp.minimum((i + 1) * hb, S // SUBLANES - 1), j))
    ins = [(dyg, row), (dyg, halo), (cw8, pl.BlockSpec((SUBLANES, FF_BLOCK), lambda j, i: (0, j))),
           (dyu, row), (dyu, halo), (cw8, pl.BlockSpec((SUBLANES, FF_BLOCK), lambda j, i: (0, j + nb)))]
    out = (_sds((S, D_FF), _MXU), row, None)
    return tilecall(lambda a, b, c, d, e, f: (back(a, b, c), back(d, e, f)), name, (nb, nrow), ins, [out, out],
                    ("parallel", "parallel"))


def _hgrn_levels(C):
    out, m = [], C // 2
    while m >= 1:
        out.append(m)
        m //= 2
    return out


def _hgrn_sum_matrix(C):
    t = np.arange(C)[:, None]
    u = np.arange(C)[None, :]
    blocks = [u <= t, u > t]
    for m in _hgrn_levels(C):
        r = (t // (2 * m)) * (2 * m) + m
        right = (t % (2 * m)) >= m
        blocks.append((right & (u > r) & (u <= t)) | ((~right) & (u > t) & (u <= r)))
    return np.concatenate(blocks, axis=0).astype(np.float32)


def _make_partial_sums(nb, C):
    @jax.custom_vjp
    def sums(mall, lf):
        hi = lf.astype(_MXU)
        mid = (lf - hi.astype(F32)).astype(_MXU)
        e2 = _dg(mall, jnp.concatenate([hi, mid], axis=1), _NN)
        e = e2[:, :HEAD_DIM] + e2[:, HEAD_DIM:]
        return tuple(e[b * C:(b + 1) * C] for b in range(nb))

    def fwd(mall, lf):
        return sums(mall, lf), mall

    def bwd(mall, gs):
        return jnp.zeros_like(mall), _dg(mall, jnp.concatenate(gs, axis=0), _TN)

    sums.defvjp(fwd, bwd)
    return sums


def _hgrn_chunk(zq, zf, v, lb, st, mall, C):
    levels = _hgrn_levels(C)
    qs = zq * jax.nn.sigmoid(zq)
    fg = lb + (1.0 - lb) * jax.nn.sigmoid(zf)
    k = 1.0 - fg
    e = _make_partial_sums(2 + len(levels), C)(mall, jnp.log(fg))
    g_incl, g_after = e[0], e[1]
    rid = lax.broadcasted_iota(jnp.int32, (C, 1), 0)
    tt = lax.broadcasted_iota(jnp.int32, (C, C), 0)
    ss = lax.broadcasted_iota(jnp.int32, (C, C), 1)
    o = kdot_nt(qs * jnp.exp(g_incl), st)
    o = o + jnp.sum(qs * k, axis=-1, keepdims=True) * v
    scores = jnp.zeros((C, C), F32)
    for li, m in enumerate(levels):
        sh = int(np.log2(m))
        right = ((rid >> sh) & 1) == 1
        both = jnp.where(right, qs, k) * jnp.exp(e[2 + li])
        pair = ((tt >> (sh + 1)) == (ss >> (sh + 1))) & (((tt >> sh) & 1) == 1) & (((ss >> sh) & 1) == 0)
        scores = scores + jnp.where(pair, kdot_nt(both, both), 0.0)
    o = o + kdot(scores, v)
    g_last = jnp.sum(jnp.where(rid == C - 1, g_incl, 0.0), axis=0, keepdims=True)
    st_new = st * jnp.exp(g_last) + kdot_tn(v, k * jnp.exp(g_after))
    return o, st_new


HGRN_HEADS_PER_STEP = 4
_HGRN_LANES = HGRN_HEADS_PER_STEP * HEAD_DIM


def _hgrn_in_specs(C, nc, rev):
    cm = (lambda c: nc - 1 - c) if rev else (lambda c: c)
    blk = lambda: pl.BlockSpec((C, _HGRN_LANES), lambda h, c: (cm(c), h))
    return cm, [blk(), blk(), blk(), pl.BlockSpec((1, _HGRN_LANES), lambda h, c: (0, h))]


def _hgrn_state_spec(cm):
    return pl.BlockSpec((HGRN_HEADS_PER_STEP, None, HEAD_DIM, HEAD_DIM), lambda h, c: (h, cm(c), 0, 0))


def hgrn_fwd(zq, zf, zi, lb, mall, name):
    S = zq.shape[0]
    C = min(HGRN_CHUNK, S)
    nc = S // C

    def kern(zq_ref, zf_ref, zi_ref, lb_ref, mall_ref, o_ref, st_ref, st):
        @pl.when(pl.program_id(1) == 0)
        def _():
            st[...] = jnp.zeros_like(st)

        mall_v = mall_ref[...]
        for g in range(HGRN_HEADS_PER_STEP):
            lanes = slice(g * HEAD_DIM, (g + 1) * HEAD_DIM)
            s_in = st[g]
            st_ref[g] = s_in
            o, s_new = _hgrn_chunk(zq_ref[:, lanes], zf_ref[:, lanes], zi_ref[:, lanes], lb_ref[:, lanes], s_in, mall_v, C)
            o_ref[:, lanes] = o
            st[g] = s_new

    cm, specs = _hgrn_in_specs(C, nc, False)
    return pl.pallas_call(
        kern, name=name, grid=(HEADS // HGRN_HEADS_PER_STEP, nc),
        in_specs=specs + [pl.BlockSpec(mall.shape, lambda h, c: (0, 0))],
        out_specs=[pl.BlockSpec((C, _HGRN_LANES), lambda h, c: (c, h)), _hgrn_state_spec(cm)],
        out_shape=[_sds((S, D_MODEL)), _sds((HEADS, nc, HEAD_DIM, HEAD_DIM))],
        scratch_shapes=[pltpu.VMEM((HGRN_HEADS_PER_STEP, HEAD_DIM, HEAD_DIM), F32)],
        compiler_params=_params(("parallel", "arbitrary")))(zq, zf, zi, lb, mall)


def hgrn_bwd(zq, zf, zi, lb, mall, states, do, name):
    S = zq.shape[0]
    C = min(HGRN_CHUNK, S)
    nc = S // C

    def kern(zq_ref, zf_ref, zi_ref, lb_ref, mall_ref, st_ref, do_ref, dq_ref, df_ref, di_ref, dlb_ref, dst):
        first = pl.program_id(1) == 0

        @pl.when(first)
        def _():
            dst[...] = jnp.zeros_like(dst)

        mall_v = mall_ref[...]
        gls = []
        for g in range(HGRN_HEADS_PER_STEP):
            lanes = slice(g * HEAD_DIM, (g + 1) * HEAD_DIM)
            _, vjp = jax.vjp(lambda a, b, c, d, e: _hgrn_chunk(a, b, c, d, e, mall_v, C),
                             zq_ref[:, lanes], zf_ref[:, lanes], zi_ref[:, lanes], lb_ref[:, lanes], st_ref[g])
            ga, gb, gv, gl, gs = vjp((do_ref[:, lanes], dst[g]))
            dq_ref[:, lanes] = ga.astype(dq_ref.dtype)
            df_ref[:, lanes] = gb.astype(df_ref.dtype)
            di_ref[:, lanes] = gv.astype(di_ref.dtype)
            gls.append(gl)
            dst[g] = gs
        _store(dlb_ref, jnp.concatenate(gls, axis=1), first)

    cm, specs = _hgrn_in_specs(C, nc, True)
    row = lambda: pl.BlockSpec((C, _HGRN_LANES), lambda h, c: (cm(c), h))
    return pl.pallas_call(
        kern, name=name, grid=(HEADS // HGRN_HEADS_PER_STEP, nc),
        in_specs=specs + [pl.BlockSpec(mall.shape, lambda h, c: (0, 0)), _hgrn_state_spec(cm), row()],
        out_specs=[row(), row(), row(), pl.BlockSpec((1, _HGRN_LANES), lambda h, c: (0, h))],
        out_shape=[_sds((S, D_MODEL), _MXU), _sds((S, D_MODEL), _MXU), _sds((S, D_MODEL), _MXU), _sds((1, D_MODEL))],
        scratch_shapes=[pltpu.VMEM((HGRN_HEADS_PER_STEP, HEAD_DIM, HEAD_DIM), F32)],
        compiler_params=_params(("parallel", "arbitrary")))(zq, zf, zi, lb, mall, states, do)


def _hgrn_out(o, g, gain):
    return _rms(o, gain, HEAD_DIM) * (g * jax.nn.sigmoid(g))


def hgrn_out_fwd(o, zg, gain, name):
    S = o.shape[0]
    ts = _row_tile(S, D_MODEL)
    blk = pl.BlockSpec((ts, HEAD_DIM), lambda i, h: (i, h))
    return tilecall(lambda o_ref, g_ref, w_ref: (_hgrn_out(o_ref[...], g_ref[...], w_ref[...]),), name, (S // ts, HEADS),
                    [(o, blk), (zg, blk), (gain, pl.BlockSpec((1, HEAD_DIM), lambda i, h: (0, 0)))],
                    [(_sds((S, D_MODEL), _MXU), blk, None)], ("parallel", "parallel"))[0]


def hgrn_out_bwd(o, zg, gain, don, name):
    S = o.shape[0]
    ts = _row_tile(S, D_MODEL)
    blk = pl.BlockSpec((ts, HEAD_DIM), lambda i, h: (i, h))
    vec = pl.BlockSpec((1, HEAD_DIM), lambda i, h: (0, 0))

    def body(o_ref, g_ref, w_ref, d_ref):
        _, vjp = jax.vjp(_hgrn_out, o_ref[...], g_ref[...], w_ref[...])
        return vjp(d_ref[...])

    return tilecall(body, name, (S // ts, HEADS), [(o, blk), (zg, blk), (gain, vec), (don, blk)],
                    [(_sds((S, D_MODEL)), blk, None), (_sds((S, D_MODEL), _MXU), blk, None),
                     (_sds((1, HEAD_DIM)), vec, lambda: (pl.program_id(0) == 0) & (pl.program_id(1) == 0))],
                    ("arbitrary", "arbitrary"))


def _lb_soft(p0, p1):
    mx = jnp.maximum(p0, p1)
    e0, e1 = jnp.exp(p0 - mx), jnp.exp(p1 - mx)
    s0, s1 = e0 / (e0 + e1), e1 / (e0 + e1)
    return (s0 + s1) - s0


def lower_bound_fwd(p, name):
    assert p.shape[0] == 2

    def body(p_ref):
        s = _lb_soft(p_ref[0:1, :], p_ref[1:2, :])
        return (jnp.concatenate([jnp.zeros_like(s), s] + [jnp.zeros_like(s)] * (SUBLANES - 2), axis=0),)

    spec8 = pl.BlockSpec((SUBLANES, p.shape[1]), lambda: (0, 0))
    return tilecall(body, name, (), [(p, pl.BlockSpec(p.shape, lambda: (0, 0)))], [(_sds((SUBLANES, p.shape[1])), spec8, None)], ())[0]


def lower_bound_bwd(p, dlb1, name):
    def body(p_ref, d_ref):
        _, vjp = jax.vjp(_lb_soft, p_ref[0:1, :], p_ref[1:2, :])
        g0, g1 = vjp(d_ref[...])
        return (jnp.concatenate([g0, g1] + [jnp.zeros_like(g0)] * (SUBLANES - 2), axis=0),)

    spec8 = pl.BlockSpec((SUBLANES, p.shape[1]), lambda: (0, 0))
    return tilecall(body, name, (), [(p, pl.BlockSpec(p.shape, lambda: (0, 0))), (dlb1, pl.BlockSpec(dlb1.shape, lambda: (0, 0)))],
                    [(_sds((SUBLANES, p.shape[1])), spec8, None)], ())[0]


@jax.custom_vjp
def _swap_rope_halves(x):
    lane = lax.broadcasted_iota(jnp.int32, x.shape, 1)
    lo = (lane >= MLA_NOPE) & (lane < MLA_NOPE + MLA_ROPE // 2)
    hi = (lane >= MLA_NOPE + MLA_ROPE // 2) & (lane < MLA_QK)
    return jnp.where(lo, pltpu.roll(x, MLA_SLOT - MLA_ROPE // 2, 1), jnp.where(hi, pltpu.roll(x, MLA_ROPE // 2, 1), 0.0))


_swap_rope_halves.defvjp(lambda x: (_swap_rope_halves(x), None), lambda _, g: (_swap_rope_halves(g),))


def _norm_rope(x, gain, cos_t, sin_t):
    y = _rms(x, gain, MLA_QK)
    return y * cos_t + _swap_rope_halves(y) * sin_t


_ATTN_SCALE = MLA_QK ** -0.5


def _qk_heads(qs, kn, kr, qn, kn_gain, cos_t, sin_t):
    q = _norm_rope(qs, qn, cos_t, sin_t) * _ATTN_SCALE
    k = _norm_rope(jnp.concatenate([kn, kr], axis=1), kn_gain, cos_t, sin_t)
    return q, k


def _qk_specs(ts):
    slot = pl.BlockSpec((ts, MLA_SLOT), lambda i, h: (i, h))
    nope = pl.BlockSpec((ts, HEAD_DIM), lambda i, h: (i, 2 * h))
    shared = pl.BlockSpec((ts, HEAD_DIM), lambda i, h: (i, 0))
    gain = pl.BlockSpec((1, MLA_SLOT), lambda i, h: (0, 0))
    table = pl.BlockSpec((ts, MLA_SLOT), lambda i, h: (i, 0))
    return slot, nope, shared, gain, table


def qk_fwd(qslots, kv, krope, qn, kn, cos_t, sin_t, name):
    S = qslots.shape[0]
    ts = _row_tile(S, D_MODEL)
    slot, nope, shared, gain, table = _qk_specs(ts)

    def body(q_ref, kn_ref, kr_ref, qn_ref, kg_ref, c_ref, s_ref):
        return _qk_heads(q_ref[...], kn_ref[...], kr_ref[...], qn_ref[...], kg_ref[...], c_ref[...], s_ref[...])

    out = _sds((S, HEADS * MLA_SLOT), _MXU)
    return tilecall(body, name, (S // ts, HEADS),
                    [(qslots, slot), (kv, nope), (krope, shared), (qn, gain), (kn, gain), (cos_t, table), (sin_t, table)],
                    [(out, slot, None), (out, slot, None)], ("parallel", "parallel"))


def qk_bwd(qslots, kv, krope, qn, kn, cos_t, sin_t, dq, dk, dv, name):
    S = qslots.shape[0]
    ts = _row_tile(S, D_MODEL)
    slot, nope, shared, gain, table = _qk_specs(ts)
    vblk = pl.BlockSpec((ts, HEAD_DIM), lambda i, h: (i, h))

    def body(q_ref, kn_ref, kr_ref, qn_ref, kg_ref, c_ref, s_ref, dq_ref, dk_ref, dv_ref):
        c, s = c_ref[...], s_ref[...]
        _, vjp = jax.vjp(lambda a, b, r, g1, g2: _qk_heads(a, b, r, g1, g2, c, s),
                         q_ref[...], kn_ref[...], kr_ref[...], qn_ref[...], kg_ref[...])
        ga, gb, gr, g1, g2 = vjp((dq_ref[...], dk_ref[...]))
        return ga, jnp.concatenate([gb, dv_ref[...]], axis=1), gr, g1, g2

    first_head = lambda: pl.program_id(1) == 0
    first = lambda: (pl.program_id(0) == 0) & (pl.program_id(1) == 0)
    wide = _sds((S, HEADS * MLA_SLOT), _MXU)
    return tilecall(body, name, (S // ts, HEADS),
                    [(qslots, slot), (kv, nope), (krope, shared), (qn, gain), (kn, gain), (cos_t, table), (sin_t, table),
                     (dq, slot), (dk, slot), (dv, vblk)],
                    [(wide, slot, None), (wide, slot, None), (_sds((S, HEAD_DIM)), shared, first_head),
                     (_sds((1, MLA_SLOT)), gain, first), (_sds((1, MLA_SLOT)), gain, first)], ("arbitrary", "arbitrary"))


ATTN_TILE_FWD = 1024
ATTN_TILE = 512
ATTN_HEADS_PER_STEP = 2


def _causal_pairs(nq, by_row):
    pairs = [(i, j) for i in range(nq) for j in range(i + 1)] if by_row else [(i, j) for j in range(nq) for i in range(j, nq)]
    return jnp.asarray([p[0] for p in pairs], jnp.int32), jnp.asarray([p[1] for p in pairs], jnp.int32)


def _diag_mask(s, tq):
    rows = lax.broadcasted_iota(jnp.int32, (tq, tq), 0)
    cols = lax.broadcasted_iota(jnp.int32, (tq, tq), 1)
    return jnp.where(rows >= cols, s, -jnp.inf)


def attn_fwd(qr, kr, kv, name):
    S = qr.shape[0]
    tq = min(S, ATTN_TILE_FWD)
    nq = S // tq
    i_tab, j_tab = _causal_pairs(nq, True)

    G = ATTN_HEADS_PER_STEP
    reps = tq // LANES

    def kern(it, jt, q_ref, k_ref, kv_ref, o_ref, lse_ref, m_s, l_s, acc):
        n = pl.program_id(1)
        i, j = it[n], jt[n]

        @pl.when(j == 0)
        def _():
            m_s[...] = jnp.full_like(m_s, -jnp.inf)
            l_s[...] = jnp.zeros_like(l_s)
            acc[...] = jnp.zeros_like(acc)

        def step(diagonal):
            for g in range(G):
                slot = slice(g * MLA_SLOT, (g + 1) * MLA_SLOT)
                s = _dg(q_ref[:, slot], k_ref[:, slot], _NT)
                if diagonal:
                    s = _diag_mask(s, tq)
                m_prev = m_s[g]
                m_new = jnp.maximum(m_prev, jnp.max(s, axis=-1, keepdims=True))
                alpha = jnp.exp(m_prev - m_new)
                p = jnp.exp(s - jnp.tile(m_new, (1, reps)))
                l_s[g] = alpha * l_s[g] + jnp.sum(p, axis=-1, keepdims=True)
                acc[g] = alpha * acc[g] + _dg(p, kv_ref[:, g * MLA_SLOT + HEAD_DIM:(g + 1) * MLA_SLOT], _NN)
                m_s[g] = m_new

        @pl.when(j < i)
        def _():
            step(False)

        @pl.when(j == i)
        def _():
            step(True)
            for g in range(G):
                l = l_s[g]
                lanes = slice(g * HEAD_DIM, (g + 1) * HEAD_DIM)
                o_ref[:, lanes] = acc[g] / l
                lse_ref[:, lanes] = m_s[g] + jnp.log(l)

    out = _sds((S, HEADS * HEAD_DIM))
    oblk = pl.BlockSpec((tq, G * HEAD_DIM), lambda h, n, it, jt: (it[n], h))
    stat = pltpu.VMEM((G, tq, HEAD_DIM), F32)
    return pl.pallas_call(
        kern, name=name,
        grid_spec=pltpu.PrefetchScalarGridSpec(
            num_scalar_prefetch=2, grid=(HEADS // G, i_tab.shape[0]),
            in_specs=[pl.BlockSpec((tq, G * MLA_SLOT), lambda h, n, it, jt: (it[n], h)),
                      pl.BlockSpec((tq, G * MLA_SLOT), lambda h, n, it, jt: (jt[n], h)),
                      pl.BlockSpec((tq, G * MLA_SLOT), lambda h, n, it, jt: (jt[n], h))],
            out_specs=[oblk, oblk], scratch_shapes=[stat, stat, stat]),
        out_shape=[out, out], compiler_params=_params(("parallel", "arbitrary")))(i_tab, j_tab, qr, kr, kv)


def attn_bwd(qr, kr, kv, o, lse, do, name):
    S = qr.shape[0]
    tq = min(S, ATTN_TILE)
    nq = S // tq
    i_tab, j_tab = _causal_pairs(nq, False)

    G = ATTN_HEADS_PER_STEP

    def kern(it, jt, q_ref, k_ref, kv_ref, o_ref, lse_ref, do_ref, dq_ref, dk_ref, dv_ref, dk_acc, dv_acc):
        n = pl.program_id(1)
        i, j = it[n], jt[n]

        @pl.when(n == 0)
        def _():
            dq_ref[...] = jnp.zeros_like(dq_ref)

        @pl.when(i == j)
        def _():
            dk_acc[...] = jnp.zeros_like(dk_acc)
            dv_acc[...] = jnp.zeros_like(dv_acc)

        def step(diagonal):
            rows = pl.ds(pl.multiple_of(i * tq, tq), tq)
            for g in range(G):
                slot = slice(g * MLA_SLOT, (g + 1) * MLA_SLOT)
                lanes = slice(g * HEAD_DIM, (g + 1) * HEAD_DIM)
                q, k = q_ref[:, slot], k_ref[:, slot]
                s = _dg(q, k, _NT) - jnp.tile(lse_ref[:, lanes], (1, tq // LANES))
                if diagonal:
                    s = _diag_mask(s, tq)
                p = jnp.exp(s)
                d = do_ref[:, lanes]
                delta = jnp.sum(d * o_ref[:, lanes], axis=-1, keepdims=True)
                dv_acc[:, lanes] += _dg(p, d, _TN)
                ds = p * (_dg(d, kv_ref[:, g * MLA_SLOT + HEAD_DIM:(g + 1) * MLA_SLOT], _NT) - delta)
                dk_acc[:, slot] += _dg(ds, q, _TN)
                dq_ref[rows, slot] += _dg(ds, k, _NN)

        @pl.when(i > j)
        def _():
            step(False)

        @pl.when(i == j)
        def _():
            step(True)

        @pl.when(i == nq - 1)
        def _():
            dk_ref[...] = dk_acc[...]
            dv_ref[...] = dv_acc[...]

    qblk = pl.BlockSpec((tq, G * MLA_SLOT), lambda h, n, it, jt: (it[n], h))
    oblk = pl.BlockSpec((tq, G * HEAD_DIM), lambda h, n, it, jt: (it[n], h))
    kblk = pl.BlockSpec((tq, G * MLA_SLOT), lambda h, n, it, jt: (jt[n], h))
    return pl.pallas_call(
        kern, name=name,
        grid_spec=pltpu.PrefetchScalarGridSpec(
            num_scalar_prefetch=2, grid=(HEADS // G, i_tab.shape[0]),
            in_specs=[qblk, kblk, kblk, oblk, oblk, oblk],
            out_specs=[pl.BlockSpec((S, G * MLA_SLOT), lambda h, n, it, jt: (0, h)), kblk,
                       pl.BlockSpec((tq, G * HEAD_DIM), lambda h, n, it, jt: (jt[n], h))],
            scratch_shapes=[pltpu.VMEM((tq, G * MLA_SLOT), F32), pltpu.VMEM((tq, G * HEAD_DIM), F32)]),
        out_shape=[_sds((S, HEADS * MLA_SLOT)), _sds((S, HEADS * MLA_SLOT)), _sds((S, HEADS * HEAD_DIM))],
        compiler_params=_params(("parallel", "arbitrary")))(i_tab, j_tab, qr, kr, kv, o, lse, do)


def loss_head(y, target, name):
    S, Dm = y.shape
    ts = _row_tile(S, Dm)

    def body(y_ref, t_ref):
        e = y_ref[...] - t_ref[...]
        tot = jnp.sum(jnp.sum(e * e, axis=-1, keepdims=True) / Dm, axis=0, keepdims=True)
        return e / Dm, jnp.broadcast_to(0.5 * tot, (SUBLANES, LANES))

    row = pl.BlockSpec((ts, Dm), lambda i: (i, 0))
    return tilecall(body, name, (S // ts,), [(y, row), (target, row)],
                    [(_sds((S, Dm)), row, None),
                     (_sds((SUBLANES, LANES)), pl.BlockSpec((SUBLANES, LANES), lambda i: (0, 0)), lambda: pl.program_id(0) == 0)],
                    ("arbitrary",))


_CHIP_FLIPS = ((1, 0), (0, 1), (1, 1))
_PEER_FLIPS = {
    "chips": ((1, 0, 0), (0, 1, 0), (1, 1, 0)),
    "sibling": ((0, 0, 1),),
    "all": tuple((a, b, c) for a in (0, 1) for b in (0, 1) for c in (0, 1))[1:],
}
_SLOT_WEIGHTS = {"chips": (2, 1, 0), "sibling": (0, 0, 1), "all": (4, 2, 1)}
_HBM = pl.BlockSpec(memory_space=pltpu.HBM)


def _me():
    return lax.axis_index("x"), lax.axis_index("y"), lax.axis_index("c")


def _remote(src, dst, send_sem, recv_sem, peer):
    return pltpu.make_async_remote_copy(src_ref=src, dst_ref=dst, send_sem=send_sem, recv_sem=recv_sem,
                                        device_id=peer, device_id_type=pl.DeviceIdType.MESH)


def exchange(arrs, group, slab_weights, name, keep_own=True):
    flips = _PEER_FLIPS[group]
    wx, wy, wc = _SLOT_WEIGHTS[group]
    n_slots = len(flips) + (1 if keep_own else 0)
    n = len(arrs)

    def slab(ref, a, pos):
        w = slab_weights[a]
        return ref if w is None else ref.at[w[0] * pos[0] + w[1] * pos[1] + w[2] * pos[2]]

    def kern(*refs):
        srcs, outs = refs[:n], refs[n:2 * n]
        send_sems, recv_sems = refs[2 * n:2 * n + 2]
        me = _me()
        my_slot = wx * me[0] + wy * me[1] + wc * me[2]
        copies = []
        if keep_own:
            local_sems = refs[2 * n + 2]
            for a in range(n):
                cp = pltpu.make_async_copy(slab(srcs[a], a, me), outs[a].at[my_slot], local_sems.at[a])
                cp.start()
                copies.append(cp)
        for f, flip in enumerate(flips):
            peer = tuple(m ^ b if b else m for m, b in zip(me, flip))
            for a in range(n):
                cp = _remote(slab(srcs[a], a, peer), outs[a].at[my_slot if keep_own else f],
                             send_sems.at[f, a], recv_sems.at[f, a], peer)
                cp.start()
                copies.append(cp)
        for cp in copies:
            cp.wait()

    out_shape = [_sds((n_slots,) + (a.shape if slab_weights[k] is None else a.shape[1:]), a.dtype) for k, a in enumerate(arrs)]
    sems = [pltpu.SemaphoreType.DMA((len(flips), n)), pltpu.SemaphoreType.DMA((len(flips), n))]
    return pl.pallas_call(
        kern, name=name, in_specs=[_HBM] * n, out_specs=[_HBM] * n, out_shape=out_shape,
        scratch_shapes=sems + ([pltpu.SemaphoreType.DMA((n,))] if keep_own else []))(*arrs)


def _chip_window(ref, kind, size, chip, layers):
    if kind == "rows":
        return ref.at[layers, pl.ds(chip * size, size), :]
    return ref.at[layers, :, pl.ds(pl.multiple_of(chip * size, LANES), size)]


def gather_big(shards, kinds, name):
    n = len(shards)
    fulls = []
    for s, kind in zip(shards, kinds):
        L, r, c = s.shape
        fulls.append(_sds((L, N_CHIPS * r, c) if kind == "rows" else (L, r, N_CHIPS * c), s.dtype))

    def kern(*refs):
        srcs, outs = refs[:n], refs[n:2 * n]
        ici_s, ici_r, d2d_s, d2d_r = refs[2 * n:]
        x, y, c = _me()
        my_chip = 2 * x + y
        ici, fwd = [], []
        for a in range(n):
            L, r, cc = shards[a].shape
            size = r if kinds[a] == "rows" else cc
            mine = pl.ds(c * (L // 2), L // 2)
            for f, (fx, fy) in enumerate(_CHIP_FLIPS):
                cp = _remote(srcs[a].at[mine], _chip_window(outs[a], kinds[a], size, my_chip, mine),
                             ici_s.at[a, f], ici_r.at[a, f], (x ^ fx, y ^ fy, c))
                cp.start()
                ici.append(cp)
        for a in range(n):
            L, r, cc = shards[a].shape
            size = r if kinds[a] == "rows" else cc
            mine = pl.ds(c * (L // 2), L // 2)
            for f, (fx, fy) in enumerate(_CHIP_FLIPS):
                ici[a * len(_CHIP_FLIPS) + f].wait_recv()
                landed = _chip_window(outs[a], kinds[a], size, 2 * (x ^ fx) + (y ^ fy), mine)
                cp = _remote(landed, landed, d2d_s.at[a, f], d2d_r.at[a, f], (x, y, 1 - c))
                cp.start()
                fwd.append(cp)
        for cp in ici:
            cp.wait_send()
        for cp in fwd:
            cp.wait()

    sem = pltpu.SemaphoreType.DMA((n, len(_CHIP_FLIPS)))
    return pl.pallas_call(kern, name=name, in_specs=[_HBM] * n, out_specs=[_HBM] * n, out_shape=fulls,
                          scratch_shapes=[sem, sem, sem, sem])(*shards)


def send_other_half(arrs, name):
    n = len(arrs)

    def kern(*refs):
        srcs, outs = refs[:n], refs[n:2 * n]
        send_sems, recv_sems = refs[2 * n:]
        x, y, c = _me()
        copies = []
        for a in range(n):
            hl = arrs[a].shape[0] // 2
            cp = _remote(srcs[a].at[pl.ds((1 - c) * hl, hl)], outs[a], send_sems.at[a], recv_sems.at[a], (x, y, 1 - c))
            cp.start()
            copies.append(cp)
        for cp in copies:
            cp.wait()

    return pl.pallas_call(
        kern, name=name, in_specs=[_HBM] * n, out_specs=[_HBM] * n,
        out_shape=[_sds((a.shape[0] // 2,) + a.shape[1:], a.dtype) for a in arrs],
        scratch_shapes=[pltpu.SemaphoreType.DMA((n,)), pltpu.SemaphoreType.DMA((n,))])(*arrs)


def scatter_to_chips(arrs, kinds, name):
    n = len(arrs)
    shapes = []
    for a, kind in zip(arrs, kinds):
        l, R, C = a.shape
        shapes.append((l, R // N_CHIPS, C) if kind == "rows" else (l, R, C // N_CHIPS))

    def kern(*refs):
        srcs, outs = refs[:n], refs[n:2 * n]
        send_sems, recv_sems = refs[2 * n:]
        x, y, c = _me()
        copies = []
        for a in range(n):
            size = shapes[a][1] if kinds[a] == "rows" else shapes[a][2]
            for f, (fx, fy) in enumerate(_CHIP_FLIPS):
                window = _chip_window(srcs[a], kinds[a], size, 2 * (x ^ fx) + (y ^ fy), slice(None))
                cp = _remote(window, outs[a].at[f], send_sems.at[a, f], recv_sems.at[a, f], (x ^ fx, y ^ fy, c))
                cp.start()
                copies.append(cp)
        for cp in copies:
            cp.wait()

    sem = pltpu.SemaphoreType.DMA((n, len(_CHIP_FLIPS)))
    return pl.pallas_call(
        kern, name=name, in_specs=[_HBM] * n, out_specs=[_HBM] * n,
        out_shape=[_sds((len(_CHIP_FLIPS),) + s, a.dtype) for s, a in zip(shapes, arrs)],
        scratch_shapes=[sem, sem])(*arrs)


def _stack_tile(r, c):
    for t in (1024, 704, 512, 352, 256, 128, 64, 32, 16):
        if r % t == 0 and t * c * 4 <= 3 * 512 * 1024:
            return t
    return r


def _window_map(kind, r, tr):
    nrt = r // tr
    if kind == "rows":
        return lambda l, i, chip: (l, chip[0] * nrt + i, 0)
    return lambda l, i, chip: (l, i, chip[0])


def place(full, shard, kind, chip, name):
    L, r, c = shard.shape
    tr = _stack_tile(r, c)
    wmap = _window_map(kind, r, tr)
    return tilecall(lambda chip_ref, s_ref, f_ref: (s_ref[...],), name, (L, r // tr),
                    [(shard, pl.BlockSpec((None, tr, c), lambda l, i, chip: (l, i, 0))), (full, pl.BlockSpec(memory_space=pl.ANY))],
                    [(_sds(full.shape, full.dtype), pl.BlockSpec((None, tr, c), lambda l, i, chip: wmap(l, i, chip)), None)],
                    ("parallel", "parallel"), prefetch=(chip,), aliases={1: 0})[0]


def add_cores(g, other, core, name):
    L, R, C = g.shape
    hl = L // 2
    tr = _stack_tile(R, C)
    blk = (None, tr, C)
    return tilecall(lambda core_ref, a_ref, b_ref: (a_ref[...] + b_ref[...],), name, (hl, R // tr),
                    [(g, pl.BlockSpec(blk, lambda l, i, core: (core[0] * hl + l, i, 0))),
                     (other, pl.BlockSpec(blk, lambda l, i, core: (l, i, 0)))],
                    [(_sds((hl, R, C), _MXU), pl.BlockSpec(blk, lambda l, i, core: (l, i, 0)), None)],
                    ("parallel", "parallel"), prefetch=(core,))[0]


def add_chips(own, got, kind, chip, name):
    nf, l, r, c = got.shape
    tr = _stack_tile(r, c)
    wmap = _window_map(kind, r, tr)

    def body(chip_ref, own_ref, *got_refs):
        acc = own_ref[...].astype(F32)
        for ref in got_refs:
            acc = acc + ref[...].astype(F32)
        return (acc,)

    return tilecall(body, name, (l, r // tr),
                    [(own, pl.BlockSpec((None, tr, c), lambda ll, i, chip: wmap(ll, i, chip)))] +
                    [(got, pl.BlockSpec((None, None, tr, c), lambda ll, i, chip, f=f: (f, ll, i, 0))) for f in range(nf)],
                    [(_sds((l, r, c)), pl.BlockSpec((None, tr, c), lambda ll, i, chip: (ll, i, 0)), None)],
                    ("parallel", "parallel"), prefetch=(chip,))[0]


def _adam_update(g, w, m, v):
    m_new = ADAM_B1 * m + (1.0 - ADAM_B1) * g
    v_new = ADAM_B2 * v + (1.0 - ADAM_B2) * jnp.square(g)
    m_hat = m_new / (1.0 - ADAM_B1 ** ADAM_STEP)
    v_hat = v_new / (1.0 - ADAM_B2 ** ADAM_STEP)
    delta = -ADAM_LR * (m_hat / (jnp.sqrt(v_hat) + ADAM_EPS) + ADAM_WD * w)
    return g, delta, m_new, v_new


def adamw_stacked(mine, theirs, w, m, v, core, name):
    L, r, c = w.shape
    hl = L // 2
    tr = _stack_tile(r, c)

    def body(core_ref, a_ref, b_ref, w_ref, m_ref, v_ref):
        is_mine = (pl.program_id(0) // hl) == core_ref[0]
        g = jnp.where(is_mine, a_ref[...], b_ref[...])
        return _adam_update(g, w_ref[...], m_ref[...], v_ref[...])

    full = pl.BlockSpec((None, tr, c), lambda l, i, core: (l, i, 0))
    out = (_sds((L, r, c)), full, None)
    return tilecall(body, name, (L, r // tr),
                    [(mine, pl.BlockSpec((None, tr, c), lambda l, i, core: (l % hl, i, 0))),
                     (theirs, pl.BlockSpec((None, None, tr, c), lambda l, i, core: (0, l % hl, i, 0))),
                     (w, full), (m, full), (v, full)],
                    [out, out, out, out], ("parallel", "parallel"), prefetch=(core,))


def _pack(arrs, dtype, row_multiple):
    flat = jnp.concatenate([a.reshape(-1).astype(dtype) for a in arrs])
    rows = -(-flat.shape[0] // LANES)
    rows = -(-rows // row_multiple) * row_multiple
    return jnp.pad(flat, (0, rows * LANES - flat.shape[0])).reshape(rows, LANES)


def _unpack(buf, shapes):
    flat = buf.reshape(-1)
    out, off = [], 0
    for s in shapes:
        n = int(np.prod(s))
        out.append(flat[off:off + n].reshape(s))
        off += n
    return out


def adamw_packed(gparts, w, m, v, name):
    P, R, _ = gparts.shape

    def body(g_ref, w_ref, m_ref, v_ref):
        g = g_ref[0]
        for p in range(1, P):
            g = g + g_ref[p]
        return _adam_update(g, w_ref[...], m_ref[...], v_ref[...])

    whole = pl.BlockSpec((R, LANES), lambda: (0, 0))
    out = (_sds((R, LANES)), whole, None)
    return tilecall(body, name, (), [(gparts, pl.BlockSpec((P, R, LANES), lambda: (0, 0, 0))), (w, whole), (m, whole), (v, whole)],
                    [out, out, out, out], ())


def _residual_out(a, w, x, next_gain):
    if next_gain is None:
        return mm(a, w, "nn", "mm_nn_add", add=x), None
    return mm(a, w, "nn", "mm_nn_add_rms", add=x, rms_gain=next_gain)


def _ffn_fwd(x, h, next_gain, layer, gain, wts, cw8, cb):
    u = mm(h, win(wts["ffn_w_up"], layer), "nn", "mm_nn")
    a = ffn_act_fwd(u, cw8, cb, "ffn_act_fwd")
    y, h_next = _residual_out(a, win(wts["ffn_w_down"], layer), x, next_gain)
    return y, h_next, (x, h, u, a)


def _ffn_bwd(dy, saved, layer, gain, wts, cw8, cb, grads):
    x, h, u, a = saved
    grads["ffn_w_down"] = mm(a, dy, "tn", "mm_tn_into", into=win(grads["ffn_w_down"], layer))
    da = mm(dy, win(wts["ffn_w_down"], layer), "nt", "mm_nt")
    dyg, dyu, dcw_g, dcw_u, dcb_g, dcb_u = ffn_act_bwd(u, cw8, cb, da, "ffn_act_bwd")
    dh = None
    for half, du in enumerate(ffn_conv_bwd(dyg, dyu, cw8, "ffn_conv_bwd")):
        cols = dict(col_off=half * D_FF, cols=D_FF)
        grads["ffn_w_up"] = mm(h, du, "tn", "mm_tn_into", into=win(grads["ffn_w_up"], layer, **cols))
        dh = mm(du, win(wts["ffn_w_up"], layer, **cols), "nt", "mm_nt" if dh is None else "mm_nt_add", add=dh)
    dx, d_gain = rms_bwd(x, gain, dh, dy, "rms_bwd")
    return dx, dict(gain=d_gain, conv_w=jnp.concatenate([dcw_g[0:3], dcw_u[0:3]], axis=1),
                    conv_b=jnp.concatenate([dcb_g, dcb_u], axis=1))


def _hgrn_w_in(wts, j, k):
    return win(wts["hgrn_w_in"], j, row_off=k * D_MODEL, rows=D_MODEL)


def _hgrn_layer_fwd(x, h, next_gain, j, gain, wts, lb, out_gain, mall):
    z = [mm(h, _hgrn_w_in(wts, j, k), "nn", "mm_nn") for k in range(4)]
    o, states = hgrn_fwd(z[0], z[1], z[2], lb, mall, "hgrn_fwd")
    on = hgrn_out_fwd(o, z[3], out_gain, "hgrn_out_fwd")
    y, h_next = _residual_out(on, win(wts["hgrn_w_out"], j), x, next_gain)
    return y, h_next, (x, h, z, o, states, on)


def _hgrn_layer_bwd(dy, saved, j, gain, wts, lb, out_gain, mall, grads):
    x, h, z, o, states, on = saved
    grads["hgrn_w_out"] = mm(on, dy, "tn", "mm_tn_into", into=win(grads["hgrn_w_out"], j))
    don = mm(dy, win(wts["hgrn_w_out"], j), "nt", "mm_nt")
    do, dzg, d_out_gain = hgrn_out_bwd(o, z[3], out_gain, don, "hgrn_out_bwd")
    dzq, dzf, dzi, dlb = hgrn_bwd(z[0], z[1], z[2], lb, mall, states, do, "hgrn_bwd")
    dz = [dzq, dzf, dzi, dzg]
    dh = None
    for k, d in enumerate(dz):
        grads["hgrn_w_in"] = mm(h, d, "tn", "mm_tn_into", into=win(grads["hgrn_w_in"], j, row_off=k * D_MODEL, rows=D_MODEL))
        dh = mm(d, _hgrn_w_in(wts, j, k), "nt", "mm_nt" if dh is None else "mm_nt_add", add=dh)
    dx, d_gain = rms_bwd(x, gain, dh, dy, "rms_bwd")
    return dx, dict(gain=d_gain, lb=dlb, out_gain=d_out_gain)


_MLA_IN_WINDOWS = ((0, MLA_LORA), (MLA_LORA, MLA_LORA), (2 * MLA_LORA, HEAD_DIM))


def _mla_layer_fwd(x, h, next_gain, j, gain, wts, qa_gain, kva_gain, qn, kn, cos_t, sin_t):
    (c0, n), (c1, n1), (c2, n2) = _MLA_IN_WINDOWS
    cq, cqn = mm(h, win(wts["mla_w_in"], j, col_off=c0, cols=n), "nn", "mm_nn_rms", rms_gain=qa_gain)
    ckv, ckvn = mm(h, win(wts["mla_w_in"], j, col_off=c1, cols=n1), "nn", "mm_nn_rms", rms_gain=kva_gain)
    kr = mm(h, win(wts["mla_w_in"], j, col_off=c2, cols=n2), "nn", "mm_nn")
    qslots = mm(cqn, win(wts["mla_w_q_up"], j), "nn", "mm_nn")
    kv = mm(ckvn, win(wts["mla_w_kv_up"], j), "nn", "mm_nn")
    qr, krot = qk_fwd(qslots, kv, kr, qn, kn, cos_t, sin_t, "qk_fwd")
    o, lse = attn_fwd(qr, krot, kv, "attn_fwd")
    y, h_next = _residual_out(o, win(wts["mla_w_out"], j), x, next_gain)
    return y, h_next, (x, h, cq, ckv, kr, cqn, ckvn, qslots, kv, qr, krot, o, lse)


def _mla_layer_bwd(dy, saved, j, gain, wts, qa_gain, kva_gain, qn, kn, cos_t, sin_t, grads):
    x, h, cq, ckv, kr, cqn, ckvn, qslots, kv, qr, krot, o, lse = saved
    grads["mla_w_out"] = mm(o, dy, "tn", "mm_tn_into", into=win(grads["mla_w_out"], j))
    do = mm(dy, win(wts["mla_w_out"], j), "nt", "mm_nt")
    dq, dk, dv = attn_bwd(qr, krot, kv, o, lse, do, "attn_bwd")
    dqslots, dkv, dkr, d_qn, d_kn = qk_bwd(qslots, kv, kr, qn, kn, cos_t, sin_t, dq, dk, dv, "qk_bwd")
    grads["mla_w_q_up"] = mm(cqn, dqslots, "tn", "mm_tn_into", into=win(grads["mla_w_q_up"], j))
    dcqn = mm(dqslots, win(wts["mla_w_q_up"], j), "nt", "mm_nt")
    grads["mla_w_kv_up"] = mm(ckvn, dkv, "tn", "mm_tn_into", into=win(grads["mla_w_kv_up"], j))
    dckvn = mm(dkv, win(wts["mla_w_kv_up"], j), "nt", "mm_nt")
    dcq, d_qa = rms_bwd(cq, qa_gain, dcqn, None, "rms_bwd")
    dckv, d_kva = rms_bwd(ckv, kva_gain, dckvn, None, "rms_bwd")
    dh = None
    for d, (c0, n) in zip((dcq, dckv, dkr), _MLA_IN_WINDOWS):
        grads["mla_w_in"] = mm(h, d, "tn", "mm_tn_into", into=win(grads["mla_w_in"], j, col_off=c0, cols=n))
        dh = mm(d, win(wts["mla_w_in"], j, col_off=c0, cols=n), "nt", "mm_nt" if dh is None else "mm_nt_add", add=dh)
    dx, d_gain = rms_bwd(x, gain, dh, dy, "rms_bwd")
    return dx, dict(gain=d_gain, qa=d_qa, kva=d_kva, qn=d_qn, kn=d_kn)


BIG = (("hgrn_w_in", "rows"), ("hgrn_w_out", "rows"), ("mla_w_in", "rows"), ("mla_w_q_up", "cols"),
       ("mla_w_kv_up", "cols"), ("mla_w_out", "rows"), ("ffn_w_up", "cols"), ("ffn_w_down", "rows"))
SMALL_SHARDED = (("mla_q_a_norm", 1), ("mla_kv_a_norm", 1), ("ffn_conv_w", 2))
REPLICATED = ("norm_mix", "norm_ffn", "hgrn_lower_bounds", "hgrn_out_norm", "mla_q_norm", "mla_k_norm", "ffn_conv_b")
WEIGHTS = ("norm_mix", "norm_ffn", "hgrn_w_in", "hgrn_lower_bounds", "hgrn_out_norm", "hgrn_w_out", "mla_w_in",
           "mla_q_a_norm", "mla_w_q_up", "mla_kv_a_norm", "mla_w_kv_up", "mla_q_norm", "mla_k_norm", "mla_w_out",
           "ffn_w_up", "ffn_conv_w", "ffn_conv_b", "ffn_w_down")


def _pad_cols(a, width):
    return jnp.pad(a, [(0, 0)] * (a.ndim - 1) + [(0, width - a.shape[-1])])


def _head_slots(w):
    lead, n = w.shape[:-1], w.shape[-1] // MLA_QK
    return _pad_cols(w.reshape(lead + (n, MLA_QK)), MLA_SLOT).reshape(lead + (n * MLA_SLOT,))


def _head_unslots(w):
    lead, n = w.shape[:-1], w.shape[-1] // MLA_SLOT
    return w.reshape(lead + (n, MLA_SLOT))[..., :MLA_QK].reshape(lead + (n * MLA_QK,))


def _to_stack_layout(name, a):
    if name == "hgrn_w_in":
        return a
    if name == "mla_w_in":
        return _pad_cols(a, MLA_IN_COLS)
    if name == "mla_w_q_up":
        return _head_slots(a)
    return a


def _from_stack_layout(name, a):
    if name == "mla_w_in":
        return a[..., :2 * MLA_LORA + MLA_ROPE]
    if name == "mla_w_q_up":
        return _head_unslots(a)
    return a


def _rope_tables(positions):
    inv_freq = ROPE_THETA ** (-jnp.arange(0, MLA_ROPE, 2, dtype=F32) / MLA_ROPE)
    ang = positions.astype(F32)[:, None] * inv_freq
    cos, sin = jnp.cos(ang), jnp.sin(ang)
    S = positions.shape[0]
    ones, zeros = jnp.ones((S, MLA_NOPE), F32), jnp.zeros((S, MLA_SLOT - MLA_QK), F32)
    return (jnp.concatenate([ones, cos, cos, zeros], axis=1),
            jnp.concatenate([jnp.zeros((S, MLA_NOPE), F32), -sin, sin, zeros], axis=1))


def kernel(x, positions, norm_mix, norm_ffn, hgrn_w_in, hgrn_lower_bounds, hgrn_out_norm, hgrn_w_out, mla_w_in, mla_q_a_norm, mla_w_q_up, mla_kv_a_norm, mla_w_kv_up, mla_q_norm, mla_k_norm, mla_w_out, ffn_w_up, ffn_conv_w, ffn_conv_b, ffn_w_down, loss_target, m_norm_mix, m_norm_ffn, m_hgrn_w_in, m_hgrn_lower_bounds, m_hgrn_out_norm, m_hgrn_w_out, m_mla_w_in, m_mla_q_a_norm, m_mla_w_q_up, m_mla_kv_a_norm, m_mla_w_kv_up, m_mla_q_norm, m_mla_k_norm, m_mla_w_out, m_ffn_w_up, m_ffn_conv_w, m_ffn_conv_b, m_ffn_w_down, v_norm_mix, v_norm_ffn, v_hgrn_w_in, v_hgrn_lower_bounds, v_hgrn_out_norm, v_hgrn_w_out, v_mla_w_in, v_mla_q_a_norm, v_mla_w_q_up, v_mla_kv_a_norm, v_mla_w_kv_up, v_mla_q_norm, v_mla_k_norm, v_mla_w_out, v_ffn_w_up, v_ffn_conv_w, v_ffn_conv_b, v_ffn_w_down):
    args = dict(locals())
    w = {n: args[n] for n in WEIGHTS}
    m = {n: args["m_" + n] for n in WEIGHTS}
    v = {n: args["v_" + n] for n in WEIGHTS}
    depth = norm_mix.shape[0]
    x0 = x[0]
    S = x0.shape[0]
    chip = (2 * lax.axis_index("x") + lax.axis_index("y")).astype(jnp.int32).reshape(1)
    core = lax.axis_index("c").astype(jnp.int32).reshape(1)
    big_names = [n for n, _ in BIG]
    kinds = [k for _, k in BIG]
    small_names = [n for n, _ in SMALL_SHARDED]
    small_axis = dict(SMALL_SHARDED)

    local = {n: _to_stack_layout(n, w[n]) for n in big_names}
    gathered = gather_big([local[n].astype(_MXU) for n in big_names], kinds, "gather_weights")
    wts = {n: place(g, local[n], k, chip, "place") for n, k, g in zip(big_names, kinds, gathered)}
    (got_small,) = exchange([_pack([w[n] for n in small_names], F32, SUBLANES)], "chips", [None], "gather_small")
    per_chip = [_unpack(got_small[p], [w[n].shape for n in small_names]) for p in range(N_CHIPS)]
    small = {n: jnp.concatenate([per_chip[p][k] for p in range(N_CHIPS)], axis=small_axis[n]) for k, n in enumerate(small_names)}

    cos_t, sin_t = _rope_tables(positions[0])
    lbs = lower_bound_fwd(hgrn_lower_bounds, "lower_bound_fwd")
    mall = jnp.asarray(_hgrn_sum_matrix(min(HGRN_CHUNK, S)), _MXU)
    qn = _pad_cols(mla_q_norm, MLA_SLOT)
    kn = _pad_cols(mla_k_norm, MLA_SLOT)
    cw8 = jnp.pad(small["ffn_conv_w"], ((0, 0), (0, SUBLANES - 3), (0, 0)))

    def mixer_args(layer):
        j = layer // 2
        if layer % 2 == 0:
            return (j, norm_mix[layer:layer + 1], wts, lbs[j:j + 1], hgrn_out_norm[j:j + 1], mall)
        return (j, norm_mix[layer:layer + 1], wts, small["mla_q_a_norm"][j:j + 1], small["mla_kv_a_norm"][j:j + 1],
                qn[j:j + 1], kn[j:j + 1], cos_t, sin_t)

    def ffn_args(layer):
        return (layer, norm_ffn[layer:layer + 1], wts, cw8[layer], ffn_conv_b[layer:layer + 1])

    xc, h = x0, rms_fwd(x0, norm_mix[0:1], "rms_fwd")
    saved = []
    for layer in range(depth):
        fwd = _hgrn_layer_fwd if layer % 2 == 0 else _mla_layer_fwd
        xc, h, s_mix = fwd(xc, h, norm_ffn[layer:layer + 1], *mixer_args(layer))
        xc, h, s_ffn = _ffn_fwd(xc, h, norm_mix[layer + 1:layer + 2] if layer + 1 < depth else None, *ffn_args(layer))
        saved.append((s_mix, s_ffn))

    dh, loss_blk = loss_head(xc, loss_target[0], "loss_head")
    loss = lax.psum(loss_blk[0, 0], MESH_AXES)

    grads = {n: lax.empty(g.shape, F32) for n, g in zip(big_names, gathered)}
    g_mix, g_ffn = [None] * depth, [None] * depth
    for layer in reversed(range(depth)):
        s_mix, s_ffn = saved[layer]
        dh, g_ffn[layer] = _ffn_bwd(dh, s_ffn, *ffn_args(layer), grads)
        bwd = _hgrn_layer_bwd if layer % 2 == 0 else _mla_layer_bwd
        dh, g_mix[layer] = bwd(dh, s_mix, *mixer_args(layer), grads)
    hg = [g_mix[l] for l in range(0, depth, 2)]
    mg = [g_mix[l] for l in range(1, depth, 2)]
    d_p = lower_bound_bwd(hgrn_lower_bounds, hg[1]["lb"], "lower_bound_bwd")

    g_list = [grads[n] for n in big_names]
    from_core = send_other_half(g_list, "reduce_cores_in")
    chip_sums = [add_cores(g, o, core, "add_cores") for g, o in zip(g_list, from_core)]
    from_chips = scatter_to_chips(chip_sums, kinds, "reduce_chips")
    reduced = [add_chips(own, got, k, chip, "add_chips") for own, got, k in zip(chip_sums, from_chips, kinds)]
    other_half = exchange(reduced, "sibling", [None] * len(reduced), "reduce_cores_out", keep_own=False)
    big_out = {}
    for n, mine, theirs in zip(big_names, reduced, other_half):
        outs = adamw_stacked(mine, theirs, local[n], _to_stack_layout(n, m[n]), _to_stack_layout(n, v[n]), core, "adamw")
        big_out[n] = [_from_stack_layout(n, o) for o in outs]

    small_grads = {
        "norm_mix": jnp.concatenate([g["gain"] for g in g_mix], axis=0),
        "norm_ffn": jnp.concatenate([g["gain"] for g in g_ffn], axis=0),
        "hgrn_lower_bounds": d_p[0:2],
        "hgrn_out_norm": jnp.concatenate([g["out_gain"] for g in hg], axis=0),
        "mla_q_a_norm": jnp.concatenate([g["qa"] for g in mg], axis=0),
        "mla_kv_a_norm": jnp.concatenate([g["kva"] for g in mg], axis=0),
        "mla_q_norm": jnp.concatenate([g["qn"][:, :MLA_QK] for g in mg], axis=0),
        "mla_k_norm": jnp.concatenate([g["kn"][:, :MLA_QK] for g in mg], axis=0),
        "ffn_conv_w": jnp.stack([g["conv_w"] for g in g_ffn]),
        "ffn_conv_b": jnp.concatenate([g["conv_b"] for g in g_ffn], axis=0),
    }

    def chip_part(n, p):
        size = w[n].shape[small_axis[n]]
        return lax.slice_in_dim(small_grads[n], p * size, (p + 1) * size, axis=small_axis[n])

    to_chips = jnp.stack([_pack([chip_part(n, p) for n in small_names], F32, SUBLANES) for p in range(N_CHIPS)])
    rep_g, shard_g = exchange([_pack([small_grads[n] for n in REPLICATED], F32, SUBLANES), to_chips], "all",
                              [None, (2, 1, 0)], "reduce_small")
    small_out = {}
    for names, gparts in ((REPLICATED, rep_g), (small_names, shard_g)):
        packed = adamw_packed(gparts, *[_pack([t[n] for n in names], F32, SUBLANES) for t in (w, m, v)], "adamw_small")
        unpacked = [_unpack(buf, [w[n].shape for n in names]) for buf in packed]
        for k, n in enumerate(names):
            small_out[n] = [u[k] for u in unpacked]

    result = [loss, dh[None]]
    for k in range(4):
        result += [(big_out[n] if n in big_out else small_out[n])[k] for n in WEIGHTS]
    return tuple(result)
```

```python
import numpy as np
import jax
import jax.numpy as jnp
from jax import lax
from jax.experimental import pallas as pl
from jax.experimental.pallas import tpu as pltpu

F32 = jnp.float32
BF16 = jnp.bfloat16
_MXU = BF16

RMS_EPS = 1e-6
D_MODEL = 1024
HEADS = 8
HEAD_DIM = 128
HGRN_CHUNK = 128
MLA_NOPE = 128
MLA_ROPE = 64
MLA_QK = MLA_NOPE + MLA_ROPE
MLA_SLOT = 256
MLA_LORA = 256
MLA_IN_COLS = 2 * MLA_LORA + HEAD_DIM
ROPE_THETA = 10000.0
D_FF = 2816
FF_BLOCK = 1408
LANES = 128
SUBLANES = 8

ADAM_LR = 0.001
ADAM_B1 = 0.9
ADAM_B2 = 0.999
ADAM_EPS = 1e-08
ADAM_WD = 0.01
ADAM_STEP = 10

VMEM_LIMIT = 56 * 1024 * 1024
MESH_AXES = ("x", "y", "c")
N_CHIPS = 4

_NN = ((1,), (0,))
_NT = ((1,), (1,))
_TN = ((0,), (0,))


def _dg(a, b, dims):
    return lax.dot_general(a.astype(_MXU), b.astype(_MXU), (dims, ((), ())), preferred_element_type=F32)


@jax.custom_vjp
def kdot(a, b):
    return _dg(a, b, _NN)


kdot.defvjp(lambda a, b: (_dg(a, b, _NN), (a, b)), lambda r, g: (_dg(g, r[1], _NT), _dg(r[0], g, _TN)))


@jax.custom_vjp
def kdot_nt(a, b):
    return _dg(a, b, _NT)


kdot_nt.defvjp(lambda a, b: (_dg(a, b, _NT), (a, b)), lambda r, g: (_dg(g, r[1], _NN), _dg(g, r[0], _TN)))


@jax.custom_vjp
def kdot_tn(a, b):
    return _dg(a, b, _TN)


kdot_tn.defvjp(lambda a, b: (_dg(a, b, _TN), (a, b)), lambda r, g: (_dg(r[1], g, _NT), _dg(r[0], g, _NN)))


def _pick(d, prefs):
    for p in prefs:
        if d >= p and d % p == 0:
            return p
    return d


def _params(sem):
    return pltpu.CompilerParams(dimension_semantics=sem, vmem_limit_bytes=VMEM_LIMIT)


def _sds(shape, dtype=F32):
    return jax.ShapeDtypeStruct(shape, dtype)


def win(arr, layer, row_off=0, col_off=0, rows=None, cols=None):
    return (arr, layer, row_off, col_off, rows or arr.shape[1] - row_off, cols or arr.shape[2] - col_off)


def mm(a, b, mode, name, add=None, out_dtype=F32, into=None, rms_gain=None):
    if isinstance(b, tuple):
        b_arr, b_layer, b_r0, b_c0, b_rows, b_cols = b
    else:
        b_arr, b_layer, b_r0, b_c0, (b_rows, b_cols) = b, None, 0, 0, b.shape
    if mode == "nn":
        (M, K), (K2, N) = a.shape, (b_rows, b_cols)
    elif mode == "nt":
        (M, K), (N, K2) = a.shape, (b_rows, b_cols)
    else:
        (K, M), (K2, N) = a.shape, (b_rows, b_cols)
    assert K == K2, (name, a.shape, b_rows, b_cols)
    tm = M if M <= 1024 else _pick(M, (1024, 1408, 512, 256, 128))
    tn = N if N <= 1024 else _pick(N, (1024, 1408, 512, 256, 128))
    tk = K if K <= 2048 else _pick(K, (2048, 2816, 1024, 512, 256, 128))
    nk = K // tk
    dims = {"nn": _NN, "nt": _NT, "tn": _TN}[mode]
    a_spec = pl.BlockSpec((tk, tm), lambda i, j, k: (k, i)) if mode == "tn" else pl.BlockSpec((tm, tk), lambda i, j, k: (i, k))
    b_blk = (tn, tk) if mode == "nt" else (tk, tn)
    assert b_r0 % b_blk[0] == 0 and b_c0 % b_blk[1] == 0, (name, b_r0, b_c0, b_blk)
    br, bc = b_r0 // b_blk[0], b_c0 // b_blk[1]
    if mode == "nt":
        b_idx = lambda i, j, k: (br + j, bc + k)
    else:
        b_idx = lambda i, j, k: (br + k, bc + j)
    if b_layer is None:
        b_spec = pl.BlockSpec(b_blk, b_idx)
    else:
        b_spec = pl.BlockSpec((None,) + b_blk, lambda i, j, k: (b_layer,) + b_idx(i, j, k))
    plain = pl.BlockSpec((tm, tn), lambda i, j, k: (i, j))
    has_add, has_rms = add is not None, rms_gain is not None
    assert not has_rms or (tn == N and into is None), name
    ins = [a, b_arr] + ([add] if has_add else []) + ([rms_gain] if has_rms else [])
    specs = [a_spec, b_spec] + ([plain] if has_add else []) + ([pl.BlockSpec((1, tn), lambda i, j, k: (0, j))] if has_rms else [])
    n_in = len(ins)
    aliases = {}
    if into is None:
        o_spec, out_shape = plain, _sds((M, N), out_dtype)
    else:
        buf, o_layer, o_r0, o_c0, o_rows, o_cols = into
        assert (o_rows, o_cols) == (M, N) and o_r0 % tm == 0 and o_c0 % tn == 0, (name, into[1:], M, N, tm, tn)
        orow, ocol = o_r0 // tm, o_c0 // tn
        o_spec = pl.BlockSpec((None, tm, tn), lambda i, j, k: (o_layer, orow + i, ocol + j))
        out_shape = _sds(buf.shape, buf.dtype)
        aliases = {len(ins): 0}
        ins.append(buf)
        specs.append(pl.BlockSpec(memory_space=pl.ANY))

    n_all_in = len(ins)

    def kern(*refs):
        a_ref, b_ref = refs[0], refs[1]
        add_ref = refs[2] if has_add else None
        o_ref = refs[n_all_in]

        def finish(r):
            if has_add:
                r = r + add_ref[...].astype(F32)
            o_ref[...] = r.astype(o_ref.dtype)
            if has_rms:
                refs[n_all_in + 1][...] = _rms(r, refs[n_in - 1][...], N).astype(_MXU)

        if nk == 1:
            finish(_dg(a_ref[...], b_ref[...], dims))
            return
        acc = refs[-1]
        k = pl.program_id(2)

        @pl.when(k == 0)
        def _():
            acc[...] = jnp.zeros_like(acc)

        acc[...] += _dg(a_ref[...], b_ref[...], dims)

        @pl.when(k == nk - 1)
        def _():
            finish(acc[...])

    if has_rms:
        o_spec, out_shape = [o_spec, plain], [out_shape, _sds((M, N), _MXU)]
    return pl.pallas_call(
        kern, name=name, grid=(M // tm, N // tn, nk), in_specs=specs, out_specs=o_spec, out_shape=out_shape,
        scratch_shapes=[pltpu.VMEM((tm, tn), F32)] if nk > 1 else [], input_output_aliases=aliases,
        compiler_params=_params(("parallel", "parallel", "arbitrary")))(*ins)


def _store(ref, val, first):
    if first is None:
        ref[...] = val.astype(ref.dtype)
        return

    @pl.when(first)
    def _():
        ref[...] = val.astype(ref.dtype)

    @pl.when(jnp.logical_not(first))
    def _():
        ref[...] += val.astype(ref.dtype)


def tilecall(body, name, grid, ins, outs, sem, prefetch=(), aliases=None):
    n_pre, n_in = len(prefetch), len(ins)

    def kern(*refs):
        vals = body(*refs[:n_pre + n_in])
        for ref, val, (_, _, first) in zip(refs[n_pre + n_in:], vals, outs):
            _store(ref, val, None if first is None else first())

    in_specs, out_specs = [s for _, s in ins], [s for _, s, _ in outs]
    kwargs = dict(name=name, out_shape=[sh for sh, _, _ in outs], compiler_params=_params(sem),
                  input_output_aliases={n_pre + k: v for k, v in (aliases or {}).items()})
    if n_pre:
        kwargs["grid_spec"] = pltpu.PrefetchScalarGridSpec(num_scalar_prefetch=n_pre, grid=grid, in_specs=in_specs,
                                                           out_specs=out_specs)
    else:
        kwargs.update(grid=grid, in_specs=in_specs, out_specs=out_specs)
    return pl.pallas_call(kern, **kwargs)(*prefetch, *[a for a, _ in ins])


def _rms(x, g, n):
    ms = jnp.sum(x * x, axis=-1, keepdims=True) / n
    return x * lax.rsqrt(ms + RMS_EPS) * g


def _row_tile(S, w):
    return min(S, 512 if w <= 1024 else 256)


def rms_fwd(x, g, name, col=0, w=None):
    S = x.shape[0]
    w = w or x.shape[1]
    ts = _row_tile(S, w)
    return tilecall(
        lambda x_ref, g_ref: (_rms(x_ref[...], g_ref[...], w),), name, (S // ts,),
        [(x, pl.BlockSpec((ts, w), lambda i: (i, col))), (g, pl.BlockSpec((1, w), lambda i: (0, 0)))],
        [(_sds((S, w), _MXU), pl.BlockSpec((ts, w), lambda i: (i, 0)), None)], ("parallel",))[0]


def rms_bwd(x, g, dh, res, name, w=None):
    S = x.shape[0]
    w = w or x.shape[1]
    ts = _row_tile(S, w)

    def body(x_ref, g_ref, dh_ref, *rest):
        _, vjp = jax.vjp(lambda xv, gv: _rms(xv, gv, w), x_ref[...], g_ref[...])
        dx, dg = vjp(dh_ref[...].astype(F32))
        if rest:
            dx = dx + rest[0][...]
        return dx, dg

    row = pl.BlockSpec((ts, w), lambda i: (i, 0))
    vec = pl.BlockSpec((1, w), lambda i: (0, 0))
    ins = [(x, row), (g, vec), (dh, row)] + ([(res, row)] if res is not None else [])
    return tilecall(body, name, (S // ts,), ins,
                    [(_sds((S, w)), row, None), (_sds((1, w)), vec, lambda: pl.program_id(0) == 0)], ("arbitrary",))


def _shifted(u, halo_ref, is_first):
    rid = lax.broadcasted_iota(jnp.int32, (SUBLANES, 1), 0)
    h7 = jnp.where(is_first, 0.0, halo_ref[7:8, :])
    h6 = jnp.where(is_first, 0.0, halo_ref[6:7, :])
    r1, r2 = pltpu.roll(u, 1, 0), pltpu.roll(u, 2, 0)
    top1 = jnp.where(rid == 0, h7, r1[:SUBLANES])
    top2 = jnp.where(rid == 0, h6, jnp.where(rid == 1, h7, r2[:SUBLANES]))
    return jnp.concatenate([top1, r1[SUBLANES:]], axis=0), jnp.concatenate([top2, r2[SUBLANES:]], axis=0)


def _conv(u, u1, u2, cw_ref, cb_ref):
    return ((cb_ref[...] + u2 * cw_ref[0:1, :]) + u1 * cw_ref[1:2, :]) + u * cw_ref[2:3, :]


def _ffn_specs(S, ts, jmap):
    hb = ts // SUBLANES
    return (pl.BlockSpec((ts, FF_BLOCK), lambda j, i: (i, jmap(j))),
            pl.BlockSpec((SUBLANES, FF_BLOCK), lambda j, i: (jnp.maximum(i * hb - 1, 0), jmap(j))),
            pl.BlockSpec((SUBLANES, FF_BLOCK), lambda j, i: (0, jmap(j))),
            pl.BlockSpec((1, FF_BLOCK), lambda j, i: (0, jmap(j))))


def ffn_act_fwd(u, cw8, cb, name):
    S = u.shape[0]
    ts = _row_tile(S, 2 * D_FF)
    nb = D_FF // FF_BLOCK

    def body(ug, hg, cwg, cbg, uu, hu, cwu, cbu):
        first = pl.program_id(1) == 0
        g = ug[...]
        g1, g2 = _shifted(g, hg, first)
        yg = _conv(g, g1, g2, cwg, cbg)
        v = uu[...]
        v1, v2 = _shifted(v, hu, first)
        yu = _conv(v, v1, v2, cwu, cbu)
        return (yg * jax.nn.sigmoid(yg) * yu,)

    sg = _ffn_specs(S, ts, lambda j: j)
    su = _ffn_specs(S, ts, lambda j: j + nb)
    ins = [(u, sg[0]), (u, sg[1]), (cw8, sg[2]), (cb, sg[3]), (u, su[0]), (u, su[1]), (cw8, su[2]), (cb, su[3])]
    return tilecall(body, name, (nb, S // ts), ins,
                    [(_sds((S, D_FF), _MXU), pl.BlockSpec((ts, FF_BLOCK), lambda j, i: (i, j)), None)],
                    ("parallel", "parallel"))[0]


def ffn_act_bwd(u, cw8, cb, da, name):
    S = u.shape[0]
    ts = _row_tile(S, 2 * D_FF)
    nb = D_FF // FF_BLOCK

    def taps(dy, x, x1, x2):
        return jnp.concatenate(
            [jnp.sum(dy * x2, axis=0, keepdims=True), jnp.sum(dy * x1, axis=0, keepdims=True),
             jnp.sum(dy * x, axis=0, keepdims=True), jnp.zeros((SUBLANES - 3, dy.shape[1]), F32)], axis=0)

    def body(ug, hg, cwg, cbg, uu, hu, cwu, cbu, da_ref):
        first = pl.program_id(1) == 0
        g = ug[...]
        g1, g2 = _shifted(g, hg, first)
        yg = _conv(g, g1, g2, cwg, cbg)
        v = uu[...]
        v1, v2 = _shifted(v, hu, first)
        yu = _conv(v, v1, v2, cwu, cbu)
        d = da_ref[...]
        sg = jax.nn.sigmoid(yg)
        dyg = d * yu * (sg * (1.0 + yg * (1.0 - sg)))
        dyu = d * (yg * sg)
        return (dyg, dyu, taps(dyg, g, g1, g2), taps(dyu, v, v1, v2),
                jnp.sum(dyg, axis=0, keepdims=True), jnp.sum(dyu, axis=0, keepdims=True))

    sg_ = _ffn_specs(S, ts, lambda j: j)
    su_ = _ffn_specs(S, ts, lambda j: j + nb)
    row = pl.BlockSpec((ts, FF_BLOCK), lambda j, i: (i, j))
    ins = [(u, sg_[0]), (u, sg_[1]), (cw8, sg_[2]), (cb, sg_[3]), (u, su_[0]), (u, su_[1]), (cw8, su_[2]), (cb, su_[3]), (da, row)]
    first_row = lambda: pl.program_id(1) == 0
    dy, dcw, dcb = (_sds((S, D_FF)), row, None), (_sds((SUBLANES, D_FF)), sg_[2], first_row), (_sds((1, D_FF)), sg_[3], first_row)
    return tilecall(body, name, (nb, S // ts), ins, [dy, dy, dcw, dcw, dcb, dcb], ("parallel", "arbitrary"))


def ffn_conv_bwd(dyg, dyu, cw8, name):
    S = dyg.shape[0]
    ts = _row_tile(S, 2 * D_FF)
    hb = ts // SUBLANES
    nrow = S // ts
    nb = D_FF // FF_BLOCK

    def back(dy_ref, halo_ref, cw_ref):
        last = pl.program_id(1) == nrow - 1
        d = dy_ref[...]
        rid = lax.broadcasted_iota(jnp.int32, (SUBLANES, 1), 0)
        n0 = jnp.where(last, 0.0, halo_ref[0:1, :])
        n1 = jnp.where(last, 0.0, halo_ref[1:2, :])
        r1, r2 = pltpu.roll(d, ts - 1, 0), pltpu.roll(d, ts - 2, 0)
        end1 = jnp.where(rid == SUBLANES - 1, n0, r1[ts - SUBLANES:])
        end2 = jnp.where(rid == SUBLANES - 1, n1, jnp.where(rid == SUBLANES - 2, n0, r2[ts - SUBLANES:]))
        d1 = jnp.concatenate([r1[:ts - SUBLANES], end1], axis=0)
        d2 = jnp.concatenate([r2[:ts - SUBLANES], end2], axis=0)
        return d * cw_ref[2:3, :] + d1 * cw_ref[1:2, :] + d2 * cw_ref[0:1, :]

    row = pl.BlockSpec((ts, FF_BLOCK), lambda j, i: (i, j))
    halo = pl.BlockSpec((SUBLANES, FF_BLOCK), lambda j, i: (jnp.minimum((i + 1) * hb, S // SUBLANES - 1), j))
    ins = [(dyg, row), (dyg, halo), (cw8, pl.BlockSpec((SUBLANES, FF_BLOCK), lambda j, i: (0, j))),
           (dyu, row), (dyu, halo), (cw8, pl.BlockSpec((SUBLANES, FF_BLOCK), lambda j, i: (0, j + nb)))]
    out = (_sds((S, D_FF), _MXU), row, None)
    return tilecall(lambda a, b, c, d, e, f: (back(a, b, c), back(d, e, f)), name, (nb, nrow), ins, [out, out],
                    ("parallel", "parallel"))


def _hgrn_levels(C):
    out, m = [], C // 2
    while m >= 1:
        out.append(m)
        m //= 2
    return out


def _hgrn_sum_matrix(C):
    t = np.arange(C)[:, None]
    u = np.arange(C)[None, :]
    blocks = [u <= t, u > t]
    for m in _hgrn_levels(C):
        r = (t // (2 * m)) * (2 * m) + m
        right = (t % (2 * m)) >= m
        blocks.append((right & (u > r) & (u <= t)) | ((~right) & (u > t) & (u <= r)))
    return np.concatenate(blocks, axis=0).astype(np.float32)


def _make_partial_sums(nb, C):
    @jax.custom_vjp
    def sums(mall, lf):
        hi = lf.astype(_MXU)
        mid = (lf - hi.astype(F32)).astype(_MXU)
        e2 = _dg(mall, jnp.concatenate([hi, mid], axis=1), _NN)
        e = e2[:, :HEAD_DIM] + e2[:, HEAD_DIM:]
        return tuple(e[b * C:(b + 1) * C] for b in range(nb))

    def fwd(mall, lf):
        return sums(mall, lf), mall

    def bwd(mall, gs):
        return jnp.zeros_like(mall), _dg(mall, jnp.concatenate(gs, axis=0), _TN)

    sums.defvjp(fwd, bwd)
    return sums


def _hgrn_chunk(zq, zf, v, lb, st, mall, C):
    levels = _hgrn_levels(C)
    qs = zq * jax.nn.sigmoid(zq)
    fg = lb + (1.0 - lb) * jax.nn.sigmoid(zf)
    k = 1.0 - fg
    e = _make_partial_sums(2 + len(levels), C)(mall, jnp.log(fg))
    g_incl, g_after = e[0], e[1]
    rid = lax.broadcasted_iota(jnp.int32, (C, 1), 0)
    tt = lax.broadcasted_iota(jnp.int32, (C, C), 0)
    ss = lax.broadcasted_iota(jnp.int32, (C, C), 1)
    o = kdot_nt(qs * jnp.exp(g_incl), st)
    o = o + jnp.sum(qs * k, axis=-1, keepdims=True) * v
    scores = jnp.zeros((C, C), F32)
    for li, m in enumerate(levels):
        sh = int(np.log2(m))
        right = ((rid >> sh) & 1) == 1
        both = jnp.where(right, qs, k) * jnp.exp(e[2 + li])
        pair = ((tt >> (sh + 1)) == (ss >> (sh + 1))) & (((tt >> sh) & 1) == 1) & (((ss >> sh) & 1) == 0)
        scores = scores + jnp.where(pair, kdot_nt(both, both), 0.0)
    o = o + kdot(scores, v)
    g_last = jnp.sum(jnp.where(rid == C - 1, g_incl, 0.0), axis=0, keepdims=True)
    st_new = st * jnp.exp(g_last) + kdot_tn(v, k * jnp.exp(g_after))
    return o, st_new


HGRN_HEADS_PER_STEP = 4
_HGRN_LANES = HGRN_HEADS_PER_STEP * HEAD_DIM


def _hgrn_in_specs(C, nc, rev):
    cm = (lambda c: nc - 1 - c) if rev else (lambda c: c)
    blk = lambda: pl.BlockSpec((C, _HGRN_LANES), lambda h, c: (cm(c), h))
    return cm, [blk(), blk(), blk(), pl.BlockSpec((1, _HGRN_LANES), lambda h, c: (0, h))]


def _hgrn_state_spec(cm):
    return pl.BlockSpec((HGRN_HEADS_PER_STEP, None, HEAD_DIM, HEAD_DIM), lambda h, c: (h, cm(c), 0, 0))


def hgrn_fwd(zq, zf, zi, lb, mall, name):
    S = zq.shape[0]
    C = min(HGRN_CHUNK, S)
    nc = S // C

    def kern(zq_ref, zf_ref, zi_ref, lb_ref, mall_ref, o_ref, st_ref, st):
        @pl.when(pl.program_id(1) == 0)
        def _():
            st[...] = jnp.zeros_like(st)

        mall_v = mall_ref[...]
        for g in range(HGRN_HEADS_PER_STEP):
            lanes = slice(g * HEAD_DIM, (g + 1) * HEAD_DIM)
            s_in = st[g]
            st_ref[g] = s_in
            o, s_new = _hgrn_chunk(zq_ref[:, lanes], zf_ref[:, lanes], zi_ref[:, lanes], lb_ref[:, lanes], s_in, mall_v, C)
            o_ref[:, lanes] = o
            st[g] = s_new

    cm, specs = _hgrn_in_specs(C, nc, False)
    return pl.pallas_call(
        kern, name=name, grid=(HEADS // HGRN_HEADS_PER_STEP, nc),
        in_specs=specs + [pl.BlockSpec(mall.shape, lambda h, c: (0, 0))],
        out_specs=[pl.BlockSpec((C, _HGRN_LANES), lambda h, c: (c, h)), _hgrn_state_spec(cm)],
        out_shape=[_sds((S, D_MODEL)), _sds((HEADS, nc, HEAD_DIM, HEAD_DIM))],
        scratch_shapes=[pltpu.VMEM((HGRN_HEADS_PER_STEP, HEAD_DIM, HEAD_DIM), F32)],
        compiler_params=_params(("parallel", "arbitrary")))(zq, zf, zi, lb, mall)


def hgrn_bwd(zq, zf, zi, lb, mall, states, do, name):
    S = zq.shape[0]
    C = min(HGRN_CHUNK, S)
    nc = S // C

    def kern(zq_ref, zf_ref, zi_ref, lb_ref, mall_ref, st_ref, do_ref, dq_ref, df_ref, di_ref, dlb_ref, dst):
        first = pl.program_id(1) == 0

        @pl.when(first)
        def _():
            dst[...] = jnp.zeros_like(dst)

        mall_v = mall_ref[...]
        gls = []
        for g in range(HGRN_HEADS_PER_STEP):
            lanes = slice(g * HEAD_DIM, (g + 1) * HEAD_DIM)
            _, vjp = jax.vjp(lambda a, b, c, d, e: _hgrn_chunk(a, b, c, d, e, mall_v, C),
                             zq_ref[:, lanes], zf_ref[:, lanes], zi_ref[:, lanes], lb_ref[:, lanes], st_ref[g])
            ga, gb, gv, gl, gs = vjp((do_ref[:, lanes], dst[g]))
            dq_ref[:, lanes] = ga.astype(dq_ref.dtype)
            df_ref[:, lanes] = gb.astype(df_ref.dtype)
            di_ref[:, lanes] = gv.astype(di_ref.dtype)
            gls.append(gl)
            dst[g] = gs
        _store(dlb_ref, jnp.concatenate(gls, axis=1), first)

    cm, specs = _hgrn_in_specs(C, nc, True)
    row = lambda: pl.BlockSpec((C, _HGRN_LANES), lambda h, c: (cm(c), h))
    return pl.pallas_call(
        kern, name=name, grid=(HEADS // HGRN_HEADS_PER_STEP, nc),
        in_specs=specs + [pl.BlockSpec(mall.shape, lambda h, c: (0, 0)), _hgrn_state_spec(cm), row()],
        out_specs=[row(), row(), row(), pl.BlockSpec((1, _HGRN_LANES), lambda h, c: (0, h))],
        out_shape=[_sds((S, D_MODEL), _MXU), _sds((S, D_MODEL), _MXU), _sds((S, D_MODEL), _MXU), _sds((1, D_MODEL))],
        scratch_shapes=[pltpu.VMEM((HGRN_HEADS_PER_STEP, HEAD_DIM, HEAD_DIM), F32)],
        compiler_params=_params(("parallel", "arbitrary")))(zq, zf, zi, lb, mall, states, do)


def _hgrn_out(o, g, gain):
    return _rms(o, gain, HEAD_DIM) * (g * jax.nn.sigmoid(g))


def hgrn_out_fwd(o, zg, gain, name):
    S = o.shape[0]
    ts = _row_tile(S, D_MODEL)
    blk = pl.BlockSpec((ts, HEAD_DIM), lambda i, h: (i, h))
    return tilecall(lambda o_ref, g_ref, w_ref: (_hgrn_out(o_ref[...], g_ref[...], w_ref[...]),), name, (S // ts, HEADS),
                    [(o, blk), (zg, blk), (gain, pl.BlockSpec((1, HEAD_DIM), lambda i, h: (0, 0)))],
                    [(_sds((S, D_MODEL), _MXU), blk, None)], ("parallel", "parallel"))[0]


def hgrn_out_bwd(o, zg, gain, don, name):
    S = o.shape[0]
    ts = _row_tile(S, D_MODEL)
    blk = pl.BlockSpec((ts, HEAD_DIM), lambda i, h: (i, h))
    vec = pl.BlockSpec((1, HEAD_DIM), lambda i, h: (0, 0))

    def body(o_ref, g_ref, w_ref, d_ref):
        _, vjp = jax.vjp(_hgrn_out, o_ref[...], g_ref[...], w_ref[...])
        return vjp(d_ref[...])

    return tilecall(body, name, (S // ts, HEADS), [(o, blk), (zg, blk), (gain, vec), (don, blk)],
                    [(_sds((S, D_MODEL)), blk, None), (_sds((S, D_MODEL), _MXU), blk, None),
                     (_sds((1, HEAD_DIM)), vec, lambda: (pl.program_id(0) == 0) & (pl.program_id(1) == 0))],
                    ("arbitrary", "arbitrary"))


def _lb_soft(p0, p1):
    mx = jnp.maximum(p0, p1)
    e0, e1 = jnp.exp(p0 - mx), jnp.exp(p1 - mx)
    s0, s1 = e0 / (e0 + e1), e1 / (e0 + e1)
    return (s0 + s1) - s0


def lower_bound_fwd(p, name):
    assert p.shape[0] == 2

    def body(p_ref):
        s = _lb_soft(p_ref[0:1, :], p_ref[1:2, :])
        return (jnp.concatenate([jnp.zeros_like(s), s] + [jnp.zeros_like(s)] * (SUBLANES - 2), axis=0),)

    spec8 = pl.BlockSpec((SUBLANES, p.shape[1]), lambda: (0, 0))
    return tilecall(body, name, (), [(p, pl.BlockSpec(p.shape, lambda: (0, 0)))], [(_sds((SUBLANES, p.shape[1])), spec8, None)], ())[0]


def lower_bound_bwd(p, dlb1, name):
    def body(p_ref, d_ref):
        _, vjp = jax.vjp(_lb_soft, p_ref[0:1, :], p_ref[1:2, :])
        g0, g1 = vjp(d_ref[...])
        return (jnp.concatenate([g0, g1] + [jnp.zeros_like(g0)] * (SUBLANES - 2), axis=0),)

    spec8 = pl.BlockSpec((SUBLANES, p.shape[1]), lambda: (0, 0))
    return tilecall(body, name, (), [(p, pl.BlockSpec(p.shape, lambda: (0, 0))), (dlb1, pl.BlockSpec(dlb1.shape, lambda: (0, 0)))],
                    [(_sds((SUBLANES, p.shape[1])), spec8, None)], ())[0]


@jax.custom_vjp
def _swap_rope_halves(x):
    lane = lax.broadcasted_iota(jnp.int32, x.shape, 1)
    lo = (lane >= MLA_NOPE) & (lane < MLA_NOPE + MLA_ROPE // 2)
    hi = (lane >= MLA_NOPE + MLA_ROPE // 2) & (lane < MLA_QK)
    return jnp.where(lo, pltpu.roll(x, MLA_SLOT - MLA_ROPE // 2, 1), jnp.where(hi, pltpu.roll(x, MLA_ROPE // 2, 1), 0.0))


_swap_rope_halves.defvjp(lambda x: (_swap_rope_halves(x), None), lambda _, g: (_swap_rope_halves(g),))


def _norm_rope(x, gain, cos_t, sin_t):
    y = _rms(x, gain, MLA_QK)
    return y * cos_t + _swap_rope_halves(y) * sin_t


_ATTN_SCALE = MLA_QK ** -0.5


def _qk_heads(qs, kn, kr, qn, kn_gain, cos_t, sin_t):
    q = _norm_rope(qs, qn, cos_t, sin_t) * _ATTN_SCALE
    k = _norm_rope(jnp.concatenate([kn, kr], axis=1), kn_gain, cos_t, sin_t)
    return q, k


def _qk_specs(ts):
    slot = pl.BlockSpec((ts, MLA_SLOT), lambda i, h: (i, h))
    nope = pl.BlockSpec((ts, HEAD_DIM), lambda i, h: (i, 2 * h))
    shared = pl.BlockSpec((ts, HEAD_DIM), lambda i, h: (i, 0))
    gain = pl.BlockSpec((1, MLA_SLOT), lambda i, h: (0, 0))
    table = pl.BlockSpec((ts, MLA_SLOT), lambda i, h: (i, 0))
    return slot, nope, shared, gain, table


def qk_fwd(qslots, kv, krope, qn, kn, cos_t, sin_t, name):
    S = qslots.shape[0]
    ts = _row_tile(S, D_MODEL)
    slot, nope, shared, gain, table = _qk_specs(ts)

    def body(q_ref, kn_ref, kr_ref, qn_ref, kg_ref, c_ref, s_ref):
        return _qk_heads(q_ref[...], kn_ref[...], kr_ref[...], qn_ref[...], kg_ref[...], c_ref[...], s_ref[...])

    out = _sds((S, HEADS * MLA_SLOT), _MXU)
    return tilecall(body, name, (S // ts, HEADS),
                    [(qslots, slot), (kv, nope), (krope, shared), (qn, gain), (kn, gain), (cos_t, table), (sin_t, table)],
                    [(out, slot, None), (out, slot, None)], ("parallel", "parallel"))


def qk_bwd(qslots, kv, krope, qn, kn, cos_t, sin_t, dq, dk, dv, name):
    S = qslots.shape[0]
    ts = _row_tile(S, D_MODEL)
    slot, nope, shared, gain, table = _qk_specs(ts)
    vblk = pl.BlockSpec((ts, HEAD_DIM), lambda i, h: (i, h))

    def body(q_ref, kn_ref, kr_ref, qn_ref, kg_ref, c_ref, s_ref, dq_ref, dk_ref, dv_ref):
        c, s = c_ref[...], s_ref[...]
        _, vjp = jax.vjp(lambda a, b, r, g1, g2: _qk_heads(a, b, r, g1, g2, c, s),
                         q_ref[...], kn_ref[...], kr_ref[...], qn_ref[...], kg_ref[...])
        ga, gb, gr, g1, g2 = vjp((dq_ref[...], dk_ref[...]))
        return ga, jnp.concatenate([gb, dv_ref[...]], axis=1), gr, g1, g2

    first_head = lambda: pl.program_id(1) == 0
    first = lambda: (pl.program_id(0) == 0) & (pl.program_id(1) == 0)
    wide = _sds((S, HEADS * MLA_SLOT), _MXU)
    return tilecall(body, name, (S // ts, HEADS),
                    [(qslots, slot), (kv, nope), (krope, shared), (qn, gain), (kn, gain), (cos_t, table), (sin_t, table),
                     (dq, slot), (dk, slot), (dv, vblk)],
                    [(wide, slot, None), (wide, slot, None), (_sds((S, HEAD_DIM)), shared, first_head),
                     (_sds((1, MLA_SLOT)), gain, first), (_sds((1, MLA_SLOT)), gain, first)], ("arbitrary", "arbitrary"))


ATTN_TILE_FWD = 1024
ATTN_TILE = 1024
ATTN_HEADS_PER_STEP = 2
ATTN_HEADS_PER_STEP_BWD = 1


def _causal_pairs(nq, by_row):
    pairs = [(i, j) for i in range(nq) for j in range(i + 1)] if by_row else [(i, j) for j in range(nq) for i in range(j, nq)]
    return jnp.asarray([p[0] for p in pairs], jnp.int32), jnp.asarray([p[1] for p in pairs], jnp.int32)


def _diag_mask(s, tq):
    rows = lax.broadcasted_iota(jnp.int32, (tq, tq), 0)
    cols = lax.broadcasted_iota(jnp.int32, (tq, tq), 1)
    return jnp.where(rows >= cols, s, -jnp.inf)


def attn_fwd(qr, kr, kv, name):
    S = qr.shape[0]
    tq = min(S, ATTN_TILE_FWD)
    nq = S // tq
    i_tab, j_tab = _causal_pairs(nq, True)

    G = ATTN_HEADS_PER_STEP
    reps = tq // LANES

    def kern(it, jt, q_ref, k_ref, kv_ref, o_ref, lse_ref, m_s, l_s, acc):
        n = pl.program_id(1)
        i, j = it[n], jt[n]

        @pl.when(j == 0)
        def _():
            m_s[...] = jnp.full_like(m_s, -jnp.inf)
            l_s[...] = jnp.zeros_like(l_s)
            acc[...] = jnp.zeros_like(acc)

        def step(diagonal):
            for g in range(G):
                slot = slice(g * MLA_SLOT, (g + 1) * MLA_SLOT)
                s = _dg(q_ref[:, slot], k_ref[:, slot], _NT)
                if diagonal:
                    s = _diag_mask(s, tq)
                m_prev = m_s[g]
                m_new = jnp.maximum(m_prev, jnp.max(s, axis=-1, keepdims=True))
                alpha = jnp.exp(m_prev - m_new)
                p = jnp.exp(s - jnp.tile(m_new, (1, reps)))
                l_s[g] = alpha * l_s[g] + jnp.sum(p, axis=-1, keepdims=True)
                acc[g] = alpha * acc[g] + _dg(p, kv_ref[:, g * MLA_SLOT + HEAD_DIM:(g + 1) * MLA_SLOT], _NN)
                m_s[g] = m_new

        @pl.when(j < i)
        def _():
            step(False)

        @pl.when(j == i)
        def _():
            step(True)
            for g in range(G):
                l = l_s[g]
                lanes = slice(g * HEAD_DIM, (g + 1) * HEAD_DIM)
                o_ref[:, lanes] = acc[g] / l
                lse_ref[:, lanes] = m_s[g] + jnp.log(l)

    out = _sds((S, HEADS * HEAD_DIM))
    oblk = pl.BlockSpec((tq, G * HEAD_DIM), lambda h, n, it, jt: (it[n], h))
    stat = pltpu.VMEM((G, tq, HEAD_DIM), F32)
    return pl.pallas_call(
        kern, name=name,
        grid_spec=pltpu.PrefetchScalarGridSpec(
            num_scalar_prefetch=2, grid=(HEADS // G, i_tab.shape[0]),
            in_specs=[pl.BlockSpec((tq, G * MLA_SLOT), lambda h, n, it, jt: (it[n], h)),
                      pl.BlockSpec((tq, G * MLA_SLOT), lambda h, n, it, jt: (jt[n], h)),
                      pl.BlockSpec((tq, G * MLA_SLOT), lambda h, n, it, jt: (jt[n], h))],
            out_specs=[oblk, oblk], scratch_shapes=[stat, stat, stat]),
        out_shape=[out, out], compiler_params=_params(("parallel", "arbitrary")))(i_tab, j_tab, qr, kr, kv)


def attn_bwd(qr, kr, kv, o, lse, do, name):
    S = qr.shape[0]
    tq = min(S, ATTN_TILE)
    nq = S // tq
    i_tab, j_tab = _causal_pairs(nq, False)

    G = ATTN_HEADS_PER_STEP_BWD

    def kern(it, jt, q_ref, k_ref, kv_ref, o_ref, lse_ref, do_ref, dq_ref, dk_ref, dv_ref, dk_acc, dv_acc):
        n = pl.program_id(1)
        i, j = it[n], jt[n]

        @pl.when(n == 0)
        def _():
            dq_ref[...] = jnp.zeros_like(dq_ref)

        @pl.when(i == j)
        def _():
            dk_acc[...] = jnp.zeros_like(dk_acc)
            dv_acc[...] = jnp.zeros_like(dv_acc)

        def step(diagonal):
            rows = pl.ds(pl.multiple_of(i * tq, tq), tq)
            for g in range(G):
                slot = slice(g * MLA_SLOT, (g + 1) * MLA_SLOT)
                lanes = slice(g * HEAD_DIM, (g + 1) * HEAD_DIM)
                q, k = q_ref[:, slot], k_ref[:, slot]
                s = _dg(q, k, _NT) - jnp.tile(lse_ref[:, lanes], (1, tq // LANES))
                if diagonal:
                    s = _diag_mask(s, tq)
                p = jnp.exp(s)
                d = do_ref[:, lanes]
                delta = jnp.sum(d * o_ref[:, lanes], axis=-1, keepdims=True)
                dv_acc[:, lanes] += _dg(p, d, _TN)
                ds = p * (_dg(d, kv_ref[:, g * MLA_SLOT + HEAD_DIM:(g + 1) * MLA_SLOT], _NT) - delta)
                dk_acc[:, slot] += _dg(ds, q, _TN)
                dq_ref[rows, slot] += _dg(ds, k, _NN)

        @pl.when(i > j)
        def _():
            step(False)

        @pl.when(i == j)
        def _():
            step(True)

        @pl.when(i == nq - 1)
        def _():
            dk_ref[...] = dk_acc[...]
            dv_ref[...] = dv_acc[...]

    qblk = pl.BlockSpec((tq, G * MLA_SLOT), lambda h, n, it, jt: (it[n], h))
    oblk = pl.BlockSpec((tq, G * HEAD_DIM), lambda h, n, it, jt: (it[n], h))
    kblk = pl.BlockSpec((tq, G * MLA_SLOT), lambda h, n, it, jt: (jt[n], h))
    return pl.pallas_call(
        kern, name=name,
        grid_spec=pltpu.PrefetchScalarGridSpec(
            num_scalar_prefetch=2, grid=(HEADS // G, i_tab.shape[0]),
            in_specs=[qblk, kblk, kblk, oblk, oblk, oblk],
            out_specs=[pl.BlockSpec((S, G * MLA_SLOT), lambda h, n, it, jt: (0, h)), kblk,
                       pl.BlockSpec((tq, G * HEAD_DIM), lambda h, n, it, jt: (jt[n], h))],
            scratch_shapes=[pltpu.VMEM((tq, G * MLA_SLOT), F32), pltpu.VMEM((tq, G * HEAD_DIM), F32)]),
        out_shape=[_sds((S, HEADS * MLA_SLOT)), _sds((S, HEADS * MLA_SLOT)), _sds((S, HEADS * HEAD_DIM))],
        compiler_params=_params(("parallel", "arbitrary")))(i_tab, j_tab, qr, kr, kv, o, lse, do)


def loss_head(y, target, name):
    S, Dm = y.shape
    ts = _row_tile(S, Dm)

    def body(y_ref, t_ref):
        e = y_ref[...] - t_ref[...]
        tot = jnp.sum(jnp.sum(e * e, axis=-1, keepdims=True) / Dm, axis=0, keepdims=True)
        return e / Dm, jnp.broadcast_to(0.5 * tot, (SUBLANES, LANES))

    row = pl.BlockSpec((ts, Dm), lambda i: (i, 0))
    return tilecall(body, name, (S // ts,), [(y, row), (target, row)],
                    [(_sds((S, Dm)), row, None),
                     (_sds((SUBLANES, LANES)), pl.BlockSpec((SUBLANES, LANES), lambda i: (0, 0)), lambda: pl.program_id(0) == 0)],
                    ("arbitrary",))


_CHIP_FLIPS = ((1, 0), (0, 1), (1, 1))
_PEER_FLIPS = {
    "chips": ((1, 0, 0), (0, 1, 0), (1, 1, 0)),
    "sibling": ((0, 0, 1),),
    "all": tuple((a, b, c) for a in (0, 1) for b in (0, 1) for c in (0, 1))[1:],
}
_SLOT_WEIGHTS = {"chips": (2, 1, 0), "sibling": (0, 0, 1), "all": (4, 2, 1)}
_HBM = pl.BlockSpec(memory_space=pltpu.HBM)


def _me():
    return lax.axis_index("x"), lax.axis_index("y"), lax.axis_index("c")


def _remote(src, dst, send_sem, recv_sem, peer):
    return pltpu.make_async_remote_copy(src_ref=src, dst_ref=dst, send_sem=send_sem, recv_sem=recv_sem,
                                        device_id=peer, device_id_type=pl.DeviceIdType.MESH)


def exchange(arrs, group, slab_weights, name, keep_own=True):
    flips = _PEER_FLIPS[group]
    wx, wy, wc = _SLOT_WEIGHTS[group]
    n_slots = len(flips) + (1 if keep_own else 0)
    n = len(arrs)

    def slab(ref, a, pos):
        w = slab_weights[a]
        return ref if w is None else ref.at[w[0] * pos[0] + w[1] * pos[1] + w[2] * pos[2]]

    def kern(*refs):
        srcs, outs = refs[:n], refs[n:2 * n]
        send_sems, recv_sems = refs[2 * n:2 * n + 2]
        me = _me()
        my_slot = wx * me[0] + wy * me[1] + wc * me[2]
        copies = []
        if keep_own:
            local_sems = refs[2 * n + 2]
            for a in range(n):
                cp = pltpu.make_async_copy(slab(srcs[a], a, me), outs[a].at[my_slot], local_sems.at[a])
                cp.start()
                copies.append(cp)
        for f, flip in enumerate(flips):
            peer = tuple(m ^ b if b else m for m, b in zip(me, flip))
            for a in range(n):
                cp = _remote(slab(srcs[a], a, peer), outs[a].at[my_slot if keep_own else f],
                             send_sems.at[f, a], recv_sems.at[f, a], peer)
                cp.start()
                copies.append(cp)
        for cp in copies:
            cp.wait()

    out_shape = [_sds((n_slots,) + (a.shape if slab_weights[k] is None else a.shape[1:]), a.dtype) for k, a in enumerate(arrs)]
    sems = [pltpu.SemaphoreType.DMA((len(flips), n)), pltpu.SemaphoreType.DMA((len(flips), n))]
    return pl.pallas_call(
        kern, name=name, in_specs=[_HBM] * n, out_specs=[_HBM] * n, out_shape=out_shape,
        scratch_shapes=sems + ([pltpu.SemaphoreType.DMA((n,))] if keep_own else []))(*arrs)


def _chip_window(ref, kind, size, chip, layers):
    if kind == "rows":
        return ref.at[layers, pl.ds(chip * size, size), :]
    return ref.at[layers, :, pl.ds(pl.multiple_of(chip * size, LANES), size)]


def gather_big(shards, kinds, name):
    n = len(shards)
    fulls = []
    for s, kind in zip(shards, kinds):
        L, r, c = s.shape
        fulls.append(_sds((L, N_CHIPS * r, c) if kind == "rows" else (L, r, N_CHIPS * c), s.dtype))

    def kern(*refs):
        srcs, outs = refs[:n], refs[n:2 * n]
        ici_s, ici_r, d2d_s, d2d_r = refs[2 * n:]
        x, y, c = _me()
        my_chip = 2 * x + y
        ici, fwd = [], []
        for a in range(n):
            L, r, cc = shards[a].shape
            size = r if kinds[a] == "rows" else cc
            mine = pl.ds(c * (L // 2), L // 2)
            for f, (fx, fy) in enumerate(_CHIP_FLIPS):
                cp = _remote(srcs[a].at[mine], _chip_window(outs[a], kinds[a], size, my_chip, mine),
                             ici_s.at[a, f], ici_r.at[a, f], (x ^ fx, y ^ fy, c))
                cp.start()
                ici.append(cp)
        for a in range(n):
            L, r, cc = shards[a].shape
            size = r if kinds[a] == "rows" else cc
            mine = pl.ds(c * (L // 2), L // 2)
            for f, (fx, fy) in enumerate(_CHIP_FLIPS):
                ici[a * len(_CHIP_FLIPS) + f].wait_recv()
                landed = _chip_window(outs[a], kinds[a], size, 2 * (x ^ fx) + (y ^ fy), mine)
                cp = _remote(landed, landed, d2d_s.at[a, f], d2d_r.at[a, f], (x, y, 1 - c))
                cp.start()
                fwd.append(cp)
        for cp in ici:
            cp.wait_send()
        for cp in fwd:
            cp.wait()

    sem = pltpu.SemaphoreType.DMA((n, len(_CHIP_FLIPS)))
    return pl.pallas_call(kern, name=name, in_specs=[_HBM] * n, out_specs=[_HBM] * n, out_shape=fulls,
                          scratch_shapes=[sem, sem, sem, sem])(*shards)


def send_other_half(arrs, name):
    n = len(arrs)

    def kern(*refs):
        srcs, outs = refs[:n], refs[n:2 * n]
        send_sems, recv_sems = refs[2 * n:]
        x, y, c = _me()
        copies = []
        for a in range(n):
            hl = arrs[a].shape[0] // 2
            cp = _remote(srcs[a].at[pl.ds((1 - c) * hl, hl)], outs[a], send_sems.at[a], recv_sems.at[a], (x, y, 1 - c))
            cp.start()
            copies.append(cp)
        for cp in copies:
            cp.wait()

    return pl.pallas_call(
        kern, name=name, in_specs=[_HBM] * n, out_specs=[_HBM] * n,
        out_shape=[_sds((a.shape[0] // 2,) + a.shape[1:], a.dtype) for a in arrs],
        scratch_shapes=[pltpu.SemaphoreType.DMA((n,)), pltpu.SemaphoreType.DMA((n,))])(*arrs)


def scatter_to_chips(arrs, kinds, name):
    n = len(arrs)
    shapes = []
    for a, kind in zip(arrs, kinds):
        l, R, C = a.shape
        shapes.append((l, R // N_CHIPS, C) if kind == "rows" else (l, R, C // N_CHIPS))

    def kern(*refs):
        srcs, outs = refs[:n], refs[n:2 * n]
        send_sems, recv_sems = refs[2 * n:]
        x, y, c = _me()
        copies = []
        for a in range(n):
            size = shapes[a][1] if kinds[a] == "rows" else shapes[a][2]
            for f, (fx, fy) in enumerate(_CHIP_FLIPS):
                window = _chip_window(srcs[a], kinds[a], size, 2 * (x ^ fx) + (y ^ fy), slice(None))
                cp = _remote(window, outs[a].at[f], send_sems.at[a, f], recv_sems.at[a, f], (x ^ fx, y ^ fy, c))
                cp.start()
                copies.append(cp)
        for cp in copies:
            cp.wait()

    sem = pltpu.SemaphoreType.DMA((n, len(_CHIP_FLIPS)))
    return pl.pallas_call(
        kern, name=name, in_specs=[_HBM] * n, out_specs=[_HBM] * n,
        out_shape=[_sds((len(_CHIP_FLIPS),) + s, a.dtype) for s, a in zip(shapes, arrs)],
        scratch_shapes=[sem, sem])(*arrs)


def _stack_tile(r, c):
    for t in (1024, 704, 512, 352, 256, 128, 64, 32, 16):
        if r % t == 0 and t * c * 4 <= 3 * 512 * 1024:
            return t
    return r


def _window_map(kind, r, tr):
    nrt = r // tr
    if kind == "rows":
        return lambda l, i, chip: (l, chip[0] * nrt + i, 0)
    return lambda l, i, chip: (l, i, chip[0])


def place(full, shard, kind, chip, name):
    L, r, c = shard.shape
    tr = _stack_tile(r, c)
    wmap = _window_map(kind, r, tr)
    return tilecall(lambda chip_ref, s_ref, f_ref: (s_ref[...],), name, (L, r // tr),
                    [(shard, pl.BlockSpec((None, tr, c), lambda l, i, chip: (l, i, 0))), (full, pl.BlockSpec(memory_space=pl.ANY))],
                    [(_sds(full.shape, full.dtype), pl.BlockSpec((None, tr, c), lambda l, i, chip: wmap(l, i, chip)), None)],
                    ("parallel", "parallel"), prefetch=(chip,), aliases={1: 0})[0]


def add_cores(g, other, core, name):
    L, R, C = g.shape
    hl = L // 2
    tr = _stack_tile(R, C)
    blk = (None, tr, C)
    return tilecall(lambda core_ref, a_ref, b_ref: (a_ref[...] + b_ref[...],), name, (hl, R // tr),
                    [(g, pl.BlockSpec(blk, lambda l, i, core: (core[0] * hl + l, i, 0))),
                     (other, pl.BlockSpec(blk, lambda l, i, core: (l, i, 0)))],
                    [(_sds((hl, R, C), _MXU), pl.BlockSpec(blk, lambda l, i, core: (l, i, 0)), None)],
                    ("parallel", "parallel"), prefetch=(core,))[0]


def add_chips(own, got, kind, chip, name):
    nf, l, r, c = got.shape
    tr = _stack_tile(r, c)
    wmap = _window_map(kind, r, tr)

    def body(chip_ref, own_ref, *got_refs):
        acc = own_ref[...].astype(F32)
        for ref in got_refs:
            acc = acc + ref[...].astype(F32)
        return (acc,)

    return tilecall(body, name, (l, r // tr),
                    [(own, pl.BlockSpec((None, tr, c), lambda ll, i, chip: wmap(ll, i, chip)))] +
                    [(got, pl.BlockSpec((None, None, tr, c), lambda ll, i, chip, f=f: (f, ll, i, 0))) for f in range(nf)],
                    [(_sds((l, r, c)), pl.BlockSpec((None, tr, c), lambda ll, i, chip: (ll, i, 0)), None)],
                    ("parallel", "parallel"), prefetch=(chip,))[0]


def _adam_update(g, w, m, v):
    m_new = ADAM_B1 * m + (1.0 - ADAM_B1) * g
    v_new = ADAM_B2 * v + (1.0 - ADAM_B2) * jnp.square(g)
    m_hat = m_new / (1.0 - ADAM_B1 ** ADAM_STEP)
    v_hat = v_new / (1.0 - ADAM_B2 ** ADAM_STEP)
    delta = -ADAM_LR * (m_hat / (jnp.sqrt(v_hat) + ADAM_EPS) + ADAM_WD * w)
    return g, delta, m_new, v_new


def adamw_stacked(mine, theirs, w, m, v, core, name):
    L, r, c = w.shape
    hl = L // 2
    tr = _stack_tile(r, c)

    def body(core_ref, a_ref, b_ref, w_ref, m_ref, v_ref):
        is_mine = (pl.program_id(0) // hl) == core_ref[0]
        g = jnp.where(is_mine, a_ref[...], b_ref[...])
        return _adam_update(g, w_ref[...], m_ref[...], v_ref[...])

    full = pl.BlockSpec((None, tr, c), lambda l, i, core: (l, i, 0))
    out = (_sds((L, r, c)), full, None)
    return tilecall(body, name, (L, r // tr),
                    [(mine, pl.BlockSpec((None, tr, c), lambda l, i, core: (l % hl, i, 0))),
                     (theirs, pl.BlockSpec((None, None, tr, c), lambda l, i, core: (0, l % hl, i, 0))),
                     (w, full), (m, full), (v, full)],
                    [out, out, out, out], ("parallel", "parallel"), prefetch=(core,))


def _pack(arrs, dtype, row_multiple):
    flat = jnp.concatenate([a.reshape(-1).astype(dtype) for a in arrs])
    rows = -(-flat.shape[0] // LANES)
    rows = -(-rows // row_multiple) * row_multiple
    return jnp.pad(flat, (0, rows * LANES - flat.shape[0])).reshape(rows, LANES)


def _unpack(buf, shapes):
    flat = buf.reshape(-1)
    out, off = [], 0
    for s in shapes:
        n = int(np.prod(s))
        out.append(flat[off:off + n].reshape(s))
        off += n
    return out


def adamw_packed(gparts, w, m, v, name):
    P, R, _ = gparts.shape

    def body(g_ref, w_ref, m_ref, v_ref):
        g = g_ref[0]
        for p in range(1, P):
            g = g + g_ref[p]
        return _adam_update(g, w_ref[...], m_ref[...], v_ref[...])

    whole = pl.BlockSpec((R, LANES), lambda: (0, 0))
    out = (_sds((R, LANES)), whole, None)
    return tilecall(body, name, (), [(gparts, pl.BlockSpec((P, R, LANES), lambda: (0, 0, 0))), (w, whole), (m, whole), (v, whole)],
                    [out, out, out, out], ())


def _residual_out(a, w, x, next_gain):
    if next_gain is None:
        return mm(a, w, "nn", "mm_nn_add", add=x), None
    return mm(a, w, "nn", "mm_nn_add_rms", add=x, rms_gain=next_gain)


def _ffn_fwd(x, h, next_gain, layer, gain, wts, cw8, cb):
    u = mm(h, win(wts["ffn_w_up"], layer), "nn", "mm_nn")
    a = ffn_act_fwd(u, cw8, cb, "ffn_act_fwd")
    y, h_next = _residual_out(a, win(wts["ffn_w_down"], layer), x, next_gain)
    return y, h_next, (x, h, u, a)


def _ffn_bwd(dy, saved, layer, gain, wts, cw8, cb, grads):
    x, h, u, a = saved
    grads["ffn_w_down"] = mm(a, dy, "tn", "mm_tn_into", into=win(grads["ffn_w_down"], layer))
    da = mm(dy, win(wts["ffn_w_down"], layer), "nt", "mm_nt")
    dyg, dyu, dcw_g, dcw_u, dcb_g, dcb_u = ffn_act_bwd(u, cw8, cb, da, "ffn_act_bwd")
    dh = None
    for half, du in enumerate(ffn_conv_bwd(dyg, dyu, cw8, "ffn_conv_bwd")):
        cols = dict(col_off=half * D_FF, cols=D_FF)
        grads["ffn_w_up"] = mm(h, du, "tn", "mm_tn_into", into=win(grads["ffn_w_up"], layer, **cols))
        dh = mm(du, win(wts["ffn_w_up"], layer, **cols), "nt", "mm_nt" if dh is None else "mm_nt_add", add=dh)
    dx, d_gain = rms_bwd(x, gain, dh, dy, "rms_bwd")
    return dx, dict(gain=d_gain, conv_w=jnp.concatenate([dcw_g[0:3], dcw_u[0:3]], axis=1),
                    conv_b=jnp.concatenate([dcb_g, dcb_u], axis=1))


def _hgrn_w_in(wts, j, k):
    return win(wts["hgrn_w_in"], j, row_off=k * D_MODEL, rows=D_MODEL)


def _hgrn_layer_fwd(x, h, next_gain, j, gain, wts, lb, out_gain, mall):
    z = [mm(h, _hgrn_w_in(wts, j, k), "nn", "mm_nn") for k in range(4)]
    o, states = hgrn_fwd(z[0], z[1], z[2], lb, mall, "hgrn_fwd")
    on = hgrn_out_fwd(o, z[3], out_gain, "hgrn_out_fwd")
    y, h_next = _residual_out(on, win(wts["hgrn_w_out"], j), x, next_gain)
    return y, h_next, (x, h, z, o, states, on)


def _hgrn_layer_bwd(dy, saved, j, gain, wts, lb, out_gain, mall, grads):
    x, h, z, o, states, on = saved
    grads["hgrn_w_out"] = mm(on, dy, "tn", "mm_tn_into", into=win(grads["hgrn_w_out"], j))
    don = mm(dy, win(wts["hgrn_w_out"], j), "nt", "mm_nt")
    do, dzg, d_out_gain = hgrn_out_bwd(o, z[3], out_gain, don, "hgrn_out_bwd")
    dzq, dzf, dzi, dlb = hgrn_bwd(z[0], z[1], z[2], lb, mall, states, do, "hgrn_bwd")
    dz = [dzq, dzf, dzi, dzg]
    dh = None
    for k, d in enumerate(dz):
        grads["hgrn_w_in"] = mm(h, d, "tn", "mm_tn_into", into=win(grads["hgrn_w_in"], j, row_off=k * D_MODEL, rows=D_MODEL))
        dh = mm(d, _hgrn_w_in(wts, j, k), "nt", "mm_nt" if dh is None else "mm_nt_add", add=dh)
    dx, d_gain = rms_bwd(x, gain, dh, dy, "rms_bwd")
    return dx, dict(gain=d_gain, lb=dlb, out_gain=d_out_gain)


_MLA_IN_WINDOWS = ((0, MLA_LORA), (MLA_LORA, MLA_LORA), (2 * MLA_LORA, HEAD_DIM))


def _mla_layer_fwd(x, h, next_gain, j, gain, wts, qa_gain, kva_gain, qn, kn, cos_t, sin_t):
    (c0, n), (c1, n1), (c2, n2) = _MLA_IN_WINDOWS
    cq, cqn = mm(h, win(wts["mla_w_in"], j, col_off=c0, cols=n), "nn", "mm_nn_rms", rms_gain=qa_gain)
    ckv, ckvn = mm(h, win(wts["mla_w_in"], j, col_off=c1, cols=n1), "nn", "mm_nn_rms", rms_gain=kva_gain)
    kr = mm(h, win(wts["mla_w_in"], j, col_off=c2, cols=n2), "nn", "mm_nn")
    qslots = mm(cqn, win(wts["mla_w_q_up"], j), "nn", "mm_nn")
    kv = mm(ckvn, win(wts["mla_w_kv_up"], j), "nn", "mm_nn")
    qr, krot = qk_fwd(qslots, kv, kr, qn, kn, cos_t, sin_t, "qk_fwd")
    o, lse = attn_fwd(qr, krot, kv, "attn_fwd")
    y, h_next = _residual_out(o, win(wts["mla_w_out"], j), x, next_gain)
    return y, h_next, (x, h, cq, ckv, kr, cqn, ckvn, qslots, kv, qr, krot, o, lse)


def _mla_layer_bwd(dy, saved, j, gain, wts, qa_gain, kva_gain, qn, kn, cos_t, sin_t, grads):
    x, h, cq, ckv, kr, cqn, ckvn, qslots, kv, qr, krot, o, lse = saved
    grads["mla_w_out"] = mm(o, dy, "tn", "mm_tn_into", into=win(grads["mla_w_out"], j))
    do = mm(dy, win(wts["mla_w_out"], j), "nt", "mm_nt")
    dq, dk, dv = attn_bwd(qr, krot, kv, o, lse, do, "attn_bwd")
    dqslots, dkv, dkr, d_qn, d_kn = qk_bwd(qslots, kv, kr, qn, kn, cos_t, sin_t, dq, dk, dv, "qk_bwd")
    grads["mla_w_q_up"] = mm(cqn, dqslots, "tn", "mm_tn_into", into=win(grads["mla_w_q_up"], j))
    dcqn = mm(dqslots, win(wts["mla_w_q_up"], j), "nt", "mm_nt")
    grads["mla_w_kv_up"] = mm(ckvn, dkv, "tn", "mm_tn_into", into=win(grads["mla_w_kv_up"], j))
    dckvn = mm(dkv, win(wts["mla_w_kv_up"], j), "nt", "mm_nt")
    dcq, d_qa = rms_bwd(cq, qa_gain, dcqn, None, "rms_bwd")
    dckv, d_kva = rms_bwd(ckv, kva_gain, dckvn, None, "rms_bwd")
    dh = None
    for d, (c0, n) in zip((dcq, dckv, dkr), _MLA_IN_WINDOWS):
        grads["mla_w_in"] = mm(h, d, "tn", "mm_tn_into", into=win(grads["mla_w_in"], j, col_off=c0, cols=n))
        dh = mm(d, win(wts["mla_w_in"], j, col_off=c0, cols=n), "nt", "mm_nt" if dh is None else "mm_nt_add", add=dh)
    dx, d_gain = rms_bwd(x, gain, dh, dy, "rms_bwd")
    return dx, dict(gain=d_gain, qa=d_qa, kva=d_kva, qn=d_qn, kn=d_kn)


BIG = (("hgrn_w_in", "rows"), ("hgrn_w_out", "rows"), ("mla_w_in", "rows"), ("mla_w_q_up", "cols"),
       ("mla_w_kv_up", "cols"), ("mla_w_out", "rows"), ("ffn_w_up", "cols"), ("ffn_w_down", "rows"))
SMALL_SHARDED = (("mla_q_a_norm", 1), ("mla_kv_a_norm", 1), ("ffn_conv_w", 2))
REPLICATED = ("norm_mix", "norm_ffn", "hgrn_lower_bounds", "hgrn_out_norm", "mla_q_norm", "mla_k_norm", "ffn_conv_b")
WEIGHTS = ("norm_mix", "norm_ffn", "hgrn_w_in", "hgrn_lower_bounds", "hgrn_out_norm", "hgrn_w_out", "mla_w_in",
           "mla_q_a_norm", "mla_w_q_up", "mla_kv_a_norm", "mla_w_kv_up", "mla_q_norm", "mla_k_norm", "mla_w_out",
           "ffn_w_up", "ffn_conv_w", "ffn_conv_b", "ffn_w_down")


def _pad_cols(a, width):
    return jnp.pad(a, [(0, 0)] * (a.ndim - 1) + [(0, width - a.shape[-1])])


def _head_slots(w):
    lead, n = w.shape[:-1], w.shape[-1] // MLA_QK
    return _pad_cols(w.reshape(lead + (n, MLA_QK)), MLA_SLOT).reshape(lead + (n * MLA_SLOT,))


def _head_unslots(w):
    lead, n = w.shape[:-1], w.shape[-1] // MLA_SLOT
    return w.reshape(lead + (n, MLA_SLOT))[..., :MLA_QK].reshape(lead + (n * MLA_QK,))


def _to_stack_layout(name, a):
    if name == "hgrn_w_in":
        return a
    if name == "mla_w_in":
        return _pad_cols(a, MLA_IN_COLS)
    if name == "mla_w_q_up":
        return _head_slots(a)
    return a


def _from_stack_layout(name, a):
    if name == "mla_w_in":
        return a[..., :2 * MLA_LORA + MLA_ROPE]
    if name == "mla_w_q_up":
        return _head_unslots(a)
    return a


def _rope_tables(positions):
    inv_freq = ROPE_THETA ** (-jnp.arange(0, MLA_ROPE, 2, dtype=F32) / MLA_ROPE)
    ang = positions.astype(F32)[:, None] * inv_freq
    cos, sin = jnp.cos(ang), jnp.sin(ang)
    S = positions.shape[0]
    ones, zeros = jnp.ones((S, MLA_NOPE), F32), jnp.zeros((S, MLA_SLOT - MLA_QK), F32)
    return (jnp.concatenate([ones, cos, cos, zeros], axis=1),
            jnp.concatenate([jnp.zeros((S, MLA_NOPE), F32), -sin, sin, zeros], axis=1))


def kernel(x, positions, norm_mix, norm_ffn, hgrn_w_in, hgrn_lower_bounds, hgrn_out_norm, hgrn_w_out, mla_w_in, mla_q_a_norm, mla_w_q_up, mla_kv_a_norm, mla_w_kv_up, mla_q_norm, mla_k_norm, mla_w_out, ffn_w_up, ffn_conv_w, ffn_conv_b, ffn_w_down, loss_target, m_norm_mix, m_norm_ffn, m_hgrn_w_in, m_hgrn_lower_bounds, m_hgrn_out_norm, m_hgrn_w_out, m_mla_w_in, m_mla_q_a_norm, m_mla_w_q_up, m_mla_kv_a_norm, m_mla_w_kv_up, m_mla_q_norm, m_mla_k_norm, m_mla_w_out, m_ffn_w_up, m_ffn_conv_w, m_ffn_conv_b, m_ffn_w_down, v_norm_mix, v_norm_ffn, v_hgrn_w_in, v_hgrn_lower_bounds, v_hgrn_out_norm, v_hgrn_w_out, v_mla_w_in, v_mla_q_a_norm, v_mla_w_q_up, v_mla_kv_a_norm, v_mla_w_kv_up, v_mla_q_norm, v_mla_k_norm, v_mla_w_out, v_ffn_w_up, v_ffn_conv_w, v_ffn_conv_b, v_ffn_w_down):
    args = dict(locals())
    w = {n: args[n] for n in WEIGHTS}
    m = {n: args["m_" + n] for n in WEIGHTS}
    v = {n: args["v_" + n] for n in WEIGHTS}
    depth = norm_mix.shape[0]
    x0 = x[0]
    S = x0.shape[0]
    chip = (2 * lax.axis_index("x") + lax.axis_index("y")).astype(jnp.int32).reshape(1)
    core = lax.axis_index("c").astype(jnp.int32).reshape(1)
    big_names = [n for n, _ in BIG]
    kinds = [k for _, k in BIG]
    small_names = [n for n, _ in SMALL_SHARDED]
    small_axis = dict(SMALL_SHARDED)

    local = {n: _to_stack_layout(n, w[n]) for n in big_names}
    gathered = gather_big([local[n].astype(_MXU) for n in big_names], kinds, "gather_weights")
    wts = {n: place(g, local[n], k, chip, "place") for n, k, g in zip(big_names, kinds, gathered)}
    (got_small,) = exchange([_pack([w[n] for n in small_names], F32, SUBLANES)], "chips", [None], "gather_small")
    per_chip = [_unpack(got_small[p], [w[n].shape for n in small_names]) for p in range(N_CHIPS)]
    small = {n: jnp.concatenate([per_chip[p][k] for p in range(N_CHIPS)], axis=small_axis[n]) for k, n in enumerate(small_names)}

    cos_t, sin_t = _rope_tables(positions[0])
    lbs = lower_bound_fwd(hgrn_lower_bounds, "lower_bound_fwd")
    mall = jnp.asarray(_hgrn_sum_matrix(min(HGRN_CHUNK, S)), _MXU)
    qn = _pad_cols(mla_q_norm, MLA_SLOT)
    kn = _pad_cols(mla_k_norm, MLA_SLOT)
    cw8 = jnp.pad(small["ffn_conv_w"], ((0, 0), (0, SUBLANES - 3), (0, 0)))

    def mixer_args(layer):
        j = layer // 2
        if layer % 2 == 0:
            return (j, norm_mix[layer:layer + 1], wts, lbs[j:j + 1], hgrn_out_norm[j:j + 1], mall)
        return (j, norm_mix[layer:layer + 1], wts, small["mla_q_a_norm"][j:j + 1], small["mla_kv_a_norm"][j:j + 1],
                qn[j:j + 1], kn[j:j + 1], cos_t, sin_t)

    def ffn_args(layer):
        return (layer, norm_ffn[layer:layer + 1], wts, cw8[layer], ffn_conv_b[layer:layer + 1])

    xc, h = x0, rms_fwd(x0, norm_mix[0:1], "rms_fwd")
    saved = []
    for layer in range(depth):
        fwd = _hgrn_layer_fwd if layer % 2 == 0 else _mla_layer_fwd
        xc, h, s_mix = fwd(xc, h, norm_ffn[layer:layer + 1], *mixer_args(layer))
        xc, h, s_ffn = _ffn_fwd(xc, h, norm_mix[layer + 1:layer + 2] if layer + 1 < depth else None, *ffn_args(layer))
        saved.append((s_mix, s_ffn))

    dh, loss_blk = loss_head(xc, loss_target[0], "loss_head")
    loss = lax.psum(loss_blk[0, 0], MESH_AXES)

    grads = {n: lax.empty(g.shape, F32) for n, g in zip(big_names, gathered)}
    g_mix, g_ffn = [None] * depth, [None] * depth
    for layer in reversed(range(depth)):
        s_mix, s_ffn = saved[layer]
        dh, g_ffn[layer] = _ffn_bwd(dh, s_ffn, *ffn_args(layer), grads)
        bwd = _hgrn_layer_bwd if layer % 2 == 0 else _mla_layer_bwd
        dh, g_mix[layer] = bwd(dh, s_mix, *mixer_args(layer), grads)
    hg = [g_mix[l] for l in range(0, depth, 2)]
    mg = [g_mix[l] for l in range(1, depth, 2)]
    d_p = lower_bound_bwd(hgrn_lower_bounds, hg[1]["lb"], "lower_bound_bwd")

    g_list = [grads[n] for n in big_names]
    from_core = send_other_half(g_list, "reduce_cores_in")
    chip_sums = [add_cores(g, o, core, "add_cores") for g, o in zip(g_list, from_core)]
    from_chips = scatter_to_chips(chip_sums, kinds, "reduce_chips")
    reduced = [add_chips(own, got, k, chip, "add_chips") for own, got, k in zip(chip_sums, from_chips, kinds)]
    other_half = exchange(reduced, "sibling", [None] * len(reduced), "reduce_cores_out", keep_own=False)
    big_out = {}
    for n, mine, theirs in zip(big_names, reduced, other_half):
        outs = adamw_stacked(mine, theirs, local[n], _to_stack_layout(n, m[n]), _to_stack_layout(n, v[n]), core, "adamw")
        big_out[n] = [_from_stack_layout(n, o) for o in outs]

    small_grads = {
        "norm_mix": jnp.concatenate([g["gain"] for g in g_mix], axis=0),
        "norm_ffn": jnp.concatenate([g["gain"] for g in g_ffn], axis=0),
        "hgrn_lower_bounds": d_p[0:2],
        "hgrn_out_norm": jnp.concatenate([g["out_gain"] for g in hg], axis=0),
        "mla_q_a_norm": jnp.concatenate([g["qa"] for g in mg], axis=0),
        "mla_kv_a_norm": jnp.concatenate([g["kva"] for g in mg], axis=0),
        "mla_q_norm": jnp.concatenate([g["qn"][:, :MLA_QK] for g in mg], axis=0),
        "mla_k_norm": jnp.concatenate([g["kn"][:, :MLA_QK] for g in mg], axis=0),
        "ffn_conv_w": jnp.stack([g["conv_w"] for g in g_ffn]),
        "ffn_conv_b": jnp.concatenate([g["conv_b"] for g in g_ffn], axis=0),
    }

    def chip_part(n, p):
        size = w[n].shape[small_axis[n]]
        return lax.slice_in_dim(small_grads[n], p * size, (p + 1) * size, axis=small_axis[n])

    to_chips = jnp.stack([_pack([chip_part(n, p) for n in small_names], F32, SUBLANES) for p in range(N_CHIPS)])
    rep_g, shard_g = exchange([_pack([small_grads[n] for n in REPLICATED], F32, SUBLANES), to_chips], "all",
                              [None, (2, 1, 0)], "reduce_small")
    small_out = {}
    for names, gparts in ((REPLICATED, rep_g), (small_names, shard_g)):
        packed = adamw_packed(gparts, *[_pack([t[n] for n in names], F32, SUBLANES) for t in (w, m, v)], "adamw_small")
        unpacked = [_unpack(buf, [w[n].shape for n in names]) for buf in packed]
        for k, n in enumerate(names):
            small_out[n] = [u[k] for u in unpacked]

    result = [loss, dh[None]]
    for k in range(4):
        result += [(big_out[n] if n in big_out else small_out[n])[k] for n in WEIGHTS]
    return tuple(result)
```

```python
import numpy as np
import jax
import jax.numpy as jnp
from jax import lax
from jax.experimental import pallas as pl
from jax.experimental.pallas import tpu as pltpu

F32 = jnp.float32
BF16 = jnp.bfloat16
_MXU = BF16

RMS_EPS = 1e-6
D_MODEL = 1024
HEADS = 8
HEAD_DIM = 128
HGRN_CHUNK = 128
MLA_NOPE = 128
MLA_ROPE = 64
MLA_QK = MLA_NOPE + MLA_ROPE
MLA_SLOT = 256
MLA_LORA = 256
MLA_IN_COLS = 2 * MLA_LORA + HEAD_DIM
ROPE_THETA = 10000.0
D_FF = 2816
FF_BLOCK = 1408
LANES = 128
SUBLANES = 8

ADAM_LR = 0.001
ADAM_B1 = 0.9
ADAM_B2 = 0.999
ADAM_EPS = 1e-08
ADAM_WD = 0.01
ADAM_STEP = 10

VMEM_LIMIT = 56 * 1024 * 1024
MESH_AXES = ("x", "y", "c")
N_CHIPS = 4

_NN = ((1,), (0,))
_NT = ((1,), (1,))
_TN = ((0,), (0,))


def _dg(a, b, dims):
    return lax.dot_general(a.astype(_MXU), b.astype(_MXU), (dims, ((), ())), preferred_element_type=F32)


@jax.custom_vjp
def kdot(a, b):
    return _dg(a, b, _NN)


kdot.defvjp(lambda a, b: (_dg(a, b, _NN), (a, b)), lambda r, g: (_dg(g, r[1], _NT), _dg(r[0], g, _TN)))


@jax.custom_vjp
def kdot_nt(a, b):
    return _dg(a, b, _NT)


kdot_nt.defvjp(lambda a, b: (_dg(a, b, _NT), (a, b)), lambda r, g: (_dg(g, r[1], _NN), _dg(g, r[0], _TN)))


@jax.custom_vjp
def kdot_tn(a, b):
    return _dg(a, b, _TN)


kdot_tn.defvjp(lambda a, b: (_dg(a, b, _TN), (a, b)), lambda r, g: (_dg(r[1], g, _NT), _dg(r[0], g, _NN)))


def _pick(d, prefs):
    for p in prefs:
        if d >= p and d % p == 0:
            return p
    return d


def _params(sem):
    return pltpu.CompilerParams(dimension_semantics=sem, vmem_limit_bytes=VMEM_LIMIT)


def _sds(shape, dtype=F32):
    return jax.ShapeDtypeStruct(shape, dtype)


def win(arr, layer, row_off=0, col_off=0, rows=None, cols=None):
    return (arr, layer, row_off, col_off, rows or arr.shape[1] - row_off, cols or arr.shape[2] - col_off)


def mm(a, b, mode, name, add=None, out_dtype=F32, into=None, rms_gain=None, norm_bwd=None):
    if isinstance(b, tuple):
        b_arr, b_layer, b_r0, b_c0, b_rows, b_cols = b
    else:
        b_arr, b_layer, b_r0, b_c0, (b_rows, b_cols) = b, None, 0, 0, b.shape
    if mode == "nn":
        (M, K), (K2, N) = a.shape, (b_rows, b_cols)
    elif mode == "nt":
        (M, K), (N, K2) = a.shape, (b_rows, b_cols)
    else:
        (K, M), (K2, N) = a.shape, (b_rows, b_cols)
    assert K == K2, (name, a.shape, b_rows, b_cols)
    tm = M if M <= 1024 else _pick(M, (1024, 1408, 512, 256, 128))
    if norm_bwd is not None:
        tm = min(tm, 512)
    tn = N if N <= 1024 else _pick(N, (1024, 1408, 512, 256, 128))
    tk = K if K <= 2048 else _pick(K, (2048, 2816, 1024, 512, 256, 128))
    nk = K // tk
    dims = {"nn": _NN, "nt": _NT, "tn": _TN}[mode]
    a_spec = pl.BlockSpec((tk, tm), lambda i, j, k: (k, i)) if mode == "tn" else pl.BlockSpec((tm, tk), lambda i, j, k: (i, k))
    b_blk = (tn, tk) if mode == "nt" else (tk, tn)
    assert b_r0 % b_blk[0] == 0 and b_c0 % b_blk[1] == 0, (name, b_r0, b_c0, b_blk)
    br, bc = b_r0 // b_blk[0], b_c0 // b_blk[1]
    if mode == "nt":
        b_idx = lambda i, j, k: (br + j, bc + k)
    else:
        b_idx = lambda i, j, k: (br + k, bc + j)
    if b_layer is None:
        b_spec = pl.BlockSpec(b_blk, b_idx)
    else:
        b_spec = pl.BlockSpec((None,) + b_blk, lambda i, j, k: (b_layer,) + b_idx(i, j, k))
    plain = pl.BlockSpec((tm, tn), lambda i, j, k: (i, j))
    has_add, has_rms, has_nb = add is not None, rms_gain is not None, norm_bwd is not None
    assert not (has_rms or has_nb) or (tn == N and into is None and not (has_rms and has_nb)), name
    vec = pl.BlockSpec((1, tn), lambda i, j, k: (0, j))
    ins = [a, b_arr] + ([add] if has_add else []) + ([rms_gain] if has_rms else [])
    specs = [a_spec, b_spec] + ([plain] if has_add else []) + ([vec] if has_rms else [])
    n_in = len(ins)
    if has_nb:
        nb_x, nb_gain, nb_res = norm_bwd
        ins += [nb_x, nb_gain] + ([nb_res] if nb_res is not None else [])
        specs += [plain, vec] + ([plain] if nb_res is not None else [])
    aliases = {}
    if into is None:
        o_spec, out_shape = plain, _sds((M, N), out_dtype)
    else:
        buf, o_layer, o_r0, o_c0, o_rows, o_cols = into
        assert (o_rows, o_cols) == (M, N) and o_r0 % tm == 0 and o_c0 % tn == 0, (name, into[1:], M, N, tm, tn)
        orow, ocol = o_r0 // tm, o_c0 // tn
        o_spec = pl.BlockSpec((None, tm, tn), lambda i, j, k: (o_layer, orow + i, ocol + j))
        out_shape = _sds(buf.shape, buf.dtype)
        aliases = {len(ins): 0}
        ins.append(buf)
        specs.append(pl.BlockSpec(memory_space=pl.ANY))

    n_all_in = len(ins)

    def kern(*refs):
        a_ref, b_ref = refs[0], refs[1]
        add_ref = refs[2] if has_add else None
        o_ref = refs[n_all_in]

        def finish(r):
            if has_add:
                r = r + add_ref[...].astype(F32)
            if has_nb:
                _, vjp = jax.vjp(lambda xv, gv: _rms(xv, gv, N), refs[n_in][...], refs[n_in + 1][...])
                dx, dgain = vjp(r)
                o_ref[...] = dx if nb_res is None else dx + refs[n_in + 2][...]
                _store(refs[n_all_in + 1], dgain, pl.program_id(0) == 0)
                return
            o_ref[...] = r.astype(o_ref.dtype)
            if has_rms:
                refs[n_all_in + 1][...] = _rms(r, refs[n_in - 1][...], N).astype(_MXU)

        if nk == 1:
            finish(_dg(a_ref[...], b_ref[...], dims))
            return
        acc = refs[-1]
        k = pl.program_id(2)

        @pl.when(k == 0)
        def _():
            acc[...] = jnp.zeros_like(acc)

        acc[...] += _dg(a_ref[...], b_ref[...], dims)

        @pl.when(k == nk - 1)
        def _():
            finish(acc[...])

    if has_rms:
        o_spec, out_shape = [o_spec, plain], [out_shape, _sds((M, N), _MXU)]
    if has_nb:
        o_spec, out_shape = [o_spec, vec], [out_shape, _sds((1, N))]
    return pl.pallas_call(
        kern, name=name, grid=(M // tm, N // tn, nk), in_specs=specs, out_specs=o_spec, out_shape=out_shape,
        scratch_shapes=[pltpu.VMEM((tm, tn), F32)] if nk > 1 else [], input_output_aliases=aliases,
        compiler_params=_params(("arbitrary" if has_nb else "parallel", "parallel", "arbitrary")))(*ins)


def _store(ref, val, first):
    if first is None:
        ref[...] = val.astype(ref.dtype)
        return

    @pl.when(first)
    def _():
        ref[...] = val.astype(ref.dtype)

    @pl.when(jnp.logical_not(first))
    def _():
        ref[...] += val.astype(ref.dtype)


def tilecall(body, name, grid, ins, outs, sem, prefetch=(), aliases=None):
    n_pre, n_in = len(prefetch), len(ins)

    def kern(*refs):
        vals = body(*refs[:n_pre + n_in])
        for ref, val, (_, _, first) in zip(refs[n_pre + n_in:], vals, outs):
            _store(ref, val, None if first is None else first())

    in_specs, out_specs = [s for _, s in ins], [s for _, s, _ in outs]
    kwargs = dict(name=name, out_shape=[sh for sh, _, _ in outs], compiler_params=_params(sem),
                  input_output_aliases={n_pre + k: v for k, v in (aliases or {}).items()})
    if n_pre:
        kwargs["grid_spec"] = pltpu.PrefetchScalarGridSpec(num_scalar_prefetch=n_pre, grid=grid, in_specs=in_specs,
                                                           out_specs=out_specs)
    else:
        kwargs.update(grid=grid, in_specs=in_specs, out_specs=out_specs)
    return pl.pallas_call(kern, **kwargs)(*prefetch, *[a for a, _ in ins])


def _rms(x, g, n):
    ms = jnp.sum(x * x, axis=-1, keepdims=True) / n
    return x * lax.rsqrt(ms + RMS_EPS) * g


def _row_tile(S, w):
    return min(S, 512 if w <= 1024 else 256)


def rms_fwd(x, g, name, col=0, w=None):
    S = x.shape[0]
    w = w or x.shape[1]
    ts = _row_tile(S, w)
    return tilecall(
        lambda x_ref, g_ref: (_rms(x_ref[...], g_ref[...], w),), name, (S // ts,),
        [(x, pl.BlockSpec((ts, w), lambda i: (i, col))), (g, pl.BlockSpec((1, w), lambda i: (0, 0)))],
        [(_sds((S, w), _MXU), pl.BlockSpec((ts, w), lambda i: (i, 0)), None)], ("parallel",))[0]


def _shifted(u, halo_ref, is_first):
    rid = lax.broadcasted_iota(jnp.int32, (SUBLANES, 1), 0)
    h7 = jnp.where(is_first, 0.0, halo_ref[7:8, :])
    h6 = jnp.where(is_first, 0.0, halo_ref[6:7, :])
    r1, r2 = pltpu.roll(u, 1, 0), pltpu.roll(u, 2, 0)
    top1 = jnp.where(rid == 0, h7, r1[:SUBLANES])
    top2 = jnp.where(rid == 0, h6, jnp.where(rid == 1, h7, r2[:SUBLANES]))
    return jnp.concatenate([top1, r1[SUBLANES:]], axis=0), jnp.concatenate([top2, r2[SUBLANES:]], axis=0)


def _conv(u, u1, u2, cw_ref, cb_ref):
    return ((cb_ref[...] + u2 * cw_ref[0:1, :]) + u1 * cw_ref[1:2, :]) + u * cw_ref[2:3, :]


def _ffn_specs(S, ts, jmap):
    hb = ts // SUBLANES
    return (pl.BlockSpec((ts, FF_BLOCK), lambda j, i: (i, jmap(j))),
            pl.BlockSpec((SUBLANES, FF_BLOCK), lambda j, i: (jnp.maximum(i * hb - 1, 0), jmap(j))),
            pl.BlockSpec((SUBLANES, FF_BLOCK), lambda j, i: (0, jmap(j))),
            pl.BlockSpec((1, FF_BLOCK), lambda j, i: (0, jmap(j))))


def ffn_act_fwd(u, cw8, cb, name):
    S = u.shape[0]
    ts = _row_tile(S, 2 * D_FF)
    nb = D_FF // FF_BLOCK

    def body(ug, hg, cwg, cbg, uu, hu, cwu, cbu):
        first = pl.program_id(1) == 0
        g = ug[...]
        g1, g2 = _shifted(g, hg, first)
        yg = _conv(g, g1, g2, cwg, cbg)
        v = uu[...]
        v1, v2 = _shifted(v, hu, first)
        yu = _conv(v, v1, v2, cwu, cbu)
        return (yg * jax.nn.sigmoid(yg) * yu,)

    sg = _ffn_specs(S, ts, lambda j: j)
    su = _ffn_specs(S, ts, lambda j: j + nb)
    ins = [(u, sg[0]), (u, sg[1]), (cw8, sg[2]), (cb, sg[3]), (u, su[0]), (u, su[1]), (cw8, su[2]), (cb, su[3])]
    return tilecall(body, name, (nb, S // ts), ins,
                    [(_sds((S, D_FF), _MXU), pl.BlockSpec((ts, FF_BLOCK), lambda j, i: (i, j)), None)],
                    ("parallel", "parallel"))[0]


def ffn_act_bwd(u, cw8, cb, da, name):
    S = u.shape[0]
    ts = _row_tile(S, 2 * D_FF)
    nb = D_FF // FF_BLOCK

    def taps(dy, x, x1, x2):
        return jnp.concatenate(
            [jnp.sum(dy * x2, axis=0, keepdims=True), jnp.sum(dy * x1, axis=0, keepdims=True),
             jnp.sum(dy * x, axis=0, keepdims=True), jnp.zeros((SUBLANES - 3, dy.shape[1]), F32)], axis=0)

    def body(ug, hg, cwg, cbg, uu, hu, cwu, cbu, da_ref):
        first = pl.program_id(1) == 0
        g = ug[...]
        g1, g2 = _shifted(g, hg, first)
        yg = _conv(g, g1, g2, cwg, cbg)
        v = uu[...]
        v1, v2 = _shifted(v, hu, first)
        yu = _conv(v, v1, v2, cwu, cbu)
        d = da_ref[...]
        sg = jax.nn.sigmoid(yg)
        dyg = d * yu * (sg * (1.0 + yg * (1.0 - sg)))
        dyu = d * (yg * sg)
        return (dyg, dyu, taps(dyg, g, g1, g2), taps(dyu, v, v1, v2),
                jnp.sum(dyg, axis=0, keepdims=True), jnp.sum(dyu, axis=0, keepdims=True))

    sg_ = _ffn_specs(S, ts, lambda j: j)
    su_ = _ffn_specs(S, ts, lambda j: j + nb)
    row = pl.BlockSpec((ts, FF_BLOCK), lambda j, i: (i, j))
    ins = [(u, sg_[0]), (u, sg_[1]), (cw8, sg_[2]), (cb, sg_[3]), (u, su_[0]), (u, su_[1]), (cw8, su_[2]), (cb, su_[3]), (da, row)]
    first_row = lambda: pl.program_id(1) == 0
    dy, dcw, dcb = (_sds((S, D_FF)), row, None), (_sds((SUBLANES, D_FF)), sg_[2], first_row), (_sds((1, D_FF)), sg_[3], first_row)
    return tilecall(body, name, (nb, S // ts), ins, [dy, dy, dcw, dcw, dcb, dcb], ("parallel", "arbitrary"))


def ffn_conv_bwd(dyg, dyu, cw8, name):
    S = dyg.shape[0]
    ts = _row_tile(S, 2 * D_FF)
    hb = ts // SUBLANES
    nrow = S // ts
    nb = D_FF // FF_BLOCK

    def back(dy_ref, halo_ref, cw_ref):
        last = pl.program_id(1) == nrow - 1
        d = dy_ref[...]
        rid = lax.broadcasted_iota(jnp.int32, (SUBLANES, 1), 0)
        n0 = jnp.where(last, 0.0, halo_ref[0:1, :])
        n1 = jnp.where(last, 0.0, halo_ref[1:2, :])
        r1, r2 = pltpu.roll(d, ts - 1, 0), pltpu.roll(d, ts - 2, 0)
        end1 = jnp.where(rid == SUBLANES - 1, n0, r1[ts - SUBLANES:])
        end2 = jnp.where(rid == SUBLANES - 1, n1, jnp.where(rid == SUBLANES - 2, n0, r2[ts - SUBLANES:]))
        d1 = jnp.concatenate([r1[:ts - SUBLANES], end1], axis=0)
        d2 = jnp.concatenate([r2[:ts - SUBLANES], end2], axis=0)
        return d * cw_ref[2:3, :] + d1 * cw_ref[1:2, :] + d2 * cw_ref[0:1, :]

    row = pl.BlockSpec((ts, FF_BLOCK), lambda j, i: (i, j))
    halo = pl.BlockSpec((SUBLANES, FF_BLOCK), lambda j, i: (jnp.minimum((i + 1) * hb, S // SUBLANES - 1), j))
    ins = [(dyg, row), (dyg, halo), (cw8, pl.BlockSpec((SUBLANES, FF_BLOCK), lambda j, i: (0, j))),
           (dyu, row), (dyu, halo), (cw8, pl.BlockSpec((SUBLANES, FF_BLOCK), lambda j, i: (0, j + nb)))]
    out = (_sds((S, D_FF), _MXU), row, None)
    return tilecall(lambda a, b, c, d, e, f: (back(a, b, c), back(d, e, f)), name, (nb, nrow), ins, [out, out],
                    ("parallel", "parallel"))


def _hgrn_levels(C):
    out, m = [], C // 2
    while m >= 1:
        out.append(m)
        m //= 2
    return out


def _hgrn_sum_matrix(C):
    t = np.arange(C)[:, None]
    u = np.arange(C)[None, :]
    blocks = [u <= t, u > t]
    for m in _hgrn_levels(C):
        r = (t // (2 * m)) * (2 * m) + m
        right = (t % (2 * m)) >= m
        blocks.append((right & (u > r) & (u <= t)) | ((~right) & (u > t) & (u <= r)))
    return np.concatenate(blocks, axis=0).astype(np.float32)


def _make_partial_sums(nb, C):
    @jax.custom_vjp
    def sums(mall, lf):
        hi = lf.astype(_MXU)
        mid = (lf - hi.astype(F32)).astype(_MXU)
        e2 = _dg(mall, jnp.concatenate([hi, mid], axis=1), _NN)
        e = e2[:, :HEAD_DIM] + e2[:, HEAD_DIM:]
        return tuple(e[b * C:(b + 1) * C] for b in range(nb))

    def fwd(mall, lf):
        return sums(mall, lf), mall

    def bwd(mall, gs):
        return jnp.zeros_like(mall), _dg(mall, jnp.concatenate(gs, axis=0), _TN)

    sums.defvjp(fwd, bwd)
    return sums


def _hgrn_chunk(zq, zf, v, lb, st, mall, C):
    levels = _hgrn_levels(C)
    qs = zq * jax.nn.sigmoid(zq)
    fg = lb + (1.0 - lb) * jax.nn.sigmoid(zf)
    k = 1.0 - fg
    e = _make_partial_sums(2 + len(levels), C)(mall, jnp.log(fg))
    g_incl, g_after = e[0], e[1]
    rid = lax.broadcasted_iota(jnp.int32, (C, 1), 0)
    tt = lax.broadcasted_iota(jnp.int32, (C, C), 0)
    ss = lax.broadcasted_iota(jnp.int32, (C, C), 1)
    o = kdot_nt(qs * jnp.exp(g_incl), st)
    o = o + jnp.sum(qs * k, axis=-1, keepdims=True) * v
    scores = jnp.zeros((C, C), F32)
    for li, m in enumerate(levels):
        sh = int(np.log2(m))
        right = ((rid >> sh) & 1) == 1
        both = jnp.where(right, qs, k) * jnp.exp(e[2 + li])
        pair = ((tt >> (sh + 1)) == (ss >> (sh + 1))) & (((tt >> sh) & 1) == 1) & (((ss >> sh) & 1) == 0)
        scores = scores + jnp.where(pair, kdot_nt(both, both), 0.0)
    o = o + kdot(scores, v)
    g_last = jnp.sum(jnp.where(rid == C - 1, g_incl, 0.0), axis=0, keepdims=True)
    st_new = st * jnp.exp(g_last) + kdot_tn(v, k * jnp.exp(g_after))
    return o, st_new


HGRN_HEADS_PER_STEP = 4
_HGRN_LANES = HGRN_HEADS_PER_STEP * HEAD_DIM


def _hgrn_in_specs(C, nc, rev):
    cm = (lambda c: nc - 1 - c) if rev else (lambda c: c)
    blk = lambda: pl.BlockSpec((C, _HGRN_LANES), lambda h, c: (cm(c), h))
    return cm, [blk(), blk(), blk(), pl.BlockSpec((1, _HGRN_LANES), lambda h, c: (0, h))]


def _hgrn_state_spec(cm):
    return pl.BlockSpec((HGRN_HEADS_PER_STEP, None, HEAD_DIM, HEAD_DIM), lambda h, c: (h, cm(c), 0, 0))


def _hgrn_out(o, g, gain):
    return _rms(o, gain, HEAD_DIM) * (g * jax.nn.sigmoid(g))


def hgrn_fwd(zq, zf, zi, zg, lb, out_gain, mall, name):
    S = zq.shape[0]
    C = min(HGRN_CHUNK, S)
    nc = S // C

    def kern(zq_ref, zf_ref, zi_ref, lb_ref, zg_ref, gain_ref, mall_ref, o_ref, on_ref, st_ref, st):
        @pl.when(pl.program_id(1) == 0)
        def _():
            st[...] = jnp.zeros_like(st)

        mall_v = mall_ref[...]
        for g in range(HGRN_HEADS_PER_STEP):
            lanes = slice(g * HEAD_DIM, (g + 1) * HEAD_DIM)
            s_in = st[g]
            st_ref[g] = s_in
            o, s_new = _hgrn_chunk(zq_ref[:, lanes], zf_ref[:, lanes], zi_ref[:, lanes], lb_ref[:, lanes], s_in, mall_v, C)
            o_ref[:, lanes] = o
            on_ref[:, lanes] = _hgrn_out(o, zg_ref[:, lanes], gain_ref[...]).astype(on_ref.dtype)
            st[g] = s_new

    cm, specs = _hgrn_in_specs(C, nc, False)
    row = pl.BlockSpec((C, _HGRN_LANES), lambda h, c: (c, h))
    return pl.pallas_call(
        kern, name=name, grid=(HEADS // HGRN_HEADS_PER_STEP, nc),
        in_specs=specs + [row, pl.BlockSpec((1, HEAD_DIM), lambda h, c: (0, 0)), pl.BlockSpec(mall.shape, lambda h, c: (0, 0))],
        out_specs=[row, row, _hgrn_state_spec(cm)],
        out_shape=[_sds((S, D_MODEL)), _sds((S, D_MODEL), _MXU), _sds((HEADS, nc, HEAD_DIM, HEAD_DIM))],
        scratch_shapes=[pltpu.VMEM((HGRN_HEADS_PER_STEP, HEAD_DIM, HEAD_DIM), F32)],
        compiler_params=_params(("parallel", "arbitrary")))(zq, zf, zi, lb, zg, out_gain, mall)


def hgrn_bwd(zq, zf, zi, zg, lb, out_gain, mall, states, o, don, name):
    S = zq.shape[0]
    C = min(HGRN_CHUNK, S)
    nc = S // C

    def kern(zq_ref, zf_ref, zi_ref, lb_ref, zg_ref, gain_ref, mall_ref, st_ref, o_ref, don_ref,
             dq_ref, df_ref, di_ref, dg_ref, dlb_ref, dgain_ref, dst):
        first = pl.program_id(1) == 0

        @pl.when(first)
        def _():
            dst[...] = jnp.zeros_like(dst)

        mall_v = mall_ref[...]
        gls, dgain = [], None
        for g in range(HGRN_HEADS_PER_STEP):
            lanes = slice(g * HEAD_DIM, (g + 1) * HEAD_DIM)
            _, out_vjp = jax.vjp(_hgrn_out, o_ref[:, lanes], zg_ref[:, lanes], gain_ref[...])
            do, dzg, dgn = out_vjp(don_ref[:, lanes])
            dg_ref[:, lanes] = dzg.astype(dg_ref.dtype)
            dgain = dgn if dgain is None else dgain + dgn
            _, vjp = jax.vjp(lambda a, b, c, d, e: _hgrn_chunk(a, b, c, d, e, mall_v, C),
                             zq_ref[:, lanes], zf_ref[:, lanes], zi_ref[:, lanes], lb_ref[:, lanes], st_ref[g])
            ga, gb, gv, gl, gs = vjp((do, dst[g]))
            dq_ref[:, lanes] = ga.astype(dq_ref.dtype)
            df_ref[:, lanes] = gb.astype(df_ref.dtype)
            di_ref[:, lanes] = gv.astype(di_ref.dtype)
            gls.append(gl)
            dst[g] = gs
        _store(dlb_ref, jnp.concatenate(gls, axis=1), first)
        _store(dgain_ref, dgain, first & (pl.program_id(0) == 0))

    cm, specs = _hgrn_in_specs(C, nc, True)
    row = lambda: pl.BlockSpec((C, _HGRN_LANES), lambda h, c: (cm(c), h))
    vec = pl.BlockSpec((1, HEAD_DIM), lambda h, c: (0, 0))
    wide = _sds((S, D_MODEL), _MXU)
    return pl.pallas_call(
        kern, name=name, grid=(HEADS // HGRN_HEADS_PER_STEP, nc),
        in_specs=specs + [row(), vec, pl.BlockSpec(mall.shape, lambda h, c: (0, 0)), _hgrn_state_spec(cm), row(), row()],
        out_specs=[row(), row(), row(), row(), pl.BlockSpec((1, _HGRN_LANES), lambda h, c: (0, h)), vec],
        out_shape=[wide, wide, wide, wide, _sds((1, D_MODEL)), _sds((1, HEAD_DIM))],
        scratch_shapes=[pltpu.VMEM((HGRN_HEADS_PER_STEP, HEAD_DIM, HEAD_DIM), F32)],
        compiler_params=_params(("arbitrary", "arbitrary")))(zq, zf, zi, lb, zg, out_gain, mall, states, o, don)


def _lb_soft(p0, p1):
    mx = jnp.maximum(p0, p1)
    e0, e1 = jnp.exp(p0 - mx), jnp.exp(p1 - mx)
    s0, s1 = e0 / (e0 + e1), e1 / (e0 + e1)
    return (s0 + s1) - s0


def lower_bound_fwd(p, name):
    assert p.shape[0] == 2

    def body(p_ref):
        s = _lb_soft(p_ref[0:1, :], p_ref[1:2, :])
        return (jnp.concatenate([jnp.zeros_like(s), s] + [jnp.zeros_like(s)] * (SUBLANES - 2), axis=0),)

    spec8 = pl.BlockSpec((SUBLANES, p.shape[1]), lambda: (0, 0))
    return tilecall(body, name, (), [(p, pl.BlockSpec(p.shape, lambda: (0, 0)))], [(_sds((SUBLANES, p.shape[1])), spec8, None)], ())[0]


def lower_bound_bwd(p, dlb1, name):
    def body(p_ref, d_ref):
        _, vjp = jax.vjp(_lb_soft, p_ref[0:1, :], p_ref[1:2, :])
        g0, g1 = vjp(d_ref[...])
        return (jnp.concatenate([g0, g1] + [jnp.zeros_like(g0)] * (SUBLANES - 2), axis=0),)

    spec8 = pl.BlockSpec((SUBLANES, p.shape[1]), lambda: (0, 0))
    return tilecall(body, name, (), [(p, pl.BlockSpec(p.shape, lambda: (0, 0))), (dlb1, pl.BlockSpec(dlb1.shape, lambda: (0, 0)))],
                    [(_sds((SUBLANES, p.shape[1])), spec8, None)], ())[0]


@jax.custom_vjp
def _swap_rope_halves(x):
    lane = lax.broadcasted_iota(jnp.int32, x.shape, 1)
    lo = (lane >= MLA_NOPE) & (lane < MLA_NOPE + MLA_ROPE // 2)
    hi = (lane >= MLA_NOPE + MLA_ROPE // 2) & (lane < MLA_QK)
    return jnp.where(lo, pltpu.roll(x, MLA_SLOT - MLA_ROPE // 2, 1), jnp.where(hi, pltpu.roll(x, MLA_ROPE // 2, 1), 0.0))


_swap_rope_halves.defvjp(lambda x: (_swap_rope_halves(x), None), lambda _, g: (_swap_rope_halves(g),))


def _norm_rope(x, gain, cos_t, sin_t):
    y = _rms(x, gain, MLA_QK)
    return y * cos_t + _swap_rope_halves(y) * sin_t


_ATTN_SCALE = MLA_QK ** -0.5


def _qk_heads(qs, kn, kr, qn, kn_gain, cos_t, sin_t):
    q = _norm_rope(qs, qn, cos_t, sin_t) * _ATTN_SCALE
    k = _norm_rope(jnp.concatenate([kn, kr], axis=1), kn_gain, cos_t, sin_t)
    return q, k


def _qk_specs(ts):
    slot = pl.BlockSpec((ts, MLA_SLOT), lambda i, h: (i, h))
    nope = pl.BlockSpec((ts, HEAD_DIM), lambda i, h: (i, 2 * h))
    shared = pl.BlockSpec((ts, HEAD_DIM), lambda i, h: (i, 0))
    gain = pl.BlockSpec((1, MLA_SLOT), lambda i, h: (0, 0))
    table = pl.BlockSpec((ts, MLA_SLOT), lambda i, h: (i, 0))
    return slot, nope, shared, gain, table


def qk_fwd(qslots, kv, krope, qn, kn, cos_t, sin_t, name):
    S = qslots.shape[0]
    ts = _row_tile(S, D_MODEL)
    slot, nope, shared, gain, table = _qk_specs(ts)

    def body(q_ref, kn_ref, kr_ref, qn_ref, kg_ref, c_ref, s_ref):
        return _qk_heads(q_ref[...], kn_ref[...], kr_ref[...], qn_ref[...], kg_ref[...], c_ref[...], s_ref[...])

    out = _sds((S, HEADS * MLA_SLOT), _MXU)
    return tilecall(body, name, (S // ts, HEADS),
                    [(qslots, slot), (kv, nope), (krope, shared), (qn, gain), (kn, gain), (cos_t, table), (sin_t, table)],
                    [(out, slot, None), (out, slot, None)], ("parallel", "parallel"))


def qk_bwd(qslots, kv, krope, qn, kn, cos_t, sin_t, dq, dk, dv, name):
    S = qslots.shape[0]
    ts = _row_tile(S, D_MODEL)
    slot, nope, shared, gain, table = _qk_specs(ts)
    vblk = pl.BlockSpec((ts, HEAD_DIM), lambda i, h: (i, h))

    def body(q_ref, kn_ref, kr_ref, qn_ref, kg_ref, c_ref, s_ref, dq_ref, dk_ref, dv_ref):
        c, s = c_ref[...], s_ref[...]
        _, vjp = jax.vjp(lambda a, b, r, g1, g2: _qk_heads(a, b, r, g1, g2, c, s),
                         q_ref[...], kn_ref[...], kr_ref[...], qn_ref[...], kg_ref[...])
        ga, gb, gr, g1, g2 = vjp((dq_ref[...], dk_ref[...]))
        return ga, jnp.concatenate([gb, dv_ref[...]], axis=1), gr, g1, g2

    first_head = lambda: pl.program_id(1) == 0
    first = lambda: (pl.program_id(0) == 0) & (pl.program_id(1) == 0)
    wide = _sds((S, HEADS * MLA_SLOT), _MXU)
    return tilecall(body, name, (S // ts, HEADS),
                    [(qslots, slot), (kv, nope), (krope, shared), (qn, gain), (kn, gain), (cos_t, table), (sin_t, table),
                     (dq, slot), (dk, slot), (dv, vblk)],
                    [(wide, slot, None), (wide, slot, None), (_sds((S, HEAD_DIM)), shared, first_head),
                     (_sds((1, MLA_SLOT)), gain, first), (_sds((1, MLA_SLOT)), gain, first)], ("arbitrary", "arbitrary"))


ATTN_TILE_FWD = 1024
ATTN_TILE = 1024
ATTN_HEADS_PER_STEP = 2
ATTN_HEADS_PER_STEP_BWD = 1


def _causal_pairs(nq, by_row):
    pairs = [(i, j) for i in range(nq) for j in range(i + 1)] if by_row else [(i, j) for j in range(nq) for i in range(j, nq)]
    return jnp.asarray([p[0] for p in pairs], jnp.int32), jnp.asarray([p[1] for p in pairs], jnp.int32)


def _diag_mask(s, tq):
    rows = lax.broadcasted_iota(jnp.int32, (tq, tq), 0)
    cols = lax.broadcasted_iota(jnp.int32, (tq, tq), 1)
    return jnp.where(rows >= cols, s, -jnp.inf)


def attn_fwd(qr, kr, kv, name):
    S = qr.shape[0]
    tq = min(S, ATTN_TILE_FWD)
    nq = S // tq
    i_tab, j_tab = _causal_pairs(nq, True)

    G = ATTN_HEADS_PER_STEP
    reps = tq // LANES

    def kern(it, jt, q_ref, k_ref, kv_ref, o_ref, lse_ref, m_s, l_s, acc):
        n = pl.program_id(1)
        i, j = it[n], jt[n]

        @pl.when(j == 0)
        def _():
            m_s[...] = jnp.full_like(m_s, -jnp.inf)
            l_s[...] = jnp.zeros_like(l_s)
            acc[...] = jnp.zeros_like(acc)

        def step(diagonal):
            for g in range(G):
                slot = slice(g * MLA_SLOT, (g + 1) * MLA_SLOT)
                s = _dg(q_ref[:, slot], k_ref[:, slot], _NT)
                if diagonal:
                    s = _diag_mask(s, tq)
                m_prev = m_s[g]
                m_new = jnp.maximum(m_prev, jnp.max(s, axis=-1, keepdims=True))
                alpha = jnp.exp(m_prev - m_new)
                p = jnp.exp(s - jnp.tile(m_new, (1, reps)))
                l_s[g] = alpha * l_s[g] + jnp.sum(p, axis=-1, keepdims=True)
                acc[g] = alpha * acc[g] + _dg(p, kv_ref[:, g * MLA_SLOT + HEAD_DIM:(g + 1) * MLA_SLOT], _NN)
                m_s[g] = m_new

        @pl.when(j < i)
        def _():
            step(False)

        @pl.when(j == i)
        def _():
            step(True)
            for g in range(G):
                l = l_s[g]
                lanes = slice(g * HEAD_DIM, (g + 1) * HEAD_DIM)
                o_ref[:, lanes] = acc[g] / l
                lse_ref[:, lanes] = m_s[g] + jnp.log(l)

    out = _sds((S, HEADS * HEAD_DIM))
    oblk = pl.BlockSpec((tq, G * HEAD_DIM), lambda h, n, it, jt: (it[n], h))
    stat = pltpu.VMEM((G, tq, HEAD_DIM), F32)
    return pl.pallas_call(
        kern, name=name,
        grid_spec=pltpu.PrefetchScalarGridSpec(
            num_scalar_prefetch=2, grid=(HEADS // G, i_tab.shape[0]),
            in_specs=[pl.BlockSpec((tq, G * MLA_SLOT), lambda h, n, it, jt: (it[n], h)),
                      pl.BlockSpec((tq, G * MLA_SLOT), lambda h, n, it, jt: (jt[n], h)),
                      pl.BlockSpec((tq, G * MLA_SLOT), lambda h, n, it, jt: (jt[n], h))],
            out_specs=[oblk, oblk], scratch_shapes=[stat, stat, stat]),
        out_shape=[out, out], compiler_params=_params(("parallel", "arbitrary")))(i_tab, j_tab, qr, kr, kv)


def attn_bwd(qr, kr, kv, o, lse, do, name):
    S = qr.shape[0]
    tq = min(S, ATTN_TILE)
    nq = S // tq
    i_tab, j_tab = _causal_pairs(nq, False)

    G = ATTN_HEADS_PER_STEP_BWD

    def kern(it, jt, q_ref, k_ref, kv_ref, o_ref, lse_ref, do_ref, dq_ref, dk_ref, dv_ref, dk_acc, dv_acc):
        n = pl.program_id(1)
        i, j = it[n], jt[n]

        @pl.when(n == 0)
        def _():
            dq_ref[...] = jnp.zeros_like(dq_ref)

        @pl.when(i == j)
        def _():
            dk_acc[...] = jnp.zeros_like(dk_acc)
            dv_acc[...] = jnp.zeros_like(dv_acc)

        def step(diagonal):
            rows = pl.ds(pl.multiple_of(i * tq, tq), tq)
            for g in range(G):
                slot = slice(g * MLA_SLOT, (g + 1) * MLA_SLOT)
                lanes = slice(g * HEAD_DIM, (g + 1) * HEAD_DIM)
                q, k = q_ref[:, slot], k_ref[:, slot]
                s = _dg(q, k, _NT) - jnp.tile(lse_ref[:, lanes], (1, tq // LANES))
                if diagonal:
                    s = _diag_mask(s, tq)
                p = jnp.exp(s)
                d = do_ref[:, lanes]
                delta = jnp.sum(d * o_ref[:, lanes], axis=-1, keepdims=True)
                dv_acc[:, lanes] += _dg(p, d, _TN)
                ds = p * (_dg(d, kv_ref[:, g * MLA_SLOT + HEAD_DIM:(g + 1) * MLA_SLOT], _NT) - delta)
                dk_acc[:, slot] += _dg(ds, q, _TN)
                dq_ref[rows, slot] += _dg(ds, k, _NN)

        @pl.when(i > j)
        def _():
            step(False)

        @pl.when(i == j)
        def _():
            step(True)

        @pl.when(i == nq - 1)
        def _():
            dk_ref[...] = dk_acc[...]
            dv_ref[...] = dv_acc[...]

    qblk = pl.BlockSpec((tq, G * MLA_SLOT), lambda h, n, it, jt: (it[n], h))
    oblk = pl.BlockSpec((tq, G * HEAD_DIM), lambda h, n, it, jt: (it[n], h))
    kblk = pl.BlockSpec((tq, G * MLA_SLOT), lambda h, n, it, jt: (jt[n], h))
    return pl.pallas_call(
        kern, name=name,
        grid_spec=pltpu.PrefetchScalarGridSpec(
            num_scalar_prefetch=2, grid=(HEADS // G, i_tab.shape[0]),
            in_specs=[qblk, kblk, kblk, oblk, oblk, oblk],
            out_specs=[pl.BlockSpec((S, G * MLA_SLOT), lambda h, n, it, jt: (0, h)), kblk,
                       pl.BlockSpec((tq, G * HEAD_DIM), lambda h, n, it, jt: (jt[n], h))],
            scratch_shapes=[pltpu.VMEM((tq, G * MLA_SLOT), F32), pltpu.VMEM((tq, G * HEAD_DIM), F32)]),
        out_shape=[_sds((S, HEADS * MLA_SLOT)), _sds((S, HEADS * MLA_SLOT)), _sds((S, HEADS * HEAD_DIM))],
        compiler_params=_params(("parallel", "arbitrary")))(i_tab, j_tab, qr, kr, kv, o, lse, do)


def loss_head(y, target, name):
    S, Dm = y.shape
    ts = _row_tile(S, Dm)

    def body(y_ref, t_ref):
        e = y_ref[...] - t_ref[...]
        tot = jnp.sum(jnp.sum(e * e, axis=-1, keepdims=True) / Dm, axis=0, keepdims=True)
        return e / Dm, jnp.broadcast_to(0.5 * tot, (SUBLANES, LANES))

    row = pl.BlockSpec((ts, Dm), lambda i: (i, 0))
    return tilecall(body, name, (S // ts,), [(y, row), (target, row)],
                    [(_sds((S, Dm)), row, None),
                     (_sds((SUBLANES, LANES)), pl.BlockSpec((SUBLANES, LANES), lambda i: (0, 0)), lambda: pl.program_id(0) == 0)],
                    ("arbitrary",))


_CHIP_FLIPS = ((1, 0), (0, 1), (1, 1))
_PEER_FLIPS = {
    "chips": ((1, 0, 0), (0, 1, 0), (1, 1, 0)),
    "sibling": ((0, 0, 1),),
    "all": tuple((a, b, c) for a in (0, 1) for b in (0, 1) for c in (0, 1))[1:],
}
_SLOT_WEIGHTS = {"chips": (2, 1, 0), "sibling": (0, 0, 1), "all": (4, 2, 1)}
_HBM = pl.BlockSpec(memory_space=pltpu.HBM)


def _me():
    return lax.axis_index("x"), lax.axis_index("y"), lax.axis_index("c")


def _remote(src, dst, send_sem, recv_sem, peer):
    return pltpu.make_async_remote_copy(src_ref=src, dst_ref=dst, send_sem=send_sem, recv_sem=recv_sem,
                                        device_id=peer, device_id_type=pl.DeviceIdType.MESH)


def exchange(arrs, group, slab_weights, name, keep_own=True):
    flips = _PEER_FLIPS[group]
    wx, wy, wc = _SLOT_WEIGHTS[group]
    n_slots = len(flips) + (1 if keep_own else 0)
    n = len(arrs)

    def slab(ref, a, pos):
        w = slab_weights[a]
        return ref if w is None else ref.at[w[0] * pos[0] + w[1] * pos[1] + w[2] * pos[2]]

    def kern(*refs):
        srcs, outs = refs[:n], refs[n:2 * n]
        send_sems, recv_sems = refs[2 * n:2 * n + 2]
        me = _me()
        my_slot = wx * me[0] + wy * me[1] + wc * me[2]
        copies = []
        if keep_own:
            local_sems = refs[2 * n + 2]
            for a in range(n):
                cp = pltpu.make_async_copy(slab(srcs[a], a, me), outs[a].at[my_slot], local_sems.at[a])
                cp.start()
                copies.append(cp)
        for f, flip in enumerate(flips):
            peer = tuple(m ^ b if b else m for m, b in zip(me, flip))
            for a in range(n):
                cp = _remote(slab(srcs[a], a, peer), outs[a].at[my_slot if keep_own else f],
                             send_sems.at[f, a], recv_sems.at[f, a], peer)
                cp.start()
                copies.append(cp)
        for cp in copies:
            cp.wait()

    out_shape = [_sds((n_slots,) + (a.shape if slab_weights[k] is None else a.shape[1:]), a.dtype) for k, a in enumerate(arrs)]
    sems = [pltpu.SemaphoreType.DMA((len(flips), n)), pltpu.SemaphoreType.DMA((len(flips), n))]
    return pl.pallas_call(
        kern, name=name, in_specs=[_HBM] * n, out_specs=[_HBM] * n, out_shape=out_shape,
        scratch_shapes=sems + ([pltpu.SemaphoreType.DMA((n,))] if keep_own else []))(*arrs)


def _chip_window(ref, kind, size, chip, layers):
    if kind == "rows":
        return ref.at[layers, pl.ds(chip * size, size), :]
    return ref.at[layers, :, pl.ds(pl.multiple_of(chip * size, LANES), size)]


def gather_big(shards, kinds, name):
    n = len(shards)
    fulls = []
    for s, kind in zip(shards, kinds):
        L, r, c = s.shape
        fulls.append(_sds((L, N_CHIPS * r, c) if kind == "rows" else (L, r, N_CHIPS * c), s.dtype))

    def kern(*refs):
        srcs, outs = refs[:n], refs[n:2 * n]
        ici_s, ici_r, d2d_s, d2d_r = refs[2 * n:]
        x, y, c = _me()
        my_chip = 2 * x + y
        ici, fwd = [], []
        for a in range(n):
            L, r, cc = shards[a].shape
            size = r if kinds[a] == "rows" else cc
            mine = pl.ds(c * (L // 2), L // 2)
            for f, (fx, fy) in enumerate(_CHIP_FLIPS):
                cp = _remote(srcs[a].at[mine], _chip_window(outs[a], kinds[a], size, my_chip, mine),
                             ici_s.at[a, f], ici_r.at[a, f], (x ^ fx, y ^ fy, c))
                cp.start()
                ici.append(cp)
        for a in range(n):
            L, r, cc = shards[a].shape
            size = r if kinds[a] == "rows" else cc
            mine = pl.ds(c * (L // 2), L // 2)
            for f, (fx, fy) in enumerate(_CHIP_FLIPS):
                ici[a * len(_CHIP_FLIPS) + f].wait_recv()
                landed = _chip_window(outs[a], kinds[a], size, 2 * (x ^ fx) + (y ^ fy), mine)
                cp = _remote(landed, landed, d2d_s.at[a, f], d2d_r.at[a, f], (x, y, 1 - c))
                cp.start()
                fwd.append(cp)
        for cp in ici:
            cp.wait_send()
        for cp in fwd:
            cp.wait()

    sem = pltpu.SemaphoreType.DMA((n, len(_CHIP_FLIPS)))
    return pl.pallas_call(kern, name=name, in_specs=[_HBM] * n, out_specs=[_HBM] * n, out_shape=fulls,
                          scratch_shapes=[sem, sem, sem, sem])(*shards)


def send_other_half(arrs, name):
    n = len(arrs)

    def kern(*refs):
        srcs, outs = refs[:n], refs[n:2 * n]
        send_sems, recv_sems = refs[2 * n:]
        x, y, c = _me()
        copies = []
        for a in range(n):
            hl = arrs[a].shape[0] // 2
            cp = _remote(srcs[a].at[pl.ds((1 - c) * hl, hl)], outs[a], send_sems.at[a], recv_sems.at[a], (x, y, 1 - c))
            cp.start()
            copies.append(cp)
        for cp in copies:
            cp.wait()

    return pl.pallas_call(
        kern, name=name, in_specs=[_HBM] * n, out_specs=[_HBM] * n,
        out_shape=[_sds((a.shape[0] // 2,) + a.shape[1:], a.dtype) for a in arrs],
        scratch_shapes=[pltpu.SemaphoreType.DMA((n,)), pltpu.SemaphoreType.DMA((n,))])(*arrs)


def scatter_to_chips(arrs, kinds, name):
    n = len(arrs)
    shapes = []
    for a, kind in zip(arrs, kinds):
        l, R, C = a.shape
        shapes.append((l, R // N_CHIPS, C) if kind == "rows" else (l, R, C // N_CHIPS))

    def kern(*refs):
        srcs, outs = refs[:n], refs[n:2 * n]
        send_sems, recv_sems = refs[2 * n:]
        x, y, c = _me()
        copies = []
        for a in range(n):
            size = shapes[a][1] if kinds[a] == "rows" else shapes[a][2]
            for f, (fx, fy) in enumerate(_CHIP_FLIPS):
                window = _chip_window(srcs[a], kinds[a], size, 2 * (x ^ fx) + (y ^ fy), slice(None))
                cp = _remote(window, outs[a].at[f], send_sems.at[a, f], recv_sems.at[a, f], (x ^ fx, y ^ fy, c))
                cp.start()
                copies.append(cp)
        for cp in copies:
            cp.wait()

    sem = pltpu.SemaphoreType.DMA((n, len(_CHIP_FLIPS)))
    return pl.pallas_call(
        kern, name=name, in_specs=[_HBM] * n, out_specs=[_HBM] * n,
        out_shape=[_sds((len(_CHIP_FLIPS),) + s, a.dtype) for s, a in zip(shapes, arrs)],
        scratch_shapes=[sem, sem])(*arrs)


def _stack_tile(r, c):
    for t in (1024, 704, 512, 352, 256, 128, 64, 32, 16):
        if r % t == 0 and t * c * 4 <= 3 * 512 * 1024:
            return t
    return r


def _window_map(kind, r, tr):
    nrt = r // tr
    if kind == "rows":
        return lambda l, i, chip: (l, chip[0] * nrt + i, 0)
    return lambda l, i, chip: (l, i, chip[0])


def place(full, shard, kind, chip, name):
    L, r, c = shard.shape
    tr = _stack_tile(r, c)
    wmap = _window_map(kind, r, tr)
    return tilecall(lambda chip_ref, s_ref, f_ref: (s_ref[...],), name, (L, r // tr),
                    [(shard, pl.BlockSpec((None, tr, c), lambda l, i, chip: (l, i, 0))), (full, pl.BlockSpec(memory_space=pl.ANY))],
                    [(_sds(full.shape, full.dtype), pl.BlockSpec((None, tr, c), lambda l, i, chip: wmap(l, i, chip)), None)],
                    ("parallel", "parallel"), prefetch=(chip,), aliases={1: 0})[0]


def add_cores(g, other, core, name):
    L, R, C = g.shape
    hl = L // 2
    tr = _stack_tile(R, C)
    blk = (None, tr, C)
    return tilecall(lambda core_ref, a_ref, b_ref: (a_ref[...] + b_ref[...],), name, (hl, R // tr),
                    [(g, pl.BlockSpec(blk, lambda l, i, core: (core[0] * hl + l, i, 0))),
                     (other, pl.BlockSpec(blk, lambda l, i, core: (l, i, 0)))],
                    [(_sds((hl, R, C), _MXU), pl.BlockSpec(blk, lambda l, i, core: (l, i, 0)), None)],
                    ("parallel", "parallel"), prefetch=(core,))[0]


def add_chips(own, got, kind, chip, name):
    nf, l, r, c = got.shape
    tr = _stack_tile(r, c)
    wmap = _window_map(kind, r, tr)

    def body(chip_ref, own_ref, *got_refs):
        acc = own_ref[...].astype(F32)
        for ref in got_refs:
            acc = acc + ref[...].astype(F32)
        return (acc,)

    return tilecall(body, name, (l, r // tr),
                    [(own, pl.BlockSpec((None, tr, c), lambda ll, i, chip: wmap(ll, i, chip)))] +
                    [(got, pl.BlockSpec((None, None, tr, c), lambda ll, i, chip, f=f: (f, ll, i, 0))) for f in range(nf)],
                    [(_sds((l, r, c)), pl.BlockSpec((None, tr, c), lambda ll, i, chip: (ll, i, 0)), None)],
                    ("parallel", "parallel"), prefetch=(chip,))[0]


def _adam_update(g, w, m, v):
    m_new = ADAM_B1 * m + (1.0 - ADAM_B1) * g
    v_new = ADAM_B2 * v + (1.0 - ADAM_B2) * jnp.square(g)
    m_hat = m_new / (1.0 - ADAM_B1 ** ADAM_STEP)
    v_hat = v_new / (1.0 - ADAM_B2 ** ADAM_STEP)
    delta = -ADAM_LR * (m_hat / (jnp.sqrt(v_hat) + ADAM_EPS) + ADAM_WD * w)
    return g, delta, m_new, v_new


def adamw_stacked(mine, theirs, w, m, v, core, name):
    L, r, c = w.shape
    hl = L // 2
    tr = _stack_tile(r, c)

    def body(core_ref, a_ref, b_ref, w_ref, m_ref, v_ref):
        is_mine = (pl.program_id(0) // hl) == core_ref[0]
        g = jnp.where(is_mine, a_ref[...], b_ref[...])
        return _adam_update(g, w_ref[...], m_ref[...], v_ref[...])

    full = pl.BlockSpec((None, tr, c), lambda l, i, core: (l, i, 0))
    out = (_sds((L, r, c)), full, None)
    return tilecall(body, name, (L, r // tr),
                    [(mine, pl.BlockSpec((None, tr, c), lambda l, i, core: (l % hl, i, 0))),
                     (theirs, pl.BlockSpec((None, None, tr, c), lambda l, i, core: (0, l % hl, i, 0))),
                     (w, full), (m, full), (v, full)],
                    [out, out, out, out], ("parallel", "parallel"), prefetch=(core,))


def _pack(arrs, dtype, row_multiple):
    flat = jnp.concatenate([a.reshape(-1).astype(dtype) for a in arrs])
    rows = -(-flat.shape[0] // LANES)
    rows = -(-rows // row_multiple) * row_multiple
    return jnp.pad(flat, (0, rows * LANES - flat.shape[0])).reshape(rows, LANES)


def _unpack(buf, shapes):
    flat = buf.reshape(-1)
    out, off = [], 0
    for s in shapes:
        n = int(np.prod(s))
        out.append(flat[off:off + n].reshape(s))
        off += n
    return out


def adamw_packed(gparts, w, m, v, name):
    P, R, _ = gparts.shape

    def body(g_ref, w_ref, m_ref, v_ref):
        g = g_ref[0]
        for p in range(1, P):
            g = g + g_ref[p]
        return _adam_update(g, w_ref[...], m_ref[...], v_ref[...])

    whole = pl.BlockSpec((R, LANES), lambda: (0, 0))
    out = (_sds((R, LANES)), whole, None)
    return tilecall(body, name, (), [(gparts, pl.BlockSpec((P, R, LANES), lambda: (0, 0, 0))), (w, whole), (m, whole), (v, whole)],
                    [out, out, out, out], ())


def _residual_out(a, w, x, next_gain):
    if next_gain is None:
        return mm(a, w, "nn", "mm_nn_add", add=x), None
    return mm(a, w, "nn", "mm_nn_add_rms", add=x, rms_gain=next_gain)


def _input_grad(pieces, x, gain, res):
    dh = None
    for d, w in pieces[:-1]:
        dh = mm(d, w, "nt", "mm_nt" if dh is None else "mm_nt_add", add=dh)
    d, w = pieces[-1]
    return mm(d, w, "nt", "mm_nt_norm_bwd", add=dh, norm_bwd=(x, gain, res))


def _ffn_fwd(x, h, next_gain, layer, gain, wts, cw8, cb):
    u = mm(h, win(wts["ffn_w_up"], layer), "nn", "mm_nn")
    a = ffn_act_fwd(u, cw8, cb, "ffn_act_fwd")
    y, h_next = _residual_out(a, win(wts["ffn_w_down"], layer), x, next_gain)
    return y, h_next, (x, h, u, a)


def _ffn_bwd(dy, saved, layer, gain, wts, cw8, cb, grads):
    x, h, u, a = saved
    grads["ffn_w_down"] = mm(a, dy, "tn", "mm_tn_into", into=win(grads["ffn_w_down"], layer))
    da = mm(dy, win(wts["ffn_w_down"], layer), "nt", "mm_nt")
    dyg, dyu, dcw_g, dcw_u, dcb_g, dcb_u = ffn_act_bwd(u, cw8, cb, da, "ffn_act_bwd")
    pieces = []
    for half, du in enumerate(ffn_conv_bwd(dyg, dyu, cw8, "ffn_conv_bwd")):
        cols = dict(col_off=half * D_FF, cols=D_FF)
        grads["ffn_w_up"] = mm(h, du, "tn", "mm_tn_into", into=win(grads["ffn_w_up"], layer, **cols))
        pieces.append((du, win(wts["ffn_w_up"], layer, **cols)))
    dx, d_gain = _input_grad(pieces, x, gain, dy)
    return dx, dict(gain=d_gain, conv_w=jnp.concatenate([dcw_g[0:3], dcw_u[0:3]], axis=1),
                    conv_b=jnp.concatenate([dcb_g, dcb_u], axis=1))


def _hgrn_w_in(wts, j, k):
    return win(wts["hgrn_w_in"], j, row_off=k * D_MODEL, rows=D_MODEL)


def _hgrn_layer_fwd(x, h, next_gain, j, gain, wts, lb, out_gain, mall):
    z = [mm(h, _hgrn_w_in(wts, j, k), "nn", "mm_nn") for k in range(4)]
    o, on, states = hgrn_fwd(z[0], z[1], z[2], z[3], lb, out_gain, mall, "hgrn_fwd")
    y, h_next = _residual_out(on, win(wts["hgrn_w_out"], j), x, next_gain)
    return y, h_next, (x, h, z, o, states, on)


def _hgrn_layer_bwd(dy, saved, j, gain, wts, lb, out_gain, mall, grads):
    x, h, z, o, states, on = saved
    grads["hgrn_w_out"] = mm(on, dy, "tn", "mm_tn_into", into=win(grads["hgrn_w_out"], j))
    don = mm(dy, win(wts["hgrn_w_out"], j), "nt", "mm_nt")
    dzq, dzf, dzi, dzg, dlb, d_out_gain = hgrn_bwd(z[0], z[1], z[2], z[3], lb, out_gain, mall, states, o, don, "hgrn_bwd")
    dz = [dzq, dzf, dzi, dzg]
    for k, d in enumerate(dz):
        grads["hgrn_w_in"] = mm(h, d, "tn", "mm_tn_into", into=win(grads["hgrn_w_in"], j, row_off=k * D_MODEL, rows=D_MODEL))
    dx, d_gain = _input_grad([(d, _hgrn_w_in(wts, j, k)) for k, d in enumerate(dz)], x, gain, dy)
    return dx, dict(gain=d_gain, lb=dlb, out_gain=d_out_gain)


_MLA_IN_WINDOWS = ((0, MLA_LORA), (MLA_LORA, MLA_LORA), (2 * MLA_LORA, HEAD_DIM))


def _mla_layer_fwd(x, h, next_gain, j, gain, wts, qa_gain, kva_gain, qn, kn, cos_t, sin_t):
    (c0, n), (c1, n1), (c2, n2) = _MLA_IN_WINDOWS
    cq, cqn = mm(h, win(wts["mla_w_in"], j, col_off=c0, cols=n), "nn", "mm_nn_rms", rms_gain=qa_gain)
    ckv, ckvn = mm(h, win(wts["mla_w_in"], j, col_off=c1, cols=n1), "nn", "mm_nn_rms", rms_gain=kva_gain)
    kr = mm(h, win(wts["mla_w_in"], j, col_off=c2, cols=n2), "nn", "mm_nn")
    qslots = mm(cqn, win(wts["mla_w_q_up"], j), "nn", "mm_nn")
    kv = mm(ckvn, win(wts["mla_w_kv_up"], j), "nn", "mm_nn")
    qr, krot = qk_fwd(qslots, kv, kr, qn, kn, cos_t, sin_t, "qk_fwd")
    o, lse = attn_fwd(qr, krot, kv, "attn_fwd")
    y, h_next = _residual_out(o, win(wts["mla_w_out"], j), x, next_gain)
    return y, h_next, (x, h, cq, ckv, kr, cqn, ckvn, qslots, kv, qr, krot, o, lse)


def _mla_layer_bwd(dy, saved, j, gain, wts, qa_gain, kva_gain, qn, kn, cos_t, sin_t, grads):
    x, h, cq, ckv, kr, cqn, ckvn, qslots, kv, qr, krot, o, lse = saved
    grads["mla_w_out"] = mm(o, dy, "tn", "mm_tn_into", into=win(grads["mla_w_out"], j))
    do = mm(dy, win(wts["mla_w_out"], j), "nt", "mm_nt")
    dq, dk, dv = attn_bwd(qr, krot, kv, o, lse, do, "attn_bwd")
    dqslots, dkv, dkr, d_qn, d_kn = qk_bwd(qslots, kv, kr, qn, kn, cos_t, sin_t, dq, dk, dv, "qk_bwd")
    grads["mla_w_q_up"] = mm(cqn, dqslots, "tn", "mm_tn_into", into=win(grads["mla_w_q_up"], j))
    dcq, d_qa = _input_grad([(dqslots, win(wts["mla_w_q_up"], j))], cq, qa_gain, None)
    grads["mla_w_kv_up"] = mm(ckvn, dkv, "tn", "mm_tn_into", into=win(grads["mla_w_kv_up"], j))
    dckv, d_kva = _input_grad([(dkv, win(wts["mla_w_kv_up"], j))], ckv, kva_gain, None)
    pieces = []
    for d, (c0, n) in zip((dcq, dckv, dkr), _MLA_IN_WINDOWS):
        grads["mla_w_in"] = mm(h, d, "tn", "mm_tn_into", into=win(grads["mla_w_in"], j, col_off=c0, cols=n))
        pieces.append((d, win(wts["mla_w_in"], j, col_off=c0, cols=n)))
    dx, d_gain = _input_grad(pieces, x, gain, dy)
    return dx, dict(gain=d_gain, qa=d_qa, kva=d_kva, qn=d_qn, kn=d_kn)


BIG = (("hgrn_w_in", "rows"), ("hgrn_w_out", "rows"), ("mla_w_in", "rows"), ("mla_w_q_up", "cols"),
       ("mla_w_kv_up", "cols"), ("mla_w_out", "rows"), ("ffn_w_up", "cols"), ("ffn_w_down", "rows"))
SMALL_SHARDED = (("mla_q_a_norm", 1), ("mla_kv_a_norm", 1), ("ffn_conv_w", 2))
REPLICATED = ("norm_mix", "norm_ffn", "hgrn_lower_bounds", "hgrn_out_norm", "mla_q_norm", "mla_k_norm", "ffn_conv_b")
WEIGHTS = ("norm_mix", "norm_ffn", "hgrn_w_in", "hgrn_lower_bounds", "hgrn_out_norm", "hgrn_w_out", "mla_w_in",
           "mla_q_a_norm", "mla_w_q_up", "mla_kv_a_norm", "mla_w_kv_up", "mla_q_norm", "mla_k_norm", "mla_w_out",
           "ffn_w_up", "ffn_conv_w", "ffn_conv_b", "ffn_w_down")


def _pad_cols(a, width):
    return jnp.pad(a, [(0, 0)] * (a.ndim - 1) + [(0, width - a.shape[-1])])


def _head_slots(w):
    lead, n = w.shape[:-1], w.shape[-1] // MLA_QK
    return _pad_cols(w.reshape(lead + (n, MLA_QK)), MLA_SLOT).reshape(lead + (n * MLA_SLOT,))


def _head_unslots(w):
    lead, n = w.shape[:-1], w.shape[-1] // MLA_SLOT
    return w.reshape(lead + (n, MLA_SLOT))[..., :MLA_QK].reshape(lead + (n * MLA_QK,))


def _to_stack_layout(name, a):
    if name == "hgrn_w_in":
        return a
    if name == "mla_w_in":
        return _pad_cols(a, MLA_IN_COLS)
    if name == "mla_w_q_up":
        return _head_slots(a)
    return a


def _from_stack_layout(name, a):
    if name == "mla_w_in":
        return a[..., :2 * MLA_LORA + MLA_ROPE]
    if name == "mla_w_q_up":
        return _head_unslots(a)
    return a


def _rope_tables(positions):
    inv_freq = ROPE_THETA ** (-jnp.arange(0, MLA_ROPE, 2, dtype=F32) / MLA_ROPE)
    ang = positions.astype(F32)[:, None] * inv_freq
    cos, sin = jnp.cos(ang), jnp.sin(ang)
    S = positions.shape[0]
    ones, zeros = jnp.ones((S, MLA_NOPE), F32), jnp.zeros((S, MLA_SLOT - MLA_QK), F32)
    return (jnp.concatenate([ones, cos, cos, zeros], axis=1),
            jnp.concatenate([jnp.zeros((S, MLA_NOPE), F32), -sin, sin, zeros], axis=1))


def kernel(x, positions, norm_mix, norm_ffn, hgrn_w_in, hgrn_lower_bounds, hgrn_out_norm, hgrn_w_out, mla_w_in, mla_q_a_norm, mla_w_q_up, mla_kv_a_norm, mla_w_kv_up, mla_q_norm, mla_k_norm, mla_w_out, ffn_w_up, ffn_conv_w, ffn_conv_b, ffn_w_down, loss_target, m_norm_mix, m_norm_ffn, m_hgrn_w_in, m_hgrn_lower_bounds, m_hgrn_out_norm, m_hgrn_w_out, m_mla_w_in, m_mla_q_a_norm, m_mla_w_q_up, m_mla_kv_a_norm, m_mla_w_kv_up, m_mla_q_norm, m_mla_k_norm, m_mla_w_out, m_ffn_w_up, m_ffn_conv_w, m_ffn_conv_b, m_ffn_w_down, v_norm_mix, v_norm_ffn, v_hgrn_w_in, v_hgrn_lower_bounds, v_hgrn_out_norm, v_hgrn_w_out, v_mla_w_in, v_mla_q_a_norm, v_mla_w_q_up, v_mla_kv_a_norm, v_mla_w_kv_up, v_mla_q_norm, v_mla_k_norm, v_mla_w_out, v_ffn_w_up, v_ffn_conv_w, v_ffn_conv_b, v_ffn_w_down):
    args = dict(locals())
    w = {n: args[n] for n in WEIGHTS}
    m = {n: args["m_" + n] for n in WEIGHTS}
    v = {n: args["v_" + n] for n in WEIGHTS}
    depth = norm_mix.shape[0]
    x0 = x[0]
    S = x0.shape[0]
    chip = (2 * lax.axis_index("x") + lax.axis_index("y")).astype(jnp.int32).reshape(1)
    core = lax.axis_index("c").astype(jnp.int32).reshape(1)
    big_names = [n for n, _ in BIG]
    kinds = [k for _, k in BIG]
    small_names = [n for n, _ in SMALL_SHARDED]
    small_axis = dict(SMALL_SHARDED)

    local = {n: _to_stack_layout(n, w[n]) for n in big_names}
    gathered = gather_big([local[n].astype(_MXU) for n in big_names], kinds, "gather_weights")
    wts = {n: place(g, local[n], k, chip, "place") for n, k, g in zip(big_names, kinds, gathered)}
    (got_small,) = exchange([_pack([w[n] for n in small_names], F32, SUBLANES)], "chips", [None], "gather_small")
    per_chip = [_unpack(got_small[p], [w[n].shape for n in small_names]) for p in range(N_CHIPS)]
    small = {n: jnp.concatenate([per_chip[p][k] for p in range(N_CHIPS)], axis=small_axis[n]) for k, n in enumerate(small_names)}

    cos_t, sin_t = _rope_tables(positions[0])
    lbs = lower_bound_fwd(hgrn_lower_bounds, "lower_bound_fwd")
    mall = jnp.asarray(_hgrn_sum_matrix(min(HGRN_CHUNK, S)), _MXU)
    qn = _pad_cols(mla_q_norm, MLA_SLOT)
    kn = _pad_cols(mla_k_norm, MLA_SLOT)
    cw8 = jnp.pad(small["ffn_conv_w"], ((0, 0), (0, SUBLANES - 3), (0, 0)))

    def mixer_args(layer):
        j = layer // 2
        if layer % 2 == 0:
            return (j, norm_mix[layer:layer + 1], wts, lbs[j:j + 1], hgrn_out_norm[j:j + 1], mall)
        return (j, norm_mix[layer:layer + 1], wts, small["mla_q_a_norm"][j:j + 1], small["mla_kv_a_norm"][j:j + 1],
                qn[j:j + 1], kn[j:j + 1], cos_t, sin_t)

    def ffn_args(layer):
        return (layer, norm_ffn[layer:layer + 1], wts, cw8[layer], ffn_conv_b[layer:layer + 1])

    xc, h = x0, rms_fwd(x0, norm_mix[0:1], "rms_fwd")
    saved = []
    for layer in range(depth):
        fwd = _hgrn_layer_fwd if layer % 2 == 0 else _mla_layer_fwd
        xc, h, s_mix = fwd(xc, h, norm_ffn[layer:layer + 1], *mixer_args(layer))
        xc, h, s_ffn = _ffn_fwd(xc, h, norm_mix[layer + 1:layer + 2] if layer + 1 < depth else None, *ffn_args(layer))
        saved.append((s_mix, s_ffn))

    dh, loss_blk = loss_head(xc, loss_target[0], "loss_head")
    loss = lax.psum(loss_blk[0, 0], MESH_AXES)

    grads = {n: lax.empty(g.shape, F32) for n, g in zip(big_names, gathered)}
    g_mix, g_ffn = [None] * depth, [None] * depth
    for layer in reversed(range(depth)):
        s_mix, s_ffn = saved[layer]
        dh, g_ffn[layer] = _ffn_bwd(dh, s_ffn, *ffn_args(layer), grads)
        bwd = _hgrn_layer_bwd if layer % 2 == 0 else _mla_layer_bwd
        dh, g_mix[layer] = bwd(dh, s_mix, *mixer_args(layer), grads)
    hg = [g_mix[l] for l in range(0, depth, 2)]
    mg = [g_mix[l] for l in range(1, depth, 2)]
    d_p = lower_bound_bwd(hgrn_lower_bounds, hg[1]["lb"], "lower_bound_bwd")

    g_list = [grads[n] for n in big_names]
    from_core = send_other_half(g_list, "reduce_cores_in")
    chip_sums = [add_cores(g, o, core, "add_cores") for g, o in zip(g_list, from_core)]
    from_chips = scatter_to_chips(chip_sums, kinds, "reduce_chips")
    reduced = [add_chips(own, got, k, chip, "add_chips") for own, got, k in zip(chip_sums, from_chips, kinds)]
    other_half = exchange(reduced, "sibling", [None] * len(reduced), "reduce_cores_out", keep_own=False)
    big_out = {}
    for n, mine, theirs in zip(big_names, reduced, other_half):
        outs = adamw_stacked(mine, theirs, local[n], _to_stack_layout(n, m[n]), _to_stack_layout(n, v[n]), core, "adamw")
        big_out[n] = [_from_stack_layout(n, o) for o in outs]

    small_grads = {
        "norm_mix": jnp.concatenate([g["gain"] for g in g_mix], axis=0),
        "norm_ffn": jnp.concatenate([g["gain"] for g in g_ffn], axis=0),
        "hgrn_lower_bounds": d_p[0:2],
        "hgrn_out_norm": jnp.concatenate([g["out_gain"] for g in hg], axis=0),
        "mla_q_a_norm": jnp.concatenate([g["qa"] for g in mg], axis=0),
        "mla_kv_a_norm": jnp.concatenate([g["kva"] for g in mg], axis=0),
        "mla_q_norm": jnp.concatenate([g["qn"][:, :MLA_QK] for g in mg], axis=0),
        "mla_k_norm": jnp.concatenate([g["kn"][:, :MLA_QK] for g in mg], axis=0),
        "ffn_conv_w": jnp.stack([g["conv_w"] for g in g_ffn]),
        "ffn_conv_b": jnp.concatenate([g["conv_b"] for g in g_ffn], axis=0),
    }

    def chip_part(n, p):
        size = w[n].shape[small_axis[n]]
        return lax.slice_in_dim(small_grads[n], p * size, (p + 1) * size, axis=small_axis[n])

    to_chips = jnp.stack([_pack([chip_part(n, p) for n in small_names], F32, SUBLANES) for p in range(N_CHIPS)])
    rep_g, shard_g = exchange([_pack([small_grads[n] for n in REPLICATED], F32, SUBLANES), to_chips], "all",
                              [None, (2, 1, 0)], "reduce_small")
    small_out = {}
    for names, gparts in ((REPLICATED, rep_g), (small_names, shard_g)):
        packed = adamw_packed(gparts, *[_pack([t[n] for n in names], F32, SUBLANES) for t in (w, m, v)], "adamw_small")
        unpacked = [_unpack(buf, [w[n].shape for n in names]) for buf in packed]
        for k, n in enumerate(names):
            small_out[n] = [u[k] for u in unpacked]

    result = [loss, dh[None]]
    for k in range(4):
        result += [(big_out[n] if n in big_out else small_out[n])[k] for n in WEIGHTS]
    return tuple(result)
```

```python
import numpy as np
import jax
import jax.numpy as jnp
from jax import lax
from jax.experimental import pallas as pl
from jax.experimental.pallas import tpu as pltpu

F32 = jnp.float32
BF16 = jnp.bfloat16
_MXU = BF16

RMS_EPS = 1e-6
D_MODEL = 1024
HEADS = 8
HEAD_DIM = 128
HGRN_CHUNK = 128
MLA_NOPE = 128
MLA_ROPE = 64
MLA_QK = MLA_NOPE + MLA_ROPE
MLA_SLOT = 256
MLA_LORA = 256
MLA_IN_COLS = 2 * MLA_LORA + HEAD_DIM
ROPE_THETA = 10000.0
D_FF = 2816
FF_BLOCK = 1408
LANES = 128
SUBLANES = 8

ADAM_LR = 0.001
ADAM_B1 = 0.9
ADAM_B2 = 0.999
ADAM_EPS = 1e-08
ADAM_WD = 0.01
ADAM_STEP = 10

VMEM_LIMIT = 56 * 1024 * 1024
MESH_AXES = ("x", "y", "c")
N_CHIPS = 4

_NN = ((1,), (0,))
_NT = ((1,), (1,))
_TN = ((0,), (0,))


def _dg(a, b, dims):
    return lax.dot_general(a.astype(_MXU), b.astype(_MXU), (dims, ((), ())), preferred_element_type=F32)


@jax.custom_vjp
def kdot(a, b):
    return _dg(a, b, _NN)


kdot.defvjp(lambda a, b: (_dg(a, b, _NN), (a, b)), lambda r, g: (_dg(g, r[1], _NT), _dg(r[0], g, _TN)))


@jax.custom_vjp
def kdot_nt(a, b):
    return _dg(a, b, _NT)


kdot_nt.defvjp(lambda a, b: (_dg(a, b, _NT), (a, b)), lambda r, g: (_dg(g, r[1], _NN), _dg(g, r[0], _TN)))


@jax.custom_vjp
def kdot_tn(a, b):
    return _dg(a, b, _TN)


kdot_tn.defvjp(lambda a, b: (_dg(a, b, _TN), (a, b)), lambda r, g: (_dg(r[1], g, _NT), _dg(r[0], g, _NN)))


def _pick(d, prefs):
    for p in prefs:
        if d >= p and d % p == 0:
            return p
    return d


def _params(sem):
    return pltpu.CompilerParams(dimension_semantics=sem, vmem_limit_bytes=VMEM_LIMIT)


def _sds(shape, dtype=F32):
    return jax.ShapeDtypeStruct(shape, dtype)


def win(arr, layer, row_off=0, col_off=0, rows=None, cols=None):
    return (arr, layer, row_off, col_off, rows or arr.shape[1] - row_off, cols or arr.shape[2] - col_off)


def mm(a, b, mode, name, add=None, out_dtype=F32, into=None, rms_gain=None, norm_bwd=None):
    if isinstance(b, tuple):
        b_arr, b_layer, b_r0, b_c0, b_rows, b_cols = b
    else:
        b_arr, b_layer, b_r0, b_c0, (b_rows, b_cols) = b, None, 0, 0, b.shape
    if mode == "nn":
        (M, K), (K2, N) = a.shape, (b_rows, b_cols)
    elif mode == "nt":
        (M, K), (N, K2) = a.shape, (b_rows, b_cols)
    else:
        (K, M), (K2, N) = a.shape, (b_rows, b_cols)
    assert K == K2, (name, a.shape, b_rows, b_cols)
    tm = M if M <= 1024 else _pick(M, (1024, 1408, 512, 256, 128))
    if norm_bwd is not None:
        tm = min(tm, 512)
    tn = N if N <= 1024 else _pick(N, (1024, 1408, 512, 256, 128))
    tk = K if K <= 2048 else _pick(K, (2048, 2816, 1024, 512, 256, 128))
    nk = K // tk
    dims = {"nn": _NN, "nt": _NT, "tn": _TN}[mode]
    a_spec = pl.BlockSpec((tk, tm), lambda i, j, k: (k, i)) if mode == "tn" else pl.BlockSpec((tm, tk), lambda i, j, k: (i, k))
    b_blk = (tn, tk) if mode == "nt" else (tk, tn)
    assert b_r0 % b_blk[0] == 0 and b_c0 % b_blk[1] == 0, (name, b_r0, b_c0, b_blk)
    br, bc = b_r0 // b_blk[0], b_c0 // b_blk[1]
    if mode == "nt":
        b_idx = lambda i, j, k: (br + j, bc + k)
    else:
        b_idx = lambda i, j, k: (br + k, bc + j)
    if b_layer is None:
        b_spec = pl.BlockSpec(b_blk, b_idx)
    else:
        b_spec = pl.BlockSpec((None,) + b_blk, lambda i, j, k: (b_layer,) + b_idx(i, j, k))
    plain = pl.BlockSpec((tm, tn), lambda i, j, k: (i, j))
    has_add, has_rms, has_nb = add is not None, rms_gain is not None, norm_bwd is not None
    assert not (has_rms or has_nb) or (tn == N and into is None and not (has_rms and has_nb)), name
    vec = pl.BlockSpec((1, tn), lambda i, j, k: (0, j))
    ins = [a, b_arr] + ([add] if has_add else []) + ([rms_gain] if has_rms else [])
    specs = [a_spec, b_spec] + ([plain] if has_add else []) + ([vec] if has_rms else [])
    n_in = len(ins)
    if has_nb:
        nb_x, nb_gain, nb_res = norm_bwd
        ins += [nb_x, nb_gain] + ([nb_res] if nb_res is not None else [])
        specs += [plain, vec] + ([plain] if nb_res is not None else [])
    aliases = {}
    if into is None:
        o_spec, out_shape = plain, _sds((M, N), out_dtype)
    else:
        buf, o_layer, o_r0, o_c0, o_rows, o_cols = into
        assert (o_rows, o_cols) == (M, N) and o_r0 % tm == 0 and o_c0 % tn == 0, (name, into[1:], M, N, tm, tn)
        orow, ocol = o_r0 // tm, o_c0 // tn
        o_spec = pl.BlockSpec((None, tm, tn), lambda i, j, k: (o_layer, orow + i, ocol + j))
        out_shape = _sds(buf.shape, buf.dtype)
        aliases = {len(ins): 0}
        ins.append(buf)
        specs.append(pl.BlockSpec(memory_space=pl.ANY))

    n_all_in = len(ins)

    def kern(*refs):
        a_ref, b_ref = refs[0], refs[1]
        add_ref = refs[2] if has_add else None
        o_ref = refs[n_all_in]

        def finish(r):
            if has_add:
                r = r + add_ref[...].astype(F32)
            if has_nb:
                _, vjp = jax.vjp(lambda xv, gv: _rms(xv, gv, N), refs[n_in][...], refs[n_in + 1][...])
                dx, dgain = vjp(r)
                o_ref[...] = dx if nb_res is None else dx + refs[n_in + 2][...]
                _store(refs[n_all_in + 1], dgain, pl.program_id(0) == 0)
                return
            o_ref[...] = r.astype(o_ref.dtype)
            if has_rms:
                refs[n_all_in + 1][...] = _rms(r, refs[n_in - 1][...], N).astype(_MXU)

        if nk == 1:
            finish(_dg(a_ref[...], b_ref[...], dims))
            return
        acc = refs[-1]
        k = pl.program_id(2)

        @pl.when(k == 0)
        def _():
            acc[...] = jnp.zeros_like(acc)

        acc[...] += _dg(a_ref[...], b_ref[...], dims)

        @pl.when(k == nk - 1)
        def _():
            finish(acc[...])

    if has_rms:
        o_spec, out_shape = [o_spec, plain], [out_shape, _sds((M, N), _MXU)]
    if has_nb:
        o_spec, out_shape = [o_spec, vec], [out_shape, _sds((1, N))]
    return pl.pallas_call(
        kern, name=name, grid=(M // tm, N // tn, nk), in_specs=specs, out_specs=o_spec, out_shape=out_shape,
        scratch_shapes=[pltpu.VMEM((tm, tn), F32)] if nk > 1 else [], input_output_aliases=aliases,
        compiler_params=_params(("arbitrary" if has_nb else "parallel", "parallel", "arbitrary")))(*ins)


def _store(ref, val, first):
    if first is None:
        ref[...] = val.astype(ref.dtype)
        return

    @pl.when(first)
    def _():
        ref[...] = val.astype(ref.dtype)

    @pl.when(jnp.logical_not(first))
    def _():
        ref[...] += val.astype(ref.dtype)


def tilecall(body, name, grid, ins, outs, sem, prefetch=(), aliases=None):
    n_pre, n_in = len(prefetch), len(ins)

    def kern(*refs):
        vals = body(*refs[:n_pre + n_in])
        for ref, val, (_, _, first) in zip(refs[n_pre + n_in:], vals, outs):
            _store(ref, val, None if first is None else first())

    in_specs, out_specs = [s for _, s in ins], [s for _, s, _ in outs]
    kwargs = dict(name=name, out_shape=[sh for sh, _, _ in outs], compiler_params=_params(sem),
                  input_output_aliases={n_pre + k: v for k, v in (aliases or {}).items()})
    if n_pre:
        kwargs["grid_spec"] = pltpu.PrefetchScalarGridSpec(num_scalar_prefetch=n_pre, grid=grid, in_specs=in_specs,
                                                           out_specs=out_specs)
    else:
        kwargs.update(grid=grid, in_specs=in_specs, out_specs=out_specs)
    return pl.pallas_call(kern, **kwargs)(*prefetch, *[a for a, _ in ins])


def _rms(x, g, n):
    ms = jnp.sum(x * x, axis=-1, keepdims=True) / n
    return x * lax.rsqrt(ms + RMS_EPS) * g


def _row_tile(S, w):
    return min(S, 512 if w <= 1024 else 256)


def rms_fwd(x, g, name, col=0, w=None):
    S = x.shape[0]
    w = w or x.shape[1]
    ts = _row_tile(S, w)
    return tilecall(
        lambda x_ref, g_ref: (_rms(x_ref[...], g_ref[...], w),), name, (S // ts,),
        [(x, pl.BlockSpec((ts, w), lambda i: (i, col))), (g, pl.BlockSpec((1, w), lambda i: (0, 0)))],
        [(_sds((S, w), _MXU), pl.BlockSpec((ts, w), lambda i: (i, 0)), None)], ("parallel",))[0]


def _shifted(u, halo_ref, is_first):
    rid = lax.broadcasted_iota(jnp.int32, (SUBLANES, 1), 0)
    h7 = jnp.where(is_first, 0.0, halo_ref[7:8, :])
    h6 = jnp.where(is_first, 0.0, halo_ref[6:7, :])
    r1, r2 = pltpu.roll(u, 1, 0), pltpu.roll(u, 2, 0)
    top1 = jnp.where(rid == 0, h7, r1[:SUBLANES])
    top2 = jnp.where(rid == 0, h6, jnp.where(rid == 1, h7, r2[:SUBLANES]))
    return jnp.concatenate([top1, r1[SUBLANES:]], axis=0), jnp.concatenate([top2, r2[SUBLANES:]], axis=0)


def _conv(u, u1, u2, cw_ref, cb_ref):
    return ((cb_ref[...] + u2 * cw_ref[0:1, :]) + u1 * cw_ref[1:2, :]) + u * cw_ref[2:3, :]


def _ffn_specs(S, ts, jmap):
    hb = ts // SUBLANES
    return (pl.BlockSpec((ts, FF_BLOCK), lambda j, i: (i, jmap(j))),
            pl.BlockSpec((SUBLANES, FF_BLOCK), lambda j, i: (jnp.maximum(i * hb - 1, 0), jmap(j))),
            pl.BlockSpec((SUBLANES, FF_BLOCK), lambda j, i: (0, jmap(j))),
            pl.BlockSpec((1, FF_BLOCK), lambda j, i: (0, jmap(j))))


def ffn_act_fwd(u, cw8, cb, name):
    S = u.shape[0]
    ts = _row_tile(S, 2 * D_FF)
    nb = D_FF // FF_BLOCK

    def body(ug, hg, cwg, cbg, uu, hu, cwu, cbu):
        first = pl.program_id(1) == 0
        g = ug[...]
        g1, g2 = _shifted(g, hg, first)
        yg = _conv(g, g1, g2, cwg, cbg)
        v = uu[...]
        v1, v2 = _shifted(v, hu, first)
        yu = _conv(v, v1, v2, cwu, cbu)
        return (yg * jax.nn.sigmoid(yg) * yu,)

    sg = _ffn_specs(S, ts, lambda j: j)
    su = _ffn_specs(S, ts, lambda j: j + nb)
    ins = [(u, sg[0]), (u, sg[1]), (cw8, sg[2]), (cb, sg[3]), (u, su[0]), (u, su[1]), (cw8, su[2]), (cb, su[3])]
    return tilecall(body, name, (nb, S // ts), ins,
                    [(_sds((S, D_FF), _MXU), pl.BlockSpec((ts, FF_BLOCK), lambda j, i: (i, j)), None)],
                    ("parallel", "parallel"))[0]


def ffn_act_bwd(u, cw8, cb, da, name):
    S = u.shape[0]
    ts = _row_tile(S, 2 * D_FF)
    nb = D_FF // FF_BLOCK

    def taps(dy, x, x1, x2):
        return jnp.concatenate(
            [jnp.sum(dy * x2, axis=0, keepdims=True), jnp.sum(dy * x1, axis=0, keepdims=True),
             jnp.sum(dy * x, axis=0, keepdims=True), jnp.zeros((SUBLANES - 3, dy.shape[1]), F32)], axis=0)

    def body(ug, hg, cwg, cbg, uu, hu, cwu, cbu, da_ref):
        first = pl.program_id(1) == 0
        g = ug[...]
        g1, g2 = _shifted(g, hg, first)
        yg = _conv(g, g1, g2, cwg, cbg)
        v = uu[...]
        v1, v2 = _shifted(v, hu, first)
        yu = _conv(v, v1, v2, cwu, cbu)
        d = da_ref[...]
        sg = jax.nn.sigmoid(yg)
        dyg = d * yu * (sg * (1.0 + yg * (1.0 - sg)))
        dyu = d * (yg * sg)
        return (dyg, dyu, taps(dyg, g, g1, g2), taps(dyu, v, v1, v2),
                jnp.sum(dyg, axis=0, keepdims=True), jnp.sum(dyu, axis=0, keepdims=True))

    sg_ = _ffn_specs(S, ts, lambda j: j)
    su_ = _ffn_specs(S, ts, lambda j: j + nb)
    row = pl.BlockSpec((ts, FF_BLOCK), lambda j, i: (i, j))
    ins = [(u, sg_[0]), (u, sg_[1]), (cw8, sg_[2]), (cb, sg_[3]), (u, su_[0]), (u, su_[1]), (cw8, su_[2]), (cb, su_[3]), (da, row)]
    first_row = lambda: pl.program_id(1) == 0
    dy, dcw, dcb = (_sds((S, D_FF)), row, None), (_sds((SUBLANES, D_FF)), sg_[2], first_row), (_sds((1, D_FF)), sg_[3], first_row)
    return tilecall(body, name, (nb, S // ts), ins, [dy, dy, dcw, dcw, dcb, dcb], ("parallel", "arbitrary"))


def ffn_conv_bwd(dyg, dyu, cw8, name):
    S = dyg.shape[0]
    ts = _row_tile(S, 2 * D_FF)
    hb = ts // SUBLANES
    nrow = S // ts
    nb = D_FF // FF_BLOCK

    def back(dy_ref, halo_ref, cw_ref):
        last = pl.program_id(1) == nrow - 1
        d = dy_ref[...]
        rid = lax.broadcasted_iota(jnp.int32, (SUBLANES, 1), 0)
        n0 = jnp.where(last, 0.0, halo_ref[0:1, :])
        n1 = jnp.where(last, 0.0, halo_ref[1:2, :])
        r1, r2 = pltpu.roll(d, ts - 1, 0), pltpu.roll(d, ts - 2, 0)
        end1 = jnp.where(rid == SUBLANES - 1, n0, r1[ts - SUBLANES:])
        end2 = jnp.where(rid == SUBLANES - 1, n1, jnp.where(rid == SUBLANES - 2, n0, r2[ts - SUBLANES:]))
        d1 = jnp.concatenate([r1[:ts - SUBLANES], end1], axis=0)
        d2 = jnp.concatenate([r2[:ts - SUBLANES], end2], axis=0)
        return d * cw_ref[2:3, :] + d1 * cw_ref[1:2, :] + d2 * cw_ref[0:1, :]

    row = pl.BlockSpec((ts, FF_BLOCK), lambda j, i: (i, j))
    halo = pl.BlockSpec((SUBLANES, FF_BLOCK), lambda j, i: (jnp.minimum((i + 1) * hb, S // SUBLANES - 1), j))
    ins = [(dyg, row), (dyg, halo), (cw8, pl.BlockSpec((SUBLANES, FF_BLOCK), lambda j, i: (0, j))),
           (dyu, row), (dyu, halo), (cw8, pl.BlockSpec((SUBLANES, FF_BLOCK), lambda j, i: (0, j + nb)))]
    out = (_sds((S, D_FF), _MXU), row, None)
    return tilecall(lambda a, b, c, d, e, f: (back(a, b, c), back(d, e, f)), name, (nb, nrow), ins, [out, out],
                    ("parallel", "parallel"))


def _hgrn_levels(C):
    out, m = [], C // 2
    while m >= 1:
        out.append(m)
        m //= 2
    return out


def _hgrn_sum_matrix(C):
    t = np.arange(C)[:, None]
    u = np.arange(C)[None, :]
    blocks = [u <= t, u > t]
    for m in _hgrn_levels(C):
        r = (t // (2 * m)) * (2 * m) + m
        right = (t % (2 * m)) >= m
        blocks.append((right & (u > r) & (u <= t)) | ((~right) & (u > t) & (u <= r)))
    return np.concatenate(blocks, axis=0).astype(np.float32)


def _make_partial_sums(nb, C):
    @jax.custom_vjp
    def sums(mall, lf):
        hi = lf.astype(_MXU)
        mid = (lf - hi.astype(F32)).astype(_MXU)
        e2 = _dg(mall, jnp.concatenate([hi, mid], axis=1), _NN)
        e = e2[:, :HEAD_DIM] + e2[:, HEAD_DIM:]
        return tuple(e[b * C:(b + 1) * C] for b in range(nb))

    def fwd(mall, lf):
        return sums(mall, lf), mall

    def bwd(mall, gs):
        return jnp.zeros_like(mall), _dg(mall, jnp.concatenate(gs, axis=0), _TN)

    sums.defvjp(fwd, bwd)
    return sums


def _hgrn_chunk(zq, zf, v, lb, st, mall, C):
    levels = _hgrn_levels(C)
    qs = zq * jax.nn.sigmoid(zq)
    fg = lb + (1.0 - lb) * jax.nn.sigmoid(zf)
    k = 1.0 - fg
    e = _make_partial_sums(2 + len(levels), C)(mall, jnp.log(fg))
    g_incl, g_after = e[0], e[1]
    rid = lax.broadcasted_iota(jnp.int32, (C, 1), 0)
    tt = lax.broadcasted_iota(jnp.int32, (C, C), 0)
    ss = lax.broadcasted_iota(jnp.int32, (C, C), 1)
    o = kdot_nt(qs * jnp.exp(g_incl), st)
    o = o + jnp.sum(qs * k, axis=-1, keepdims=True) * v
    scores = jnp.zeros((C, C), F32)
    for li, m in enumerate(levels):
        sh = int(np.log2(m))
        right = ((rid >> sh) & 1) == 1
        both = jnp.where(right, qs, k) * jnp.exp(e[2 + li])
        pair = ((tt >> (sh + 1)) == (ss >> (sh + 1))) & (((tt >> sh) & 1) == 1) & (((ss >> sh) & 1) == 0)
        scores = scores + jnp.where(pair, kdot_nt(both, both), 0.0)
    o = o + kdot(scores, v)
    g_last = jnp.sum(jnp.where(rid == C - 1, g_incl, 0.0), axis=0, keepdims=True)
    st_new = st * jnp.exp(g_last) + kdot_tn(v, k * jnp.exp(g_after))
    return o, st_new


HGRN_HEADS_PER_STEP = 4
_HGRN_LANES = HGRN_HEADS_PER_STEP * HEAD_DIM


def _hgrn_in_specs(C, nc, rev):
    cm = (lambda c: nc - 1 - c) if rev else (lambda c: c)
    blk = lambda: pl.BlockSpec((C, _HGRN_LANES), lambda h, c: (cm(c), h))
    return cm, [blk(), blk(), blk(), pl.BlockSpec((1, _HGRN_LANES), lambda h, c: (0, h))]


def _hgrn_state_spec(cm):
    return pl.BlockSpec((HGRN_HEADS_PER_STEP, None, HEAD_DIM, HEAD_DIM), lambda h, c: (h, cm(c), 0, 0))


def _hgrn_out(o, g, gain):
    return _rms(o, gain, HEAD_DIM) * (g * jax.nn.sigmoid(g))


def hgrn_fwd(zq, zf, zi, zg, lb, out_gain, mall, name):
    S = zq.shape[0]
    C = min(HGRN_CHUNK, S)
    nc = S // C

    def kern(zq_ref, zf_ref, zi_ref, lb_ref, zg_ref, gain_ref, mall_ref, o_ref, on_ref, st_ref, st):
        @pl.when(pl.program_id(1) == 0)
        def _():
            st[...] = jnp.zeros_like(st)

        mall_v = mall_ref[...]
        for g in range(HGRN_HEADS_PER_STEP):
            lanes = slice(g * HEAD_DIM, (g + 1) * HEAD_DIM)
            s_in = st[g]
            st_ref[g] = s_in
            o, s_new = _hgrn_chunk(zq_ref[:, lanes], zf_ref[:, lanes], zi_ref[:, lanes], lb_ref[:, lanes], s_in, mall_v, C)
            o_ref[:, lanes] = o
            on_ref[:, lanes] = _hgrn_out(o, zg_ref[:, lanes], gain_ref[...]).astype(on_ref.dtype)
            st[g] = s_new

    cm, specs = _hgrn_in_specs(C, nc, False)
    row = pl.BlockSpec((C, _HGRN_LANES), lambda h, c: (c, h))
    return pl.pallas_call(
        kern, name=name, grid=(HEADS // HGRN_HEADS_PER_STEP, nc),
        in_specs=specs + [row, pl.BlockSpec((1, HEAD_DIM), lambda h, c: (0, 0)), pl.BlockSpec(mall.shape, lambda h, c: (0, 0))],
        out_specs=[row, row, _hgrn_state_spec(cm)],
        out_shape=[_sds((S, D_MODEL)), _sds((S, D_MODEL), _MXU), _sds((HEADS, nc, HEAD_DIM, HEAD_DIM))],
        scratch_shapes=[pltpu.VMEM((HGRN_HEADS_PER_STEP, HEAD_DIM, HEAD_DIM), F32)],
        compiler_params=_params(("parallel", "arbitrary")))(zq, zf, zi, lb, zg, out_gain, mall)


def hgrn_bwd(zq, zf, zi, zg, lb, out_gain, mall, states, o, don, name):
    S = zq.shape[0]
    C = min(HGRN_CHUNK, S)
    nc = S // C

    def kern(zq_ref, zf_ref, zi_ref, lb_ref, zg_ref, gain_ref, mall_ref, st_ref, o_ref, don_ref,
             dq_ref, df_ref, di_ref, dg_ref, dlb_ref, dgain_ref, dst):
        first = pl.program_id(1) == 0

        @pl.when(first)
        def _():
            dst[...] = jnp.zeros_like(dst)

        mall_v = mall_ref[...]
        gls, dgain = [], None
        for g in range(HGRN_HEADS_PER_STEP):
            lanes = slice(g * HEAD_DIM, (g + 1) * HEAD_DIM)
            _, out_vjp = jax.vjp(_hgrn_out, o_ref[:, lanes], zg_ref[:, lanes], gain_ref[...])
            do, dzg, dgn = out_vjp(don_ref[:, lanes])
            dg_ref[:, lanes] = dzg.astype(dg_ref.dtype)
            dgain = dgn if dgain is None else dgain + dgn
            _, vjp = jax.vjp(lambda a, b, c, d, e: _hgrn_chunk(a, b, c, d, e, mall_v, C),
                             zq_ref[:, lanes], zf_ref[:, lanes], zi_ref[:, lanes], lb_ref[:, lanes], st_ref[g])
            ga, gb, gv, gl, gs = vjp((do, dst[g]))
            dq_ref[:, lanes] = ga.astype(dq_ref.dtype)
            df_ref[:, lanes] = gb.astype(df_ref.dtype)
            di_ref[:, lanes] = gv.astype(di_ref.dtype)
            gls.append(gl)
            dst[g] = gs
        _store(dlb_ref, jnp.concatenate(gls, axis=1), first)
        _store(dgain_ref, dgain, first & (pl.program_id(0) == 0))

    cm, specs = _hgrn_in_specs(C, nc, True)
    row = lambda: pl.BlockSpec((C, _HGRN_LANES), lambda h, c: (cm(c), h))
    vec = pl.BlockSpec((1, HEAD_DIM), lambda h, c: (0, 0))
    wide = _sds((S, D_MODEL), _MXU)
    return pl.pallas_call(
        kern, name=name, grid=(HEADS // HGRN_HEADS_PER_STEP, nc),
        in_specs=specs + [row(), vec, pl.BlockSpec(mall.shape, lambda h, c: (0, 0)), _hgrn_state_spec(cm), row(), row()],
        out_specs=[row(), row(), row(), row(), pl.BlockSpec((1, _HGRN_LANES), lambda h, c: (0, h)), vec],
        out_shape=[wide, wide, wide, wide, _sds((1, D_MODEL)), _sds((1, HEAD_DIM))],
        scratch_shapes=[pltpu.VMEM((HGRN_HEADS_PER_STEP, HEAD_DIM, HEAD_DIM), F32)],
        compiler_params=_params(("arbitrary", "arbitrary")))(zq, zf, zi, lb, zg, out_gain, mall, states, o, don)


def _lb_soft(p0, p1):
    mx = jnp.maximum(p0, p1)
    e0, e1 = jnp.exp(p0 - mx), jnp.exp(p1 - mx)
    s0, s1 = e0 / (e0 + e1), e1 / (e0 + e1)
    return (s0 + s1) - s0


def lower_bound_fwd(p, name):
    assert p.shape[0] == 2

    def body(p_ref):
        s = _lb_soft(p_ref[0:1, :], p_ref[1:2, :])
        return (jnp.concatenate([jnp.zeros_like(s), s] + [jnp.zeros_like(s)] * (SUBLANES - 2), axis=0),)

    spec8 = pl.BlockSpec((SUBLANES, p.shape[1]), lambda: (0, 0))
    return tilecall(body, name, (), [(p, pl.BlockSpec(p.shape, lambda: (0, 0)))], [(_sds((SUBLANES, p.shape[1])), spec8, None)], ())[0]


def lower_bound_bwd(p, dlb1, name):
    def body(p_ref, d_ref):
        _, vjp = jax.vjp(_lb_soft, p_ref[0:1, :], p_ref[1:2, :])
        g0, g1 = vjp(d_ref[...])
        return (jnp.concatenate([g0, g1] + [jnp.zeros_like(g0)] * (SUBLANES - 2), axis=0),)

    spec8 = pl.BlockSpec((SUBLANES, p.shape[1]), lambda: (0, 0))
    return tilecall(body, name, (), [(p, pl.BlockSpec(p.shape, lambda: (0, 0))), (dlb1, pl.BlockSpec(dlb1.shape, lambda: (0, 0)))],
                    [(_sds((SUBLANES, p.shape[1])), spec8, None)], ())[0]


@jax.custom_vjp
def _swap_rope_halves(x):
    lane = lax.broadcasted_iota(jnp.int32, x.shape, 1)
    lo = (lane >= MLA_NOPE) & (lane < MLA_NOPE + MLA_ROPE // 2)
    hi = (lane >= MLA_NOPE + MLA_ROPE // 2) & (lane < MLA_QK)
    return jnp.where(lo, pltpu.roll(x, MLA_SLOT - MLA_ROPE // 2, 1), jnp.where(hi, pltpu.roll(x, MLA_ROPE // 2, 1), 0.0))


_swap_rope_halves.defvjp(lambda x: (_swap_rope_halves(x), None), lambda _, g: (_swap_rope_halves(g),))


def _norm_rope(x, gain, cos_t, sin_t):
    y = _rms(x, gain, MLA_QK)
    return y * cos_t + _swap_rope_halves(y) * sin_t


_ATTN_SCALE = MLA_QK ** -0.5


def _qk_heads(qs, kn, kr, qn, kn_gain, cos_t, sin_t):
    q = _norm_rope(qs, qn, cos_t, sin_t) * _ATTN_SCALE
    k = _norm_rope(jnp.concatenate([kn, kr], axis=1), kn_gain, cos_t, sin_t)
    return q, k


def _qk_specs(ts):
    slot = pl.BlockSpec((ts, MLA_SLOT), lambda i, h: (i, h))
    nope = pl.BlockSpec((ts, HEAD_DIM), lambda i, h: (i, 2 * h))
    shared = pl.BlockSpec((ts, HEAD_DIM), lambda i, h: (i, 0))
    gain = pl.BlockSpec((1, MLA_SLOT), lambda i, h: (0, 0))
    table = pl.BlockSpec((ts, MLA_SLOT), lambda i, h: (i, 0))
    return slot, nope, shared, gain, table


QK_ROWS = 1024


def qk_fwd(qslots, kv, krope, qn, kn, cos_t, sin_t, name):
    S = qslots.shape[0]
    ts = min(S, QK_ROWS)
    slot, nope, shared, gain, table = _qk_specs(ts)

    def body(q_ref, kn_ref, kr_ref, qn_ref, kg_ref, c_ref, s_ref):
        return _qk_heads(q_ref[...], kn_ref[...], kr_ref[...], qn_ref[...], kg_ref[...], c_ref[...], s_ref[...])

    out = _sds((S, HEADS * MLA_SLOT), _MXU)
    return tilecall(body, name, (S // ts, HEADS),
                    [(qslots, slot), (kv, nope), (krope, shared), (qn, gain), (kn, gain), (cos_t, table), (sin_t, table)],
                    [(out, slot, None), (out, slot, None)], ("parallel", "parallel"))


def qk_bwd(qslots, kv, krope, qn, kn, cos_t, sin_t, dq, dk, dv, name):
    S = qslots.shape[0]
    ts = min(S, QK_ROWS)
    slot, nope, shared, gain, table = _qk_specs(ts)
    vblk = pl.BlockSpec((ts, HEAD_DIM), lambda i, h: (i, h))

    def body(q_ref, kn_ref, kr_ref, qn_ref, kg_ref, c_ref, s_ref, dq_ref, dk_ref, dv_ref):
        c, s = c_ref[...], s_ref[...]
        _, vjp = jax.vjp(lambda a, b, r, g1, g2: _qk_heads(a, b, r, g1, g2, c, s),
                         q_ref[...], kn_ref[...], kr_ref[...], qn_ref[...], kg_ref[...])
        ga, gb, gr, g1, g2 = vjp((dq_ref[...], dk_ref[...]))
        return ga, jnp.concatenate([gb, dv_ref[...]], axis=1), gr, g1, g2

    first_head = lambda: pl.program_id(1) == 0
    first = lambda: (pl.program_id(0) == 0) & (pl.program_id(1) == 0)
    wide = _sds((S, HEADS * MLA_SLOT), _MXU)
    return tilecall(body, name, (S // ts, HEADS),
                    [(qslots, slot), (kv, nope), (krope, shared), (qn, gain), (kn, gain), (cos_t, table), (sin_t, table),
                     (dq, slot), (dk, slot), (dv, vblk)],
                    [(wide, slot, None), (wide, slot, None), (_sds((S, HEAD_DIM)), shared, first_head),
                     (_sds((1, MLA_SLOT)), gain, first), (_sds((1, MLA_SLOT)), gain, first)], ("arbitrary", "arbitrary"))


ATTN_TILE_FWD = 1024
ATTN_TILE = 1024
ATTN_HEADS_PER_STEP = 2
ATTN_HEADS_PER_STEP_BWD = 1


def _causal_pairs(nq, by_row):
    pairs = [(i, j) for i in range(nq) for j in range(i + 1)] if by_row else [(i, j) for j in range(nq) for i in range(j, nq)]
    return jnp.asarray([p[0] for p in pairs], jnp.int32), jnp.asarray([p[1] for p in pairs], jnp.int32)


def _diag_mask(s, tq):
    rows = lax.broadcasted_iota(jnp.int32, (tq, tq), 0)
    cols = lax.broadcasted_iota(jnp.int32, (tq, tq), 1)
    return jnp.where(rows >= cols, s, -jnp.inf)


def attn_fwd(qr, kr, kv, name):
    S = qr.shape[0]
    tq = min(S, ATTN_TILE_FWD)
    nq = S // tq
    i_tab, j_tab = _causal_pairs(nq, True)

    G = ATTN_HEADS_PER_STEP
    reps = tq // LANES

    def kern(it, jt, q_ref, k_ref, kv_ref, o_ref, lse_ref, m_s, l_s, acc):
        n = pl.program_id(1)
        i, j = it[n], jt[n]

        @pl.when(j == 0)
        def _():
            m_s[...] = jnp.full_like(m_s, -jnp.inf)
            l_s[...] = jnp.zeros_like(l_s)
            acc[...] = jnp.zeros_like(acc)

        def step(diagonal):
            for g in range(G):
                slot = slice(g * MLA_SLOT, (g + 1) * MLA_SLOT)
                s = _dg(q_ref[:, slot], k_ref[:, slot], _NT)
                if diagonal:
                    s = _diag_mask(s, tq)
                m_prev = m_s[g]
                m_new = jnp.maximum(m_prev, jnp.max(s, axis=-1, keepdims=True))
                alpha = jnp.exp(m_prev - m_new)
                p = jnp.exp(s - jnp.tile(m_new, (1, reps)))
                l_s[g] = alpha * l_s[g] + jnp.sum(p, axis=-1, keepdims=True)
                acc[g] = alpha * acc[g] + _dg(p, kv_ref[:, g * MLA_SLOT + HEAD_DIM:(g + 1) * MLA_SLOT], _NN)
                m_s[g] = m_new

        @pl.when(j < i)
        def _():
            step(False)

        @pl.when(j == i)
        def _():
            step(True)
            for g in range(G):
                l = l_s[g]
                lanes = slice(g * HEAD_DIM, (g + 1) * HEAD_DIM)
                o_ref[:, lanes] = acc[g] / l
                lse_ref[:, lanes] = m_s[g] + jnp.log(l)

    out = _sds((S, HEADS * HEAD_DIM))
    oblk = pl.BlockSpec((tq, G * HEAD_DIM), lambda h, n, it, jt: (it[n], h))
    stat = pltpu.VMEM((G, tq, HEAD_DIM), F32)
    return pl.pallas_call(
        kern, name=name,
        grid_spec=pltpu.PrefetchScalarGridSpec(
            num_scalar_prefetch=2, grid=(HEADS // G, i_tab.shape[0]),
            in_specs=[pl.BlockSpec((tq, G * MLA_SLOT), lambda h, n, it, jt: (it[n], h)),
                      pl.BlockSpec((tq, G * MLA_SLOT), lambda h, n, it, jt: (jt[n], h)),
                      pl.BlockSpec((tq, G * MLA_SLOT), lambda h, n, it, jt: (jt[n], h))],
            out_specs=[oblk, oblk], scratch_shapes=[stat, stat, stat]),
        out_shape=[out, out], compiler_params=_params(("parallel", "arbitrary")))(i_tab, j_tab, qr, kr, kv)


def attn_bwd(qr, kr, kv, o, lse, do, name):
    S = qr.shape[0]
    tq = min(S, ATTN_TILE)
    nq = S // tq
    i_tab, j_tab = _causal_pairs(nq, False)

    G = ATTN_HEADS_PER_STEP_BWD

    def kern(it, jt, q_ref, k_ref, kv_ref, o_ref, lse_ref, do_ref, dq_ref, dk_ref, dv_ref, dk_acc, dv_acc):
        n = pl.program_id(1)
        i, j = it[n], jt[n]

        @pl.when(n == 0)
        def _():
            dq_ref[...] = jnp.zeros_like(dq_ref)

        @pl.when(i == j)
        def _():
            dk_acc[...] = jnp.zeros_like(dk_acc)
            dv_acc[...] = jnp.zeros_like(dv_acc)

        def step(diagonal):
            rows = pl.ds(pl.multiple_of(i * tq, tq), tq)
            for g in range(G):
                slot = slice(g * MLA_SLOT, (g + 1) * MLA_SLOT)
                lanes = slice(g * HEAD_DIM, (g + 1) * HEAD_DIM)
                q, k = q_ref[:, slot], k_ref[:, slot]
                s = _dg(q, k, _NT) - jnp.tile(lse_ref[:, lanes], (1, tq // LANES))
                if diagonal:
                    s = _diag_mask(s, tq)
                p = jnp.exp(s)
                d = do_ref[:, lanes]
                delta = jnp.sum(d * o_ref[:, lanes], axis=-1, keepdims=True)
                dv_acc[:, lanes] += _dg(p, d, _TN)
                ds = p * (_dg(d, kv_ref[:, g * MLA_SLOT + HEAD_DIM:(g + 1) * MLA_SLOT], _NT) - delta)
                dk_acc[:, slot] += _dg(ds, q, _TN)
                dq_ref[rows, slot] += _dg(ds, k, _NN)

        @pl.when(i > j)
        def _():
            step(False)

        @pl.when(i == j)
        def _():
            step(True)

        @pl.when(i == nq - 1)
        def _():
            dk_ref[...] = dk_acc[...]
            dv_ref[...] = dv_acc[...]

    qblk = pl.BlockSpec((tq, G * MLA_SLOT), lambda h, n, it, jt: (it[n], h))
    oblk = pl.BlockSpec((tq, G * HEAD_DIM), lambda h, n, it, jt: (it[n], h))
    kblk = pl.BlockSpec((tq, G * MLA_SLOT), lambda h, n, it, jt: (jt[n], h))
    return pl.pallas_call(
        kern, name=name,
        grid_spec=pltpu.PrefetchScalarGridSpec(
            num_scalar_prefetch=2, grid=(HEADS // G, i_tab.shape[0]),
            in_specs=[qblk, kblk, kblk, oblk, oblk, oblk],
            out_specs=[pl.BlockSpec((S, G * MLA_SLOT), lambda h, n, it, jt: (0, h)), kblk,
                       pl.BlockSpec((tq, G * HEAD_DIM), lambda h, n, it, jt: (jt[n], h))],
            scratch_shapes=[pltpu.VMEM((tq, G * MLA_SLOT), F32), pltpu.VMEM((tq, G * HEAD_DIM), F32)]),
        out_shape=[_sds((S, HEADS * MLA_SLOT)), _sds((S, HEADS * MLA_SLOT)), _sds((S, HEADS * HEAD_DIM))],
        compiler_params=_params(("parallel", "arbitrary")))(i_tab, j_tab, qr, kr, kv, o, lse, do)


def loss_head(y, target, name):
    S, Dm = y.shape
    ts = _row_tile(S, Dm)

    def body(y_ref, t_ref):
        e = y_ref[...] - t_ref[...]
        tot = jnp.sum(jnp.sum(e * e, axis=-1, keepdims=True) / Dm, axis=0, keepdims=True)
        return e / Dm, jnp.broadcast_to(0.5 * tot, (SUBLANES, LANES))

    row = pl.BlockSpec((ts, Dm), lambda i: (i, 0))
    return tilecall(body, name, (S // ts,), [(y, row), (target, row)],
                    [(_sds((S, Dm)), row, None),
                     (_sds((SUBLANES, LANES)), pl.BlockSpec((SUBLANES, LANES), lambda i: (0, 0)), lambda: pl.program_id(0) == 0)],
                    ("arbitrary",))


_CHIP_FLIPS = ((1, 0), (0, 1), (1, 1))
_PEER_FLIPS = {
    "chips": ((1, 0, 0), (0, 1, 0), (1, 1, 0)),
    "sibling": ((0, 0, 1),),
    "all": tuple((a, b, c) for a in (0, 1) for b in (0, 1) for c in (0, 1))[1:],
}
_SLOT_WEIGHTS = {"chips": (2, 1, 0), "sibling": (0, 0, 1), "all": (4, 2, 1)}
_HBM = pl.BlockSpec(memory_space=pltpu.HBM)


def _me():
    return lax.axis_index("x"), lax.axis_index("y"), lax.axis_index("c")


def _remote(src, dst, send_sem, recv_sem, peer):
    return pltpu.make_async_remote_copy(src_ref=src, dst_ref=dst, send_sem=send_sem, recv_sem=recv_sem,
                                        device_id=peer, device_id_type=pl.DeviceIdType.MESH)


def exchange(arrs, group, slab_weights, name, keep_own=True):
    flips = _PEER_FLIPS[group]
    wx, wy, wc = _SLOT_WEIGHTS[group]
    n_slots = len(flips) + (1 if keep_own else 0)
    n = len(arrs)

    def slab(ref, a, pos):
        w = slab_weights[a]
        return ref if w is None else ref.at[w[0] * pos[0] + w[1] * pos[1] + w[2] * pos[2]]

    def kern(*refs):
        srcs, outs = refs[:n], refs[n:2 * n]
        send_sems, recv_sems = refs[2 * n:2 * n + 2]
        me = _me()
        my_slot = wx * me[0] + wy * me[1] + wc * me[2]
        copies = []
        if keep_own:
            local_sems = refs[2 * n + 2]
            for a in range(n):
                cp = pltpu.make_async_copy(slab(srcs[a], a, me), outs[a].at[my_slot], local_sems.at[a])
                cp.start()
                copies.append(cp)
        for f, flip in enumerate(flips):
            peer = tuple(m ^ b if b else m for m, b in zip(me, flip))
            for a in range(n):
                cp = _remote(slab(srcs[a], a, peer), outs[a].at[my_slot if keep_own else f],
                             send_sems.at[f, a], recv_sems.at[f, a], peer)
                cp.start()
                copies.append(cp)
        for cp in copies:
            cp.wait()

    out_shape = [_sds((n_slots,) + (a.shape if slab_weights[k] is None else a.shape[1:]), a.dtype) for k, a in enumerate(arrs)]
    sems = [pltpu.SemaphoreType.DMA((len(flips), n)), pltpu.SemaphoreType.DMA((len(flips), n))]
    return pl.pallas_call(
        kern, name=name, in_specs=[_HBM] * n, out_specs=[_HBM] * n, out_shape=out_shape,
        scratch_shapes=sems + ([pltpu.SemaphoreType.DMA((n,))] if keep_own else []))(*arrs)


def _chip_window(ref, kind, size, chip, layers):
    if kind == "rows":
        return ref.at[layers, pl.ds(chip * size, size), :]
    return ref.at[layers, :, pl.ds(pl.multiple_of(chip * size, LANES), size)]


def gather_big(shards, kinds, name):
    n = len(shards)
    fulls = []
    for s, kind in zip(shards, kinds):
        L, r, c = s.shape
        fulls.append(_sds((L, N_CHIPS * r, c) if kind == "rows" else (L, r, N_CHIPS * c), s.dtype))

    def kern(*refs):
        srcs, outs = refs[:n], refs[n:2 * n]
        ici_s, ici_r, relay_s, relay_r, d2d_s, d2d_r = refs[2 * n:]
        x, y, c = _me()
        sibling = (x, y, 1 - c)
        nbrs = ((x ^ 1, y), (x, y ^ 1))
        relay_from = (x ^ (1 - c), y ^ c)
        relay_to = (x ^ c, y ^ (1 - c), c)
        diagonal = (x ^ 1, y ^ 1)

        def window(a, chip):
            L, r, cc = shards[a].shape
            size = r if kinds[a] == "rows" else cc
            return _chip_window(outs[a], kinds[a], size, 2 * chip[0] + chip[1], pl.ds(c * (L // 2), L // 2))

        def direct(a, f):
            L = shards[a].shape[0]
            return _remote(srcs[a].at[pl.ds(c * (L // 2), L // 2)], window(a, (x, y)), ici_s.at[a, f], ici_r.at[a, f],
                           (nbrs[f][0], nbrs[f][1], c))

        def to_sibling(a, k, chip):
            return _remote(window(a, chip), window(a, chip), d2d_s.at[a, k], d2d_r.at[a, k], sibling)

        sends, relays, passed = [], [], []
        for a in range(n):
            for f in range(2):
                cp = direct(a, f)
                cp.start()
                sends.append(cp)
        for a in range(n):
            _remote(window(a, relay_from), window(a, relay_from), ici_s.at[a, c], ici_r.at[a, c], relay_to).wait_recv()
            cp = _remote(window(a, relay_from), window(a, relay_from), relay_s.at[a], relay_r.at[a], relay_to)
            cp.start()
            relays.append(cp)
            passed.append(to_sibling(a, 0, relay_from))
            passed[-1].start()
            _remote(window(a, relay_from), window(a, relay_from), ici_s.at[a, 1 - c], ici_r.at[a, 1 - c], relay_to).wait_recv()
            passed.append(to_sibling(a, 1, (relay_to[0], relay_to[1])))
            passed[-1].start()
        for a in range(n):
            relays[a].wait_recv()
            passed.append(to_sibling(a, 2, diagonal))
            passed[-1].start()
        for cp in sends + relays:
            cp.wait_send()
        for cp in passed:
            cp.wait()

    pair = pltpu.SemaphoreType.DMA((n, 2))
    one = pltpu.SemaphoreType.DMA((n,))
    three = pltpu.SemaphoreType.DMA((n, 3))
    return pl.pallas_call(kern, name=name, in_specs=[_HBM] * n, out_specs=[_HBM] * n, out_shape=fulls,
                          scratch_shapes=[pair, pair, one, one, three, three])(*shards)


def send_other_half(arrs, name):
    n = len(arrs)

    def kern(*refs):
        srcs, outs = refs[:n], refs[n:2 * n]
        send_sems, recv_sems = refs[2 * n:]
        x, y, c = _me()
        copies = []
        for a in range(n):
            hl = arrs[a].shape[0] // 2
            cp = _remote(srcs[a].at[pl.ds((1 - c) * hl, hl)], outs[a], send_sems.at[a], recv_sems.at[a], (x, y, 1 - c))
            cp.start()
            copies.append(cp)
        for cp in copies:
            cp.wait()

    return pl.pallas_call(
        kern, name=name, in_specs=[_HBM] * n, out_specs=[_HBM] * n,
        out_shape=[_sds((a.shape[0] // 2,) + a.shape[1:], a.dtype) for a in arrs],
        scratch_shapes=[pltpu.SemaphoreType.DMA((n,)), pltpu.SemaphoreType.DMA((n,))])(*arrs)


def scatter_to_chips(arrs, kinds, name):
    n = len(arrs)
    shapes = []
    for a, kind in zip(arrs, kinds):
        l, R, C = a.shape
        shapes.append((l, R // N_CHIPS, C) if kind == "rows" else (l, R, C // N_CHIPS))

    def kern(*refs):
        srcs, outs = refs[:n], refs[n:2 * n]
        send_sems, recv_sems = refs[2 * n:]
        x, y, c = _me()
        copies = []
        for a in range(n):
            size = shapes[a][1] if kinds[a] == "rows" else shapes[a][2]
            for f, (fx, fy) in enumerate(_CHIP_FLIPS):
                window = _chip_window(srcs[a], kinds[a], size, 2 * (x ^ fx) + (y ^ fy), slice(None))
                cp = _remote(window, outs[a].at[f], send_sems.at[a, f], recv_sems.at[a, f], (x ^ fx, y ^ fy, c))
                cp.start()
                copies.append(cp)
        for cp in copies:
            cp.wait()

    sem = pltpu.SemaphoreType.DMA((n, len(_CHIP_FLIPS)))
    return pl.pallas_call(
        kern, name=name, in_specs=[_HBM] * n, out_specs=[_HBM] * n,
        out_shape=[_sds((len(_CHIP_FLIPS),) + s, a.dtype) for s, a in zip(shapes, arrs)],
        scratch_shapes=[sem, sem])(*arrs)


def _stack_tile(r, c):
    for t in (1024, 704, 512, 352, 256, 128, 64, 32, 16):
        if r % t == 0 and t * c * 4 <= 3 * 512 * 1024:
            return t
    return r


def _window_map(kind, r, tr):
    nrt = r // tr
    if kind == "rows":
        return lambda l, i, chip: (l, chip[0] * nrt + i, 0)
    return lambda l, i, chip: (l, i, chip[0])


def place(full, shard, kind, chip, name):
    L, r, c = shard.shape
    tr = _stack_tile(r, c)
    wmap = _window_map(kind, r, tr)
    return tilecall(lambda chip_ref, s_ref, f_ref: (s_ref[...],), name, (L, r // tr),
                    [(shard, pl.BlockSpec((None, tr, c), lambda l, i, chip: (l, i, 0))), (full, pl.BlockSpec(memory_space=pl.ANY))],
                    [(_sds(full.shape, full.dtype), pl.BlockSpec((None, tr, c), lambda l, i, chip: wmap(l, i, chip)), None)],
                    ("parallel", "parallel"), prefetch=(chip,), aliases={1: 0})[0]


def add_cores(g, other, core, name):
    L, R, C = g.shape
    hl = L // 2
    tr = _stack_tile(R, C)
    blk = (None, tr, C)
    return tilecall(lambda core_ref, a_ref, b_ref: (a_ref[...] + b_ref[...],), name, (hl, R // tr),
                    [(g, pl.BlockSpec(blk, lambda l, i, core: (core[0] * hl + l, i, 0))),
                     (other, pl.BlockSpec(blk, lambda l, i, core: (l, i, 0)))],
                    [(_sds((hl, R, C), _MXU), pl.BlockSpec(blk, lambda l, i, core: (l, i, 0)), None)],
                    ("parallel", "parallel"), prefetch=(core,))[0]


def add_chips(own, got, kind, chip, name):
    nf, l, r, c = got.shape
    tr = _stack_tile(r, c)
    wmap = _window_map(kind, r, tr)

    def body(chip_ref, own_ref, *got_refs):
        acc = own_ref[...].astype(F32)
        for ref in got_refs:
            acc = acc + ref[...].astype(F32)
        return (acc,)

    return tilecall(body, name, (l, r // tr),
                    [(own, pl.BlockSpec((None, tr, c), lambda ll, i, chip: wmap(ll, i, chip)))] +
                    [(got, pl.BlockSpec((None, None, tr, c), lambda ll, i, chip, f=f: (f, ll, i, 0))) for f in range(nf)],
                    [(_sds((l, r, c)), pl.BlockSpec((None, tr, c), lambda ll, i, chip: (ll, i, 0)), None)],
                    ("parallel", "parallel"), prefetch=(chip,))[0]


def _adam_update(g, w, m, v):
    m_new = ADAM_B1 * m + (1.0 - ADAM_B1) * g
    v_new = ADAM_B2 * v + (1.0 - ADAM_B2) * jnp.square(g)
    m_hat = m_new / (1.0 - ADAM_B1 ** ADAM_STEP)
    v_hat = v_new / (1.0 - ADAM_B2 ** ADAM_STEP)
    delta = -ADAM_LR * (m_hat / (jnp.sqrt(v_hat) + ADAM_EPS) + ADAM_WD * w)
    return g, delta, m_new, v_new


def adamw_stacked(mine, theirs, w, m, v, core, name):
    L, r, c = w.shape
    hl = L // 2
    tr = _stack_tile(r, c)

    def body(core_ref, a_ref, b_ref, w_ref, m_ref, v_ref):
        is_mine = (pl.program_id(0) // hl) == core_ref[0]
        g = jnp.where(is_mine, a_ref[...], b_ref[...])
        return _adam_update(g, w_ref[...], m_ref[...], v_ref[...])

    full = pl.BlockSpec((None, tr, c), lambda l, i, core: (l, i, 0))
    out = (_sds((L, r, c)), full, None)
    return tilecall(body, name, (L, r // tr),
                    [(mine, pl.BlockSpec((None, tr, c), lambda l, i, core: (l % hl, i, 0))),
                     (theirs, pl.BlockSpec((None, None, tr, c), lambda l, i, core: (0, l % hl, i, 0))),
                     (w, full), (m, full), (v, full)],
                    [out, out, out, out], ("parallel", "parallel"), prefetch=(core,))


def _pack(arrs, dtype, row_multiple):
    flat = jnp.concatenate([a.reshape(-1).astype(dtype) for a in arrs])
    rows = -(-flat.shape[0] // LANES)
    rows = -(-rows // row_multiple) * row_multiple
    return jnp.pad(flat, (0, rows * LANES - flat.shape[0])).reshape(rows, LANES)


def _unpack(buf, shapes):
    flat = buf.reshape(-1)
    out, off = [], 0
    for s in shapes:
        n = int(np.prod(s))
        out.append(flat[off:off + n].reshape(s))
        off += n
    return out


def adamw_packed(gparts, w, m, v, name):
    P, R, _ = gparts.shape

    def body(g_ref, w_ref, m_ref, v_ref):
        g = g_ref[0]
        for p in range(1, P):
            g = g + g_ref[p]
        return _adam_update(g, w_ref[...], m_ref[...], v_ref[...])

    whole = pl.BlockSpec((R, LANES), lambda: (0, 0))
    out = (_sds((R, LANES)), whole, None)
    return tilecall(body, name, (), [(gparts, pl.BlockSpec((P, R, LANES), lambda: (0, 0, 0))), (w, whole), (m, whole), (v, whole)],
                    [out, out, out, out], ())


def _residual_out(a, w, x, next_gain):
    if next_gain is None:
        return mm(a, w, "nn", "mm_nn_add", add=x), None
    return mm(a, w, "nn", "mm_nn_add_rms", add=x, rms_gain=next_gain)


def _input_grad(pieces, x, gain, res):
    dh = None
    for d, w in pieces[:-1]:
        dh = mm(d, w, "nt", "mm_nt" if dh is None else "mm_nt_add", add=dh)
    d, w = pieces[-1]
    return mm(d, w, "nt", "mm_nt_norm_bwd", add=dh, norm_bwd=(x, gain, res))


def _ffn_fwd(x, h, next_gain, layer, gain, wts, cw8, cb):
    u = mm(h, win(wts["ffn_w_up"], layer), "nn", "mm_nn")
    a = ffn_act_fwd(u, cw8, cb, "ffn_act_fwd")
    y, h_next = _residual_out(a, win(wts["ffn_w_down"], layer), x, next_gain)
    return y, h_next, (x, h, u, a)


def _ffn_bwd(dy, saved, layer, gain, wts, cw8, cb, grads):
    x, h, u, a = saved
    grads["ffn_w_down"] = mm(a, dy, "tn", "mm_tn_into", into=win(grads["ffn_w_down"], layer))
    da = mm(dy, win(wts["ffn_w_down"], layer), "nt", "mm_nt")
    dyg, dyu, dcw_g, dcw_u, dcb_g, dcb_u = ffn_act_bwd(u, cw8, cb, da, "ffn_act_bwd")
    pieces = []
    for half, du in enumerate(ffn_conv_bwd(dyg, dyu, cw8, "ffn_conv_bwd")):
        cols = dict(col_off=half * D_FF, cols=D_FF)
        grads["ffn_w_up"] = mm(h, du, "tn", "mm_tn_into", into=win(grads["ffn_w_up"], layer, **cols))
        pieces.append((du, win(wts["ffn_w_up"], layer, **cols)))
    dx, d_gain = _input_grad(pieces, x, gain, dy)
    return dx, dict(gain=d_gain, conv_w=jnp.concatenate([dcw_g[0:3], dcw_u[0:3]], axis=1),
                    conv_b=jnp.concatenate([dcb_g, dcb_u], axis=1))


def _hgrn_w_in(wts, j, k):
    return win(wts["hgrn_w_in"], j, row_off=k * D_MODEL, rows=D_MODEL)


def _hgrn_layer_fwd(x, h, next_gain, j, gain, wts, lb, out_gain, mall):
    z = [mm(h, _hgrn_w_in(wts, j, k), "nn", "mm_nn") for k in range(4)]
    o, on, states = hgrn_fwd(z[0], z[1], z[2], z[3], lb, out_gain, mall, "hgrn_fwd")
    y, h_next = _residual_out(on, win(wts["hgrn_w_out"], j), x, next_gain)
    return y, h_next, (x, h, z, o, states, on)


def _hgrn_layer_bwd(dy, saved, j, gain, wts, lb, out_gain, mall, grads):
    x, h, z, o, states, on = saved
    grads["hgrn_w_out"] = mm(on, dy, "tn", "mm_tn_into", into=win(grads["hgrn_w_out"], j))
    don = mm(dy, win(wts["hgrn_w_out"], j), "nt", "mm_nt")
    dzq, dzf, dzi, dzg, dlb, d_out_gain = hgrn_bwd(z[0], z[1], z[2], z[3], lb, out_gain, mall, states, o, don, "hgrn_bwd")
    dz = [dzq, dzf, dzi, dzg]
    for k, d in enumerate(dz):
        grads["hgrn_w_in"] = mm(h, d, "tn", "mm_tn_into", into=win(grads["hgrn_w_in"], j, row_off=k * D_MODEL, rows=D_MODEL))
    dx, d_gain = _input_grad([(d, _hgrn_w_in(wts, j, k)) for k, d in enumerate(dz)], x, gain, dy)
    return dx, dict(gain=d_gain, lb=dlb, out_gain=d_out_gain)


_MLA_IN_WINDOWS = ((0, MLA_LORA), (MLA_LORA, MLA_LORA), (2 * MLA_LORA, HEAD_DIM))


def _mla_layer_fwd(x, h, next_gain, j, gain, wts, qa_gain, kva_gain, qn, kn, cos_t, sin_t):
    (c0, n), (c1, n1), (c2, n2) = _MLA_IN_WINDOWS
    cq, cqn = mm(h, win(wts["mla_w_in"], j, col_off=c0, cols=n), "nn", "mm_nn_rms", rms_gain=qa_gain)
    ckv, ckvn = mm(h, win(wts["mla_w_in"], j, col_off=c1, cols=n1), "nn", "mm_nn_rms", rms_gain=kva_gain)
    kr = mm(h, win(wts["mla_w_in"], j, col_off=c2, cols=n2), "nn", "mm_nn")
    qslots = mm(cqn, win(wts["mla_w_q_up"], j), "nn", "mm_nn")
    kv = mm(ckvn, win(wts["mla_w_kv_up"], j), "nn", "mm_nn")
    qr, krot = qk_fwd(qslots, kv, kr, qn, kn, cos_t, sin_t, "qk_fwd")
    o, lse = attn_fwd(qr, krot, kv, "attn_fwd")
    y, h_next = _residual_out(o, win(wts["mla_w_out"], j), x, next_gain)
    return y, h_next, (x, h, cq, ckv, kr, cqn, ckvn, qslots, kv, qr, krot, o, lse)


def _mla_layer_bwd(dy, saved, j, gain, wts, qa_gain, kva_gain, qn, kn, cos_t, sin_t, grads):
    x, h, cq, ckv, kr, cqn, ckvn, qslots, kv, qr, krot, o, lse = saved
    grads["mla_w_out"] = mm(o, dy, "tn", "mm_tn_into", into=win(grads["mla_w_out"], j))
    do = mm(dy, win(wts["mla_w_out"], j), "nt", "mm_nt")
    dq, dk, dv = attn_bwd(qr, krot, kv, o, lse, do, "attn_bwd")
    dqslots, dkv, dkr, d_qn, d_kn = qk_bwd(qslots, kv, kr, qn, kn, cos_t, sin_t, dq, dk, dv, "qk_bwd")
    grads["mla_w_q_up"] = mm(cqn, dqslots, "tn", "mm_tn_into", into=win(grads["mla_w_q_up"], j))
    dcq, d_qa = _input_grad([(dqslots, win(wts["mla_w_q_up"], j))], cq, qa_gain, None)
    grads["mla_w_kv_up"] = mm(ckvn, dkv, "tn", "mm_tn_into", into=win(grads["mla_w_kv_up"], j))
    dckv, d_kva = _input_grad([(dkv, win(wts["mla_w_kv_up"], j))], ckv, kva_gain, None)
    pieces = []
    for d, (c0, n) in zip((dcq, dckv, dkr), _MLA_IN_WINDOWS):
        grads["mla_w_in"] = mm(h, d, "tn", "mm_tn_into", into=win(grads["mla_w_in"], j, col_off=c0, cols=n))
        pieces.append((d, win(wts["mla_w_in"], j, col_off=c0, cols=n)))
    dx, d_gain = _input_grad(pieces, x, gain, dy)
    return dx, dict(gain=d_gain, qa=d_qa, kva=d_kva, qn=d_qn, kn=d_kn)


BIG = (("hgrn_w_in", "rows"), ("hgrn_w_out", "rows"), ("mla_w_in", "rows"), ("mla_w_q_up", "cols"),
       ("mla_w_kv_up", "cols"), ("mla_w_out", "rows"), ("ffn_w_up", "cols"), ("ffn_w_down", "rows"))
SMALL_SHARDED = (("mla_q_a_norm", 1), ("mla_kv_a_norm", 1), ("ffn_conv_w", 2))
REPLICATED = ("norm_mix", "norm_ffn", "hgrn_lower_bounds", "hgrn_out_norm", "mla_q_norm", "mla_k_norm", "ffn_conv_b")
WEIGHTS = ("norm_mix", "norm_ffn", "hgrn_w_in", "hgrn_lower_bounds", "hgrn_out_norm", "hgrn_w_out", "mla_w_in",
           "mla_q_a_norm", "mla_w_q_up", "mla_kv_a_norm", "mla_w_kv_up", "mla_q_norm", "mla_k_norm", "mla_w_out",
           "ffn_w_up", "ffn_conv_w", "ffn_conv_b", "ffn_w_down")


def _pad_cols(a, width):
    return jnp.pad(a, [(0, 0)] * (a.ndim - 1) + [(0, width - a.shape[-1])])


def _head_slots(w):
    lead, n = w.shape[:-1], w.shape[-1] // MLA_QK
    return _pad_cols(w.reshape(lead + (n, MLA_QK)), MLA_SLOT).reshape(lead + (n * MLA_SLOT,))


def _head_unslots(w):
    lead, n = w.shape[:-1], w.shape[-1] // MLA_SLOT
    return w.reshape(lead + (n, MLA_SLOT))[..., :MLA_QK].reshape(lead + (n * MLA_QK,))


def _to_stack_layout(name, a):
    if name == "hgrn_w_in":
        return a
    if name == "mla_w_in":
        return _pad_cols(a, MLA_IN_COLS)
    if name == "mla_w_q_up":
        return _head_slots(a)
    return a


def _from_stack_layout(name, a):
    if name == "mla_w_in":
        return a[..., :2 * MLA_LORA + MLA_ROPE]
    if name == "mla_w_q_up":
        return _head_unslots(a)
    return a


def _rope_tables(positions):
    inv_freq = ROPE_THETA ** (-jnp.arange(0, MLA_ROPE, 2, dtype=F32) / MLA_ROPE)
    ang = positions.astype(F32)[:, None] * inv_freq
    cos, sin = jnp.cos(ang), jnp.sin(ang)
    S = positions.shape[0]
    ones, zeros = jnp.ones((S, MLA_NOPE), F32), jnp.zeros((S, MLA_SLOT - MLA_QK), F32)
    return (jnp.concatenate([ones, cos, cos, zeros], axis=1),
            jnp.concatenate([jnp.zeros((S, MLA_NOPE), F32), -sin, sin, zeros], axis=1))


def kernel(x, positions, norm_mix, norm_ffn, hgrn_w_in, hgrn_lower_bounds, hgrn_out_norm, hgrn_w_out, mla_w_in, mla_q_a_norm, mla_w_q_up, mla_kv_a_norm, mla_w_kv_up, mla_q_norm, mla_k_norm, mla_w_out, ffn_w_up, ffn_conv_w, ffn_conv_b, ffn_w_down, loss_target, m_norm_mix, m_norm_ffn, m_hgrn_w_in, m_hgrn_lower_bounds, m_hgrn_out_norm, m_hgrn_w_out, m_mla_w_in, m_mla_q_a_norm, m_mla_w_q_up, m_mla_kv_a_norm, m_mla_w_kv_up, m_mla_q_norm, m_mla_k_norm, m_mla_w_out, m_ffn_w_up, m_ffn_conv_w, m_ffn_conv_b, m_ffn_w_down, v_norm_mix, v_norm_ffn, v_hgrn_w_in, v_hgrn_lower_bounds, v_hgrn_out_norm, v_hgrn_w_out, v_mla_w_in, v_mla_q_a_norm, v_mla_w_q_up, v_mla_kv_a_norm, v_mla_w_kv_up, v_mla_q_norm, v_mla_k_norm, v_mla_w_out, v_ffn_w_up, v_ffn_conv_w, v_ffn_conv_b, v_ffn_w_down):
    args = dict(locals())
    w = {n: args[n] for n in WEIGHTS}
    m = {n: args["m_" + n] for n in WEIGHTS}
    v = {n: args["v_" + n] for n in WEIGHTS}
    depth = norm_mix.shape[0]
    x0 = x[0]
    S = x0.shape[0]
    chip = (2 * lax.axis_index("x") + lax.axis_index("y")).astype(jnp.int32).reshape(1)
    core = lax.axis_index("c").astype(jnp.int32).reshape(1)
    big_names = [n for n, _ in BIG]
    kinds = [k for _, k in BIG]
    small_names = [n for n, _ in SMALL_SHARDED]
    small_axis = dict(SMALL_SHARDED)

    local = {n: _to_stack_layout(n, w[n]) for n in big_names}
    gathered = gather_big([local[n].astype(_MXU) for n in big_names], kinds, "gather_weights")
    wts = {n: place(g, local[n], k, chip, "place") for n, k, g in zip(big_names, kinds, gathered)}
    (got_small,) = exchange([_pack([w[n] for n in small_names], F32, SUBLANES)], "chips", [None], "gather_small")
    per_chip = [_unpack(got_small[p], [w[n].shape for n in small_names]) for p in range(N_CHIPS)]
    small = {n: jnp.concatenate([per_chip[p][k] for p in range(N_CHIPS)], axis=small_axis[n]) for k, n in enumerate(small_names)}

    cos_t, sin_t = _rope_tables(positions[0])
    lbs = lower_bound_fwd(hgrn_lower_bounds, "lower_bound_fwd")
    mall = jnp.asarray(_hgrn_sum_matrix(min(HGRN_CHUNK, S)), _MXU)
    qn = _pad_cols(mla_q_norm, MLA_SLOT)
    kn = _pad_cols(mla_k_norm, MLA_SLOT)
    cw8 = jnp.pad(small["ffn_conv_w"], ((0, 0), (0, SUBLANES - 3), (0, 0)))

    def mixer_args(layer):
        j = layer // 2
        if layer % 2 == 0:
            return (j, norm_mix[layer:layer + 1], wts, lbs[j:j + 1], hgrn_out_norm[j:j + 1], mall)
        return (j, norm_mix[layer:layer + 1], wts, small["mla_q_a_norm"][j:j + 1], small["mla_kv_a_norm"][j:j + 1],
                qn[j:j + 1], kn[j:j + 1], cos_t, sin_t)

    def ffn_args(layer):
        return (layer, norm_ffn[layer:layer + 1], wts, cw8[layer], ffn_conv_b[layer:layer + 1])

    xc, h = x0, rms_fwd(x0, norm_mix[0:1], "rms_fwd")
    saved = []
    for layer in range(depth):
        fwd = _hgrn_layer_fwd if layer % 2 == 0 else _mla_layer_fwd
        xc, h, s_mix = fwd(xc, h, norm_ffn[layer:layer + 1], *mixer_args(layer))
        xc, h, s_ffn = _ffn_fwd(xc, h, norm_mix[layer + 1:layer + 2] if layer + 1 < depth else None, *ffn_args(layer))
        saved.append((s_mix, s_ffn))

    dh, loss_blk = loss_head(xc, loss_target[0], "loss_head")
    loss = lax.psum(loss_blk[0, 0], MESH_AXES)

    grads = {n: lax.empty(g.shape, F32) for n, g in zip(big_names, gathered)}
    g_mix, g_ffn = [None] * depth, [None] * depth
    for layer in reversed(range(depth)):
        s_mix, s_ffn = saved[layer]
        dh, g_ffn[layer] = _ffn_bwd(dh, s_ffn, *ffn_args(layer), grads)
        bwd = _hgrn_layer_bwd if layer % 2 == 0 else _mla_layer_bwd
        dh, g_mix[layer] = bwd(dh, s_mix, *mixer_args(layer), grads)
    hg = [g_mix[l] for l in range(0, depth, 2)]
    mg = [g_mix[l] for l in range(1, depth, 2)]
    d_p = lower_bound_bwd(hgrn_lower_bounds, hg[1]["lb"], "lower_bound_bwd")

    g_list = [grads[n] for n in big_names]
    from_core = send_other_half(g_list, "reduce_cores_in")
    chip_sums = [add_cores(g, o, core, "add_cores") for g, o in zip(g_list, from_core)]
    from_chips = scatter_to_chips(chip_sums, kinds, "reduce_chips")
    reduced = [add_chips(own, got, k, chip, "add_chips") for own, got, k in zip(chip_sums, from_chips, kinds)]
    other_half = exchange(reduced, "sibling", [None] * len(reduced), "reduce_cores_out", keep_own=False)
    big_out = {}
    for n, mine, theirs in zip(big_names, reduced, other_half):
        outs = adamw_stacked(mine, theirs, local[n], _to_stack_layout(n, m[n]), _to_stack_layout(n, v[n]), core, "adamw")
        big_out[n] = [_from_stack_layout(n, o) for o in outs]

    small_grads = {
        "norm_mix": jnp.concatenate([g["gain"] for g in g_mix], axis=0),
        "norm_ffn": jnp.concatenate([g["gain"] for g in g_ffn], axis=0),
        "hgrn_lower_bounds": d_p[0:2],
        "hgrn_out_norm": jnp.concatenate([g["out_gain"] for g in hg], axis=0),
        "mla_q_a_norm": jnp.concatenate([g["qa"] for g in mg], axis=0),
        "mla_kv_a_norm": jnp.concatenate([g["kva"] for g in mg], axis=0),
        "mla_q_norm": jnp.concatenate([g["qn"][:, :MLA_QK] for g in mg], axis=0),
        "mla_k_norm": jnp.concatenate([g["kn"][:, :MLA_QK] for g in mg], axis=0),
        "ffn_conv_w": jnp.stack([g["conv_w"] for g in g_ffn]),
        "ffn_conv_b": jnp.concatenate([g["conv_b"] for g in g_ffn], axis=0),
    }

    def chip_part(n, p):
        size = w[n].shape[small_axis[n]]
        return lax.slice_in_dim(small_grads[n], p * size, (p + 1) * size, axis=small_axis[n])

    to_chips = jnp.stack([_pack([chip_part(n, p) for n in small_names], F32, SUBLANES) for p in range(N_CHIPS)])
    rep_g, shard_g = exchange([_pack([small_grads[n] for n in REPLICATED], F32, SUBLANES), to_chips], "all",
                              [None, (2, 1, 0)], "reduce_small")
    small_out = {}
    for names, gparts in ((REPLICATED, rep_g), (small_names, shard_g)):
        packed = adamw_packed(gparts, *[_pack([t[n] for n in names], F32, SUBLANES) for t in (w, m, v)], "adamw_small")
        unpacked = [_unpack(buf, [w[n].shape for n in names]) for buf in packed]
        for k, n in enumerate(names):
            small_out[n] = [u[k] for u in unpacked]

    result = [loss, dh[None]]
    for k in range(4):
        result += [(big_out[n] if n in big_out else small_out[n])[k] for n in WEIGHTS]
    return tuple(result)
```

```python
import numpy as np
import jax
import jax.numpy as jnp
from jax import lax
from jax.experimental import pallas as pl
from jax.experimental.pallas import tpu as pltpu

F32 = jnp.float32
BF16 = jnp.bfloat16
_MXU = BF16

RMS_EPS = 1e-6
D_MODEL = 1024
HEADS = 8
HEAD_DIM = 128
HGRN_CHUNK = 128
MLA_NOPE = 128
MLA_ROPE = 64
MLA_QK = MLA_NOPE + MLA_ROPE
MLA_SLOT = 256
MLA_LORA = 256
MLA_IN_COLS = 2 * MLA_LORA + HEAD_DIM
ROPE_THETA = 10000.0
D_FF = 2816
FF_BLOCK = 1408
LANES = 128
SUBLANES = 8

ADAM_LR = 0.001
ADAM_B1 = 0.9
ADAM_B2 = 0.999
ADAM_EPS = 1e-08
ADAM_WD = 0.01
ADAM_STEP = 10

VMEM_LIMIT = 56 * 1024 * 1024
MM_VMEM_BUDGET = 46 * 1024 * 1024
MESH_AXES = ("x", "y", "c")
N_CHIPS = 4

_NN = ((1,), (0,))
_NT = ((1,), (1,))
_TN = ((0,), (0,))


def _dg(a, b, dims):
    return lax.dot_general(a.astype(_MXU), b.astype(_MXU), (dims, ((), ())), preferred_element_type=F32)


@jax.custom_vjp
def kdot(a, b):
    return _dg(a, b, _NN)


kdot.defvjp(lambda a, b: (_dg(a, b, _NN), (a, b)), lambda r, g: (_dg(g, r[1], _NT), _dg(r[0], g, _TN)))


@jax.custom_vjp
def kdot_nt(a, b):
    return _dg(a, b, _NT)


kdot_nt.defvjp(lambda a, b: (_dg(a, b, _NT), (a, b)), lambda r, g: (_dg(g, r[1], _NN), _dg(g, r[0], _TN)))


@jax.custom_vjp
def kdot_tn(a, b):
    return _dg(a, b, _TN)


kdot_tn.defvjp(lambda a, b: (_dg(a, b, _TN), (a, b)), lambda r, g: (_dg(r[1], g, _NT), _dg(r[0], g, _NN)))


def _pick(d, prefs):
    for p in prefs:
        if d >= p and d % p == 0:
            return p
    return d


def _params(sem):
    return pltpu.CompilerParams(dimension_semantics=sem, vmem_limit_bytes=VMEM_LIMIT)


def _sds(shape, dtype=F32):
    return jax.ShapeDtypeStruct(shape, dtype)


def win(arr, layer, row_off=0, col_off=0, rows=None, cols=None):
    return (arr, layer, row_off, col_off, rows or arr.shape[1] - row_off, cols or arr.shape[2] - col_off)


def mm(a, b, mode, name, add=None, out_dtype=F32, into=None, rms_gain=None, norm_bwd=None):
    if isinstance(b, tuple):
        b_arr, b_layer, b_r0, b_c0, b_rows, b_cols = b
    else:
        b_arr, b_layer, b_r0, b_c0, (b_rows, b_cols) = b, None, 0, 0, b.shape
    if mode == "nn":
        (M, K), (K2, N) = a.shape, (b_rows, b_cols)
    elif mode == "nt":
        (M, K), (N, K2) = a.shape, (b_rows, b_cols)
    else:
        (K, M), (K2, N) = a.shape, (b_rows, b_cols)
    assert K == K2, (name, a.shape, b_rows, b_cols)
    tn = N if N <= 1024 else _pick(N, (1024, 1408, 512, 256, 128))
    tk = K if K <= 2048 else _pick(K, (2048, 2816, 1024, 512, 256, 128))
    nk = K // tk
    a_bytes = jnp.dtype(a.dtype).itemsize
    b_bytes = jnp.dtype(b_arr.dtype).itemsize
    extra_tiles = (add is not None) + (rms_gain is not None) / 2 + (3 if norm_bwd is not None else 0)

    def vmem_bytes(tm_):
        tiles = 2 * (tm_ * tk * a_bytes + tk * tn * b_bytes) + tm_ * tn * 4 * (2 + 2 * extra_tiles + (nk > 1))
        return tiles

    tm = M
    if M > 1024:
        fits = [t for t in (2048, 1024, 1408, 512, 256, 128) if M % t == 0 and vmem_bytes(t) <= MM_VMEM_BUDGET]
        tm = fits[0] if fits else _pick(M, (128,))
    dims = {"nn": _NN, "nt": _NT, "tn": _TN}[mode]
    a_spec = pl.BlockSpec((tk, tm), lambda i, j, k: (k, i)) if mode == "tn" else pl.BlockSpec((tm, tk), lambda i, j, k: (i, k))
    b_blk = (tn, tk) if mode == "nt" else (tk, tn)
    assert b_r0 % b_blk[0] == 0 and b_c0 % b_blk[1] == 0, (name, b_r0, b_c0, b_blk)
    br, bc = b_r0 // b_blk[0], b_c0 // b_blk[1]
    if mode == "nt":
        b_idx = lambda i, j, k: (br + j, bc + k)
    else:
        b_idx = lambda i, j, k: (br + k, bc + j)
    if b_layer is None:
        b_spec = pl.BlockSpec(b_blk, b_idx)
    else:
        b_spec = pl.BlockSpec((None,) + b_blk, lambda i, j, k: (b_layer,) + b_idx(i, j, k))
    plain = pl.BlockSpec((tm, tn), lambda i, j, k: (i, j))
    has_add, has_rms, has_nb = add is not None, rms_gain is not None, norm_bwd is not None
    assert not (has_rms or has_nb) or (tn == N and into is None and not (has_rms and has_nb)), name
    vec = pl.BlockSpec((1, tn), lambda i, j, k: (0, j))
    ins = [a, b_arr] + ([add] if has_add else []) + ([rms_gain] if has_rms else [])
    specs = [a_spec, b_spec] + ([plain] if has_add else []) + ([vec] if has_rms else [])
    n_in = len(ins)
    if has_nb:
        nb_x, nb_gain, nb_res = norm_bwd
        ins += [nb_x, nb_gain] + ([nb_res] if nb_res is not None else [])
        specs += [plain, vec] + ([plain] if nb_res is not None else [])
    aliases = {}
    if into is None:
        o_spec, out_shape = plain, _sds((M, N), out_dtype)
    else:
        buf, o_layer, o_r0, o_c0, o_rows, o_cols = into
        assert (o_rows, o_cols) == (M, N) and o_r0 % tm == 0 and o_c0 % tn == 0, (name, into[1:], M, N, tm, tn)
        orow, ocol = o_r0 // tm, o_c0 // tn
        o_spec = pl.BlockSpec((None, tm, tn), lambda i, j, k: (o_layer, orow + i, ocol + j))
        out_shape = _sds(buf.shape, buf.dtype)
        aliases = {len(ins): 0}
        ins.append(buf)
        specs.append(pl.BlockSpec(memory_space=pl.ANY))

    n_all_in = len(ins)

    def kern(*refs):
        a_ref, b_ref = refs[0], refs[1]
        add_ref = refs[2] if has_add else None
        o_ref = refs[n_all_in]

        def finish(r):
            if has_add:
                r = r + add_ref[...].astype(F32)
            if has_nb:
                _, vjp = jax.vjp(lambda xv, gv: _rms(xv, gv, N), refs[n_in][...], refs[n_in + 1][...])
                dx, dgain = vjp(r)
                o_ref[...] = dx if nb_res is None else dx + refs[n_in + 2][...]
                _store(refs[n_all_in + 1], dgain, pl.program_id(0) == 0)
                return
            o_ref[...] = r.astype(o_ref.dtype)
            if has_rms:
                refs[n_all_in + 1][...] = _rms(r, refs[n_in - 1][...], N).astype(_MXU)

        if nk == 1:
            finish(_dg(a_ref[...], b_ref[...], dims))
            return
        acc = refs[-1]
        k = pl.program_id(2)

        @pl.when(k == 0)
        def _():
            acc[...] = jnp.zeros_like(acc)

        acc[...] += _dg(a_ref[...], b_ref[...], dims)

        @pl.when(k == nk - 1)
        def _():
            finish(acc[...])

    if has_rms:
        o_spec, out_shape = [o_spec, plain], [out_shape, _sds((M, N), _MXU)]
    if has_nb:
        o_spec, out_shape = [o_spec, vec], [out_shape, _sds((1, N))]
    return pl.pallas_call(
        kern, name=name, grid=(M // tm, N // tn, nk), in_specs=specs, out_specs=o_spec, out_shape=out_shape,
        scratch_shapes=[pltpu.VMEM((tm, tn), F32)] if nk > 1 else [], input_output_aliases=aliases,
        compiler_params=_params(("arbitrary" if has_nb else "parallel", "parallel", "arbitrary")))(*ins)


def _store(ref, val, first):
    if first is None:
        ref[...] = val.astype(ref.dtype)
        return

    @pl.when(first)
    def _():
        ref[...] = val.astype(ref.dtype)

    @pl.when(jnp.logical_not(first))
    def _():
        ref[...] += val.astype(ref.dtype)


def tilecall(body, name, grid, ins, outs, sem, prefetch=(), aliases=None):
    n_pre, n_in = len(prefetch), len(ins)

    def kern(*refs):
        vals = body(*refs[:n_pre + n_in])
        for ref, val, (_, _, first) in zip(refs[n_pre + n_in:], vals, outs):
            _store(ref, val, None if first is None else first())

    in_specs, out_specs = [s for _, s in ins], [s for _, s, _ in outs]
    kwargs = dict(name=name, out_shape=[sh for sh, _, _ in outs], compiler_params=_params(sem),
                  input_output_aliases={n_pre + k: v for k, v in (aliases or {}).items()})
    if n_pre:
        kwargs["grid_spec"] = pltpu.PrefetchScalarGridSpec(num_scalar_prefetch=n_pre, grid=grid, in_specs=in_specs,
                                                           out_specs=out_specs)
    else:
        kwargs.update(grid=grid, in_specs=in_specs, out_specs=out_specs)
    return pl.pallas_call(kern, **kwargs)(*prefetch, *[a for a, _ in ins])


def _rms(x, g, n):
    ms = jnp.sum(x * x, axis=-1, keepdims=True) / n
    return x * lax.rsqrt(ms + RMS_EPS) * g


def _row_tile(S, w):
    return min(S, 512 if w <= 1024 else 256)


def rms_fwd(x, g, name, col=0, w=None):
    S = x.shape[0]
    w = w or x.shape[1]
    ts = _row_tile(S, w)
    return tilecall(
        lambda x_ref, g_ref: (_rms(x_ref[...], g_ref[...], w),), name, (S // ts,),
        [(x, pl.BlockSpec((ts, w), lambda i: (i, col))), (g, pl.BlockSpec((1, w), lambda i: (0, 0)))],
        [(_sds((S, w), _MXU), pl.BlockSpec((ts, w), lambda i: (i, 0)), None)], ("parallel",))[0]


def _shifted(u, halo_ref, is_first):
    rid = lax.broadcasted_iota(jnp.int32, (SUBLANES, 1), 0)
    h7 = jnp.where(is_first, 0.0, halo_ref[7:8, :])
    h6 = jnp.where(is_first, 0.0, halo_ref[6:7, :])
    r1, r2 = pltpu.roll(u, 1, 0), pltpu.roll(u, 2, 0)
    top1 = jnp.where(rid == 0, h7, r1[:SUBLANES])
    top2 = jnp.where(rid == 0, h6, jnp.where(rid == 1, h7, r2[:SUBLANES]))
    return jnp.concatenate([top1, r1[SUBLANES:]], axis=0), jnp.concatenate([top2, r2[SUBLANES:]], axis=0)


def _conv(u, u1, u2, cw_ref, cb_ref):
    return ((cb_ref[...] + u2 * cw_ref[0:1, :]) + u1 * cw_ref[1:2, :]) + u * cw_ref[2:3, :]


def _ffn_specs(S, ts, jmap):
    hb = ts // SUBLANES
    return (pl.BlockSpec((ts, FF_BLOCK), lambda j, i: (i, jmap(j))),
            pl.BlockSpec((SUBLANES, FF_BLOCK), lambda j, i: (jnp.maximum(i * hb - 1, 0), jmap(j))),
            pl.BlockSpec((SUBLANES, FF_BLOCK), lambda j, i: (0, jmap(j))),
            pl.BlockSpec((1, FF_BLOCK), lambda j, i: (0, jmap(j))))


def ffn_act_fwd(u, cw8, cb, name):
    S = u.shape[0]
    ts = _row_tile(S, 2 * D_FF)
    nb = D_FF // FF_BLOCK

    def body(ug, hg, cwg, cbg, uu, hu, cwu, cbu):
        first = pl.program_id(1) == 0
        g = ug[...]
        g1, g2 = _shifted(g, hg, first)
        yg = _conv(g, g1, g2, cwg, cbg)
        v = uu[...]
        v1, v2 = _shifted(v, hu, first)
        yu = _conv(v, v1, v2, cwu, cbu)
        return (yg * jax.nn.sigmoid(yg) * yu,)

    sg = _ffn_specs(S, ts, lambda j: j)
    su = _ffn_specs(S, ts, lambda j: j + nb)
    ins = [(u, sg[0]), (u, sg[1]), (cw8, sg[2]), (cb, sg[3]), (u, su[0]), (u, su[1]), (cw8, su[2]), (cb, su[3])]
    return tilecall(body, name, (nb, S // ts), ins,
                    [(_sds((S, D_FF), _MXU), pl.BlockSpec((ts, FF_BLOCK), lambda j, i: (i, j)), None)],
                    ("parallel", "parallel"))[0]


def ffn_act_bwd(u, cw8, cb, da, name):
    S = u.shape[0]
    ts = _row_tile(S, 2 * D_FF)
    nb = D_FF // FF_BLOCK

    def taps(dy, x, x1, x2):
        return jnp.concatenate(
            [jnp.sum(dy * x2, axis=0, keepdims=True), jnp.sum(dy * x1, axis=0, keepdims=True),
             jnp.sum(dy * x, axis=0, keepdims=True), jnp.zeros((SUBLANES - 3, dy.shape[1]), F32)], axis=0)

    def body(ug, hg, cwg, cbg, uu, hu, cwu, cbu, da_ref):
        first = pl.program_id(1) == 0
        g = ug[...]
        g1, g2 = _shifted(g, hg, first)
        yg = _conv(g, g1, g2, cwg, cbg)
        v = uu[...]
        v1, v2 = _shifted(v, hu, first)
        yu = _conv(v, v1, v2, cwu, cbu)
        d = da_ref[...]
        sg = jax.nn.sigmoid(yg)
        dyg = d * yu * (sg * (1.0 + yg * (1.0 - sg)))
        dyu = d * (yg * sg)
        return (dyg, dyu, taps(dyg, g, g1, g2), taps(dyu, v, v1, v2),
                jnp.sum(dyg, axis=0, keepdims=True), jnp.sum(dyu, axis=0, keepdims=True))

    sg_ = _ffn_specs(S, ts, lambda j: j)
    su_ = _ffn_specs(S, ts, lambda j: j + nb)
    row = pl.BlockSpec((ts, FF_BLOCK), lambda j, i: (i, j))
    ins = [(u, sg_[0]), (u, sg_[1]), (cw8, sg_[2]), (cb, sg_[3]), (u, su_[0]), (u, su_[1]), (cw8, su_[2]), (cb, su_[3]), (da, row)]
    first_row = lambda: pl.program_id(1) == 0
    dy, dcw, dcb = (_sds((S, D_FF)), row, None), (_sds((SUBLANES, D_FF)), sg_[2], first_row), (_sds((1, D_FF)), sg_[3], first_row)
    return tilecall(body, name, (nb, S // ts), ins, [dy, dy, dcw, dcw, dcb, dcb], ("parallel", "arbitrary"))


def ffn_conv_bwd(dyg, dyu, cw8, name):
    S = dyg.shape[0]
    ts = _row_tile(S, 2 * D_FF)
    hb = ts // SUBLANES
    nrow = S // ts
    nb = D_FF // FF_BLOCK

    def back(dy_ref, halo_ref, cw_ref):
        last = pl.program_id(1) == nrow - 1
        d = dy_ref[...]
        rid = lax.broadcasted_iota(jnp.int32, (SUBLANES, 1), 0)
        n0 = jnp.where(last, 0.0, halo_ref[0:1, :])
        n1 = jnp.where(last, 0.0, halo_ref[1:2, :])
        r1, r2 = pltpu.roll(d, ts - 1, 0), pltpu.roll(d, ts - 2, 0)
        end1 = jnp.where(rid == SUBLANES - 1, n0, r1[ts - SUBLANES:])
        end2 = jnp.where(rid == SUBLANES - 1, n1, jnp.where(rid == SUBLANES - 2, n0, r2[ts - SUBLANES:]))
        d1 = jnp.concatenate([r1[:ts - SUBLANES], end1], axis=0)
        d2 = jnp.concatenate([r2[:ts - SUBLANES], end2], axis=0)
        return d * cw_ref[2:3, :] + d1 * cw_ref[1:2, :] + d2 * cw_ref[0:1, :]

    row = pl.BlockSpec((ts, FF_BLOCK), lambda j, i: (i, j))
    halo = pl.BlockSpec((SUBLANES, FF_BLOCK), lambda j, i: (jnp.minimum((i + 1) * hb, S // SUBLANES - 1), j))
    ins = [(dyg, row), (dyg, halo), (cw8, pl.BlockSpec((SUBLANES, FF_BLOCK), lambda j, i: (0, j))),
           (dyu, row), (dyu, halo), (cw8, pl.BlockSpec((SUBLANES, FF_BLOCK), lambda j, i: (0, j + nb)))]
    out = (_sds((S, D_FF), _MXU), row, None)
    return tilecall(lambda a, b, c, d, e, f: (back(a, b, c), back(d, e, f)), name, (nb, nrow), ins, [out, out],
                    ("parallel", "parallel"))


def _hgrn_levels(C):
    out, m = [], C // 2
    while m >= 1:
        out.append(m)
        m //= 2
    return out


def _hgrn_sum_matrix(C):
    t = np.arange(C)[:, None]
    u = np.arange(C)[None, :]
    blocks = [u <= t, u > t]
    for m in _hgrn_levels(C):
        r = (t // (2 * m)) * (2 * m) + m
        right = (t % (2 * m)) >= m
        blocks.append((right & (u > r) & (u <= t)) | ((~right) & (u > t) & (u <= r)))
    return np.concatenate(blocks, axis=0).astype(np.float32)


def _make_partial_sums(nb, C):
    @jax.custom_vjp
    def sums(mall, lf):
        hi = lf.astype(_MXU)
        mid = (lf - hi.astype(F32)).astype(_MXU)
        e2 = _dg(mall, jnp.concatenate([hi, mid], axis=1), _NN)
        e = e2[:, :HEAD_DIM] + e2[:, HEAD_DIM:]
        return tuple(e[b * C:(b + 1) * C] for b in range(nb))

    def fwd(mall, lf):
        return sums(mall, lf), mall

    def bwd(mall, gs):
        return jnp.zeros_like(mall), _dg(mall, jnp.concatenate(gs, axis=0), _TN)

    sums.defvjp(fwd, bwd)
    return sums


def _hgrn_chunk(zq, zf, v, lb, st, mall, C):
    levels = _hgrn_levels(C)
    qs = zq * jax.nn.sigmoid(zq)
    fg = lb + (1.0 - lb) * jax.nn.sigmoid(zf)
    k = 1.0 - fg
    e = _make_partial_sums(2 + len(levels), C)(mall, jnp.log(fg))
    g_incl, g_after = e[0], e[1]
    rid = lax.broadcasted_iota(jnp.int32, (C, 1), 0)
    tt = lax.broadcasted_iota(jnp.int32, (C, C), 0)
    ss = lax.broadcasted_iota(jnp.int32, (C, C), 1)
    o = kdot_nt(qs * jnp.exp(g_incl), st)
    o = o + jnp.sum(qs * k, axis=-1, keepdims=True) * v
    scores = jnp.zeros((C, C), F32)
    for li, m in enumerate(levels):
        sh = int(np.log2(m))
        right = ((rid >> sh) & 1) == 1
        both = jnp.where(right, qs, k) * jnp.exp(e[2 + li])
        pair = ((tt >> (sh + 1)) == (ss >> (sh + 1))) & (((tt >> sh) & 1) == 1) & (((ss >> sh) & 1) == 0)
        scores = scores + jnp.where(pair, kdot_nt(both, both), 0.0)
    o = o + kdot(scores, v)
    g_last = jnp.sum(jnp.where(rid == C - 1, g_incl, 0.0), axis=0, keepdims=True)
    st_new = st * jnp.exp(g_last) + kdot_tn(v, k * jnp.exp(g_after))
    return o, st_new


HGRN_HEADS_PER_STEP = 8
_HGRN_LANES = HGRN_HEADS_PER_STEP * HEAD_DIM


def _hgrn_in_specs(C, nc, rev):
    cm = (lambda c: nc - 1 - c) if rev else (lambda c: c)
    blk = lambda: pl.BlockSpec((C, _HGRN_LANES), lambda h, c: (cm(c), h))
    return cm, [blk(), blk(), blk(), pl.BlockSpec((1, _HGRN_LANES), lambda h, c: (0, h))]


def _hgrn_state_spec(cm):
    return pl.BlockSpec((HGRN_HEADS_PER_STEP, None, HEAD_DIM, HEAD_DIM), lambda h, c: (h, cm(c), 0, 0))


def _hgrn_out(o, g, gain):
    return _rms(o, gain, HEAD_DIM) * (g * jax.nn.sigmoid(g))


def hgrn_fwd(zq, zf, zi, zg, lb, out_gain, mall, name):
    S = zq.shape[0]
    C = min(HGRN_CHUNK, S)
    nc = S // C

    def kern(zq_ref, zf_ref, zi_ref, lb_ref, zg_ref, gain_ref, mall_ref, o_ref, on_ref, st_ref, st):
        @pl.when(pl.program_id(1) == 0)
        def _():
            st[...] = jnp.zeros_like(st)

        mall_v = mall_ref[...]
        for g in range(HGRN_HEADS_PER_STEP):
            lanes = slice(g * HEAD_DIM, (g + 1) * HEAD_DIM)
            s_in = st[g]
            st_ref[g] = s_in
            o, s_new = _hgrn_chunk(zq_ref[:, lanes], zf_ref[:, lanes], zi_ref[:, lanes], lb_ref[:, lanes], s_in, mall_v, C)
            o_ref[:, lanes] = o
            on_ref[:, lanes] = _hgrn_out(o, zg_ref[:, lanes], gain_ref[...]).astype(on_ref.dtype)
            st[g] = s_new

    cm, specs = _hgrn_in_specs(C, nc, False)
    row = pl.BlockSpec((C, _HGRN_LANES), lambda h, c: (c, h))
    return pl.pallas_call(
        kern, name=name, grid=(HEADS // HGRN_HEADS_PER_STEP, nc),
        in_specs=specs + [row, pl.BlockSpec((1, HEAD_DIM), lambda h, c: (0, 0)), pl.BlockSpec(mall.shape, lambda h, c: (0, 0))],
        out_specs=[row, row, _hgrn_state_spec(cm)],
        out_shape=[_sds((S, D_MODEL)), _sds((S, D_MODEL), _MXU), _sds((HEADS, nc, HEAD_DIM, HEAD_DIM))],
        scratch_shapes=[pltpu.VMEM((HGRN_HEADS_PER_STEP, HEAD_DIM, HEAD_DIM), F32)],
        compiler_params=_params(("parallel", "arbitrary")))(zq, zf, zi, lb, zg, out_gain, mall)


def hgrn_bwd(zq, zf, zi, zg, lb, out_gain, mall, states, o, don, name):
    S = zq.shape[0]
    C = min(HGRN_CHUNK, S)
    nc = S // C

    def kern(zq_ref, zf_ref, zi_ref, lb_ref, zg_ref, gain_ref, mall_ref, st_ref, o_ref, don_ref,
             dq_ref, df_ref, di_ref, dg_ref, dlb_ref, dgain_ref, dst):
        first = pl.program_id(1) == 0

        @pl.when(first)
        def _():
            dst[...] = jnp.zeros_like(dst)

        mall_v = mall_ref[...]
        gls, dgain = [], None
        for g in range(HGRN_HEADS_PER_STEP):
            lanes = slice(g * HEAD_DIM, (g + 1) * HEAD_DIM)
            _, out_vjp = jax.vjp(_hgrn_out, o_ref[:, lanes], zg_ref[:, lanes], gain_ref[...])
            do, dzg, dgn = out_vjp(don_ref[:, lanes])
            dg_ref[:, lanes] = dzg.astype(dg_ref.dtype)
            dgain = dgn if dgain is None else dgain + dgn
            _, vjp = jax.vjp(lambda a, b, c, d, e: _hgrn_chunk(a, b, c, d, e, mall_v, C),
                             zq_ref[:, lanes], zf_ref[:, lanes], zi_ref[:, lanes], lb_ref[:, lanes], st_ref[g])
            ga, gb, gv, gl, gs = vjp((do, dst[g]))
            dq_ref[:, lanes] = ga.astype(dq_ref.dtype)
            df_ref[:, lanes] = gb.astype(df_ref.dtype)
            di_ref[:, lanes] = gv.astype(di_ref.dtype)
            gls.append(gl)
            dst[g] = gs
        _store(dlb_ref, jnp.concatenate(gls, axis=1), first)
        _store(dgain_ref, dgain, first & (pl.program_id(0) == 0))

    cm, specs = _hgrn_in_specs(C, nc, True)
    row = lambda: pl.BlockSpec((C, _HGRN_LANES), lambda h, c: (cm(c), h))
    vec = pl.BlockSpec((1, HEAD_DIM), lambda h, c: (0, 0))
    wide = _sds((S, D_MODEL), _MXU)
    return pl.pallas_call(
        kern, name=name, grid=(HEADS // HGRN_HEADS_PER_STEP, nc),
        in_specs=specs + [row(), vec, pl.BlockSpec(mall.shape, lambda h, c: (0, 0)), _hgrn_state_spec(cm), row(), row()],
        out_specs=[row(), row(), row(), row(), pl.BlockSpec((1, _HGRN_LANES), lambda h, c: (0, h)), vec],
        out_shape=[wide, wide, wide, wide, _sds((1, D_MODEL)), _sds((1, HEAD_DIM))],
        scratch_shapes=[pltpu.VMEM((HGRN_HEADS_PER_STEP, HEAD_DIM, HEAD_DIM), F32)],
        compiler_params=_params(("arbitrary", "arbitrary")))(zq, zf, zi, lb, zg, out_gain, mall, states, o, don)


def _lb_soft(p0, p1):
    mx = jnp.maximum(p0, p1)
    e0, e1 = jnp.exp(p0 - mx), jnp.exp(p1 - mx)
    s0, s1 = e0 / (e0 + e1), e1 / (e0 + e1)
    return (s0 + s1) - s0


def lower_bound_fwd(p, name):
    assert p.shape[0] == 2

    def body(p_ref):
        s = _lb_soft(p_ref[0:1, :], p_ref[1:2, :])
        return (jnp.concatenate([jnp.zeros_like(s), s] + [jnp.zeros_like(s)] * (SUBLANES - 2), axis=0),)

    spec8 = pl.BlockSpec((SUBLANES, p.shape[1]), lambda: (0, 0))
    return tilecall(body, name, (), [(p, pl.BlockSpec(p.shape, lambda: (0, 0)))], [(_sds((SUBLANES, p.shape[1])), spec8, None)], ())[0]


def lower_bound_bwd(p, dlb1, name):
    def body(p_ref, d_ref):
        _, vjp = jax.vjp(_lb_soft, p_ref[0:1, :], p_ref[1:2, :])
        g0, g1 = vjp(d_ref[...])
        return (jnp.concatenate([g0, g1] + [jnp.zeros_like(g0)] * (SUBLANES - 2), axis=0),)

    spec8 = pl.BlockSpec((SUBLANES, p.shape[1]), lambda: (0, 0))
    return tilecall(body, name, (), [(p, pl.BlockSpec(p.shape, lambda: (0, 0))), (dlb1, pl.BlockSpec(dlb1.shape, lambda: (0, 0)))],
                    [(_sds((SUBLANES, p.shape[1])), spec8, None)], ())[0]


@jax.custom_vjp
def _swap_rope_halves(x):
    lane = lax.broadcasted_iota(jnp.int32, x.shape, 1)
    lo = (lane >= MLA_NOPE) & (lane < MLA_NOPE + MLA_ROPE // 2)
    hi = (lane >= MLA_NOPE + MLA_ROPE // 2) & (lane < MLA_QK)
    return jnp.where(lo, pltpu.roll(x, MLA_SLOT - MLA_ROPE // 2, 1), jnp.where(hi, pltpu.roll(x, MLA_ROPE // 2, 1), 0.0))


_swap_rope_halves.defvjp(lambda x: (_swap_rope_halves(x), None), lambda _, g: (_swap_rope_halves(g),))


def _norm_rope(x, gain, cos_t, sin_t):
    y = _rms(x, gain, MLA_QK)
    return y * cos_t + _swap_rope_halves(y) * sin_t


_ATTN_SCALE = MLA_QK ** -0.5


def _qk_heads(qs, kn, kr, qn, kn_gain, cos_t, sin_t):
    q = _norm_rope(qs, qn, cos_t, sin_t) * _ATTN_SCALE
    k = _norm_rope(jnp.concatenate([kn, kr], axis=1), kn_gain, cos_t, sin_t)
    return q, k


def _qk_specs(ts):
    slot = pl.BlockSpec((ts, MLA_SLOT), lambda i, h: (i, h))
    nope = pl.BlockSpec((ts, HEAD_DIM), lambda i, h: (i, 2 * h))
    shared = pl.BlockSpec((ts, HEAD_DIM), lambda i, h: (i, 0))
    gain = pl.BlockSpec((1, MLA_SLOT), lambda i, h: (0, 0))
    table = pl.BlockSpec((ts, MLA_SLOT), lambda i, h: (i, 0))
    return slot, nope, shared, gain, table


QK_ROWS = 1024


def qk_fwd(qslots, kv, krope, qn, kn, cos_t, sin_t, name):
    S = qslots.shape[0]
    ts = min(S, QK_ROWS)
    slot, nope, shared, gain, table = _qk_specs(ts)

    def body(q_ref, kn_ref, kr_ref, qn_ref, kg_ref, c_ref, s_ref):
        return _qk_heads(q_ref[...], kn_ref[...], kr_ref[...], qn_ref[...], kg_ref[...], c_ref[...], s_ref[...])

    out = _sds((S, HEADS * MLA_SLOT), _MXU)
    return tilecall(body, name, (S // ts, HEADS),
                    [(qslots, slot), (kv, nope), (krope, shared), (qn, gain), (kn, gain), (cos_t, table), (sin_t, table)],
                    [(out, slot, None), (out, slot, None)], ("parallel", "parallel"))


def qk_bwd(qslots, kv, krope, qn, kn, cos_t, sin_t, dq, dk, dv, name):
    S = qslots.shape[0]
    ts = min(S, QK_ROWS)
    slot, nope, shared, gain, table = _qk_specs(ts)
    vblk = pl.BlockSpec((ts, HEAD_DIM), lambda i, h: (i, h))

    def body(q_ref, kn_ref, kr_ref, qn_ref, kg_ref, c_ref, s_ref, dq_ref, dk_ref, dv_ref):
        c, s = c_ref[...], s_ref[...]
        _, vjp = jax.vjp(lambda a, b, r, g1, g2: _qk_heads(a, b, r, g1, g2, c, s),
                         q_ref[...], kn_ref[...], kr_ref[...], qn_ref[...], kg_ref[...])
        ga, gb, gr, g1, g2 = vjp((dq_ref[...], dk_ref[...]))
        return ga, jnp.concatenate([gb, dv_ref[...]], axis=1), gr, g1, g2

    first_head = lambda: pl.program_id(1) == 0
    first = lambda: (pl.program_id(0) == 0) & (pl.program_id(1) == 0)
    wide = _sds((S, HEADS * MLA_SLOT), _MXU)
    return tilecall(body, name, (S // ts, HEADS),
                    [(qslots, slot), (kv, nope), (krope, shared), (qn, gain), (kn, gain), (cos_t, table), (sin_t, table),
                     (dq, slot), (dk, slot), (dv, vblk)],
                    [(wide, slot, None), (wide, slot, None), (_sds((S, HEAD_DIM)), shared, first_head),
                     (_sds((1, MLA_SLOT)), gain, first), (_sds((1, MLA_SLOT)), gain, first)], ("arbitrary", "arbitrary"))


ATTN_TILE_FWD = 1024
ATTN_TILE = 1024
ATTN_HEADS_PER_STEP = 2
ATTN_HEADS_PER_STEP_BWD = 1


def _causal_pairs(nq, by_row):
    pairs = [(i, j) for i in range(nq) for j in range(i + 1)] if by_row else [(i, j) for j in range(nq) for i in range(j, nq)]
    return jnp.asarray([p[0] for p in pairs], jnp.int32), jnp.asarray([p[1] for p in pairs], jnp.int32)


def _diag_mask(s, tq):
    rows = lax.broadcasted_iota(jnp.int32, (tq, tq), 0)
    cols = lax.broadcasted_iota(jnp.int32, (tq, tq), 1)
    return jnp.where(rows >= cols, s, -jnp.inf)


def attn_fwd(qr, kr, kv, name):
    S = qr.shape[0]
    tq = min(S, ATTN_TILE_FWD)
    nq = S // tq
    i_tab, j_tab = _causal_pairs(nq, True)

    G = ATTN_HEADS_PER_STEP
    reps = tq // LANES

    def kern(it, jt, q_ref, k_ref, kv_ref, o_ref, lse_ref, m_s, l_s, acc):
        n = pl.program_id(1)
        i, j = it[n], jt[n]

        @pl.when(j == 0)
        def _():
            m_s[...] = jnp.full_like(m_s, -jnp.inf)
            l_s[...] = jnp.zeros_like(l_s)
            acc[...] = jnp.zeros_like(acc)

        def step(diagonal):
            for g in range(G):
                slot = slice(g * MLA_SLOT, (g + 1) * MLA_SLOT)
                s = _dg(q_ref[:, slot], k_ref[:, slot], _NT)
                if diagonal:
                    s = _diag_mask(s, tq)
                m_prev = m_s[g]
                m_new = jnp.maximum(m_prev, jnp.max(s, axis=-1, keepdims=True))
                alpha = jnp.exp(m_prev - m_new)
                p = jnp.exp(s - jnp.tile(m_new, (1, reps)))
                l_s[g] = alpha * l_s[g] + jnp.sum(p, axis=-1, keepdims=True)
                acc[g] = alpha * acc[g] + _dg(p, kv_ref[:, g * MLA_SLOT + HEAD_DIM:(g + 1) * MLA_SLOT], _NN)
                m_s[g] = m_new

        @pl.when(j < i)
        def _():
            step(False)

        @pl.when(j == i)
        def _():
            step(True)
            for g in range(G):
                l = l_s[g]
                lanes = slice(g * HEAD_DIM, (g + 1) * HEAD_DIM)
                o_ref[:, lanes] = acc[g] / l
                lse_ref[:, lanes] = m_s[g] + jnp.log(l)

    out = _sds((S, HEADS * HEAD_DIM))
    oblk = pl.BlockSpec((tq, G * HEAD_DIM), lambda h, n, it, jt: (it[n], h))
    stat = pltpu.VMEM((G, tq, HEAD_DIM), F32)
    return pl.pallas_call(
        kern, name=name,
        grid_spec=pltpu.PrefetchScalarGridSpec(
            num_scalar_prefetch=2, grid=(HEADS // G, i_tab.shape[0]),
            in_specs=[pl.BlockSpec((tq, G * MLA_SLOT), lambda h, n, it, jt: (it[n], h)),
                      pl.BlockSpec((tq, G * MLA_SLOT), lambda h, n, it, jt: (jt[n], h)),
                      pl.BlockSpec((tq, G * MLA_SLOT), lambda h, n, it, jt: (jt[n], h))],
            out_specs=[oblk, oblk], scratch_shapes=[stat, stat, stat]),
        out_shape=[out, out], compiler_params=_params(("parallel", "arbitrary")))(i_tab, j_tab, qr, kr, kv)


def attn_bwd(qr, kr, kv, o, lse, do, name):
    S = qr.shape[0]
    tq = min(S, ATTN_TILE)
    nq = S // tq
    i_tab, j_tab = _causal_pairs(nq, False)

    G = ATTN_HEADS_PER_STEP_BWD

    def kern(it, jt, q_ref, k_ref, kv_ref, o_ref, lse_ref, do_ref, dq_ref, dk_ref, dv_ref, dk_acc, dv_acc):
        n = pl.program_id(1)
        i, j = it[n], jt[n]

        @pl.when(n == 0)
        def _():
            dq_ref[...] = jnp.zeros_like(dq_ref)

        @pl.when(i == j)
        def _():
            dk_acc[...] = jnp.zeros_like(dk_acc)
            dv_acc[...] = jnp.zeros_like(dv_acc)

        def step(diagonal):
            rows = pl.ds(pl.multiple_of(i * tq, tq), tq)
            for g in range(G):
                slot = slice(g * MLA_SLOT, (g + 1) * MLA_SLOT)
                lanes = slice(g * HEAD_DIM, (g + 1) * HEAD_DIM)
                q, k = q_ref[:, slot], k_ref[:, slot]
                s = _dg(q, k, _NT) - jnp.tile(lse_ref[:, lanes], (1, tq // LANES))
                if diagonal:
                    s = _diag_mask(s, tq)
                p = jnp.exp(s)
                d = do_ref[:, lanes]
                delta = jnp.sum(d * o_ref[:, lanes], axis=-1, keepdims=True)
                dv_acc[:, lanes] += _dg(p, d, _TN)
                ds = p * (_dg(d, kv_ref[:, g * MLA_SLOT + HEAD_DIM:(g + 1) * MLA_SLOT], _NT) - delta)
                dk_acc[:, slot] += _dg(ds, q, _TN)
                dq_ref[rows, slot] += _dg(ds, k, _NN)

        @pl.when(i > j)
        def _():
            step(False)

        @pl.when(i == j)
        def _():
            step(True)

        @pl.when(i == nq - 1)
        def _():
            dk_ref[...] = dk_acc[...]
            dv_ref[...] = dv_acc[...]

    qblk = pl.BlockSpec((tq, G * MLA_SLOT), lambda h, n, it, jt: (it[n], h))
    oblk = pl.BlockSpec((tq, G * HEAD_DIM), lambda h, n, it, jt: (it[n], h))
    kblk = pl.BlockSpec((tq, G * MLA_SLOT), lambda h, n, it, jt: (jt[n], h))
    return pl.pallas_call(
        kern, name=name,
        grid_spec=pltpu.PrefetchScalarGridSpec(
            num_scalar_prefetch=2, grid=(HEADS // G, i_tab.shape[0]),
            in_specs=[qblk, kblk, kblk, oblk, oblk, oblk],
            out_specs=[pl.BlockSpec((S, G * MLA_SLOT), lambda h, n, it, jt: (0, h)), kblk,
                       pl.BlockSpec((tq, G * HEAD_DIM), lambda h, n, it, jt: (jt[n], h))],
            scratch_shapes=[pltpu.VMEM((tq, G * MLA_SLOT), F32), pltpu.VMEM((tq, G * HEAD_DIM), F32)]),
        out_shape=[_sds((S, HEADS * MLA_SLOT)), _sds((S, HEADS * MLA_SLOT)), _sds((S, HEADS * HEAD_DIM))],
        compiler_params=_params(("parallel", "arbitrary")))(i_tab, j_tab, qr, kr, kv, o, lse, do)


def loss_head(y, target, name):
    S, Dm = y.shape
    ts = _row_tile(S, Dm)

    def body(y_ref, t_ref):
        e = y_ref[...] - t_ref[...]
        tot = jnp.sum(jnp.sum(e * e, axis=-1, keepdims=True) / Dm, axis=0, keepdims=True)
        return e / Dm, jnp.broadcast_to(0.5 * tot, (SUBLANES, LANES))

    row = pl.BlockSpec((ts, Dm), lambda i: (i, 0))
    return tilecall(body, name, (S // ts,), [(y, row), (target, row)],
                    [(_sds((S, Dm)), row, None),
                     (_sds((SUBLANES, LANES)), pl.BlockSpec((SUBLANES, LANES), lambda i: (0, 0)), lambda: pl.program_id(0) == 0)],
                    ("arbitrary",))


_CHIP_FLIPS = ((1, 0), (0, 1), (1, 1))
_PEER_FLIPS = {
    "chips": ((1, 0, 0), (0, 1, 0), (1, 1, 0)),
    "sibling": ((0, 0, 1),),
    "all": tuple((a, b, c) for a in (0, 1) for b in (0, 1) for c in (0, 1))[1:],
}
_SLOT_WEIGHTS = {"chips": (2, 1, 0), "sibling": (0, 0, 1), "all": (4, 2, 1)}
_HBM = pl.BlockSpec(memory_space=pltpu.HBM)


def _me():
    return lax.axis_index("x"), lax.axis_index("y"), lax.axis_index("c")


def _remote(src, dst, send_sem, recv_sem, peer):
    return pltpu.make_async_remote_copy(src_ref=src, dst_ref=dst, send_sem=send_sem, recv_sem=recv_sem,
                                        device_id=peer, device_id_type=pl.DeviceIdType.MESH)


def exchange(arrs, group, slab_weights, name, keep_own=True):
    flips = _PEER_FLIPS[group]
    wx, wy, wc = _SLOT_WEIGHTS[group]
    n_slots = len(flips) + (1 if keep_own else 0)
    n = len(arrs)

    def slab(ref, a, pos):
        w = slab_weights[a]
        return ref if w is None else ref.at[w[0] * pos[0] + w[1] * pos[1] + w[2] * pos[2]]

    def kern(*refs):
        srcs, outs = refs[:n], refs[n:2 * n]
        send_sems, recv_sems = refs[2 * n:2 * n + 2]
        me = _me()
        my_slot = wx * me[0] + wy * me[1] + wc * me[2]
        copies = []
        if keep_own:
            local_sems = refs[2 * n + 2]
            for a in range(n):
                cp = pltpu.make_async_copy(slab(srcs[a], a, me), outs[a].at[my_slot], local_sems.at[a])
                cp.start()
                copies.append(cp)
        for f, flip in enumerate(flips):
            peer = tuple(m ^ b if b else m for m, b in zip(me, flip))
            for a in range(n):
                cp = _remote(slab(srcs[a], a, peer), outs[a].at[my_slot if keep_own else f],
                             send_sems.at[f, a], recv_sems.at[f, a], peer)
                cp.start()
                copies.append(cp)
        for cp in copies:
            cp.wait()

    out_shape = [_sds((n_slots,) + (a.shape if slab_weights[k] is None else a.shape[1:]), a.dtype) for k, a in enumerate(arrs)]
    sems = [pltpu.SemaphoreType.DMA((len(flips), n)), pltpu.SemaphoreType.DMA((len(flips), n))]
    return pl.pallas_call(
        kern, name=name, in_specs=[_HBM] * n, out_specs=[_HBM] * n, out_shape=out_shape,
        scratch_shapes=sems + ([pltpu.SemaphoreType.DMA((n,))] if keep_own else []))(*arrs)


def _chip_window(ref, kind, size, chip, layers):
    if kind == "rows":
        return ref.at[layers, pl.ds(chip * size, size), :]
    return ref.at[layers, :, pl.ds(pl.multiple_of(chip * size, LANES), size)]


def gather_big(shards, kinds, name):
    n = len(shards)
    fulls = []
    for s, kind in zip(shards, kinds):
        L, r, c = s.shape
        fulls.append(_sds((L, N_CHIPS * r, c) if kind == "rows" else (L, r, N_CHIPS * c), s.dtype))

    def kern(*refs):
        srcs, outs = refs[:n], refs[n:2 * n]
        ici_s, ici_r, relay_s, relay_r, d2d_s, d2d_r = refs[2 * n:]
        x, y, c = _me()
        sibling = (x, y, 1 - c)
        nbrs = ((x ^ 1, y), (x, y ^ 1))
        relay_from = (x ^ (1 - c), y ^ c)
        relay_to = (x ^ c, y ^ (1 - c), c)
        diagonal = (x ^ 1, y ^ 1)

        def window(a, chip):
            L, r, cc = shards[a].shape
            size = r if kinds[a] == "rows" else cc
            return _chip_window(outs[a], kinds[a], size, 2 * chip[0] + chip[1], pl.ds(c * (L // 2), L // 2))

        def direct(a, f):
            L = shards[a].shape[0]
            return _remote(srcs[a].at[pl.ds(c * (L // 2), L // 2)], window(a, (x, y)), ici_s.at[a, f], ici_r.at[a, f],
                           (nbrs[f][0], nbrs[f][1], c))

        def to_sibling(a, k, chip):
            return _remote(window(a, chip), window(a, chip), d2d_s.at[a, k], d2d_r.at[a, k], sibling)

        sends, relays, passed = [], [], []
        for a in range(n):
            for f in range(2):
                cp = direct(a, f)
                cp.start()
                sends.append(cp)
        for a in range(n):
            _remote(window(a, relay_from), window(a, relay_from), ici_s.at[a, c], ici_r.at[a, c], relay_to).wait_recv()
            cp = _remote(window(a, relay_from), window(a, relay_from), relay_s.at[a], relay_r.at[a], relay_to)
            cp.start()
            relays.append(cp)
            passed.append(to_sibling(a, 0, relay_from))
            passed[-1].start()
            _remote(window(a, relay_from), window(a, relay_from), ici_s.at[a, 1 - c], ici_r.at[a, 1 - c], relay_to).wait_recv()
            passed.append(to_sibling(a, 1, (relay_to[0], relay_to[1])))
            passed[-1].start()
        for a in range(n):
            relays[a].wait_recv()
            passed.append(to_sibling(a, 2, diagonal))
            passed[-1].start()
        for cp in sends + relays:
            cp.wait_send()
        for cp in passed:
            cp.wait()

    pair = pltpu.SemaphoreType.DMA((n, 2))
    one = pltpu.SemaphoreType.DMA((n,))
    three = pltpu.SemaphoreType.DMA((n, 3))
    return pl.pallas_call(kern, name=name, in_specs=[_HBM] * n, out_specs=[_HBM] * n, out_shape=fulls,
                          scratch_shapes=[pair, pair, one, one, three, three])(*shards)


def send_other_half(arrs, name):
    n = len(arrs)

    def kern(*refs):
        srcs, outs = refs[:n], refs[n:2 * n]
        send_sems, recv_sems = refs[2 * n:]
        x, y, c = _me()
        copies = []
        for a in range(n):
            hl = arrs[a].shape[0] // 2
            cp = _remote(srcs[a].at[pl.ds((1 - c) * hl, hl)], outs[a], send_sems.at[a], recv_sems.at[a], (x, y, 1 - c))
            cp.start()
            copies.append(cp)
        for cp in copies:
            cp.wait()

    return pl.pallas_call(
        kern, name=name, in_specs=[_HBM] * n, out_specs=[_HBM] * n,
        out_shape=[_sds((a.shape[0] // 2,) + a.shape[1:], a.dtype) for a in arrs],
        scratch_shapes=[pltpu.SemaphoreType.DMA((n,)), pltpu.SemaphoreType.DMA((n,))])(*arrs)


def scatter_to_chips(arrs, kinds, name):
    n = len(arrs)
    shapes = []
    for a, kind in zip(arrs, kinds):
        l, R, C = a.shape
        shapes.append((l, R // N_CHIPS, C) if kind == "rows" else (l, R, C // N_CHIPS))

    def kern(*refs):
        srcs, outs = refs[:n], refs[n:2 * n]
        send_sems, recv_sems = refs[2 * n:]
        x, y, c = _me()
        copies = []
        for a in range(n):
            size = shapes[a][1] if kinds[a] == "rows" else shapes[a][2]
            for f, (fx, fy) in enumerate(_CHIP_FLIPS):
                window = _chip_window(srcs[a], kinds[a], size, 2 * (x ^ fx) + (y ^ fy), slice(None))
                cp = _remote(window, outs[a].at[f], send_sems.at[a, f], recv_sems.at[a, f], (x ^ fx, y ^ fy, c))
                cp.start()
                copies.append(cp)
        for cp in copies:
            cp.wait()

    sem = pltpu.SemaphoreType.DMA((n, len(_CHIP_FLIPS)))
    return pl.pallas_call(
        kern, name=name, in_specs=[_HBM] * n, out_specs=[_HBM] * n,
        out_shape=[_sds((len(_CHIP_FLIPS),) + s, a.dtype) for s, a in zip(shapes, arrs)],
        scratch_shapes=[sem, sem])(*arrs)


def _stack_tile(r, c):
    for t in (1024, 704, 512, 352, 256, 128, 64, 32, 16):
        if r % t == 0 and t * c * 4 <= 3 * 512 * 1024:
            return t
    return r


def _window_map(kind, r, tr):
    nrt = r // tr
    if kind == "rows":
        return lambda l, i, chip: (l, chip[0] * nrt + i, 0)
    return lambda l, i, chip: (l, i, chip[0])


def place(full, shard, kind, chip, name):
    L, r, c = shard.shape
    tr = _stack_tile(r, c)
    wmap = _window_map(kind, r, tr)
    return tilecall(lambda chip_ref, s_ref, f_ref: (s_ref[...],), name, (L, r // tr),
                    [(shard, pl.BlockSpec((None, tr, c), lambda l, i, chip: (l, i, 0))), (full, pl.BlockSpec(memory_space=pl.ANY))],
                    [(_sds(full.shape, full.dtype), pl.BlockSpec((None, tr, c), lambda l, i, chip: wmap(l, i, chip)), None)],
                    ("parallel", "parallel"), prefetch=(chip,), aliases={1: 0})[0]


def add_cores(g, other, core, name):
    L, R, C = g.shape
    hl = L // 2
    tr = _stack_tile(R, C)
    blk = (None, tr, C)
    return tilecall(lambda core_ref, a_ref, b_ref: (a_ref[...] + b_ref[...],), name, (hl, R // tr),
                    [(g, pl.BlockSpec(blk, lambda l, i, core: (core[0] * hl + l, i, 0))),
                     (other, pl.BlockSpec(blk, lambda l, i, core: (l, i, 0)))],
                    [(_sds((hl, R, C), _MXU), pl.BlockSpec(blk, lambda l, i, core: (l, i, 0)), None)],
                    ("parallel", "parallel"), prefetch=(core,))[0]


def add_chips(own, got, kind, chip, name):
    nf, l, r, c = got.shape
    tr = _stack_tile(r, c)
    wmap = _window_map(kind, r, tr)

    def body(chip_ref, own_ref, *got_refs):
        acc = own_ref[...].astype(F32)
        for ref in got_refs:
            acc = acc + ref[...].astype(F32)
        return (acc,)

    return tilecall(body, name, (l, r // tr),
                    [(own, pl.BlockSpec((None, tr, c), lambda ll, i, chip: wmap(ll, i, chip)))] +
                    [(got, pl.BlockSpec((None, None, tr, c), lambda ll, i, chip, f=f: (f, ll, i, 0))) for f in range(nf)],
                    [(_sds((l, r, c)), pl.BlockSpec((None, tr, c), lambda ll, i, chip: (ll, i, 0)), None)],
                    ("parallel", "parallel"), prefetch=(chip,))[0]


def _adam_update(g, w, m, v):
    m_new = ADAM_B1 * m + (1.0 - ADAM_B1) * g
    v_new = ADAM_B2 * v + (1.0 - ADAM_B2) * jnp.square(g)
    m_hat = m_new / (1.0 - ADAM_B1 ** ADAM_STEP)
    v_hat = v_new / (1.0 - ADAM_B2 ** ADAM_STEP)
    delta = -ADAM_LR * (m_hat / (jnp.sqrt(v_hat) + ADAM_EPS) + ADAM_WD * w)
    return g, delta, m_new, v_new


def adamw_stacked(mine, theirs, w, m, v, core, name):
    L, r, c = w.shape
    hl = L // 2
    tr = _stack_tile(r, c)

    def body(core_ref, a_ref, b_ref, w_ref, m_ref, v_ref):
        is_mine = (pl.program_id(0) // hl) == core_ref[0]
        g = jnp.where(is_mine, a_ref[...], b_ref[...])
        return _adam_update(g, w_ref[...], m_ref[...], v_ref[...])

    full = pl.BlockSpec((None, tr, c), lambda l, i, core: (l, i, 0))
    out = (_sds((L, r, c)), full, None)
    return tilecall(body, name, (L, r // tr),
                    [(mine, pl.BlockSpec((None, tr, c), lambda l, i, core: (l % hl, i, 0))),
                     (theirs, pl.BlockSpec((None, None, tr, c), lambda l, i, core: (0, l % hl, i, 0))),
                     (w, full), (m, full), (v, full)],
                    [out, out, out, out], ("parallel", "parallel"), prefetch=(core,))


def _pack(arrs, dtype, row_multiple):
    flat = jnp.concatenate([a.reshape(-1).astype(dtype) for a in arrs])
    rows = -(-flat.shape[0] // LANES)
    rows = -(-rows // row_multiple) * row_multiple
    return jnp.pad(flat, (0, rows * LANES - flat.shape[0])).reshape(rows, LANES)


def _unpack(buf, shapes):
    flat = buf.reshape(-1)
    out, off = [], 0
    for s in shapes:
        n = int(np.prod(s))
        out.append(flat[off:off + n].reshape(s))
        off += n
    return out


def adamw_packed(gparts, w, m, v, name):
    P, R, _ = gparts.shape

    def body(g_ref, w_ref, m_ref, v_ref):
        g = g_ref[0]
        for p in range(1, P):
            g = g + g_ref[p]
        return _adam_update(g, w_ref[...], m_ref[...], v_ref[...])

    whole = pl.BlockSpec((R, LANES), lambda: (0, 0))
    out = (_sds((R, LANES)), whole, None)
    return tilecall(body, name, (), [(gparts, pl.BlockSpec((P, R, LANES), lambda: (0, 0, 0))), (w, whole), (m, whole), (v, whole)],
                    [out, out, out, out], ())


def _residual_out(a, w, x, next_gain):
    if next_gain is None:
        return mm(a, w, "nn", "mm_nn_add", add=x), None
    return mm(a, w, "nn", "mm_nn_add_rms", add=x, rms_gain=next_gain)


def _input_grad(pieces, x, gain, res):
    dh = None
    for d, w in pieces[:-1]:
        dh = mm(d, w, "nt", "mm_nt" if dh is None else "mm_nt_add", add=dh)
    d, w = pieces[-1]
    return mm(d, w, "nt", "mm_nt_norm_bwd", add=dh, norm_bwd=(x, gain, res))


def _ffn_fwd(x, h, next_gain, layer, gain, wts, cw8, cb):
    u = mm(h, win(wts["ffn_w_up"], layer), "nn", "mm_nn")
    a = ffn_act_fwd(u, cw8, cb, "ffn_act_fwd")
    y, h_next = _residual_out(a, win(wts["ffn_w_down"], layer), x, next_gain)
    return y, h_next, (x, h, u, a)


def _ffn_bwd(dy, saved, layer, gain, wts, cw8, cb, grads):
    x, h, u, a = saved
    grads["ffn_w_down"] = mm(a, dy, "tn", "mm_tn_into", into=win(grads["ffn_w_down"], layer))
    da = mm(dy, win(wts["ffn_w_down"], layer), "nt", "mm_nt")
    dyg, dyu, dcw_g, dcw_u, dcb_g, dcb_u = ffn_act_bwd(u, cw8, cb, da, "ffn_act_bwd")
    pieces = []
    for half, du in enumerate(ffn_conv_bwd(dyg, dyu, cw8, "ffn_conv_bwd")):
        cols = dict(col_off=half * D_FF, cols=D_FF)
        grads["ffn_w_up"] = mm(h, du, "tn", "mm_tn_into", into=win(grads["ffn_w_up"], layer, **cols))
        pieces.append((du, win(wts["ffn_w_up"], layer, **cols)))
    dx, d_gain = _input_grad(pieces, x, gain, dy)
    return dx, dict(gain=d_gain, conv_w=jnp.concatenate([dcw_g[0:3], dcw_u[0:3]], axis=1),
                    conv_b=jnp.concatenate([dcb_g, dcb_u], axis=1))


def _hgrn_w_in(wts, j, k):
    return win(wts["hgrn_w_in"], j, row_off=k * D_MODEL, rows=D_MODEL)


def _hgrn_layer_fwd(x, h, next_gain, j, gain, wts, lb, out_gain, mall):
    z = [mm(h, _hgrn_w_in(wts, j, k), "nn", "mm_nn") for k in range(4)]
    o, on, states = hgrn_fwd(z[0], z[1], z[2], z[3], lb, out_gain, mall, "hgrn_fwd")
    y, h_next = _residual_out(on, win(wts["hgrn_w_out"], j), x, next_gain)
    return y, h_next, (x, h, z, o, states, on)


def _hgrn_layer_bwd(dy, saved, j, gain, wts, lb, out_gain, mall, grads):
    x, h, z, o, states, on = saved
    grads["hgrn_w_out"] = mm(on, dy, "tn", "mm_tn_into", into=win(grads["hgrn_w_out"], j))
    don = mm(dy, win(wts["hgrn_w_out"], j), "nt", "mm_nt")
    dzq, dzf, dzi, dzg, dlb, d_out_gain = hgrn_bwd(z[0], z[1], z[2], z[3], lb, out_gain, mall, states, o, don, "hgrn_bwd")
    dz = [dzq, dzf, dzi, dzg]
    for k, d in enumerate(dz):
        grads["hgrn_w_in"] = mm(h, d, "tn", "mm_tn_into", into=win(grads["hgrn_w_in"], j, row_off=k * D_MODEL, rows=D_MODEL))
    dx, d_gain = _input_grad([(d, _hgrn_w_in(wts, j, k)) for k, d in enumerate(dz)], x, gain, dy)
    return dx, dict(gain=d_gain, lb=dlb, out_gain=d_out_gain)


_MLA_IN_WINDOWS = ((0, MLA_LORA), (MLA_LORA, MLA_LORA), (2 * MLA_LORA, HEAD_DIM))


def _mla_layer_fwd(x, h, next_gain, j, gain, wts, qa_gain, kva_gain, qn, kn, cos_t, sin_t):
    (c0, n), (c1, n1), (c2, n2) = _MLA_IN_WINDOWS
    cq, cqn = mm(h, win(wts["mla_w_in"], j, col_off=c0, cols=n), "nn", "mm_nn_rms", rms_gain=qa_gain)
    ckv, ckvn = mm(h, win(wts["mla_w_in"], j, col_off=c1, cols=n1), "nn", "mm_nn_rms", rms_gain=kva_gain)
    kr = mm(h, win(wts["mla_w_in"], j, col_off=c2, cols=n2), "nn", "mm_nn")
    qslots = mm(cqn, win(wts["mla_w_q_up"], j), "nn", "mm_nn")
    kv = mm(ckvn, win(wts["mla_w_kv_up"], j), "nn", "mm_nn")
    qr, krot = qk_fwd(qslots, kv, kr, qn, kn, cos_t, sin_t, "qk_fwd")
    o, lse = attn_fwd(qr, krot, kv, "attn_fwd")
    y, h_next = _residual_out(o, win(wts["mla_w_out"], j), x, next_gain)
    return y, h_next, (x, h, cq, ckv, kr, cqn, ckvn, qslots, kv, qr, krot, o, lse)


def _mla_layer_bwd(dy, saved, j, gain, wts, qa_gain, kva_gain, qn, kn, cos_t, sin_t, grads):
    x, h, cq, ckv, kr, cqn, ckvn, qslots, kv, qr, krot, o, lse = saved
    grads["mla_w_out"] = mm(o, dy, "tn", "mm_tn_into", into=win(grads["mla_w_out"], j))
    do = mm(dy, win(wts["mla_w_out"], j), "nt", "mm_nt")
    dq, dk, dv = attn_bwd(qr, krot, kv, o, lse, do, "attn_bwd")
    dqslots, dkv, dkr, d_qn, d_kn = qk_bwd(qslots, kv, kr, qn, kn, cos_t, sin_t, dq, dk, dv, "qk_bwd")
    grads["mla_w_q_up"] = mm(cqn, dqslots, "tn", "mm_tn_into", into=win(grads["mla_w_q_up"], j))
    dcq, d_qa = _input_grad([(dqslots, win(wts["mla_w_q_up"], j))], cq, qa_gain, None)
    grads["mla_w_kv_up"] = mm(ckvn, dkv, "tn", "mm_tn_into", into=win(grads["mla_w_kv_up"], j))
    dckv, d_kva = _input_grad([(dkv, win(wts["mla_w_kv_up"], j))], ckv, kva_gain, None)
    pieces = []
    for d, (c0, n) in zip((dcq, dckv, dkr), _MLA_IN_WINDOWS):
        grads["mla_w_in"] = mm(h, d, "tn", "mm_tn_into", into=win(grads["mla_w_in"], j, col_off=c0, cols=n))
        pieces.append((d, win(wts["mla_w_in"], j, col_off=c0, cols=n)))
    dx, d_gain = _input_grad(pieces, x, gain, dy)
    return dx, dict(gain=d_gain, qa=d_qa, kva=d_kva, qn=d_qn, kn=d_kn)


BIG = (("hgrn_w_in", "rows"), ("hgrn_w_out", "rows"), ("mla_w_in", "rows"), ("mla_w_q_up", "cols"),
       ("mla_w_kv_up", "cols"), ("mla_w_out", "rows"), ("ffn_w_up", "cols"), ("ffn_w_down", "rows"))
SMALL_SHARDED = (("mla_q_a_norm", 1), ("mla_kv_a_norm", 1), ("ffn_conv_w", 2))
REPLICATED = ("norm_mix", "norm_ffn", "hgrn_lower_bounds", "hgrn_out_norm", "mla_q_norm", "mla_k_norm", "ffn_conv_b")
WEIGHTS = ("norm_mix", "norm_ffn", "hgrn_w_in", "hgrn_lower_bounds", "hgrn_out_norm", "hgrn_w_out", "mla_w_in",
           "mla_q_a_norm", "mla_w_q_up", "mla_kv_a_norm", "mla_w_kv_up", "mla_q_norm", "mla_k_norm", "mla_w_out",
           "ffn_w_up", "ffn_conv_w", "ffn_conv_b", "ffn_w_down")


def _pad_cols(a, width):
    return jnp.pad(a, [(0, 0)] * (a.ndim - 1) + [(0, width - a.shape[-1])])


def _head_slots(w):
    lead, n = w.shape[:-1], w.shape[-1] // MLA_QK
    return _pad_cols(w.reshape(lead + (n, MLA_QK)), MLA_SLOT).reshape(lead + (n * MLA_SLOT,))


def _head_unslots(w):
    lead, n = w.shape[:-1], w.shape[-1] // MLA_SLOT
    return w.reshape(lead + (n, MLA_SLOT))[..., :MLA_QK].reshape(lead + (n * MLA_QK,))


def _to_stack_layout(name, a):
    if name == "hgrn_w_in":
        return a
    if name == "mla_w_in":
        return _pad_cols(a, MLA_IN_COLS)
    if name == "mla_w_q_up":
        return _head_slots(a)
    return a


def _from_stack_layout(name, a):
    if name == "mla_w_in":
        return a[..., :2 * MLA_LORA + MLA_ROPE]
    if name == "mla_w_q_up":
        return _head_unslots(a)
    return a


def _rope_tables(positions):
    inv_freq = ROPE_THETA ** (-jnp.arange(0, MLA_ROPE, 2, dtype=F32) / MLA_ROPE)
    ang = positions.astype(F32)[:, None] * inv_freq
    cos, sin = jnp.cos(ang), jnp.sin(ang)
    S = positions.shape[0]
    ones, zeros = jnp.ones((S, MLA_NOPE), F32), jnp.zeros((S, MLA_SLOT - MLA_QK), F32)
    return (jnp.concatenate([ones, cos, cos, zeros], axis=1),
            jnp.concatenate([jnp.zeros((S, MLA_NOPE), F32), -sin, sin, zeros], axis=1))


def kernel(x, positions, norm_mix, norm_ffn, hgrn_w_in, hgrn_lower_bounds, hgrn_out_norm, hgrn_w_out, mla_w_in, mla_q_a_norm, mla_w_q_up, mla_kv_a_norm, mla_w_kv_up, mla_q_norm, mla_k_norm, mla_w_out, ffn_w_up, ffn_conv_w, ffn_conv_b, ffn_w_down, loss_target, m_norm_mix, m_norm_ffn, m_hgrn_w_in, m_hgrn_lower_bounds, m_hgrn_out_norm, m_hgrn_w_out, m_mla_w_in, m_mla_q_a_norm, m_mla_w_q_up, m_mla_kv_a_norm, m_mla_w_kv_up, m_mla_q_norm, m_mla_k_norm, m_mla_w_out, m_ffn_w_up, m_ffn_conv_w, m_ffn_conv_b, m_ffn_w_down, v_norm_mix, v_norm_ffn, v_hgrn_w_in, v_hgrn_lower_bounds, v_hgrn_out_norm, v_hgrn_w_out, v_mla_w_in, v_mla_q_a_norm, v_mla_w_q_up, v_mla_kv_a_norm, v_mla_w_kv_up, v_mla_q_norm, v_mla_k_norm, v_mla_w_out, v_ffn_w_up, v_ffn_conv_w, v_ffn_conv_b, v_ffn_w_down):
    args = dict(locals())
    w = {n: args[n] for n in WEIGHTS}
    m = {n: args["m_" + n] for n in WEIGHTS}
    v = {n: args["v_" + n] for n in WEIGHTS}
    depth = norm_mix.shape[0]
    x0 = x[0]
    S = x0.shape[0]
    chip = (2 * lax.axis_index("x") + lax.axis_index("y")).astype(jnp.int32).reshape(1)
    core = lax.axis_index("c").astype(jnp.int32).reshape(1)
    big_names = [n for n, _ in BIG]
    kinds = [k for _, k in BIG]
    small_names = [n for n, _ in SMALL_SHARDED]
    small_axis = dict(SMALL_SHARDED)

    local = {n: _to_stack_layout(n, w[n]) for n in big_names}
    gathered = gather_big([local[n].astype(_MXU) for n in big_names], kinds, "gather_weights")
    wts = {n: place(g, local[n], k, chip, "place") for n, k, g in zip(big_names, kinds, gathered)}
    (got_small,) = exchange([_pack([w[n] for n in small_names], F32, SUBLANES)], "chips", [None], "gather_small")
    per_chip = [_unpack(got_small[p], [w[n].shape for n in small_names]) for p in range(N_CHIPS)]
    small = {n: jnp.concatenate([per_chip[p][k] for p in range(N_CHIPS)], axis=small_axis[n]) for k, n in enumerate(small_names)}

    cos_t, sin_t = _rope_tables(positions[0])
    lbs = lower_bound_fwd(hgrn_lower_bounds, "lower_bound_fwd")
    mall = jnp.asarray(_hgrn_sum_matrix(min(HGRN_CHUNK, S)), _MXU)
    qn = _pad_cols(mla_q_norm, MLA_SLOT)
    kn = _pad_cols(mla_k_norm, MLA_SLOT)
    cw8 = jnp.pad(small["ffn_conv_w"], ((0, 0), (0, SUBLANES - 3), (0, 0)))

    def mixer_args(layer):
        j = layer // 2
        if layer % 2 == 0:
            return (j, norm_mix[layer:layer + 1], wts, lbs[j:j + 1], hgrn_out_norm[j:j + 1], mall)
        return (j, norm_mix[layer:layer + 1], wts, small["mla_q_a_norm"][j:j + 1], small["mla_kv_a_norm"][j:j + 1],
                qn[j:j + 1], kn[j:j + 1], cos_t, sin_t)

    def ffn_args(layer):
        return (layer, norm_ffn[layer:layer + 1], wts, cw8[layer], ffn_conv_b[layer:layer + 1])

    xc, h = x0, rms_fwd(x0, norm_mix[0:1], "rms_fwd")
    saved = []
    for layer in range(depth):
        fwd = _hgrn_layer_fwd if layer % 2 == 0 else _mla_layer_fwd
        xc, h, s_mix = fwd(xc, h, norm_ffn[layer:layer + 1], *mixer_args(layer))
        xc, h, s_ffn = _ffn_fwd(xc, h, norm_mix[layer + 1:layer + 2] if layer + 1 < depth else None, *ffn_args(layer))
        saved.append((s_mix, s_ffn))

    dh, loss_blk = loss_head(xc, loss_target[0], "loss_head")
    loss = lax.psum(loss_blk[0, 0], MESH_AXES)

    grads = {n: lax.empty(g.shape, F32) for n, g in zip(big_names, gathered)}
    g_mix, g_ffn = [None] * depth, [None] * depth
    for layer in reversed(range(depth)):
        s_mix, s_ffn = saved[layer]
        dh, g_ffn[layer] = _ffn_bwd(dh, s_ffn, *ffn_args(layer), grads)
        bwd = _hgrn_layer_bwd if layer % 2 == 0 else _mla_layer_bwd
        dh, g_mix[layer] = bwd(dh, s_mix, *mixer_args(layer), grads)
    hg = [g_mix[l] for l in range(0, depth, 2)]
    mg = [g_mix[l] for l in range(1, depth, 2)]
    d_p = lower_bound_bwd(hgrn_lower_bounds, hg[1]["lb"], "lower_bound_bwd")

    g_list = [grads[n] for n in big_names]
    from_core = send_other_half(g_list, "reduce_cores_in")
    chip_sums = [add_cores(g, o, core, "add_cores") for g, o in zip(g_list, from_core)]
    from_chips = scatter_to_chips(chip_sums, kinds, "reduce_chips")
    reduced = [add_chips(own, got, k, chip, "add_chips") for own, got, k in zip(chip_sums, from_chips, kinds)]
    other_half = exchange(reduced, "sibling", [None] * len(reduced), "reduce_cores_out", keep_own=False)
    big_out = {}
    for n, mine, theirs in zip(big_names, reduced, other_half):
        outs = adamw_stacked(mine, theirs, local[n], _to_stack_layout(n, m[n]), _to_stack_layout(n, v[n]), core, "adamw")
        big_out[n] = [_from_stack_layout(n, o) for o in outs]

    small_grads = {
        "norm_mix": jnp.concatenate([g["gain"] for g in g_mix], axis=0),
        "norm_ffn": jnp.concatenate([g["gain"] for g in g_ffn], axis=0),
        "hgrn_lower_bounds": d_p[0:2],
        "hgrn_out_norm": jnp.concatenate([g["out_gain"] for g in hg], axis=0),
        "mla_q_a_norm": jnp.concatenate([g["qa"] for g in mg], axis=0),
        "mla_kv_a_norm": jnp.concatenate([g["kva"] for g in mg], axis=0),
        "mla_q_norm": jnp.concatenate([g["qn"][:, :MLA_QK] for g in mg], axis=0),
        "mla_k_norm": jnp.concatenate([g["kn"][:, :MLA_QK] for g in mg], axis=0),
        "ffn_conv_w": jnp.stack([g["conv_w"] for g in g_ffn]),
        "ffn_conv_b": jnp.concatenate([g["conv_b"] for g in g_ffn], axis=0),
    }

    def chip_part(n, p):
        size = w[n].shape[small_axis[n]]
        return lax.slice_in_dim(small_grads[n], p * size, (p + 1) * size, axis=small_axis[n])

    to_chips = jnp.stack([_pack([chip_part(n, p) for n in small_names], F32, SUBLANES) for p in range(N_CHIPS)])
    rep_g, shard_g = exchange([_pack([small_grads[n] for n in REPLICATED], F32, SUBLANES), to_chips], "all",
                              [None, (2, 1, 0)], "reduce_small")
    small_out = {}
    for names, gparts in ((REPLICATED, rep_g), (small_names, shard_g)):
        packed = adamw_packed(gparts, *[_pack([t[n] for n in names], F32, SUBLANES) for t in (w, m, v)], "adamw_small")
        unpacked = [_unpack(buf, [w[n].shape for n in names]) for buf in packed]
        for k, n in enumerate(names):
            small_out[n] = [u[k] for u in unpacked]

    result = [loss, dh[None]]
    for k in range(4):
        result += [(big_out[n] if n in big_out else small_out[n])[k] for n in WEIGHTS]
    return tuple(result)
```

```python
import numpy as np
import jax
import jax.numpy as jnp
from jax import lax
from jax.experimental import pallas as pl
from jax.experimental.pallas import tpu as pltpu

F32 = jnp.float32
BF16 = jnp.bfloat16
_MXU = BF16

RMS_EPS = 1e-6
D_MODEL = 1024
HEADS = 8
HEAD_DIM = 128
HGRN_CHUNK = 128
MLA_NOPE = 128
MLA_ROPE = 64
MLA_QK = MLA_NOPE + MLA_ROPE
MLA_SLOT = 256
MLA_LORA = 256
MLA_IN_COLS = 2 * MLA_LORA + HEAD_DIM
ROPE_THETA = 10000.0
D_FF = 2816
FF_BLOCK = 1408
LANES = 128
SUBLANES = 8

ADAM_LR = 0.001
ADAM_B1 = 0.9
ADAM_B2 = 0.999
ADAM_EPS = 1e-08
ADAM_WD = 0.01
ADAM_STEP = 10

VMEM_LIMIT = 56 * 1024 * 1024
MM_VMEM_BUDGET = 46 * 1024 * 1024
MESH_AXES = ("x", "y", "c")
N_CHIPS = 4

_NN = ((1,), (0,))
_NT = ((1,), (1,))
_TN = ((0,), (0,))


def _dg(a, b, dims):
    return lax.dot_general(a.astype(_MXU), b.astype(_MXU), (dims, ((), ())), preferred_element_type=F32)


@jax.custom_vjp
def kdot(a, b):
    return _dg(a, b, _NN)


kdot.defvjp(lambda a, b: (_dg(a, b, _NN), (a, b)), lambda r, g: (_dg(g, r[1], _NT), _dg(r[0], g, _TN)))


@jax.custom_vjp
def kdot_nt(a, b):
    return _dg(a, b, _NT)


kdot_nt.defvjp(lambda a, b: (_dg(a, b, _NT), (a, b)), lambda r, g: (_dg(g, r[1], _NN), _dg(g, r[0], _TN)))


@jax.custom_vjp
def kdot_tn(a, b):
    return _dg(a, b, _TN)


kdot_tn.defvjp(lambda a, b: (_dg(a, b, _TN), (a, b)), lambda r, g: (_dg(r[1], g, _NT), _dg(r[0], g, _NN)))


def _pick(d, prefs):
    for p in prefs:
        if d >= p and d % p == 0:
            return p
    return d


def _params(sem):
    return pltpu.CompilerParams(dimension_semantics=sem, vmem_limit_bytes=VMEM_LIMIT)


def _sds(shape, dtype=F32):
    return jax.ShapeDtypeStruct(shape, dtype)


def win(arr, layer, row_off=0, col_off=0, rows=None, cols=None):
    return (arr, layer, row_off, col_off, rows or arr.shape[1] - row_off, cols or arr.shape[2] - col_off)


def mm(a, b, mode, name, add=None, out_dtype=F32, into=None, rms_gain=None, norm_bwd=None):
    if isinstance(b, tuple):
        b_arr, b_layer, b_r0, b_c0, b_rows, b_cols = b
    else:
        b_arr, b_layer, b_r0, b_c0, (b_rows, b_cols) = b, None, 0, 0, b.shape
    if mode == "nn":
        (M, K), (K2, N) = a.shape, (b_rows, b_cols)
    elif mode == "nt":
        (M, K), (N, K2) = a.shape, (b_rows, b_cols)
    else:
        (K, M), (K2, N) = a.shape, (b_rows, b_cols)
    assert K == K2, (name, a.shape, b_rows, b_cols)
    tn = N if N <= 1024 else _pick(N, (1024, 1408, 512, 256, 128))
    tk = K if K <= 2048 else _pick(K, (2048, 2816, 1024, 512, 256, 128))
    nk = K // tk
    a_bytes = jnp.dtype(a.dtype).itemsize
    b_bytes = jnp.dtype(b_arr.dtype).itemsize
    extra_tiles = (add is not None) + (rms_gain is not None) / 2 + (3 if norm_bwd is not None else 0)

    def vmem_bytes(tm_):
        tiles = 2 * (tm_ * tk * a_bytes + tk * tn * b_bytes) + tm_ * tn * 4 * (2 + 2 * extra_tiles + (nk > 1))
        return tiles

    tm = M
    if M > 1024:
        fits = [t for t in (2048, 1024, 1408, 512, 256, 128) if M % t == 0 and vmem_bytes(t) <= MM_VMEM_BUDGET]
        tm = fits[0] if fits else _pick(M, (128,))
    dims = {"nn": _NN, "nt": _NT, "tn": _TN}[mode]
    a_spec = pl.BlockSpec((tk, tm), lambda i, j, k: (k, i)) if mode == "tn" else pl.BlockSpec((tm, tk), lambda i, j, k: (i, k))
    b_blk = (tn, tk) if mode == "nt" else (tk, tn)
    assert b_r0 % b_blk[0] == 0 and b_c0 % b_blk[1] == 0, (name, b_r0, b_c0, b_blk)
    br, bc = b_r0 // b_blk[0], b_c0 // b_blk[1]
    if mode == "nt":
        b_idx = lambda i, j, k: (br + j, bc + k)
    else:
        b_idx = lambda i, j, k: (br + k, bc + j)
    if b_layer is None:
        b_spec = pl.BlockSpec(b_blk, b_idx)
    else:
        b_spec = pl.BlockSpec((None,) + b_blk, lambda i, j, k: (b_layer,) + b_idx(i, j, k))
    plain = pl.BlockSpec((tm, tn), lambda i, j, k: (i, j))
    has_add, has_rms, has_nb = add is not None, rms_gain is not None, norm_bwd is not None
    assert not (has_rms or has_nb) or (tn == N and into is None and not (has_rms and has_nb)), name
    vec = pl.BlockSpec((1, tn), lambda i, j, k: (0, j))
    ins = [a, b_arr] + ([add] if has_add else []) + ([rms_gain] if has_rms else [])
    specs = [a_spec, b_spec] + ([plain] if has_add else []) + ([vec] if has_rms else [])
    n_in = len(ins)
    if has_nb:
        nb_x, nb_gain, nb_res = norm_bwd
        ins += [nb_x, nb_gain] + ([nb_res] if nb_res is not None else [])
        specs += [plain, vec] + ([plain] if nb_res is not None else [])
    aliases = {}
    if into is None:
        o_spec, out_shape = plain, _sds((M, N), out_dtype)
    else:
        buf, o_layer, o_r0, o_c0, o_rows, o_cols = into
        assert (o_rows, o_cols) == (M, N) and o_r0 % tm == 0 and o_c0 % tn == 0, (name, into[1:], M, N, tm, tn)
        orow, ocol = o_r0 // tm, o_c0 // tn
        o_spec = pl.BlockSpec((None, tm, tn), lambda i, j, k: (o_layer, orow + i, ocol + j))
        out_shape = _sds(buf.shape, buf.dtype)
        aliases = {len(ins): 0}
        ins.append(buf)
        specs.append(pl.BlockSpec(memory_space=pl.ANY))

    n_all_in = len(ins)

    def kern(*refs):
        a_ref, b_ref = refs[0], refs[1]
        add_ref = refs[2] if has_add else None
        o_ref = refs[n_all_in]

        def finish(r):
            if has_add:
                r = r + add_ref[...].astype(F32)
            if has_nb:
                _, vjp = jax.vjp(lambda xv, gv: _rms(xv, gv, N), refs[n_in][...], refs[n_in + 1][...])
                dx, dgain = vjp(r)
                o_ref[...] = dx if nb_res is None else dx + refs[n_in + 2][...]
                _store(refs[n_all_in + 1], dgain, pl.program_id(0) == 0)
                return
            o_ref[...] = r.astype(o_ref.dtype)
            if has_rms:
                refs[n_all_in + 1][...] = _rms(r, refs[n_in - 1][...], N).astype(_MXU)

        if nk == 1:
            finish(_dg(a_ref[...], b_ref[...], dims))
            return
        acc = refs[-1]
        k = pl.program_id(2)

        @pl.when(k == 0)
        def _():
            acc[...] = jnp.zeros_like(acc)

        acc[...] += _dg(a_ref[...], b_ref[...], dims)

        @pl.when(k == nk - 1)
        def _():
            finish(acc[...])

    if has_rms:
        o_spec, out_shape = [o_spec, plain], [out_shape, _sds((M, N), _MXU)]
    if has_nb:
        o_spec, out_shape = [o_spec, vec], [out_shape, _sds((1, N))]
    return pl.pallas_call(
        kern, name=name, grid=(M // tm, N // tn, nk), in_specs=specs, out_specs=o_spec, out_shape=out_shape,
        scratch_shapes=[pltpu.VMEM((tm, tn), F32)] if nk > 1 else [], input_output_aliases=aliases,
        compiler_params=_params(("arbitrary" if has_nb else "parallel", "parallel", "arbitrary")))(*ins)


def _store(ref, val, first):
    if first is None:
        ref[...] = val.astype(ref.dtype)
        return

    @pl.when(first)
    def _():
        ref[...] = val.astype(ref.dtype)

    @pl.when(jnp.logical_not(first))
    def _():
        ref[...] += val.astype(ref.dtype)


def tilecall(body, name, grid, ins, outs, sem, prefetch=(), aliases=None):
    n_pre, n_in = len(prefetch), len(ins)

    def kern(*refs):
        vals = body(*refs[:n_pre + n_in])
        for ref, val, (_, _, first) in zip(refs[n_pre + n_in:], vals, outs):
            _store(ref, val, None if first is None else first())

    in_specs, out_specs = [s for _, s in ins], [s for _, s, _ in outs]
    kwargs = dict(name=name, out_shape=[sh for sh, _, _ in outs], compiler_params=_params(sem),
                  input_output_aliases={n_pre + k: v for k, v in (aliases or {}).items()})
    if n_pre:
        kwargs["grid_spec"] = pltpu.PrefetchScalarGridSpec(num_scalar_prefetch=n_pre, grid=grid, in_specs=in_specs,
                                                           out_specs=out_specs)
    else:
        kwargs.update(grid=grid, in_specs=in_specs, out_specs=out_specs)
    return pl.pallas_call(kern, **kwargs)(*prefetch, *[a for a, _ in ins])


def _rms(x, g, n):
    ms = jnp.sum(x * x, axis=-1, keepdims=True) / n
    return x * lax.rsqrt(ms + RMS_EPS) * g


def _row_tile(S, w):
    return min(S, 512 if w <= 1024 else 256)


def rms_fwd(x, g, name, col=0, w=None):
    S = x.shape[0]
    w = w or x.shape[1]
    ts = _row_tile(S, w)
    return tilecall(
        lambda x_ref, g_ref: (_rms(x_ref[...], g_ref[...], w),), name, (S // ts,),
        [(x, pl.BlockSpec((ts, w), lambda i: (i, col))), (g, pl.BlockSpec((1, w), lambda i: (0, 0)))],
        [(_sds((S, w), _MXU), pl.BlockSpec((ts, w), lambda i: (i, 0)), None)], ("parallel",))[0]


HALO_ROWS = 16


def _shifted(u, halo_ref, is_first):
    rid = lax.broadcasted_iota(jnp.int32, (SUBLANES, 1), 0)
    halo = halo_ref[...].astype(F32)
    hrow = lax.broadcasted_iota(jnp.int32, (HALO_ROWS, 1), 0)

    def halo_row(r):
        return jnp.where(is_first, 0.0, jnp.sum(jnp.where(hrow == r, halo, 0.0), axis=0, keepdims=True))

    h7, h6 = halo_row(HALO_ROWS - 1), halo_row(HALO_ROWS - 2)
    r1, r2 = pltpu.roll(u, 1, 0), pltpu.roll(u, 2, 0)
    top1 = jnp.where(rid == 0, h7, r1[:SUBLANES])
    top2 = jnp.where(rid == 0, h6, jnp.where(rid == 1, h7, r2[:SUBLANES]))
    return jnp.concatenate([top1, r1[SUBLANES:]], axis=0), jnp.concatenate([top2, r2[SUBLANES:]], axis=0)


def _conv(u, u1, u2, cw_ref, cb_ref):
    return ((cb_ref[...] + u2 * cw_ref[0:1, :]) + u1 * cw_ref[1:2, :]) + u * cw_ref[2:3, :]


def _ffn_specs(S, ts, jmap):
    hb = ts // HALO_ROWS
    return (pl.BlockSpec((ts, FF_BLOCK), lambda j, i: (i, jmap(j))),
            pl.BlockSpec((HALO_ROWS, FF_BLOCK), lambda j, i: (jnp.maximum(i * hb - 1, 0), jmap(j))),
            pl.BlockSpec((SUBLANES, FF_BLOCK), lambda j, i: (0, jmap(j))),
            pl.BlockSpec((1, FF_BLOCK), lambda j, i: (0, jmap(j))))


def ffn_act_fwd(u, cw8, cb, name):
    S = u.shape[0]
    ts = _row_tile(S, 2 * D_FF)
    nb = D_FF // FF_BLOCK

    def body(ug, hg, cwg, cbg, uu, hu, cwu, cbu):
        first = pl.program_id(1) == 0
        g = ug[...].astype(F32)
        g1, g2 = _shifted(g, hg, first)
        yg = _conv(g, g1, g2, cwg, cbg)
        v = uu[...].astype(F32)
        v1, v2 = _shifted(v, hu, first)
        yu = _conv(v, v1, v2, cwu, cbu)
        return (yg * jax.nn.sigmoid(yg) * yu,)

    sg = _ffn_specs(S, ts, lambda j: j)
    su = _ffn_specs(S, ts, lambda j: j + nb)
    ins = [(u, sg[0]), (u, sg[1]), (cw8, sg[2]), (cb, sg[3]), (u, su[0]), (u, su[1]), (cw8, su[2]), (cb, su[3])]
    return tilecall(body, name, (nb, S // ts), ins,
                    [(_sds((S, D_FF), _MXU), pl.BlockSpec((ts, FF_BLOCK), lambda j, i: (i, j)), None)],
                    ("parallel", "parallel"))[0]


def ffn_act_bwd(u, cw8, cb, da, name):
    S = u.shape[0]
    ts = _row_tile(S, 2 * D_FF)
    nb = D_FF // FF_BLOCK

    def taps(dy, x, x1, x2):
        return jnp.concatenate(
            [jnp.sum(dy * x2, axis=0, keepdims=True), jnp.sum(dy * x1, axis=0, keepdims=True),
             jnp.sum(dy * x, axis=0, keepdims=True), jnp.zeros((SUBLANES - 3, dy.shape[1]), F32)], axis=0)

    def body(ug, hg, cwg, cbg, uu, hu, cwu, cbu, da_ref):
        first = pl.program_id(1) == 0
        g = ug[...].astype(F32)
        g1, g2 = _shifted(g, hg, first)
        yg = _conv(g, g1, g2, cwg, cbg)
        v = uu[...].astype(F32)
        v1, v2 = _shifted(v, hu, first)
        yu = _conv(v, v1, v2, cwu, cbu)
        d = da_ref[...].astype(F32)
        sg = jax.nn.sigmoid(yg)
        dyg = d * yu * (sg * (1.0 + yg * (1.0 - sg)))
        dyu = d * (yg * sg)
        return (dyg, dyu, taps(dyg, g, g1, g2), taps(dyu, v, v1, v2),
                jnp.sum(dyg, axis=0, keepdims=True), jnp.sum(dyu, axis=0, keepdims=True))

    sg_ = _ffn_specs(S, ts, lambda j: j)
    su_ = _ffn_specs(S, ts, lambda j: j + nb)
    row = pl.BlockSpec((ts, FF_BLOCK), lambda j, i: (i, j))
    ins = [(u, sg_[0]), (u, sg_[1]), (cw8, sg_[2]), (cb, sg_[3]), (u, su_[0]), (u, su_[1]), (cw8, su_[2]), (cb, su_[3]), (da, row)]
    first_row = lambda: pl.program_id(1) == 0
    dy, dcw, dcb = (_sds((S, D_FF)), row, None), (_sds((SUBLANES, D_FF)), sg_[2], first_row), (_sds((1, D_FF)), sg_[3], first_row)
    return tilecall(body, name, (nb, S // ts), ins, [dy, dy, dcw, dcw, dcb, dcb], ("parallel", "arbitrary"))


def ffn_conv_bwd(dyg, dyu, cw8, name):
    S = dyg.shape[0]
    ts = _row_tile(S, 2 * D_FF)
    hb = ts // SUBLANES
    nrow = S // ts
    nb = D_FF // FF_BLOCK

    def back(dy_ref, halo_ref, cw_ref):
        last = pl.program_id(1) == nrow - 1
        d = dy_ref[...]
        rid = lax.broadcasted_iota(jnp.int32, (SUBLANES, 1), 0)
        n0 = jnp.where(last, 0.0, halo_ref[0:1, :])
        n1 = jnp.where(last, 0.0, halo_ref[1:2, :])
        r1, r2 = pltpu.roll(d, ts - 1, 0), pltpu.roll(d, ts - 2, 0)
        end1 = jnp.where(rid == SUBLANES - 1, n0, r1[ts - SUBLANES:])
        end2 = jnp.where(rid == SUBLANES - 1, n1, jnp.where(rid == SUBLANES - 2, n0, r2[ts - SUBLANES:]))
        d1 = jnp.concatenate([r1[:ts - SUBLANES], end1], axis=0)
        d2 = jnp.concatenate([r2[:ts - SUBLANES], end2], axis=0)
        return d * cw_ref[2:3, :] + d1 * cw_ref[1:2, :] + d2 * cw_ref[0:1, :]

    row = pl.BlockSpec((ts, FF_BLOCK), lambda j, i: (i, j))
    halo = pl.BlockSpec((SUBLANES, FF_BLOCK), lambda j, i: (jnp.minimum((i + 1) * hb, S // SUBLANES - 1), j))
    ins = [(dyg, row), (dyg, halo), (cw8, pl.BlockSpec((SUBLANES, FF_BLOCK), lambda j, i: (0, j))),
           (dyu, row), (dyu, halo), (cw8, pl.BlockSpec((SUBLANES, FF_BLOCK), lambda j, i: (0, j + nb)))]
    out = (_sds((S, D_FF), _MXU), row, None)
    return tilecall(lambda a, b, c, d, e, f: (back(a, b, c), back(d, e, f)), name, (nb, nrow), ins, [out, out],
                    ("parallel", "parallel"))


def _hgrn_levels(C):
    out, m = [], C // 2
    while m >= 1:
        out.append(m)
        m //= 2
    return out


def _hgrn_sum_matrix(C):
    t = np.arange(C)[:, None]
    u = np.arange(C)[None, :]
    blocks = [u <= t, u > t]
    for m in _hgrn_levels(C):
        r = (t // (2 * m)) * (2 * m) + m
        right = (t % (2 * m)) >= m
        blocks.append((right & (u > r) & (u <= t)) | ((~right) & (u > t) & (u <= r)))
    return np.concatenate(blocks, axis=0).astype(np.float32)


def _make_partial_sums(nb, C):
    @jax.custom_vjp
    def sums(mall, lf):
        hi = lf.astype(_MXU)
        mid = (lf - hi.astype(F32)).astype(_MXU)
        e2 = _dg(mall, jnp.concatenate([hi, mid], axis=1), _NN)
        e = e2[:, :HEAD_DIM] + e2[:, HEAD_DIM:]
        return tuple(e[b * C:(b + 1) * C] for b in range(nb))

    def fwd(mall, lf):
        return sums(mall, lf), mall

    def bwd(mall, gs):
        return jnp.zeros_like(mall), _dg(mall, jnp.concatenate(gs, axis=0), _TN)

    sums.defvjp(fwd, bwd)
    return sums


def _hgrn_chunk(zq, zf, v, lb, st, mall, C):
    levels = _hgrn_levels(C)
    qs = zq * jax.nn.sigmoid(zq)
    fg = lb + (1.0 - lb) * jax.nn.sigmoid(zf)
    k = 1.0 - fg
    e = _make_partial_sums(2 + len(levels), C)(mall, jnp.log(fg))
    g_incl, g_after = e[0], e[1]
    rid = lax.broadcasted_iota(jnp.int32, (C, 1), 0)
    tt = lax.broadcasted_iota(jnp.int32, (C, C), 0)
    ss = lax.broadcasted_iota(jnp.int32, (C, C), 1)
    o = kdot_nt(qs * jnp.exp(g_incl), st)
    o = o + jnp.sum(qs * k, axis=-1, keepdims=True) * v
    scores = jnp.zeros((C, C), F32)
    for li, m in enumerate(levels):
        sh = int(np.log2(m))
        right = ((rid >> sh) & 1) == 1
        both = jnp.where(right, qs, k) * jnp.exp(e[2 + li])
        pair = ((tt >> (sh + 1)) == (ss >> (sh + 1))) & (((tt >> sh) & 1) == 1) & (((ss >> sh) & 1) == 0)
        scores = scores + jnp.where(pair, kdot_nt(both, both), 0.0)
    o = o + kdot(scores, v)
    g_last = jnp.sum(jnp.where(rid == C - 1, g_incl, 0.0), axis=0, keepdims=True)
    st_new = st * jnp.exp(g_last) + kdot_tn(v, k * jnp.exp(g_after))
    return o, st_new


HGRN_HEADS_PER_STEP = 8
_HGRN_LANES = HGRN_HEADS_PER_STEP * HEAD_DIM


def _hgrn_in_specs(C, nc, rev):
    cm = (lambda c: nc - 1 - c) if rev else (lambda c: c)
    blk = lambda: pl.BlockSpec((C, _HGRN_LANES), lambda h, c: (cm(c), h))
    return cm, [blk(), blk(), blk(), pl.BlockSpec((1, _HGRN_LANES), lambda h, c: (0, h))]


def _hgrn_state_spec(cm):
    return pl.BlockSpec((HGRN_HEADS_PER_STEP, None, HEAD_DIM, HEAD_DIM), lambda h, c: (h, cm(c), 0, 0))


def _hgrn_out(o, g, gain):
    return _rms(o, gain, HEAD_DIM) * (g * jax.nn.sigmoid(g))


def hgrn_fwd(zq, zf, zi, zg, lb, out_gain, mall, name):
    S = zq.shape[0]
    C = min(HGRN_CHUNK, S)
    nc = S // C

    def kern(zq_ref, zf_ref, zi_ref, lb_ref, zg_ref, gain_ref, mall_ref, o_ref, on_ref, st_ref, st):
        @pl.when(pl.program_id(1) == 0)
        def _():
            st[...] = jnp.zeros_like(st)

        mall_v = mall_ref[...]
        for g in range(HGRN_HEADS_PER_STEP):
            lanes = slice(g * HEAD_DIM, (g + 1) * HEAD_DIM)
            s_in = st[g]
            st_ref[g] = s_in
            o, s_new = _hgrn_chunk(zq_ref[:, lanes], zf_ref[:, lanes], zi_ref[:, lanes], lb_ref[:, lanes], s_in, mall_v, C)
            o_ref[:, lanes] = o
            on_ref[:, lanes] = _hgrn_out(o, zg_ref[:, lanes], gain_ref[...]).astype(on_ref.dtype)
            st[g] = s_new

    cm, specs = _hgrn_in_specs(C, nc, False)
    row = pl.BlockSpec((C, _HGRN_LANES), lambda h, c: (c, h))
    return pl.pallas_call(
        kern, name=name, grid=(HEADS // HGRN_HEADS_PER_STEP, nc),
        in_specs=specs + [row, pl.BlockSpec((1, HEAD_DIM), lambda h, c: (0, 0)), pl.BlockSpec(mall.shape, lambda h, c: (0, 0))],
        out_specs=[row, row, _hgrn_state_spec(cm)],
        out_shape=[_sds((S, D_MODEL)), _sds((S, D_MODEL), _MXU), _sds((HEADS, nc, HEAD_DIM, HEAD_DIM))],
        scratch_shapes=[pltpu.VMEM((HGRN_HEADS_PER_STEP, HEAD_DIM, HEAD_DIM), F32)],
        compiler_params=_params(("parallel", "arbitrary")))(zq, zf, zi, lb, zg, out_gain, mall)


def hgrn_bwd(zq, zf, zi, zg, lb, out_gain, mall, states, o, don, name):
    S = zq.shape[0]
    C = min(HGRN_CHUNK, S)
    nc = S // C

    def kern(zq_ref, zf_ref, zi_ref, lb_ref, zg_ref, gain_ref, mall_ref, st_ref, o_ref, don_ref,
             dq_ref, df_ref, di_ref, dg_ref, dlb_ref, dgain_ref, dst):
        first = pl.program_id(1) == 0

        @pl.when(first)
        def _():
            dst[...] = jnp.zeros_like(dst)

        mall_v = mall_ref[...]
        gls, dgain = [], None
        for g in range(HGRN_HEADS_PER_STEP):
            lanes = slice(g * HEAD_DIM, (g + 1) * HEAD_DIM)
            _, out_vjp = jax.vjp(_hgrn_out, o_ref[:, lanes], zg_ref[:, lanes], gain_ref[...])
            do, dzg, dgn = out_vjp(don_ref[:, lanes])
            dg_ref[:, lanes] = dzg.astype(dg_ref.dtype)
            dgain = dgn if dgain is None else dgain + dgn
            _, vjp = jax.vjp(lambda a, b, c, d, e: _hgrn_chunk(a, b, c, d, e, mall_v, C),
                             zq_ref[:, lanes], zf_ref[:, lanes], zi_ref[:, lanes], lb_ref[:, lanes], st_ref[g])
            ga, gb, gv, gl, gs = vjp((do, dst[g]))
            dq_ref[:, lanes] = ga.astype(dq_ref.dtype)
            df_ref[:, lanes] = gb.astype(df_ref.dtype)
            di_ref[:, lanes] = gv.astype(di_ref.dtype)
            gls.append(gl)
            dst[g] = gs
        _store(dlb_ref, jnp.concatenate(gls, axis=1), first)
        _store(dgain_ref, dgain, first & (pl.program_id(0) == 0))

    cm, specs = _hgrn_in_specs(C, nc, True)
    row = lambda: pl.BlockSpec((C, _HGRN_LANES), lambda h, c: (cm(c), h))
    vec = pl.BlockSpec((1, HEAD_DIM), lambda h, c: (0, 0))
    wide = _sds((S, D_MODEL), _MXU)
    return pl.pallas_call(
        kern, name=name, grid=(HEADS // HGRN_HEADS_PER_STEP, nc),
        in_specs=specs + [row(), vec, pl.BlockSpec(mall.shape, lambda h, c: (0, 0)), _hgrn_state_spec(cm), row(), row()],
        out_specs=[row(), row(), row(), row(), pl.BlockSpec((1, _HGRN_LANES), lambda h, c: (0, h)), vec],
        out_shape=[wide, wide, wide, wide, _sds((1, D_MODEL)), _sds((1, HEAD_DIM))],
        scratch_shapes=[pltpu.VMEM((HGRN_HEADS_PER_STEP, HEAD_DIM, HEAD_DIM), F32)],
        compiler_params=_params(("arbitrary", "arbitrary")))(zq, zf, zi, lb, zg, out_gain, mall, states, o, don)


def _lb_soft(p0, p1):
    mx = jnp.maximum(p0, p1)
    e0, e1 = jnp.exp(p0 - mx), jnp.exp(p1 - mx)
    s0, s1 = e0 / (e0 + e1), e1 / (e0 + e1)
    return (s0 + s1) - s0


def lower_bound_fwd(p, name):
    assert p.shape[0] == 2

    def body(p_ref):
        s = _lb_soft(p_ref[0:1, :], p_ref[1:2, :])
        return (jnp.concatenate([jnp.zeros_like(s), s] + [jnp.zeros_like(s)] * (SUBLANES - 2), axis=0),)

    spec8 = pl.BlockSpec((SUBLANES, p.shape[1]), lambda: (0, 0))
    return tilecall(body, name, (), [(p, pl.BlockSpec(p.shape, lambda: (0, 0)))], [(_sds((SUBLANES, p.shape[1])), spec8, None)], ())[0]


def lower_bound_bwd(p, dlb1, name):
    def body(p_ref, d_ref):
        _, vjp = jax.vjp(_lb_soft, p_ref[0:1, :], p_ref[1:2, :])
        g0, g1 = vjp(d_ref[...])
        return (jnp.concatenate([g0, g1] + [jnp.zeros_like(g0)] * (SUBLANES - 2), axis=0),)

    spec8 = pl.BlockSpec((SUBLANES, p.shape[1]), lambda: (0, 0))
    return tilecall(body, name, (), [(p, pl.BlockSpec(p.shape, lambda: (0, 0))), (dlb1, pl.BlockSpec(dlb1.shape, lambda: (0, 0)))],
                    [(_sds((SUBLANES, p.shape[1])), spec8, None)], ())[0]


@jax.custom_vjp
def _swap_rope_halves(x):
    lane = lax.broadcasted_iota(jnp.int32, x.shape, 1)
    lo = (lane >= MLA_NOPE) & (lane < MLA_NOPE + MLA_ROPE // 2)
    hi = (lane >= MLA_NOPE + MLA_ROPE // 2) & (lane < MLA_QK)
    return jnp.where(lo, pltpu.roll(x, MLA_SLOT - MLA_ROPE // 2, 1), jnp.where(hi, pltpu.roll(x, MLA_ROPE // 2, 1), 0.0))


_swap_rope_halves.defvjp(lambda x: (_swap_rope_halves(x), None), lambda _, g: (_swap_rope_halves(g),))


def _norm_rope(x, gain, cos_t, sin_t):
    y = _rms(x, gain, MLA_QK)
    return y * cos_t + _swap_rope_halves(y) * sin_t


_ATTN_SCALE = MLA_QK ** -0.5


def _qk_heads(qs, kn, kr, qn, kn_gain, cos_t, sin_t):
    q = _norm_rope(qs, qn, cos_t, sin_t) * _ATTN_SCALE
    k = _norm_rope(jnp.concatenate([kn, kr], axis=1), kn_gain, cos_t, sin_t)
    return q, k


def _qk_specs(ts):
    slot = pl.BlockSpec((ts, MLA_SLOT), lambda i, h: (i, h))
    nope = pl.BlockSpec((ts, HEAD_DIM), lambda i, h: (i, 2 * h))
    shared = pl.BlockSpec((ts, HEAD_DIM), lambda i, h: (i, 0))
    gain = pl.BlockSpec((1, MLA_SLOT), lambda i, h: (0, 0))
    table = pl.BlockSpec((ts, MLA_SLOT), lambda i, h: (i, 0))
    return slot, nope, shared, gain, table


QK_ROWS = 1024


def qk_fwd(qslots, kv, krope, qn, kn, cos_t, sin_t, name):
    S = qslots.shape[0]
    ts = min(S, QK_ROWS)
    slot, nope, shared, gain, table = _qk_specs(ts)

    def body(q_ref, kn_ref, kr_ref, qn_ref, kg_ref, c_ref, s_ref):
        return _qk_heads(q_ref[...], kn_ref[...], kr_ref[...], qn_ref[...], kg_ref[...], c_ref[...], s_ref[...])

    out = _sds((S, HEADS * MLA_SLOT), _MXU)
    return tilecall(body, name, (S // ts, HEADS),
                    [(qslots, slot), (kv, nope), (krope, shared), (qn, gain), (kn, gain), (cos_t, table), (sin_t, table)],
                    [(out, slot, None), (out, slot, None)], ("parallel", "parallel"))


def qk_bwd(qslots, kv, krope, qn, kn, cos_t, sin_t, dq, dk, dv, name):
    S = qslots.shape[0]
    ts = min(S, QK_ROWS)
    slot, nope, shared, gain, table = _qk_specs(ts)
    vblk = pl.BlockSpec((ts, HEAD_DIM), lambda i, h: (i, h))

    def body(q_ref, kn_ref, kr_ref, qn_ref, kg_ref, c_ref, s_ref, dq_ref, dk_ref, dv_ref):
        c, s = c_ref[...], s_ref[...]
        _, vjp = jax.vjp(lambda a, b, r, g1, g2: _qk_heads(a, b, r, g1, g2, c, s),
                         q_ref[...], kn_ref[...], kr_ref[...], qn_ref[...], kg_ref[...])
        ga, gb, gr, g1, g2 = vjp((dq_ref[...], dk_ref[...]))
        return ga, jnp.concatenate([gb, dv_ref[...]], axis=1), gr, g1, g2

    first_head = lambda: pl.program_id(1) == 0
    first = lambda: (pl.program_id(0) == 0) & (pl.program_id(1) == 0)
    wide = _sds((S, HEADS * MLA_SLOT), _MXU)
    return tilecall(body, name, (S // ts, HEADS),
                    [(qslots, slot), (kv, nope), (krope, shared), (qn, gain), (kn, gain), (cos_t, table), (sin_t, table),
                     (dq, slot), (dk, slot), (dv, vblk)],
                    [(wide, slot, None), (wide, slot, None), (_sds((S, HEAD_DIM)), shared, first_head),
                     (_sds((1, MLA_SLOT)), gain, first), (_sds((1, MLA_SLOT)), gain, first)], ("arbitrary", "arbitrary"))


ATTN_TILE_FWD = 1024
ATTN_TILE = 1024
ATTN_HEADS_PER_STEP = 2
ATTN_HEADS_PER_STEP_BWD = 1


def _causal_pairs(nq, by_row):
    pairs = [(i, j) for i in range(nq) for j in range(i + 1)] if by_row else [(i, j) for j in range(nq) for i in range(j, nq)]
    return jnp.asarray([p[0] for p in pairs], jnp.int32), jnp.asarray([p[1] for p in pairs], jnp.int32)


def _diag_mask(s, tq):
    rows = lax.broadcasted_iota(jnp.int32, (tq, tq), 0)
    cols = lax.broadcasted_iota(jnp.int32, (tq, tq), 1)
    return jnp.where(rows >= cols, s, -jnp.inf)


def attn_fwd(qr, kr, kv, name):
    S = qr.shape[0]
    tq = min(S, ATTN_TILE_FWD)
    nq = S // tq
    i_tab, j_tab = _causal_pairs(nq, True)

    G = ATTN_HEADS_PER_STEP
    reps = tq // LANES

    def kern(it, jt, q_ref, k_ref, kv_ref, o_ref, lse_ref, m_s, l_s, acc):
        n = pl.program_id(1)
        i, j = it[n], jt[n]

        @pl.when(j == 0)
        def _():
            m_s[...] = jnp.full_like(m_s, -jnp.inf)
            l_s[...] = jnp.zeros_like(l_s)
            acc[...] = jnp.zeros_like(acc)

        def step(diagonal):
            for g in range(G):
                slot = slice(g * MLA_SLOT, (g + 1) * MLA_SLOT)
                s = _dg(q_ref[:, slot], k_ref[:, slot], _NT)
                if diagonal:
                    s = _diag_mask(s, tq)
                m_prev = m_s[g]
                m_new = jnp.maximum(m_prev, jnp.max(s, axis=-1, keepdims=True))
                alpha = jnp.exp(m_prev - m_new)
                p = jnp.exp(s - jnp.tile(m_new, (1, reps)))
                l_s[g] = alpha * l_s[g] + jnp.sum(p, axis=-1, keepdims=True)
                acc[g] = alpha * acc[g] + _dg(p, kv_ref[:, g * MLA_SLOT + HEAD_DIM:(g + 1) * MLA_SLOT], _NN)
                m_s[g] = m_new

        @pl.when(j < i)
        def _():
            step(False)

        @pl.when(j == i)
        def _():
            step(True)
            for g in range(G):
                l = l_s[g]
                lanes = slice(g * HEAD_DIM, (g + 1) * HEAD_DIM)
                o_ref[:, lanes] = acc[g] / l
                lse_ref[:, lanes] = m_s[g] + jnp.log(l)

    out = _sds((S, HEADS * HEAD_DIM))
    oblk = pl.BlockSpec((tq, G * HEAD_DIM), lambda h, n, it, jt: (it[n], h))
    stat = pltpu.VMEM((G, tq, HEAD_DIM), F32)
    return pl.pallas_call(
        kern, name=name,
        grid_spec=pltpu.PrefetchScalarGridSpec(
            num_scalar_prefetch=2, grid=(HEADS // G, i_tab.shape[0]),
            in_specs=[pl.BlockSpec((tq, G * MLA_SLOT), lambda h, n, it, jt: (it[n], h)),
                      pl.BlockSpec((tq, G * MLA_SLOT), lambda h, n, it, jt: (jt[n], h)),
                      pl.BlockSpec((tq, G * MLA_SLOT), lambda h, n, it, jt: (jt[n], h))],
            out_specs=[oblk, oblk], scratch_shapes=[stat, stat, stat]),
        out_shape=[out, out], compiler_params=_params(("parallel", "arbitrary")))(i_tab, j_tab, qr, kr, kv)


def attn_bwd(qr, kr, kv, o, lse, do, name):
    S = qr.shape[0]
    tq = min(S, ATTN_TILE)
    nq = S // tq
    i_tab, j_tab = _causal_pairs(nq, False)

    G = ATTN_HEADS_PER_STEP_BWD

    def kern(it, jt, q_ref, k_ref, kv_ref, o_ref, lse_ref, do_ref, dq_ref, dk_ref, dv_ref, dk_acc, dv_acc):
        n = pl.program_id(1)
        i, j = it[n], jt[n]

        @pl.when(n == 0)
        def _():
            dq_ref[...] = jnp.zeros_like(dq_ref)

        @pl.when(i == j)
        def _():
            dk_acc[...] = jnp.zeros_like(dk_acc)
            dv_acc[...] = jnp.zeros_like(dv_acc)

        def step(diagonal):
            rows = pl.ds(pl.multiple_of(i * tq, tq), tq)
            for g in range(G):
                slot = slice(g * MLA_SLOT, (g + 1) * MLA_SLOT)
                lanes = slice(g * HEAD_DIM, (g + 1) * HEAD_DIM)
                q, k = q_ref[:, slot], k_ref[:, slot]
                s = _dg(q, k, _NT) - jnp.tile(lse_ref[:, lanes], (1, tq // LANES))
                if diagonal:
                    s = _diag_mask(s, tq)
                p = jnp.exp(s)
                d = do_ref[:, lanes]
                delta = jnp.sum(d * o_ref[:, lanes], axis=-1, keepdims=True)
                dv_acc[:, lanes] += _dg(p, d, _TN)
                ds = p * (_dg(d, kv_ref[:, g * MLA_SLOT + HEAD_DIM:(g + 1) * MLA_SLOT], _NT) - delta)
                dk_acc[:, slot] += _dg(ds, q, _TN)
                dq_ref[rows, slot] += _dg(ds, k, _NN)

        @pl.when(i > j)
        def _():
            step(False)

        @pl.when(i == j)
        def _():
            step(True)

        @pl.when(i == nq - 1)
        def _():
            dk_ref[...] = dk_acc[...]
            dv_ref[...] = dv_acc[...]

    qblk = pl.BlockSpec((tq, G * MLA_SLOT), lambda h, n, it, jt: (it[n], h))
    oblk = pl.BlockSpec((tq, G * HEAD_DIM), lambda h, n, it, jt: (it[n], h))
    kblk = pl.BlockSpec((tq, G * MLA_SLOT), lambda h, n, it, jt: (jt[n], h))
    return pl.pallas_call(
        kern, name=name,
        grid_spec=pltpu.PrefetchScalarGridSpec(
            num_scalar_prefetch=2, grid=(HEADS // G, i_tab.shape[0]),
            in_specs=[qblk, kblk, kblk, oblk, oblk, oblk],
            out_specs=[pl.BlockSpec((S, G * MLA_SLOT), lambda h, n, it, jt: (0, h)), kblk,
                       pl.BlockSpec((tq, G * HEAD_DIM), lambda h, n, it, jt: (jt[n], h))],
            scratch_shapes=[pltpu.VMEM((tq, G * MLA_SLOT), F32), pltpu.VMEM((tq, G * HEAD_DIM), F32)]),
        out_shape=[_sds((S, HEADS * MLA_SLOT)), _sds((S, HEADS * MLA_SLOT)), _sds((S, HEADS * HEAD_DIM))],
        compiler_params=_params(("parallel", "arbitrary")))(i_tab, j_tab, qr, kr, kv, o, lse, do)


def loss_head(y, target, name):
    S, Dm = y.shape
    ts = _row_tile(S, Dm)

    def body(y_ref, t_ref):
        e = y_ref[...] - t_ref[...]
        tot = jnp.sum(jnp.sum(e * e, axis=-1, keepdims=True) / Dm, axis=0, keepdims=True)
        return e / Dm, jnp.broadcast_to(0.5 * tot, (SUBLANES, LANES))

    row = pl.BlockSpec((ts, Dm), lambda i: (i, 0))
    return tilecall(body, name, (S // ts,), [(y, row), (target, row)],
                    [(_sds((S, Dm)), row, None),
                     (_sds((SUBLANES, LANES)), pl.BlockSpec((SUBLANES, LANES), lambda i: (0, 0)), lambda: pl.program_id(0) == 0)],
                    ("arbitrary",))


_CHIP_FLIPS = ((1, 0), (0, 1), (1, 1))
_PEER_FLIPS = {
    "chips": ((1, 0, 0), (0, 1, 0), (1, 1, 0)),
    "sibling": ((0, 0, 1),),
    "all": tuple((a, b, c) for a in (0, 1) for b in (0, 1) for c in (0, 1))[1:],
}
_SLOT_WEIGHTS = {"chips": (2, 1, 0), "sibling": (0, 0, 1), "all": (4, 2, 1)}
_HBM = pl.BlockSpec(memory_space=pltpu.HBM)


def _me():
    return lax.axis_index("x"), lax.axis_index("y"), lax.axis_index("c")


def _remote(src, dst, send_sem, recv_sem, peer):
    return pltpu.make_async_remote_copy(src_ref=src, dst_ref=dst, send_sem=send_sem, recv_sem=recv_sem,
                                        device_id=peer, device_id_type=pl.DeviceIdType.MESH)


def exchange(arrs, group, slab_weights, name, keep_own=True):
    flips = _PEER_FLIPS[group]
    wx, wy, wc = _SLOT_WEIGHTS[group]
    n_slots = len(flips) + (1 if keep_own else 0)
    n = len(arrs)

    def slab(ref, a, pos):
        w = slab_weights[a]
        return ref if w is None else ref.at[w[0] * pos[0] + w[1] * pos[1] + w[2] * pos[2]]

    def kern(*refs):
        srcs, outs = refs[:n], refs[n:2 * n]
        send_sems, recv_sems = refs[2 * n:2 * n + 2]
        me = _me()
        my_slot = wx * me[0] + wy * me[1] + wc * me[2]
        copies = []
        if keep_own:
            local_sems = refs[2 * n + 2]
            for a in range(n):
                cp = pltpu.make_async_copy(slab(srcs[a], a, me), outs[a].at[my_slot], local_sems.at[a])
                cp.start()
                copies.append(cp)
        for f, flip in enumerate(flips):
            peer = tuple(m ^ b if b else m for m, b in zip(me, flip))
            for a in range(n):
                cp = _remote(slab(srcs[a], a, peer), outs[a].at[my_slot if keep_own else f],
                             send_sems.at[f, a], recv_sems.at[f, a], peer)
                cp.start()
                copies.append(cp)
        for cp in copies:
            cp.wait()

    out_shape = [_sds((n_slots,) + (a.shape if slab_weights[k] is None else a.shape[1:]), a.dtype) for k, a in enumerate(arrs)]
    sems = [pltpu.SemaphoreType.DMA((len(flips), n)), pltpu.SemaphoreType.DMA((len(flips), n))]
    return pl.pallas_call(
        kern, name=name, in_specs=[_HBM] * n, out_specs=[_HBM] * n, out_shape=out_shape,
        scratch_shapes=sems + ([pltpu.SemaphoreType.DMA((n,))] if keep_own else []))(*arrs)


def _chip_window(ref, kind, size, chip, layers):
    if kind == "rows":
        return ref.at[layers, pl.ds(chip * size, size), :]
    return ref.at[layers, :, pl.ds(pl.multiple_of(chip * size, LANES), size)]


def gather_big(shards, kinds, name):
    n = len(shards)
    fulls = []
    for s, kind in zip(shards, kinds):
        L, r, c = s.shape
        fulls.append(_sds((L, N_CHIPS * r, c) if kind == "rows" else (L, r, N_CHIPS * c), s.dtype))

    def kern(*refs):
        srcs, outs = refs[:n], refs[n:2 * n]
        ici_s, ici_r, relay_s, relay_r, d2d_s, d2d_r = refs[2 * n:]
        x, y, c = _me()
        sibling = (x, y, 1 - c)
        nbrs = ((x ^ 1, y), (x, y ^ 1))
        relay_from = (x ^ (1 - c), y ^ c)
        relay_to = (x ^ c, y ^ (1 - c), c)
        diagonal = (x ^ 1, y ^ 1)

        def window(a, chip):
            L, r, cc = shards[a].shape
            size = r if kinds[a] == "rows" else cc
            return _chip_window(outs[a], kinds[a], size, 2 * chip[0] + chip[1], pl.ds(c * (L // 2), L // 2))

        def direct(a, f):
            L = shards[a].shape[0]
            return _remote(srcs[a].at[pl.ds(c * (L // 2), L // 2)], window(a, (x, y)), ici_s.at[a, f], ici_r.at[a, f],
                           (nbrs[f][0], nbrs[f][1], c))

        def to_sibling(a, k, chip):
            return _remote(window(a, chip), window(a, chip), d2d_s.at[a, k], d2d_r.at[a, k], sibling)

        sends, relays, passed = [], [], []
        for a in range(n):
            for f in range(2):
                cp = direct(a, f)
                cp.start()
                sends.append(cp)
        for a in range(n):
            _remote(window(a, relay_from), window(a, relay_from), ici_s.at[a, c], ici_r.at[a, c], relay_to).wait_recv()
            cp = _remote(window(a, relay_from), window(a, relay_from), relay_s.at[a], relay_r.at[a], relay_to)
            cp.start()
            relays.append(cp)
            passed.append(to_sibling(a, 0, relay_from))
            passed[-1].start()
            _remote(window(a, relay_from), window(a, relay_from), ici_s.at[a, 1 - c], ici_r.at[a, 1 - c], relay_to).wait_recv()
            passed.append(to_sibling(a, 1, (relay_to[0], relay_to[1])))
            passed[-1].start()
        for a in range(n):
            relays[a].wait_recv()
            passed.append(to_sibling(a, 2, diagonal))
            passed[-1].start()
        for cp in sends + relays:
            cp.wait_send()
        for cp in passed:
            cp.wait()

    pair = pltpu.SemaphoreType.DMA((n, 2))
    one = pltpu.SemaphoreType.DMA((n,))
    three = pltpu.SemaphoreType.DMA((n, 3))
    return pl.pallas_call(kern, name=name, in_specs=[_HBM] * n, out_specs=[_HBM] * n, out_shape=fulls,
                          scratch_shapes=[pair, pair, one, one, three, three])(*shards)


def send_other_half(arrs, name):
    n = len(arrs)

    def kern(*refs):
        srcs, outs = refs[:n], refs[n:2 * n]
        send_sems, recv_sems = refs[2 * n:]
        x, y, c = _me()
        copies = []
        for a in range(n):
            hl = arrs[a].shape[0] // 2
            cp = _remote(srcs[a].at[pl.ds((1 - c) * hl, hl)], outs[a], send_sems.at[a], recv_sems.at[a], (x, y, 1 - c))
            cp.start()
            copies.append(cp)
        for cp in copies:
            cp.wait()

    return pl.pallas_call(
        kern, name=name, in_specs=[_HBM] * n, out_specs=[_HBM] * n,
        out_shape=[_sds((a.shape[0] // 2,) + a.shape[1:], a.dtype) for a in arrs],
        scratch_shapes=[pltpu.SemaphoreType.DMA((n,)), pltpu.SemaphoreType.DMA((n,))])(*arrs)


def scatter_to_chips(arrs, kinds, name):
    n = len(arrs)
    shapes = []
    for a, kind in zip(arrs, kinds):
        l, R, C = a.shape
        shapes.append((l, R // N_CHIPS, C) if kind == "rows" else (l, R, C // N_CHIPS))

    def kern(*refs):
        srcs, outs = refs[:n], refs[n:2 * n]
        send_sems, recv_sems = refs[2 * n:]
        x, y, c = _me()
        copies = []
        for a in range(n):
            size = shapes[a][1] if kinds[a] == "rows" else shapes[a][2]
            for f, (fx, fy) in enumerate(_CHIP_FLIPS):
                window = _chip_window(srcs[a], kinds[a], size, 2 * (x ^ fx) + (y ^ fy), slice(None))
                cp = _remote(window, outs[a].at[f], send_sems.at[a, f], recv_sems.at[a, f], (x ^ fx, y ^ fy, c))
                cp.start()
                copies.append(cp)
        for cp in copies:
            cp.wait()

    sem = pltpu.SemaphoreType.DMA((n, len(_CHIP_FLIPS)))
    return pl.pallas_call(
        kern, name=name, in_specs=[_HBM] * n, out_specs=[_HBM] * n,
        out_shape=[_sds((len(_CHIP_FLIPS),) + s, a.dtype) for s, a in zip(shapes, arrs)],
        scratch_shapes=[sem, sem])(*arrs)


def _stack_tile(r, c):
    for t in (1024, 704, 512, 352, 256, 128, 64, 32, 16):
        if r % t == 0 and t * c * 4 <= 3 * 512 * 1024:
            return t
    return r


def _window_map(kind, r, tr):
    nrt = r // tr
    if kind == "rows":
        return lambda l, i, chip: (l, chip[0] * nrt + i, 0)
    return lambda l, i, chip: (l, i, chip[0])


def place(full, shard, kind, chip, name):
    L, r, c = shard.shape
    tr = _stack_tile(r, c)
    wmap = _window_map(kind, r, tr)
    return tilecall(lambda chip_ref, s_ref, f_ref: (s_ref[...],), name, (L, r // tr),
                    [(shard, pl.BlockSpec((None, tr, c), lambda l, i, chip: (l, i, 0))), (full, pl.BlockSpec(memory_space=pl.ANY))],
                    [(_sds(full.shape, full.dtype), pl.BlockSpec((None, tr, c), lambda l, i, chip: wmap(l, i, chip)), None)],
                    ("parallel", "parallel"), prefetch=(chip,), aliases={1: 0})[0]


def add_cores(g, other, core, name):
    L, R, C = g.shape
    hl = L // 2
    tr = _stack_tile(R, C)
    blk = (None, tr, C)
    return tilecall(lambda core_ref, a_ref, b_ref: (a_ref[...] + b_ref[...],), name, (hl, R // tr),
                    [(g, pl.BlockSpec(blk, lambda l, i, core: (core[0] * hl + l, i, 0))),
                     (other, pl.BlockSpec(blk, lambda l, i, core: (l, i, 0)))],
                    [(_sds((hl, R, C), _MXU), pl.BlockSpec(blk, lambda l, i, core: (l, i, 0)), None)],
                    ("parallel", "parallel"), prefetch=(core,))[0]


def add_chips(own, got, kind, chip, name):
    nf, l, r, c = got.shape
    tr = _stack_tile(r, c)
    wmap = _window_map(kind, r, tr)

    def body(chip_ref, own_ref, *got_refs):
        acc = own_ref[...].astype(F32)
        for ref in got_refs:
            acc = acc + ref[...].astype(F32)
        return (acc,)

    return tilecall(body, name, (l, r // tr),
                    [(own, pl.BlockSpec((None, tr, c), lambda ll, i, chip: wmap(ll, i, chip)))] +
                    [(got, pl.BlockSpec((None, None, tr, c), lambda ll, i, chip, f=f: (f, ll, i, 0))) for f in range(nf)],
                    [(_sds((l, r, c)), pl.BlockSpec((None, tr, c), lambda ll, i, chip: (ll, i, 0)), None)],
                    ("parallel", "parallel"), prefetch=(chip,))[0]


def _adam_update(g, w, m, v):
    m_new = ADAM_B1 * m + (1.0 - ADAM_B1) * g
    v_new = ADAM_B2 * v + (1.0 - ADAM_B2) * jnp.square(g)
    m_hat = m_new / (1.0 - ADAM_B1 ** ADAM_STEP)
    v_hat = v_new / (1.0 - ADAM_B2 ** ADAM_STEP)
    delta = -ADAM_LR * (m_hat / (jnp.sqrt(v_hat) + ADAM_EPS) + ADAM_WD * w)
    return g, delta, m_new, v_new


def adamw_stacked(mine, theirs, w, m, v, core, name):
    L, r, c = w.shape
    hl = L // 2
    tr = _stack_tile(r, c)

    def body(core_ref, a_ref, b_ref, w_ref, m_ref, v_ref):
        is_mine = (pl.program_id(0) // hl) == core_ref[0]
        g = jnp.where(is_mine, a_ref[...], b_ref[...])
        return _adam_update(g, w_ref[...], m_ref[...], v_ref[...])

    full = pl.BlockSpec((None, tr, c), lambda l, i, core: (l, i, 0))
    out = (_sds((L, r, c)), full, None)
    return tilecall(body, name, (L, r // tr),
                    [(mine, pl.BlockSpec((None, tr, c), lambda l, i, core: (l % hl, i, 0))),
                     (theirs, pl.BlockSpec((None, None, tr, c), lambda l, i, core: (0, l % hl, i, 0))),
                     (w, full), (m, full), (v, full)],
                    [out, out, out, out], ("parallel", "parallel"), prefetch=(core,))


def _pack(arrs, dtype, row_multiple):
    flat = jnp.concatenate([a.reshape(-1).astype(dtype) for a in arrs])
    rows = -(-flat.shape[0] // LANES)
    rows = -(-rows // row_multiple) * row_multiple
    return jnp.pad(flat, (0, rows * LANES - flat.shape[0])).reshape(rows, LANES)


def _unpack(buf, shapes):
    flat = buf.reshape(-1)
    out, off = [], 0
    for s in shapes:
        n = int(np.prod(s))
        out.append(flat[off:off + n].reshape(s))
        off += n
    return out


def adamw_packed(gparts, w, m, v, name):
    P, R, _ = gparts.shape

    def body(g_ref, w_ref, m_ref, v_ref):
        g = g_ref[0]
        for p in range(1, P):
            g = g + g_ref[p]
        return _adam_update(g, w_ref[...], m_ref[...], v_ref[...])

    whole = pl.BlockSpec((R, LANES), lambda: (0, 0))
    out = (_sds((R, LANES)), whole, None)
    return tilecall(body, name, (), [(gparts, pl.BlockSpec((P, R, LANES), lambda: (0, 0, 0))), (w, whole), (m, whole), (v, whole)],
                    [out, out, out, out], ())


def _residual_out(a, w, x, next_gain):
    if next_gain is None:
        return mm(a, w, "nn", "mm_nn_add", add=x), None
    return mm(a, w, "nn", "mm_nn_add_rms", add=x, rms_gain=next_gain)


def _input_grad(pieces, x, gain, res):
    dh = None
    for d, w in pieces[:-1]:
        dh = mm(d, w, "nt", "mm_nt" if dh is None else "mm_nt_add", add=dh)
    d, w = pieces[-1]
    return mm(d, w, "nt", "mm_nt_norm_bwd", add=dh, norm_bwd=(x, gain, res))


def _ffn_fwd(x, h, next_gain, layer, gain, wts, cw8, cb):
    u = mm(h, win(wts["ffn_w_up"], layer), "nn", "mm_nn", out_dtype=_MXU)
    a = ffn_act_fwd(u, cw8, cb, "ffn_act_fwd")
    y, h_next = _residual_out(a, win(wts["ffn_w_down"], layer), x, next_gain)
    return y, h_next, (x, h, u, a)


def _ffn_bwd(dy, saved, layer, gain, wts, cw8, cb, grads):
    x, h, u, a = saved
    grads["ffn_w_down"] = mm(a, dy, "tn", "mm_tn_into", into=win(grads["ffn_w_down"], layer))
    da = mm(dy, win(wts["ffn_w_down"], layer), "nt", "mm_nt", out_dtype=_MXU)
    dyg, dyu, dcw_g, dcw_u, dcb_g, dcb_u = ffn_act_bwd(u, cw8, cb, da, "ffn_act_bwd")
    pieces = []
    for half, du in enumerate(ffn_conv_bwd(dyg, dyu, cw8, "ffn_conv_bwd")):
        cols = dict(col_off=half * D_FF, cols=D_FF)
        grads["ffn_w_up"] = mm(h, du, "tn", "mm_tn_into", into=win(grads["ffn_w_up"], layer, **cols))
        pieces.append((du, win(wts["ffn_w_up"], layer, **cols)))
    dx, d_gain = _input_grad(pieces, x, gain, dy)
    return dx, dict(gain=d_gain, conv_w=jnp.concatenate([dcw_g[0:3], dcw_u[0:3]], axis=1),
                    conv_b=jnp.concatenate([dcb_g, dcb_u], axis=1))


def _hgrn_w_in(wts, j, k):
    return win(wts["hgrn_w_in"], j, row_off=k * D_MODEL, rows=D_MODEL)


def _hgrn_layer_fwd(x, h, next_gain, j, gain, wts, lb, out_gain, mall):
    z = [mm(h, _hgrn_w_in(wts, j, k), "nn", "mm_nn") for k in range(4)]
    o, on, states = hgrn_fwd(z[0], z[1], z[2], z[3], lb, out_gain, mall, "hgrn_fwd")
    y, h_next = _residual_out(on, win(wts["hgrn_w_out"], j), x, next_gain)
    return y, h_next, (x, h, z, o, states, on)


def _hgrn_layer_bwd(dy, saved, j, gain, wts, lb, out_gain, mall, grads):
    x, h, z, o, states, on = saved
    grads["hgrn_w_out"] = mm(on, dy, "tn", "mm_tn_into", into=win(grads["hgrn_w_out"], j))
    don = mm(dy, win(wts["hgrn_w_out"], j), "nt", "mm_nt")
    dzq, dzf, dzi, dzg, dlb, d_out_gain = hgrn_bwd(z[0], z[1], z[2], z[3], lb, out_gain, mall, states, o, don, "hgrn_bwd")
    dz = [dzq, dzf, dzi, dzg]
    for k, d in enumerate(dz):
        grads["hgrn_w_in"] = mm(h, d, "tn", "mm_tn_into", into=win(grads["hgrn_w_in"], j, row_off=k * D_MODEL, rows=D_MODEL))
    dx, d_gain = _input_grad([(d, _hgrn_w_in(wts, j, k)) for k, d in enumerate(dz)], x, gain, dy)
    return dx, dict(gain=d_gain, lb=dlb, out_gain=d_out_gain)


_MLA_IN_WINDOWS = ((0, MLA_LORA), (MLA_LORA, MLA_LORA), (2 * MLA_LORA, HEAD_DIM))


def _mla_layer_fwd(x, h, next_gain, j, gain, wts, qa_gain, kva_gain, qn, kn, cos_t, sin_t):
    (c0, n), (c1, n1), (c2, n2) = _MLA_IN_WINDOWS
    cq, cqn = mm(h, win(wts["mla_w_in"], j, col_off=c0, cols=n), "nn", "mm_nn_rms", rms_gain=qa_gain)
    ckv, ckvn = mm(h, win(wts["mla_w_in"], j, col_off=c1, cols=n1), "nn", "mm_nn_rms", rms_gain=kva_gain)
    kr = mm(h, win(wts["mla_w_in"], j, col_off=c2, cols=n2), "nn", "mm_nn")
    qslots = mm(cqn, win(wts["mla_w_q_up"], j), "nn", "mm_nn")
    kv = mm(ckvn, win(wts["mla_w_kv_up"], j), "nn", "mm_nn")
    qr, krot = qk_fwd(qslots, kv, kr, qn, kn, cos_t, sin_t, "qk_fwd")
    o, lse = attn_fwd(qr, krot, kv, "attn_fwd")
    y, h_next = _residual_out(o, win(wts["mla_w_out"], j), x, next_gain)
    return y, h_next, (x, h, cq, ckv, kr, cqn, ckvn, qslots, kv, qr, krot, o, lse)


def _mla_layer_bwd(dy, saved, j, gain, wts, qa_gain, kva_gain, qn, kn, cos_t, sin_t, grads):
    x, h, cq, ckv, kr, cqn, ckvn, qslots, kv, qr, krot, o, lse = saved
    grads["mla_w_out"] = mm(o, dy, "tn", "mm_tn_into", into=win(grads["mla_w_out"], j))
    do = mm(dy, win(wts["mla_w_out"], j), "nt", "mm_nt")
    dq, dk, dv = attn_bwd(qr, krot, kv, o, lse, do, "attn_bwd")
    dqslots, dkv, dkr, d_qn, d_kn = qk_bwd(qslots, kv, kr, qn, kn, cos_t, sin_t, dq, dk, dv, "qk_bwd")
    grads["mla_w_q_up"] = mm(cqn, dqslots, "tn", "mm_tn_into", into=win(grads["mla_w_q_up"], j))
    dcq, d_qa = _input_grad([(dqslots, win(wts["mla_w_q_up"], j))], cq, qa_gain, None)
    grads["mla_w_kv_up"] = mm(ckvn, dkv, "tn", "mm_tn_into", into=win(grads["mla_w_kv_up"], j))
    dckv, d_kva = _input_grad([(dkv, win(wts["mla_w_kv_up"], j))], ckv, kva_gain, None)
    pieces = []
    for d, (c0, n) in zip((dcq, dckv, dkr), _MLA_IN_WINDOWS):
        grads["mla_w_in"] = mm(h, d, "tn", "mm_tn_into", into=win(grads["mla_w_in"], j, col_off=c0, cols=n))
        pieces.append((d, win(wts["mla_w_in"], j, col_off=c0, cols=n)))
    dx, d_gain = _input_grad(pieces, x, gain, dy)
    return dx, dict(gain=d_gain, qa=d_qa, kva=d_kva, qn=d_qn, kn=d_kn)


BIG = (("hgrn_w_in", "rows"), ("hgrn_w_out", "rows"), ("mla_w_in", "rows"), ("mla_w_q_up", "cols"),
       ("mla_w_kv_up", "cols"), ("mla_w_out", "rows"), ("ffn_w_up", "cols"), ("ffn_w_down", "rows"))
SMALL_SHARDED = (("mla_q_a_norm", 1), ("mla_kv_a_norm", 1), ("ffn_conv_w", 2))
REPLICATED = ("norm_mix", "norm_ffn", "hgrn_lower_bounds", "hgrn_out_norm", "mla_q_norm", "mla_k_norm", "ffn_conv_b")
WEIGHTS = ("norm_mix", "norm_ffn", "hgrn_w_in", "hgrn_lower_bounds", "hgrn_out_norm", "hgrn_w_out", "mla_w_in",
           "mla_q_a_norm", "mla_w_q_up", "mla_kv_a_norm", "mla_w_kv_up", "mla_q_norm", "mla_k_norm", "mla_w_out",
           "ffn_w_up", "ffn_conv_w", "ffn_conv_b", "ffn_w_down")


def _pad_cols(a, width):
    return jnp.pad(a, [(0, 0)] * (a.ndim - 1) + [(0, width - a.shape[-1])])


def _head_slots(w):
    lead, n = w.shape[:-1], w.shape[-1] // MLA_QK
    return _pad_cols(w.reshape(lead + (n, MLA_QK)), MLA_SLOT).reshape(lead + (n * MLA_SLOT,))


def _head_unslots(w):
    lead, n = w.shape[:-1], w.shape[-1] // MLA_SLOT
    return w.reshape(lead + (n, MLA_SLOT))[..., :MLA_QK].reshape(lead + (n * MLA_QK,))


def _to_stack_layout(name, a):
    if name == "hgrn_w_in":
        return a
    if name == "mla_w_in":
        return _pad_cols(a, MLA_IN_COLS)
    if name == "mla_w_q_up":
        return _head_slots(a)
    return a


def _from_stack_layout(name, a):
    if name == "mla_w_in":
        return a[..., :2 * MLA_LORA + MLA_ROPE]
    if name == "mla_w_q_up":
        return _head_unslots(a)
    return a


def _rope_tables(positions):
    inv_freq = ROPE_THETA ** (-jnp.arange(0, MLA_ROPE, 2, dtype=F32) / MLA_ROPE)
    ang = positions.astype(F32)[:, None] * inv_freq
    cos, sin = jnp.cos(ang), jnp.sin(ang)
    S = positions.shape[0]
    ones, zeros = jnp.ones((S, MLA_NOPE), F32), jnp.zeros((S, MLA_SLOT - MLA_QK), F32)
    return (jnp.concatenate([ones, cos, cos, zeros], axis=1),
            jnp.concatenate([jnp.zeros((S, MLA_NOPE), F32), -sin, sin, zeros], axis=1))


def kernel(x, positions, norm_mix, norm_ffn, hgrn_w_in, hgrn_lower_bounds, hgrn_out_norm, hgrn_w_out, mla_w_in, mla_q_a_norm, mla_w_q_up, mla_kv_a_norm, mla_w_kv_up, mla_q_norm, mla_k_norm, mla_w_out, ffn_w_up, ffn_conv_w, ffn_conv_b, ffn_w_down, loss_target, m_norm_mix, m_norm_ffn, m_hgrn_w_in, m_hgrn_lower_bounds, m_hgrn_out_norm, m_hgrn_w_out, m_mla_w_in, m_mla_q_a_norm, m_mla_w_q_up, m_mla_kv_a_norm, m_mla_w_kv_up, m_mla_q_norm, m_mla_k_norm, m_mla_w_out, m_ffn_w_up, m_ffn_conv_w, m_ffn_conv_b, m_ffn_w_down, v_norm_mix, v_norm_ffn, v_hgrn_w_in, v_hgrn_lower_bounds, v_hgrn_out_norm, v_hgrn_w_out, v_mla_w_in, v_mla_q_a_norm, v_mla_w_q_up, v_mla_kv_a_norm, v_mla_w_kv_up, v_mla_q_norm, v_mla_k_norm, v_mla_w_out, v_ffn_w_up, v_ffn_conv_w, v_ffn_conv_b, v_ffn_w_down):
    args = dict(locals())
    w = {n: args[n] for n in WEIGHTS}
    m = {n: args["m_" + n] for n in WEIGHTS}
    v = {n: args["v_" + n] for n in WEIGHTS}
    depth = norm_mix.shape[0]
    x0 = x[0]
    S = x0.shape[0]
    chip = (2 * lax.axis_index("x") + lax.axis_index("y")).astype(jnp.int32).reshape(1)
    core = lax.axis_index("c").astype(jnp.int32).reshape(1)
    big_names = [n for n, _ in BIG]
    kinds = [k for _, k in BIG]
    small_names = [n for n, _ in SMALL_SHARDED]
    small_axis = dict(SMALL_SHARDED)

    local = {n: _to_stack_layout(n, w[n]) for n in big_names}
    gathered = gather_big([local[n].astype(_MXU) for n in big_names], kinds, "gather_weights")
    wts = {n: place(g, local[n], k, chip, "place") for n, k, g in zip(big_names, kinds, gathered)}
    (got_small,) = exchange([_pack([w[n] for n in small_names], F32, SUBLANES)], "chips", [None], "gather_small")
    per_chip = [_unpack(got_small[p], [w[n].shape for n in small_names]) for p in range(N_CHIPS)]
    small = {n: jnp.concatenate([per_chip[p][k] for p in range(N_CHIPS)], axis=small_axis[n]) for k, n in enumerate(small_names)}

    cos_t, sin_t = _rope_tables(positions[0])
    lbs = lower_bound_fwd(hgrn_lower_bounds, "lower_bound_fwd")
    mall = jnp.asarray(_hgrn_sum_matrix(min(HGRN_CHUNK, S)), _MXU)
    qn = _pad_cols(mla_q_norm, MLA_SLOT)
    kn = _pad_cols(mla_k_norm, MLA_SLOT)
    cw8 = jnp.pad(small["ffn_conv_w"], ((0, 0), (0, SUBLANES - 3), (0, 0)))

    def mixer_args(layer):
        j = layer // 2
        if layer % 2 == 0:
            return (j, norm_mix[layer:layer + 1], wts, lbs[j:j + 1], hgrn_out_norm[j:j + 1], mall)
        return (j, norm_mix[layer:layer + 1], wts, small["mla_q_a_norm"][j:j + 1], small["mla_kv_a_norm"][j:j + 1],
                qn[j:j + 1], kn[j:j + 1], cos_t, sin_t)

    def ffn_args(layer):
        return (layer, norm_ffn[layer:layer + 1], wts, cw8[layer], ffn_conv_b[layer:layer + 1])

    xc, h = x0, rms_fwd(x0, norm_mix[0:1], "rms_fwd")
    saved = []
    for layer in range(depth):
        fwd = _hgrn_layer_fwd if layer % 2 == 0 else _mla_layer_fwd
        xc, h, s_mix = fwd(xc, h, norm_ffn[layer:layer + 1], *mixer_args(layer))
        xc, h, s_ffn = _ffn_fwd(xc, h, norm_mix[layer + 1:layer + 2] if layer + 1 < depth else None, *ffn_args(layer))
        saved.append((s_mix, s_ffn))

    dh, loss_blk = loss_head(xc, loss_target[0], "loss_head")
    loss = lax.psum(loss_blk[0, 0], MESH_AXES)

    grads = {n: lax.empty(g.shape, F32) for n, g in zip(big_names, gathered)}
    g_mix, g_ffn = [None] * depth, [None] * depth
    for layer in reversed(range(depth)):
        s_mix, s_ffn = saved[layer]
        dh, g_ffn[layer] = _ffn_bwd(dh, s_ffn, *ffn_args(layer), grads)
        bwd = _hgrn_layer_bwd if layer % 2 == 0 else _mla_layer_bwd
        dh, g_mix[layer] = bwd(dh, s_mix, *mixer_args(layer), grads)
    hg = [g_mix[l] for l in range(0, depth, 2)]
    mg = [g_mix[l] for l in range(1, depth, 2)]
    d_p = lower_bound_bwd(hgrn_lower_bounds, hg[1]["lb"], "lower_bound_bwd")

    g_list = [grads[n] for n in big_names]
    from_core = send_other_half(g_list, "reduce_cores_in")
    chip_sums = [add_cores(g, o, core, "add_cores") for g, o in zip(g_list, from_core)]
    from_chips = scatter_to_chips(chip_sums, kinds, "reduce_chips")
    reduced = [add_chips(own, got, k, chip, "add_chips") for own, got, k in zip(chip_sums, from_chips, kinds)]
    other_half = exchange(reduced, "sibling", [None] * len(reduced), "reduce_cores_out", keep_own=False)
    big_out = {}
    for n, mine, theirs in zip(big_names, reduced, other_half):
        outs = adamw_stacked(mine, theirs, local[n], _to_stack_layout(n, m[n]), _to_stack_layout(n, v[n]), core, "adamw")
        big_out[n] = [_from_stack_layout(n, o) for o in outs]

    small_grads = {
        "norm_mix": jnp.concatenate([g["gain"] for g in g_mix], axis=0),
        "norm_ffn": jnp.concatenate([g["gain"] for g in g_ffn], axis=0),
        "hgrn_lower_bounds": d_p[0:2],
        "hgrn_out_norm": jnp.concatenate([g["out_gain"] for g in hg], axis=0),
        "mla_q_a_norm": jnp.concatenate([g["qa"] for g in mg], axis=0),
        "mla_kv_a_norm": jnp.concatenate([g["kva"] for g in mg], axis=0),
        "mla_q_norm": jnp.concatenate([g["qn"][:, :MLA_QK] for g in mg], axis=0),
        "mla_k_norm": jnp.concatenate([g["kn"][:, :MLA_QK] for g in mg], axis=0),
        "ffn_conv_w": jnp.stack([g["conv_w"] for g in g_ffn]),
        "ffn_conv_b": jnp.concatenate([g["conv_b"] for g in g_ffn], axis=0),
    }

    def chip_part(n, p):
        size = w[n].shape[small_axis[n]]
        return lax.slice_in_dim(small_grads[n], p * size, (p + 1) * size, axis=small_axis[n])

    to_chips = jnp.stack([_pack([chip_part(n, p) for n in small_names], F32, SUBLANES) for p in range(N_CHIPS)])
    rep_g, shard_g = exchange([_pack([small_grads[n] for n in REPLICATED], F32, SUBLANES), to_chips], "all",
                              [None, (2, 1, 0)], "reduce_small")
    small_out = {}
    for names, gparts in ((REPLICATED, rep_g), (small_names, shard_g)):
        packed = adamw_packed(gparts, *[_pack([t[n] for n in names], F32, SUBLANES) for t in (w, m, v)], "adamw_small")
        unpacked = [_unpack(buf, [w[n].shape for n in names]) for buf in packed]
        for k, n in enumerate(names):
            small_out[n] = [u[k] for u in unpacked]

    result = [loss, dh[None]]
    for k in range(4):
        result += [(big_out[n] if n in big_out else small_out[n])[k] for n in WEIGHTS]
    return tuple(result)
```

```python
import numpy as np
import jax
import jax.numpy as jnp
from jax import lax
from jax.experimental import pallas as pl
from jax.experimental.pallas import tpu as pltpu

F32 = jnp.float32
BF16 = jnp.bfloat16
_MXU = BF16

RMS_EPS = 1e-6
D_MODEL = 1024
HEADS = 8
HEAD_DIM = 128
HGRN_CHUNK = 128
MLA_NOPE = 128
MLA_ROPE = 64
MLA_QK = MLA_NOPE + MLA_ROPE
MLA_SLOT = 256
MLA_LORA = 256
MLA_IN_COLS = 2 * MLA_LORA + HEAD_DIM
ROPE_THETA = 10000.0
D_FF = 2816
FF_BLOCK = 1408
LANES = 128
SUBLANES = 8

ADAM_LR = 0.001
ADAM_B1 = 0.9
ADAM_B2 = 0.999
ADAM_EPS = 1e-08
ADAM_WD = 0.01
ADAM_STEP = 10

VMEM_LIMIT = 56 * 1024 * 1024
MM_VMEM_BUDGET = 46 * 1024 * 1024
MESH_AXES = ("x", "y", "c")
N_CHIPS = 4

_NN = ((1,), (0,))
_NT = ((1,), (1,))
_TN = ((0,), (0,))


def _dg(a, b, dims):
    return lax.dot_general(a.astype(_MXU), b.astype(_MXU), (dims, ((), ())), preferred_element_type=F32)


@jax.custom_vjp
def kdot(a, b):
    return _dg(a, b, _NN)


kdot.defvjp(lambda a, b: (_dg(a, b, _NN), (a, b)), lambda r, g: (_dg(g, r[1], _NT), _dg(r[0], g, _TN)))


@jax.custom_vjp
def kdot_nt(a, b):
    return _dg(a, b, _NT)


kdot_nt.defvjp(lambda a, b: (_dg(a, b, _NT), (a, b)), lambda r, g: (_dg(g, r[1], _NN), _dg(g, r[0], _TN)))


@jax.custom_vjp
def kdot_tn(a, b):
    return _dg(a, b, _TN)


kdot_tn.defvjp(lambda a, b: (_dg(a, b, _TN), (a, b)), lambda r, g: (_dg(r[1], g, _NT), _dg(r[0], g, _NN)))


def _pick(d, prefs):
    for p in prefs:
        if d >= p and d % p == 0:
            return p
    return d


def _params(sem):
    return pltpu.CompilerParams(dimension_semantics=sem, vmem_limit_bytes=VMEM_LIMIT)


def _sds(shape, dtype=F32):
    return jax.ShapeDtypeStruct(shape, dtype)


def win(arr, layer, row_off=0, col_off=0, rows=None, cols=None):
    return (arr, layer, row_off, col_off, rows or arr.shape[1] - row_off, cols or arr.shape[2] - col_off)


def mm(a, b, mode, name, add=None, out_dtype=F32, into=None, rms_gain=None, norm_bwd=None):
    if isinstance(b, tuple):
        b_arr, b_layer, b_r0, b_c0, b_rows, b_cols = b
    else:
        b_arr, b_layer, b_r0, b_c0, (b_rows, b_cols) = b, None, 0, 0, b.shape
    if mode == "nn":
        (M, K), (K2, N) = a.shape, (b_rows, b_cols)
    elif mode == "nt":
        (M, K), (N, K2) = a.shape, (b_rows, b_cols)
    else:
        (K, M), (K2, N) = a.shape, (b_rows, b_cols)
    assert K == K2, (name, a.shape, b_rows, b_cols)
    tn = N if N <= 1024 else _pick(N, (1024, 1408, 512, 256, 128))
    tk = K if K <= 2048 else _pick(K, (2048, 2816, 1024, 512, 256, 128))
    nk = K // tk
    a_bytes = jnp.dtype(a.dtype).itemsize
    b_bytes = jnp.dtype(b_arr.dtype).itemsize
    extra_tiles = (add is not None) + (rms_gain is not None) / 2 + (3 if norm_bwd is not None else 0)

    def vmem_bytes(tm_):
        tiles = 2 * (tm_ * tk * a_bytes + tk * tn * b_bytes) + tm_ * tn * 4 * (2 + 2 * extra_tiles + (nk > 1))
        return tiles

    tm = M
    if M > 1024:
        fits = [t for t in (2048, 1024, 1408, 512, 256, 128) if M % t == 0 and vmem_bytes(t) <= MM_VMEM_BUDGET]
        tm = fits[0] if fits else _pick(M, (128,))
    dims = {"nn": _NN, "nt": _NT, "tn": _TN}[mode]
    a_spec = pl.BlockSpec((tk, tm), lambda i, j, k: (k, i)) if mode == "tn" else pl.BlockSpec((tm, tk), lambda i, j, k: (i, k))
    b_blk = (tn, tk) if mode == "nt" else (tk, tn)
    assert b_r0 % b_blk[0] == 0 and b_c0 % b_blk[1] == 0, (name, b_r0, b_c0, b_blk)
    br, bc = b_r0 // b_blk[0], b_c0 // b_blk[1]
    if mode == "nt":
        b_idx = lambda i, j, k: (br + j, bc + k)
    else:
        b_idx = lambda i, j, k: (br + k, bc + j)
    if b_layer is None:
        b_spec = pl.BlockSpec(b_blk, b_idx)
    else:
        b_spec = pl.BlockSpec((None,) + b_blk, lambda i, j, k: (b_layer,) + b_idx(i, j, k))
    plain = pl.BlockSpec((tm, tn), lambda i, j, k: (i, j))
    has_add, has_rms, has_nb = add is not None, rms_gain is not None, norm_bwd is not None
    assert not (has_rms or has_nb) or (tn == N and into is None and not (has_rms and has_nb)), name
    vec = pl.BlockSpec((1, tn), lambda i, j, k: (0, j))
    ins = [a, b_arr] + ([add] if has_add else []) + ([rms_gain] if has_rms else [])
    specs = [a_spec, b_spec] + ([plain] if has_add else []) + ([vec] if has_rms else [])
    n_in = len(ins)
    if has_nb:
        nb_x, nb_gain, nb_res = norm_bwd
        ins += [nb_x, nb_gain] + ([nb_res] if nb_res is not None else [])
        specs += [plain, vec] + ([plain] if nb_res is not None else [])
    aliases = {}
    if into is None:
        o_spec, out_shape = plain, _sds((M, N), out_dtype)
    else:
        buf, o_layer, o_r0, o_c0, o_rows, o_cols = into
        assert (o_rows, o_cols) == (M, N) and o_r0 % tm == 0 and o_c0 % tn == 0, (name, into[1:], M, N, tm, tn)
        orow, ocol = o_r0 // tm, o_c0 // tn
        o_spec = pl.BlockSpec((None, tm, tn), lambda i, j, k: (o_layer, orow + i, ocol + j))
        out_shape = _sds(buf.shape, buf.dtype)
        aliases = {len(ins): 0}
        ins.append(buf)
        specs.append(pl.BlockSpec(memory_space=pl.ANY))

    n_all_in = len(ins)

    def kern(*refs):
        a_ref, b_ref = refs[0], refs[1]
        add_ref = refs[2] if has_add else None
        o_ref = refs[n_all_in]

        def finish(r):
            if has_add:
                r = r + add_ref[...].astype(F32)
            if has_nb:
                _, vjp = jax.vjp(lambda xv, gv: _rms(xv, gv, N), refs[n_in][...], refs[n_in + 1][...])
                dx, dgain = vjp(r)
                o_ref[...] = dx if nb_res is None else dx + refs[n_in + 2][...]
                _store(refs[n_all_in + 1], dgain, pl.program_id(0) == 0)
                return
            o_ref[...] = r.astype(o_ref.dtype)
            if has_rms:
                refs[n_all_in + 1][...] = _rms(r, refs[n_in - 1][...], N).astype(_MXU)

        if nk == 1:
            finish(_dg(a_ref[...], b_ref[...], dims))
            return
        acc = refs[-1]
        k = pl.program_id(2)

        @pl.when(k == 0)
        def _():
            acc[...] = jnp.zeros_like(acc)

        acc[...] += _dg(a_ref[...], b_ref[...], dims)

        @pl.when(k == nk - 1)
        def _():
            finish(acc[...])

    if has_rms:
        o_spec, out_shape = [o_spec, plain], [out_shape, _sds((M, N), _MXU)]
    if has_nb:
        o_spec, out_shape = [o_spec, vec], [out_shape, _sds((1, N))]
    return pl.pallas_call(
        kern, name=name, grid=(M // tm, N // tn, nk), in_specs=specs, out_specs=o_spec, out_shape=out_shape,
        scratch_shapes=[pltpu.VMEM((tm, tn), F32)] if nk > 1 else [], input_output_aliases=aliases,
        compiler_params=_params(("arbitrary" if has_nb else "parallel", "parallel", "arbitrary")))(*ins)


def _store(ref, val, first):
    if first is None:
        ref[...] = val.astype(ref.dtype)
        return

    @pl.when(first)
    def _():
        ref[...] = val.astype(ref.dtype)

    @pl.when(jnp.logical_not(first))
    def _():
        ref[...] += val.astype(ref.dtype)


def tilecall(body, name, grid, ins, outs, sem, prefetch=(), aliases=None):
    n_pre, n_in = len(prefetch), len(ins)

    def kern(*refs):
        vals = body(*refs[:n_pre + n_in])
        for ref, val, (_, _, first) in zip(refs[n_pre + n_in:], vals, outs):
            _store(ref, val, None if first is None else first())

    in_specs, out_specs = [s for _, s in ins], [s for _, s, _ in outs]
    kwargs = dict(name=name, out_shape=[sh for sh, _, _ in outs], compiler_params=_params(sem),
                  input_output_aliases={n_pre + k: v for k, v in (aliases or {}).items()})
    if n_pre:
        kwargs["grid_spec"] = pltpu.PrefetchScalarGridSpec(num_scalar_prefetch=n_pre, grid=grid, in_specs=in_specs,
                                                           out_specs=out_specs)
    else:
        kwargs.update(grid=grid, in_specs=in_specs, out_specs=out_specs)
    return pl.pallas_call(kern, **kwargs)(*prefetch, *[a for a, _ in ins])


def _rms(x, g, n):
    ms = jnp.sum(x * x, axis=-1, keepdims=True) / n
    return x * lax.rsqrt(ms + RMS_EPS) * g


def _row_tile(S, w):
    return min(S, 512 if w <= 1024 else 256)


def rms_fwd(x, g, name, col=0, w=None):
    S = x.shape[0]
    w = w or x.shape[1]
    ts = _row_tile(S, w)
    return tilecall(
        lambda x_ref, g_ref: (_rms(x_ref[...], g_ref[...], w),), name, (S // ts,),
        [(x, pl.BlockSpec((ts, w), lambda i: (i, col))), (g, pl.BlockSpec((1, w), lambda i: (0, 0)))],
        [(_sds((S, w), _MXU), pl.BlockSpec((ts, w), lambda i: (i, 0)), None)], ("parallel",))[0]


HALO_ROWS = 16


def _shifted(u, halo_ref, is_first):
    rid = lax.broadcasted_iota(jnp.int32, (SUBLANES, 1), 0)
    halo = halo_ref[...].astype(F32)
    hrow = lax.broadcasted_iota(jnp.int32, (HALO_ROWS, 1), 0)

    def halo_row(r):
        return jnp.where(is_first, 0.0, jnp.sum(jnp.where(hrow == r, halo, 0.0), axis=0, keepdims=True))

    h7, h6 = halo_row(HALO_ROWS - 1), halo_row(HALO_ROWS - 2)
    r1, r2 = pltpu.roll(u, 1, 0), pltpu.roll(u, 2, 0)
    top1 = jnp.where(rid == 0, h7, r1[:SUBLANES])
    top2 = jnp.where(rid == 0, h6, jnp.where(rid == 1, h7, r2[:SUBLANES]))
    return jnp.concatenate([top1, r1[SUBLANES:]], axis=0), jnp.concatenate([top2, r2[SUBLANES:]], axis=0)


def _conv(u, u1, u2, cw_ref, cb_ref):
    return ((cb_ref[...] + u2 * cw_ref[0:1, :]) + u1 * cw_ref[1:2, :]) + u * cw_ref[2:3, :]


def _ffn_specs(S, ts, jmap):
    hb = ts // HALO_ROWS
    return (pl.BlockSpec((ts, FF_BLOCK), lambda j, i: (i, jmap(j))),
            pl.BlockSpec((HALO_ROWS, FF_BLOCK), lambda j, i: (jnp.maximum(i * hb - 1, 0), jmap(j))),
            pl.BlockSpec((SUBLANES, FF_BLOCK), lambda j, i: (0, jmap(j))),
            pl.BlockSpec((1, FF_BLOCK), lambda j, i: (0, jmap(j))))


def ffn_act_fwd(u, cw8, cb, name):
    S = u.shape[0]
    ts = _row_tile(S, 2 * D_FF)
    nb = D_FF // FF_BLOCK

    def body(ug, hg, cwg, cbg, uu, hu, cwu, cbu):
        first = pl.program_id(1) == 0
        g = ug[...].astype(F32)
        g1, g2 = _shifted(g, hg, first)
        yg = _conv(g, g1, g2, cwg, cbg)
        v = uu[...].astype(F32)
        v1, v2 = _shifted(v, hu, first)
        yu = _conv(v, v1, v2, cwu, cbu)
        return (yg * jax.nn.sigmoid(yg) * yu,)

    sg = _ffn_specs(S, ts, lambda j: j)
    su = _ffn_specs(S, ts, lambda j: j + nb)
    ins = [(u, sg[0]), (u, sg[1]), (cw8, sg[2]), (cb, sg[3]), (u, su[0]), (u, su[1]), (cw8, su[2]), (cb, su[3])]
    return tilecall(body, name, (nb, S // ts), ins,
                    [(_sds((S, D_FF), _MXU), pl.BlockSpec((ts, FF_BLOCK), lambda j, i: (i, j)), None)],
                    ("parallel", "parallel"))[0]


def ffn_act_bwd(u, cw8, cb, da, name):
    S = u.shape[0]
    ts = _row_tile(S, 2 * D_FF)
    nb = D_FF // FF_BLOCK

    def taps(dy, x, x1, x2):
        return jnp.concatenate(
            [jnp.sum(dy * x2, axis=0, keepdims=True), jnp.sum(dy * x1, axis=0, keepdims=True),
             jnp.sum(dy * x, axis=0, keepdims=True), jnp.zeros((SUBLANES - 3, dy.shape[1]), F32)], axis=0)

    def body(ug, hg, cwg, cbg, uu, hu, cwu, cbu, da_ref):
        first = pl.program_id(1) == 0
        g = ug[...].astype(F32)
        g1, g2 = _shifted(g, hg, first)
        yg = _conv(g, g1, g2, cwg, cbg)
        v = uu[...].astype(F32)
        v1, v2 = _shifted(v, hu, first)
        yu = _conv(v, v1, v2, cwu, cbu)
        d = da_ref[...].astype(F32)
        sg = jax.nn.sigmoid(yg)
        dyg = d * yu * (sg * (1.0 + yg * (1.0 - sg)))
        dyu = d * (yg * sg)
        return (dyg, dyu, taps(dyg, g, g1, g2), taps(dyu, v, v1, v2),
                jnp.sum(dyg, axis=0, keepdims=True), jnp.sum(dyu, axis=0, keepdims=True))

    sg_ = _ffn_specs(S, ts, lambda j: j)
    su_ = _ffn_specs(S, ts, lambda j: j + nb)
    row = pl.BlockSpec((ts, FF_BLOCK), lambda j, i: (i, j))
    ins = [(u, sg_[0]), (u, sg_[1]), (cw8, sg_[2]), (cb, sg_[3]), (u, su_[0]), (u, su_[1]), (cw8, su_[2]), (cb, su_[3]), (da, row)]
    first_row = lambda: pl.program_id(1) == 0
    dy, dcw, dcb = (_sds((S, D_FF)), row, None), (_sds((SUBLANES, D_FF)), sg_[2], first_row), (_sds((1, D_FF)), sg_[3], first_row)
    return tilecall(body, name, (nb, S // ts), ins, [dy, dy, dcw, dcw, dcb, dcb], ("parallel", "arbitrary"))


def ffn_conv_bwd(dyg, dyu, cw8, name):
    S = dyg.shape[0]
    ts = _row_tile(S, 2 * D_FF)
    hb = ts // SUBLANES
    nrow = S // ts
    nb = D_FF // FF_BLOCK

    def back(dy_ref, halo_ref, cw_ref):
        last = pl.program_id(1) == nrow - 1
        d = dy_ref[...]
        rid = lax.broadcasted_iota(jnp.int32, (SUBLANES, 1), 0)
        n0 = jnp.where(last, 0.0, halo_ref[0:1, :])
        n1 = jnp.where(last, 0.0, halo_ref[1:2, :])
        r1, r2 = pltpu.roll(d, ts - 1, 0), pltpu.roll(d, ts - 2, 0)
        end1 = jnp.where(rid == SUBLANES - 1, n0, r1[ts - SUBLANES:])
        end2 = jnp.where(rid == SUBLANES - 1, n1, jnp.where(rid == SUBLANES - 2, n0, r2[ts - SUBLANES:]))
        d1 = jnp.concatenate([r1[:ts - SUBLANES], end1], axis=0)
        d2 = jnp.concatenate([r2[:ts - SUBLANES], end2], axis=0)
        return d * cw_ref[2:3, :] + d1 * cw_ref[1:2, :] + d2 * cw_ref[0:1, :]

    row = pl.BlockSpec((ts, FF_BLOCK), lambda j, i: (i, j))
    halo = pl.BlockSpec((SUBLANES, FF_BLOCK), lambda j, i: (jnp.minimum((i + 1) * hb, S // SUBLANES - 1), j))
    ins = [(dyg, row), (dyg, halo), (cw8, pl.BlockSpec((SUBLANES, FF_BLOCK), lambda j, i: (0, j))),
           (dyu, row), (dyu, halo), (cw8, pl.BlockSpec((SUBLANES, FF_BLOCK), lambda j, i: (0, j + nb)))]
    out = (_sds((S, D_FF), _MXU), row, None)
    return tilecall(lambda a, b, c, d, e, f: (back(a, b, c), back(d, e, f)), name, (nb, nrow), ins, [out, out],
                    ("parallel", "parallel"))


def _hgrn_levels(C):
    out, m = [], C // 2
    while m >= 1:
        out.append(m)
        m //= 2
    return out


def _hgrn_sum_matrix(C):
    t = np.arange(C)[:, None]
    u = np.arange(C)[None, :]
    blocks = [u <= t, u > t]
    for m in _hgrn_levels(C):
        r = (t // (2 * m)) * (2 * m) + m
        right = (t % (2 * m)) >= m
        blocks.append((right & (u > r) & (u <= t)) | ((~right) & (u > t) & (u <= r)))
    return np.concatenate(blocks, axis=0).astype(np.float32)


def _make_partial_sums(nb, C):
    @jax.custom_vjp
    def sums(mall, lf):
        hi = lf.astype(_MXU)
        mid = (lf - hi.astype(F32)).astype(_MXU)
        e2 = _dg(mall, jnp.concatenate([hi, mid], axis=1), _NN)
        e = e2[:, :HEAD_DIM] + e2[:, HEAD_DIM:]
        return tuple(e[b * C:(b + 1) * C] for b in range(nb))

    def fwd(mall, lf):
        return sums(mall, lf), mall

    def bwd(mall, gs):
        return jnp.zeros_like(mall), _dg(mall, jnp.concatenate(gs, axis=0), _TN)

    sums.defvjp(fwd, bwd)
    return sums


def _hgrn_chunk(zq, zf, v, lb, st, mall, C):
    levels = _hgrn_levels(C)
    qs = zq * jax.nn.sigmoid(zq)
    fg = lb + (1.0 - lb) * jax.nn.sigmoid(zf)
    k = 1.0 - fg
    e = _make_partial_sums(2 + len(levels), C)(mall, jnp.log(fg))
    g_incl, g_after = e[0], e[1]
    rid = lax.broadcasted_iota(jnp.int32, (C, 1), 0)
    tt = lax.broadcasted_iota(jnp.int32, (C, C), 0)
    ss = lax.broadcasted_iota(jnp.int32, (C, C), 1)
    o = kdot_nt(qs * jnp.exp(g_incl), st)
    o = o + jnp.sum(qs * k, axis=-1, keepdims=True) * v
    scores = jnp.zeros((C, C), F32)
    for li, m in enumerate(levels):
        sh = int(np.log2(m))
        right = ((rid >> sh) & 1) == 1
        both = jnp.where(right, qs, k) * jnp.exp(e[2 + li])
        pair = ((tt >> (sh + 1)) == (ss >> (sh + 1))) & (((tt >> sh) & 1) == 1) & (((ss >> sh) & 1) == 0)
        scores = scores + jnp.where(pair, kdot_nt(both, both), 0.0)
    o = o + kdot(scores, v)
    g_last = jnp.sum(jnp.where(rid == C - 1, g_incl, 0.0), axis=0, keepdims=True)
    st_new = st * jnp.exp(g_last) + kdot_tn(v, k * jnp.exp(g_after))
    return o, st_new


HGRN_HEADS_PER_STEP = 8
_HGRN_LANES = HGRN_HEADS_PER_STEP * HEAD_DIM


def _hgrn_in_specs(C, nc, rev):
    cm = (lambda c: nc - 1 - c) if rev else (lambda c: c)
    blk = lambda: pl.BlockSpec((C, _HGRN_LANES), lambda h, c: (cm(c), h))
    return cm, [blk(), blk(), blk(), pl.BlockSpec((1, _HGRN_LANES), lambda h, c: (0, h))]


def _hgrn_state_spec(cm):
    return pl.BlockSpec((HGRN_HEADS_PER_STEP, None, HEAD_DIM, HEAD_DIM), lambda h, c: (h, cm(c), 0, 0))


def _hgrn_out(o, g, gain):
    return _rms(o, gain, HEAD_DIM) * (g * jax.nn.sigmoid(g))


def hgrn_fwd(zq, zf, zi, zg, lb, out_gain, mall, name):
    S = zq.shape[0]
    C = min(HGRN_CHUNK, S)
    nc = S // C

    def kern(zq_ref, zf_ref, zi_ref, lb_ref, zg_ref, gain_ref, mall_ref, o_ref, on_ref, st_ref, st):
        @pl.when(pl.program_id(1) == 0)
        def _():
            st[...] = jnp.zeros_like(st)

        mall_v = mall_ref[...]
        for g in range(HGRN_HEADS_PER_STEP):
            lanes = slice(g * HEAD_DIM, (g + 1) * HEAD_DIM)
            s_in = st[g]
            st_ref[g] = s_in
            o, s_new = _hgrn_chunk(zq_ref[:, lanes], zf_ref[:, lanes], zi_ref[:, lanes], lb_ref[:, lanes], s_in, mall_v, C)
            o_ref[:, lanes] = o
            on_ref[:, lanes] = _hgrn_out(o, zg_ref[:, lanes], gain_ref[...]).astype(on_ref.dtype)
            st[g] = s_new

    cm, specs = _hgrn_in_specs(C, nc, False)
    row = pl.BlockSpec((C, _HGRN_LANES), lambda h, c: (c, h))
    return pl.pallas_call(
        kern, name=name, grid=(HEADS // HGRN_HEADS_PER_STEP, nc),
        in_specs=specs + [row, pl.BlockSpec((1, HEAD_DIM), lambda h, c: (0, 0)), pl.BlockSpec(mall.shape, lambda h, c: (0, 0))],
        out_specs=[row, row, _hgrn_state_spec(cm)],
        out_shape=[_sds((S, D_MODEL)), _sds((S, D_MODEL), _MXU), _sds((HEADS, nc, HEAD_DIM, HEAD_DIM))],
        scratch_shapes=[pltpu.VMEM((HGRN_HEADS_PER_STEP, HEAD_DIM, HEAD_DIM), F32)],
        compiler_params=_params(("parallel", "arbitrary")))(zq, zf, zi, lb, zg, out_gain, mall)


def hgrn_bwd(zq, zf, zi, zg, lb, out_gain, mall, states, o, don, name):
    S = zq.shape[0]
    C = min(HGRN_CHUNK, S)
    nc = S // C

    def kern(zq_ref, zf_ref, zi_ref, lb_ref, zg_ref, gain_ref, mall_ref, st_ref, o_ref, don_ref,
             dq_ref, df_ref, di_ref, dg_ref, dlb_ref, dgain_ref, dst):
        first = pl.program_id(1) == 0

        @pl.when(first)
        def _():
            dst[...] = jnp.zeros_like(dst)

        mall_v = mall_ref[...]
        gls, dgain = [], None
        for g in range(HGRN_HEADS_PER_STEP):
            lanes = slice(g * HEAD_DIM, (g + 1) * HEAD_DIM)
            _, out_vjp = jax.vjp(_hgrn_out, o_ref[:, lanes], zg_ref[:, lanes], gain_ref[...])
            do, dzg, dgn = out_vjp(don_ref[:, lanes])
            dg_ref[:, lanes] = dzg.astype(dg_ref.dtype)
            dgain = dgn if dgain is None else dgain + dgn
            _, vjp = jax.vjp(lambda a, b, c, d, e: _hgrn_chunk(a, b, c, d, e, mall_v, C),
                             zq_ref[:, lanes], zf_ref[:, lanes], zi_ref[:, lanes], lb_ref[:, lanes], st_ref[g])
            ga, gb, gv, gl, gs = vjp((do, dst[g]))
            dq_ref[:, lanes] = ga.astype(dq_ref.dtype)
            df_ref[:, lanes] = gb.astype(df_ref.dtype)
            di_ref[:, lanes] = gv.astype(di_ref.dtype)
            gls.append(gl)
            dst[g] = gs
        _store(dlb_ref, jnp.concatenate(gls, axis=1), first)
        _store(dgain_ref, dgain, first & (pl.program_id(0) == 0))

    cm, specs = _hgrn_in_specs(C, nc, True)
    row = lambda: pl.BlockSpec((C, _HGRN_LANES), lambda h, c: (cm(c), h))
    vec = pl.BlockSpec((1, HEAD_DIM), lambda h, c: (0, 0))
    wide = _sds((S, D_MODEL), _MXU)
    return pl.pallas_call(
        kern, name=name, grid=(HEADS // HGRN_HEADS_PER_STEP, nc),
        in_specs=specs + [row(), vec, pl.BlockSpec(mall.shape, lambda h, c: (0, 0)), _hgrn_state_spec(cm), row(), row()],
        out_specs=[row(), row(), row(), row(), pl.BlockSpec((1, _HGRN_LANES), lambda h, c: (0, h)), vec],
        out_shape=[wide, wide, wide, wide, _sds((1, D_MODEL)), _sds((1, HEAD_DIM))],
        scratch_shapes=[pltpu.VMEM((HGRN_HEADS_PER_STEP, HEAD_DIM, HEAD_DIM), F32)],
        compiler_params=_params(("arbitrary", "arbitrary")))(zq, zf, zi, lb, zg, out_gain, mall, states, o, don)


def _lb_soft(p0, p1):
    mx = jnp.maximum(p0, p1)
    e0, e1 = jnp.exp(p0 - mx), jnp.exp(p1 - mx)
    s0, s1 = e0 / (e0 + e1), e1 / (e0 + e1)
    return (s0 + s1) - s0


def lower_bound_fwd(p, name):
    assert p.shape[0] == 2

    def body(p_ref):
        s = _lb_soft(p_ref[0:1, :], p_ref[1:2, :])
        return (jnp.concatenate([jnp.zeros_like(s), s] + [jnp.zeros_like(s)] * (SUBLANES - 2), axis=0),)

    spec8 = pl.BlockSpec((SUBLANES, p.shape[1]), lambda: (0, 0))
    return tilecall(body, name, (), [(p, pl.BlockSpec(p.shape, lambda: (0, 0)))], [(_sds((SUBLANES, p.shape[1])), spec8, None)], ())[0]


def lower_bound_bwd(p, dlb1, name):
    def body(p_ref, d_ref):
        _, vjp = jax.vjp(_lb_soft, p_ref[0:1, :], p_ref[1:2, :])
        g0, g1 = vjp(d_ref[...])
        return (jnp.concatenate([g0, g1] + [jnp.zeros_like(g0)] * (SUBLANES - 2), axis=0),)

    spec8 = pl.BlockSpec((SUBLANES, p.shape[1]), lambda: (0, 0))
    return tilecall(body, name, (), [(p, pl.BlockSpec(p.shape, lambda: (0, 0))), (dlb1, pl.BlockSpec(dlb1.shape, lambda: (0, 0)))],
                    [(_sds((SUBLANES, p.shape[1])), spec8, None)], ())[0]


@jax.custom_vjp
def _swap_rope_halves(x):
    lane = lax.broadcasted_iota(jnp.int32, x.shape, 1)
    lo = (lane >= MLA_NOPE) & (lane < MLA_NOPE + MLA_ROPE // 2)
    hi = (lane >= MLA_NOPE + MLA_ROPE // 2) & (lane < MLA_QK)
    return jnp.where(lo, pltpu.roll(x, MLA_SLOT - MLA_ROPE // 2, 1), jnp.where(hi, pltpu.roll(x, MLA_ROPE // 2, 1), 0.0))


_swap_rope_halves.defvjp(lambda x: (_swap_rope_halves(x), None), lambda _, g: (_swap_rope_halves(g),))


def _norm_rope(x, gain, cos_t, sin_t):
    y = _rms(x, gain, MLA_QK)
    return y * cos_t + _swap_rope_halves(y) * sin_t


_ATTN_SCALE = MLA_QK ** -0.5


def _qk_heads(qs, kn, kr, qn, kn_gain, cos_t, sin_t):
    q = _norm_rope(qs, qn, cos_t, sin_t) * _ATTN_SCALE
    k = _norm_rope(jnp.concatenate([kn, kr], axis=1), kn_gain, cos_t, sin_t)
    return q, k


def _qk_specs(ts):
    slot = pl.BlockSpec((ts, MLA_SLOT), lambda i, h: (i, h))
    nope = pl.BlockSpec((ts, HEAD_DIM), lambda i, h: (i, 2 * h))
    shared = pl.BlockSpec((ts, HEAD_DIM), lambda i, h: (i, 0))
    gain = pl.BlockSpec((1, MLA_SLOT), lambda i, h: (0, 0))
    table = pl.BlockSpec((ts, MLA_SLOT), lambda i, h: (i, 0))
    return slot, nope, shared, gain, table


QK_ROWS = 1024


def qk_fwd(qslots, kv, krope, qn, kn, cos_t, sin_t, name):
    S = qslots.shape[0]
    ts = min(S, QK_ROWS)
    slot, nope, shared, gain, table = _qk_specs(ts)

    def body(q_ref, kn_ref, kr_ref, qn_ref, kg_ref, c_ref, s_ref):
        return _qk_heads(q_ref[...], kn_ref[...], kr_ref[...], qn_ref[...], kg_ref[...], c_ref[...], s_ref[...])

    out = _sds((S, HEADS * MLA_SLOT), _MXU)
    return tilecall(body, name, (S // ts, HEADS),
                    [(qslots, slot), (kv, nope), (krope, shared), (qn, gain), (kn, gain), (cos_t, table), (sin_t, table)],
                    [(out, slot, None), (out, slot, None)], ("parallel", "parallel"))


def qk_bwd(qslots, kv, krope, qn, kn, cos_t, sin_t, dq, dk, dv, name):
    S = qslots.shape[0]
    ts = min(S, QK_ROWS)
    slot, nope, shared, gain, table = _qk_specs(ts)
    vblk = pl.BlockSpec((ts, HEAD_DIM), lambda i, h: (i, h))

    def body(q_ref, kn_ref, kr_ref, qn_ref, kg_ref, c_ref, s_ref, dq_ref, dk_ref, dv_ref):
        c, s = c_ref[...], s_ref[...]
        _, vjp = jax.vjp(lambda a, b, r, g1, g2: _qk_heads(a, b, r, g1, g2, c, s),
                         q_ref[...], kn_ref[...], kr_ref[...], qn_ref[...], kg_ref[...])
        ga, gb, gr, g1, g2 = vjp((dq_ref[...], dk_ref[...]))
        return ga, jnp.concatenate([gb, dv_ref[...]], axis=1), gr, g1, g2

    first_head = lambda: pl.program_id(1) == 0
    first = lambda: (pl.program_id(0) == 0) & (pl.program_id(1) == 0)
    wide = _sds((S, HEADS * MLA_SLOT), _MXU)
    return tilecall(body, name, (S // ts, HEADS),
                    [(qslots, slot), (kv, nope), (krope, shared), (qn, gain), (kn, gain), (cos_t, table), (sin_t, table),
                     (dq, slot), (dk, slot), (dv, vblk)],
                    [(wide, slot, None), (wide, slot, None), (_sds((S, HEAD_DIM)), shared, first_head),
                     (_sds((1, MLA_SLOT)), gain, first), (_sds((1, MLA_SLOT)), gain, first)], ("arbitrary", "arbitrary"))


ATTN_TILE_FWD = 1024
ATTN_TILE = 1024
ATTN_HEADS_PER_STEP = 2
ATTN_HEADS_PER_STEP_BWD = 1


def _causal_pairs(nq, by_row):
    pairs = [(i, j) for i in range(nq) for j in range(i + 1)] if by_row else [(i, j) for j in range(nq) for i in range(j, nq)]
    return jnp.asarray([p[0] for p in pairs], jnp.int32), jnp.asarray([p[1] for p in pairs], jnp.int32)


def _diag_mask(s, tq):
    rows = lax.broadcasted_iota(jnp.int32, (tq, tq), 0)
    cols = lax.broadcasted_iota(jnp.int32, (tq, tq), 1)
    return jnp.where(rows >= cols, s, -jnp.inf)


def attn_fwd(qr, kr, kv, name):
    S = qr.shape[0]
    tq = min(S, ATTN_TILE_FWD)
    nq = S // tq
    i_tab, j_tab = _causal_pairs(nq, True)

    G = ATTN_HEADS_PER_STEP
    reps = tq // LANES

    def kern(it, jt, q_ref, k_ref, kv_ref, o_ref, lse_ref, m_s, l_s, acc):
        n = pl.program_id(1)
        i, j = it[n], jt[n]

        @pl.when(j == 0)
        def _():
            m_s[...] = jnp.full_like(m_s, -jnp.inf)
            l_s[...] = jnp.zeros_like(l_s)
            acc[...] = jnp.zeros_like(acc)

        def step(diagonal):
            for g in range(G):
                slot = slice(g * MLA_SLOT, (g + 1) * MLA_SLOT)
                s = _dg(q_ref[:, slot], k_ref[:, slot], _NT)
                if diagonal:
                    s = _diag_mask(s, tq)
                m_prev = m_s[g]
                m_new = jnp.maximum(m_prev, jnp.max(s, axis=-1, keepdims=True))
                alpha = jnp.exp(m_prev - m_new)
                p = jnp.exp(s - jnp.tile(m_new, (1, reps)))
                l_s[g] = alpha * l_s[g] + jnp.sum(p, axis=-1, keepdims=True)
                acc[g] = alpha * acc[g] + _dg(p, kv_ref[:, g * MLA_SLOT + HEAD_DIM:(g + 1) * MLA_SLOT], _NN)
                m_s[g] = m_new

        @pl.when(j < i)
        def _():
            step(False)

        @pl.when(j == i)
        def _():
            step(True)
            for g in range(G):
                l = l_s[g]
                lanes = slice(g * HEAD_DIM, (g + 1) * HEAD_DIM)
                o_ref[:, lanes] = acc[g] / l
                lse_ref[:, lanes] = m_s[g] + jnp.log(l)

    out = _sds((S, HEADS * HEAD_DIM))
    oblk = pl.BlockSpec((tq, G * HEAD_DIM), lambda h, n, it, jt: (it[n], h))
    stat = pltpu.VMEM((G, tq, HEAD_DIM), F32)
    return pl.pallas_call(
        kern, name=name,
        grid_spec=pltpu.PrefetchScalarGridSpec(
            num_scalar_prefetch=2, grid=(HEADS // G, i_tab.shape[0]),
            in_specs=[pl.BlockSpec((tq, G * MLA_SLOT), lambda h, n, it, jt: (it[n], h)),
                      pl.BlockSpec((tq, G * MLA_SLOT), lambda h, n, it, jt: (jt[n], h)),
                      pl.BlockSpec((tq, G * MLA_SLOT), lambda h, n, it, jt: (jt[n], h))],
            out_specs=[oblk, oblk], scratch_shapes=[stat, stat, stat]),
        out_shape=[out, out], compiler_params=_params(("parallel", "arbitrary")))(i_tab, j_tab, qr, kr, kv)


def attn_bwd(qr, kr, kv, o, lse, do, name):
    S = qr.shape[0]
    tq = min(S, ATTN_TILE)
    nq = S // tq
    i_tab, j_tab = _causal_pairs(nq, False)

    G = ATTN_HEADS_PER_STEP_BWD

    def kern(it, jt, q_ref, k_ref, kv_ref, o_ref, lse_ref, do_ref, dq_ref, dk_ref, dv_ref, dk_acc, dv_acc):
        n = pl.program_id(1)
        i, j = it[n], jt[n]

        @pl.when(n == 0)
        def _():
            dq_ref[...] = jnp.zeros_like(dq_ref)

        @pl.when(i == j)
        def _():
            dk_acc[...] = jnp.zeros_like(dk_acc)
            dv_acc[...] = jnp.zeros_like(dv_acc)

        def step(diagonal):
            rows = pl.ds(pl.multiple_of(i * tq, tq), tq)
            for g in range(G):
                slot = slice(g * MLA_SLOT, (g + 1) * MLA_SLOT)
                lanes = slice(g * HEAD_DIM, (g + 1) * HEAD_DIM)
                q, k = q_ref[:, slot], k_ref[:, slot]
                s = _dg(q, k, _NT) - jnp.tile(lse_ref[:, lanes], (1, tq // LANES))
                if diagonal:
                    s = _diag_mask(s, tq)
                p = jnp.exp(s)
                d = do_ref[:, lanes]
                delta = jnp.sum(d * o_ref[:, lanes], axis=-1, keepdims=True)
                dv_acc[:, lanes] += _dg(p, d, _TN)
                ds = p * (_dg(d, kv_ref[:, g * MLA_SLOT + HEAD_DIM:(g + 1) * MLA_SLOT], _NT) - delta)
                dk_acc[:, slot] += _dg(ds, q, _TN)
                dq_ref[rows, slot] += _dg(ds, k, _NN)

        @pl.when(i > j)
        def _():
            step(False)

        @pl.when(i == j)
        def _():
            step(True)

        @pl.when(i == nq - 1)
        def _():
            dk_ref[...] = dk_acc[...]
            dv_ref[...] = dv_acc[...]

    qblk = pl.BlockSpec((tq, G * MLA_SLOT), lambda h, n, it, jt: (it[n], h))
    oblk = pl.BlockSpec((tq, G * HEAD_DIM), lambda h, n, it, jt: (it[n], h))
    kblk = pl.BlockSpec((tq, G * MLA_SLOT), lambda h, n, it, jt: (jt[n], h))
    return pl.pallas_call(
        kern, name=name,
        grid_spec=pltpu.PrefetchScalarGridSpec(
            num_scalar_prefetch=2, grid=(HEADS // G, i_tab.shape[0]),
            in_specs=[qblk, kblk, kblk, oblk, oblk, oblk],
            out_specs=[pl.BlockSpec((S, G * MLA_SLOT), lambda h, n, it, jt: (0, h)), kblk,
                       pl.BlockSpec((tq, G * HEAD_DIM), lambda h, n, it, jt: (jt[n], h))],
            scratch_shapes=[pltpu.VMEM((tq, G * MLA_SLOT), F32), pltpu.VMEM((tq, G * HEAD_DIM), F32)]),
        out_shape=[_sds((S, HEADS * MLA_SLOT)), _sds((S, HEADS * MLA_SLOT)), _sds((S, HEADS * HEAD_DIM))],
        compiler_params=_params(("parallel", "arbitrary")))(i_tab, j_tab, qr, kr, kv, o, lse, do)


def loss_head(y, target, name):
    S, Dm = y.shape
    ts = _row_tile(S, Dm)

    def body(y_ref, t_ref):
        e = y_ref[...] - t_ref[...]
        tot = jnp.sum(jnp.sum(e * e, axis=-1, keepdims=True) / Dm, axis=0, keepdims=True)
        return e / Dm, jnp.broadcast_to(0.5 * tot, (SUBLANES, LANES))

    row = pl.BlockSpec((ts, Dm), lambda i: (i, 0))
    return tilecall(body, name, (S // ts,), [(y, row), (target, row)],
                    [(_sds((S, Dm)), row, None),
                     (_sds((SUBLANES, LANES)), pl.BlockSpec((SUBLANES, LANES), lambda i: (0, 0)), lambda: pl.program_id(0) == 0)],
                    ("arbitrary",))


_PEER_FLIPS = {
    "chips": ((1, 0, 0), (0, 1, 0), (1, 1, 0)),
    "sibling": ((0, 0, 1),),
    "all": tuple((a, b, c) for a in (0, 1) for b in (0, 1) for c in (0, 1))[1:],
}
_SLOT_WEIGHTS = {"chips": (2, 1, 0), "sibling": (0, 0, 1), "all": (4, 2, 1)}
_HBM = pl.BlockSpec(memory_space=pltpu.HBM)


def _me():
    return lax.axis_index("x"), lax.axis_index("y"), lax.axis_index("c")


def _remote(src, dst, send_sem, recv_sem, peer):
    return pltpu.make_async_remote_copy(src_ref=src, dst_ref=dst, send_sem=send_sem, recv_sem=recv_sem,
                                        device_id=peer, device_id_type=pl.DeviceIdType.MESH)


def exchange(arrs, group, slab_weights, name, keep_own=True):
    flips = _PEER_FLIPS[group]
    wx, wy, wc = _SLOT_WEIGHTS[group]
    n_slots = len(flips) + (1 if keep_own else 0)
    n = len(arrs)

    def slab(ref, a, pos):
        w = slab_weights[a]
        return ref if w is None else ref.at[w[0] * pos[0] + w[1] * pos[1] + w[2] * pos[2]]

    def kern(*refs):
        srcs, outs = refs[:n], refs[n:2 * n]
        send_sems, recv_sems = refs[2 * n:2 * n + 2]
        me = _me()
        my_slot = wx * me[0] + wy * me[1] + wc * me[2]
        copies = []
        if keep_own:
            local_sems = refs[2 * n + 2]
            for a in range(n):
                cp = pltpu.make_async_copy(slab(srcs[a], a, me), outs[a].at[my_slot], local_sems.at[a])
                cp.start()
                copies.append(cp)
        for f, flip in enumerate(flips):
            peer = tuple(m ^ b if b else m for m, b in zip(me, flip))
            for a in range(n):
                cp = _remote(slab(srcs[a], a, peer), outs[a].at[my_slot if keep_own else f],
                             send_sems.at[f, a], recv_sems.at[f, a], peer)
                cp.start()
                copies.append(cp)
        for cp in copies:
            cp.wait()

    out_shape = [_sds((n_slots,) + (a.shape if slab_weights[k] is None else a.shape[1:]), a.dtype) for k, a in enumerate(arrs)]
    sems = [pltpu.SemaphoreType.DMA((len(flips), n)), pltpu.SemaphoreType.DMA((len(flips), n))]
    return pl.pallas_call(
        kern, name=name, in_specs=[_HBM] * n, out_specs=[_HBM] * n, out_shape=out_shape,
        scratch_shapes=sems + ([pltpu.SemaphoreType.DMA((n,))] if keep_own else []))(*arrs)


def _chip_window(ref, kind, size, chip, layers):
    if kind == "rows":
        return ref.at[layers, pl.ds(chip * size, size), :]
    return ref.at[layers, :, pl.ds(pl.multiple_of(chip * size, LANES), size)]


def gather_big(shards, kinds, name):
    n = len(shards)
    fulls = []
    for s, kind in zip(shards, kinds):
        L, r, c = s.shape
        fulls.append(_sds((L, N_CHIPS * r, c) if kind == "rows" else (L, r, N_CHIPS * c), s.dtype))

    def kern(*refs):
        srcs, outs = refs[:n], refs[n:2 * n]
        ici_s, ici_r, relay_s, relay_r, d2d_s, d2d_r = refs[2 * n:]
        x, y, c = _me()
        sibling = (x, y, 1 - c)
        nbrs = ((x ^ 1, y), (x, y ^ 1))
        relay_from = (x ^ (1 - c), y ^ c)
        relay_to = (x ^ c, y ^ (1 - c), c)
        diagonal = (x ^ 1, y ^ 1)

        def window(a, chip):
            L, r, cc = shards[a].shape
            size = r if kinds[a] == "rows" else cc
            return _chip_window(outs[a], kinds[a], size, 2 * chip[0] + chip[1], pl.ds(c * (L // 2), L // 2))

        def direct(a, f):
            L = shards[a].shape[0]
            return _remote(srcs[a].at[pl.ds(c * (L // 2), L // 2)], window(a, (x, y)), ici_s.at[a, f], ici_r.at[a, f],
                           (nbrs[f][0], nbrs[f][1], c))

        def to_sibling(a, k, chip):
            return _remote(window(a, chip), window(a, chip), d2d_s.at[a, k], d2d_r.at[a, k], sibling)

        sends, relays, passed = [], [], []
        for a in range(n):
            for f in range(2):
                cp = direct(a, f)
                cp.start()
                sends.append(cp)
        for a in range(n):
            _remote(window(a, relay_from), window(a, relay_from), ici_s.at[a, c], ici_r.at[a, c], relay_to).wait_recv()
            cp = _remote(window(a, relay_from), window(a, relay_from), relay_s.at[a], relay_r.at[a], relay_to)
            cp.start()
            relays.append(cp)
            passed.append(to_sibling(a, 0, relay_from))
            passed[-1].start()
            _remote(window(a, relay_from), window(a, relay_from), ici_s.at[a, 1 - c], ici_r.at[a, 1 - c], relay_to).wait_recv()
            passed.append(to_sibling(a, 1, (relay_to[0], relay_to[1])))
            passed[-1].start()
        for a in range(n):
            relays[a].wait_recv()
            passed.append(to_sibling(a, 2, diagonal))
            passed[-1].start()
        for cp in sends + relays:
            cp.wait_send()
        for cp in passed:
            cp.wait()

    pair = pltpu.SemaphoreType.DMA((n, 2))
    one = pltpu.SemaphoreType.DMA((n,))
    three = pltpu.SemaphoreType.DMA((n, 3))
    return pl.pallas_call(kern, name=name, in_specs=[_HBM] * n, out_specs=[_HBM] * n, out_shape=fulls,
                          scratch_shapes=[pair, pair, one, one, three, three])(*shards)


def send_other_half(arrs, name):
    n = len(arrs)

    def kern(*refs):
        srcs, outs = refs[:n], refs[n:2 * n]
        send_sems, recv_sems = refs[2 * n:]
        x, y, c = _me()
        copies = []
        for a in range(n):
            hl = arrs[a].shape[0] // 2
            cp = _remote(srcs[a].at[pl.ds((1 - c) * hl, hl)], outs[a], send_sems.at[a], recv_sems.at[a], (x, y, 1 - c))
            cp.start()
            copies.append(cp)
        for cp in copies:
            cp.wait()

    return pl.pallas_call(
        kern, name=name, in_specs=[_HBM] * n, out_specs=[_HBM] * n,
        out_shape=[_sds((a.shape[0] // 2,) + a.shape[1:], a.dtype) for a in arrs],
        scratch_shapes=[pltpu.SemaphoreType.DMA((n,)), pltpu.SemaphoreType.DMA((n,))])(*arrs)


def _axis_neighbours():
    x, y, c = _me()
    return (x, y, c), (x ^ (1 - c), y ^ c), (x ^ c, y ^ (1 - c)), (x ^ 1, y ^ 1)


def reduce_first_axis(arrs, kinds, name):
    n = len(arrs)
    shapes = []
    for a, kind in zip(arrs, kinds):
        l, R, C = a.shape
        shapes.append((l, R // N_CHIPS, C) if kind == "rows" else (l, R, C // N_CHIPS))

    def kern(*refs):
        srcs, outs = refs[:n], refs[n:2 * n]
        send_sems, recv_sems = refs[2 * n:]
        (x, y, c), first, second, diagonal = _axis_neighbours()
        copies = []
        for a in range(n):
            size = shapes[a][1] if kinds[a] == "rows" else shapes[a][2]
            for k, chip in enumerate((first, diagonal)):
                window = _chip_window(srcs[a], kinds[a], size, 2 * chip[0] + chip[1], slice(None))
                cp = _remote(window, outs[a].at[k], send_sems.at[a, k], recv_sems.at[a, k], (first[0], first[1], c))
                cp.start()
                copies.append(cp)
        for cp in copies:
            cp.wait()

    sem = pltpu.SemaphoreType.DMA((n, 2))
    return pl.pallas_call(
        kern, name=name, in_specs=[_HBM] * n, out_specs=[_HBM] * n,
        out_shape=[_sds((2,) + s, a.dtype) for s, a in zip(shapes, arrs)], scratch_shapes=[sem, sem])(*arrs)


def reduce_second_axis(arrs, name):
    n = len(arrs)

    def kern(*refs):
        srcs, outs = refs[:n], refs[n:2 * n]
        send_sems, recv_sems = refs[2 * n:]
        (x, y, c), first, second, diagonal = _axis_neighbours()
        copies = []
        for a in range(n):
            cp = _remote(srcs[a], outs[a], send_sems.at[a], recv_sems.at[a], (second[0], second[1], c))
            cp.start()
            copies.append(cp)
        for cp in copies:
            cp.wait()

    return pl.pallas_call(
        kern, name=name, in_specs=[_HBM] * n, out_specs=[_HBM] * n, out_shape=[_sds(a.shape, a.dtype) for a in arrs],
        scratch_shapes=[pltpu.SemaphoreType.DMA((n,)), pltpu.SemaphoreType.DMA((n,))])(*arrs)


def _stack_tile(r, c):
    for t in (1024, 704, 512, 352, 256, 128, 64, 32, 16):
        if r % t == 0 and t * c * 4 <= 3 * 512 * 1024:
            return t
    return r


def _window_map(kind, r, tr):
    nrt = r // tr
    if kind == "rows":
        return lambda l, i, chip: (l, chip[0] * nrt + i, 0)
    return lambda l, i, chip: (l, i, chip[0])


def place(full, shard, kind, chip, name):
    L, r, c = shard.shape
    tr = _stack_tile(r, c)
    wmap = _window_map(kind, r, tr)
    return tilecall(lambda chip_ref, s_ref, f_ref: (s_ref[...],), name, (L, r // tr),
                    [(shard, pl.BlockSpec((None, tr, c), lambda l, i, chip: (l, i, 0))), (full, pl.BlockSpec(memory_space=pl.ANY))],
                    [(_sds(full.shape, full.dtype), pl.BlockSpec((None, tr, c), lambda l, i, chip: wmap(l, i, chip)), None)],
                    ("parallel", "parallel"), prefetch=(chip,), aliases={1: 0})[0]


def add_cores(g, other, core, name):
    L, R, C = g.shape
    hl = L // 2
    tr = _stack_tile(R, C)
    blk = (None, tr, C)
    return tilecall(lambda core_ref, a_ref, b_ref: (a_ref[...] + b_ref[...],), name, (hl, R // tr),
                    [(g, pl.BlockSpec(blk, lambda l, i, core: (core[0] * hl + l, i, 0))),
                     (other, pl.BlockSpec(blk, lambda l, i, core: (l, i, 0)))],
                    [(_sds((hl, R, C), _MXU), pl.BlockSpec(blk, lambda l, i, core: (l, i, 0)), None)],
                    ("parallel", "parallel"), prefetch=(core,))[0]


def add_first_axis(own, got, kind, chips, name):
    _, l, r, c = got.shape
    tr = _stack_tile(r, c)
    nrt = r // tr

    def window(k):
        if kind == "rows":
            return pl.BlockSpec((None, tr, c), lambda ll, i, ch: (ll, ch[k] * nrt + i, 0))
        return pl.BlockSpec((None, tr, c), lambda ll, i, ch: (ll, i, ch[k]))

    def body(chips_ref, own0, own1, got0, got1):
        return own0[...].astype(F32) + got0[...].astype(F32), own1[...].astype(F32) + got1[...].astype(F32)

    plain = pl.BlockSpec((None, tr, c), lambda ll, i, ch: (ll, i, 0))
    return tilecall(body, name, (l, r // tr),
                    [(own, window(0)), (own, window(1))] +
                    [(got, pl.BlockSpec((None, None, tr, c), lambda ll, i, ch, k=k: (k, ll, i, 0))) for k in range(2)],
                    [(_sds((l, r, c)), plain, None), (_sds((l, r, c), _MXU), plain, None)],
                    ("parallel", "parallel"), prefetch=(chips,))


def add_second_axis(mine, got, name):
    l, r, c = mine.shape
    tr = _stack_tile(r, c)
    blk = pl.BlockSpec((None, tr, c), lambda ll, i: (ll, i, 0))
    return tilecall(lambda a_ref, b_ref: (a_ref[...] + b_ref[...].astype(F32),), name, (l, r // tr),
                    [(mine, blk), (got, blk)], [(_sds((l, r, c)), blk, None)], ("parallel", "parallel"))[0]


def _adam_update(g, w, m, v):
    m_new = ADAM_B1 * m + (1.0 - ADAM_B1) * g
    v_new = ADAM_B2 * v + (1.0 - ADAM_B2) * jnp.square(g)
    m_hat = m_new / (1.0 - ADAM_B1 ** ADAM_STEP)
    v_hat = v_new / (1.0 - ADAM_B2 ** ADAM_STEP)
    delta = -ADAM_LR * (m_hat / (jnp.sqrt(v_hat) + ADAM_EPS) + ADAM_WD * w)
    return g, delta, m_new, v_new


def adamw_stacked(mine, theirs, w, m, v, core, name):
    L, r, c = w.shape
    hl = L // 2
    tr = _stack_tile(r, c)

    def body(core_ref, a_ref, b_ref, w_ref, m_ref, v_ref):
        is_mine = (pl.program_id(0) // hl) == core_ref[0]
        g = jnp.where(is_mine, a_ref[...], b_ref[...])
        return _adam_update(g, w_ref[...], m_ref[...], v_ref[...])

    full = pl.BlockSpec((None, tr, c), lambda l, i, core: (l, i, 0))
    out = (_sds((L, r, c)), full, None)
    return tilecall(body, name, (L, r // tr),
                    [(mine, pl.BlockSpec((None, tr, c), lambda l, i, core: (l % hl, i, 0))),
                     (theirs, pl.BlockSpec((None, None, tr, c), lambda l, i, core: (0, l % hl, i, 0))),
                     (w, full), (m, full), (v, full)],
                    [out, out, out, out], ("parallel", "parallel"), prefetch=(core,))


def _pack(arrs, dtype, row_multiple):
    flat = jnp.concatenate([a.reshape(-1).astype(dtype) for a in arrs])
    rows = -(-flat.shape[0] // LANES)
    rows = -(-rows // row_multiple) * row_multiple
    return jnp.pad(flat, (0, rows * LANES - flat.shape[0])).reshape(rows, LANES)


def _unpack(buf, shapes):
    flat = buf.reshape(-1)
    out, off = [], 0
    for s in shapes:
        n = int(np.prod(s))
        out.append(flat[off:off + n].reshape(s))
        off += n
    return out


def adamw_packed(gparts, w, m, v, name):
    P, R, _ = gparts.shape

    def body(g_ref, w_ref, m_ref, v_ref):
        g = g_ref[0]
        for p in range(1, P):
            g = g + g_ref[p]
        return _adam_update(g, w_ref[...], m_ref[...], v_ref[...])

    whole = pl.BlockSpec((R, LANES), lambda: (0, 0))
    out = (_sds((R, LANES)), whole, None)
    return tilecall(body, name, (), [(gparts, pl.BlockSpec((P, R, LANES), lambda: (0, 0, 0))), (w, whole), (m, whole), (v, whole)],
                    [out, out, out, out], ())


def _residual_out(a, w, x, next_gain):
    if next_gain is None:
        return mm(a, w, "nn", "mm_nn_add", add=x), None
    return mm(a, w, "nn", "mm_nn_add_rms", add=x, rms_gain=next_gain)


def _input_grad(pieces, x, gain, res):
    dh = None
    for d, w in pieces[:-1]:
        dh = mm(d, w, "nt", "mm_nt" if dh is None else "mm_nt_add", add=dh)
    d, w = pieces[-1]
    return mm(d, w, "nt", "mm_nt_norm_bwd", add=dh, norm_bwd=(x, gain, res))


def _ffn_fwd(x, h, next_gain, layer, gain, wts, cw8, cb):
    u = mm(h, win(wts["ffn_w_up"], layer), "nn", "mm_nn", out_dtype=_MXU)
    a = ffn_act_fwd(u, cw8, cb, "ffn_act_fwd")
    y, h_next = _residual_out(a, win(wts["ffn_w_down"], layer), x, next_gain)
    return y, h_next, (x, h, u, a)


def _ffn_bwd(dy, saved, layer, gain, wts, cw8, cb, grads):
    x, h, u, a = saved
    grads["ffn_w_down"] = mm(a, dy, "tn", "mm_tn_into", into=win(grads["ffn_w_down"], layer))
    da = mm(dy, win(wts["ffn_w_down"], layer), "nt", "mm_nt", out_dtype=_MXU)
    dyg, dyu, dcw_g, dcw_u, dcb_g, dcb_u = ffn_act_bwd(u, cw8, cb, da, "ffn_act_bwd")
    pieces = []
    for half, du in enumerate(ffn_conv_bwd(dyg, dyu, cw8, "ffn_conv_bwd")):
        cols = dict(col_off=half * D_FF, cols=D_FF)
        grads["ffn_w_up"] = mm(h, du, "tn", "mm_tn_into", into=win(grads["ffn_w_up"], layer, **cols))
        pieces.append((du, win(wts["ffn_w_up"], layer, **cols)))
    dx, d_gain = _input_grad(pieces, x, gain, dy)
    return dx, dict(gain=d_gain, conv_w=jnp.concatenate([dcw_g[0:3], dcw_u[0:3]], axis=1),
                    conv_b=jnp.concatenate([dcb_g, dcb_u], axis=1))


def _hgrn_w_in(wts, j, k):
    return win(wts["hgrn_w_in"], j, row_off=k * D_MODEL, rows=D_MODEL)


def _hgrn_layer_fwd(x, h, next_gain, j, gain, wts, lb, out_gain, mall):
    z = [mm(h, _hgrn_w_in(wts, j, k), "nn", "mm_nn") for k in range(4)]
    o, on, states = hgrn_fwd(z[0], z[1], z[2], z[3], lb, out_gain, mall, "hgrn_fwd")
    y, h_next = _residual_out(on, win(wts["hgrn_w_out"], j), x, next_gain)
    return y, h_next, (x, h, z, o, states, on)


def _hgrn_layer_bwd(dy, saved, j, gain, wts, lb, out_gain, mall, grads):
    x, h, z, o, states, on = saved
    grads["hgrn_w_out"] = mm(on, dy, "tn", "mm_tn_into", into=win(grads["hgrn_w_out"], j))
    don = mm(dy, win(wts["hgrn_w_out"], j), "nt", "mm_nt")
    dzq, dzf, dzi, dzg, dlb, d_out_gain = hgrn_bwd(z[0], z[1], z[2], z[3], lb, out_gain, mall, states, o, don, "hgrn_bwd")
    dz = [dzq, dzf, dzi, dzg]
    for k, d in enumerate(dz):
        grads["hgrn_w_in"] = mm(h, d, "tn", "mm_tn_into", into=win(grads["hgrn_w_in"], j, row_off=k * D_MODEL, rows=D_MODEL))
    dx, d_gain = _input_grad([(d, _hgrn_w_in(wts, j, k)) for k, d in enumerate(dz)], x, gain, dy)
    return dx, dict(gain=d_gain, lb=dlb, out_gain=d_out_gain)


_MLA_IN_WINDOWS = ((0, MLA_LORA), (MLA_LORA, MLA_LORA), (2 * MLA_LORA, HEAD_DIM))


def _mla_layer_fwd(x, h, next_gain, j, gain, wts, qa_gain, kva_gain, qn, kn, cos_t, sin_t):
    (c0, n), (c1, n1), (c2, n2) = _MLA_IN_WINDOWS
    cq, cqn = mm(h, win(wts["mla_w_in"], j, col_off=c0, cols=n), "nn", "mm_nn_rms", rms_gain=qa_gain)
    ckv, ckvn = mm(h, win(wts["mla_w_in"], j, col_off=c1, cols=n1), "nn", "mm_nn_rms", rms_gain=kva_gain)
    kr = mm(h, win(wts["mla_w_in"], j, col_off=c2, cols=n2), "nn", "mm_nn")
    qslots = mm(cqn, win(wts["mla_w_q_up"], j), "nn", "mm_nn")
    kv = mm(ckvn, win(wts["mla_w_kv_up"], j), "nn", "mm_nn")
    qr, krot = qk_fwd(qslots, kv, kr, qn, kn, cos_t, sin_t, "qk_fwd")
    o, lse = attn_fwd(qr, krot, kv, "attn_fwd")
    y, h_next = _residual_out(o, win(wts["mla_w_out"], j), x, next_gain)
    return y, h_next, (x, h, cq, ckv, kr, cqn, ckvn, qslots, kv, qr, krot, o, lse)


def _mla_layer_bwd(dy, saved, j, gain, wts, qa_gain, kva_gain, qn, kn, cos_t, sin_t, grads):
    x, h, cq, ckv, kr, cqn, ckvn, qslots, kv, qr, krot, o, lse = saved
    grads["mla_w_out"] = mm(o, dy, "tn", "mm_tn_into", into=win(grads["mla_w_out"], j))
    do = mm(dy, win(wts["mla_w_out"], j), "nt", "mm_nt")
    dq, dk, dv = attn_bwd(qr, krot, kv, o, lse, do, "attn_bwd")
    dqslots, dkv, dkr, d_qn, d_kn = qk_bwd(qslots, kv, kr, qn, kn, cos_t, sin_t, dq, dk, dv, "qk_bwd")
    grads["mla_w_q_up"] = mm(cqn, dqslots, "tn", "mm_tn_into", into=win(grads["mla_w_q_up"], j))
    dcq, d_qa = _input_grad([(dqslots, win(wts["mla_w_q_up"], j))], cq, qa_gain, None)
    grads["mla_w_kv_up"] = mm(ckvn, dkv, "tn", "mm_tn_into", into=win(grads["mla_w_kv_up"], j))
    dckv, d_kva = _input_grad([(dkv, win(wts["mla_w_kv_up"], j))], ckv, kva_gain, None)
    pieces = []
    for d, (c0, n) in zip((dcq, dckv, dkr), _MLA_IN_WINDOWS):
        grads["mla_w_in"] = mm(h, d, "tn", "mm_tn_into", into=win(grads["mla_w_in"], j, col_off=c0, cols=n))
        pieces.append((d, win(wts["mla_w_in"], j, col_off=c0, cols=n)))
    dx, d_gain = _input_grad(pieces, x, gain, dy)
    return dx, dict(gain=d_gain, qa=d_qa, kva=d_kva, qn=d_qn, kn=d_kn)


BIG = (("hgrn_w_in", "rows"), ("hgrn_w_out", "rows"), ("mla_w_in", "rows"), ("mla_w_q_up", "cols"),
       ("mla_w_kv_up", "cols"), ("mla_w_out", "rows"), ("ffn_w_up", "cols"), ("ffn_w_down", "rows"))
SMALL_SHARDED = (("mla_q_a_norm", 1), ("mla_kv_a_norm", 1), ("ffn_conv_w", 2))
REPLICATED = ("norm_mix", "norm_ffn", "hgrn_lower_bounds", "hgrn_out_norm", "mla_q_norm", "mla_k_norm", "ffn_conv_b")
WEIGHTS = ("norm_mix", "norm_ffn", "hgrn_w_in", "hgrn_lower_bounds", "hgrn_out_norm", "hgrn_w_out", "mla_w_in",
           "mla_q_a_norm", "mla_w_q_up", "mla_kv_a_norm", "mla_w_kv_up", "mla_q_norm", "mla_k_norm", "mla_w_out",
           "ffn_w_up", "ffn_conv_w", "ffn_conv_b", "ffn_w_down")


def _pad_cols(a, width):
    return jnp.pad(a, [(0, 0)] * (a.ndim - 1) + [(0, width - a.shape[-1])])


def _head_slots(w):
    lead, n = w.shape[:-1], w.shape[-1] // MLA_QK
    return _pad_cols(w.reshape(lead + (n, MLA_QK)), MLA_SLOT).reshape(lead + (n * MLA_SLOT,))


def _head_unslots(w):
    lead, n = w.shape[:-1], w.shape[-1] // MLA_SLOT
    return w.reshape(lead + (n, MLA_SLOT))[..., :MLA_QK].reshape(lead + (n * MLA_QK,))


def _to_stack_layout(name, a):
    if name == "hgrn_w_in":
        return a
    if name == "mla_w_in":
        return _pad_cols(a, MLA_IN_COLS)
    if name == "mla_w_q_up":
        return _head_slots(a)
    return a


def _from_stack_layout(name, a):
    if name == "mla_w_in":
        return a[..., :2 * MLA_LORA + MLA_ROPE]
    if name == "mla_w_q_up":
        return _head_unslots(a)
    return a


def _rope_tables(positions):
    inv_freq = ROPE_THETA ** (-jnp.arange(0, MLA_ROPE, 2, dtype=F32) / MLA_ROPE)
    ang = positions.astype(F32)[:, None] * inv_freq
    cos, sin = jnp.cos(ang), jnp.sin(ang)
    S = positions.shape[0]
    ones, zeros = jnp.ones((S, MLA_NOPE), F32), jnp.zeros((S, MLA_SLOT - MLA_QK), F32)
    return (jnp.concatenate([ones, cos, cos, zeros], axis=1),
            jnp.concatenate([jnp.zeros((S, MLA_NOPE), F32), -sin, sin, zeros], axis=1))


def kernel(x, positions, norm_mix, norm_ffn, hgrn_w_in, hgrn_lower_bounds, hgrn_out_norm, hgrn_w_out, mla_w_in, mla_q_a_norm, mla_w_q_up, mla_kv_a_norm, mla_w_kv_up, mla_q_norm, mla_k_norm, mla_w_out, ffn_w_up, ffn_conv_w, ffn_conv_b, ffn_w_down, loss_target, m_norm_mix, m_norm_ffn, m_hgrn_w_in, m_hgrn_lower_bounds, m_hgrn_out_norm, m_hgrn_w_out, m_mla_w_in, m_mla_q_a_norm, m_mla_w_q_up, m_mla_kv_a_norm, m_mla_w_kv_up, m_mla_q_norm, m_mla_k_norm, m_mla_w_out, m_ffn_w_up, m_ffn_conv_w, m_ffn_conv_b, m_ffn_w_down, v_norm_mix, v_norm_ffn, v_hgrn_w_in, v_hgrn_lower_bounds, v_hgrn_out_norm, v_hgrn_w_out, v_mla_w_in, v_mla_q_a_norm, v_mla_w_q_up, v_mla_kv_a_norm, v_mla_w_kv_up, v_mla_q_norm, v_mla_k_norm, v_mla_w_out, v_ffn_w_up, v_ffn_conv_w, v_ffn_conv_b, v_ffn_w_down):
    args = dict(locals())
    w = {n: args[n] for n in WEIGHTS}
    m = {n: args["m_" + n] for n in WEIGHTS}
    v = {n: args["v_" + n] for n in WEIGHTS}
    depth = norm_mix.shape[0]
    x0 = x[0]
    S = x0.shape[0]
    chip = (2 * lax.axis_index("x") + lax.axis_index("y")).astype(jnp.int32).reshape(1)
    core = lax.axis_index("c").astype(jnp.int32).reshape(1)
    big_names = [n for n, _ in BIG]
    kinds = [k for _, k in BIG]
    small_names = [n for n, _ in SMALL_SHARDED]
    small_axis = dict(SMALL_SHARDED)

    local = {n: _to_stack_layout(n, w[n]) for n in big_names}
    gathered = gather_big([local[n].astype(_MXU) for n in big_names], kinds, "gather_weights")
    wts = {n: place(g, local[n], k, chip, "place") for n, k, g in zip(big_names, kinds, gathered)}
    (got_small,) = exchange([_pack([w[n] for n in small_names], F32, SUBLANES)], "chips", [None], "gather_small")
    per_chip = [_unpack(got_small[p], [w[n].shape for n in small_names]) for p in range(N_CHIPS)]
    small = {n: jnp.concatenate([per_chip[p][k] for p in range(N_CHIPS)], axis=small_axis[n]) for k, n in enumerate(small_names)}

    cos_t, sin_t = _rope_tables(positions[0])
    lbs = lower_bound_fwd(hgrn_lower_bounds, "lower_bound_fwd")
    mall = jnp.asarray(_hgrn_sum_matrix(min(HGRN_CHUNK, S)), _MXU)
    qn = _pad_cols(mla_q_norm, MLA_SLOT)
    kn = _pad_cols(mla_k_norm, MLA_SLOT)
    cw8 = jnp.pad(small["ffn_conv_w"], ((0, 0), (0, SUBLANES - 3), (0, 0)))

    def mixer_args(layer):
        j = layer // 2
        if layer % 2 == 0:
            return (j, norm_mix[layer:layer + 1], wts, lbs[j:j + 1], hgrn_out_norm[j:j + 1], mall)
        return (j, norm_mix[layer:layer + 1], wts, small["mla_q_a_norm"][j:j + 1], small["mla_kv_a_norm"][j:j + 1],
                qn[j:j + 1], kn[j:j + 1], cos_t, sin_t)

    def ffn_args(layer):
        return (layer, norm_ffn[layer:layer + 1], wts, cw8[layer], ffn_conv_b[layer:layer + 1])

    xc, h = x0, rms_fwd(x0, norm_mix[0:1], "rms_fwd")
    saved = []
    for layer in range(depth):
        fwd = _hgrn_layer_fwd if layer % 2 == 0 else _mla_layer_fwd
        xc, h, s_mix = fwd(xc, h, norm_ffn[layer:layer + 1], *mixer_args(layer))
        xc, h, s_ffn = _ffn_fwd(xc, h, norm_mix[layer + 1:layer + 2] if layer + 1 < depth else None, *ffn_args(layer))
        saved.append((s_mix, s_ffn))

    dh, loss_blk = loss_head(xc, loss_target[0], "loss_head")
    loss = lax.psum(loss_blk[0, 0], MESH_AXES)

    grads = {n: lax.empty(g.shape, F32) for n, g in zip(big_names, gathered)}
    g_mix, g_ffn = [None] * depth, [None] * depth
    for layer in reversed(range(depth)):
        s_mix, s_ffn = saved[layer]
        dh, g_ffn[layer] = _ffn_bwd(dh, s_ffn, *ffn_args(layer), grads)
        bwd = _hgrn_layer_bwd if layer % 2 == 0 else _mla_layer_bwd
        dh, g_mix[layer] = bwd(dh, s_mix, *mixer_args(layer), grads)
    hg = [g_mix[l] for l in range(0, depth, 2)]
    mg = [g_mix[l] for l in range(1, depth, 2)]
    d_p = lower_bound_bwd(hgrn_lower_bounds, hg[1]["lb"], "lower_bound_bwd")

    g_list = [grads[n] for n in big_names]
    from_core = send_other_half(g_list, "reduce_cores_in")
    chip_sums = [add_cores(g, o, core, "add_cores") for g, o in zip(g_list, from_core)]
    cx, cy, cc = lax.axis_index("x"), lax.axis_index("y"), lax.axis_index("c")
    chips = jnp.stack([2 * cx + cy, 2 * (cx ^ cc) + (cy ^ (1 - cc))]).astype(jnp.int32)
    from_first = reduce_first_axis(chip_sums, kinds, "reduce_axis_1")
    partial = [add_first_axis(own, got, k, chips, "add_axis_1") for own, got, k in zip(chip_sums, from_first, kinds)]
    from_second = reduce_second_axis([p[1] for p in partial], "reduce_axis_2")
    reduced = [add_second_axis(p[0], got, "add_axis_2") for p, got in zip(partial, from_second)]
    other_half = exchange(reduced, "sibling", [None] * len(reduced), "reduce_cores_out", keep_own=False)
    big_out = {}
    for n, mine, theirs in zip(big_names, reduced, other_half):
        outs = adamw_stacked(mine, theirs, local[n], _to_stack_layout(n, m[n]), _to_stack_layout(n, v[n]), core, "adamw")
        big_out[n] = [_from_stack_layout(n, o) for o in outs]

    small_grads = {
        "norm_mix": jnp.concatenate([g["gain"] for g in g_mix], axis=0),
        "norm_ffn": jnp.concatenate([g["gain"] for g in g_ffn], axis=0),
        "hgrn_lower_bounds": d_p[0:2],
        "hgrn_out_norm": jnp.concatenate([g["out_gain"] for g in hg], axis=0),
        "mla_q_a_norm": jnp.concatenate([g["qa"] for g in mg], axis=0),
        "mla_kv_a_norm": jnp.concatenate([g["kva"] for g in mg], axis=0),
        "mla_q_norm": jnp.concatenate([g["qn"][:, :MLA_QK] for g in mg], axis=0),
        "mla_k_norm": jnp.concatenate([g["kn"][:, :MLA_QK] for g in mg], axis=0),
        "ffn_conv_w": jnp.stack([g["conv_w"] for g in g_ffn]),
        "ffn_conv_b": jnp.concatenate([g["conv_b"] for g in g_ffn], axis=0),
    }

    def chip_part(n, p):
        size = w[n].shape[small_axis[n]]
        return lax.slice_in_dim(small_grads[n], p * size, (p + 1) * size, axis=small_axis[n])

    to_chips = jnp.stack([_pack([chip_part(n, p) for n in small_names], F32, SUBLANES) for p in range(N_CHIPS)])
    rep_g, shard_g = exchange([_pack([small_grads[n] for n in REPLICATED], F32, SUBLANES), to_chips], "all",
                              [None, (2, 1, 0)], "reduce_small")
    small_out = {}
    for names, gparts in ((REPLICATED, rep_g), (small_names, shard_g)):
        packed = adamw_packed(gparts, *[_pack([t[n] for n in names], F32, SUBLANES) for t in (w, m, v)], "adamw_small")
        unpacked = [_unpack(buf, [w[n].shape for n in names]) for buf in packed]
        for k, n in enumerate(names):
            small_out[n] = [u[k] for u in unpacked]

    result = [loss, dh[None]]
    for k in range(4):
        result += [(big_out[n] if n in big_out else small_out[n])[k] for n in WEIGHTS]
    return tuple(result)
```

```python
import numpy as np
import jax
import jax.numpy as jnp
from jax import lax
from jax.experimental import pallas as pl
from jax.experimental.pallas import tpu as pltpu

F32 = jnp.float32
BF16 = jnp.bfloat16
_MXU = BF16

RMS_EPS = 1e-6
D_MODEL = 1024
HEADS = 8
HEAD_DIM = 128
HGRN_CHUNK = 128
MLA_NOPE = 128
MLA_ROPE = 64
MLA_QK = MLA_NOPE + MLA_ROPE
MLA_SLOT = 256
MLA_LORA = 256
MLA_IN_COLS = 2 * MLA_LORA + HEAD_DIM
ROPE_THETA = 10000.0
D_FF = 2816
FF_BLOCK = 1408
LANES = 128
SUBLANES = 8

ADAM_LR = 0.001
ADAM_B1 = 0.9
ADAM_B2 = 0.999
ADAM_EPS = 1e-08
ADAM_WD = 0.01
ADAM_STEP = 10

VMEM_LIMIT = 56 * 1024 * 1024
MM_VMEM_BUDGET = 46 * 1024 * 1024
MESH_AXES = ("x", "y", "c")
N_CHIPS = 4

_NN = ((1,), (0,))
_NT = ((1,), (1,))
_TN = ((0,), (0,))


def _dg(a, b, dims):
    return lax.dot_general(a.astype(_MXU), b.astype(_MXU), (dims, ((), ())), preferred_element_type=F32)


@jax.custom_vjp
def kdot(a, b):
    return _dg(a, b, _NN)


kdot.defvjp(lambda a, b: (_dg(a, b, _NN), (a, b)), lambda r, g: (_dg(g, r[1], _NT), _dg(r[0], g, _TN)))


@jax.custom_vjp
def kdot_nt(a, b):
    return _dg(a, b, _NT)


kdot_nt.defvjp(lambda a, b: (_dg(a, b, _NT), (a, b)), lambda r, g: (_dg(g, r[1], _NN), _dg(g, r[0], _TN)))


@jax.custom_vjp
def kdot_tn(a, b):
    return _dg(a, b, _TN)


kdot_tn.defvjp(lambda a, b: (_dg(a, b, _TN), (a, b)), lambda r, g: (_dg(r[1], g, _NT), _dg(r[0], g, _NN)))


def _pick(d, prefs):
    for p in prefs:
        if d >= p and d % p == 0:
            return p
    return d


def _params(sem):
    return pltpu.CompilerParams(dimension_semantics=sem, vmem_limit_bytes=VMEM_LIMIT)


def _sds(shape, dtype=F32):
    return jax.ShapeDtypeStruct(shape, dtype)


def win(arr, layer, row_off=0, col_off=0, rows=None, cols=None):
    return (arr, layer, row_off, col_off, rows or arr.shape[1] - row_off, cols or arr.shape[2] - col_off)


def mm(a, b, mode, name, add=None, out_dtype=F32, into=None, rms_gain=None, norm_bwd=None):
    if isinstance(b, tuple):
        b_arr, b_layer, b_r0, b_c0, b_rows, b_cols = b
    else:
        b_arr, b_layer, b_r0, b_c0, (b_rows, b_cols) = b, None, 0, 0, b.shape
    if mode == "nn":
        (M, K), (K2, N) = a.shape, (b_rows, b_cols)
    elif mode == "nt":
        (M, K), (N, K2) = a.shape, (b_rows, b_cols)
    else:
        (K, M), (K2, N) = a.shape, (b_rows, b_cols)
    assert K == K2, (name, a.shape, b_rows, b_cols)
    tn = N if N <= 1024 else _pick(N, (1024, 1408, 512, 256, 128))
    tk = K if K <= 2048 else _pick(K, (2048, 2816, 1024, 512, 256, 128))
    nk = K // tk
    a_bytes = jnp.dtype(a.dtype).itemsize
    b_bytes = jnp.dtype(b_arr.dtype).itemsize
    extra_tiles = (add is not None) + (rms_gain is not None) / 2 + (3 if norm_bwd is not None else 0)

    def vmem_bytes(tm_):
        tiles = 2 * (tm_ * tk * a_bytes + tk * tn * b_bytes) + tm_ * tn * 4 * (2 + 2 * extra_tiles + (nk > 1))
        return tiles

    tm = M
    if M > 1024:
        fits = [t for t in (2048, 1024, 1408, 512, 256, 128) if M % t == 0 and vmem_bytes(t) <= MM_VMEM_BUDGET]
        tm = fits[0] if fits else _pick(M, (128,))
    dims = {"nn": _NN, "nt": _NT, "tn": _TN}[mode]
    a_spec = pl.BlockSpec((tk, tm), lambda i, j, k: (k, i)) if mode == "tn" else pl.BlockSpec((tm, tk), lambda i, j, k: (i, k))
    b_blk = (tn, tk) if mode == "nt" else (tk, tn)
    assert b_r0 % b_blk[0] == 0 and b_c0 % b_blk[1] == 0, (name, b_r0, b_c0, b_blk)
    br, bc = b_r0 // b_blk[0], b_c0 // b_blk[1]
    if mode == "nt":
        b_idx = lambda i, j, k: (br + j, bc + k)
    else:
        b_idx = lambda i, j, k: (br + k, bc + j)
    if b_layer is None:
        b_spec = pl.BlockSpec(b_blk, b_idx)
    else:
        b_spec = pl.BlockSpec((None,) + b_blk, lambda i, j, k: (b_layer,) + b_idx(i, j, k))
    plain = pl.BlockSpec((tm, tn), lambda i, j, k: (i, j))
    has_add, has_rms, has_nb = add is not None, rms_gain is not None, norm_bwd is not None
    assert not (has_rms or has_nb) or (tn == N and into is None and not (has_rms and has_nb)), name
    vec = pl.BlockSpec((1, tn), lambda i, j, k: (0, j))
    ins = [a, b_arr] + ([add] if has_add else []) + ([rms_gain] if has_rms else [])
    specs = [a_spec, b_spec] + ([plain] if has_add else []) + ([vec] if has_rms else [])
    n_in = len(ins)
    if has_nb:
        nb_x, nb_gain, nb_res = norm_bwd
        ins += [nb_x, nb_gain] + ([nb_res] if nb_res is not None else [])
        specs += [plain, vec] + ([plain] if nb_res is not None else [])
    aliases = {}
    if into is None:
        o_spec, out_shape = plain, _sds((M, N), out_dtype)
    else:
        buf, o_layer, o_r0, o_c0, o_rows, o_cols = into
        assert (o_rows, o_cols) == (M, N) and o_r0 % tm == 0 and o_c0 % tn == 0, (name, into[1:], M, N, tm, tn)
        orow, ocol = o_r0 // tm, o_c0 // tn
        o_spec = pl.BlockSpec((None, tm, tn), lambda i, j, k: (o_layer, orow + i, ocol + j))
        out_shape = _sds(buf.shape, buf.dtype)
        aliases = {len(ins): 0}
        ins.append(buf)
        specs.append(pl.BlockSpec(memory_space=pl.ANY))

    n_all_in = len(ins)

    def kern(*refs):
        a_ref, b_ref = refs[0], refs[1]
        add_ref = refs[2] if has_add else None
        o_ref = refs[n_all_in]

        def finish(r):
            if has_add:
                r = r + add_ref[...].astype(F32)
            if has_nb:
                _, vjp = jax.vjp(lambda xv, gv: _rms(xv, gv, N), refs[n_in][...], refs[n_in + 1][...])
                dx, dgain = vjp(r)
                o_ref[...] = dx if nb_res is None else dx + refs[n_in + 2][...]
                _store(refs[n_all_in + 1], dgain, pl.program_id(0) == 0)
                return
            o_ref[...] = r.astype(o_ref.dtype)
            if has_rms:
                refs[n_all_in + 1][...] = _rms(r, refs[n_in - 1][...], N).astype(_MXU)

        if nk == 1:
            finish(_dg(a_ref[...], b_ref[...], dims))
            return
        acc = refs[-1]
        k = pl.program_id(2)

        @pl.when(k == 0)
        def _():
            acc[...] = jnp.zeros_like(acc)

        acc[...] += _dg(a_ref[...], b_ref[...], dims)

        @pl.when(k == nk - 1)
        def _():
            finish(acc[...])

    if has_rms:
        o_spec, out_shape = [o_spec, plain], [out_shape, _sds((M, N), _MXU)]
    if has_nb:
        o_spec, out_shape = [o_spec, vec], [out_shape, _sds((1, N))]
    return pl.pallas_call(
        kern, name=name, grid=(M // tm, N // tn, nk), in_specs=specs, out_specs=o_spec, out_shape=out_shape,
        scratch_shapes=[pltpu.VMEM((tm, tn), F32)] if nk > 1 else [], input_output_aliases=aliases,
        compiler_params=_params(("arbitrary" if has_nb else "parallel", "parallel", "arbitrary")))(*ins)


def _store(ref, val, first):
    if first is None:
        ref[...] = val.astype(ref.dtype)
        return

    @pl.when(first)
    def _():
        ref[...] = val.astype(ref.dtype)

    @pl.when(jnp.logical_not(first))
    def _():
        ref[...] += val.astype(ref.dtype)


def tilecall(body, name, grid, ins, outs, sem, prefetch=(), aliases=None):
    n_pre, n_in = len(prefetch), len(ins)

    def kern(*refs):
        vals = body(*refs[:n_pre + n_in])
        for ref, val, (_, _, first) in zip(refs[n_pre + n_in:], vals, outs):
            _store(ref, val, None if first is None else first())

    in_specs, out_specs = [s for _, s in ins], [s for _, s, _ in outs]
    kwargs = dict(name=name, out_shape=[sh for sh, _, _ in outs], compiler_params=_params(sem),
                  input_output_aliases={n_pre + k: v for k, v in (aliases or {}).items()})
    if n_pre:
        kwargs["grid_spec"] = pltpu.PrefetchScalarGridSpec(num_scalar_prefetch=n_pre, grid=grid, in_specs=in_specs,
                                                           out_specs=out_specs)
    else:
        kwargs.update(grid=grid, in_specs=in_specs, out_specs=out_specs)
    return pl.pallas_call(kern, **kwargs)(*prefetch, *[a for a, _ in ins])


def _rms(x, g, n):
    ms = jnp.sum(x * x, axis=-1, keepdims=True) / n
    return x * lax.rsqrt(ms + RMS_EPS) * g


def _row_tile(S, w):
    return min(S, 512 if w <= 1024 else 256)


def rms_fwd(x, g, name, col=0, w=None):
    S = x.shape[0]
    w = w or x.shape[1]
    ts = _row_tile(S, w)
    return tilecall(
        lambda x_ref, g_ref: (_rms(x_ref[...], g_ref[...], w),), name, (S // ts,),
        [(x, pl.BlockSpec((ts, w), lambda i: (i, col))), (g, pl.BlockSpec((1, w), lambda i: (0, 0)))],
        [(_sds((S, w), _MXU), pl.BlockSpec((ts, w), lambda i: (i, 0)), None)], ("parallel",))[0]


HALO_ROWS = 16


def _shifted(u, halo_ref, is_first):
    rid = lax.broadcasted_iota(jnp.int32, (SUBLANES, 1), 0)
    halo = halo_ref[...].astype(F32)
    hrow = lax.broadcasted_iota(jnp.int32, (HALO_ROWS, 1), 0)

    def halo_row(r):
        return jnp.where(is_first, 0.0, jnp.sum(jnp.where(hrow == r, halo, 0.0), axis=0, keepdims=True))

    h7, h6 = halo_row(HALO_ROWS - 1), halo_row(HALO_ROWS - 2)
    r1, r2 = pltpu.roll(u, 1, 0), pltpu.roll(u, 2, 0)
    top1 = jnp.where(rid == 0, h7, r1[:SUBLANES])
    top2 = jnp.where(rid == 0, h6, jnp.where(rid == 1, h7, r2[:SUBLANES]))
    return jnp.concatenate([top1, r1[SUBLANES:]], axis=0), jnp.concatenate([top2, r2[SUBLANES:]], axis=0)


def _conv(u, u1, u2, cw_ref, cb_ref):
    return ((cb_ref[...] + u2 * cw_ref[0:1, :]) + u1 * cw_ref[1:2, :]) + u * cw_ref[2:3, :]


def _ffn_specs(S, ts, jmap):
    hb = ts // HALO_ROWS
    return (pl.BlockSpec((ts, FF_BLOCK), lambda j, i: (i, jmap(j))),
            pl.BlockSpec((HALO_ROWS, FF_BLOCK), lambda j, i: (jnp.maximum(i * hb - 1, 0), jmap(j))),
            pl.BlockSpec((SUBLANES, FF_BLOCK), lambda j, i: (0, jmap(j))),
            pl.BlockSpec((1, FF_BLOCK), lambda j, i: (0, jmap(j))))


def ffn_act_fwd(u, cw8, cb, name):
    S = u.shape[0]
    ts = _row_tile(S, 2 * D_FF)
    nb = D_FF // FF_BLOCK

    def body(ug, hg, cwg, cbg, uu, hu, cwu, cbu):
        first = pl.program_id(1) == 0
        g = ug[...].astype(F32)
        g1, g2 = _shifted(g, hg, first)
        yg = _conv(g, g1, g2, cwg, cbg)
        v = uu[...].astype(F32)
        v1, v2 = _shifted(v, hu, first)
        yu = _conv(v, v1, v2, cwu, cbu)
        return (yg * jax.nn.sigmoid(yg) * yu,)

    sg = _ffn_specs(S, ts, lambda j: j)
    su = _ffn_specs(S, ts, lambda j: j + nb)
    ins = [(u, sg[0]), (u, sg[1]), (cw8, sg[2]), (cb, sg[3]), (u, su[0]), (u, su[1]), (cw8, su[2]), (cb, su[3])]
    return tilecall(body, name, (nb, S // ts), ins,
                    [(_sds((S, D_FF), _MXU), pl.BlockSpec((ts, FF_BLOCK), lambda j, i: (i, j)), None)],
                    ("parallel", "parallel"))[0]


def ffn_act_bwd(u, cw8, cb, da, name):
    S = u.shape[0]
    ts = _row_tile(S, 2 * D_FF)
    nb = D_FF // FF_BLOCK

    def taps(dy, x, x1, x2):
        return jnp.concatenate(
            [jnp.sum(dy * x2, axis=0, keepdims=True), jnp.sum(dy * x1, axis=0, keepdims=True),
             jnp.sum(dy * x, axis=0, keepdims=True), jnp.zeros((SUBLANES - 3, dy.shape[1]), F32)], axis=0)

    def body(ug, hg, cwg, cbg, uu, hu, cwu, cbu, da_ref):
        first = pl.program_id(1) == 0
        g = ug[...].astype(F32)
        g1, g2 = _shifted(g, hg, first)
        yg = _conv(g, g1, g2, cwg, cbg)
        v = uu[...].astype(F32)
        v1, v2 = _shifted(v, hu, first)
        yu = _conv(v, v1, v2, cwu, cbu)
        d = da_ref[...].astype(F32)
        sg = jax.nn.sigmoid(yg)
        dyg = d * yu * (sg * (1.0 + yg * (1.0 - sg)))
        dyu = d * (yg * sg)
        return (dyg, dyu, taps(dyg, g, g1, g2), taps(dyu, v, v1, v2),
                jnp.sum(dyg, axis=0, keepdims=True), jnp.sum(dyu, axis=0, keepdims=True))

    sg_ = _ffn_specs(S, ts, lambda j: j)
    su_ = _ffn_specs(S, ts, lambda j: j + nb)
    row = pl.BlockSpec((ts, FF_BLOCK), lambda j, i: (i, j))
    ins = [(u, sg_[0]), (u, sg_[1]), (cw8, sg_[2]), (cb, sg_[3]), (u, su_[0]), (u, su_[1]), (cw8, su_[2]), (cb, su_[3]), (da, row)]
    first_row = lambda: pl.program_id(1) == 0
    dy, dcw, dcb = (_sds((S, D_FF)), row, None), (_sds((SUBLANES, D_FF)), sg_[2], first_row), (_sds((1, D_FF)), sg_[3], first_row)
    return tilecall(body, name, (nb, S // ts), ins, [dy, dy, dcw, dcw, dcb, dcb], ("parallel", "arbitrary"))


def ffn_conv_bwd(dyg, dyu, cw8, name):
    S = dyg.shape[0]
    ts = _row_tile(S, 2 * D_FF)
    hb = ts // SUBLANES
    nrow = S // ts
    nb = D_FF // FF_BLOCK

    def back(dy_ref, halo_ref, cw_ref):
        last = pl.program_id(1) == nrow - 1
        d = dy_ref[...]
        rid = lax.broadcasted_iota(jnp.int32, (SUBLANES, 1), 0)
        n0 = jnp.where(last, 0.0, halo_ref[0:1, :])
        n1 = jnp.where(last, 0.0, halo_ref[1:2, :])
        r1, r2 = pltpu.roll(d, ts - 1, 0), pltpu.roll(d, ts - 2, 0)
        end1 = jnp.where(rid == SUBLANES - 1, n0, r1[ts - SUBLANES:])
        end2 = jnp.where(rid == SUBLANES - 1, n1, jnp.where(rid == SUBLANES - 2, n0, r2[ts - SUBLANES:]))
        d1 = jnp.concatenate([r1[:ts - SUBLANES], end1], axis=0)
        d2 = jnp.concatenate([r2[:ts - SUBLANES], end2], axis=0)
        return d * cw_ref[2:3, :] + d1 * cw_ref[1:2, :] + d2 * cw_ref[0:1, :]

    row = pl.BlockSpec((ts, FF_BLOCK), lambda j, i: (i, j))
    halo = pl.BlockSpec((SUBLANES, FF_BLOCK), lambda j, i: (jnp.minimum((i + 1) * hb, S // SUBLANES - 1), j))
    ins = [(dyg, row), (dyg, halo), (cw8, pl.BlockSpec((SUBLANES, FF_BLOCK), lambda j, i: (0, j))),
           (dyu, row), (dyu, halo), (cw8, pl.BlockSpec((SUBLANES, FF_BLOCK), lambda j, i: (0, j + nb)))]
    out = (_sds((S, D_FF), _MXU), row, None)
    return tilecall(lambda a, b, c, d, e, f: (back(a, b, c), back(d, e, f)), name, (nb, nrow), ins, [out, out],
                    ("parallel", "parallel"))


def _hgrn_levels(C):
    out, m = [], C // 2
    while m >= 1:
        out.append(m)
        m //= 2
    return out


def _hgrn_sum_matrix(C):
    t = np.arange(C)[:, None]
    u = np.arange(C)[None, :]
    blocks = [u <= t, u > t]
    for m in _hgrn_levels(C):
        r = (t // (2 * m)) * (2 * m) + m
        right = (t % (2 * m)) >= m
        blocks.append((right & (u > r) & (u <= t)) | ((~right) & (u > t) & (u <= r)))
    return np.concatenate(blocks, axis=0).astype(np.float32)


def _make_partial_sums(nb, C):
    @jax.custom_vjp
    def sums(mall, lf):
        hi = lf.astype(_MXU)
        mid = (lf - hi.astype(F32)).astype(_MXU)
        e2 = _dg(mall, jnp.concatenate([hi, mid], axis=1), _NN)
        e = e2[:, :HEAD_DIM] + e2[:, HEAD_DIM:]
        return tuple(e[b * C:(b + 1) * C] for b in range(nb))

    def fwd(mall, lf):
        return sums(mall, lf), mall

    def bwd(mall, gs):
        return jnp.zeros_like(mall), _dg(mall, jnp.concatenate(gs, axis=0), _TN)

    sums.defvjp(fwd, bwd)
    return sums


def _hgrn_chunk(zq, zf, v, lb, st, mall, C):
    levels = _hgrn_levels(C)
    qs = zq * jax.nn.sigmoid(zq)
    fg = lb + (1.0 - lb) * jax.nn.sigmoid(zf)
    k = 1.0 - fg
    e = _make_partial_sums(2 + len(levels), C)(mall, jnp.log(fg))
    g_incl, g_after = e[0], e[1]
    rid = lax.broadcasted_iota(jnp.int32, (C, 1), 0)
    tt = lax.broadcasted_iota(jnp.int32, (C, C), 0)
    ss = lax.broadcasted_iota(jnp.int32, (C, C), 1)
    o = kdot_nt(qs * jnp.exp(g_incl), st)
    o = o + jnp.sum(qs * k, axis=-1, keepdims=True) * v
    scores = jnp.zeros((C, C), F32)
    for li, m in enumerate(levels):
        sh = int(np.log2(m))
        right = ((rid >> sh) & 1) == 1
        both = jnp.where(right, qs, k) * jnp.exp(e[2 + li])
        pair = ((tt >> (sh + 1)) == (ss >> (sh + 1))) & (((tt >> sh) & 1) == 1) & (((ss >> sh) & 1) == 0)
        scores = scores + jnp.where(pair, kdot_nt(both, both), 0.0)
    o = o + kdot(scores, v)
    g_last = jnp.sum(jnp.where(rid == C - 1, g_incl, 0.0), axis=0, keepdims=True)
    st_new = st * jnp.exp(g_last) + kdot_tn(v, k * jnp.exp(g_after))
    return o, st_new


HGRN_HEADS_PER_STEP = 8
_HGRN_LANES = HGRN_HEADS_PER_STEP * HEAD_DIM


def _hgrn_in_specs(C, nc, rev):
    cm = (lambda c: nc - 1 - c) if rev else (lambda c: c)
    blk = lambda: pl.BlockSpec((C, _HGRN_LANES), lambda h, c: (cm(c), h))
    return cm, [blk(), blk(), blk(), pl.BlockSpec((1, _HGRN_LANES), lambda h, c: (0, h))]


def _hgrn_state_spec(cm):
    return pl.BlockSpec((HGRN_HEADS_PER_STEP, None, HEAD_DIM, HEAD_DIM), lambda h, c: (h, cm(c), 0, 0))


def _hgrn_out(o, g, gain):
    return _rms(o, gain, HEAD_DIM) * (g * jax.nn.sigmoid(g))


def hgrn_fwd(zq, zf, zi, zg, lb, out_gain, mall, name):
    S = zq.shape[0]
    C = min(HGRN_CHUNK, S)
    nc = S // C

    def kern(zq_ref, zf_ref, zi_ref, lb_ref, zg_ref, gain_ref, mall_ref, o_ref, on_ref, st_ref, st):
        @pl.when(pl.program_id(1) == 0)
        def _():
            st[...] = jnp.zeros_like(st)

        mall_v = mall_ref[...]
        for g in range(HGRN_HEADS_PER_STEP):
            lanes = slice(g * HEAD_DIM, (g + 1) * HEAD_DIM)
            s_in = st[g]
            st_ref[g] = s_in
            o, s_new = _hgrn_chunk(zq_ref[:, lanes], zf_ref[:, lanes], zi_ref[:, lanes], lb_ref[:, lanes], s_in, mall_v, C)
            o_ref[:, lanes] = o
            on_ref[:, lanes] = _hgrn_out(o, zg_ref[:, lanes], gain_ref[...]).astype(on_ref.dtype)
            st[g] = s_new

    cm, specs = _hgrn_in_specs(C, nc, False)
    row = pl.BlockSpec((C, _HGRN_LANES), lambda h, c: (c, h))
    return pl.pallas_call(
        kern, name=name, grid=(HEADS // HGRN_HEADS_PER_STEP, nc),
        in_specs=specs + [row, pl.BlockSpec((1, HEAD_DIM), lambda h, c: (0, 0)), pl.BlockSpec(mall.shape, lambda h, c: (0, 0))],
        out_specs=[row, row, _hgrn_state_spec(cm)],
        out_shape=[_sds((S, D_MODEL)), _sds((S, D_MODEL), _MXU), _sds((HEADS, nc, HEAD_DIM, HEAD_DIM))],
        scratch_shapes=[pltpu.VMEM((HGRN_HEADS_PER_STEP, HEAD_DIM, HEAD_DIM), F32)],
        compiler_params=_params(("parallel", "arbitrary")))(zq, zf, zi, lb, zg, out_gain, mall)


def hgrn_bwd(zq, zf, zi, zg, lb, out_gain, mall, states, o, don, name):
    S = zq.shape[0]
    C = min(HGRN_CHUNK, S)
    nc = S // C

    def kern(zq_ref, zf_ref, zi_ref, lb_ref, zg_ref, gain_ref, mall_ref, st_ref, o_ref, don_ref,
             dq_ref, df_ref, di_ref, dg_ref, dlb_ref, dgain_ref, dst):
        first = pl.program_id(1) == 0

        @pl.when(first)
        def _():
            dst[...] = jnp.zeros_like(dst)

        mall_v = mall_ref[...]
        gls, dgain = [], None
        for g in range(HGRN_HEADS_PER_STEP):
            lanes = slice(g * HEAD_DIM, (g + 1) * HEAD_DIM)
            _, out_vjp = jax.vjp(_hgrn_out, o_ref[:, lanes], zg_ref[:, lanes], gain_ref[...])
            do, dzg, dgn = out_vjp(don_ref[:, lanes])
            dg_ref[:, lanes] = dzg.astype(dg_ref.dtype)
            dgain = dgn if dgain is None else dgain + dgn
            _, vjp = jax.vjp(lambda a, b, c, d, e: _hgrn_chunk(a, b, c, d, e, mall_v, C),
                             zq_ref[:, lanes], zf_ref[:, lanes], zi_ref[:, lanes], lb_ref[:, lanes], st_ref[g])
            ga, gb, gv, gl, gs = vjp((do, dst[g]))
            dq_ref[:, lanes] = ga.astype(dq_ref.dtype)
            df_ref[:, lanes] = gb.astype(df_ref.dtype)
            di_ref[:, lanes] = gv.astype(di_ref.dtype)
            gls.append(gl)
            dst[g] = gs
        _store(dlb_ref, jnp.concatenate(gls, axis=1), first)
        _store(dgain_ref, dgain, first & (pl.program_id(0) == 0))

    cm, specs = _hgrn_in_specs(C, nc, True)
    row = lambda: pl.BlockSpec((C, _HGRN_LANES), lambda h, c: (cm(c), h))
    vec = pl.BlockSpec((1, HEAD_DIM), lambda h, c: (0, 0))
    wide = _sds((S, D_MODEL), _MXU)
    return pl.pallas_call(
        kern, name=name, grid=(HEADS // HGRN_HEADS_PER_STEP, nc),
        in_specs=specs + [row(), vec, pl.BlockSpec(mall.shape, lambda h, c: (0, 0)), _hgrn_state_spec(cm), row(), row()],
        out_specs=[row(), row(), row(), row(), pl.BlockSpec((1, _HGRN_LANES), lambda h, c: (0, h)), vec],
        out_shape=[wide, wide, wide, wide, _sds((1, D_MODEL)), _sds((1, HEAD_DIM))],
        scratch_shapes=[pltpu.VMEM((HGRN_HEADS_PER_STEP, HEAD_DIM, HEAD_DIM), F32)],
        compiler_params=_params(("arbitrary", "arbitrary")))(zq, zf, zi, lb, zg, out_gain, mall, states, o, don)


def _lb_soft(p0, p1):
    mx = jnp.maximum(p0, p1)
    e0, e1 = jnp.exp(p0 - mx), jnp.exp(p1 - mx)
    s0, s1 = e0 / (e0 + e1), e1 / (e0 + e1)
    return (s0 + s1) - s0


def lower_bound_fwd(p, name):
    assert p.shape[0] == 2

    def body(p_ref):
        s = _lb_soft(p_ref[0:1, :], p_ref[1:2, :])
        return (jnp.concatenate([jnp.zeros_like(s), s] + [jnp.zeros_like(s)] * (SUBLANES - 2), axis=0),)

    spec8 = pl.BlockSpec((SUBLANES, p.shape[1]), lambda: (0, 0))
    return tilecall(body, name, (), [(p, pl.BlockSpec(p.shape, lambda: (0, 0)))], [(_sds((SUBLANES, p.shape[1])), spec8, None)], ())[0]


def lower_bound_bwd(p, dlb1, name):
    def body(p_ref, d_ref):
        _, vjp = jax.vjp(_lb_soft, p_ref[0:1, :], p_ref[1:2, :])
        g0, g1 = vjp(d_ref[...])
        return (jnp.concatenate([g0, g1] + [jnp.zeros_like(g0)] * (SUBLANES - 2), axis=0),)

    spec8 = pl.BlockSpec((SUBLANES, p.shape[1]), lambda: (0, 0))
    return tilecall(body, name, (), [(p, pl.BlockSpec(p.shape, lambda: (0, 0))), (dlb1, pl.BlockSpec(dlb1.shape, lambda: (0, 0)))],
                    [(_sds((SUBLANES, p.shape[1])), spec8, None)], ())[0]


@jax.custom_vjp
def _swap_rope_halves(x):
    lane = lax.broadcasted_iota(jnp.int32, x.shape, 1)
    lo = (lane >= MLA_NOPE) & (lane < MLA_NOPE + MLA_ROPE // 2)
    hi = (lane >= MLA_NOPE + MLA_ROPE // 2) & (lane < MLA_QK)
    return jnp.where(lo, pltpu.roll(x, MLA_SLOT - MLA_ROPE // 2, 1), jnp.where(hi, pltpu.roll(x, MLA_ROPE // 2, 1), 0.0))


_swap_rope_halves.defvjp(lambda x: (_swap_rope_halves(x), None), lambda _, g: (_swap_rope_halves(g),))


def _norm_rope(x, gain, cos_t, sin_t):
    y = _rms(x, gain, MLA_QK)
    return y * cos_t + _swap_rope_halves(y) * sin_t


_ATTN_SCALE = MLA_QK ** -0.5


def _qk_heads(qs, kn, kr, qn, kn_gain, cos_t, sin_t):
    q = _norm_rope(qs, qn, cos_t, sin_t) * _ATTN_SCALE
    k = _norm_rope(jnp.concatenate([kn, kr], axis=1), kn_gain, cos_t, sin_t)
    return q, k


def _qk_specs(ts):
    slot = pl.BlockSpec((ts, MLA_SLOT), lambda i, h: (i, h))
    nope = pl.BlockSpec((ts, HEAD_DIM), lambda i, h: (i, 2 * h))
    shared = pl.BlockSpec((ts, HEAD_DIM), lambda i, h: (i, 0))
    gain = pl.BlockSpec((1, MLA_SLOT), lambda i, h: (0, 0))
    table = pl.BlockSpec((ts, MLA_SLOT), lambda i, h: (i, 0))
    return slot, nope, shared, gain, table


QK_ROWS = 1024


def qk_fwd(qslots, kv, krope, qn, kn, cos_t, sin_t, name):
    S = qslots.shape[0]
    ts = min(S, QK_ROWS)
    slot, nope, shared, gain, table = _qk_specs(ts)

    def body(q_ref, kn_ref, kr_ref, qn_ref, kg_ref, c_ref, s_ref):
        return _qk_heads(q_ref[...], kn_ref[...], kr_ref[...], qn_ref[...], kg_ref[...], c_ref[...], s_ref[...])

    out = _sds((S, HEADS * MLA_SLOT), _MXU)
    return tilecall(body, name, (S // ts, HEADS),
                    [(qslots, slot), (kv, nope), (krope, shared), (qn, gain), (kn, gain), (cos_t, table), (sin_t, table)],
                    [(out, slot, None), (out, slot, None)], ("parallel", "parallel"))


def qk_bwd(qslots, kv, krope, qn, kn, cos_t, sin_t, dq, dk, dv, name):
    S = qslots.shape[0]
    ts = min(S, QK_ROWS)
    slot, nope, shared, gain, table = _qk_specs(ts)
    vblk = pl.BlockSpec((ts, HEAD_DIM), lambda i, h: (i, h))

    def body(q_ref, kn_ref, kr_ref, qn_ref, kg_ref, c_ref, s_ref, dq_ref, dk_ref, dv_ref):
        c, s = c_ref[...], s_ref[...]
        _, vjp = jax.vjp(lambda a, b, r, g1, g2: _qk_heads(a, b, r, g1, g2, c, s),
                         q_ref[...], kn_ref[...], kr_ref[...], qn_ref[...], kg_ref[...])
        ga, gb, gr, g1, g2 = vjp((dq_ref[...], dk_ref[...]))
        return ga, jnp.concatenate([gb, dv_ref[...]], axis=1), gr, g1, g2

    first_head = lambda: pl.program_id(1) == 0
    first = lambda: (pl.program_id(0) == 0) & (pl.program_id(1) == 0)
    wide = _sds((S, HEADS * MLA_SLOT), _MXU)
    return tilecall(body, name, (S // ts, HEADS),
                    [(qslots, slot), (kv, nope), (krope, shared), (qn, gain), (kn, gain), (cos_t, table), (sin_t, table),
                     (dq, slot), (dk, slot), (dv, vblk)],
                    [(wide, slot, None), (wide, slot, None), (_sds((S, HEAD_DIM)), shared, first_head),
                     (_sds((1, MLA_SLOT)), gain, first), (_sds((1, MLA_SLOT)), gain, first)], ("arbitrary", "arbitrary"))


ATTN_TILE_FWD = 1024
ATTN_TILE = 1024
ATTN_HEADS_PER_STEP = 2
ATTN_HEADS_PER_STEP_BWD = 1


def _causal_pairs(nq, by_row):
    pairs = [(i, j) for i in range(nq) for j in range(i + 1)] if by_row else [(i, j) for j in range(nq) for i in range(j, nq)]
    return jnp.asarray([p[0] for p in pairs], jnp.int32), jnp.asarray([p[1] for p in pairs], jnp.int32)


def _diag_mask(s, tq):
    rows = lax.broadcasted_iota(jnp.int32, (tq, tq), 0)
    cols = lax.broadcasted_iota(jnp.int32, (tq, tq), 1)
    return jnp.where(rows >= cols, s, -jnp.inf)


def attn_fwd(qr, kr, kv, name):
    S = qr.shape[0]
    tq = min(S, ATTN_TILE_FWD)
    nq = S // tq
    i_tab, j_tab = _causal_pairs(nq, True)

    G = ATTN_HEADS_PER_STEP
    reps = tq // LANES

    def kern(it, jt, q_ref, k_ref, kv_ref, o_ref, lse_ref, m_s, l_s, acc):
        n = pl.program_id(1)
        i, j = it[n], jt[n]

        @pl.when(j == 0)
        def _():
            m_s[...] = jnp.full_like(m_s, -jnp.inf)
            l_s[...] = jnp.zeros_like(l_s)
            acc[...] = jnp.zeros_like(acc)

        def step(diagonal):
            for g in range(G):
                slot = slice(g * MLA_SLOT, (g + 1) * MLA_SLOT)
                s = _dg(q_ref[:, slot], k_ref[:, slot], _NT)
                if diagonal:
                    s = _diag_mask(s, tq)
                m_prev = m_s[g]
                m_new = jnp.maximum(m_prev, jnp.max(s, axis=-1, keepdims=True))
                alpha = jnp.exp(m_prev - m_new)
                p = jnp.exp(s - jnp.tile(m_new, (1, reps)))
                l_s[g] = alpha * l_s[g] + jnp.sum(p, axis=-1, keepdims=True)
                acc[g] = alpha * acc[g] + _dg(p, kv_ref[:, g * MLA_SLOT + HEAD_DIM:(g + 1) * MLA_SLOT], _NN)
                m_s[g] = m_new

        @pl.when(j < i)
        def _():
            step(False)

        @pl.when(j == i)
        def _():
            step(True)
            for g in range(G):
                l = l_s[g]
                lanes = slice(g * HEAD_DIM, (g + 1) * HEAD_DIM)
                o_ref[:, lanes] = acc[g] / l
                lse_ref[:, lanes] = m_s[g] + jnp.log(l)

    out = _sds((S, HEADS * HEAD_DIM))
    oblk = pl.BlockSpec((tq, G * HEAD_DIM), lambda h, n, it, jt: (it[n], h))
    stat = pltpu.VMEM((G, tq, HEAD_DIM), F32)
    return pl.pallas_call(
        kern, name=name,
        grid_spec=pltpu.PrefetchScalarGridSpec(
            num_scalar_prefetch=2, grid=(HEADS // G, i_tab.shape[0]),
            in_specs=[pl.BlockSpec((tq, G * MLA_SLOT), lambda h, n, it, jt: (it[n], h)),
                      pl.BlockSpec((tq, G * MLA_SLOT), lambda h, n, it, jt: (jt[n], h)),
                      pl.BlockSpec((tq, G * MLA_SLOT), lambda h, n, it, jt: (jt[n], h))],
            out_specs=[oblk, oblk], scratch_shapes=[stat, stat, stat]),
        out_shape=[out, out], compiler_params=_params(("parallel", "arbitrary")))(i_tab, j_tab, qr, kr, kv)


def attn_bwd(qr, kr, kv, o, lse, do, name):
    S = qr.shape[0]
    tq = min(S, ATTN_TILE)
    nq = S // tq
    i_tab, j_tab = _causal_pairs(nq, False)

    G = ATTN_HEADS_PER_STEP_BWD

    def kern(it, jt, q_ref, k_ref, kv_ref, o_ref, lse_ref, do_ref, dq_ref, dk_ref, dv_ref, dk_acc, dv_acc):
        n = pl.program_id(1)
        i, j = it[n], jt[n]

        @pl.when(n == 0)
        def _():
            dq_ref[...] = jnp.zeros_like(dq_ref)

        @pl.when(i == j)
        def _():
            dk_acc[...] = jnp.zeros_like(dk_acc)
            dv_acc[...] = jnp.zeros_like(dv_acc)

        def step(diagonal):
            rows = pl.ds(pl.multiple_of(i * tq, tq), tq)
            for g in range(G):
                slot = slice(g * MLA_SLOT, (g + 1) * MLA_SLOT)
                lanes = slice(g * HEAD_DIM, (g + 1) * HEAD_DIM)
                q, k = q_ref[:, slot], k_ref[:, slot]
                s = _dg(q, k, _NT) - jnp.tile(lse_ref[:, lanes], (1, tq // LANES))
                if diagonal:
                    s = _diag_mask(s, tq)
                p = jnp.exp(s)
                d = do_ref[:, lanes]
                delta = jnp.sum(d * o_ref[:, lanes], axis=-1, keepdims=True)
                dv_acc[:, lanes] += _dg(p, d, _TN)
                ds = p * (_dg(d, kv_ref[:, g * MLA_SLOT + HEAD_DIM:(g + 1) * MLA_SLOT], _NT) - delta)
                dk_acc[:, slot] += _dg(ds, q, _TN)
                dq_ref[rows, slot] += _dg(ds, k, _NN)

        @pl.when(i > j)
        def _():
            step(False)

        @pl.when(i == j)
        def _():
            step(True)

        @pl.when(i == nq - 1)
        def _():
            dk_ref[...] = dk_acc[...]
            dv_ref[...] = dv_acc[...]

    qblk = pl.BlockSpec((tq, G * MLA_SLOT), lambda h, n, it, jt: (it[n], h))
    oblk = pl.BlockSpec((tq, G * HEAD_DIM), lambda h, n, it, jt: (it[n], h))
    kblk = pl.BlockSpec((tq, G * MLA_SLOT), lambda h, n, it, jt: (jt[n], h))
    return pl.pallas_call(
        kern, name=name,
        grid_spec=pltpu.PrefetchScalarGridSpec(
            num_scalar_prefetch=2, grid=(HEADS // G, i_tab.shape[0]),
            in_specs=[qblk, kblk, kblk, oblk, oblk, oblk],
            out_specs=[pl.BlockSpec((S, G * MLA_SLOT), lambda h, n, it, jt: (0, h)), kblk,
                       pl.BlockSpec((tq, G * HEAD_DIM), lambda h, n, it, jt: (jt[n], h))],
            scratch_shapes=[pltpu.VMEM((tq, G * MLA_SLOT), F32), pltpu.VMEM((tq, G * HEAD_DIM), F32)]),
        out_shape=[_sds((S, HEADS * MLA_SLOT)), _sds((S, HEADS * MLA_SLOT)), _sds((S, HEADS * HEAD_DIM))],
        compiler_params=_params(("parallel", "arbitrary")))(i_tab, j_tab, qr, kr, kv, o, lse, do)


def loss_head(y, target, name):
    S, Dm = y.shape
    ts = _row_tile(S, Dm)

    def body(y_ref, t_ref):
        e = y_ref[...] - t_ref[...]
        tot = jnp.sum(jnp.sum(e * e, axis=-1, keepdims=True) / Dm, axis=0, keepdims=True)
        return e / Dm, jnp.broadcast_to(0.5 * tot, (SUBLANES, LANES))

    row = pl.BlockSpec((ts, Dm), lambda i: (i, 0))
    return tilecall(body, name, (S // ts,), [(y, row), (target, row)],
                    [(_sds((S, Dm)), row, None),
                     (_sds((SUBLANES, LANES)), pl.BlockSpec((SUBLANES, LANES), lambda i: (0, 0)), lambda: pl.program_id(0) == 0)],
                    ("arbitrary",))


_PEER_FLIPS = {
    "chips": ((1, 0, 0), (0, 1, 0), (1, 1, 0)),
    "sibling": ((0, 0, 1),),
    "all": tuple((a, b, c) for a in (0, 1) for b in (0, 1) for c in (0, 1))[1:],
}
_SLOT_WEIGHTS = {"chips": (2, 1, 0), "sibling": (0, 0, 1), "all": (4, 2, 1)}
_HBM = pl.BlockSpec(memory_space=pltpu.HBM)


def _me():
    return lax.axis_index("x"), lax.axis_index("y"), lax.axis_index("c")


def _remote(src, dst, send_sem, recv_sem, peer):
    return pltpu.make_async_remote_copy(src_ref=src, dst_ref=dst, send_sem=send_sem, recv_sem=recv_sem,
                                        device_id=peer, device_id_type=pl.DeviceIdType.MESH)


def exchange(arrs, group, slab_weights, name, keep_own=True):
    flips = _PEER_FLIPS[group]
    wx, wy, wc = _SLOT_WEIGHTS[group]
    n_slots = len(flips) + (1 if keep_own else 0)
    n = len(arrs)

    def slab(ref, a, pos):
        w = slab_weights[a]
        return ref if w is None else ref.at[w[0] * pos[0] + w[1] * pos[1] + w[2] * pos[2]]

    def kern(*refs):
        srcs, outs = refs[:n], refs[n:2 * n]
        send_sems, recv_sems = refs[2 * n:2 * n + 2]
        me = _me()
        my_slot = wx * me[0] + wy * me[1] + wc * me[2]
        copies = []
        if keep_own:
            local_sems = refs[2 * n + 2]
            for a in range(n):
                cp = pltpu.make_async_copy(slab(srcs[a], a, me), outs[a].at[my_slot], local_sems.at[a])
                cp.start()
                copies.append(cp)
        for f, flip in enumerate(flips):
            peer = tuple(m ^ b if b else m for m, b in zip(me, flip))
            for a in range(n):
                cp = _remote(slab(srcs[a], a, peer), outs[a].at[my_slot if keep_own else f],
                             send_sems.at[f, a], recv_sems.at[f, a], peer)
                cp.start()
                copies.append(cp)
        for cp in copies:
            cp.wait()

    out_shape = [_sds((n_slots,) + (a.shape if slab_weights[k] is None else a.shape[1:]), a.dtype) for k, a in enumerate(arrs)]
    sems = [pltpu.SemaphoreType.DMA((len(flips), n)), pltpu.SemaphoreType.DMA((len(flips), n))]
    return pl.pallas_call(
        kern, name=name, in_specs=[_HBM] * n, out_specs=[_HBM] * n, out_shape=out_shape,
        scratch_shapes=sems + ([pltpu.SemaphoreType.DMA((n,))] if keep_own else []))(*arrs)


def _chip_window(ref, kind, size, chip, layers):
    if kind == "rows":
        return ref.at[layers, pl.ds(chip * size, size), :]
    return ref.at[layers, :, pl.ds(pl.multiple_of(chip * size, LANES), size)]


def gather_big(shards, kinds, name):
    n = len(shards)
    fulls = []
    for s, kind in zip(shards, kinds):
        L, r, c = s.shape
        fulls.append(_sds((L, N_CHIPS * r, c) if kind == "rows" else (L, r, N_CHIPS * c), s.dtype))

    def kern(*refs):
        srcs, outs = refs[:n], refs[n:2 * n]
        ici_s, ici_r, relay_s, relay_r, d2d_s, d2d_r = refs[2 * n:]
        x, y, c = _me()
        sibling = (x, y, 1 - c)
        nbrs = ((x ^ 1, y), (x, y ^ 1))
        relay_from = (x ^ (1 - c), y ^ c)
        relay_to = (x ^ c, y ^ (1 - c), c)
        diagonal = (x ^ 1, y ^ 1)

        def window(a, chip):
            L, r, cc = shards[a].shape
            size = r if kinds[a] == "rows" else cc
            return _chip_window(outs[a], kinds[a], size, 2 * chip[0] + chip[1], pl.ds(c * (L // 2), L // 2))

        def direct(a, f):
            L = shards[a].shape[0]
            return _remote(srcs[a].at[pl.ds(c * (L // 2), L // 2)], window(a, (x, y)), ici_s.at[a, f], ici_r.at[a, f],
                           (nbrs[f][0], nbrs[f][1], c))

        def to_sibling(a, k, chip):
            return _remote(window(a, chip), window(a, chip), d2d_s.at[a, k], d2d_r.at[a, k], sibling)

        sends, relays, passed = [], [], []
        for a in range(n):
            for f in range(2):
                cp = direct(a, f)
                cp.start()
                sends.append(cp)
        for a in range(n):
            _remote(window(a, relay_from), window(a, relay_from), ici_s.at[a, c], ici_r.at[a, c], relay_to).wait_recv()
            cp = _remote(window(a, relay_from), window(a, relay_from), relay_s.at[a], relay_r.at[a], relay_to)
            cp.start()
            relays.append(cp)
            passed.append(to_sibling(a, 0, relay_from))
            passed[-1].start()
            _remote(window(a, relay_from), window(a, relay_from), ici_s.at[a, 1 - c], ici_r.at[a, 1 - c], relay_to).wait_recv()
            passed.append(to_sibling(a, 1, (relay_to[0], relay_to[1])))
            passed[-1].start()
        for a in range(n):
            relays[a].wait_recv()
            passed.append(to_sibling(a, 2, diagonal))
            passed[-1].start()
        for cp in sends + relays:
            cp.wait_send()
        for cp in passed:
            cp.wait()

    pair = pltpu.SemaphoreType.DMA((n, 2))
    one = pltpu.SemaphoreType.DMA((n,))
    three = pltpu.SemaphoreType.DMA((n, 3))
    return pl.pallas_call(kern, name=name, in_specs=[_HBM] * n, out_specs=[_HBM] * n, out_shape=fulls,
                          scratch_shapes=[pair, pair, one, one, three, three])(*shards)


def send_other_half(arrs, name):
    n = len(arrs)

    def kern(*refs):
        srcs, outs = refs[:n], refs[n:2 * n]
        send_sems, recv_sems = refs[2 * n:]
        x, y, c = _me()
        copies = []
        for a in range(n):
            hl = arrs[a].shape[0] // 2
            cp = _remote(srcs[a].at[pl.ds((1 - c) * hl, hl)], outs[a], send_sems.at[a], recv_sems.at[a], (x, y, 1 - c))
            cp.start()
            copies.append(cp)
        for cp in copies:
            cp.wait()

    return pl.pallas_call(
        kern, name=name, in_specs=[_HBM] * n, out_specs=[_HBM] * n,
        out_shape=[_sds((a.shape[0] // 2,) + a.shape[1:], a.dtype) for a in arrs],
        scratch_shapes=[pltpu.SemaphoreType.DMA((n,)), pltpu.SemaphoreType.DMA((n,))])(*arrs)


def _axis_neighbours():
    x, y, c = _me()
    return (x, y, c), (x ^ (1 - c), y ^ c), (x ^ c, y ^ (1 - c)), (x ^ 1, y ^ 1)


def reduce_first_axis(arrs, kinds, name):
    n = len(arrs)
    shapes = []
    for a, kind in zip(arrs, kinds):
        l, R, C = a.shape
        shapes.append((l, R // N_CHIPS, C) if kind == "rows" else (l, R, C // N_CHIPS))

    def kern(*refs):
        srcs, outs = refs[:n], refs[n:2 * n]
        send_sems, recv_sems = refs[2 * n:]
        (x, y, c), first, second, diagonal = _axis_neighbours()
        copies = []
        for a in range(n):
            size = shapes[a][1] if kinds[a] == "rows" else shapes[a][2]
            for k, chip in enumerate((first, diagonal)):
                window = _chip_window(srcs[a], kinds[a], size, 2 * chip[0] + chip[1], slice(None))
                cp = _remote(window, outs[a].at[k], send_sems.at[a, k], recv_sems.at[a, k], (first[0], first[1], c))
                cp.start()
                copies.append(cp)
        for cp in copies:
            cp.wait()

    sem = pltpu.SemaphoreType.DMA((n, 2))
    return pl.pallas_call(
        kern, name=name, in_specs=[_HBM] * n, out_specs=[_HBM] * n,
        out_shape=[_sds((2,) + s, a.dtype) for s, a in zip(shapes, arrs)], scratch_shapes=[sem, sem])(*arrs)


def reduce_second_axis(arrs, name):
    n = len(arrs)

    def kern(*refs):
        srcs, outs = refs[:n], refs[n:2 * n]
        send_sems, recv_sems = refs[2 * n:]
        (x, y, c), first, second, diagonal = _axis_neighbours()
        copies = []
        for a in range(n):
            cp = _remote(srcs[a], outs[a], send_sems.at[a], recv_sems.at[a], (second[0], second[1], c))
            cp.start()
            copies.append(cp)
        for cp in copies:
            cp.wait()

    return pl.pallas_call(
        kern, name=name, in_specs=[_HBM] * n, out_specs=[_HBM] * n, out_shape=[_sds(a.shape, a.dtype) for a in arrs],
        scratch_shapes=[pltpu.SemaphoreType.DMA((n,)), pltpu.SemaphoreType.DMA((n,))])(*arrs)


def _stack_tile(r, c):
    for t in (1024, 704, 512, 352, 256, 128, 64, 32, 16):
        if r % t == 0 and t * c * 4 <= 3 * 512 * 1024:
            return t
    return r


def _window_map(kind, r, tr):
    nrt = r // tr
    if kind == "rows":
        return lambda l, i, chip: (l, chip[0] * nrt + i, 0)
    return lambda l, i, chip: (l, i, chip[0])


def place(full, shard, kind, chip, name):
    L, r, c = shard.shape
    tr = _stack_tile(r, c)
    wmap = _window_map(kind, r, tr)
    return tilecall(lambda chip_ref, s_ref, f_ref: (s_ref[...],), name, (L, r // tr),
                    [(shard, pl.BlockSpec((None, tr, c), lambda l, i, chip: (l, i, 0))), (full, pl.BlockSpec(memory_space=pl.ANY))],
                    [(_sds(full.shape, full.dtype), pl.BlockSpec((None, tr, c), lambda l, i, chip: wmap(l, i, chip)), None)],
                    ("parallel", "parallel"), prefetch=(chip,), aliases={1: 0})[0]


def add_cores(g, other, core, name):
    L, R, C = g.shape
    hl = L // 2
    tr = _stack_tile(R, C)
    blk = (None, tr, C)
    return tilecall(lambda core_ref, a_ref, b_ref: (a_ref[...] + b_ref[...],), name, (hl, R // tr),
                    [(g, pl.BlockSpec(blk, lambda l, i, core: (core[0] * hl + l, i, 0))),
                     (other, pl.BlockSpec(blk, lambda l, i, core: (l, i, 0)))],
                    [(_sds((hl, R, C), _MXU), pl.BlockSpec(blk, lambda l, i, core: (l, i, 0)), None)],
                    ("parallel", "parallel"), prefetch=(core,))[0]


def add_first_axis(own, got, kind, chips, name):
    _, l, r, c = got.shape
    tr = _stack_tile(r, c)
    nrt = r // tr

    def window(k):
        if kind == "rows":
            return pl.BlockSpec((None, tr, c), lambda ll, i, *ch: (ll, ch[k][0] * nrt + i, 0))
        return pl.BlockSpec((None, tr, c), lambda ll, i, *ch: (ll, i, ch[k][0]))

    def body(chip0_ref, chip1_ref, own0, own1, got0, got1):
        return own0[...].astype(F32) + got0[...].astype(F32), own1[...].astype(F32) + got1[...].astype(F32)

    plain = pl.BlockSpec((None, tr, c), lambda ll, i, *ch: (ll, i, 0))
    return tilecall(body, name, (l, r // tr),
                    [(own, window(0)), (own, window(1))] +
                    [(got, pl.BlockSpec((None, None, tr, c), lambda ll, i, *ch, k=k: (k, ll, i, 0))) for k in range(2)],
                    [(_sds((l, r, c)), plain, None), (_sds((l, r, c), _MXU), plain, None)],
                    ("parallel", "parallel"), prefetch=tuple(chips))


def add_second_axis(mine, got, name):
    l, r, c = mine.shape
    tr = _stack_tile(r, c)
    blk = pl.BlockSpec((None, tr, c), lambda ll, i: (ll, i, 0))
    return tilecall(lambda a_ref, b_ref: (a_ref[...] + b_ref[...].astype(F32),), name, (l, r // tr),
                    [(mine, blk), (got, blk)], [(_sds((l, r, c)), blk, None)], ("parallel", "parallel"))[0]


def _adam_update(g, w, m, v):
    m_new = ADAM_B1 * m + (1.0 - ADAM_B1) * g
    v_new = ADAM_B2 * v + (1.0 - ADAM_B2) * jnp.square(g)
    m_hat = m_new / (1.0 - ADAM_B1 ** ADAM_STEP)
    v_hat = v_new / (1.0 - ADAM_B2 ** ADAM_STEP)
    delta = -ADAM_LR * (m_hat / (jnp.sqrt(v_hat) + ADAM_EPS) + ADAM_WD * w)
    return g, delta, m_new, v_new


def adamw_stacked(mine, theirs, w, m, v, core, name):
    L, r, c = w.shape
    hl = L // 2
    tr = _stack_tile(r, c)

    def body(core_ref, a_ref, b_ref, w_ref, m_ref, v_ref):
        is_mine = (pl.program_id(0) // hl) == core_ref[0]
        g = jnp.where(is_mine, a_ref[...], b_ref[...])
        return _adam_update(g, w_ref[...], m_ref[...], v_ref[...])

    full = pl.BlockSpec((None, tr, c), lambda l, i, core: (l, i, 0))
    out = (_sds((L, r, c)), full, None)
    return tilecall(body, name, (L, r // tr),
                    [(mine, pl.BlockSpec((None, tr, c), lambda l, i, core: (l % hl, i, 0))),
                     (theirs, pl.BlockSpec((None, None, tr, c), lambda l, i, core: (0, l % hl, i, 0))),
                     (w, full), (m, full), (v, full)],
                    [out, out, out, out], ("parallel", "parallel"), prefetch=(core,))


def _pack(arrs, dtype, row_multiple):
    flat = jnp.concatenate([a.reshape(-1).astype(dtype) for a in arrs])
    rows = -(-flat.shape[0] // LANES)
    rows = -(-rows // row_multiple) * row_multiple
    return jnp.pad(flat, (0, rows * LANES - flat.shape[0])).reshape(rows, LANES)


def _unpack(buf, shapes):
    flat = buf.reshape(-1)
    out, off = [], 0
    for s in shapes:
        n = int(np.prod(s))
        out.append(flat[off:off + n].reshape(s))
        off += n
    return out


def adamw_packed(gparts, w, m, v, name):
    P, R, _ = gparts.shape

    def body(g_ref, w_ref, m_ref, v_ref):
        g = g_ref[0]
        for p in range(1, P):
            g = g + g_ref[p]
        return _adam_update(g, w_ref[...], m_ref[...], v_ref[...])

    whole = pl.BlockSpec((R, LANES), lambda: (0, 0))
    out = (_sds((R, LANES)), whole, None)
    return tilecall(body, name, (), [(gparts, pl.BlockSpec((P, R, LANES), lambda: (0, 0, 0))), (w, whole), (m, whole), (v, whole)],
                    [out, out, out, out], ())


def _residual_out(a, w, x, next_gain):
    if next_gain is None:
        return mm(a, w, "nn", "mm_nn_add", add=x), None
    return mm(a, w, "nn", "mm_nn_add_rms", add=x, rms_gain=next_gain)


def _input_grad(pieces, x, gain, res):
    dh = None
    for d, w in pieces[:-1]:
        dh = mm(d, w, "nt", "mm_nt" if dh is None else "mm_nt_add", add=dh)
    d, w = pieces[-1]
    return mm(d, w, "nt", "mm_nt_norm_bwd", add=dh, norm_bwd=(x, gain, res))


def _ffn_fwd(x, h, next_gain, layer, gain, wts, cw8, cb):
    u = mm(h, win(wts["ffn_w_up"], layer), "nn", "mm_nn", out_dtype=_MXU)
    a = ffn_act_fwd(u, cw8, cb, "ffn_act_fwd")
    y, h_next = _residual_out(a, win(wts["ffn_w_down"], layer), x, next_gain)
    return y, h_next, (x, h, u, a)


def _ffn_bwd(dy, saved, layer, gain, wts, cw8, cb, grads):
    x, h, u, a = saved
    grads["ffn_w_down"] = mm(a, dy, "tn", "mm_tn_into", into=win(grads["ffn_w_down"], layer))
    da = mm(dy, win(wts["ffn_w_down"], layer), "nt", "mm_nt", out_dtype=_MXU)
    dyg, dyu, dcw_g, dcw_u, dcb_g, dcb_u = ffn_act_bwd(u, cw8, cb, da, "ffn_act_bwd")
    pieces = []
    for half, du in enumerate(ffn_conv_bwd(dyg, dyu, cw8, "ffn_conv_bwd")):
        cols = dict(col_off=half * D_FF, cols=D_FF)
        grads["ffn_w_up"] = mm(h, du, "tn", "mm_tn_into", into=win(grads["ffn_w_up"], layer, **cols))
        pieces.append((du, win(wts["ffn_w_up"], layer, **cols)))
    dx, d_gain = _input_grad(pieces, x, gain, dy)
    return dx, dict(gain=d_gain, conv_w=jnp.concatenate([dcw_g[0:3], dcw_u[0:3]], axis=1),
                    conv_b=jnp.concatenate([dcb_g, dcb_u], axis=1))


def _hgrn_w_in(wts, j, k):
    return win(wts["hgrn_w_in"], j, row_off=k * D_MODEL, rows=D_MODEL)


def _hgrn_layer_fwd(x, h, next_gain, j, gain, wts, lb, out_gain, mall):
    z = [mm(h, _hgrn_w_in(wts, j, k), "nn", "mm_nn") for k in range(4)]
    o, on, states = hgrn_fwd(z[0], z[1], z[2], z[3], lb, out_gain, mall, "hgrn_fwd")
    y, h_next = _residual_out(on, win(wts["hgrn_w_out"], j), x, next_gain)
    return y, h_next, (x, h, z, o, states, on)


def _hgrn_layer_bwd(dy, saved, j, gain, wts, lb, out_gain, mall, grads):
    x, h, z, o, states, on = saved
    grads["hgrn_w_out"] = mm(on, dy, "tn", "mm_tn_into", into=win(grads["hgrn_w_out"], j))
    don = mm(dy, win(wts["hgrn_w_out"], j), "nt", "mm_nt")
    dzq, dzf, dzi, dzg, dlb, d_out_gain = hgrn_bwd(z[0], z[1], z[2], z[3], lb, out_gain, mall, states, o, don, "hgrn_bwd")
    dz = [dzq, dzf, dzi, dzg]
    for k, d in enumerate(dz):
        grads["hgrn_w_in"] = mm(h, d, "tn", "mm_tn_into", into=win(grads["hgrn_w_in"], j, row_off=k * D_MODEL, rows=D_MODEL))
    dx, d_gain = _input_grad([(d, _hgrn_w_in(wts, j, k)) for k, d in enumerate(dz)], x, gain, dy)
    return dx, dict(gain=d_gain, lb=dlb, out_gain=d_out_gain)


_MLA_IN_WINDOWS = ((0, MLA_LORA), (MLA_LORA, MLA_LORA), (2 * MLA_LORA, HEAD_DIM))


def _mla_layer_fwd(x, h, next_gain, j, gain, wts, qa_gain, kva_gain, qn, kn, cos_t, sin_t):
    (c0, n), (c1, n1), (c2, n2) = _MLA_IN_WINDOWS
    cq, cqn = mm(h, win(wts["mla_w_in"], j, col_off=c0, cols=n), "nn", "mm_nn_rms", rms_gain=qa_gain)
    ckv, ckvn = mm(h, win(wts["mla_w_in"], j, col_off=c1, cols=n1), "nn", "mm_nn_rms", rms_gain=kva_gain)
    kr = mm(h, win(wts["mla_w_in"], j, col_off=c2, cols=n2), "nn", "mm_nn")
    qslots = mm(cqn, win(wts["mla_w_q_up"], j), "nn", "mm_nn")
    kv = mm(ckvn, win(wts["mla_w_kv_up"], j), "nn", "mm_nn")
    qr, krot = qk_fwd(qslots, kv, kr, qn, kn, cos_t, sin_t, "qk_fwd")
    o, lse = attn_fwd(qr, krot, kv, "attn_fwd")
    y, h_next = _residual_out(o, win(wts["mla_w_out"], j), x, next_gain)
    return y, h_next, (x, h, cq, ckv, kr, cqn, ckvn, qslots, kv, qr, krot, o, lse)


def _mla_layer_bwd(dy, saved, j, gain, wts, qa_gain, kva_gain, qn, kn, cos_t, sin_t, grads):
    x, h, cq, ckv, kr, cqn, ckvn, qslots, kv, qr, krot, o, lse = saved
    grads["mla_w_out"] = mm(o, dy, "tn", "mm_tn_into", into=win(grads["mla_w_out"], j))
    do = mm(dy, win(wts["mla_w_out"], j), "nt", "mm_nt")
    dq, dk, dv = attn_bwd(qr, krot, kv, o, lse, do, "attn_bwd")
    dqslots, dkv, dkr, d_qn, d_kn = qk_bwd(qslots, kv, kr, qn, kn, cos_t, sin_t, dq, dk, dv, "qk_bwd")
    grads["mla_w_q_up"] = mm(cqn, dqslots, "tn", "mm_tn_into", into=win(grads["mla_w_q_up"], j))
    dcq, d_qa = _input_grad([(dqslots, win(wts["mla_w_q_up"], j))], cq, qa_gain, None)
    grads["mla_w_kv_up"] = mm(ckvn, dkv, "tn", "mm_tn_into", into=win(grads["mla_w_kv_up"], j))
    dckv, d_kva = _input_grad([(dkv, win(wts["mla_w_kv_up"], j))], ckv, kva_gain, None)
    pieces = []
    for d, (c0, n) in zip((dcq, dckv, dkr), _MLA_IN_WINDOWS):
        grads["mla_w_in"] = mm(h, d, "tn", "mm_tn_into", into=win(grads["mla_w_in"], j, col_off=c0, cols=n))
        pieces.append((d, win(wts["mla_w_in"], j, col_off=c0, cols=n)))
    dx, d_gain = _input_grad(pieces, x, gain, dy)
    return dx, dict(gain=d_gain, qa=d_qa, kva=d_kva, qn=d_qn, kn=d_kn)


BIG = (("hgrn_w_in", "rows"), ("hgrn_w_out", "rows"), ("mla_w_in", "rows"), ("mla_w_q_up", "cols"),
       ("mla_w_kv_up", "cols"), ("mla_w_out", "rows"), ("ffn_w_up", "cols"), ("ffn_w_down", "rows"))
SMALL_SHARDED = (("mla_q_a_norm", 1), ("mla_kv_a_norm", 1), ("ffn_conv_w", 2))
REPLICATED = ("norm_mix", "norm_ffn", "hgrn_lower_bounds", "hgrn_out_norm", "mla_q_norm", "mla_k_norm", "ffn_conv_b")
WEIGHTS = ("norm_mix", "norm_ffn", "hgrn_w_in", "hgrn_lower_bounds", "hgrn_out_norm", "hgrn_w_out", "mla_w_in",
           "mla_q_a_norm", "mla_w_q_up", "mla_kv_a_norm", "mla_w_kv_up", "mla_q_norm", "mla_k_norm", "mla_w_out",
           "ffn_w_up", "ffn_conv_w", "ffn_conv_b", "ffn_w_down")


def _pad_cols(a, width):
    return jnp.pad(a, [(0, 0)] * (a.ndim - 1) + [(0, width - a.shape[-1])])


def _head_slots(w):
    lead, n = w.shape[:-1], w.shape[-1] // MLA_QK
    return _pad_cols(w.reshape(lead + (n, MLA_QK)), MLA_SLOT).reshape(lead + (n * MLA_SLOT,))


def _head_unslots(w):
    lead, n = w.shape[:-1], w.shape[-1] // MLA_SLOT
    return w.reshape(lead + (n, MLA_SLOT))[..., :MLA_QK].reshape(lead + (n * MLA_QK,))


def _to_stack_layout(name, a):
    if name == "hgrn_w_in":
        return a
    if name == "mla_w_in":
        return _pad_cols(a, MLA_IN_COLS)
    if name == "mla_w_q_up":
        return _head_slots(a)
    return a


def _from_stack_layout(name, a):
    if name == "mla_w_in":
        return a[..., :2 * MLA_LORA + MLA_ROPE]
    if name == "mla_w_q_up":
        return _head_unslots(a)
    return a


def _rope_tables(positions):
    inv_freq = ROPE_THETA ** (-jnp.arange(0, MLA_ROPE, 2, dtype=F32) / MLA_ROPE)
    ang = positions.astype(F32)[:, None] * inv_freq
    cos, sin = jnp.cos(ang), jnp.sin(ang)
    S = positions.shape[0]
    ones, zeros = jnp.ones((S, MLA_NOPE), F32), jnp.zeros((S, MLA_SLOT - MLA_QK), F32)
    return (jnp.concatenate([ones, cos, cos, zeros], axis=1),
            jnp.concatenate([jnp.zeros((S, MLA_NOPE), F32), -sin, sin, zeros], axis=1))


def kernel(x, positions, norm_mix, norm_ffn, hgrn_w_in, hgrn_lower_bounds, hgrn_out_norm, hgrn_w_out, mla_w_in, mla_q_a_norm, mla_w_q_up, mla_kv_a_norm, mla_w_kv_up, mla_q_norm, mla_k_norm, mla_w_out, ffn_w_up, ffn_conv_w, ffn_conv_b, ffn_w_down, loss_target, m_norm_mix, m_norm_ffn, m_hgrn_w_in, m_hgrn_lower_bounds, m_hgrn_out_norm, m_hgrn_w_out, m_mla_w_in, m_mla_q_a_norm, m_mla_w_q_up, m_mla_kv_a_norm, m_mla_w_kv_up, m_mla_q_norm, m_mla_k_norm, m_mla_w_out, m_ffn_w_up, m_ffn_conv_w, m_ffn_conv_b, m_ffn_w_down, v_norm_mix, v_norm_ffn, v_hgrn_w_in, v_hgrn_lower_bounds, v_hgrn_out_norm, v_hgrn_w_out, v_mla_w_in, v_mla_q_a_norm, v_mla_w_q_up, v_mla_kv_a_norm, v_mla_w_kv_up, v_mla_q_norm, v_mla_k_norm, v_mla_w_out, v_ffn_w_up, v_ffn_conv_w, v_ffn_conv_b, v_ffn_w_down):
    args = dict(locals())
    w = {n: args[n] for n in WEIGHTS}
    m = {n: args["m_" + n] for n in WEIGHTS}
    v = {n: args["v_" + n] for n in WEIGHTS}
    depth = norm_mix.shape[0]
    x0 = x[0]
    S = x0.shape[0]
    chip = (2 * lax.axis_index("x") + lax.axis_index("y")).astype(jnp.int32).reshape(1)
    core = lax.axis_index("c").astype(jnp.int32).reshape(1)
    big_names = [n for n, _ in BIG]
    kinds = [k for _, k in BIG]
    small_names = [n for n, _ in SMALL_SHARDED]
    small_axis = dict(SMALL_SHARDED)

    local = {n: _to_stack_layout(n, w[n]) for n in big_names}
    gathered = gather_big([local[n].astype(_MXU) for n in big_names], kinds, "gather_weights")
    wts = {n: place(g, local[n], k, chip, "place") for n, k, g in zip(big_names, kinds, gathered)}
    (got_small,) = exchange([_pack([w[n] for n in small_names], F32, SUBLANES)], "chips", [None], "gather_small")
    per_chip = [_unpack(got_small[p], [w[n].shape for n in small_names]) for p in range(N_CHIPS)]
    small = {n: jnp.concatenate([per_chip[p][k] for p in range(N_CHIPS)], axis=small_axis[n]) for k, n in enumerate(small_names)}

    cos_t, sin_t = _rope_tables(positions[0])
    lbs = lower_bound_fwd(hgrn_lower_bounds, "lower_bound_fwd")
    mall = jnp.asarray(_hgrn_sum_matrix(min(HGRN_CHUNK, S)), _MXU)
    qn = _pad_cols(mla_q_norm, MLA_SLOT)
    kn = _pad_cols(mla_k_norm, MLA_SLOT)
    cw8 = jnp.pad(small["ffn_conv_w"], ((0, 0), (0, SUBLANES - 3), (0, 0)))

    def mixer_args(layer):
        j = layer // 2
        if layer % 2 == 0:
            return (j, norm_mix[layer:layer + 1], wts, lbs[j:j + 1], hgrn_out_norm[j:j + 1], mall)
        return (j, norm_mix[layer:layer + 1], wts, small["mla_q_a_norm"][j:j + 1], small["mla_kv_a_norm"][j:j + 1],
                qn[j:j + 1], kn[j:j + 1], cos_t, sin_t)

    def ffn_args(layer):
        return (layer, norm_ffn[layer:layer + 1], wts, cw8[layer], ffn_conv_b[layer:layer + 1])

    xc, h = x0, rms_fwd(x0, norm_mix[0:1], "rms_fwd")
    saved = []
    for layer in range(depth):
        fwd = _hgrn_layer_fwd if layer % 2 == 0 else _mla_layer_fwd
        xc, h, s_mix = fwd(xc, h, norm_ffn[layer:layer + 1], *mixer_args(layer))
        xc, h, s_ffn = _ffn_fwd(xc, h, norm_mix[layer + 1:layer + 2] if layer + 1 < depth else None, *ffn_args(layer))
        saved.append((s_mix, s_ffn))

    dh, loss_blk = loss_head(xc, loss_target[0], "loss_head")
    loss = lax.psum(loss_blk[0, 0], MESH_AXES)

    grads = {n: lax.empty(g.shape, F32) for n, g in zip(big_names, gathered)}
    g_mix, g_ffn = [None] * depth, [None] * depth
    for layer in reversed(range(depth)):
        s_mix, s_ffn = saved[layer]
        dh, g_ffn[layer] = _ffn_bwd(dh, s_ffn, *ffn_args(layer), grads)
        bwd = _hgrn_layer_bwd if layer % 2 == 0 else _mla_layer_bwd
        dh, g_mix[layer] = bwd(dh, s_mix, *mixer_args(layer), grads)
    hg = [g_mix[l] for l in range(0, depth, 2)]
    mg = [g_mix[l] for l in range(1, depth, 2)]
    d_p = lower_bound_bwd(hgrn_lower_bounds, hg[1]["lb"], "lower_bound_bwd")

    g_list = [grads[n] for n in big_names]
    from_core = send_other_half(g_list, "reduce_cores_in")
    chip_sums = [add_cores(g, o, core, "add_cores") for g, o in zip(g_list, from_core)]
    cx, cy, cc = lax.axis_index("x"), lax.axis_index("y"), lax.axis_index("c")
    second = 2 * (cx + cc - 2 * cx * cc) + (cy + (1 - cc) - 2 * cy * (1 - cc))
    chips = (chip, second.astype(jnp.int32).reshape(1))
    from_first = reduce_first_axis(chip_sums, kinds, "reduce_axis_1")
    partial = [add_first_axis(own, got, k, chips, "add_axis_1") for own, got, k in zip(chip_sums, from_first, kinds)]
    from_second = reduce_second_axis([p[1] for p in partial], "reduce_axis_2")
    reduced = [add_second_axis(p[0], got, "add_axis_2") for p, got in zip(partial, from_second)]
    other_half = exchange(reduced, "sibling", [None] * len(reduced), "reduce_cores_out", keep_own=False)
    big_out = {}
    for n, mine, theirs in zip(big_names, reduced, other_half):
        outs = adamw_stacked(mine, theirs, local[n], _to_stack_layout(n, m[n]), _to_stack_layout(n, v[n]), core, "adamw")
        big_out[n] = [_from_stack_layout(n, o) for o in outs]

    small_grads = {
        "norm_mix": jnp.concatenate([g["gain"] for g in g_mix], axis=0),
        "norm_ffn": jnp.concatenate([g["gain"] for g in g_ffn], axis=0),
        "hgrn_lower_bounds": d_p[0:2],
        "hgrn_out_norm": jnp.concatenate([g["out_gain"] for g in hg], axis=0),
        "mla_q_a_norm": jnp.concatenate([g["qa"] for g in mg], axis=0),
        "mla_kv_a_norm": jnp.concatenate([g["kva"] for g in mg], axis=0),
        "mla_q_norm": jnp.concatenate([g["qn"][:, :MLA_QK] for g in mg], axis=0),
        "mla_k_norm": jnp.concatenate([g["kn"][:, :MLA_QK] for g in mg], axis=0),
        "ffn_conv_w": jnp.stack([g["conv_w"] for g in g_ffn]),
        "ffn_conv_b": jnp.concatenate([g["conv_b"] for g in g_ffn], axis=0),
    }

    def chip_part(n, p):
        size = w[n].shape[small_axis[n]]
        return lax.slice_in_dim(small_grads[n], p * size, (p + 1) * size, axis=small_axis[n])

    to_chips = jnp.stack([_pack([chip_part(n, p) for n in small_names], F32, SUBLANES) for p in range(N_CHIPS)])
    rep_g, shard_g = exchange([_pack([small_grads[n] for n in REPLICATED], F32, SUBLANES), to_chips], "all",
                              [None, (2, 1, 0)], "reduce_small")
    small_out = {}
    for names, gparts in ((REPLICATED, rep_g), (small_names, shard_g)):
        packed = adamw_packed(gparts, *[_pack([t[n] for n in names], F32, SUBLANES) for t in (w, m, v)], "adamw_small")
        unpacked = [_unpack(buf, [w[n].shape for n in names]) for buf in packed]
        for k, n in enumerate(names):
            small_out[n] = [u[k] for u in unpacked]

    result = [loss, dh[None]]
    for k in range(4):
        result += [(big_out[n] if n in big_out else small_out[n])[k] for n in WEIGHTS]
    return tuple(result)
```

```python
import numpy as np
import jax
import jax.numpy as jnp
from jax import lax
from jax.experimental import pallas as pl
from jax.experimental.pallas import tpu as pltpu

F32 = jnp.float32
BF16 = jnp.bfloat16
_MXU = BF16

RMS_EPS = 1e-6
D_MODEL = 1024
HEADS = 8
HEAD_DIM = 128
HGRN_CHUNK = 128
MLA_NOPE = 128
MLA_ROPE = 64
MLA_QK = MLA_NOPE + MLA_ROPE
MLA_SLOT = 256
MLA_LORA = 256
MLA_IN_COLS = 2 * MLA_LORA + HEAD_DIM
ROPE_THETA = 10000.0
D_FF = 2816
FF_BLOCK = 1408
LANES = 128
SUBLANES = 8

ADAM_LR = 0.001
ADAM_B1 = 0.9
ADAM_B2 = 0.999
ADAM_EPS = 1e-08
ADAM_WD = 0.01
ADAM_STEP = 10

VMEM_LIMIT = 56 * 1024 * 1024
MM_VMEM_BUDGET = 46 * 1024 * 1024
MESH_AXES = ("x", "y", "c")
N_CHIPS = 4

_NN = ((1,), (0,))
_NT = ((1,), (1,))
_TN = ((0,), (0,))


def _dg(a, b, dims):
    return lax.dot_general(a.astype(_MXU), b.astype(_MXU), (dims, ((), ())), preferred_element_type=F32)


@jax.custom_vjp
def kdot(a, b):
    return _dg(a, b, _NN)


kdot.defvjp(lambda a, b: (_dg(a, b, _NN), (a, b)), lambda r, g: (_dg(g, r[1], _NT), _dg(r[0], g, _TN)))


@jax.custom_vjp
def kdot_nt(a, b):
    return _dg(a, b, _NT)


kdot_nt.defvjp(lambda a, b: (_dg(a, b, _NT), (a, b)), lambda r, g: (_dg(g, r[1], _NN), _dg(g, r[0], _TN)))


@jax.custom_vjp
def kdot_tn(a, b):
    return _dg(a, b, _TN)


kdot_tn.defvjp(lambda a, b: (_dg(a, b, _TN), (a, b)), lambda r, g: (_dg(r[1], g, _NT), _dg(r[0], g, _NN)))


def _pick(d, prefs):
    for p in prefs:
        if d >= p and d % p == 0:
            return p
    return d


def _params(sem):
    return pltpu.CompilerParams(dimension_semantics=sem, vmem_limit_bytes=VMEM_LIMIT)


def _sds(shape, dtype=F32):
    return jax.ShapeDtypeStruct(shape, dtype)


def win(arr, layer, row_off=0, col_off=0, rows=None, cols=None):
    return (arr, layer, row_off, col_off, rows or arr.shape[1] - row_off, cols or arr.shape[2] - col_off)


def mm(a, b, mode, name, add=None, out_dtype=F32, into=None, rms_gain=None, norm_bwd=None):
    if isinstance(b, tuple):
        b_arr, b_layer, b_r0, b_c0, b_rows, b_cols = b
    else:
        b_arr, b_layer, b_r0, b_c0, (b_rows, b_cols) = b, None, 0, 0, b.shape
    if mode == "nn":
        (M, K), (K2, N) = a.shape, (b_rows, b_cols)
    elif mode == "nt":
        (M, K), (N, K2) = a.shape, (b_rows, b_cols)
    else:
        (K, M), (K2, N) = a.shape, (b_rows, b_cols)
    assert K == K2, (name, a.shape, b_rows, b_cols)
    tn = N if N <= 1024 else _pick(N, (1024, 1408, 512, 256, 128))
    tk = K if K <= 2048 else _pick(K, (2048, 2816, 1024, 512, 256, 128))
    nk = K // tk
    a_bytes = jnp.dtype(a.dtype).itemsize
    b_bytes = jnp.dtype(b_arr.dtype).itemsize
    extra_tiles = (add is not None) + (rms_gain is not None) / 2 + (3 if norm_bwd is not None else 0)

    def vmem_bytes(tm_):
        tiles = 2 * (tm_ * tk * a_bytes + tk * tn * b_bytes) + tm_ * tn * 4 * (2 + 2 * extra_tiles + (nk > 1))
        return tiles

    tm = M
    if M > 1024:
        fits = [t for t in (2048, 1024, 1408, 512, 256, 128) if M % t == 0 and vmem_bytes(t) <= MM_VMEM_BUDGET]
        tm = fits[0] if fits else _pick(M, (128,))
    dims = {"nn": _NN, "nt": _NT, "tn": _TN}[mode]
    a_spec = pl.BlockSpec((tk, tm), lambda i, j, k: (k, i)) if mode == "tn" else pl.BlockSpec((tm, tk), lambda i, j, k: (i, k))
    b_blk = (tn, tk) if mode == "nt" else (tk, tn)
    assert b_r0 % b_blk[0] == 0 and b_c0 % b_blk[1] == 0, (name, b_r0, b_c0, b_blk)
    br, bc = b_r0 // b_blk[0], b_c0 // b_blk[1]
    if mode == "nt":
        b_idx = lambda i, j, k: (br + j, bc + k)
    else:
        b_idx = lambda i, j, k: (br + k, bc + j)
    if b_layer is None:
        b_spec = pl.BlockSpec(b_blk, b_idx)
    else:
        b_spec = pl.BlockSpec((None,) + b_blk, lambda i, j, k: (b_layer,) + b_idx(i, j, k))
    plain = pl.BlockSpec((tm, tn), lambda i, j, k: (i, j))
    has_add, has_rms, has_nb = add is not None, rms_gain is not None, norm_bwd is not None
    assert not (has_rms or has_nb) or (tn == N and into is None and not (has_rms and has_nb)), name
    vec = pl.BlockSpec((1, tn), lambda i, j, k: (0, j))
    ins = [a, b_arr] + ([add] if has_add else []) + ([rms_gain] if has_rms else [])
    specs = [a_spec, b_spec] + ([plain] if has_add else []) + ([vec] if has_rms else [])
    n_in = len(ins)
    if has_nb:
        nb_x, nb_gain, nb_res = norm_bwd
        ins += [nb_x, nb_gain] + ([nb_res] if nb_res is not None else [])
        specs += [plain, vec] + ([plain] if nb_res is not None else [])
    aliases = {}
    if into is None:
        o_spec, out_shape = plain, _sds((M, N), out_dtype)
    else:
        buf, o_layer, o_r0, o_c0, o_rows, o_cols = into
        assert (o_rows, o_cols) == (M, N) and o_r0 % tm == 0 and o_c0 % tn == 0, (name, into[1:], M, N, tm, tn)
        orow, ocol = o_r0 // tm, o_c0 // tn
        o_spec = pl.BlockSpec((None, tm, tn), lambda i, j, k: (o_layer, orow + i, ocol + j))
        out_shape = _sds(buf.shape, buf.dtype)
        aliases = {len(ins): 0}
        ins.append(buf)
        specs.append(pl.BlockSpec(memory_space=pl.ANY))

    n_all_in = len(ins)

    def kern(*refs):
        a_ref, b_ref = refs[0], refs[1]
        add_ref = refs[2] if has_add else None
        o_ref = refs[n_all_in]

        def finish(r):
            if has_add:
                r = r + add_ref[...].astype(F32)
            if has_nb:
                _, vjp = jax.vjp(lambda xv, gv: _rms(xv, gv, N), refs[n_in][...], refs[n_in + 1][...])
                dx, dgain = vjp(r)
                o_ref[...] = dx if nb_res is None else dx + refs[n_in + 2][...]
                _store(refs[n_all_in + 1], dgain, pl.program_id(0) == 0)
                return
            o_ref[...] = r.astype(o_ref.dtype)
            if has_rms:
                refs[n_all_in + 1][...] = _rms(r, refs[n_in - 1][...], N).astype(_MXU)

        if nk == 1:
            finish(_dg(a_ref[...], b_ref[...], dims))
            return
        acc = refs[-1]
        k = pl.program_id(2)

        @pl.when(k == 0)
        def _():
            acc[...] = jnp.zeros_like(acc)

        acc[...] += _dg(a_ref[...], b_ref[...], dims)

        @pl.when(k == nk - 1)
        def _():
            finish(acc[...])

    if has_rms:
        o_spec, out_shape = [o_spec, plain], [out_shape, _sds((M, N), _MXU)]
    if has_nb:
        o_spec, out_shape = [o_spec, vec], [out_shape, _sds((1, N))]
    return pl.pallas_call(
        kern, name=name, grid=(M // tm, N // tn, nk), in_specs=specs, out_specs=o_spec, out_shape=out_shape,
        scratch_shapes=[pltpu.VMEM((tm, tn), F32)] if nk > 1 else [], input_output_aliases=aliases,
        compiler_params=_params(("arbitrary" if has_nb else "parallel", "parallel", "arbitrary")))(*ins)


def _store(ref, val, first):
    if first is None:
        ref[...] = val.astype(ref.dtype)
        return

    @pl.when(first)
    def _():
        ref[...] = val.astype(ref.dtype)

    @pl.when(jnp.logical_not(first))
    def _():
        ref[...] += val.astype(ref.dtype)


def tilecall(body, name, grid, ins, outs, sem, prefetch=(), aliases=None):
    n_pre, n_in = len(prefetch), len(ins)

    def kern(*refs):
        vals = body(*refs[:n_pre + n_in])
        for ref, val, (_, _, first) in zip(refs[n_pre + n_in:], vals, outs):
            _store(ref, val, None if first is None else first())

    in_specs, out_specs = [s for _, s in ins], [s for _, s, _ in outs]
    kwargs = dict(name=name, out_shape=[sh for sh, _, _ in outs], compiler_params=_params(sem),
                  input_output_aliases={n_pre + k: v for k, v in (aliases or {}).items()})
    if n_pre:
        kwargs["grid_spec"] = pltpu.PrefetchScalarGridSpec(num_scalar_prefetch=n_pre, grid=grid, in_specs=in_specs,
                                                           out_specs=out_specs)
    else:
        kwargs.update(grid=grid, in_specs=in_specs, out_specs=out_specs)
    return pl.pallas_call(kern, **kwargs)(*prefetch, *[a for a, _ in ins])


def _rms(x, g, n):
    ms = jnp.sum(x * x, axis=-1, keepdims=True) / n
    return x * lax.rsqrt(ms + RMS_EPS) * g


def _row_tile(S, w):
    return min(S, 512 if w <= 1024 else 256)


def rms_fwd(x, g, name, col=0, w=None):
    S = x.shape[0]
    w = w or x.shape[1]
    ts = _row_tile(S, w)
    return tilecall(
        lambda x_ref, g_ref: (_rms(x_ref[...], g_ref[...], w),), name, (S // ts,),
        [(x, pl.BlockSpec((ts, w), lambda i: (i, col))), (g, pl.BlockSpec((1, w), lambda i: (0, 0)))],
        [(_sds((S, w), _MXU), pl.BlockSpec((ts, w), lambda i: (i, 0)), None)], ("parallel",))[0]


HALO_ROWS = 16


def _shifted(u, halo_ref, is_first):
    rid = lax.broadcasted_iota(jnp.int32, (SUBLANES, 1), 0)
    halo = halo_ref[...].astype(F32)
    hrow = lax.broadcasted_iota(jnp.int32, (HALO_ROWS, 1), 0)

    def halo_row(r):
        return jnp.where(is_first, 0.0, jnp.sum(jnp.where(hrow == r, halo, 0.0), axis=0, keepdims=True))

    h7, h6 = halo_row(HALO_ROWS - 1), halo_row(HALO_ROWS - 2)
    r1, r2 = pltpu.roll(u, 1, 0), pltpu.roll(u, 2, 0)
    top1 = jnp.where(rid == 0, h7, r1[:SUBLANES])
    top2 = jnp.where(rid == 0, h6, jnp.where(rid == 1, h7, r2[:SUBLANES]))
    return jnp.concatenate([top1, r1[SUBLANES:]], axis=0), jnp.concatenate([top2, r2[SUBLANES:]], axis=0)


def _conv(u, u1, u2, cw_ref, cb_ref):
    return ((cb_ref[...] + u2 * cw_ref[0:1, :]) + u1 * cw_ref[1:2, :]) + u * cw_ref[2:3, :]


def _ffn_specs(S, ts, jmap):
    hb = ts // HALO_ROWS
    return (pl.BlockSpec((ts, FF_BLOCK), lambda j, i: (i, jmap(j))),
            pl.BlockSpec((HALO_ROWS, FF_BLOCK), lambda j, i: (jnp.maximum(i * hb - 1, 0), jmap(j))),
            pl.BlockSpec((SUBLANES, FF_BLOCK), lambda j, i: (0, jmap(j))),
            pl.BlockSpec((1, FF_BLOCK), lambda j, i: (0, jmap(j))))


def ffn_act_fwd(u, cw8, cb, name):
    S = u.shape[0]
    ts = _row_tile(S, 2 * D_FF)
    nb = D_FF // FF_BLOCK

    def body(ug, hg, cwg, cbg, uu, hu, cwu, cbu):
        first = pl.program_id(1) == 0
        g = ug[...].astype(F32)
        g1, g2 = _shifted(g, hg, first)
        yg = _conv(g, g1, g2, cwg, cbg)
        v = uu[...].astype(F32)
        v1, v2 = _shifted(v, hu, first)
        yu = _conv(v, v1, v2, cwu, cbu)
        return (yg * jax.nn.sigmoid(yg) * yu,)

    sg = _ffn_specs(S, ts, lambda j: j)
    su = _ffn_specs(S, ts, lambda j: j + nb)
    ins = [(u, sg[0]), (u, sg[1]), (cw8, sg[2]), (cb, sg[3]), (u, su[0]), (u, su[1]), (cw8, su[2]), (cb, su[3])]
    return tilecall(body, name, (nb, S // ts), ins,
                    [(_sds((S, D_FF), _MXU), pl.BlockSpec((ts, FF_BLOCK), lambda j, i: (i, j)), None)],
                    ("parallel", "parallel"))[0]


def ffn_act_bwd(u, cw8, cb, da, name):
    S = u.shape[0]
    ts = _row_tile(S, 2 * D_FF)
    nb = D_FF // FF_BLOCK

    def taps(dy, x, x1, x2):
        return jnp.concatenate(
            [jnp.sum(dy * x2, axis=0, keepdims=True), jnp.sum(dy * x1, axis=0, keepdims=True),
             jnp.sum(dy * x, axis=0, keepdims=True), jnp.zeros((SUBLANES - 3, dy.shape[1]), F32)], axis=0)

    def body(ug, hg, cwg, cbg, uu, hu, cwu, cbu, da_ref):
        first = pl.program_id(1) == 0
        g = ug[...].astype(F32)
        g1, g2 = _shifted(g, hg, first)
        yg = _conv(g, g1, g2, cwg, cbg)
        v = uu[...].astype(F32)
        v1, v2 = _shifted(v, hu, first)
        yu = _conv(v, v1, v2, cwu, cbu)
        d = da_ref[...].astype(F32)
        sg = jax.nn.sigmoid(yg)
        dyg = d * yu * (sg * (1.0 + yg * (1.0 - sg)))
        dyu = d * (yg * sg)
        return (dyg, dyu, taps(dyg, g, g1, g2), taps(dyu, v, v1, v2),
                jnp.sum(dyg, axis=0, keepdims=True), jnp.sum(dyu, axis=0, keepdims=True))

    sg_ = _ffn_specs(S, ts, lambda j: j)
    su_ = _ffn_specs(S, ts, lambda j: j + nb)
    row = pl.BlockSpec((ts, FF_BLOCK), lambda j, i: (i, j))
    ins = [(u, sg_[0]), (u, sg_[1]), (cw8, sg_[2]), (cb, sg_[3]), (u, su_[0]), (u, su_[1]), (cw8, su_[2]), (cb, su_[3]), (da, row)]
    first_row = lambda: pl.program_id(1) == 0
    dy, dcw, dcb = (_sds((S, D_FF)), row, None), (_sds((SUBLANES, D_FF)), sg_[2], first_row), (_sds((1, D_FF)), sg_[3], first_row)
    return tilecall(body, name, (nb, S // ts), ins, [dy, dy, dcw, dcw, dcb, dcb], ("parallel", "arbitrary"))


def ffn_conv_bwd(dyg, dyu, cw8, name):
    S = dyg.shape[0]
    ts = _row_tile(S, 2 * D_FF)
    hb = ts // SUBLANES
    nrow = S // ts
    nb = D_FF // FF_BLOCK

    def back(dy_ref, halo_ref, cw_ref):
        last = pl.program_id(1) == nrow - 1
        d = dy_ref[...]
        rid = lax.broadcasted_iota(jnp.int32, (SUBLANES, 1), 0)
        n0 = jnp.where(last, 0.0, halo_ref[0:1, :])
        n1 = jnp.where(last, 0.0, halo_ref[1:2, :])
        r1, r2 = pltpu.roll(d, ts - 1, 0), pltpu.roll(d, ts - 2, 0)
        end1 = jnp.where(rid == SUBLANES - 1, n0, r1[ts - SUBLANES:])
        end2 = jnp.where(rid == SUBLANES - 1, n1, jnp.where(rid == SUBLANES - 2, n0, r2[ts - SUBLANES:]))
        d1 = jnp.concatenate([r1[:ts - SUBLANES], end1], axis=0)
        d2 = jnp.concatenate([r2[:ts - SUBLANES], end2], axis=0)
        return d * cw_ref[2:3, :] + d1 * cw_ref[1:2, :] + d2 * cw_ref[0:1, :]

    row = pl.BlockSpec((ts, FF_BLOCK), lambda j, i: (i, j))
    halo = pl.BlockSpec((SUBLANES, FF_BLOCK), lambda j, i: (jnp.minimum((i + 1) * hb, S // SUBLANES - 1), j))
    ins = [(dyg, row), (dyg, halo), (cw8, pl.BlockSpec((SUBLANES, FF_BLOCK), lambda j, i: (0, j))),
           (dyu, row), (dyu, halo), (cw8, pl.BlockSpec((SUBLANES, FF_BLOCK), lambda j, i: (0, j + nb)))]
    out = (_sds((S, D_FF), _MXU), row, None)
    return tilecall(lambda a, b, c, d, e, f: (back(a, b, c), back(d, e, f)), name, (nb, nrow), ins, [out, out],
                    ("parallel", "parallel"))


def _hgrn_levels(C):
    out, m = [], C // 2
    while m >= 1:
        out.append(m)
        m //= 2
    return out


def _hgrn_sum_matrix(C):
    t = np.arange(C)[:, None]
    u = np.arange(C)[None, :]
    blocks = [u <= t, u > t]
    for m in _hgrn_levels(C):
        r = (t // (2 * m)) * (2 * m) + m
        right = (t % (2 * m)) >= m
        blocks.append((right & (u > r) & (u <= t)) | ((~right) & (u > t) & (u <= r)))
    return np.concatenate(blocks, axis=0).astype(np.float32)


def _make_partial_sums(nb, C):
    @jax.custom_vjp
    def sums(mall, lf):
        hi = lf.astype(_MXU)
        mid = (lf - hi.astype(F32)).astype(_MXU)
        e2 = _dg(mall, jnp.concatenate([hi, mid], axis=1), _NN)
        e = e2[:, :HEAD_DIM] + e2[:, HEAD_DIM:]
        return tuple(e[b * C:(b + 1) * C] for b in range(nb))

    def fwd(mall, lf):
        return sums(mall, lf), mall

    def bwd(mall, gs):
        return jnp.zeros_like(mall), _dg(mall, jnp.concatenate(gs, axis=0), _TN)

    sums.defvjp(fwd, bwd)
    return sums


def _hgrn_chunk(zq, zf, v, lb, st, mall, C):
    levels = _hgrn_levels(C)
    qs = zq * jax.nn.sigmoid(zq)
    fg = lb + (1.0 - lb) * jax.nn.sigmoid(zf)
    k = 1.0 - fg
    e = _make_partial_sums(2 + len(levels), C)(mall, jnp.log(fg))
    g_incl, g_after = e[0], e[1]
    rid = lax.broadcasted_iota(jnp.int32, (C, 1), 0)
    tt = lax.broadcasted_iota(jnp.int32, (C, C), 0)
    ss = lax.broadcasted_iota(jnp.int32, (C, C), 1)
    o = kdot_nt(qs * jnp.exp(g_incl), st)
    o = o + jnp.sum(qs * k, axis=-1, keepdims=True) * v
    scores = jnp.zeros((C, C), F32)
    for li, m in enumerate(levels):
        sh = int(np.log2(m))
        right = ((rid >> sh) & 1) == 1
        both = jnp.where(right, qs, k) * jnp.exp(e[2 + li])
        pair = ((tt >> (sh + 1)) == (ss >> (sh + 1))) & (((tt >> sh) & 1) == 1) & (((ss >> sh) & 1) == 0)
        scores = scores + jnp.where(pair, kdot_nt(both, both), 0.0)
    o = o + kdot(scores, v)
    g_last = jnp.sum(jnp.where(rid == C - 1, g_incl, 0.0), axis=0, keepdims=True)
    st_new = st * jnp.exp(g_last) + kdot_tn(v, k * jnp.exp(g_after))
    return o, st_new


HGRN_HEADS_PER_STEP = 8
_HGRN_LANES = HGRN_HEADS_PER_STEP * HEAD_DIM


def _hgrn_in_specs(C, nc, rev):
    cm = (lambda c: nc - 1 - c) if rev else (lambda c: c)
    blk = lambda: pl.BlockSpec((C, _HGRN_LANES), lambda h, c: (cm(c), h))
    return cm, [blk(), blk(), blk(), pl.BlockSpec((1, _HGRN_LANES), lambda h, c: (0, h))]


def _hgrn_state_spec(cm):
    return pl.BlockSpec((HGRN_HEADS_PER_STEP, None, HEAD_DIM, HEAD_DIM), lambda h, c: (h, cm(c), 0, 0))


def _hgrn_out(o, g, gain):
    return _rms(o, gain, HEAD_DIM) * (g * jax.nn.sigmoid(g))


def hgrn_fwd(zq, zf, zi, zg, lb, out_gain, mall, name):
    S = zq.shape[0]
    C = min(HGRN_CHUNK, S)
    nc = S // C

    def kern(zq_ref, zf_ref, zi_ref, lb_ref, zg_ref, gain_ref, mall_ref, o_ref, on_ref, st_ref, st):
        @pl.when(pl.program_id(1) == 0)
        def _():
            st[...] = jnp.zeros_like(st)

        mall_v = mall_ref[...]
        for g in range(HGRN_HEADS_PER_STEP):
            lanes = slice(g * HEAD_DIM, (g + 1) * HEAD_DIM)
            s_in = st[g]
            st_ref[g] = s_in
            o, s_new = _hgrn_chunk(zq_ref[:, lanes], zf_ref[:, lanes], zi_ref[:, lanes], lb_ref[:, lanes], s_in, mall_v, C)
            o_ref[:, lanes] = o
            on_ref[:, lanes] = _hgrn_out(o, zg_ref[:, lanes], gain_ref[...]).astype(on_ref.dtype)
            st[g] = s_new

    cm, specs = _hgrn_in_specs(C, nc, False)
    row = pl.BlockSpec((C, _HGRN_LANES), lambda h, c: (c, h))
    return pl.pallas_call(
        kern, name=name, grid=(HEADS // HGRN_HEADS_PER_STEP, nc),
        in_specs=specs + [row, pl.BlockSpec((1, HEAD_DIM), lambda h, c: (0, 0)), pl.BlockSpec(mall.shape, lambda h, c: (0, 0))],
        out_specs=[row, row, _hgrn_state_spec(cm)],
        out_shape=[_sds((S, D_MODEL)), _sds((S, D_MODEL), _MXU), _sds((HEADS, nc, HEAD_DIM, HEAD_DIM))],
        scratch_shapes=[pltpu.VMEM((HGRN_HEADS_PER_STEP, HEAD_DIM, HEAD_DIM), F32)],
        compiler_params=_params(("parallel", "arbitrary")))(zq, zf, zi, lb, zg, out_gain, mall)


def hgrn_bwd(zq, zf, zi, zg, lb, out_gain, mall, states, o, don, name):
    S = zq.shape[0]
    C = min(HGRN_CHUNK, S)
    nc = S // C

    def kern(zq_ref, zf_ref, zi_ref, lb_ref, zg_ref, gain_ref, mall_ref, st_ref, o_ref, don_ref,
             dq_ref, df_ref, di_ref, dg_ref, dlb_ref, dgain_ref, dst):
        first = pl.program_id(1) == 0

        @pl.when(first)
        def _():
            dst[...] = jnp.zeros_like(dst)

        mall_v = mall_ref[...]
        gls, dgain = [], None
        for g in range(HGRN_HEADS_PER_STEP):
            lanes = slice(g * HEAD_DIM, (g + 1) * HEAD_DIM)
            _, out_vjp = jax.vjp(_hgrn_out, o_ref[:, lanes], zg_ref[:, lanes], gain_ref[...])
            do, dzg, dgn = out_vjp(don_ref[:, lanes])
            dg_ref[:, lanes] = dzg.astype(dg_ref.dtype)
            dgain = dgn if dgain is None else dgain + dgn
            _, vjp = jax.vjp(lambda a, b, c, d, e: _hgrn_chunk(a, b, c, d, e, mall_v, C),
                             zq_ref[:, lanes], zf_ref[:, lanes], zi_ref[:, lanes], lb_ref[:, lanes], st_ref[g])
            ga, gb, gv, gl, gs = vjp((do, dst[g]))
            dq_ref[:, lanes] = ga.astype(dq_ref.dtype)
            df_ref[:, lanes] = gb.astype(df_ref.dtype)
            di_ref[:, lanes] = gv.astype(di_ref.dtype)
            gls.append(gl)
            dst[g] = gs
        _store(dlb_ref, jnp.concatenate(gls, axis=1), first)
        _store(dgain_ref, dgain, first & (pl.program_id(0) == 0))

    cm, specs = _hgrn_in_specs(C, nc, True)
    row = lambda: pl.BlockSpec((C, _HGRN_LANES), lambda h, c: (cm(c), h))
    vec = pl.BlockSpec((1, HEAD_DIM), lambda h, c: (0, 0))
    wide = _sds((S, D_MODEL), _MXU)
    return pl.pallas_call(
        kern, name=name, grid=(HEADS // HGRN_HEADS_PER_STEP, nc),
        in_specs=specs + [row(), vec, pl.BlockSpec(mall.shape, lambda h, c: (0, 0)), _hgrn_state_spec(cm), row(), row()],
        out_specs=[row(), row(), row(), row(), pl.BlockSpec((1, _HGRN_LANES), lambda h, c: (0, h)), vec],
        out_shape=[wide, wide, wide, wide, _sds((1, D_MODEL)), _sds((1, HEAD_DIM))],
        scratch_shapes=[pltpu.VMEM((HGRN_HEADS_PER_STEP, HEAD_DIM, HEAD_DIM), F32)],
        compiler_params=_params(("arbitrary", "arbitrary")))(zq, zf, zi, lb, zg, out_gain, mall, states, o, don)


def _lb_soft(p0, p1):
    mx = jnp.maximum(p0, p1)
    e0, e1 = jnp.exp(p0 - mx), jnp.exp(p1 - mx)
    s0, s1 = e0 / (e0 + e1), e1 / (e0 + e1)
    return (s0 + s1) - s0


def lower_bound_fwd(p, name):
    assert p.shape[0] == 2

    def body(p_ref):
        s = _lb_soft(p_ref[0:1, :], p_ref[1:2, :])
        return (jnp.concatenate([jnp.zeros_like(s), s] + [jnp.zeros_like(s)] * (SUBLANES - 2), axis=0),)

    spec8 = pl.BlockSpec((SUBLANES, p.shape[1]), lambda: (0, 0))
    return tilecall(body, name, (), [(p, pl.BlockSpec(p.shape, lambda: (0, 0)))], [(_sds((SUBLANES, p.shape[1])), spec8, None)], ())[0]


def lower_bound_bwd(p, dlb1, name):
    def body(p_ref, d_ref):
        _, vjp = jax.vjp(_lb_soft, p_ref[0:1, :], p_ref[1:2, :])
        g0, g1 = vjp(d_ref[...])
        return (jnp.concatenate([g0, g1] + [jnp.zeros_like(g0)] * (SUBLANES - 2), axis=0),)

    spec8 = pl.BlockSpec((SUBLANES, p.shape[1]), lambda: (0, 0))
    return tilecall(body, name, (), [(p, pl.BlockSpec(p.shape, lambda: (0, 0))), (dlb1, pl.BlockSpec(dlb1.shape, lambda: (0, 0)))],
                    [(_sds((SUBLANES, p.shape[1])), spec8, None)], ())[0]


@jax.custom_vjp
def _swap_rope_halves(x):
    lane = lax.broadcasted_iota(jnp.int32, x.shape, 1)
    lo = (lane >= MLA_NOPE) & (lane < MLA_NOPE + MLA_ROPE // 2)
    hi = (lane >= MLA_NOPE + MLA_ROPE // 2) & (lane < MLA_QK)
    return jnp.where(lo, pltpu.roll(x, MLA_SLOT - MLA_ROPE // 2, 1), jnp.where(hi, pltpu.roll(x, MLA_ROPE // 2, 1), 0.0))


_swap_rope_halves.defvjp(lambda x: (_swap_rope_halves(x), None), lambda _, g: (_swap_rope_halves(g),))


def _norm_rope(x, gain, cos_t, sin_t):
    y = _rms(x, gain, MLA_QK)
    return y * cos_t + _swap_rope_halves(y) * sin_t


_ATTN_SCALE = MLA_QK ** -0.5


def _qk_heads(qs, kn, kr, qn, kn_gain, cos_t, sin_t):
    q = _norm_rope(qs, qn, cos_t, sin_t) * _ATTN_SCALE
    k = _norm_rope(jnp.concatenate([kn, kr], axis=1), kn_gain, cos_t, sin_t)
    return q, k


QK_ROWS = 512
QK_HEADS_PER_STEP = 4


def _qk_specs(ts):
    G = QK_HEADS_PER_STEP
    slot = pl.BlockSpec((ts, G * MLA_SLOT), lambda i, h: (i, h))
    shared = pl.BlockSpec((ts, HEAD_DIM), lambda i, h: (i, 0))
    gain = pl.BlockSpec((1, MLA_SLOT), lambda i, h: (0, 0))
    table = pl.BlockSpec((ts, MLA_SLOT), lambda i, h: (i, 0))
    return slot, shared, gain, table


def _head_slot(g):
    return slice(g * MLA_SLOT, (g + 1) * MLA_SLOT)


def _head_nope(g):
    return slice(g * MLA_SLOT, g * MLA_SLOT + HEAD_DIM)


def qk_fwd(qslots, kv, krope, qn, kn, cos_t, sin_t, name):
    S = qslots.shape[0]
    ts = min(S, QK_ROWS)
    slot, shared, gain, table = _qk_specs(ts)

    def body(q_ref, kv_ref, kr_ref, qn_ref, kg_ref, c_ref, s_ref):
        kr, qg, kg, c, s = kr_ref[...], qn_ref[...], kg_ref[...], c_ref[...], s_ref[...]
        heads = [_qk_heads(q_ref[:, _head_slot(g)], kv_ref[:, _head_nope(g)], kr, qg, kg, c, s)
                 for g in range(QK_HEADS_PER_STEP)]
        return jnp.concatenate([h[0] for h in heads], axis=1), jnp.concatenate([h[1] for h in heads], axis=1)

    out = _sds((S, HEADS * MLA_SLOT), _MXU)
    return tilecall(body, name, (S // ts, HEADS // QK_HEADS_PER_STEP),
                    [(qslots, slot), (kv, slot), (krope, shared), (qn, gain), (kn, gain), (cos_t, table), (sin_t, table)],
                    [(out, slot, None), (out, slot, None)], ("parallel", "parallel"))


def qk_bwd(qslots, kv, krope, qn, kn, cos_t, sin_t, dq, dk, dv, name):
    S = qslots.shape[0]
    ts = min(S, QK_ROWS)
    slot, shared, gain, table = _qk_specs(ts)
    vblk = pl.BlockSpec((ts, QK_HEADS_PER_STEP * HEAD_DIM), lambda i, h: (i, h))

    def body(q_ref, kv_ref, kr_ref, qn_ref, kg_ref, c_ref, s_ref, dq_ref, dk_ref, dv_ref):
        kr, qg, kg, c, s = kr_ref[...], qn_ref[...], kg_ref[...], c_ref[...], s_ref[...]
        dqs, dkvs, gr, g1, g2 = [], [], None, None, None
        for g in range(QK_HEADS_PER_STEP):
            _, vjp = jax.vjp(lambda a, b, r, w1, w2: _qk_heads(a, b, r, w1, w2, c, s),
                             q_ref[:, _head_slot(g)], kv_ref[:, _head_nope(g)], kr, qg, kg)
            ga, gb, r_, a_, b_ = vjp((dq_ref[:, _head_slot(g)], dk_ref[:, _head_slot(g)]))
            dqs.append(ga)
            dkvs += [gb, dv_ref[:, g * HEAD_DIM:(g + 1) * HEAD_DIM]]
            gr, g1, g2 = (r_, a_, b_) if gr is None else (gr + r_, g1 + a_, g2 + b_)
        return jnp.concatenate(dqs, axis=1), jnp.concatenate(dkvs, axis=1), gr, g1, g2

    first_head = lambda: pl.program_id(1) == 0
    first = lambda: (pl.program_id(0) == 0) & (pl.program_id(1) == 0)
    wide = _sds((S, HEADS * MLA_SLOT), _MXU)
    return tilecall(body, name, (S // ts, HEADS // QK_HEADS_PER_STEP),
                    [(qslots, slot), (kv, slot), (krope, shared), (qn, gain), (kn, gain), (cos_t, table), (sin_t, table),
                     (dq, slot), (dk, slot), (dv, vblk)],
                    [(wide, slot, None), (wide, slot, None), (_sds((S, HEAD_DIM)), shared, first_head),
                     (_sds((1, MLA_SLOT)), gain, first), (_sds((1, MLA_SLOT)), gain, first)], ("arbitrary", "arbitrary"))


ATTN_TILE_FWD = 1024
ATTN_TILE = 1024
ATTN_HEADS_PER_STEP = 4
ATTN_HEADS_PER_STEP_BWD = 1


def _causal_pairs(nq, by_row):
    pairs = [(i, j) for i in range(nq) for j in range(i + 1)] if by_row else [(i, j) for j in range(nq) for i in range(j, nq)]
    return jnp.asarray([p[0] for p in pairs], jnp.int32), jnp.asarray([p[1] for p in pairs], jnp.int32)


def _diag_mask(s, tq):
    rows = lax.broadcasted_iota(jnp.int32, (tq, tq), 0)
    cols = lax.broadcasted_iota(jnp.int32, (tq, tq), 1)
    return jnp.where(rows >= cols, s, -jnp.inf)


def attn_fwd(qr, kr, kv, name):
    S = qr.shape[0]
    tq = min(S, ATTN_TILE_FWD)
    nq = S // tq
    i_tab, j_tab = _causal_pairs(nq, True)

    G = ATTN_HEADS_PER_STEP
    reps = tq // LANES

    def kern(it, jt, q_ref, k_ref, kv_ref, o_ref, lse_ref, m_s, l_s, acc):
        n = pl.program_id(1)
        i, j = it[n], jt[n]

        @pl.when(j == 0)
        def _():
            m_s[...] = jnp.full_like(m_s, -jnp.inf)
            l_s[...] = jnp.zeros_like(l_s)
            acc[...] = jnp.zeros_like(acc)

        def step(diagonal):
            for g in range(G):
                slot = slice(g * MLA_SLOT, (g + 1) * MLA_SLOT)
                s = _dg(q_ref[:, slot], k_ref[:, slot], _NT)
                if diagonal:
                    s = _diag_mask(s, tq)
                m_prev = m_s[g]
                m_new = jnp.maximum(m_prev, jnp.max(s, axis=-1, keepdims=True))
                alpha = jnp.exp(m_prev - m_new)
                p = jnp.exp(s - jnp.tile(m_new, (1, reps)))
                l_s[g] = alpha * l_s[g] + jnp.sum(p, axis=-1, keepdims=True)
                acc[g] = alpha * acc[g] + _dg(p, kv_ref[:, g * MLA_SLOT + HEAD_DIM:(g + 1) * MLA_SLOT], _NN)
                m_s[g] = m_new

        @pl.when(j < i)
        def _():
            step(False)

        @pl.when(j == i)
        def _():
            step(True)
            for g in range(G):
                l = l_s[g]
                lanes = slice(g * HEAD_DIM, (g + 1) * HEAD_DIM)
                o_ref[:, lanes] = acc[g] / l
                lse_ref[:, lanes] = m_s[g] + jnp.log(l)

    out = _sds((S, HEADS * HEAD_DIM))
    oblk = pl.BlockSpec((tq, G * HEAD_DIM), lambda h, n, it, jt: (it[n], h))
    stat = pltpu.VMEM((G, tq, HEAD_DIM), F32)
    return pl.pallas_call(
        kern, name=name,
        grid_spec=pltpu.PrefetchScalarGridSpec(
            num_scalar_prefetch=2, grid=(HEADS // G, i_tab.shape[0]),
            in_specs=[pl.BlockSpec((tq, G * MLA_SLOT), lambda h, n, it, jt: (it[n], h)),
                      pl.BlockSpec((tq, G * MLA_SLOT), lambda h, n, it, jt: (jt[n], h)),
                      pl.BlockSpec((tq, G * MLA_SLOT), lambda h, n, it, jt: (jt[n], h))],
            out_specs=[oblk, oblk], scratch_shapes=[stat, stat, stat]),
        out_shape=[out, out], compiler_params=_params(("parallel", "arbitrary")))(i_tab, j_tab, qr, kr, kv)


def attn_bwd(qr, kr, kv, o, lse, do, name):
    S = qr.shape[0]
    tq = min(S, ATTN_TILE)
    nq = S // tq
    i_tab, j_tab = _causal_pairs(nq, False)

    G = ATTN_HEADS_PER_STEP_BWD

    def kern(it, jt, q_ref, k_ref, kv_ref, o_ref, lse_ref, do_ref, dq_ref, dk_ref, dv_ref, dk_acc, dv_acc):
        n = pl.program_id(1)
        i, j = it[n], jt[n]

        @pl.when(n == 0)
        def _():
            dq_ref[...] = jnp.zeros_like(dq_ref)

        @pl.when(i == j)
        def _():
            dk_acc[...] = jnp.zeros_like(dk_acc)
            dv_acc[...] = jnp.zeros_like(dv_acc)

        def step(diagonal):
            rows = pl.ds(pl.multiple_of(i * tq, tq), tq)
            for g in range(G):
                slot = slice(g * MLA_SLOT, (g + 1) * MLA_SLOT)
                lanes = slice(g * HEAD_DIM, (g + 1) * HEAD_DIM)
                q, k = q_ref[:, slot], k_ref[:, slot]
                s = _dg(q, k, _NT) - jnp.tile(lse_ref[:, lanes], (1, tq // LANES))
                if diagonal:
                    s = _diag_mask(s, tq)
                p = jnp.exp(s)
                d = do_ref[:, lanes]
                delta = jnp.sum(d * o_ref[:, lanes], axis=-1, keepdims=True)
                dv_acc[:, lanes] += _dg(p, d, _TN)
                ds = p * (_dg(d, kv_ref[:, g * MLA_SLOT + HEAD_DIM:(g + 1) * MLA_SLOT], _NT) - delta)
                dk_acc[:, slot] += _dg(ds, q, _TN)
                dq_ref[rows, slot] += _dg(ds, k, _NN)

        @pl.when(i > j)
        def _():
            step(False)

        @pl.when(i == j)
        def _():
            step(True)

        @pl.when(i == nq - 1)
        def _():
            dk_ref[...] = dk_acc[...]
            dv_ref[...] = dv_acc[...]

    qblk = pl.BlockSpec((tq, G * MLA_SLOT), lambda h, n, it, jt: (it[n], h))
    oblk = pl.BlockSpec((tq, G * HEAD_DIM), lambda h, n, it, jt: (it[n], h))
    kblk = pl.BlockSpec((tq, G * MLA_SLOT), lambda h, n, it, jt: (jt[n], h))
    return pl.pallas_call(
        kern, name=name,
        grid_spec=pltpu.PrefetchScalarGridSpec(
            num_scalar_prefetch=2, grid=(HEADS // G, i_tab.shape[0]),
            in_specs=[qblk, kblk, kblk, oblk, oblk, oblk],
            out_specs=[pl.BlockSpec((S, G * MLA_SLOT), lambda h, n, it, jt: (0, h)), kblk,
                       pl.BlockSpec((tq, G * HEAD_DIM), lambda h, n, it, jt: (jt[n], h))],
            scratch_shapes=[pltpu.VMEM((tq, G * MLA_SLOT), F32), pltpu.VMEM((tq, G * HEAD_DIM), F32)]),
        out_shape=[_sds((S, HEADS * MLA_SLOT)), _sds((S, HEADS * MLA_SLOT)), _sds((S, HEADS * HEAD_DIM))],
        compiler_params=_params(("parallel", "arbitrary")))(i_tab, j_tab, qr, kr, kv, o, lse, do)


def loss_head(y, target, name):
    S, Dm = y.shape
    ts = _row_tile(S, Dm)

    def body(y_ref, t_ref):
        e = y_ref[...] - t_ref[...]
        tot = jnp.sum(jnp.sum(e * e, axis=-1, keepdims=True) / Dm, axis=0, keepdims=True)
        return e / Dm, jnp.broadcast_to(0.5 * tot, (SUBLANES, LANES))

    row = pl.BlockSpec((ts, Dm), lambda i: (i, 0))
    return tilecall(body, name, (S // ts,), [(y, row), (target, row)],
                    [(_sds((S, Dm)), row, None),
                     (_sds((SUBLANES, LANES)), pl.BlockSpec((SUBLANES, LANES), lambda i: (0, 0)), lambda: pl.program_id(0) == 0)],
                    ("arbitrary",))


_PEER_FLIPS = {
    "chips": ((1, 0, 0), (0, 1, 0), (1, 1, 0)),
    "sibling": ((0, 0, 1),),
    "all": tuple((a, b, c) for a in (0, 1) for b in (0, 1) for c in (0, 1))[1:],
}
_SLOT_WEIGHTS = {"chips": (2, 1, 0), "sibling": (0, 0, 1), "all": (4, 2, 1)}
_HBM = pl.BlockSpec(memory_space=pltpu.HBM)


def _me():
    return lax.axis_index("x"), lax.axis_index("y"), lax.axis_index("c")


def _remote(src, dst, send_sem, recv_sem, peer):
    return pltpu.make_async_remote_copy(src_ref=src, dst_ref=dst, send_sem=send_sem, recv_sem=recv_sem,
                                        device_id=peer, device_id_type=pl.DeviceIdType.MESH)


def exchange(arrs, group, slab_weights, name, keep_own=True):
    flips = _PEER_FLIPS[group]
    wx, wy, wc = _SLOT_WEIGHTS[group]
    n_slots = len(flips) + (1 if keep_own else 0)
    n = len(arrs)

    def slab(ref, a, pos):
        w = slab_weights[a]
        return ref if w is None else ref.at[w[0] * pos[0] + w[1] * pos[1] + w[2] * pos[2]]

    def kern(*refs):
        srcs, outs = refs[:n], refs[n:2 * n]
        send_sems, recv_sems = refs[2 * n:2 * n + 2]
        me = _me()
        my_slot = wx * me[0] + wy * me[1] + wc * me[2]
        copies = []
        if keep_own:
            local_sems = refs[2 * n + 2]
            for a in range(n):
                cp = pltpu.make_async_copy(slab(srcs[a], a, me), outs[a].at[my_slot], local_sems.at[a])
                cp.start()
                copies.append(cp)
        for f, flip in enumerate(flips):
            peer = tuple(m ^ b if b else m for m, b in zip(me, flip))
            for a in range(n):
                cp = _remote(slab(srcs[a], a, peer), outs[a].at[my_slot if keep_own else f],
                             send_sems.at[f, a], recv_sems.at[f, a], peer)
                cp.start()
                copies.append(cp)
        for cp in copies:
            cp.wait()

    out_shape = [_sds((n_slots,) + (a.shape if slab_weights[k] is None else a.shape[1:]), a.dtype) for k, a in enumerate(arrs)]
    sems = [pltpu.SemaphoreType.DMA((len(flips), n)), pltpu.SemaphoreType.DMA((len(flips), n))]
    return pl.pallas_call(
        kern, name=name, in_specs=[_HBM] * n, out_specs=[_HBM] * n, out_shape=out_shape,
        scratch_shapes=sems + ([pltpu.SemaphoreType.DMA((n,))] if keep_own else []))(*arrs)


def _chip_window(ref, kind, size, chip, layers):
    if kind == "rows":
        return ref.at[layers, pl.ds(chip * size, size), :]
    return ref.at[layers, :, pl.ds(pl.multiple_of(chip * size, LANES), size)]


def gather_big(shards, kinds, name):
    n = len(shards)
    fulls = []
    for s, kind in zip(shards, kinds):
        L, r, c = s.shape
        fulls.append(_sds((L, N_CHIPS * r, c) if kind == "rows" else (L, r, N_CHIPS * c), s.dtype))

    def kern(*refs):
        srcs, outs = refs[:n], refs[n:2 * n]
        ici_s, ici_r, relay_s, relay_r, d2d_s, d2d_r = refs[2 * n:]
        x, y, c = _me()
        sibling = (x, y, 1 - c)
        nbrs = ((x ^ 1, y), (x, y ^ 1))
        relay_from = (x ^ (1 - c), y ^ c)
        relay_to = (x ^ c, y ^ (1 - c), c)
        diagonal = (x ^ 1, y ^ 1)

        def window(a, chip):
            L, r, cc = shards[a].shape
            size = r if kinds[a] == "rows" else cc
            return _chip_window(outs[a], kinds[a], size, 2 * chip[0] + chip[1], pl.ds(c * (L // 2), L // 2))

        def direct(a, f):
            L = shards[a].shape[0]
            return _remote(srcs[a].at[pl.ds(c * (L // 2), L // 2)], window(a, (x, y)), ici_s.at[a, f], ici_r.at[a, f],
                           (nbrs[f][0], nbrs[f][1], c))

        def to_sibling(a, k, chip):
            return _remote(window(a, chip), window(a, chip), d2d_s.at[a, k], d2d_r.at[a, k], sibling)

        sends, relays, passed = [], [], []
        for a in range(n):
            for f in range(2):
                cp = direct(a, f)
                cp.start()
                sends.append(cp)
        for a in range(n):
            _remote(window(a, relay_from), window(a, relay_from), ici_s.at[a, c], ici_r.at[a, c], relay_to).wait_recv()
            cp = _remote(window(a, relay_from), window(a, relay_from), relay_s.at[a], relay_r.at[a], relay_to)
            cp.start()
            relays.append(cp)
            passed.append(to_sibling(a, 0, relay_from))
            passed[-1].start()
            _remote(window(a, relay_from), window(a, relay_from), ici_s.at[a, 1 - c], ici_r.at[a, 1 - c], relay_to).wait_recv()
            passed.append(to_sibling(a, 1, (relay_to[0], relay_to[1])))
            passed[-1].start()
        for a in range(n):
            relays[a].wait_recv()
            passed.append(to_sibling(a, 2, diagonal))
            passed[-1].start()
        for cp in sends + relays:
            cp.wait_send()
        for cp in passed:
            cp.wait()

    pair = pltpu.SemaphoreType.DMA((n, 2))
    one = pltpu.SemaphoreType.DMA((n,))
    three = pltpu.SemaphoreType.DMA((n, 3))
    return pl.pallas_call(kern, name=name, in_specs=[_HBM] * n, out_specs=[_HBM] * n, out_shape=fulls,
                          scratch_shapes=[pair, pair, one, one, three, three])(*shards)


def send_other_half(arrs, name):
    n = len(arrs)

    def kern(*refs):
        srcs, outs = refs[:n], refs[n:2 * n]
        send_sems, recv_sems = refs[2 * n:]
        x, y, c = _me()
        copies = []
        for a in range(n):
            hl = arrs[a].shape[0] // 2
            cp = _remote(srcs[a].at[pl.ds((1 - c) * hl, hl)], outs[a], send_sems.at[a], recv_sems.at[a], (x, y, 1 - c))
            cp.start()
            copies.append(cp)
        for cp in copies:
            cp.wait()

    return pl.pallas_call(
        kern, name=name, in_specs=[_HBM] * n, out_specs=[_HBM] * n,
        out_shape=[_sds((a.shape[0] // 2,) + a.shape[1:], a.dtype) for a in arrs],
        scratch_shapes=[pltpu.SemaphoreType.DMA((n,)), pltpu.SemaphoreType.DMA((n,))])(*arrs)


def _axis_neighbours():
    x, y, c = _me()
    return (x, y, c), (x ^ (1 - c), y ^ c), (x ^ c, y ^ (1 - c)), (x ^ 1, y ^ 1)


def reduce_first_axis(arrs, kinds, name):
    n = len(arrs)
    shapes = []
    for a, kind in zip(arrs, kinds):
        l, R, C = a.shape
        shapes.append((l, R // N_CHIPS, C) if kind == "rows" else (l, R, C // N_CHIPS))

    def kern(*refs):
        srcs, outs = refs[:n], refs[n:2 * n]
        send_sems, recv_sems = refs[2 * n:]
        (x, y, c), first, second, diagonal = _axis_neighbours()
        copies = []
        for a in range(n):
            size = shapes[a][1] if kinds[a] == "rows" else shapes[a][2]
            for k, chip in enumerate((first, diagonal)):
                window = _chip_window(srcs[a], kinds[a], size, 2 * chip[0] + chip[1], slice(None))
                cp = _remote(window, outs[a].at[k], send_sems.at[a, k], recv_sems.at[a, k], (first[0], first[1], c))
                cp.start()
                copies.append(cp)
        for cp in copies:
            cp.wait()

    sem = pltpu.SemaphoreType.DMA((n, 2))
    return pl.pallas_call(
        kern, name=name, in_specs=[_HBM] * n, out_specs=[_HBM] * n,
        out_shape=[_sds((2,) + s, a.dtype) for s, a in zip(shapes, arrs)], scratch_shapes=[sem, sem])(*arrs)


def reduce_second_axis(arrs, name):
    n = len(arrs)

    def kern(*refs):
        srcs, outs = refs[:n], refs[n:2 * n]
        send_sems, recv_sems = refs[2 * n:]
        (x, y, c), first, second, diagonal = _axis_neighbours()
        copies = []
        for a in range(n):
            cp = _remote(srcs[a], outs[a], send_sems.at[a], recv_sems.at[a], (second[0], second[1], c))
            cp.start()
            copies.append(cp)
        for cp in copies:
            cp.wait()

    return pl.pallas_call(
        kern, name=name, in_specs=[_HBM] * n, out_specs=[_HBM] * n, out_shape=[_sds(a.shape, a.dtype) for a in arrs],
        scratch_shapes=[pltpu.SemaphoreType.DMA((n,)), pltpu.SemaphoreType.DMA((n,))])(*arrs)


def _stack_tile(r, c):
    for t in (1024, 704, 512, 352, 256, 128, 64, 32, 16):
        if r % t == 0 and t * c * 4 <= 3 * 512 * 1024:
            return t
    return r


def _window_map(kind, r, tr):
    nrt = r // tr
    if kind == "rows":
        return lambda l, i, chip: (l, chip[0] * nrt + i, 0)
    return lambda l, i, chip: (l, i, chip[0])


def place(full, shard, kind, chip, name):
    L, r, c = shard.shape
    tr = _stack_tile(r, c)
    wmap = _window_map(kind, r, tr)
    return tilecall(lambda chip_ref, s_ref, f_ref: (s_ref[...],), name, (L, r // tr),
                    [(shard, pl.BlockSpec((None, tr, c), lambda l, i, chip: (l, i, 0))), (full, pl.BlockSpec(memory_space=pl.ANY))],
                    [(_sds(full.shape, full.dtype), pl.BlockSpec((None, tr, c), lambda l, i, chip: wmap(l, i, chip)), None)],
                    ("parallel", "parallel"), prefetch=(chip,), aliases={1: 0})[0]


def add_cores(g, other, core, name):
    L, R, C = g.shape
    hl = L // 2
    tr = _stack_tile(R, C)
    blk = (None, tr, C)
    return tilecall(lambda core_ref, a_ref, b_ref: (a_ref[...] + b_ref[...],), name, (hl, R // tr),
                    [(g, pl.BlockSpec(blk, lambda l, i, core: (core[0] * hl + l, i, 0))),
                     (other, pl.BlockSpec(blk, lambda l, i, core: (l, i, 0)))],
                    [(_sds((hl, R, C), _MXU), pl.BlockSpec(blk, lambda l, i, core: (l, i, 0)), None)],
                    ("parallel", "parallel"), prefetch=(core,))[0]


def add_first_axis(own, got, kind, chips, name):
    _, l, r, c = got.shape
    tr = _stack_tile(r, c)
    nrt = r // tr

    def window(k):
        if kind == "rows":
            return pl.BlockSpec((None, tr, c), lambda ll, i, *ch: (ll, ch[k][0] * nrt + i, 0))
        return pl.BlockSpec((None, tr, c), lambda ll, i, *ch: (ll, i, ch[k][0]))

    def body(chip0_ref, chip1_ref, own0, own1, got0, got1):
        return own0[...].astype(F32) + got0[...].astype(F32), own1[...].astype(F32) + got1[...].astype(F32)

    plain = pl.BlockSpec((None, tr, c), lambda ll, i, *ch: (ll, i, 0))
    return tilecall(body, name, (l, r // tr),
                    [(own, window(0)), (own, window(1))] +
                    [(got, pl.BlockSpec((None, None, tr, c), lambda ll, i, *ch, k=k: (k, ll, i, 0))) for k in range(2)],
                    [(_sds((l, r, c)), plain, None), (_sds((l, r, c), _MXU), plain, None)],
                    ("parallel", "parallel"), prefetch=tuple(chips))


def add_second_axis(mine, got, name):
    l, r, c = mine.shape
    tr = _stack_tile(r, c)
    blk = pl.BlockSpec((None, tr, c), lambda ll, i: (ll, i, 0))
    return tilecall(lambda a_ref, b_ref: (a_ref[...] + b_ref[...].astype(F32),), name, (l, r // tr),
                    [(mine, blk), (got, blk)], [(_sds((l, r, c)), blk, None)], ("parallel", "parallel"))[0]


def _adam_update(g, w, m, v):
    m_new = ADAM_B1 * m + (1.0 - ADAM_B1) * g
    v_new = ADAM_B2 * v + (1.0 - ADAM_B2) * jnp.square(g)
    m_hat = m_new / (1.0 - ADAM_B1 ** ADAM_STEP)
    v_hat = v_new / (1.0 - ADAM_B2 ** ADAM_STEP)
    delta = -ADAM_LR * (m_hat / (jnp.sqrt(v_hat) + ADAM_EPS) + ADAM_WD * w)
    return g, delta, m_new, v_new


def adamw_stacked(mine, theirs, w, m, v, core, name):
    L, r, c = w.shape
    hl = L // 2
    tr = _stack_tile(r, c)

    def body(core_ref, a_ref, b_ref, w_ref, m_ref, v_ref):
        is_mine = (pl.program_id(0) // hl) == core_ref[0]
        g = jnp.where(is_mine, a_ref[...], b_ref[...])
        return _adam_update(g, w_ref[...], m_ref[...], v_ref[...])

    full = pl.BlockSpec((None, tr, c), lambda l, i, core: (l, i, 0))
    out = (_sds((L, r, c)), full, None)
    return tilecall(body, name, (L, r // tr),
                    [(mine, pl.BlockSpec((None, tr, c), lambda l, i, core: (l % hl, i, 0))),
                     (theirs, pl.BlockSpec((None, None, tr, c), lambda l, i, core: (0, l % hl, i, 0))),
                     (w, full), (m, full), (v, full)],
                    [out, out, out, out], ("parallel", "parallel"), prefetch=(core,))


def _pack(arrs, dtype, row_multiple):
    flat = jnp.concatenate([a.reshape(-1).astype(dtype) for a in arrs])
    rows = -(-flat.shape[0] // LANES)
    rows = -(-rows // row_multiple) * row_multiple
    return jnp.pad(flat, (0, rows * LANES - flat.shape[0])).reshape(rows, LANES)


def _unpack(buf, shapes):
    flat = buf.reshape(-1)
    out, off = [], 0
    for s in shapes:
        n = int(np.prod(s))
        out.append(flat[off:off + n].reshape(s))
        off += n
    return out


def adamw_packed(gparts, w, m, v, name):
    P, R, _ = gparts.shape

    def body(g_ref, w_ref, m_ref, v_ref):
        g = g_ref[0]
        for p in range(1, P):
            g = g + g_ref[p]
        return _adam_update(g, w_ref[...], m_ref[...], v_ref[...])

    whole = pl.BlockSpec((R, LANES), lambda: (0, 0))
    out = (_sds((R, LANES)), whole, None)
    return tilecall(body, name, (), [(gparts, pl.BlockSpec((P, R, LANES), lambda: (0, 0, 0))), (w, whole), (m, whole), (v, whole)],
                    [out, out, out, out], ())


def _residual_out(a, w, x, next_gain):
    if next_gain is None:
        return mm(a, w, "nn", "mm_nn_add", add=x), None
    return mm(a, w, "nn", "mm_nn_add_rms", add=x, rms_gain=next_gain)


def _input_grad(pieces, x, gain, res):
    dh = None
    for d, w in pieces[:-1]:
        dh = mm(d, w, "nt", "mm_nt" if dh is None else "mm_nt_add", add=dh)
    d, w = pieces[-1]
    return mm(d, w, "nt", "mm_nt_norm_bwd", add=dh, norm_bwd=(x, gain, res))


def _ffn_fwd(x, h, next_gain, layer, gain, wts, cw8, cb):
    u = mm(h, win(wts["ffn_w_up"], layer), "nn", "mm_nn", out_dtype=_MXU)
    a = ffn_act_fwd(u, cw8, cb, "ffn_act_fwd")
    y, h_next = _residual_out(a, win(wts["ffn_w_down"], layer), x, next_gain)
    return y, h_next, (x, h, u, a)


def _ffn_bwd(dy, saved, layer, gain, wts, cw8, cb, grads):
    x, h, u, a = saved
    grads["ffn_w_down"] = mm(a, dy, "tn", "mm_tn_into", into=win(grads["ffn_w_down"], layer))
    da = mm(dy, win(wts["ffn_w_down"], layer), "nt", "mm_nt", out_dtype=_MXU)
    dyg, dyu, dcw_g, dcw_u, dcb_g, dcb_u = ffn_act_bwd(u, cw8, cb, da, "ffn_act_bwd")
    pieces = []
    for half, du in enumerate(ffn_conv_bwd(dyg, dyu, cw8, "ffn_conv_bwd")):
        cols = dict(col_off=half * D_FF, cols=D_FF)
        grads["ffn_w_up"] = mm(h, du, "tn", "mm_tn_into", into=win(grads["ffn_w_up"], layer, **cols))
        pieces.append((du, win(wts["ffn_w_up"], layer, **cols)))
    dx, d_gain = _input_grad(pieces, x, gain, dy)
    return dx, dict(gain=d_gain, conv_w=jnp.concatenate([dcw_g[0:3], dcw_u[0:3]], axis=1),
                    conv_b=jnp.concatenate([dcb_g, dcb_u], axis=1))


def _hgrn_w_in(wts, j, k):
    return win(wts["hgrn_w_in"], j, row_off=k * D_MODEL, rows=D_MODEL)


def _hgrn_layer_fwd(x, h, next_gain, j, gain, wts, lb, out_gain, mall):
    z = [mm(h, _hgrn_w_in(wts, j, k), "nn", "mm_nn") for k in range(4)]
    o, on, states = hgrn_fwd(z[0], z[1], z[2], z[3], lb, out_gain, mall, "hgrn_fwd")
    y, h_next = _residual_out(on, win(wts["hgrn_w_out"], j), x, next_gain)
    return y, h_next, (x, h, z, o, states, on)


def _hgrn_layer_bwd(dy, saved, j, gain, wts, lb, out_gain, mall, grads):
    x, h, z, o, states, on = saved
    grads["hgrn_w_out"] = mm(on, dy, "tn", "mm_tn_into", into=win(grads["hgrn_w_out"], j))
    don = mm(dy, win(wts["hgrn_w_out"], j), "nt", "mm_nt")
    dzq, dzf, dzi, dzg, dlb, d_out_gain = hgrn_bwd(z[0], z[1], z[2], z[3], lb, out_gain, mall, states, o, don, "hgrn_bwd")
    dz = [dzq, dzf, dzi, dzg]
    for k, d in enumerate(dz):
        grads["hgrn_w_in"] = mm(h, d, "tn", "mm_tn_into", into=win(grads["hgrn_w_in"], j, row_off=k * D_MODEL, rows=D_MODEL))
    dx, d_gain = _input_grad([(d, _hgrn_w_in(wts, j, k)) for k, d in enumerate(dz)], x, gain, dy)
    return dx, dict(gain=d_gain, lb=dlb, out_gain=d_out_gain)


_MLA_IN_WINDOWS = ((0, MLA_LORA), (MLA_LORA, MLA_LORA), (2 * MLA_LORA, HEAD_DIM))


def _mla_layer_fwd(x, h, next_gain, j, gain, wts, qa_gain, kva_gain, qn, kn, cos_t, sin_t):
    (c0, n), (c1, n1), (c2, n2) = _MLA_IN_WINDOWS
    cq, cqn = mm(h, win(wts["mla_w_in"], j, col_off=c0, cols=n), "nn", "mm_nn_rms", rms_gain=qa_gain)
    ckv, ckvn = mm(h, win(wts["mla_w_in"], j, col_off=c1, cols=n1), "nn", "mm_nn_rms", rms_gain=kva_gain)
    kr = mm(h, win(wts["mla_w_in"], j, col_off=c2, cols=n2), "nn", "mm_nn")
    qslots = mm(cqn, win(wts["mla_w_q_up"], j), "nn", "mm_nn")
    kv = mm(ckvn, win(wts["mla_w_kv_up"], j), "nn", "mm_nn")
    qr, krot = qk_fwd(qslots, kv, kr, qn, kn, cos_t, sin_t, "qk_fwd")
    o, lse = attn_fwd(qr, krot, kv, "attn_fwd")
    y, h_next = _residual_out(o, win(wts["mla_w_out"], j), x, next_gain)
    return y, h_next, (x, h, cq, ckv, kr, cqn, ckvn, qslots, kv, qr, krot, o, lse)


def _mla_layer_bwd(dy, saved, j, gain, wts, qa_gain, kva_gain, qn, kn, cos_t, sin_t, grads):
    x, h, cq, ckv, kr, cqn, ckvn, qslots, kv, qr, krot, o, lse = saved
    grads["mla_w_out"] = mm(o, dy, "tn", "mm_tn_into", into=win(grads["mla_w_out"], j))
    do = mm(dy, win(wts["mla_w_out"], j), "nt", "mm_nt")
    dq, dk, dv = attn_bwd(qr, krot, kv, o, lse, do, "attn_bwd")
    dqslots, dkv, dkr, d_qn, d_kn = qk_bwd(qslots, kv, kr, qn, kn, cos_t, sin_t, dq, dk, dv, "qk_bwd")
    grads["mla_w_q_up"] = mm(cqn, dqslots, "tn", "mm_tn_into", into=win(grads["mla_w_q_up"], j))
    dcq, d_qa = _input_grad([(dqslots, win(wts["mla_w_q_up"], j))], cq, qa_gain, None)
    grads["mla_w_kv_up"] = mm(ckvn, dkv, "tn", "mm_tn_into", into=win(grads["mla_w_kv_up"], j))
    dckv, d_kva = _input_grad([(dkv, win(wts["mla_w_kv_up"], j))], ckv, kva_gain, None)
    pieces = []
    for d, (c0, n) in zip((dcq, dckv, dkr), _MLA_IN_WINDOWS):
        grads["mla_w_in"] = mm(h, d, "tn", "mm_tn_into", into=win(grads["mla_w_in"], j, col_off=c0, cols=n))
        pieces.append((d, win(wts["mla_w_in"], j, col_off=c0, cols=n)))
    dx, d_gain = _input_grad(pieces, x, gain, dy)
    return dx, dict(gain=d_gain, qa=d_qa, kva=d_kva, qn=d_qn, kn=d_kn)


BIG = (("hgrn_w_in", "rows"), ("hgrn_w_out", "rows"), ("mla_w_in", "rows"), ("mla_w_q_up", "cols"),
       ("mla_w_kv_up", "cols"), ("mla_w_out", "rows"), ("ffn_w_up", "cols"), ("ffn_w_down", "rows"))
SMALL_SHARDED = (("mla_q_a_norm", 1), ("mla_kv_a_norm", 1), ("ffn_conv_w", 2))
REPLICATED = ("norm_mix", "norm_ffn", "hgrn_lower_bounds", "hgrn_out_norm", "mla_q_norm", "mla_k_norm", "ffn_conv_b")
WEIGHTS = ("norm_mix", "norm_ffn", "hgrn_w_in", "hgrn_lower_bounds", "hgrn_out_norm", "hgrn_w_out", "mla_w_in",
           "mla_q_a_norm", "mla_w_q_up", "mla_kv_a_norm", "mla_w_kv_up", "mla_q_norm", "mla_k_norm", "mla_w_out",
           "ffn_w_up", "ffn_conv_w", "ffn_conv_b", "ffn_w_down")


def _pad_cols(a, width):
    return jnp.pad(a, [(0, 0)] * (a.ndim - 1) + [(0, width - a.shape[-1])])


def _head_slots(w):
    lead, n = w.shape[:-1], w.shape[-1] // MLA_QK
    return _pad_cols(w.reshape(lead + (n, MLA_QK)), MLA_SLOT).reshape(lead + (n * MLA_SLOT,))


def _head_unslots(w):
    lead, n = w.shape[:-1], w.shape[-1] // MLA_SLOT
    return w.reshape(lead + (n, MLA_SLOT))[..., :MLA_QK].reshape(lead + (n * MLA_QK,))


def _to_stack_layout(name, a):
    if name == "hgrn_w_in":
        return a
    if name == "mla_w_in":
        return _pad_cols(a, MLA_IN_COLS)
    if name == "mla_w_q_up":
        return _head_slots(a)
    return a


def _from_stack_layout(name, a):
    if name == "mla_w_in":
        return a[..., :2 * MLA_LORA + MLA_ROPE]
    if name == "mla_w_q_up":
        return _head_unslots(a)
    return a


def _rope_tables(positions):
    inv_freq = ROPE_THETA ** (-jnp.arange(0, MLA_ROPE, 2, dtype=F32) / MLA_ROPE)
    ang = positions.astype(F32)[:, None] * inv_freq
    cos, sin = jnp.cos(ang), jnp.sin(ang)
    S = positions.shape[0]
    ones, zeros = jnp.ones((S, MLA_NOPE), F32), jnp.zeros((S, MLA_SLOT - MLA_QK), F32)
    return (jnp.concatenate([ones, cos, cos, zeros], axis=1),
            jnp.concatenate([jnp.zeros((S, MLA_NOPE), F32), -sin, sin, zeros], axis=1))


def kernel(x, positions, norm_mix, norm_ffn, hgrn_w_in, hgrn_lower_bounds, hgrn_out_norm, hgrn_w_out, mla_w_in, mla_q_a_norm, mla_w_q_up, mla_kv_a_norm, mla_w_kv_up, mla_q_norm, mla_k_norm, mla_w_out, ffn_w_up, ffn_conv_w, ffn_conv_b, ffn_w_down, loss_target, m_norm_mix, m_norm_ffn, m_hgrn_w_in, m_hgrn_lower_bounds, m_hgrn_out_norm, m_hgrn_w_out, m_mla_w_in, m_mla_q_a_norm, m_mla_w_q_up, m_mla_kv_a_norm, m_mla_w_kv_up, m_mla_q_norm, m_mla_k_norm, m_mla_w_out, m_ffn_w_up, m_ffn_conv_w, m_ffn_conv_b, m_ffn_w_down, v_norm_mix, v_norm_ffn, v_hgrn_w_in, v_hgrn_lower_bounds, v_hgrn_out_norm, v_hgrn_w_out, v_mla_w_in, v_mla_q_a_norm, v_mla_w_q_up, v_mla_kv_a_norm, v_mla_w_kv_up, v_mla_q_norm, v_mla_k_norm, v_mla_w_out, v_ffn_w_up, v_ffn_conv_w, v_ffn_conv_b, v_ffn_w_down):
    args = dict(locals())
    w = {n: args[n] for n in WEIGHTS}
    m = {n: args["m_" + n] for n in WEIGHTS}
    v = {n: args["v_" + n] for n in WEIGHTS}
    depth = norm_mix.shape[0]
    x0 = x[0]
    S = x0.shape[0]
    chip = (2 * lax.axis_index("x") + lax.axis_index("y")).astype(jnp.int32).reshape(1)
    core = lax.axis_index("c").astype(jnp.int32).reshape(1)
    big_names = [n for n, _ in BIG]
    kinds = [k for _, k in BIG]
    small_names = [n for n, _ in SMALL_SHARDED]
    small_axis = dict(SMALL_SHARDED)

    local = {n: _to_stack_layout(n, w[n]) for n in big_names}
    gathered = gather_big([local[n].astype(_MXU) for n in big_names], kinds, "gather_weights")
    wts = {n: place(g, local[n], k, chip, "place") for n, k, g in zip(big_names, kinds, gathered)}
    (got_small,) = exchange([_pack([w[n] for n in small_names], F32, SUBLANES)], "chips", [None], "gather_small")
    per_chip = [_unpack(got_small[p], [w[n].shape for n in small_names]) for p in range(N_CHIPS)]
    small = {n: jnp.concatenate([per_chip[p][k] for p in range(N_CHIPS)], axis=small_axis[n]) for k, n in enumerate(small_names)}

    cos_t, sin_t = _rope_tables(positions[0])
    lbs = lower_bound_fwd(hgrn_lower_bounds, "lower_bound_fwd")
    mall = jnp.asarray(_hgrn_sum_matrix(min(HGRN_CHUNK, S)), _MXU)
    qn = _pad_cols(mla_q_norm, MLA_SLOT)
    kn = _pad_cols(mla_k_norm, MLA_SLOT)
    cw8 = jnp.pad(small["ffn_conv_w"], ((0, 0), (0, SUBLANES - 3), (0, 0)))

    def mixer_args(layer):
        j = layer // 2
        if layer % 2 == 0:
            return (j, norm_mix[layer:layer + 1], wts, lbs[j:j + 1], hgrn_out_norm[j:j + 1], mall)
        return (j, norm_mix[layer:layer + 1], wts, small["mla_q_a_norm"][j:j + 1], small["mla_kv_a_norm"][j:j + 1],
                qn[j:j + 1], kn[j:j + 1], cos_t, sin_t)

    def ffn_args(layer):
        return (layer, norm_ffn[layer:layer + 1], wts, cw8[layer], ffn_conv_b[layer:layer + 1])

    xc, h = x0, rms_fwd(x0, norm_mix[0:1], "rms_fwd")
    saved = []
    for layer in range(depth):
        fwd = _hgrn_layer_fwd if layer % 2 == 0 else _mla_layer_fwd
        xc, h, s_mix = fwd(xc, h, norm_ffn[layer:layer + 1], *mixer_args(layer))
        xc, h, s_ffn = _ffn_fwd(xc, h, norm_mix[layer + 1:layer + 2] if layer + 1 < depth else None, *ffn_args(layer))
        saved.append((s_mix, s_ffn))

    dh, loss_blk = loss_head(xc, loss_target[0], "loss_head")
    loss = lax.psum(loss_blk[0, 0], MESH_AXES)

    grads = {n: lax.empty(g.shape, F32) for n, g in zip(big_names, gathered)}
    g_mix, g_ffn = [None] * depth, [None] * depth
    for layer in reversed(range(depth)):
        s_mix, s_ffn = saved[layer]
        dh, g_ffn[layer] = _ffn_bwd(dh, s_ffn, *ffn_args(layer), grads)
        bwd = _hgrn_layer_bwd if layer % 2 == 0 else _mla_layer_bwd
        dh, g_mix[layer] = bwd(dh, s_mix, *mixer_args(layer), grads)
    hg = [g_mix[l] for l in range(0, depth, 2)]
    mg = [g_mix[l] for l in range(1, depth, 2)]
    d_p = lower_bound_bwd(hgrn_lower_bounds, hg[1]["lb"], "lower_bound_bwd")

    g_list = [grads[n] for n in big_names]
    from_core = send_other_half(g_list, "reduce_cores_in")
    chip_sums = [add_cores(g, o, core, "add_cores") for g, o in zip(g_list, from_core)]
    cx, cy, cc = lax.axis_index("x"), lax.axis_index("y"), lax.axis_index("c")
    second = 2 * (cx + cc - 2 * cx * cc) + (cy + (1 - cc) - 2 * cy * (1 - cc))
    chips = (chip, second.astype(jnp.int32).reshape(1))
    from_first = reduce_first_axis(chip_sums, kinds, "reduce_axis_1")
    partial = [add_first_axis(own, got, k, chips, "add_axis_1") for own, got, k in zip(chip_sums, from_first, kinds)]
    from_second = reduce_second_axis([p[1] for p in partial], "reduce_axis_2")
    reduced = [add_second_axis(p[0], got, "add_axis_2") for p, got in zip(partial, from_second)]
    other_half = exchange(reduced, "sibling", [None] * len(reduced), "reduce_cores_out", keep_own=False)
    big_out = {}
    for n, mine, theirs in zip(big_names, reduced, other_half):
        outs = adamw_stacked(mine, theirs, local[n], _to_stack_layout(n, m[n]), _to_stack_layout(n, v[n]), core, "adamw")
        big_out[n] = [_from_stack_layout(n, o) for o in outs]

    small_grads = {
        "norm_mix": jnp.concatenate([g["gain"] for g in g_mix], axis=0),
        "norm_ffn": jnp.concatenate([g["gain"] for g in g_ffn], axis=0),
        "hgrn_lower_bounds": d_p[0:2],
        "hgrn_out_norm": jnp.concatenate([g["out_gain"] for g in hg], axis=0),
        "mla_q_a_norm": jnp.concatenate([g["qa"] for g in mg], axis=0),
        "mla_kv_a_norm": jnp.concatenate([g["kva"] for g in mg], axis=0),
        "mla_q_norm": jnp.concatenate([g["qn"][:, :MLA_QK] for g in mg], axis=0),
        "mla_k_norm": jnp.concatenate([g["kn"][:, :MLA_QK] for g in mg], axis=0),
        "ffn_conv_w": jnp.stack([g["conv_w"] for g in g_ffn]),
        "ffn_conv_b": jnp.concatenate([g["conv_b"] for g in g_ffn], axis=0),
    }

    def chip_part(n, p):
        size = w[n].shape[small_axis[n]]
        return lax.slice_in_dim(small_grads[n], p * size, (p + 1) * size, axis=small_axis[n])

    to_chips = jnp.stack([_pack([chip_part(n, p) for n in small_names], F32, SUBLANES) for p in range(N_CHIPS)])
    rep_g, shard_g = exchange([_pack([small_grads[n] for n in REPLICATED], F32, SUBLANES), to_chips], "all",
                              [None, (2, 1, 0)], "reduce_small")
    small_out = {}
    for names, gparts in ((REPLICATED, rep_g), (small_names, shard_g)):
        packed = adamw_packed(gparts, *[_pack([t[n] for n in names], F32, SUBLANES) for t in (w, m, v)], "adamw_small")
        unpacked = [_unpack(buf, [w[n].shape for n in names]) for buf in packed]
        for k, n in enumerate(names):
            small_out[n] = [u[k] for u in unpacked]

    result = [loss, dh[None]]
    for k in range(4):
        result += [(big_out[n] if n in big_out else small_out[n])[k] for n in WEIGHTS]
    return tuple(result)
```

```python
import numpy as np
import jax
import jax.numpy as jnp
from jax import lax
from jax.experimental import pallas as pl
from jax.experimental.pallas import tpu as pltpu

F32 = jnp.float32
BF16 = jnp.bfloat16
_MXU = BF16

RMS_EPS = 1e-6
D_MODEL = 1024
HEADS = 8
HEAD_DIM = 128
HGRN_CHUNK = 128
MLA_NOPE = 128
MLA_ROPE = 64
MLA_QK = MLA_NOPE + MLA_ROPE
MLA_SLOT = 256
MLA_LORA = 256
MLA_IN_COLS = 2 * MLA_LORA + HEAD_DIM
ROPE_THETA = 10000.0
D_FF = 2816
FF_BLOCK = 1408
LANES = 128
SUBLANES = 8

ADAM_LR = 0.001
ADAM_B1 = 0.9
ADAM_B2 = 0.999
ADAM_EPS = 1e-08
ADAM_WD = 0.01
ADAM_STEP = 10

VMEM_LIMIT = 56 * 1024 * 1024
MM_VMEM_BUDGET = 46 * 1024 * 1024
MESH_AXES = ("x", "y", "c")
N_CHIPS = 4

_NN = ((1,), (0,))
_NT = ((1,), (1,))
_TN = ((0,), (0,))


def _dg(a, b, dims):
    return lax.dot_general(a.astype(_MXU), b.astype(_MXU), (dims, ((), ())), preferred_element_type=F32)


@jax.custom_vjp
def kdot(a, b):
    return _dg(a, b, _NN)


kdot.defvjp(lambda a, b: (_dg(a, b, _NN), (a, b)), lambda r, g: (_dg(g, r[1], _NT), _dg(r[0], g, _TN)))


@jax.custom_vjp
def kdot_nt(a, b):
    return _dg(a, b, _NT)


kdot_nt.defvjp(lambda a, b: (_dg(a, b, _NT), (a, b)), lambda r, g: (_dg(g, r[1], _NN), _dg(g, r[0], _TN)))


@jax.custom_vjp
def kdot_tn(a, b):
    return _dg(a, b, _TN)


kdot_tn.defvjp(lambda a, b: (_dg(a, b, _TN), (a, b)), lambda r, g: (_dg(r[1], g, _NT), _dg(r[0], g, _NN)))


def _pick(d, prefs):
    for p in prefs:
        if d >= p and d % p == 0:
            return p
    return d


def _params(sem):
    return pltpu.CompilerParams(dimension_semantics=sem, vmem_limit_bytes=VMEM_LIMIT)


def _sds(shape, dtype=F32):
    return jax.ShapeDtypeStruct(shape, dtype)


def win(arr, layer, row_off=0, col_off=0, rows=None, cols=None):
    return (arr, layer, row_off, col_off, rows or arr.shape[1] - row_off, cols or arr.shape[2] - col_off)


def mm(a, b, mode, name, add=None, out_dtype=F32, into=None, rms_gain=None, norm_bwd=None):
    if isinstance(b, tuple):
        b_arr, b_layer, b_r0, b_c0, b_rows, b_cols = b
    else:
        b_arr, b_layer, b_r0, b_c0, (b_rows, b_cols) = b, None, 0, 0, b.shape
    if mode == "nn":
        (M, K), (K2, N) = a.shape, (b_rows, b_cols)
    elif mode == "nt":
        (M, K), (N, K2) = a.shape, (b_rows, b_cols)
    else:
        (K, M), (K2, N) = a.shape, (b_rows, b_cols)
    assert K == K2, (name, a.shape, b_rows, b_cols)
    tn = N if N <= 1024 else _pick(N, (1024, 1408, 512, 256, 128))
    tk = K if K <= 2048 else _pick(K, (2048, 2816, 1024, 512, 256, 128))
    nk = K // tk
    a_bytes = jnp.dtype(a.dtype).itemsize
    b_bytes = jnp.dtype(b_arr.dtype).itemsize
    extra_tiles = (add is not None) + (rms_gain is not None) / 2 + (3 if norm_bwd is not None else 0)

    def vmem_bytes(tm_):
        tiles = 2 * (tm_ * tk * a_bytes + tk * tn * b_bytes) + tm_ * tn * 4 * (2 + 2 * extra_tiles + (nk > 1))
        return tiles

    tm = M
    if M > 1024:
        fits = [t for t in (2048, 1024, 1408, 512, 256, 128) if M % t == 0 and vmem_bytes(t) <= MM_VMEM_BUDGET]
        tm = fits[0] if fits else _pick(M, (128,))
    dims = {"nn": _NN, "nt": _NT, "tn": _TN}[mode]
    a_spec = pl.BlockSpec((tk, tm), lambda i, j, k: (k, i)) if mode == "tn" else pl.BlockSpec((tm, tk), lambda i, j, k: (i, k))
    b_blk = (tn, tk) if mode == "nt" else (tk, tn)
    assert b_r0 % b_blk[0] == 0 and b_c0 % b_blk[1] == 0, (name, b_r0, b_c0, b_blk)
    br, bc = b_r0 // b_blk[0], b_c0 // b_blk[1]
    if mode == "nt":
        b_idx = lambda i, j, k: (br + j, bc + k)
    else:
        b_idx = lambda i, j, k: (br + k, bc + j)
    if b_layer is None:
        b_spec = pl.BlockSpec(b_blk, b_idx)
    else:
        b_spec = pl.BlockSpec((None,) + b_blk, lambda i, j, k: (b_layer,) + b_idx(i, j, k))
    plain = pl.BlockSpec((tm, tn), lambda i, j, k: (i, j))
    has_add, has_rms, has_nb = add is not None, rms_gain is not None, norm_bwd is not None
    assert not (has_rms or has_nb) or (tn == N and into is None and not (has_rms and has_nb)), name
    vec = pl.BlockSpec((1, tn), lambda i, j, k: (0, j))
    ins = [a, b_arr] + ([add] if has_add else []) + ([rms_gain] if has_rms else [])
    specs = [a_spec, b_spec] + ([plain] if has_add else []) + ([vec] if has_rms else [])
    n_in = len(ins)
    if has_nb:
        nb_x, nb_gain, nb_res = norm_bwd
        ins += [nb_x, nb_gain] + ([nb_res] if nb_res is not None else [])
        specs += [plain, vec] + ([plain] if nb_res is not None else [])
    aliases = {}
    if into is None:
        o_spec, out_shape = plain, _sds((M, N), out_dtype)
    else:
        buf, o_layer, o_r0, o_c0, o_rows, o_cols = into
        assert (o_rows, o_cols) == (M, N) and o_r0 % tm == 0 and o_c0 % tn == 0, (name, into[1:], M, N, tm, tn)
        orow, ocol = o_r0 // tm, o_c0 // tn
        o_spec = pl.BlockSpec((None, tm, tn), lambda i, j, k: (o_layer, orow + i, ocol + j))
        out_shape = _sds(buf.shape, buf.dtype)
        aliases = {len(ins): 0}
        ins.append(buf)
        specs.append(pl.BlockSpec(memory_space=pl.ANY))

    n_all_in = len(ins)

    def kern(*refs):
        a_ref, b_ref = refs[0], refs[1]
        add_ref = refs[2] if has_add else None
        o_ref = refs[n_all_in]

        def finish(r):
            if has_add:
                r = r + add_ref[...].astype(F32)
            if has_nb:
                _, vjp = jax.vjp(lambda xv, gv: _rms(xv, gv, N), refs[n_in][...], refs[n_in + 1][...])
                dx, dgain = vjp(r)
                o_ref[...] = dx if nb_res is None else dx + refs[n_in + 2][...]
                _store(refs[n_all_in + 1], dgain, pl.program_id(0) == 0)
                return
            o_ref[...] = r.astype(o_ref.dtype)
            if has_rms:
                refs[n_all_in + 1][...] = _rms(r, refs[n_in - 1][...], N).astype(_MXU)

        if nk == 1:
            finish(_dg(a_ref[...], b_ref[...], dims))
            return
        acc = refs[-1]
        k = pl.program_id(2)

        @pl.when(k == 0)
        def _():
            acc[...] = jnp.zeros_like(acc)

        acc[...] += _dg(a_ref[...], b_ref[...], dims)

        @pl.when(k == nk - 1)
        def _():
            finish(acc[...])

    if has_rms:
        o_spec, out_shape = [o_spec, plain], [out_shape, _sds((M, N), _MXU)]
    if has_nb:
        o_spec, out_shape = [o_spec, vec], [out_shape, _sds((1, N))]
    return pl.pallas_call(
        kern, name=name, grid=(M // tm, N // tn, nk), in_specs=specs, out_specs=o_spec, out_shape=out_shape,
        scratch_shapes=[pltpu.VMEM((tm, tn), F32)] if nk > 1 else [], input_output_aliases=aliases,
        compiler_params=_params(("arbitrary" if has_nb else "parallel", "parallel", "arbitrary")))(*ins)


def _store(ref, val, first):
    if first is None:
        ref[...] = val.astype(ref.dtype)
        return

    @pl.when(first)
    def _():
        ref[...] = val.astype(ref.dtype)

    @pl.when(jnp.logical_not(first))
    def _():
        ref[...] += val.astype(ref.dtype)


def tilecall(body, name, grid, ins, outs, sem, prefetch=(), aliases=None):
    n_pre, n_in = len(prefetch), len(ins)

    def kern(*refs):
        vals = body(*refs[:n_pre + n_in])
        for ref, val, (_, _, first) in zip(refs[n_pre + n_in:], vals, outs):
            _store(ref, val, None if first is None else first())

    in_specs, out_specs = [s for _, s in ins], [s for _, s, _ in outs]
    kwargs = dict(name=name, out_shape=[sh for sh, _, _ in outs], compiler_params=_params(sem),
                  input_output_aliases={n_pre + k: v for k, v in (aliases or {}).items()})
    if n_pre:
        kwargs["grid_spec"] = pltpu.PrefetchScalarGridSpec(num_scalar_prefetch=n_pre, grid=grid, in_specs=in_specs,
                                                           out_specs=out_specs)
    else:
        kwargs.update(grid=grid, in_specs=in_specs, out_specs=out_specs)
    return pl.pallas_call(kern, **kwargs)(*prefetch, *[a for a, _ in ins])


def _rms(x, g, n):
    ms = jnp.sum(x * x, axis=-1, keepdims=True) / n
    return x * lax.rsqrt(ms + RMS_EPS) * g


def _row_tile(S, w):
    return min(S, 512 if w <= 1024 else 256)


def rms_fwd(x, g, name, col=0, w=None):
    S = x.shape[0]
    w = w or x.shape[1]
    ts = _row_tile(S, w)
    return tilecall(
        lambda x_ref, g_ref: (_rms(x_ref[...], g_ref[...], w),), name, (S // ts,),
        [(x, pl.BlockSpec((ts, w), lambda i: (i, col))), (g, pl.BlockSpec((1, w), lambda i: (0, 0)))],
        [(_sds((S, w), _MXU), pl.BlockSpec((ts, w), lambda i: (i, 0)), None)], ("parallel",))[0]


FF_ROWS = 512
HALO_ROWS = 16


def _shifted(u, halo_ref, is_first):
    rid = lax.broadcasted_iota(jnp.int32, (SUBLANES, 1), 0)
    halo = halo_ref[...].astype(F32)
    hrow = lax.broadcasted_iota(jnp.int32, (HALO_ROWS, 1), 0)

    def halo_row(r):
        return jnp.where(is_first, 0.0, jnp.sum(jnp.where(hrow == r, halo, 0.0), axis=0, keepdims=True))

    h7, h6 = halo_row(HALO_ROWS - 1), halo_row(HALO_ROWS - 2)
    r1, r2 = pltpu.roll(u, 1, 0), pltpu.roll(u, 2, 0)
    top1 = jnp.where(rid == 0, h7, r1[:SUBLANES])
    top2 = jnp.where(rid == 0, h6, jnp.where(rid == 1, h7, r2[:SUBLANES]))
    return jnp.concatenate([top1, r1[SUBLANES:]], axis=0), jnp.concatenate([top2, r2[SUBLANES:]], axis=0)


def _conv(u, u1, u2, cw_ref, cb_ref):
    return ((cb_ref[...] + u2 * cw_ref[0:1, :]) + u1 * cw_ref[1:2, :]) + u * cw_ref[2:3, :]


def _ffn_specs(S, ts, jmap):
    hb = ts // HALO_ROWS
    return (pl.BlockSpec((ts, FF_BLOCK), lambda j, i: (i, jmap(j))),
            pl.BlockSpec((HALO_ROWS, FF_BLOCK), lambda j, i: (jnp.maximum(i * hb - 1, 0), jmap(j))),
            pl.BlockSpec((SUBLANES, FF_BLOCK), lambda j, i: (0, jmap(j))),
            pl.BlockSpec((1, FF_BLOCK), lambda j, i: (0, jmap(j))))


def ffn_act_fwd(u, cw8, cb, name):
    S = u.shape[0]
    ts = min(S, FF_ROWS)
    nb = D_FF // FF_BLOCK

    def body(ug, hg, cwg, cbg, uu, hu, cwu, cbu):
        first = pl.program_id(1) == 0
        g = ug[...].astype(F32)
        g1, g2 = _shifted(g, hg, first)
        yg = _conv(g, g1, g2, cwg, cbg)
        v = uu[...].astype(F32)
        v1, v2 = _shifted(v, hu, first)
        yu = _conv(v, v1, v2, cwu, cbu)
        return (yg * jax.nn.sigmoid(yg) * yu,)

    sg = _ffn_specs(S, ts, lambda j: j)
    su = _ffn_specs(S, ts, lambda j: j + nb)
    ins = [(u, sg[0]), (u, sg[1]), (cw8, sg[2]), (cb, sg[3]), (u, su[0]), (u, su[1]), (cw8, su[2]), (cb, su[3])]
    return tilecall(body, name, (nb, S // ts), ins,
                    [(_sds((S, D_FF), _MXU), pl.BlockSpec((ts, FF_BLOCK), lambda j, i: (i, j)), None)],
                    ("parallel", "parallel"))[0]


def ffn_act_bwd(u, cw8, cb, da, name):
    S = u.shape[0]
    ts = min(S, FF_ROWS)
    nb = D_FF // FF_BLOCK

    def taps(dy, x, x1, x2):
        return jnp.concatenate(
            [jnp.sum(dy * x2, axis=0, keepdims=True), jnp.sum(dy * x1, axis=0, keepdims=True),
             jnp.sum(dy * x, axis=0, keepdims=True), jnp.zeros((SUBLANES - 3, dy.shape[1]), F32)], axis=0)

    def body(ug, hg, cwg, cbg, uu, hu, cwu, cbu, da_ref):
        first = pl.program_id(1) == 0
        g = ug[...].astype(F32)
        g1, g2 = _shifted(g, hg, first)
        yg = _conv(g, g1, g2, cwg, cbg)
        v = uu[...].astype(F32)
        v1, v2 = _shifted(v, hu, first)
        yu = _conv(v, v1, v2, cwu, cbu)
        d = da_ref[...].astype(F32)
        sg = jax.nn.sigmoid(yg)
        dyg = d * yu * (sg * (1.0 + yg * (1.0 - sg)))
        dyu = d * (yg * sg)
        return (dyg, dyu, taps(dyg, g, g1, g2), taps(dyu, v, v1, v2),
                jnp.sum(dyg, axis=0, keepdims=True), jnp.sum(dyu, axis=0, keepdims=True))

    sg_ = _ffn_specs(S, ts, lambda j: j)
    su_ = _ffn_specs(S, ts, lambda j: j + nb)
    row = pl.BlockSpec((ts, FF_BLOCK), lambda j, i: (i, j))
    ins = [(u, sg_[0]), (u, sg_[1]), (cw8, sg_[2]), (cb, sg_[3]), (u, su_[0]), (u, su_[1]), (cw8, su_[2]), (cb, su_[3]), (da, row)]
    first_row = lambda: pl.program_id(1) == 0
    dy, dcw, dcb = (_sds((S, D_FF)), row, None), (_sds((SUBLANES, D_FF)), sg_[2], first_row), (_sds((1, D_FF)), sg_[3], first_row)
    return tilecall(body, name, (nb, S // ts), ins, [dy, dy, dcw, dcw, dcb, dcb], ("parallel", "arbitrary"))


def ffn_conv_bwd(dyg, dyu, cw8, name):
    S = dyg.shape[0]
    ts = min(S, FF_ROWS)
    hb = ts // SUBLANES
    nrow = S // ts
    nb = D_FF // FF_BLOCK

    def back(dy_ref, halo_ref, cw_ref):
        last = pl.program_id(1) == nrow - 1
        d = dy_ref[...]
        rid = lax.broadcasted_iota(jnp.int32, (SUBLANES, 1), 0)
        n0 = jnp.where(last, 0.0, halo_ref[0:1, :])
        n1 = jnp.where(last, 0.0, halo_ref[1:2, :])
        r1, r2 = pltpu.roll(d, ts - 1, 0), pltpu.roll(d, ts - 2, 0)
        end1 = jnp.where(rid == SUBLANES - 1, n0, r1[ts - SUBLANES:])
        end2 = jnp.where(rid == SUBLANES - 1, n1, jnp.where(rid == SUBLANES - 2, n0, r2[ts - SUBLANES:]))
        d1 = jnp.concatenate([r1[:ts - SUBLANES], end1], axis=0)
        d2 = jnp.concatenate([r2[:ts - SUBLANES], end2], axis=0)
        return d * cw_ref[2:3, :] + d1 * cw_ref[1:2, :] + d2 * cw_ref[0:1, :]

    row = pl.BlockSpec((ts, FF_BLOCK), lambda j, i: (i, j))
    halo = pl.BlockSpec((SUBLANES, FF_BLOCK), lambda j, i: (jnp.minimum((i + 1) * hb, S // SUBLANES - 1), j))
    ins = [(dyg, row), (dyg, halo), (cw8, pl.BlockSpec((SUBLANES, FF_BLOCK), lambda j, i: (0, j))),
           (dyu, row), (dyu, halo), (cw8, pl.BlockSpec((SUBLANES, FF_BLOCK), lambda j, i: (0, j + nb)))]
    out = (_sds((S, D_FF), _MXU), row, None)
    return tilecall(lambda a, b, c, d, e, f: (back(a, b, c), back(d, e, f)), name, (nb, nrow), ins, [out, out],
                    ("parallel", "parallel"))


def _hgrn_levels(C):
    out, m = [], C // 2
    while m >= 1:
        out.append(m)
        m //= 2
    return out


def _hgrn_sum_matrix(C):
    t = np.arange(C)[:, None]
    u = np.arange(C)[None, :]
    blocks = [u <= t, u > t]
    for m in _hgrn_levels(C):
        r = (t // (2 * m)) * (2 * m) + m
        right = (t % (2 * m)) >= m
        blocks.append((right & (u > r) & (u <= t)) | ((~right) & (u > t) & (u <= r)))
    return np.concatenate(blocks, axis=0).astype(np.float32)


def _make_partial_sums(nb, C):
    @jax.custom_vjp
    def sums(mall, lf):
        hi = lf.astype(_MXU)
        mid = (lf - hi.astype(F32)).astype(_MXU)
        e2 = _dg(mall, jnp.concatenate([hi, mid], axis=1), _NN)
        e = e2[:, :HEAD_DIM] + e2[:, HEAD_DIM:]
        return tuple(e[b * C:(b + 1) * C] for b in range(nb))

    def fwd(mall, lf):
        return sums(mall, lf), mall

    def bwd(mall, gs):
        return jnp.zeros_like(mall), _dg(mall, jnp.concatenate(gs, axis=0), _TN)

    sums.defvjp(fwd, bwd)
    return sums


def _hgrn_chunk(zq, zf, v, lb, st, mall, C):
    levels = _hgrn_levels(C)
    qs = zq * jax.nn.sigmoid(zq)
    fg = lb + (1.0 - lb) * jax.nn.sigmoid(zf)
    k = 1.0 - fg
    e = _make_partial_sums(2 + len(levels), C)(mall, jnp.log(fg))
    g_incl, g_after = e[0], e[1]
    rid = lax.broadcasted_iota(jnp.int32, (C, 1), 0)
    tt = lax.broadcasted_iota(jnp.int32, (C, C), 0)
    ss = lax.broadcasted_iota(jnp.int32, (C, C), 1)
    o = kdot_nt(qs * jnp.exp(g_incl), st)
    o = o + jnp.sum(qs * k, axis=-1, keepdims=True) * v
    scores = jnp.zeros((C, C), F32)
    for li, m in enumerate(levels):
        sh = int(np.log2(m))
        right = ((rid >> sh) & 1) == 1
        both = jnp.where(right, qs, k) * jnp.exp(e[2 + li])
        pair = ((tt >> (sh + 1)) == (ss >> (sh + 1))) & (((tt >> sh) & 1) == 1) & (((ss >> sh) & 1) == 0)
        scores = scores + jnp.where(pair, kdot_nt(both, both), 0.0)
    o = o + kdot(scores, v)
    g_last = jnp.sum(jnp.where(rid == C - 1, g_incl, 0.0), axis=0, keepdims=True)
    st_new = st * jnp.exp(g_last) + kdot_tn(v, k * jnp.exp(g_after))
    return o, st_new


HGRN_HEADS_PER_STEP = 8
_HGRN_LANES = HGRN_HEADS_PER_STEP * HEAD_DIM


def _hgrn_in_specs(C, nc, rev):
    cm = (lambda c: nc - 1 - c) if rev else (lambda c: c)
    blk = lambda: pl.BlockSpec((C, _HGRN_LANES), lambda h, c: (cm(c), h))
    return cm, [blk(), blk(), blk(), pl.BlockSpec((1, _HGRN_LANES), lambda h, c: (0, h))]


def _hgrn_state_spec(cm):
    return pl.BlockSpec((HGRN_HEADS_PER_STEP, None, HEAD_DIM, HEAD_DIM), lambda h, c: (h, cm(c), 0, 0))


def _hgrn_out(o, g, gain):
    return _rms(o, gain, HEAD_DIM) * (g * jax.nn.sigmoid(g))


def hgrn_fwd(zq, zf, zi, zg, lb, out_gain, mall, name):
    S = zq.shape[0]
    C = min(HGRN_CHUNK, S)
    nc = S // C

    def kern(zq_ref, zf_ref, zi_ref, lb_ref, zg_ref, gain_ref, mall_ref, o_ref, on_ref, st_ref, st):
        @pl.when(pl.program_id(1) == 0)
        def _():
            st[...] = jnp.zeros_like(st)

        mall_v = mall_ref[...]
        for g in range(HGRN_HEADS_PER_STEP):
            lanes = slice(g * HEAD_DIM, (g + 1) * HEAD_DIM)
            s_in = st[g]
            st_ref[g] = s_in
            o, s_new = _hgrn_chunk(zq_ref[:, lanes], zf_ref[:, lanes], zi_ref[:, lanes], lb_ref[:, lanes], s_in, mall_v, C)
            o_ref[:, lanes] = o
            on_ref[:, lanes] = _hgrn_out(o, zg_ref[:, lanes], gain_ref[...]).astype(on_ref.dtype)
            st[g] = s_new

    cm, specs = _hgrn_in_specs(C, nc, False)
    row = pl.BlockSpec((C, _HGRN_LANES), lambda h, c: (c, h))
    return pl.pallas_call(
        kern, name=name, grid=(HEADS // HGRN_HEADS_PER_STEP, nc),
        in_specs=specs + [row, pl.BlockSpec((1, HEAD_DIM), lambda h, c: (0, 0)), pl.BlockSpec(mall.shape, lambda h, c: (0, 0))],
        out_specs=[row, row, _hgrn_state_spec(cm)],
        out_shape=[_sds((S, D_MODEL)), _sds((S, D_MODEL), _MXU), _sds((HEADS, nc, HEAD_DIM, HEAD_DIM))],
        scratch_shapes=[pltpu.VMEM((HGRN_HEADS_PER_STEP, HEAD_DIM, HEAD_DIM), F32)],
        compiler_params=_params(("parallel", "arbitrary")))(zq, zf, zi, lb, zg, out_gain, mall)


def hgrn_bwd(zq, zf, zi, zg, lb, out_gain, mall, states, o, don, name):
    S = zq.shape[0]
    C = min(HGRN_CHUNK, S)
    nc = S // C

    def kern(zq_ref, zf_ref, zi_ref, lb_ref, zg_ref, gain_ref, mall_ref, st_ref, o_ref, don_ref,
             dq_ref, df_ref, di_ref, dg_ref, dlb_ref, dgain_ref, dst):
        first = pl.program_id(1) == 0

        @pl.when(first)
        def _():
            dst[...] = jnp.zeros_like(dst)

        mall_v = mall_ref[...]
        gls, dgain = [], None
        for g in range(HGRN_HEADS_PER_STEP):
            lanes = slice(g * HEAD_DIM, (g + 1) * HEAD_DIM)
            _, out_vjp = jax.vjp(_hgrn_out, o_ref[:, lanes], zg_ref[:, lanes], gain_ref[...])
            do, dzg, dgn = out_vjp(don_ref[:, lanes])
            dg_ref[:, lanes] = dzg.astype(dg_ref.dtype)
            dgain = dgn if dgain is None else dgain + dgn
            _, vjp = jax.vjp(lambda a, b, c, d, e: _hgrn_chunk(a, b, c, d, e, mall_v, C),
                             zq_ref[:, lanes], zf_ref[:, lanes], zi_ref[:, lanes], lb_ref[:, lanes], st_ref[g])
            ga, gb, gv, gl, gs = vjp((do, dst[g]))
            dq_ref[:, lanes] = ga.astype(dq_ref.dtype)
            df_ref[:, lanes] = gb.astype(df_ref.dtype)
            di_ref[:, lanes] = gv.astype(di_ref.dtype)
            gls.append(gl)
            dst[g] = gs
        _store(dlb_ref, jnp.concatenate(gls, axis=1), first)
        _store(dgain_ref, dgain, first & (pl.program_id(0) == 0))

    cm, specs = _hgrn_in_specs(C, nc, True)
    row = lambda: pl.BlockSpec((C, _HGRN_LANES), lambda h, c: (cm(c), h))
    vec = pl.BlockSpec((1, HEAD_DIM), lambda h, c: (0, 0))
    wide = _sds((S, D_MODEL), _MXU)
    return pl.pallas_call(
        kern, name=name, grid=(HEADS // HGRN_HEADS_PER_STEP, nc),
        in_specs=specs + [row(), vec, pl.BlockSpec(mall.shape, lambda h, c: (0, 0)), _hgrn_state_spec(cm), row(), row()],
        out_specs=[row(), row(), row(), row(), pl.BlockSpec((1, _HGRN_LANES), lambda h, c: (0, h)), vec],
        out_shape=[wide, wide, wide, wide, _sds((1, D_MODEL)), _sds((1, HEAD_DIM))],
        scratch_shapes=[pltpu.VMEM((HGRN_HEADS_PER_STEP, HEAD_DIM, HEAD_DIM), F32)],
        compiler_params=_params(("arbitrary", "arbitrary")))(zq, zf, zi, lb, zg, out_gain, mall, states, o, don)


def _lb_soft(p0, p1):
    mx = jnp.maximum(p0, p1)
    e0, e1 = jnp.exp(p0 - mx), jnp.exp(p1 - mx)
    s0, s1 = e0 / (e0 + e1), e1 / (e0 + e1)
    return (s0 + s1) - s0


def lower_bound_fwd(p, name):
    assert p.shape[0] == 2

    def body(p_ref):
        s = _lb_soft(p_ref[0:1, :], p_ref[1:2, :])
        return (jnp.concatenate([jnp.zeros_like(s), s] + [jnp.zeros_like(s)] * (SUBLANES - 2), axis=0),)

    spec8 = pl.BlockSpec((SUBLANES, p.shape[1]), lambda: (0, 0))
    return tilecall(body, name, (), [(p, pl.BlockSpec(p.shape, lambda: (0, 0)))], [(_sds((SUBLANES, p.shape[1])), spec8, None)], ())[0]


def lower_bound_bwd(p, dlb1, name):
    def body(p_ref, d_ref):
        _, vjp = jax.vjp(_lb_soft, p_ref[0:1, :], p_ref[1:2, :])
        g0, g1 = vjp(d_ref[...])
        return (jnp.concatenate([g0, g1] + [jnp.zeros_like(g0)] * (SUBLANES - 2), axis=0),)

    spec8 = pl.BlockSpec((SUBLANES, p.shape[1]), lambda: (0, 0))
    return tilecall(body, name, (), [(p, pl.BlockSpec(p.shape, lambda: (0, 0))), (dlb1, pl.BlockSpec(dlb1.shape, lambda: (0, 0)))],
                    [(_sds((SUBLANES, p.shape[1])), spec8, None)], ())[0]


@jax.custom_vjp
def _swap_rope_halves(x):
    lane = lax.broadcasted_iota(jnp.int32, x.shape, 1)
    lo = (lane >= MLA_NOPE) & (lane < MLA_NOPE + MLA_ROPE // 2)
    hi = (lane >= MLA_NOPE + MLA_ROPE // 2) & (lane < MLA_QK)
    return jnp.where(lo, pltpu.roll(x, MLA_SLOT - MLA_ROPE // 2, 1), jnp.where(hi, pltpu.roll(x, MLA_ROPE // 2, 1), 0.0))


_swap_rope_halves.defvjp(lambda x: (_swap_rope_halves(x), None), lambda _, g: (_swap_rope_halves(g),))


def _norm_rope(x, gain, cos_t, sin_t):
    y = _rms(x, gain, MLA_QK)
    return y * cos_t + _swap_rope_halves(y) * sin_t


_ATTN_SCALE = MLA_QK ** -0.5


def _qk_heads(qs, kn, kr, qn, kn_gain, cos_t, sin_t):
    q = _norm_rope(qs, qn, cos_t, sin_t) * _ATTN_SCALE
    k = _norm_rope(jnp.concatenate([kn, kr], axis=1), kn_gain, cos_t, sin_t)
    return q, k


QK_ROWS = 512
QK_HEADS_PER_STEP = 4


def _qk_specs(ts):
    G = QK_HEADS_PER_STEP
    slot = pl.BlockSpec((ts, G * MLA_SLOT), lambda i, h: (i, h))
    shared = pl.BlockSpec((ts, HEAD_DIM), lambda i, h: (i, 0))
    gain = pl.BlockSpec((1, MLA_SLOT), lambda i, h: (0, 0))
    table = pl.BlockSpec((ts, MLA_SLOT), lambda i, h: (i, 0))
    return slot, shared, gain, table


def _head_slot(g):
    return slice(g * MLA_SLOT, (g + 1) * MLA_SLOT)


def _head_nope(g):
    return slice(g * MLA_SLOT, g * MLA_SLOT + HEAD_DIM)


def qk_fwd(qslots, kv, krope, qn, kn, cos_t, sin_t, name):
    S = qslots.shape[0]
    ts = min(S, QK_ROWS)
    slot, shared, gain, table = _qk_specs(ts)

    def body(q_ref, kv_ref, kr_ref, qn_ref, kg_ref, c_ref, s_ref):
        kr, qg, kg, c, s = kr_ref[...], qn_ref[...], kg_ref[...], c_ref[...], s_ref[...]
        heads = [_qk_heads(q_ref[:, _head_slot(g)], kv_ref[:, _head_nope(g)], kr, qg, kg, c, s)
                 for g in range(QK_HEADS_PER_STEP)]
        return jnp.concatenate([h[0] for h in heads], axis=1), jnp.concatenate([h[1] for h in heads], axis=1)

    out = _sds((S, HEADS * MLA_SLOT), _MXU)
    return tilecall(body, name, (S // ts, HEADS // QK_HEADS_PER_STEP),
                    [(qslots, slot), (kv, slot), (krope, shared), (qn, gain), (kn, gain), (cos_t, table), (sin_t, table)],
                    [(out, slot, None), (out, slot, None)], ("parallel", "parallel"))


def qk_bwd(qslots, kv, krope, qn, kn, cos_t, sin_t, dq, dk, dv, name):
    S = qslots.shape[0]
    ts = min(S, QK_ROWS)
    slot, shared, gain, table = _qk_specs(ts)
    vblk = pl.BlockSpec((ts, QK_HEADS_PER_STEP * HEAD_DIM), lambda i, h: (i, h))

    def body(q_ref, kv_ref, kr_ref, qn_ref, kg_ref, c_ref, s_ref, dq_ref, dk_ref, dv_ref):
        kr, qg, kg, c, s = kr_ref[...], qn_ref[...], kg_ref[...], c_ref[...], s_ref[...]
        dqs, dkvs, gr, g1, g2 = [], [], None, None, None
        for g in range(QK_HEADS_PER_STEP):
            _, vjp = jax.vjp(lambda a, b, r, w1, w2: _qk_heads(a, b, r, w1, w2, c, s),
                             q_ref[:, _head_slot(g)], kv_ref[:, _head_nope(g)], kr, qg, kg)
            ga, gb, r_, a_, b_ = vjp((dq_ref[:, _head_slot(g)], dk_ref[:, _head_slot(g)]))
            dqs.append(ga)
            dkvs += [gb, dv_ref[:, g * HEAD_DIM:(g + 1) * HEAD_DIM]]
            gr, g1, g2 = (r_, a_, b_) if gr is None else (gr + r_, g1 + a_, g2 + b_)
        return jnp.concatenate(dqs, axis=1), jnp.concatenate(dkvs, axis=1), gr, g1, g2

    first_head = lambda: pl.program_id(1) == 0
    first = lambda: (pl.program_id(0) == 0) & (pl.program_id(1) == 0)
    wide = _sds((S, HEADS * MLA_SLOT), _MXU)
    return tilecall(body, name, (S // ts, HEADS // QK_HEADS_PER_STEP),
                    [(qslots, slot), (kv, slot), (krope, shared), (qn, gain), (kn, gain), (cos_t, table), (sin_t, table),
                     (dq, slot), (dk, slot), (dv, vblk)],
                    [(wide, slot, None), (wide, slot, None), (_sds((S, HEAD_DIM)), shared, first_head),
                     (_sds((1, MLA_SLOT)), gain, first), (_sds((1, MLA_SLOT)), gain, first)], ("arbitrary", "arbitrary"))


ATTN_TILE_FWD = 1024
ATTN_TILE = 1024
ATTN_HEADS_PER_STEP = 4
ATTN_HEADS_PER_STEP_BWD = 1


def _causal_pairs(nq, by_row):
    pairs = [(i, j) for i in range(nq) for j in range(i + 1)] if by_row else [(i, j) for j in range(nq) for i in range(j, nq)]
    return jnp.asarray([p[0] for p in pairs], jnp.int32), jnp.asarray([p[1] for p in pairs], jnp.int32)


def _diag_mask(s, tq):
    rows = lax.broadcasted_iota(jnp.int32, (tq, tq), 0)
    cols = lax.broadcasted_iota(jnp.int32, (tq, tq), 1)
    return jnp.where(rows >= cols, s, -jnp.inf)


def attn_fwd(qr, kr, kv, name):
    S = qr.shape[0]
    tq = min(S, ATTN_TILE_FWD)
    nq = S // tq
    i_tab, j_tab = _causal_pairs(nq, True)

    G = ATTN_HEADS_PER_STEP
    reps = tq // LANES

    def kern(it, jt, q_ref, k_ref, kv_ref, o_ref, lse_ref, m_s, l_s, acc):
        n = pl.program_id(1)
        i, j = it[n], jt[n]

        @pl.when(j == 0)
        def _():
            m_s[...] = jnp.full_like(m_s, -jnp.inf)
            l_s[...] = jnp.zeros_like(l_s)
            acc[...] = jnp.zeros_like(acc)

        def step(diagonal):
            for g in range(G):
                slot = slice(g * MLA_SLOT, (g + 1) * MLA_SLOT)
                s = _dg(q_ref[:, slot], k_ref[:, slot], _NT)
                if diagonal:
                    s = _diag_mask(s, tq)
                m_prev = m_s[g]
                m_new = jnp.maximum(m_prev, jnp.max(s, axis=-1, keepdims=True))
                alpha = jnp.exp(m_prev - m_new)
                p = jnp.exp(s - jnp.tile(m_new, (1, reps)))
                l_s[g] = alpha * l_s[g] + jnp.sum(p, axis=-1, keepdims=True)
                acc[g] = alpha * acc[g] + _dg(p, kv_ref[:, g * MLA_SLOT + HEAD_DIM:(g + 1) * MLA_SLOT], _NN)
                m_s[g] = m_new

        @pl.when(j < i)
        def _():
            step(False)

        @pl.when(j == i)
        def _():
            step(True)
            for g in range(G):
                l = l_s[g]
                lanes = slice(g * HEAD_DIM, (g + 1) * HEAD_DIM)
                o_ref[:, lanes] = acc[g] / l
                lse_ref[:, lanes] = m_s[g] + jnp.log(l)

    out = _sds((S, HEADS * HEAD_DIM))
    oblk = pl.BlockSpec((tq, G * HEAD_DIM), lambda h, n, it, jt: (it[n], h))
    stat = pltpu.VMEM((G, tq, HEAD_DIM), F32)
    return pl.pallas_call(
        kern, name=name,
        grid_spec=pltpu.PrefetchScalarGridSpec(
            num_scalar_prefetch=2, grid=(HEADS // G, i_tab.shape[0]),
            in_specs=[pl.BlockSpec((tq, G * MLA_SLOT), lambda h, n, it, jt: (it[n], h)),
                      pl.BlockSpec((tq, G * MLA_SLOT), lambda h, n, it, jt: (jt[n], h)),
                      pl.BlockSpec((tq, G * MLA_SLOT), lambda h, n, it, jt: (jt[n], h))],
            out_specs=[oblk, oblk], scratch_shapes=[stat, stat, stat]),
        out_shape=[out, out], compiler_params=_params(("parallel", "arbitrary")))(i_tab, j_tab, qr, kr, kv)


def attn_bwd(qr, kr, kv, o, lse, do, name):
    S = qr.shape[0]
    tq = min(S, ATTN_TILE)
    nq = S // tq
    i_tab, j_tab = _causal_pairs(nq, False)

    G = ATTN_HEADS_PER_STEP_BWD

    def kern(it, jt, q_ref, k_ref, kv_ref, o_ref, lse_ref, do_ref, dq_ref, dk_ref, dv_ref, dk_acc, dv_acc):
        n = pl.program_id(1)
        i, j = it[n], jt[n]

        @pl.when(n == 0)
        def _():
            dq_ref[...] = jnp.zeros_like(dq_ref)

        @pl.when(i == j)
        def _():
            dk_acc[...] = jnp.zeros_like(dk_acc)
            dv_acc[...] = jnp.zeros_like(dv_acc)

        def step(diagonal):
            rows = pl.ds(pl.multiple_of(i * tq, tq), tq)
            for g in range(G):
                slot = slice(g * MLA_SLOT, (g + 1) * MLA_SLOT)
                lanes = slice(g * HEAD_DIM, (g + 1) * HEAD_DIM)
                q, k = q_ref[:, slot], k_ref[:, slot]
                s = _dg(q, k, _NT) - jnp.tile(lse_ref[:, lanes], (1, tq // LANES))
                if diagonal:
                    s = _diag_mask(s, tq)
                p = jnp.exp(s)
                d = do_ref[:, lanes]
                delta = jnp.sum(d * o_ref[:, lanes], axis=-1, keepdims=True)
                dv_acc[:, lanes] += _dg(p, d, _TN)
                ds = p * (_dg(d, kv_ref[:, g * MLA_SLOT + HEAD_DIM:(g + 1) * MLA_SLOT], _NT) - delta)
                dk_acc[:, slot] += _dg(ds, q, _TN)
                dq_ref[rows, slot] += _dg(ds, k, _NN)

        @pl.when(i > j)
        def _():
            step(False)

        @pl.when(i == j)
        def _():
            step(True)

        @pl.when(i == nq - 1)
        def _():
            dk_ref[...] = dk_acc[...]
            dv_ref[...] = dv_acc[...]

    qblk = pl.BlockSpec((tq, G * MLA_SLOT), lambda h, n, it, jt: (it[n], h))
    oblk = pl.BlockSpec((tq, G * HEAD_DIM), lambda h, n, it, jt: (it[n], h))
    kblk = pl.BlockSpec((tq, G * MLA_SLOT), lambda h, n, it, jt: (jt[n], h))
    return pl.pallas_call(
        kern, name=name,
        grid_spec=pltpu.PrefetchScalarGridSpec(
            num_scalar_prefetch=2, grid=(HEADS // G, i_tab.shape[0]),
            in_specs=[qblk, kblk, kblk, oblk, oblk, oblk],
            out_specs=[pl.BlockSpec((S, G * MLA_SLOT), lambda h, n, it, jt: (0, h)), kblk,
                       pl.BlockSpec((tq, G * HEAD_DIM), lambda h, n, it, jt: (jt[n], h))],
            scratch_shapes=[pltpu.VMEM((tq, G * MLA_SLOT), F32), pltpu.VMEM((tq, G * HEAD_DIM), F32)]),
        out_shape=[_sds((S, HEADS * MLA_SLOT)), _sds((S, HEADS * MLA_SLOT)), _sds((S, HEADS * HEAD_DIM))],
        compiler_params=_params(("parallel", "arbitrary")))(i_tab, j_tab, qr, kr, kv, o, lse, do)


def loss_head(y, target, name):
    S, Dm = y.shape
    ts = _row_tile(S, Dm)

    def body(y_ref, t_ref):
        e = y_ref[...] - t_ref[...]
        tot = jnp.sum(jnp.sum(e * e, axis=-1, keepdims=True) / Dm, axis=0, keepdims=True)
        return e / Dm, jnp.broadcast_to(0.5 * tot, (SUBLANES, LANES))

    row = pl.BlockSpec((ts, Dm), lambda i: (i, 0))
    return tilecall(body, name, (S // ts,), [(y, row), (target, row)],
                    [(_sds((S, Dm)), row, None),
                     (_sds((SUBLANES, LANES)), pl.BlockSpec((SUBLANES, LANES), lambda i: (0, 0)), lambda: pl.program_id(0) == 0)],
                    ("arbitrary",))


_PEER_FLIPS = {
    "chips": ((1, 0, 0), (0, 1, 0), (1, 1, 0)),
    "sibling": ((0, 0, 1),),
    "all": tuple((a, b, c) for a in (0, 1) for b in (0, 1) for c in (0, 1))[1:],
}
_SLOT_WEIGHTS = {"chips": (2, 1, 0), "sibling": (0, 0, 1), "all": (4, 2, 1)}
_HBM = pl.BlockSpec(memory_space=pltpu.HBM)


def _me():
    return lax.axis_index("x"), lax.axis_index("y"), lax.axis_index("c")


def _remote(src, dst, send_sem, recv_sem, peer):
    return pltpu.make_async_remote_copy(src_ref=src, dst_ref=dst, send_sem=send_sem, recv_sem=recv_sem,
                                        device_id=peer, device_id_type=pl.DeviceIdType.MESH)


def exchange(arrs, group, slab_weights, name, keep_own=True):
    flips = _PEER_FLIPS[group]
    wx, wy, wc = _SLOT_WEIGHTS[group]
    n_slots = len(flips) + (1 if keep_own else 0)
    n = len(arrs)

    def slab(ref, a, pos):
        w = slab_weights[a]
        return ref if w is None else ref.at[w[0] * pos[0] + w[1] * pos[1] + w[2] * pos[2]]

    def kern(*refs):
        srcs, outs = refs[:n], refs[n:2 * n]
        send_sems, recv_sems = refs[2 * n:2 * n + 2]
        me = _me()
        my_slot = wx * me[0] + wy * me[1] + wc * me[2]
        copies = []
        if keep_own:
            local_sems = refs[2 * n + 2]
            for a in range(n):
                cp = pltpu.make_async_copy(slab(srcs[a], a, me), outs[a].at[my_slot], local_sems.at[a])
                cp.start()
                copies.append(cp)
        for f, flip in enumerate(flips):
            peer = tuple(m ^ b if b else m for m, b in zip(me, flip))
            for a in range(n):
                cp = _remote(slab(srcs[a], a, peer), outs[a].at[my_slot if keep_own else f],
                             send_sems.at[f, a], recv_sems.at[f, a], peer)
                cp.start()
                copies.append(cp)
        for cp in copies:
            cp.wait()

    out_shape = [_sds((n_slots,) + (a.shape if slab_weights[k] is None else a.shape[1:]), a.dtype) for k, a in enumerate(arrs)]
    sems = [pltpu.SemaphoreType.DMA((len(flips), n)), pltpu.SemaphoreType.DMA((len(flips), n))]
    return pl.pallas_call(
        kern, name=name, in_specs=[_HBM] * n, out_specs=[_HBM] * n, out_shape=out_shape,
        scratch_shapes=sems + ([pltpu.SemaphoreType.DMA((n,))] if keep_own else []))(*arrs)


def _chip_window(ref, kind, size, chip, layers):
    if kind == "rows":
        return ref.at[layers, pl.ds(chip * size, size), :]
    return ref.at[layers, :, pl.ds(pl.multiple_of(chip * size, LANES), size)]


def gather_big(shards, kinds, name):
    n = len(shards)
    fulls = []
    for s, kind in zip(shards, kinds):
        L, r, c = s.shape
        fulls.append(_sds((L, N_CHIPS * r, c) if kind == "rows" else (L, r, N_CHIPS * c), s.dtype))

    def kern(*refs):
        srcs, outs = refs[:n], refs[n:2 * n]
        ici_s, ici_r, relay_s, relay_r, d2d_s, d2d_r = refs[2 * n:]
        x, y, c = _me()
        sibling = (x, y, 1 - c)
        nbrs = ((x ^ 1, y), (x, y ^ 1))
        relay_from = (x ^ (1 - c), y ^ c)
        relay_to = (x ^ c, y ^ (1 - c), c)
        diagonal = (x ^ 1, y ^ 1)

        def window(a, chip):
            L, r, cc = shards[a].shape
            size = r if kinds[a] == "rows" else cc
            return _chip_window(outs[a], kinds[a], size, 2 * chip[0] + chip[1], pl.ds(c * (L // 2), L // 2))

        def direct(a, f):
            L = shards[a].shape[0]
            return _remote(srcs[a].at[pl.ds(c * (L // 2), L // 2)], window(a, (x, y)), ici_s.at[a, f], ici_r.at[a, f],
                           (nbrs[f][0], nbrs[f][1], c))

        def to_sibling(a, k, chip):
            return _remote(window(a, chip), window(a, chip), d2d_s.at[a, k], d2d_r.at[a, k], sibling)

        sends, relays, passed = [], [], []
        for a in range(n):
            for f in range(2):
                cp = direct(a, f)
                cp.start()
                sends.append(cp)
        for a in range(n):
            _remote(window(a, relay_from), window(a, relay_from), ici_s.at[a, c], ici_r.at[a, c], relay_to).wait_recv()
            cp = _remote(window(a, relay_from), window(a, relay_from), relay_s.at[a], relay_r.at[a], relay_to)
            cp.start()
            relays.append(cp)
            passed.append(to_sibling(a, 0, relay_from))
            passed[-1].start()
            _remote(window(a, relay_from), window(a, relay_from), ici_s.at[a, 1 - c], ici_r.at[a, 1 - c], relay_to).wait_recv()
            passed.append(to_sibling(a, 1, (relay_to[0], relay_to[1])))
            passed[-1].start()
        for a in range(n):
            relays[a].wait_recv()
            passed.append(to_sibling(a, 2, diagonal))
            passed[-1].start()
        for cp in sends + relays:
            cp.wait_send()
        for cp in passed:
            cp.wait()

    pair = pltpu.SemaphoreType.DMA((n, 2))
    one = pltpu.SemaphoreType.DMA((n,))
    three = pltpu.SemaphoreType.DMA((n, 3))
    return pl.pallas_call(kern, name=name, in_specs=[_HBM] * n, out_specs=[_HBM] * n, out_shape=fulls,
                          scratch_shapes=[pair, pair, one, one, three, three])(*shards)


def send_other_half(arrs, name):
    n = len(arrs)

    def kern(*refs):
        srcs, outs = refs[:n], refs[n:2 * n]
        send_sems, recv_sems = refs[2 * n:]
        x, y, c = _me()
        copies = []
        for a in range(n):
            hl = arrs[a].shape[0] // 2
            cp = _remote(srcs[a].at[pl.ds((1 - c) * hl, hl)], outs[a], send_sems.at[a], recv_sems.at[a], (x, y, 1 - c))
            cp.start()
            copies.append(cp)
        for cp in copies:
            cp.wait()

    return pl.pallas_call(
        kern, name=name, in_specs=[_HBM] * n, out_specs=[_HBM] * n,
        out_shape=[_sds((a.shape[0] // 2,) + a.shape[1:], a.dtype) for a in arrs],
        scratch_shapes=[pltpu.SemaphoreType.DMA((n,)), pltpu.SemaphoreType.DMA((n,))])(*arrs)


def _axis_neighbours():
    x, y, c = _me()
    return (x, y, c), (x ^ (1 - c), y ^ c), (x ^ c, y ^ (1 - c)), (x ^ 1, y ^ 1)


def reduce_first_axis(arrs, kinds, name):
    n = len(arrs)
    shapes = []
    for a, kind in zip(arrs, kinds):
        l, R, C = a.shape
        shapes.append((l, R // N_CHIPS, C) if kind == "rows" else (l, R, C // N_CHIPS))

    def kern(*refs):
        srcs, outs = refs[:n], refs[n:2 * n]
        send_sems, recv_sems = refs[2 * n:]
        (x, y, c), first, second, diagonal = _axis_neighbours()
        copies = []
        for a in range(n):
            size = shapes[a][1] if kinds[a] == "rows" else shapes[a][2]
            for k, chip in enumerate((first, diagonal)):
                window = _chip_window(srcs[a], kinds[a], size, 2 * chip[0] + chip[1], slice(None))
                cp = _remote(window, outs[a].at[k], send_sems.at[a, k], recv_sems.at[a, k], (first[0], first[1], c))
                cp.start()
                copies.append(cp)
        for cp in copies:
            cp.wait()

    sem = pltpu.SemaphoreType.DMA((n, 2))
    return pl.pallas_call(
        kern, name=name, in_specs=[_HBM] * n, out_specs=[_HBM] * n,
        out_shape=[_sds((2,) + s, a.dtype) for s, a in zip(shapes, arrs)], scratch_shapes=[sem, sem])(*arrs)


def reduce_second_axis(arrs, name):
    n = len(arrs)

    def kern(*refs):
        srcs, outs = refs[:n], refs[n:2 * n]
        send_sems, recv_sems = refs[2 * n:]
        (x, y, c), first, second, diagonal = _axis_neighbours()
        copies = []
        for a in range(n):
            cp = _remote(srcs[a], outs[a], send_sems.at[a], recv_sems.at[a], (second[0], second[1], c))
            cp.start()
            copies.append(cp)
        for cp in copies:
            cp.wait()

    return pl.pallas_call(
        kern, name=name, in_specs=[_HBM] * n, out_specs=[_HBM] * n, out_shape=[_sds(a.shape, a.dtype) for a in arrs],
        scratch_shapes=[pltpu.SemaphoreType.DMA((n,)), pltpu.SemaphoreType.DMA((n,))])(*arrs)


def _stack_tile(r, c):
    for t in (1024, 704, 512, 352, 256, 128, 64, 32, 16):
        if r % t == 0 and t * c * 4 <= 3 * 512 * 1024:
            return t
    return r


def _window_map(kind, r, tr):
    nrt = r // tr
    if kind == "rows":
        return lambda l, i, chip: (l, chip[0] * nrt + i, 0)
    return lambda l, i, chip: (l, i, chip[0])


def place(full, shard, kind, chip, name):
    L, r, c = shard.shape
    tr = _stack_tile(r, c)
    wmap = _window_map(kind, r, tr)
    return tilecall(lambda chip_ref, s_ref, f_ref: (s_ref[...],), name, (L, r // tr),
                    [(shard, pl.BlockSpec((None, tr, c), lambda l, i, chip: (l, i, 0))), (full, pl.BlockSpec(memory_space=pl.ANY))],
                    [(_sds(full.shape, full.dtype), pl.BlockSpec((None, tr, c), lambda l, i, chip: wmap(l, i, chip)), None)],
                    ("parallel", "parallel"), prefetch=(chip,), aliases={1: 0})[0]


def add_cores(g, other, core, name):
    L, R, C = g.shape
    hl = L // 2
    tr = _stack_tile(R, C)
    blk = (None, tr, C)
    return tilecall(lambda core_ref, a_ref, b_ref: (a_ref[...] + b_ref[...],), name, (hl, R // tr),
                    [(g, pl.BlockSpec(blk, lambda l, i, core: (core[0] * hl + l, i, 0))),
                     (other, pl.BlockSpec(blk, lambda l, i, core: (l, i, 0)))],
                    [(_sds((hl, R, C), _MXU), pl.BlockSpec(blk, lambda l, i, core: (l, i, 0)), None)],
                    ("parallel", "parallel"), prefetch=(core,))[0]


def add_first_axis(own, got, kind, chips, name):
    _, l, r, c = got.shape
    tr = _stack_tile(r, c)
    nrt = r // tr

    def window(k):
        if kind == "rows":
            return pl.BlockSpec((None, tr, c), lambda ll, i, *ch: (ll, ch[k][0] * nrt + i, 0))
        return pl.BlockSpec((None, tr, c), lambda ll, i, *ch: (ll, i, ch[k][0]))

    def body(chip0_ref, chip1_ref, own0, own1, got0, got1):
        return own0[...].astype(F32) + got0[...].astype(F32), own1[...].astype(F32) + got1[...].astype(F32)

    plain = pl.BlockSpec((None, tr, c), lambda ll, i, *ch: (ll, i, 0))
    return tilecall(body, name, (l, r // tr),
                    [(own, window(0)), (own, window(1))] +
                    [(got, pl.BlockSpec((None, None, tr, c), lambda ll, i, *ch, k=k: (k, ll, i, 0))) for k in range(2)],
                    [(_sds((l, r, c)), plain, None), (_sds((l, r, c), _MXU), plain, None)],
                    ("parallel", "parallel"), prefetch=tuple(chips))


def add_second_axis(mine, got, name):
    l, r, c = mine.shape
    tr = _stack_tile(r, c)
    blk = pl.BlockSpec((None, tr, c), lambda ll, i: (ll, i, 0))
    return tilecall(lambda a_ref, b_ref: (a_ref[...] + b_ref[...].astype(F32),), name, (l, r // tr),
                    [(mine, blk), (got, blk)], [(_sds((l, r, c)), blk, None)], ("parallel", "parallel"))[0]


def _adam_update(g, w, m, v):
    m_new = ADAM_B1 * m + (1.0 - ADAM_B1) * g
    v_new = ADAM_B2 * v + (1.0 - ADAM_B2) * jnp.square(g)
    m_hat = m_new / (1.0 - ADAM_B1 ** ADAM_STEP)
    v_hat = v_new / (1.0 - ADAM_B2 ** ADAM_STEP)
    delta = -ADAM_LR * (m_hat / (jnp.sqrt(v_hat) + ADAM_EPS) + ADAM_WD * w)
    return g, delta, m_new, v_new


def adamw_stacked(mine, theirs, w, m, v, core, name):
    L, r, c = w.shape
    hl = L // 2
    tr = _stack_tile(r, c)

    def body(core_ref, a_ref, b_ref, w_ref, m_ref, v_ref):
        is_mine = (pl.program_id(0) // hl) == core_ref[0]
        g = jnp.where(is_mine, a_ref[...], b_ref[...])
        return _adam_update(g, w_ref[...], m_ref[...], v_ref[...])

    full = pl.BlockSpec((None, tr, c), lambda l, i, core: (l, i, 0))
    out = (_sds((L, r, c)), full, None)
    return tilecall(body, name, (L, r // tr),
                    [(mine, pl.BlockSpec((None, tr, c), lambda l, i, core: (l % hl, i, 0))),
                     (theirs, pl.BlockSpec((None, None, tr, c), lambda l, i, core: (0, l % hl, i, 0))),
                     (w, full), (m, full), (v, full)],
                    [out, out, out, out], ("parallel", "parallel"), prefetch=(core,))


def _pack(arrs, dtype, row_multiple):
    flat = jnp.concatenate([a.reshape(-1).astype(dtype) for a in arrs])
    rows = -(-flat.shape[0] // LANES)
    rows = -(-rows // row_multiple) * row_multiple
    return jnp.pad(flat, (0, rows * LANES - flat.shape[0])).reshape(rows, LANES)


def _unpack(buf, shapes):
    flat = buf.reshape(-1)
    out, off = [], 0
    for s in shapes:
        n = int(np.prod(s))
        out.append(flat[off:off + n].reshape(s))
        off += n
    return out


def adamw_packed(gparts, w, m, v, name):
    P, R, _ = gparts.shape

    def body(g_ref, w_ref, m_ref, v_ref):
        g = g_ref[0]
        for p in range(1, P):
            g = g + g_ref[p]
        return _adam_update(g, w_ref[...], m_ref[...], v_ref[...])

    whole = pl.BlockSpec((R, LANES), lambda: (0, 0))
    out = (_sds((R, LANES)), whole, None)
    return tilecall(body, name, (), [(gparts, pl.BlockSpec((P, R, LANES), lambda: (0, 0, 0))), (w, whole), (m, whole), (v, whole)],
                    [out, out, out, out], ())


def _residual_out(a, w, x, next_gain):
    if next_gain is None:
        return mm(a, w, "nn", "mm_nn_add", add=x), None
    return mm(a, w, "nn", "mm_nn_add_rms", add=x, rms_gain=next_gain)


def _input_grad(pieces, x, gain, res):
    dh = None
    for d, w in pieces[:-1]:
        dh = mm(d, w, "nt", "mm_nt" if dh is None else "mm_nt_add", add=dh)
    d, w = pieces[-1]
    return mm(d, w, "nt", "mm_nt_norm_bwd", add=dh, norm_bwd=(x, gain, res))


def _ffn_fwd(x, h, next_gain, layer, gain, wts, cw8, cb):
    u = mm(h, win(wts["ffn_w_up"], layer), "nn", "mm_nn", out_dtype=_MXU)
    a = ffn_act_fwd(u, cw8, cb, "ffn_act_fwd")
    y, h_next = _residual_out(a, win(wts["ffn_w_down"], layer), x, next_gain)
    return y, h_next, (x, h, u, a)


def _ffn_bwd(dy, saved, layer, gain, wts, cw8, cb, grads):
    x, h, u, a = saved
    grads["ffn_w_down"] = mm(a, dy, "tn", "mm_tn_into", into=win(grads["ffn_w_down"], layer))
    da = mm(dy, win(wts["ffn_w_down"], layer), "nt", "mm_nt", out_dtype=_MXU)
    dyg, dyu, dcw_g, dcw_u, dcb_g, dcb_u = ffn_act_bwd(u, cw8, cb, da, "ffn_act_bwd")
    pieces = []
    for half, du in enumerate(ffn_conv_bwd(dyg, dyu, cw8, "ffn_conv_bwd")):
        cols = dict(col_off=half * D_FF, cols=D_FF)
        grads["ffn_w_up"] = mm(h, du, "tn", "mm_tn_into", into=win(grads["ffn_w_up"], layer, **cols))
        pieces.append((du, win(wts["ffn_w_up"], layer, **cols)))
    dx, d_gain = _input_grad(pieces, x, gain, dy)
    return dx, dict(gain=d_gain, conv_w=jnp.concatenate([dcw_g[0:3], dcw_u[0:3]], axis=1),
                    conv_b=jnp.concatenate([dcb_g, dcb_u], axis=1))


def _hgrn_w_in(wts, j, k):
    return win(wts["hgrn_w_in"], j, row_off=k * D_MODEL, rows=D_MODEL)


def _hgrn_layer_fwd(x, h, next_gain, j, gain, wts, lb, out_gain, mall):
    z = [mm(h, _hgrn_w_in(wts, j, k), "nn", "mm_nn") for k in range(4)]
    o, on, states = hgrn_fwd(z[0], z[1], z[2], z[3], lb, out_gain, mall, "hgrn_fwd")
    y, h_next = _residual_out(on, win(wts["hgrn_w_out"], j), x, next_gain)
    return y, h_next, (x, h, z, o, states, on)


def _hgrn_layer_bwd(dy, saved, j, gain, wts, lb, out_gain, mall, grads):
    x, h, z, o, states, on = saved
    grads["hgrn_w_out"] = mm(on, dy, "tn", "mm_tn_into", into=win(grads["hgrn_w_out"], j))
    don = mm(dy, win(wts["hgrn_w_out"], j), "nt", "mm_nt")
    dzq, dzf, dzi, dzg, dlb, d_out_gain = hgrn_bwd(z[0], z[1], z[2], z[3], lb, out_gain, mall, states, o, don, "hgrn_bwd")
    dz = [dzq, dzf, dzi, dzg]
    for k, d in enumerate(dz):
        grads["hgrn_w_in"] = mm(h, d, "tn", "mm_tn_into", into=win(grads["hgrn_w_in"], j, row_off=k * D_MODEL, rows=D_MODEL))
    dx, d_gain = _input_grad([(d, _hgrn_w_in(wts, j, k)) for k, d in enumerate(dz)], x, gain, dy)
    return dx, dict(gain=d_gain, lb=dlb, out_gain=d_out_gain)


_MLA_IN_WINDOWS = ((0, MLA_LORA), (MLA_LORA, MLA_LORA), (2 * MLA_LORA, HEAD_DIM))


def _mla_layer_fwd(x, h, next_gain, j, gain, wts, qa_gain, kva_gain, qn, kn, cos_t, sin_t):
    (c0, n), (c1, n1), (c2, n2) = _MLA_IN_WINDOWS
    cq, cqn = mm(h, win(wts["mla_w_in"], j, col_off=c0, cols=n), "nn", "mm_nn_rms", rms_gain=qa_gain)
    ckv, ckvn = mm(h, win(wts["mla_w_in"], j, col_off=c1, cols=n1), "nn", "mm_nn_rms", rms_gain=kva_gain)
    kr = mm(h, win(wts["mla_w_in"], j, col_off=c2, cols=n2), "nn", "mm_nn")
    qslots = mm(cqn, win(wts["mla_w_q_up"], j), "nn", "mm_nn")
    kv = mm(ckvn, win(wts["mla_w_kv_up"], j), "nn", "mm_nn")
    qr, krot = qk_fwd(qslots, kv, kr, qn, kn, cos_t, sin_t, "qk_fwd")
    o, lse = attn_fwd(qr, krot, kv, "attn_fwd")
    y, h_next = _residual_out(o, win(wts["mla_w_out"], j), x, next_gain)
    return y, h_next, (x, h, cq, ckv, kr, cqn, ckvn, qslots, kv, qr, krot, o, lse)


def _mla_layer_bwd(dy, saved, j, gain, wts, qa_gain, kva_gain, qn, kn, cos_t, sin_t, grads):
    x, h, cq, ckv, kr, cqn, ckvn, qslots, kv, qr, krot, o, lse = saved
    grads["mla_w_out"] = mm(o, dy, "tn", "mm_tn_into", into=win(grads["mla_w_out"], j))
    do = mm(dy, win(wts["mla_w_out"], j), "nt", "mm_nt")
    dq, dk, dv = attn_bwd(qr, krot, kv, o, lse, do, "attn_bwd")
    dqslots, dkv, dkr, d_qn, d_kn = qk_bwd(qslots, kv, kr, qn, kn, cos_t, sin_t, dq, dk, dv, "qk_bwd")
    grads["mla_w_q_up"] = mm(cqn, dqslots, "tn", "mm_tn_into", into=win(grads["mla_w_q_up"], j))
    dcq, d_qa = _input_grad([(dqslots, win(wts["mla_w_q_up"], j))], cq, qa_gain, None)
    grads["mla_w_kv_up"] = mm(ckvn, dkv, "tn", "mm_tn_into", into=win(grads["mla_w_kv_up"], j))
    dckv, d_kva = _input_grad([(dkv, win(wts["mla_w_kv_up"], j))], ckv, kva_gain, None)
    pieces = []
    for d, (c0, n) in zip((dcq, dckv, dkr), _MLA_IN_WINDOWS):
        grads["mla_w_in"] = mm(h, d, "tn", "mm_tn_into", into=win(grads["mla_w_in"], j, col_off=c0, cols=n))
        pieces.append((d, win(wts["mla_w_in"], j, col_off=c0, cols=n)))
    dx, d_gain = _input_grad(pieces, x, gain, dy)
    return dx, dict(gain=d_gain, qa=d_qa, kva=d_kva, qn=d_qn, kn=d_kn)


BIG = (("hgrn_w_in", "rows"), ("hgrn_w_out", "rows"), ("mla_w_in", "rows"), ("mla_w_q_up", "cols"),
       ("mla_w_kv_up", "cols"), ("mla_w_out", "rows"), ("ffn_w_up", "cols"), ("ffn_w_down", "rows"))
SMALL_SHARDED = (("mla_q_a_norm", 1), ("mla_kv_a_norm", 1), ("ffn_conv_w", 2))
REPLICATED = ("norm_mix", "norm_ffn", "hgrn_lower_bounds", "hgrn_out_norm", "mla_q_norm", "mla_k_norm", "ffn_conv_b")
WEIGHTS = ("norm_mix", "norm_ffn", "hgrn_w_in", "hgrn_lower_bounds", "hgrn_out_norm", "hgrn_w_out", "mla_w_in",
           "mla_q_a_norm", "mla_w_q_up", "mla_kv_a_norm", "mla_w_kv_up", "mla_q_norm", "mla_k_norm", "mla_w_out",
           "ffn_w_up", "ffn_conv_w", "ffn_conv_b", "ffn_w_down")


def _pad_cols(a, width):
    return jnp.pad(a, [(0, 0)] * (a.ndim - 1) + [(0, width - a.shape[-1])])


def _head_slots(w):
    lead, n = w.shape[:-1], w.shape[-1] // MLA_QK
    return _pad_cols(w.reshape(lead + (n, MLA_QK)), MLA_SLOT).reshape(lead + (n * MLA_SLOT,))


def _head_unslots(w):
    lead, n = w.shape[:-1], w.shape[-1] // MLA_SLOT
    return w.reshape(lead + (n, MLA_SLOT))[..., :MLA_QK].reshape(lead + (n * MLA_QK,))


def _to_stack_layout(name, a):
    if name == "hgrn_w_in":
        return a
    if name == "mla_w_in":
        return _pad_cols(a, MLA_IN_COLS)
    if name == "mla_w_q_up":
        return _head_slots(a)
    return a


def _from_stack_layout(name, a):
    if name == "mla_w_in":
        return a[..., :2 * MLA_LORA + MLA_ROPE]
    if name == "mla_w_q_up":
        return _head_unslots(a)
    return a


def _rope_tables(positions):
    inv_freq = ROPE_THETA ** (-jnp.arange(0, MLA_ROPE, 2, dtype=F32) / MLA_ROPE)
    ang = positions.astype(F32)[:, None] * inv_freq
    cos, sin = jnp.cos(ang), jnp.sin(ang)
    S = positions.shape[0]
    ones, zeros = jnp.ones((S, MLA_NOPE), F32), jnp.zeros((S, MLA_SLOT - MLA_QK), F32)
    return (jnp.concatenate([ones, cos, cos, zeros], axis=1),
            jnp.concatenate([jnp.zeros((S, MLA_NOPE), F32), -sin, sin, zeros], axis=1))


def kernel(x, positions, norm_mix, norm_ffn, hgrn_w_in, hgrn_lower_bounds, hgrn_out_norm, hgrn_w_out, mla_w_in, mla_q_a_norm, mla_w_q_up, mla_kv_a_norm, mla_w_kv_up, mla_q_norm, mla_k_norm, mla_w_out, ffn_w_up, ffn_conv_w, ffn_conv_b, ffn_w_down, loss_target, m_norm_mix, m_norm_ffn, m_hgrn_w_in, m_hgrn_lower_bounds, m_hgrn_out_norm, m_hgrn_w_out, m_mla_w_in, m_mla_q_a_norm, m_mla_w_q_up, m_mla_kv_a_norm, m_mla_w_kv_up, m_mla_q_norm, m_mla_k_norm, m_mla_w_out, m_ffn_w_up, m_ffn_conv_w, m_ffn_conv_b, m_ffn_w_down, v_norm_mix, v_norm_ffn, v_hgrn_w_in, v_hgrn_lower_bounds, v_hgrn_out_norm, v_hgrn_w_out, v_mla_w_in, v_mla_q_a_norm, v_mla_w_q_up, v_mla_kv_a_norm, v_mla_w_kv_up, v_mla_q_norm, v_mla_k_norm, v_mla_w_out, v_ffn_w_up, v_ffn_conv_w, v_ffn_conv_b, v_ffn_w_down):
    args = dict(locals())
    w = {n: args[n] for n in WEIGHTS}
    m = {n: args["m_" + n] for n in WEIGHTS}
    v = {n: args["v_" + n] for n in WEIGHTS}
    depth = norm_mix.shape[0]
    x0 = x[0]
    S = x0.shape[0]
    chip = (2 * lax.axis_index("x") + lax.axis_index("y")).astype(jnp.int32).reshape(1)
    core = lax.axis_index("c").astype(jnp.int32).reshape(1)
    big_names = [n for n, _ in BIG]
    kinds = [k for _, k in BIG]
    small_names = [n for n, _ in SMALL_SHARDED]
    small_axis = dict(SMALL_SHARDED)

    local = {n: _to_stack_layout(n, w[n]) for n in big_names}
    gathered = gather_big([local[n].astype(_MXU) for n in big_names], kinds, "gather_weights")
    wts = {n: place(g, local[n], k, chip, "place") for n, k, g in zip(big_names, kinds, gathered)}
    (got_small,) = exchange([_pack([w[n] for n in small_names], F32, SUBLANES)], "chips", [None], "gather_small")
    per_chip = [_unpack(got_small[p], [w[n].shape for n in small_names]) for p in range(N_CHIPS)]
    small = {n: jnp.concatenate([per_chip[p][k] for p in range(N_CHIPS)], axis=small_axis[n]) for k, n in enumerate(small_names)}

    cos_t, sin_t = _rope_tables(positions[0])
    lbs = lower_bound_fwd(hgrn_lower_bounds, "lower_bound_fwd")
    mall = jnp.asarray(_hgrn_sum_matrix(min(HGRN_CHUNK, S)), _MXU)
    qn = _pad_cols(mla_q_norm, MLA_SLOT)
    kn = _pad_cols(mla_k_norm, MLA_SLOT)
    cw8 = jnp.pad(small["ffn_conv_w"], ((0, 0), (0, SUBLANES - 3), (0, 0)))

    def mixer_args(layer):
        j = layer // 2
        if layer % 2 == 0:
            return (j, norm_mix[layer:layer + 1], wts, lbs[j:j + 1], hgrn_out_norm[j:j + 1], mall)
        return (j, norm_mix[layer:layer + 1], wts, small["mla_q_a_norm"][j:j + 1], small["mla_kv_a_norm"][j:j + 1],
                qn[j:j + 1], kn[j:j + 1], cos_t, sin_t)

    def ffn_args(layer):
        return (layer, norm_ffn[layer:layer + 1], wts, cw8[layer], ffn_conv_b[layer:layer + 1])

    xc, h = x0, rms_fwd(x0, norm_mix[0:1], "rms_fwd")
    saved = []
    for layer in range(depth):
        fwd = _hgrn_layer_fwd if layer % 2 == 0 else _mla_layer_fwd
        xc, h, s_mix = fwd(xc, h, norm_ffn[layer:layer + 1], *mixer_args(layer))
        xc, h, s_ffn = _ffn_fwd(xc, h, norm_mix[layer + 1:layer + 2] if layer + 1 < depth else None, *ffn_args(layer))
        saved.append((s_mix, s_ffn))

    dh, loss_blk = loss_head(xc, loss_target[0], "loss_head")
    loss = lax.psum(loss_blk[0, 0], MESH_AXES)

    grads = {n: lax.empty(g.shape, F32) for n, g in zip(big_names, gathered)}
    g_mix, g_ffn = [None] * depth, [None] * depth
    for layer in reversed(range(depth)):
        s_mix, s_ffn = saved[layer]
        dh, g_ffn[layer] = _ffn_bwd(dh, s_ffn, *ffn_args(layer), grads)
        bwd = _hgrn_layer_bwd if layer % 2 == 0 else _mla_layer_bwd
        dh, g_mix[layer] = bwd(dh, s_mix, *mixer_args(layer), grads)
    hg = [g_mix[l] for l in range(0, depth, 2)]
    mg = [g_mix[l] for l in range(1, depth, 2)]
    d_p = lower_bound_bwd(hgrn_lower_bounds, hg[1]["lb"], "lower_bound_bwd")

    g_list = [grads[n] for n in big_names]
    from_core = send_other_half(g_list, "reduce_cores_in")
    chip_sums = [add_cores(g, o, core, "add_cores") for g, o in zip(g_list, from_core)]
    cx, cy, cc = lax.axis_index("x"), lax.axis_index("y"), lax.axis_index("c")
    second = 2 * (cx + cc - 2 * cx * cc) + (cy + (1 - cc) - 2 * cy * (1 - cc))
    chips = (chip, second.astype(jnp.int32).reshape(1))
    from_first = reduce_first_axis(chip_sums, kinds, "reduce_axis_1")
    partial = [add_first_axis(own, got, k, chips, "add_axis_1") for own, got, k in zip(chip_sums, from_first, kinds)]
    from_second = reduce_second_axis([p[1] for p in partial], "reduce_axis_2")
    reduced = [add_second_axis(p[0], got, "add_axis_2") for p, got in zip(partial, from_second)]
    other_half = exchange(reduced, "sibling", [None] * len(reduced), "reduce_cores_out", keep_own=False)
    big_out = {}
    for n, mine, theirs in zip(big_names, reduced, other_half):
        outs = adamw_stacked(mine, theirs, local[n], _to_stack_layout(n, m[n]), _to_stack_layout(n, v[n]), core, "adamw")
        big_out[n] = [_from_stack_layout(n, o) for o in outs]

    small_grads = {
        "norm_mix": jnp.concatenate([g["gain"] for g in g_mix], axis=0),
        "norm_ffn": jnp.concatenate([g["gain"] for g in g_ffn], axis=0),
        "hgrn_lower_bounds": d_p[0:2],
        "hgrn_out_norm": jnp.concatenate([g["out_gain"] for g in hg], axis=0),
        "mla_q_a_norm": jnp.concatenate([g["qa"] for g in mg], axis=0),
        "mla_kv_a_norm": jnp.concatenate([g["kva"] for g in mg], axis=0),
        "mla_q_norm": jnp.concatenate([g["qn"][:, :MLA_QK] for g in mg], axis=0),
        "mla_k_norm": jnp.concatenate([g["kn"][:, :MLA_QK] for g in mg], axis=0),
        "ffn_conv_w": jnp.stack([g["conv_w"] for g in g_ffn]),
        "ffn_conv_b": jnp.concatenate([g["conv_b"] for g in g_ffn], axis=0),
    }

    def chip_part(n, p):
        size = w[n].shape[small_axis[n]]
        return lax.slice_in_dim(small_grads[n], p * size, (p + 1) * size, axis=small_axis[n])

    to_chips = jnp.stack([_pack([chip_part(n, p) for n in small_names], F32, SUBLANES) for p in range(N_CHIPS)])
    rep_g, shard_g = exchange([_pack([small_grads[n] for n in REPLICATED], F32, SUBLANES), to_chips], "all",
                              [None, (2, 1, 0)], "reduce_small")
    small_out = {}
    for names, gparts in ((REPLICATED, rep_g), (small_names, shard_g)):
        packed = adamw_packed(gparts, *[_pack([t[n] for n in names], F32, SUBLANES) for t in (w, m, v)], "adamw_small")
        unpacked = [_unpack(buf, [w[n].shape for n in names]) for buf in packed]
        for k, n in enumerate(names):
            small_out[n] = [u[k] for u in unpacked]

    result = [loss, dh[None]]
    for k in range(4):
        result += [(big_out[n] if n in big_out else small_out[n])[k] for n in WEIGHTS]
    return tuple(result)
```

```python
import numpy as np
import jax
import jax.numpy as jnp
from jax import lax
from jax.experimental import pallas as pl
from jax.experimental.pallas import tpu as pltpu

F32 = jnp.float32
BF16 = jnp.bfloat16
_MXU = BF16

RMS_EPS = 1e-6
D_MODEL = 1024
HEADS = 8
HEAD_DIM = 128
HGRN_CHUNK = 128
MLA_NOPE = 128
MLA_ROPE = 64
MLA_QK = MLA_NOPE + MLA_ROPE
MLA_SLOT = 256
MLA_LORA = 256
MLA_IN_COLS = 2 * MLA_LORA + HEAD_DIM
ROPE_THETA = 10000.0
D_FF = 2816
FF_BLOCK = 1408
LANES = 128
SUBLANES = 8

ADAM_LR = 0.001
ADAM_B1 = 0.9
ADAM_B2 = 0.999
ADAM_EPS = 1e-08
ADAM_WD = 0.01
ADAM_STEP = 10

VMEM_LIMIT = 56 * 1024 * 1024
MM_VMEM_BUDGET = 46 * 1024 * 1024
MESH_AXES = ("x", "y", "c")
N_CHIPS = 4

_NN = ((1,), (0,))
_NT = ((1,), (1,))
_TN = ((0,), (0,))


def _dg(a, b, dims):
    return lax.dot_general(a.astype(_MXU), b.astype(_MXU), (dims, ((), ())), preferred_element_type=F32)


@jax.custom_vjp
def kdot(a, b):
    return _dg(a, b, _NN)


kdot.defvjp(lambda a, b: (_dg(a, b, _NN), (a, b)), lambda r, g: (_dg(g, r[1], _NT), _dg(r[0], g, _TN)))


@jax.custom_vjp
def kdot_nt(a, b):
    return _dg(a, b, _NT)


kdot_nt.defvjp(lambda a, b: (_dg(a, b, _NT), (a, b)), lambda r, g: (_dg(g, r[1], _NN), _dg(g, r[0], _TN)))


@jax.custom_vjp
def kdot_tn(a, b):
    return _dg(a, b, _TN)


kdot_tn.defvjp(lambda a, b: (_dg(a, b, _TN), (a, b)), lambda r, g: (_dg(r[1], g, _NT), _dg(r[0], g, _NN)))


def _pick(d, prefs):
    for p in prefs:
        if d >= p and d % p == 0:
            return p
    return d


def _params(sem):
    return pltpu.CompilerParams(dimension_semantics=sem, vmem_limit_bytes=VMEM_LIMIT)


def _sds(shape, dtype=F32):
    return jax.ShapeDtypeStruct(shape, dtype)


def win(arr, layer, row_off=0, col_off=0, rows=None, cols=None):
    return (arr, layer, row_off, col_off, rows or arr.shape[1] - row_off, cols or arr.shape[2] - col_off)


def mm(a, b, mode, name, add=None, out_dtype=F32, into=None, rms_gain=None, norm_bwd=None):
    if isinstance(b, tuple):
        b_arr, b_layer, b_r0, b_c0, b_rows, b_cols = b
    else:
        b_arr, b_layer, b_r0, b_c0, (b_rows, b_cols) = b, None, 0, 0, b.shape
    if mode == "nn":
        (M, K), (K2, N) = a.shape, (b_rows, b_cols)
    elif mode == "nt":
        (M, K), (N, K2) = a.shape, (b_rows, b_cols)
    else:
        (K, M), (K2, N) = a.shape, (b_rows, b_cols)
    assert K == K2, (name, a.shape, b_rows, b_cols)
    tn = N if N <= 1024 else _pick(N, (1024, 1408, 512, 256, 128))
    tk = K if K <= 2048 else _pick(K, (2048, 2816, 1024, 512, 256, 128))
    nk = K // tk
    a_bytes = jnp.dtype(a.dtype).itemsize
    b_bytes = jnp.dtype(b_arr.dtype).itemsize
    extra_tiles = (add is not None) + (rms_gain is not None) / 2 + (3 if norm_bwd is not None else 0)

    def vmem_bytes(tm_):
        tiles = 2 * (tm_ * tk * a_bytes + tk * tn * b_bytes) + tm_ * tn * 4 * (2 + 2 * extra_tiles + (nk > 1))
        return tiles

    tm = M
    if M > 1024:
        fits = [t for t in (2048, 1024, 1408, 512, 256, 128) if M % t == 0 and vmem_bytes(t) <= MM_VMEM_BUDGET]
        tm = fits[0] if fits else _pick(M, (128,))
    dims = {"nn": _NN, "nt": _NT, "tn": _TN}[mode]
    a_spec = pl.BlockSpec((tk, tm), lambda i, j, k: (k, i)) if mode == "tn" else pl.BlockSpec((tm, tk), lambda i, j, k: (i, k))
    b_blk = (tn, tk) if mode == "nt" else (tk, tn)
    assert b_r0 % b_blk[0] == 0 and b_c0 % b_blk[1] == 0, (name, b_r0, b_c0, b_blk)
    br, bc = b_r0 // b_blk[0], b_c0 // b_blk[1]
    if mode == "nt":
        b_idx = lambda i, j, k: (br + j, bc + k)
    else:
        b_idx = lambda i, j, k: (br + k, bc + j)
    if b_layer is None:
        b_spec = pl.BlockSpec(b_blk, b_idx)
    else:
        b_spec = pl.BlockSpec((None,) + b_blk, lambda i, j, k: (b_layer,) + b_idx(i, j, k))
    plain = pl.BlockSpec((tm, tn), lambda i, j, k: (i, j))
    has_add, has_rms, has_nb = add is not None, rms_gain is not None, norm_bwd is not None
    assert not (has_rms or has_nb) or (tn == N and into is None and not (has_rms and has_nb)), name
    vec = pl.BlockSpec((1, tn), lambda i, j, k: (0, j))
    ins = [a, b_arr] + ([add] if has_add else []) + ([rms_gain] if has_rms else [])
    specs = [a_spec, b_spec] + ([plain] if has_add else []) + ([vec] if has_rms else [])
    n_in = len(ins)
    if has_nb:
        nb_x, nb_gain, nb_res = norm_bwd
        ins += [nb_x, nb_gain] + ([nb_res] if nb_res is not None else [])
        specs += [plain, vec] + ([plain] if nb_res is not None else [])
    aliases = {}
    if into is None:
        o_spec, out_shape = plain, _sds((M, N), out_dtype)
    else:
        buf, o_layer, o_r0, o_c0, o_rows, o_cols = into
        assert (o_rows, o_cols) == (M, N) and o_r0 % tm == 0 and o_c0 % tn == 0, (name, into[1:], M, N, tm, tn)
        orow, ocol = o_r0 // tm, o_c0 // tn
        o_spec = pl.BlockSpec((None, tm, tn), lambda i, j, k: (o_layer, orow + i, ocol + j))
        out_shape = _sds(buf.shape, buf.dtype)
        aliases = {len(ins): 0}
        ins.append(buf)
        specs.append(pl.BlockSpec(memory_space=pl.ANY))

    n_all_in = len(ins)

    def kern(*refs):
        a_ref, b_ref = refs[0], refs[1]
        add_ref = refs[2] if has_add else None
        o_ref = refs[n_all_in]

        def finish(r):
            if has_add:
                r = r + add_ref[...].astype(F32)
            if has_nb:
                _, vjp = jax.vjp(lambda xv, gv: _rms(xv, gv, N), refs[n_in][...], refs[n_in + 1][...])
                dx, dgain = vjp(r)
                if nb_res is None:
                    o_ref[...] = dx.astype(o_ref.dtype)
                else:
                    dx = dx + refs[n_in + 2][...]
                    o_ref[...] = dx
                    refs[n_all_in + 2][...] = dx.astype(_MXU)
                _store(refs[n_all_in + 1], dgain, pl.program_id(0) == 0)
                return
            o_ref[...] = r.astype(o_ref.dtype)
            if has_rms:
                refs[n_all_in + 1][...] = _rms(r, refs[n_in - 1][...], N).astype(_MXU)

        if nk == 1:
            finish(_dg(a_ref[...], b_ref[...], dims))
            return
        acc = refs[-1]
        k = pl.program_id(2)

        @pl.when(k == 0)
        def _():
            acc[...] = jnp.zeros_like(acc)

        acc[...] += _dg(a_ref[...], b_ref[...], dims)

        @pl.when(k == nk - 1)
        def _():
            finish(acc[...])

    if has_rms:
        o_spec, out_shape = [o_spec, plain], [out_shape, _sds((M, N), _MXU)]
    if has_nb:
        if nb_res is None:
            o_spec, out_shape = [o_spec, vec], [_sds((M, N), _MXU), _sds((1, N))]
        else:
            o_spec, out_shape = [o_spec, vec, plain], [out_shape, _sds((1, N)), _sds((M, N), _MXU)]
    return pl.pallas_call(
        kern, name=name, grid=(M // tm, N // tn, nk), in_specs=specs, out_specs=o_spec, out_shape=out_shape,
        scratch_shapes=[pltpu.VMEM((tm, tn), F32)] if nk > 1 else [], input_output_aliases=aliases,
        compiler_params=_params(("arbitrary" if has_nb else "parallel", "parallel", "arbitrary")))(*ins)


def _store(ref, val, first):
    if first is None:
        ref[...] = val.astype(ref.dtype)
        return

    @pl.when(first)
    def _():
        ref[...] = val.astype(ref.dtype)

    @pl.when(jnp.logical_not(first))
    def _():
        ref[...] += val.astype(ref.dtype)


def tilecall(body, name, grid, ins, outs, sem, prefetch=(), aliases=None):
    n_pre, n_in = len(prefetch), len(ins)

    def kern(*refs):
        vals = body(*refs[:n_pre + n_in])
        for ref, val, (_, _, first) in zip(refs[n_pre + n_in:], vals, outs):
            _store(ref, val, None if first is None else first())

    in_specs, out_specs = [s for _, s in ins], [s for _, s, _ in outs]
    kwargs = dict(name=name, out_shape=[sh for sh, _, _ in outs], compiler_params=_params(sem),
                  input_output_aliases={n_pre + k: v for k, v in (aliases or {}).items()})
    if n_pre:
        kwargs["grid_spec"] = pltpu.PrefetchScalarGridSpec(num_scalar_prefetch=n_pre, grid=grid, in_specs=in_specs,
                                                           out_specs=out_specs)
    else:
        kwargs.update(grid=grid, in_specs=in_specs, out_specs=out_specs)
    return pl.pallas_call(kern, **kwargs)(*prefetch, *[a for a, _ in ins])


def _rms(x, g, n):
    ms = jnp.sum(x * x, axis=-1, keepdims=True) / n
    return x * lax.rsqrt(ms + RMS_EPS) * g


def _row_tile(S, w):
    return min(S, 512 if w <= 1024 else 256)


def rms_fwd(x, g, name, col=0, w=None):
    S = x.shape[0]
    w = w or x.shape[1]
    ts = _row_tile(S, w)
    return tilecall(
        lambda x_ref, g_ref: (_rms(x_ref[...], g_ref[...], w),), name, (S // ts,),
        [(x, pl.BlockSpec((ts, w), lambda i: (i, col))), (g, pl.BlockSpec((1, w), lambda i: (0, 0)))],
        [(_sds((S, w), _MXU), pl.BlockSpec((ts, w), lambda i: (i, 0)), None)], ("parallel",))[0]


FF_ROWS = 512
HALO_ROWS = 16


def _shifted(u, halo_ref, is_first):
    rid = lax.broadcasted_iota(jnp.int32, (SUBLANES, 1), 0)
    halo = halo_ref[...].astype(F32)
    hrow = lax.broadcasted_iota(jnp.int32, (HALO_ROWS, 1), 0)

    def halo_row(r):
        return jnp.where(is_first, 0.0, jnp.sum(jnp.where(hrow == r, halo, 0.0), axis=0, keepdims=True))

    h7, h6 = halo_row(HALO_ROWS - 1), halo_row(HALO_ROWS - 2)
    r1, r2 = pltpu.roll(u, 1, 0), pltpu.roll(u, 2, 0)
    top1 = jnp.where(rid == 0, h7, r1[:SUBLANES])
    top2 = jnp.where(rid == 0, h6, jnp.where(rid == 1, h7, r2[:SUBLANES]))
    return jnp.concatenate([top1, r1[SUBLANES:]], axis=0), jnp.concatenate([top2, r2[SUBLANES:]], axis=0)


def _conv(u, u1, u2, cw_ref, cb_ref):
    return ((cb_ref[...] + u2 * cw_ref[0:1, :]) + u1 * cw_ref[1:2, :]) + u * cw_ref[2:3, :]


def _ffn_specs(S, ts, jmap):
    hb = ts // HALO_ROWS
    return (pl.BlockSpec((ts, FF_BLOCK), lambda j, i: (i, jmap(j))),
            pl.BlockSpec((HALO_ROWS, FF_BLOCK), lambda j, i: (jnp.maximum(i * hb - 1, 0), jmap(j))),
            pl.BlockSpec((SUBLANES, FF_BLOCK), lambda j, i: (0, jmap(j))),
            pl.BlockSpec((1, FF_BLOCK), lambda j, i: (0, jmap(j))))


def ffn_act_fwd(u, cw8, cb, name):
    S = u.shape[0]
    ts = min(S, FF_ROWS)
    nb = D_FF // FF_BLOCK

    def body(ug, hg, cwg, cbg, uu, hu, cwu, cbu):
        first = pl.program_id(1) == 0
        g = ug[...].astype(F32)
        g1, g2 = _shifted(g, hg, first)
        yg = _conv(g, g1, g2, cwg, cbg)
        v = uu[...].astype(F32)
        v1, v2 = _shifted(v, hu, first)
        yu = _conv(v, v1, v2, cwu, cbu)
        return (yg * jax.nn.sigmoid(yg) * yu,)

    sg = _ffn_specs(S, ts, lambda j: j)
    su = _ffn_specs(S, ts, lambda j: j + nb)
    ins = [(u, sg[0]), (u, sg[1]), (cw8, sg[2]), (cb, sg[3]), (u, su[0]), (u, su[1]), (cw8, su[2]), (cb, su[3])]
    return tilecall(body, name, (nb, S // ts), ins,
                    [(_sds((S, D_FF), _MXU), pl.BlockSpec((ts, FF_BLOCK), lambda j, i: (i, j)), None)],
                    ("parallel", "parallel"))[0]


def ffn_act_bwd(u, cw8, cb, da, name):
    S = u.shape[0]
    ts = min(S, FF_ROWS)
    nb = D_FF // FF_BLOCK

    def taps(dy, x, x1, x2):
        return jnp.concatenate(
            [jnp.sum(dy * x2, axis=0, keepdims=True), jnp.sum(dy * x1, axis=0, keepdims=True),
             jnp.sum(dy * x, axis=0, keepdims=True), jnp.zeros((SUBLANES - 3, dy.shape[1]), F32)], axis=0)

    def body(ug, hg, cwg, cbg, uu, hu, cwu, cbu, da_ref):
        first = pl.program_id(1) == 0
        g = ug[...].astype(F32)
        g1, g2 = _shifted(g, hg, first)
        yg = _conv(g, g1, g2, cwg, cbg)
        v = uu[...].astype(F32)
        v1, v2 = _shifted(v, hu, first)
        yu = _conv(v, v1, v2, cwu, cbu)
        d = da_ref[...].astype(F32)
        sg = jax.nn.sigmoid(yg)
        dyg = d * yu * (sg * (1.0 + yg * (1.0 - sg)))
        dyu = d * (yg * sg)
        return (dyg, dyu, taps(dyg, g, g1, g2), taps(dyu, v, v1, v2),
                jnp.sum(dyg, axis=0, keepdims=True), jnp.sum(dyu, axis=0, keepdims=True))

    sg_ = _ffn_specs(S, ts, lambda j: j)
    su_ = _ffn_specs(S, ts, lambda j: j + nb)
    row = pl.BlockSpec((ts, FF_BLOCK), lambda j, i: (i, j))
    ins = [(u, sg_[0]), (u, sg_[1]), (cw8, sg_[2]), (cb, sg_[3]), (u, su_[0]), (u, su_[1]), (cw8, su_[2]), (cb, su_[3]), (da, row)]
    first_row = lambda: pl.program_id(1) == 0
    dy, dcw, dcb = (_sds((S, D_FF)), row, None), (_sds((SUBLANES, D_FF)), sg_[2], first_row), (_sds((1, D_FF)), sg_[3], first_row)
    return tilecall(body, name, (nb, S // ts), ins, [dy, dy, dcw, dcw, dcb, dcb], ("parallel", "arbitrary"))


def ffn_conv_bwd(dyg, dyu, cw8, name):
    S = dyg.shape[0]
    ts = min(S, FF_ROWS)
    hb = ts // SUBLANES
    nrow = S // ts
    nb = D_FF // FF_BLOCK

    def back(dy_ref, halo_ref, cw_ref):
        last = pl.program_id(1) == nrow - 1
        d = dy_ref[...]
        rid = lax.broadcasted_iota(jnp.int32, (SUBLANES, 1), 0)
        n0 = jnp.where(last, 0.0, halo_ref[0:1, :])
        n1 = jnp.where(last, 0.0, halo_ref[1:2, :])
        r1, r2 = pltpu.roll(d, ts - 1, 0), pltpu.roll(d, ts - 2, 0)
        end1 = jnp.where(rid == SUBLANES - 1, n0, r1[ts - SUBLANES:])
        end2 = jnp.where(rid == SUBLANES - 1, n1, jnp.where(rid == SUBLANES - 2, n0, r2[ts - SUBLANES:]))
        d1 = jnp.concatenate([r1[:ts - SUBLANES], end1], axis=0)
        d2 = jnp.concatenate([r2[:ts - SUBLANES], end2], axis=0)
        return d * cw_ref[2:3, :] + d1 * cw_ref[1:2, :] + d2 * cw_ref[0:1, :]

    row = pl.BlockSpec((ts, FF_BLOCK), lambda j, i: (i, j))
    halo = pl.BlockSpec((SUBLANES, FF_BLOCK), lambda j, i: (jnp.minimum((i + 1) * hb, S // SUBLANES - 1), j))
    ins = [(dyg, row), (dyg, halo), (cw8, pl.BlockSpec((SUBLANES, FF_BLOCK), lambda j, i: (0, j))),
           (dyu, row), (dyu, halo), (cw8, pl.BlockSpec((SUBLANES, FF_BLOCK), lambda j, i: (0, j + nb)))]
    out = (_sds((S, D_FF), _MXU), row, None)
    return tilecall(lambda a, b, c, d, e, f: (back(a, b, c), back(d, e, f)), name, (nb, nrow), ins, [out, out],
                    ("parallel", "parallel"))


def _hgrn_levels(C):
    out, m = [], C // 2
    while m >= 1:
        out.append(m)
        m //= 2
    return out


def _hgrn_sum_matrix(C):
    t = np.arange(C)[:, None]
    u = np.arange(C)[None, :]
    blocks = [u <= t, u > t]
    for m in _hgrn_levels(C):
        r = (t // (2 * m)) * (2 * m) + m
        right = (t % (2 * m)) >= m
        blocks.append((right & (u > r) & (u <= t)) | ((~right) & (u > t) & (u <= r)))
    return np.concatenate(blocks, axis=0).astype(np.float32)


def _make_partial_sums(nb, C):
    @jax.custom_vjp
    def sums(mall, lf):
        hi = lf.astype(_MXU)
        mid = (lf - hi.astype(F32)).astype(_MXU)
        e2 = _dg(mall, jnp.concatenate([hi, mid], axis=1), _NN)
        e = e2[:, :HEAD_DIM] + e2[:, HEAD_DIM:]
        return tuple(e[b * C:(b + 1) * C] for b in range(nb))

    def fwd(mall, lf):
        return sums(mall, lf), mall

    def bwd(mall, gs):
        return jnp.zeros_like(mall), _dg(mall, jnp.concatenate(gs, axis=0), _TN)

    sums.defvjp(fwd, bwd)
    return sums


def _hgrn_chunk(zq, zf, v, lb, st, mall, C):
    levels = _hgrn_levels(C)
    qs = zq * jax.nn.sigmoid(zq)
    fg = lb + (1.0 - lb) * jax.nn.sigmoid(zf)
    k = 1.0 - fg
    e = _make_partial_sums(2 + len(levels), C)(mall, jnp.log(fg))
    g_incl, g_after = e[0], e[1]
    rid = lax.broadcasted_iota(jnp.int32, (C, 1), 0)
    tt = lax.broadcasted_iota(jnp.int32, (C, C), 0)
    ss = lax.broadcasted_iota(jnp.int32, (C, C), 1)
    o = kdot_nt(qs * jnp.exp(g_incl), st)
    o = o + jnp.sum(qs * k, axis=-1, keepdims=True) * v
    scores = jnp.zeros((C, C), F32)
    for li, m in enumerate(levels):
        sh = int(np.log2(m))
        right = ((rid >> sh) & 1) == 1
        both = jnp.where(right, qs, k) * jnp.exp(e[2 + li])
        pair = ((tt >> (sh + 1)) == (ss >> (sh + 1))) & (((tt >> sh) & 1) == 1) & (((ss >> sh) & 1) == 0)
        scores = scores + jnp.where(pair, kdot_nt(both, both), 0.0)
    o = o + kdot(scores, v)
    g_last = jnp.sum(jnp.where(rid == C - 1, g_incl, 0.0), axis=0, keepdims=True)
    st_new = st * jnp.exp(g_last) + kdot_tn(v, k * jnp.exp(g_after))
    return o, st_new


HGRN_HEADS_PER_STEP = 8
_HGRN_LANES = HGRN_HEADS_PER_STEP * HEAD_DIM


def _hgrn_in_specs(C, nc, rev):
    cm = (lambda c: nc - 1 - c) if rev else (lambda c: c)
    blk = lambda: pl.BlockSpec((C, _HGRN_LANES), lambda h, c: (cm(c), h))
    return cm, [blk(), blk(), blk(), pl.BlockSpec((1, _HGRN_LANES), lambda h, c: (0, h))]


def _hgrn_state_spec(cm):
    return pl.BlockSpec((HGRN_HEADS_PER_STEP, None, HEAD_DIM, HEAD_DIM), lambda h, c: (h, cm(c), 0, 0))


def _hgrn_out(o, g, gain):
    return _rms(o, gain, HEAD_DIM) * (g * jax.nn.sigmoid(g))


def hgrn_fwd(zq, zf, zi, zg, lb, out_gain, mall, name):
    S = zq.shape[0]
    C = min(HGRN_CHUNK, S)
    nc = S // C

    def kern(zq_ref, zf_ref, zi_ref, lb_ref, zg_ref, gain_ref, mall_ref, o_ref, on_ref, st_ref, st):
        @pl.when(pl.program_id(1) == 0)
        def _():
            st[...] = jnp.zeros_like(st)

        mall_v = mall_ref[...]
        for g in range(HGRN_HEADS_PER_STEP):
            lanes = slice(g * HEAD_DIM, (g + 1) * HEAD_DIM)
            s_in = st[g]
            st_ref[g] = s_in
            o, s_new = _hgrn_chunk(zq_ref[:, lanes], zf_ref[:, lanes], zi_ref[:, lanes], lb_ref[:, lanes], s_in, mall_v, C)
            o_ref[:, lanes] = o
            on_ref[:, lanes] = _hgrn_out(o, zg_ref[:, lanes], gain_ref[...]).astype(on_ref.dtype)
            st[g] = s_new

    cm, specs = _hgrn_in_specs(C, nc, False)
    row = pl.BlockSpec((C, _HGRN_LANES), lambda h, c: (c, h))
    return pl.pallas_call(
        kern, name=name, grid=(HEADS // HGRN_HEADS_PER_STEP, nc),
        in_specs=specs + [row, pl.BlockSpec((1, HEAD_DIM), lambda h, c: (0, 0)), pl.BlockSpec(mall.shape, lambda h, c: (0, 0))],
        out_specs=[row, row, _hgrn_state_spec(cm)],
        out_shape=[_sds((S, D_MODEL)), _sds((S, D_MODEL), _MXU), _sds((HEADS, nc, HEAD_DIM, HEAD_DIM))],
        scratch_shapes=[pltpu.VMEM((HGRN_HEADS_PER_STEP, HEAD_DIM, HEAD_DIM), F32)],
        compiler_params=_params(("parallel", "arbitrary")))(zq, zf, zi, lb, zg, out_gain, mall)


def hgrn_bwd(zq, zf, zi, zg, lb, out_gain, mall, states, o, don, name):
    S = zq.shape[0]
    C = min(HGRN_CHUNK, S)
    nc = S // C

    def kern(zq_ref, zf_ref, zi_ref, lb_ref, zg_ref, gain_ref, mall_ref, st_ref, o_ref, don_ref,
             dq_ref, df_ref, di_ref, dg_ref, dlb_ref, dgain_ref, dst):
        first = pl.program_id(1) == 0

        @pl.when(first)
        def _():
            dst[...] = jnp.zeros_like(dst)

        mall_v = mall_ref[...]
        gls, dgain = [], None
        for g in range(HGRN_HEADS_PER_STEP):
            lanes = slice(g * HEAD_DIM, (g + 1) * HEAD_DIM)
            _, out_vjp = jax.vjp(_hgrn_out, o_ref[:, lanes], zg_ref[:, lanes], gain_ref[...])
            do, dzg, dgn = out_vjp(don_ref[:, lanes])
            dg_ref[:, lanes] = dzg.astype(dg_ref.dtype)
            dgain = dgn if dgain is None else dgain + dgn
            _, vjp = jax.vjp(lambda a, b, c, d, e: _hgrn_chunk(a, b, c, d, e, mall_v, C),
                             zq_ref[:, lanes], zf_ref[:, lanes], zi_ref[:, lanes], lb_ref[:, lanes], st_ref[g])
            ga, gb, gv, gl, gs = vjp((do, dst[g]))
            dq_ref[:, lanes] = ga.astype(dq_ref.dtype)
            df_ref[:, lanes] = gb.astype(df_ref.dtype)
            di_ref[:, lanes] = gv.astype(di_ref.dtype)
            gls.append(gl)
            dst[g] = gs
        _store(dlb_ref, jnp.concatenate(gls, axis=1), first)
        _store(dgain_ref, dgain, first & (pl.program_id(0) == 0))

    cm, specs = _hgrn_in_specs(C, nc, True)
    row = lambda: pl.BlockSpec((C, _HGRN_LANES), lambda h, c: (cm(c), h))
    vec = pl.BlockSpec((1, HEAD_DIM), lambda h, c: (0, 0))
    wide = _sds((S, D_MODEL), _MXU)
    return pl.pallas_call(
        kern, name=name, grid=(HEADS // HGRN_HEADS_PER_STEP, nc),
        in_specs=specs + [row(), vec, pl.BlockSpec(mall.shape, lambda h, c: (0, 0)), _hgrn_state_spec(cm), row(), row()],
        out_specs=[row(), row(), row(), row(), pl.BlockSpec((1, _HGRN_LANES), lambda h, c: (0, h)), vec],
        out_shape=[wide, wide, wide, wide, _sds((1, D_MODEL)), _sds((1, HEAD_DIM))],
        scratch_shapes=[pltpu.VMEM((HGRN_HEADS_PER_STEP, HEAD_DIM, HEAD_DIM), F32)],
        compiler_params=_params(("arbitrary", "arbitrary")))(zq, zf, zi, lb, zg, out_gain, mall, states, o, don)


def _lb_soft(p0, p1):
    mx = jnp.maximum(p0, p1)
    e0, e1 = jnp.exp(p0 - mx), jnp.exp(p1 - mx)
    s0, s1 = e0 / (e0 + e1), e1 / (e0 + e1)
    return (s0 + s1) - s0


def lower_bound_fwd(p, name):
    assert p.shape[0] == 2

    def body(p_ref):
        s = _lb_soft(p_ref[0:1, :], p_ref[1:2, :])
        return (jnp.concatenate([jnp.zeros_like(s), s] + [jnp.zeros_like(s)] * (SUBLANES - 2), axis=0),)

    spec8 = pl.BlockSpec((SUBLANES, p.shape[1]), lambda: (0, 0))
    return tilecall(body, name, (), [(p, pl.BlockSpec(p.shape, lambda: (0, 0)))], [(_sds((SUBLANES, p.shape[1])), spec8, None)], ())[0]


def lower_bound_bwd(p, dlb1, name):
    def body(p_ref, d_ref):
        _, vjp = jax.vjp(_lb_soft, p_ref[0:1, :], p_ref[1:2, :])
        g0, g1 = vjp(d_ref[...])
        return (jnp.concatenate([g0, g1] + [jnp.zeros_like(g0)] * (SUBLANES - 2), axis=0),)

    spec8 = pl.BlockSpec((SUBLANES, p.shape[1]), lambda: (0, 0))
    return tilecall(body, name, (), [(p, pl.BlockSpec(p.shape, lambda: (0, 0))), (dlb1, pl.BlockSpec(dlb1.shape, lambda: (0, 0)))],
                    [(_sds((SUBLANES, p.shape[1])), spec8, None)], ())[0]


@jax.custom_vjp
def _swap_rope_halves(x):
    lane = lax.broadcasted_iota(jnp.int32, x.shape, 1)
    lo = (lane >= MLA_NOPE) & (lane < MLA_NOPE + MLA_ROPE // 2)
    hi = (lane >= MLA_NOPE + MLA_ROPE // 2) & (lane < MLA_QK)
    return jnp.where(lo, pltpu.roll(x, MLA_SLOT - MLA_ROPE // 2, 1), jnp.where(hi, pltpu.roll(x, MLA_ROPE // 2, 1), 0.0))


_swap_rope_halves.defvjp(lambda x: (_swap_rope_halves(x), None), lambda _, g: (_swap_rope_halves(g),))


def _norm_rope(x, gain, cos_t, sin_t):
    y = _rms(x, gain, MLA_QK)
    return y * cos_t + _swap_rope_halves(y) * sin_t


_ATTN_SCALE = MLA_QK ** -0.5


def _qk_heads(qs, kn, kr, qn, kn_gain, cos_t, sin_t):
    q = _norm_rope(qs, qn, cos_t, sin_t) * _ATTN_SCALE
    k = _norm_rope(jnp.concatenate([kn, kr], axis=1), kn_gain, cos_t, sin_t)
    return q, k


QK_ROWS = 512
QK_HEADS_PER_STEP = 4


def _qk_specs(ts):
    G = QK_HEADS_PER_STEP
    slot = pl.BlockSpec((ts, G * MLA_SLOT), lambda i, h: (i, h))
    shared = pl.BlockSpec((ts, HEAD_DIM), lambda i, h: (i, 0))
    gain = pl.BlockSpec((1, MLA_SLOT), lambda i, h: (0, 0))
    table = pl.BlockSpec((ts, MLA_SLOT), lambda i, h: (i, 0))
    return slot, shared, gain, table


def _head_slot(g):
    return slice(g * MLA_SLOT, (g + 1) * MLA_SLOT)


def _head_nope(g):
    return slice(g * MLA_SLOT, g * MLA_SLOT + HEAD_DIM)


def qk_fwd(qslots, kv, krope, qn, kn, cos_t, sin_t, name):
    S = qslots.shape[0]
    ts = min(S, QK_ROWS)
    slot, shared, gain, table = _qk_specs(ts)

    def body(q_ref, kv_ref, kr_ref, qn_ref, kg_ref, c_ref, s_ref):
        kr, qg, kg, c, s = kr_ref[...], qn_ref[...], kg_ref[...], c_ref[...], s_ref[...]
        heads = [_qk_heads(q_ref[:, _head_slot(g)], kv_ref[:, _head_nope(g)], kr, qg, kg, c, s)
                 for g in range(QK_HEADS_PER_STEP)]
        return jnp.concatenate([h[0] for h in heads], axis=1), jnp.concatenate([h[1] for h in heads], axis=1)

    out = _sds((S, HEADS * MLA_SLOT), _MXU)
    return tilecall(body, name, (S // ts, HEADS // QK_HEADS_PER_STEP),
                    [(qslots, slot), (kv, slot), (krope, shared), (qn, gain), (kn, gain), (cos_t, table), (sin_t, table)],
                    [(out, slot, None), (out, slot, None)], ("parallel", "parallel"))


def qk_bwd(qslots, kv, krope, qn, kn, cos_t, sin_t, dq, dk, dv, name):
    S = qslots.shape[0]
    ts = min(S, QK_ROWS)
    slot, shared, gain, table = _qk_specs(ts)
    vblk = pl.BlockSpec((ts, QK_HEADS_PER_STEP * HEAD_DIM), lambda i, h: (i, h))

    def body(q_ref, kv_ref, kr_ref, qn_ref, kg_ref, c_ref, s_ref, dq_ref, dk_ref, dv_ref):
        kr, qg, kg, c, s = kr_ref[...], qn_ref[...], kg_ref[...], c_ref[...], s_ref[...]
        dqs, dkvs, gr, g1, g2 = [], [], None, None, None
        for g in range(QK_HEADS_PER_STEP):
            _, vjp = jax.vjp(lambda a, b, r, w1, w2: _qk_heads(a, b, r, w1, w2, c, s),
                             q_ref[:, _head_slot(g)], kv_ref[:, _head_nope(g)], kr, qg, kg)
            ga, gb, r_, a_, b_ = vjp((dq_ref[:, _head_slot(g)], dk_ref[:, _head_slot(g)]))
            dqs.append(ga)
            dkvs += [gb, dv_ref[:, g * HEAD_DIM:(g + 1) * HEAD_DIM]]
            gr, g1, g2 = (r_, a_, b_) if gr is None else (gr + r_, g1 + a_, g2 + b_)
        return jnp.concatenate(dqs, axis=1), jnp.concatenate(dkvs, axis=1), gr, g1, g2

    first_head = lambda: pl.program_id(1) == 0
    first = lambda: (pl.program_id(0) == 0) & (pl.program_id(1) == 0)
    wide = _sds((S, HEADS * MLA_SLOT), _MXU)
    return tilecall(body, name, (S // ts, HEADS // QK_HEADS_PER_STEP),
                    [(qslots, slot), (kv, slot), (krope, shared), (qn, gain), (kn, gain), (cos_t, table), (sin_t, table),
                     (dq, slot), (dk, slot), (dv, vblk)],
                    [(wide, slot, None), (wide, slot, None), (_sds((S, HEAD_DIM)), shared, first_head),
                     (_sds((1, MLA_SLOT)), gain, first), (_sds((1, MLA_SLOT)), gain, first)], ("arbitrary", "arbitrary"))


ATTN_TILE_FWD = 1024
ATTN_TILE = 1024
ATTN_HEADS_PER_STEP = 4
ATTN_HEADS_PER_STEP_BWD = 1


def _causal_pairs(nq, by_row):
    pairs = [(i, j) for i in range(nq) for j in range(i + 1)] if by_row else [(i, j) for j in range(nq) for i in range(j, nq)]
    return jnp.asarray([p[0] for p in pairs], jnp.int32), jnp.asarray([p[1] for p in pairs], jnp.int32)


def _diag_mask(s, tq):
    rows = lax.broadcasted_iota(jnp.int32, (tq, tq), 0)
    cols = lax.broadcasted_iota(jnp.int32, (tq, tq), 1)
    return jnp.where(rows >= cols, s, -jnp.inf)


def attn_fwd(qr, kr, kv, name):
    S = qr.shape[0]
    tq = min(S, ATTN_TILE_FWD)
    nq = S // tq
    i_tab, j_tab = _causal_pairs(nq, True)

    G = ATTN_HEADS_PER_STEP
    reps = tq // LANES

    def kern(it, jt, q_ref, k_ref, kv_ref, o_ref, lse_ref, m_s, l_s, acc):
        n = pl.program_id(1)
        i, j = it[n], jt[n]

        @pl.when(j == 0)
        def _():
            m_s[...] = jnp.full_like(m_s, -jnp.inf)
            l_s[...] = jnp.zeros_like(l_s)
            acc[...] = jnp.zeros_like(acc)

        def step(diagonal):
            for g in range(G):
                slot = slice(g * MLA_SLOT, (g + 1) * MLA_SLOT)
                s = _dg(q_ref[:, slot], k_ref[:, slot], _NT)
                if diagonal:
                    s = _diag_mask(s, tq)
                m_prev = m_s[g]
                m_new = jnp.maximum(m_prev, jnp.max(s, axis=-1, keepdims=True))
                alpha = jnp.exp(m_prev - m_new)
                p = jnp.exp(s - jnp.tile(m_new, (1, reps)))
                l_s[g] = alpha * l_s[g] + jnp.sum(p, axis=-1, keepdims=True)
                acc[g] = alpha * acc[g] + _dg(p, kv_ref[:, g * MLA_SLOT + HEAD_DIM:(g + 1) * MLA_SLOT], _NN)
                m_s[g] = m_new

        @pl.when(j < i)
        def _():
            step(False)

        @pl.when(j == i)
        def _():
            step(True)
            for g in range(G):
                l = l_s[g]
                lanes = slice(g * HEAD_DIM, (g + 1) * HEAD_DIM)
                o_ref[:, lanes] = acc[g] / l
                lse_ref[:, lanes] = m_s[g] + jnp.log(l)

    out = _sds((S, HEADS * HEAD_DIM))
    oblk = pl.BlockSpec((tq, G * HEAD_DIM), lambda h, n, it, jt: (it[n], h))
    stat = pltpu.VMEM((G, tq, HEAD_DIM), F32)
    return pl.pallas_call(
        kern, name=name,
        grid_spec=pltpu.PrefetchScalarGridSpec(
            num_scalar_prefetch=2, grid=(HEADS // G, i_tab.shape[0]),
            in_specs=[pl.BlockSpec((tq, G * MLA_SLOT), lambda h, n, it, jt: (it[n], h)),
                      pl.BlockSpec((tq, G * MLA_SLOT), lambda h, n, it, jt: (jt[n], h)),
                      pl.BlockSpec((tq, G * MLA_SLOT), lambda h, n, it, jt: (jt[n], h))],
            out_specs=[oblk, oblk], scratch_shapes=[stat, stat, stat]),
        out_shape=[out, out], compiler_params=_params(("parallel", "arbitrary")))(i_tab, j_tab, qr, kr, kv)


def attn_bwd(qr, kr, kv, o, lse, do, name):
    S = qr.shape[0]
    tq = min(S, ATTN_TILE)
    nq = S // tq
    i_tab, j_tab = _causal_pairs(nq, False)

    G = ATTN_HEADS_PER_STEP_BWD

    def kern(it, jt, q_ref, k_ref, kv_ref, o_ref, lse_ref, do_ref, dq_ref, dk_ref, dv_ref, dk_acc, dv_acc):
        n = pl.program_id(1)
        i, j = it[n], jt[n]

        @pl.when(n == 0)
        def _():
            dq_ref[...] = jnp.zeros_like(dq_ref)

        @pl.when(i == j)
        def _():
            dk_acc[...] = jnp.zeros_like(dk_acc)
            dv_acc[...] = jnp.zeros_like(dv_acc)

        def step(diagonal):
            rows = pl.ds(pl.multiple_of(i * tq, tq), tq)
            for g in range(G):
                slot = slice(g * MLA_SLOT, (g + 1) * MLA_SLOT)
                lanes = slice(g * HEAD_DIM, (g + 1) * HEAD_DIM)
                q, k = q_ref[:, slot], k_ref[:, slot]
                s = _dg(q, k, _NT) - jnp.tile(lse_ref[:, lanes], (1, tq // LANES))
                if diagonal:
                    s = _diag_mask(s, tq)
                p = jnp.exp(s)
                d = do_ref[:, lanes]
                delta = jnp.sum(d * o_ref[:, lanes], axis=-1, keepdims=True)
                dv_acc[:, lanes] += _dg(p, d, _TN)
                ds = p * (_dg(d, kv_ref[:, g * MLA_SLOT + HEAD_DIM:(g + 1) * MLA_SLOT], _NT) - delta)
                dk_acc[:, slot] += _dg(ds, q, _TN)
                dq_ref[rows, slot] += _dg(ds, k, _NN)

        @pl.when(i > j)
        def _():
            step(False)

        @pl.when(i == j)
        def _():
            step(True)

        @pl.when(i == nq - 1)
        def _():
            dk_ref[...] = dk_acc[...]
            dv_ref[...] = dv_acc[...]

    qblk = pl.BlockSpec((tq, G * MLA_SLOT), lambda h, n, it, jt: (it[n], h))
    oblk = pl.BlockSpec((tq, G * HEAD_DIM), lambda h, n, it, jt: (it[n], h))
    kblk = pl.BlockSpec((tq, G * MLA_SLOT), lambda h, n, it, jt: (jt[n], h))
    return pl.pallas_call(
        kern, name=name,
        grid_spec=pltpu.PrefetchScalarGridSpec(
            num_scalar_prefetch=2, grid=(HEADS // G, i_tab.shape[0]),
            in_specs=[qblk, kblk, kblk, oblk, oblk, oblk],
            out_specs=[pl.BlockSpec((S, G * MLA_SLOT), lambda h, n, it, jt: (0, h)), kblk,
                       pl.BlockSpec((tq, G * HEAD_DIM), lambda h, n, it, jt: (jt[n], h))],
            scratch_shapes=[pltpu.VMEM((tq, G * MLA_SLOT), F32), pltpu.VMEM((tq, G * HEAD_DIM), F32)]),
        out_shape=[_sds((S, HEADS * MLA_SLOT)), _sds((S, HEADS * MLA_SLOT)), _sds((S, HEADS * HEAD_DIM))],
        compiler_params=_params(("parallel", "arbitrary")))(i_tab, j_tab, qr, kr, kv, o, lse, do)


def loss_head(y, target, name):
    S, Dm = y.shape
    ts = _row_tile(S, Dm)

    def body(y_ref, t_ref):
        e = y_ref[...] - t_ref[...]
        tot = jnp.sum(jnp.sum(e * e, axis=-1, keepdims=True) / Dm, axis=0, keepdims=True)
        return e / Dm, jnp.broadcast_to(0.5 * tot, (SUBLANES, LANES)), e / Dm

    row = pl.BlockSpec((ts, Dm), lambda i: (i, 0))
    return tilecall(body, name, (S // ts,), [(y, row), (target, row)],
                    [(_sds((S, Dm)), row, None),
                     (_sds((SUBLANES, LANES)), pl.BlockSpec((SUBLANES, LANES), lambda i: (0, 0)), lambda: pl.program_id(0) == 0),
                     (_sds((S, Dm), _MXU), row, None)],
                    ("arbitrary",))


_PEER_FLIPS = {
    "chips": ((1, 0, 0), (0, 1, 0), (1, 1, 0)),
    "sibling": ((0, 0, 1),),
    "all": tuple((a, b, c) for a in (0, 1) for b in (0, 1) for c in (0, 1))[1:],
}
_SLOT_WEIGHTS = {"chips": (2, 1, 0), "sibling": (0, 0, 1), "all": (4, 2, 1)}
_HBM = pl.BlockSpec(memory_space=pltpu.HBM)


def _me():
    return lax.axis_index("x"), lax.axis_index("y"), lax.axis_index("c")


def _remote(src, dst, send_sem, recv_sem, peer):
    return pltpu.make_async_remote_copy(src_ref=src, dst_ref=dst, send_sem=send_sem, recv_sem=recv_sem,
                                        device_id=peer, device_id_type=pl.DeviceIdType.MESH)


def exchange(arrs, group, slab_weights, name, keep_own=True):
    flips = _PEER_FLIPS[group]
    wx, wy, wc = _SLOT_WEIGHTS[group]
    n_slots = len(flips) + (1 if keep_own else 0)
    n = len(arrs)

    def slab(ref, a, pos):
        w = slab_weights[a]
        return ref if w is None else ref.at[w[0] * pos[0] + w[1] * pos[1] + w[2] * pos[2]]

    def kern(*refs):
        srcs, outs = refs[:n], refs[n:2 * n]
        send_sems, recv_sems = refs[2 * n:2 * n + 2]
        me = _me()
        my_slot = wx * me[0] + wy * me[1] + wc * me[2]
        copies = []
        if keep_own:
            local_sems = refs[2 * n + 2]
            for a in range(n):
                cp = pltpu.make_async_copy(slab(srcs[a], a, me), outs[a].at[my_slot], local_sems.at[a])
                cp.start()
                copies.append(cp)
        for f, flip in enumerate(flips):
            peer = tuple(m ^ b if b else m for m, b in zip(me, flip))
            for a in range(n):
                cp = _remote(slab(srcs[a], a, peer), outs[a].at[my_slot if keep_own else f],
                             send_sems.at[f, a], recv_sems.at[f, a], peer)
                cp.start()
                copies.append(cp)
        for cp in copies:
            cp.wait()

    out_shape = [_sds((n_slots,) + (a.shape if slab_weights[k] is None else a.shape[1:]), a.dtype) for k, a in enumerate(arrs)]
    sems = [pltpu.SemaphoreType.DMA((len(flips), n)), pltpu.SemaphoreType.DMA((len(flips), n))]
    return pl.pallas_call(
        kern, name=name, in_specs=[_HBM] * n, out_specs=[_HBM] * n, out_shape=out_shape,
        scratch_shapes=sems + ([pltpu.SemaphoreType.DMA((n,))] if keep_own else []))(*arrs)


def _chip_window(ref, kind, size, chip, layers):
    if kind == "rows":
        return ref.at[layers, pl.ds(chip * size, size), :]
    return ref.at[layers, :, pl.ds(pl.multiple_of(chip * size, LANES), size)]


def gather_big(shards, kinds, name):
    n = len(shards)
    fulls = []
    for s, kind in zip(shards, kinds):
        L, r, c = s.shape
        fulls.append(_sds((L, N_CHIPS * r, c) if kind == "rows" else (L, r, N_CHIPS * c), s.dtype))

    def kern(*refs):
        srcs, outs = refs[:n], refs[n:2 * n]
        ici_s, ici_r, relay_s, relay_r, d2d_s, d2d_r = refs[2 * n:]
        x, y, c = _me()
        sibling = (x, y, 1 - c)
        nbrs = ((x ^ 1, y), (x, y ^ 1))
        relay_from = (x ^ (1 - c), y ^ c)
        relay_to = (x ^ c, y ^ (1 - c), c)
        diagonal = (x ^ 1, y ^ 1)

        def window(a, chip):
            L, r, cc = shards[a].shape
            size = r if kinds[a] == "rows" else cc
            return _chip_window(outs[a], kinds[a], size, 2 * chip[0] + chip[1], pl.ds(c * (L // 2), L // 2))

        def direct(a, f):
            L = shards[a].shape[0]
            return _remote(srcs[a].at[pl.ds(c * (L // 2), L // 2)], window(a, (x, y)), ici_s.at[a, f], ici_r.at[a, f],
                           (nbrs[f][0], nbrs[f][1], c))

        def to_sibling(a, k, chip):
            return _remote(window(a, chip), window(a, chip), d2d_s.at[a, k], d2d_r.at[a, k], sibling)

        sends, relays, passed = [], [], []
        for a in range(n):
            for f in range(2):
                cp = direct(a, f)
                cp.start()
                sends.append(cp)
        for a in range(n):
            _remote(window(a, relay_from), window(a, relay_from), ici_s.at[a, c], ici_r.at[a, c], relay_to).wait_recv()
            cp = _remote(window(a, relay_from), window(a, relay_from), relay_s.at[a], relay_r.at[a], relay_to)
            cp.start()
            relays.append(cp)
            passed.append(to_sibling(a, 0, relay_from))
            passed[-1].start()
            _remote(window(a, relay_from), window(a, relay_from), ici_s.at[a, 1 - c], ici_r.at[a, 1 - c], relay_to).wait_recv()
            passed.append(to_sibling(a, 1, (relay_to[0], relay_to[1])))
            passed[-1].start()
        for a in range(n):
            relays[a].wait_recv()
            passed.append(to_sibling(a, 2, diagonal))
            passed[-1].start()
        for cp in sends + relays:
            cp.wait_send()
        for cp in passed:
            cp.wait()

    pair = pltpu.SemaphoreType.DMA((n, 2))
    one = pltpu.SemaphoreType.DMA((n,))
    three = pltpu.SemaphoreType.DMA((n, 3))
    return pl.pallas_call(kern, name=name, in_specs=[_HBM] * n, out_specs=[_HBM] * n, out_shape=fulls,
                          scratch_shapes=[pair, pair, one, one, three, three])(*shards)


def send_other_half(arrs, name):
    n = len(arrs)

    def kern(*refs):
        srcs, outs = refs[:n], refs[n:2 * n]
        send_sems, recv_sems = refs[2 * n:]
        x, y, c = _me()
        copies = []
        for a in range(n):
            hl = arrs[a].shape[0] // 2
            cp = _remote(srcs[a].at[pl.ds((1 - c) * hl, hl)], outs[a], send_sems.at[a], recv_sems.at[a], (x, y, 1 - c))
            cp.start()
            copies.append(cp)
        for cp in copies:
            cp.wait()

    return pl.pallas_call(
        kern, name=name, in_specs=[_HBM] * n, out_specs=[_HBM] * n,
        out_shape=[_sds((a.shape[0] // 2,) + a.shape[1:], a.dtype) for a in arrs],
        scratch_shapes=[pltpu.SemaphoreType.DMA((n,)), pltpu.SemaphoreType.DMA((n,))])(*arrs)


def _axis_neighbours():
    x, y, c = _me()
    return (x, y, c), (x ^ (1 - c), y ^ c), (x ^ c, y ^ (1 - c)), (x ^ 1, y ^ 1)


def reduce_first_axis(arrs, kinds, name):
    n = len(arrs)
    shapes = []
    for a, kind in zip(arrs, kinds):
        l, R, C = a.shape
        shapes.append((l, R // N_CHIPS, C) if kind == "rows" else (l, R, C // N_CHIPS))

    def kern(*refs):
        srcs, outs = refs[:n], refs[n:2 * n]
        send_sems, recv_sems = refs[2 * n:]
        (x, y, c), first, second, diagonal = _axis_neighbours()
        copies = []
        for a in range(n):
            size = shapes[a][1] if kinds[a] == "rows" else shapes[a][2]
            for k, chip in enumerate((first, diagonal)):
                window = _chip_window(srcs[a], kinds[a], size, 2 * chip[0] + chip[1], slice(None))
                cp = _remote(window, outs[a].at[k], send_sems.at[a, k], recv_sems.at[a, k], (first[0], first[1], c))
                cp.start()
                copies.append(cp)
        for cp in copies:
            cp.wait()

    sem = pltpu.SemaphoreType.DMA((n, 2))
    return pl.pallas_call(
        kern, name=name, in_specs=[_HBM] * n, out_specs=[_HBM] * n,
        out_shape=[_sds((2,) + s, a.dtype) for s, a in zip(shapes, arrs)], scratch_shapes=[sem, sem])(*arrs)


def reduce_second_axis(arrs, name):
    n = len(arrs)

    def kern(*refs):
        srcs, outs = refs[:n], refs[n:2 * n]
        send_sems, recv_sems = refs[2 * n:]
        (x, y, c), first, second, diagonal = _axis_neighbours()
        copies = []
        for a in range(n):
            cp = _remote(srcs[a], outs[a], send_sems.at[a], recv_sems.at[a], (second[0], second[1], c))
            cp.start()
            copies.append(cp)
        for cp in copies:
            cp.wait()

    return pl.pallas_call(
        kern, name=name, in_specs=[_HBM] * n, out_specs=[_HBM] * n, out_shape=[_sds(a.shape, a.dtype) for a in arrs],
        scratch_shapes=[pltpu.SemaphoreType.DMA((n,)), pltpu.SemaphoreType.DMA((n,))])(*arrs)


def _stack_tile(r, c):
    for t in (1024, 704, 512, 352, 256, 128, 64, 32, 16):
        if r % t == 0 and t * c * 4 <= 3 * 512 * 1024:
            return t
    return r


def _window_map(kind, r, tr):
    nrt = r // tr
    if kind == "rows":
        return lambda l, i, chip: (l, chip[0] * nrt + i, 0)
    return lambda l, i, chip: (l, i, chip[0])


def place(full, shard, kind, chip, name):
    L, r, c = shard.shape
    tr = _stack_tile(r, c)
    wmap = _window_map(kind, r, tr)
    return tilecall(lambda chip_ref, s_ref, f_ref: (s_ref[...],), name, (L, r // tr),
                    [(shard, pl.BlockSpec((None, tr, c), lambda l, i, chip: (l, i, 0))), (full, pl.BlockSpec(memory_space=pl.ANY))],
                    [(_sds(full.shape, full.dtype), pl.BlockSpec((None, tr, c), lambda l, i, chip: wmap(l, i, chip)), None)],
                    ("parallel", "parallel"), prefetch=(chip,), aliases={1: 0})[0]


def add_cores(g, other, core, name):
    L, R, C = g.shape
    hl = L // 2
    tr = _stack_tile(R, C)
    blk = (None, tr, C)
    return tilecall(lambda core_ref, a_ref, b_ref: (a_ref[...] + b_ref[...],), name, (hl, R // tr),
                    [(g, pl.BlockSpec(blk, lambda l, i, core: (core[0] * hl + l, i, 0))),
                     (other, pl.BlockSpec(blk, lambda l, i, core: (l, i, 0)))],
                    [(_sds((hl, R, C), _MXU), pl.BlockSpec(blk, lambda l, i, core: (l, i, 0)), None)],
                    ("parallel", "parallel"), prefetch=(core,))[0]


def add_first_axis(own, got, kind, chips, name):
    _, l, r, c = got.shape
    tr = _stack_tile(r, c)
    nrt = r // tr

    def window(k):
        if kind == "rows":
            return pl.BlockSpec((None, tr, c), lambda ll, i, *ch: (ll, ch[k][0] * nrt + i, 0))
        return pl.BlockSpec((None, tr, c), lambda ll, i, *ch: (ll, i, ch[k][0]))

    def body(chip0_ref, chip1_ref, own0, own1, got0, got1):
        return own0[...].astype(F32) + got0[...].astype(F32), own1[...].astype(F32) + got1[...].astype(F32)

    plain = pl.BlockSpec((None, tr, c), lambda ll, i, *ch: (ll, i, 0))
    return tilecall(body, name, (l, r // tr),
                    [(own, window(0)), (own, window(1))] +
                    [(got, pl.BlockSpec((None, None, tr, c), lambda ll, i, *ch, k=k: (k, ll, i, 0))) for k in range(2)],
                    [(_sds((l, r, c)), plain, None), (_sds((l, r, c), _MXU), plain, None)],
                    ("parallel", "parallel"), prefetch=tuple(chips))


def add_second_axis(mine, got, name):
    l, r, c = mine.shape
    tr = _stack_tile(r, c)
    blk = pl.BlockSpec((None, tr, c), lambda ll, i: (ll, i, 0))
    return tilecall(lambda a_ref, b_ref: (a_ref[...] + b_ref[...].astype(F32),), name, (l, r // tr),
                    [(mine, blk), (got, blk)], [(_sds((l, r, c)), blk, None)], ("parallel", "parallel"))[0]


def _adam_update(g, w, m, v):
    m_new = ADAM_B1 * m + (1.0 - ADAM_B1) * g
    v_new = ADAM_B2 * v + (1.0 - ADAM_B2) * jnp.square(g)
    m_hat = m_new / (1.0 - ADAM_B1 ** ADAM_STEP)
    v_hat = v_new / (1.0 - ADAM_B2 ** ADAM_STEP)
    delta = -ADAM_LR * (m_hat / (jnp.sqrt(v_hat) + ADAM_EPS) + ADAM_WD * w)
    return g, delta, m_new, v_new


def adamw_stacked(mine, theirs, w, m, v, core, name):
    L, r, c = w.shape
    hl = L // 2
    tr = _stack_tile(r, c)

    def body(core_ref, a_ref, b_ref, w_ref, m_ref, v_ref):
        is_mine = (pl.program_id(0) // hl) == core_ref[0]
        g = jnp.where(is_mine, a_ref[...], b_ref[...])
        return _adam_update(g, w_ref[...], m_ref[...], v_ref[...])

    full = pl.BlockSpec((None, tr, c), lambda l, i, core: (l, i, 0))
    out = (_sds((L, r, c)), full, None)
    return tilecall(body, name, (L, r // tr),
                    [(mine, pl.BlockSpec((None, tr, c), lambda l, i, core: (l % hl, i, 0))),
                     (theirs, pl.BlockSpec((None, None, tr, c), lambda l, i, core: (0, l % hl, i, 0))),
                     (w, full), (m, full), (v, full)],
                    [out, out, out, out], ("parallel", "parallel"), prefetch=(core,))


def _pack(arrs, dtype, row_multiple):
    flat = jnp.concatenate([a.reshape(-1).astype(dtype) for a in arrs])
    rows = -(-flat.shape[0] // LANES)
    rows = -(-rows // row_multiple) * row_multiple
    return jnp.pad(flat, (0, rows * LANES - flat.shape[0])).reshape(rows, LANES)


def _unpack(buf, shapes):
    flat = buf.reshape(-1)
    out, off = [], 0
    for s in shapes:
        n = int(np.prod(s))
        out.append(flat[off:off + n].reshape(s))
        off += n
    return out


def adamw_packed(gparts, w, m, v, name):
    P, R, _ = gparts.shape

    def body(g_ref, w_ref, m_ref, v_ref):
        g = g_ref[0]
        for p in range(1, P):
            g = g + g_ref[p]
        return _adam_update(g, w_ref[...], m_ref[...], v_ref[...])

    whole = pl.BlockSpec((R, LANES), lambda: (0, 0))
    out = (_sds((R, LANES)), whole, None)
    return tilecall(body, name, (), [(gparts, pl.BlockSpec((P, R, LANES), lambda: (0, 0, 0))), (w, whole), (m, whole), (v, whole)],
                    [out, out, out, out], ())


def _residual_out(a, w, x, next_gain):
    if next_gain is None:
        return mm(a, w, "nn", "mm_nn_add", add=x), None
    return mm(a, w, "nn", "mm_nn_add_rms", add=x, rms_gain=next_gain)


def _input_grad(pieces, x, gain, res):
    dh = None
    for d, w in pieces[:-1]:
        dh = mm(d, w, "nt", "mm_nt" if dh is None else "mm_nt_add", add=dh)
    d, w = pieces[-1]
    return mm(d, w, "nt", "mm_nt_norm_bwd", add=dh, norm_bwd=(x, gain, res))


def _ffn_fwd(x, h, next_gain, layer, gain, wts, cw8, cb):
    u = mm(h, win(wts["ffn_w_up"], layer), "nn", "mm_nn", out_dtype=_MXU)
    a = ffn_act_fwd(u, cw8, cb, "ffn_act_fwd")
    y, h_next = _residual_out(a, win(wts["ffn_w_down"], layer), x, next_gain)
    return y, h_next, (x, h, u, a)


def _ffn_bwd(dy, saved, layer, gain, wts, cw8, cb, grads):
    x, h, u, a = saved
    dy, dy16 = dy
    grads["ffn_w_down"] = mm(a, dy16, "tn", "mm_tn_into", into=win(grads["ffn_w_down"], layer))
    da = mm(dy16, win(wts["ffn_w_down"], layer), "nt", "mm_nt", out_dtype=_MXU)
    dyg, dyu, dcw_g, dcw_u, dcb_g, dcb_u = ffn_act_bwd(u, cw8, cb, da, "ffn_act_bwd")
    pieces = []
    for half, du in enumerate(ffn_conv_bwd(dyg, dyu, cw8, "ffn_conv_bwd")):
        cols = dict(col_off=half * D_FF, cols=D_FF)
        grads["ffn_w_up"] = mm(h, du, "tn", "mm_tn_into", into=win(grads["ffn_w_up"], layer, **cols))
        pieces.append((du, win(wts["ffn_w_up"], layer, **cols)))
    dx, d_gain, dx16 = _input_grad(pieces, x, gain, dy)
    return (dx, dx16), dict(gain=d_gain, conv_w=jnp.concatenate([dcw_g[0:3], dcw_u[0:3]], axis=1),
                    conv_b=jnp.concatenate([dcb_g, dcb_u], axis=1))


def _hgrn_w_in(wts, j, k):
    return win(wts["hgrn_w_in"], j, row_off=k * D_MODEL, rows=D_MODEL)


def _hgrn_layer_fwd(x, h, next_gain, j, gain, wts, lb, out_gain, mall):
    z = [mm(h, _hgrn_w_in(wts, j, k), "nn", "mm_nn") for k in range(4)]
    o, on, states = hgrn_fwd(z[0], z[1], z[2], z[3], lb, out_gain, mall, "hgrn_fwd")
    y, h_next = _residual_out(on, win(wts["hgrn_w_out"], j), x, next_gain)
    return y, h_next, (x, h, z, o, states, on)


def _hgrn_layer_bwd(dy, saved, j, gain, wts, lb, out_gain, mall, grads):
    x, h, z, o, states, on = saved
    dy, dy16 = dy
    grads["hgrn_w_out"] = mm(on, dy16, "tn", "mm_tn_into", into=win(grads["hgrn_w_out"], j))
    don = mm(dy16, win(wts["hgrn_w_out"], j), "nt", "mm_nt")
    dzq, dzf, dzi, dzg, dlb, d_out_gain = hgrn_bwd(z[0], z[1], z[2], z[3], lb, out_gain, mall, states, o, don, "hgrn_bwd")
    dz = [dzq, dzf, dzi, dzg]
    for k, d in enumerate(dz):
        grads["hgrn_w_in"] = mm(h, d, "tn", "mm_tn_into", into=win(grads["hgrn_w_in"], j, row_off=k * D_MODEL, rows=D_MODEL))
    dx, d_gain, dx16 = _input_grad([(d, _hgrn_w_in(wts, j, k)) for k, d in enumerate(dz)], x, gain, dy)
    return (dx, dx16), dict(gain=d_gain, lb=dlb, out_gain=d_out_gain)


_MLA_IN_WINDOWS = ((0, MLA_LORA), (MLA_LORA, MLA_LORA), (2 * MLA_LORA, HEAD_DIM))


def _mla_layer_fwd(x, h, next_gain, j, gain, wts, qa_gain, kva_gain, qn, kn, cos_t, sin_t):
    (c0, n), (c1, n1), (c2, n2) = _MLA_IN_WINDOWS
    cq, cqn = mm(h, win(wts["mla_w_in"], j, col_off=c0, cols=n), "nn", "mm_nn_rms", rms_gain=qa_gain)
    ckv, ckvn = mm(h, win(wts["mla_w_in"], j, col_off=c1, cols=n1), "nn", "mm_nn_rms", rms_gain=kva_gain)
    kr = mm(h, win(wts["mla_w_in"], j, col_off=c2, cols=n2), "nn", "mm_nn")
    qslots = mm(cqn, win(wts["mla_w_q_up"], j), "nn", "mm_nn")
    kv = mm(ckvn, win(wts["mla_w_kv_up"], j), "nn", "mm_nn")
    qr, krot = qk_fwd(qslots, kv, kr, qn, kn, cos_t, sin_t, "qk_fwd")
    o, lse = attn_fwd(qr, krot, kv, "attn_fwd")
    y, h_next = _residual_out(o, win(wts["mla_w_out"], j), x, next_gain)
    return y, h_next, (x, h, cq, ckv, kr, cqn, ckvn, qslots, kv, qr, krot, o, lse)


def _mla_layer_bwd(dy, saved, j, gain, wts, qa_gain, kva_gain, qn, kn, cos_t, sin_t, grads):
    x, h, cq, ckv, kr, cqn, ckvn, qslots, kv, qr, krot, o, lse = saved
    dy, dy16 = dy
    grads["mla_w_out"] = mm(o, dy16, "tn", "mm_tn_into", into=win(grads["mla_w_out"], j))
    do = mm(dy16, win(wts["mla_w_out"], j), "nt", "mm_nt")
    dq, dk, dv = attn_bwd(qr, krot, kv, o, lse, do, "attn_bwd")
    dqslots, dkv, dkr, d_qn, d_kn = qk_bwd(qslots, kv, kr, qn, kn, cos_t, sin_t, dq, dk, dv, "qk_bwd")
    grads["mla_w_q_up"] = mm(cqn, dqslots, "tn", "mm_tn_into", into=win(grads["mla_w_q_up"], j))
    dcq, d_qa = _input_grad([(dqslots, win(wts["mla_w_q_up"], j))], cq, qa_gain, None)
    grads["mla_w_kv_up"] = mm(ckvn, dkv, "tn", "mm_tn_into", into=win(grads["mla_w_kv_up"], j))
    dckv, d_kva = _input_grad([(dkv, win(wts["mla_w_kv_up"], j))], ckv, kva_gain, None)
    pieces = []
    for d, (c0, n) in zip((dcq, dckv, dkr), _MLA_IN_WINDOWS):
        grads["mla_w_in"] = mm(h, d, "tn", "mm_tn_into", into=win(grads["mla_w_in"], j, col_off=c0, cols=n))
        pieces.append((d, win(wts["mla_w_in"], j, col_off=c0, cols=n)))
    dx, d_gain, dx16 = _input_grad(pieces, x, gain, dy)
    return (dx, dx16), dict(gain=d_gain, qa=d_qa, kva=d_kva, qn=d_qn, kn=d_kn)


BIG = (("hgrn_w_in", "rows"), ("hgrn_w_out", "rows"), ("mla_w_in", "rows"), ("mla_w_q_up", "cols"),
       ("mla_w_kv_up", "cols"), ("mla_w_out", "rows"), ("ffn_w_up", "cols"), ("ffn_w_down", "rows"))
SMALL_SHARDED = (("mla_q_a_norm", 1), ("mla_kv_a_norm", 1), ("ffn_conv_w", 2))
REPLICATED = ("norm_mix", "norm_ffn", "hgrn_lower_bounds", "hgrn_out_norm", "mla_q_norm", "mla_k_norm", "ffn_conv_b")
WEIGHTS = ("norm_mix", "norm_ffn", "hgrn_w_in", "hgrn_lower_bounds", "hgrn_out_norm", "hgrn_w_out", "mla_w_in",
           "mla_q_a_norm", "mla_w_q_up", "mla_kv_a_norm", "mla_w_kv_up", "mla_q_norm", "mla_k_norm", "mla_w_out",
           "ffn_w_up", "ffn_conv_w", "ffn_conv_b", "ffn_w_down")


def _pad_cols(a, width):
    return jnp.pad(a, [(0, 0)] * (a.ndim - 1) + [(0, width - a.shape[-1])])


def _head_slots(w):
    lead, n = w.shape[:-1], w.shape[-1] // MLA_QK
    return _pad_cols(w.reshape(lead + (n, MLA_QK)), MLA_SLOT).reshape(lead + (n * MLA_SLOT,))


def _head_unslots(w):
    lead, n = w.shape[:-1], w.shape[-1] // MLA_SLOT
    return w.reshape(lead + (n, MLA_SLOT))[..., :MLA_QK].reshape(lead + (n * MLA_QK,))


def _to_stack_layout(name, a):
    if name == "hgrn_w_in":
        return a
    if name == "mla_w_in":
        return _pad_cols(a, MLA_IN_COLS)
    if name == "mla_w_q_up":
        return _head_slots(a)
    return a


def _from_stack_layout(name, a):
    if name == "mla_w_in":
        return a[..., :2 * MLA_LORA + MLA_ROPE]
    if name == "mla_w_q_up":
        return _head_unslots(a)
    return a


def _rope_tables(positions):
    inv_freq = ROPE_THETA ** (-jnp.arange(0, MLA_ROPE, 2, dtype=F32) / MLA_ROPE)
    ang = positions.astype(F32)[:, None] * inv_freq
    cos, sin = jnp.cos(ang), jnp.sin(ang)
    S = positions.shape[0]
    ones, zeros = jnp.ones((S, MLA_NOPE), F32), jnp.zeros((S, MLA_SLOT - MLA_QK), F32)
    return (jnp.concatenate([ones, cos, cos, zeros], axis=1),
            jnp.concatenate([jnp.zeros((S, MLA_NOPE), F32), -sin, sin, zeros], axis=1))


def kernel(x, positions, norm_mix, norm_ffn, hgrn_w_in, hgrn_lower_bounds, hgrn_out_norm, hgrn_w_out, mla_w_in, mla_q_a_norm, mla_w_q_up, mla_kv_a_norm, mla_w_kv_up, mla_q_norm, mla_k_norm, mla_w_out, ffn_w_up, ffn_conv_w, ffn_conv_b, ffn_w_down, loss_target, m_norm_mix, m_norm_ffn, m_hgrn_w_in, m_hgrn_lower_bounds, m_hgrn_out_norm, m_hgrn_w_out, m_mla_w_in, m_mla_q_a_norm, m_mla_w_q_up, m_mla_kv_a_norm, m_mla_w_kv_up, m_mla_q_norm, m_mla_k_norm, m_mla_w_out, m_ffn_w_up, m_ffn_conv_w, m_ffn_conv_b, m_ffn_w_down, v_norm_mix, v_norm_ffn, v_hgrn_w_in, v_hgrn_lower_bounds, v_hgrn_out_norm, v_hgrn_w_out, v_mla_w_in, v_mla_q_a_norm, v_mla_w_q_up, v_mla_kv_a_norm, v_mla_w_kv_up, v_mla_q_norm, v_mla_k_norm, v_mla_w_out, v_ffn_w_up, v_ffn_conv_w, v_ffn_conv_b, v_ffn_w_down):
    args = dict(locals())
    w = {n: args[n] for n in WEIGHTS}
    m = {n: args["m_" + n] for n in WEIGHTS}
    v = {n: args["v_" + n] for n in WEIGHTS}
    depth = norm_mix.shape[0]
    x0 = x[0]
    S = x0.shape[0]
    chip = (2 * lax.axis_index("x") + lax.axis_index("y")).astype(jnp.int32).reshape(1)
    core = lax.axis_index("c").astype(jnp.int32).reshape(1)
    big_names = [n for n, _ in BIG]
    kinds = [k for _, k in BIG]
    small_names = [n for n, _ in SMALL_SHARDED]
    small_axis = dict(SMALL_SHARDED)

    local = {n: _to_stack_layout(n, w[n]) for n in big_names}
    gathered = gather_big([local[n].astype(_MXU) for n in big_names], kinds, "gather_weights")
    wts = {n: place(g, local[n], k, chip, "place") for n, k, g in zip(big_names, kinds, gathered)}
    (got_small,) = exchange([_pack([w[n] for n in small_names], F32, SUBLANES)], "chips", [None], "gather_small")
    per_chip = [_unpack(got_small[p], [w[n].shape for n in small_names]) for p in range(N_CHIPS)]
    small = {n: jnp.concatenate([per_chip[p][k] for p in range(N_CHIPS)], axis=small_axis[n]) for k, n in enumerate(small_names)}

    cos_t, sin_t = _rope_tables(positions[0])
    lbs = lower_bound_fwd(hgrn_lower_bounds, "lower_bound_fwd")
    mall = jnp.asarray(_hgrn_sum_matrix(min(HGRN_CHUNK, S)), _MXU)
    qn = _pad_cols(mla_q_norm, MLA_SLOT)
    kn = _pad_cols(mla_k_norm, MLA_SLOT)
    cw8 = jnp.pad(small["ffn_conv_w"], ((0, 0), (0, SUBLANES - 3), (0, 0)))

    def mixer_args(layer):
        j = layer // 2
        if layer % 2 == 0:
            return (j, norm_mix[layer:layer + 1], wts, lbs[j:j + 1], hgrn_out_norm[j:j + 1], mall)
        return (j, norm_mix[layer:layer + 1], wts, small["mla_q_a_norm"][j:j + 1], small["mla_kv_a_norm"][j:j + 1],
                qn[j:j + 1], kn[j:j + 1], cos_t, sin_t)

    def ffn_args(layer):
        return (layer, norm_ffn[layer:layer + 1], wts, cw8[layer], ffn_conv_b[layer:layer + 1])

    xc, h = x0, rms_fwd(x0, norm_mix[0:1], "rms_fwd")
    saved = []
    for layer in range(depth):
        fwd = _hgrn_layer_fwd if layer % 2 == 0 else _mla_layer_fwd
        xc, h, s_mix = fwd(xc, h, norm_ffn[layer:layer + 1], *mixer_args(layer))
        xc, h, s_ffn = _ffn_fwd(xc, h, norm_mix[layer + 1:layer + 2] if layer + 1 < depth else None, *ffn_args(layer))
        saved.append((s_mix, s_ffn))

    dh32, loss_blk, dh16 = loss_head(xc, loss_target[0], "loss_head")
    dh = (dh32, dh16)
    loss = lax.psum(loss_blk[0, 0], MESH_AXES)

    grads = {n: lax.empty(g.shape, F32) for n, g in zip(big_names, gathered)}
    g_mix, g_ffn = [None] * depth, [None] * depth
    for layer in reversed(range(depth)):
        s_mix, s_ffn = saved[layer]
        dh, g_ffn[layer] = _ffn_bwd(dh, s_ffn, *ffn_args(layer), grads)
        bwd = _hgrn_layer_bwd if layer % 2 == 0 else _mla_layer_bwd
        dh, g_mix[layer] = bwd(dh, s_mix, *mixer_args(layer), grads)
    hg = [g_mix[l] for l in range(0, depth, 2)]
    mg = [g_mix[l] for l in range(1, depth, 2)]
    d_p = lower_bound_bwd(hgrn_lower_bounds, hg[1]["lb"], "lower_bound_bwd")

    g_list = [grads[n] for n in big_names]
    from_core = send_other_half(g_list, "reduce_cores_in")
    chip_sums = [add_cores(g, o, core, "add_cores") for g, o in zip(g_list, from_core)]
    cx, cy, cc = lax.axis_index("x"), lax.axis_index("y"), lax.axis_index("c")
    second = 2 * (cx + cc - 2 * cx * cc) + (cy + (1 - cc) - 2 * cy * (1 - cc))
    chips = (chip, second.astype(jnp.int32).reshape(1))
    from_first = reduce_first_axis(chip_sums, kinds, "reduce_axis_1")
    partial = [add_first_axis(own, got, k, chips, "add_axis_1") for own, got, k in zip(chip_sums, from_first, kinds)]
    from_second = reduce_second_axis([p[1] for p in partial], "reduce_axis_2")
    reduced = [add_second_axis(p[0], got, "add_axis_2") for p, got in zip(partial, from_second)]
    other_half = exchange(reduced, "sibling", [None] * len(reduced), "reduce_cores_out", keep_own=False)
    big_out = {}
    for n, mine, theirs in zip(big_names, reduced, other_half):
        outs = adamw_stacked(mine, theirs, local[n], _to_stack_layout(n, m[n]), _to_stack_layout(n, v[n]), core, "adamw")
        big_out[n] = [_from_stack_layout(n, o) for o in outs]

    small_grads = {
        "norm_mix": jnp.concatenate([g["gain"] for g in g_mix], axis=0),
        "norm_ffn": jnp.concatenate([g["gain"] for g in g_ffn], axis=0),
        "hgrn_lower_bounds": d_p[0:2],
        "hgrn_out_norm": jnp.concatenate([g["out_gain"] for g in hg], axis=0),
        "mla_q_a_norm": jnp.concatenate([g["qa"] for g in mg], axis=0),
        "mla_kv_a_norm": jnp.concatenate([g["kva"] for g in mg], axis=0),
        "mla_q_norm": jnp.concatenate([g["qn"][:, :MLA_QK] for g in mg], axis=0),
        "mla_k_norm": jnp.concatenate([g["kn"][:, :MLA_QK] for g in mg], axis=0),
        "ffn_conv_w": jnp.stack([g["conv_w"] for g in g_ffn]),
        "ffn_conv_b": jnp.concatenate([g["conv_b"] for g in g_ffn], axis=0),
    }

    def chip_part(n, p):
        size = w[n].shape[small_axis[n]]
        return lax.slice_in_dim(small_grads[n], p * size, (p + 1) * size, axis=small_axis[n])

    to_chips = jnp.stack([_pack([chip_part(n, p) for n in small_names], F32, SUBLANES) for p in range(N_CHIPS)])
    rep_g, shard_g = exchange([_pack([small_grads[n] for n in REPLICATED], F32, SUBLANES), to_chips], "all",
                              [None, (2, 1, 0)], "reduce_small")
    small_out = {}
    for names, gparts in ((REPLICATED, rep_g), (small_names, shard_g)):
        packed = adamw_packed(gparts, *[_pack([t[n] for n in names], F32, SUBLANES) for t in (w, m, v)], "adamw_small")
        unpacked = [_unpack(buf, [w[n].shape for n in names]) for buf in packed]
        for k, n in enumerate(names):
            small_out[n] = [u[k] for u in unpacked]

    result = [loss, dh[0][None]]
    for k in range(4):
        result += [(big_out[n] if n in big_out else small_out[n])[k] for n in WEIGHTS]
    return tuple(result)
```

```python
import numpy as np
import jax
import jax.numpy as jnp
from jax import lax
from jax.experimental import pallas as pl
from jax.experimental.pallas import tpu as pltpu

F32 = jnp.float32
BF16 = jnp.bfloat16
_MXU = BF16

RMS_EPS = 1e-6
D_MODEL = 1024
HEADS = 8
HEAD_DIM = 128
HGRN_CHUNK = 128
MLA_NOPE = 128
MLA_ROPE = 64
MLA_QK = MLA_NOPE + MLA_ROPE
MLA_SLOT = 256
MLA_LORA = 256
MLA_IN_COLS = 2 * MLA_LORA + HEAD_DIM
ROPE_THETA = 10000.0
D_FF = 2816
FF_BLOCK = 1408
LANES = 128
SUBLANES = 8

ADAM_LR = 0.001
ADAM_B1 = 0.9
ADAM_B2 = 0.999
ADAM_EPS = 1e-08
ADAM_WD = 0.01
ADAM_STEP = 10

VMEM_LIMIT = 56 * 1024 * 1024
MM_B_BUFFERS = 2
MM_VMEM_BUDGET = 46 * 1024 * 1024
MESH_AXES = ("x", "y", "c")
N_CHIPS = 4

_NN = ((1,), (0,))
_NT = ((1,), (1,))
_TN = ((0,), (0,))


def _dg(a, b, dims):
    return lax.dot_general(a.astype(_MXU), b.astype(_MXU), (dims, ((), ())), preferred_element_type=F32)


@jax.custom_vjp
def kdot(a, b):
    return _dg(a, b, _NN)


kdot.defvjp(lambda a, b: (_dg(a, b, _NN), (a, b)), lambda r, g: (_dg(g, r[1], _NT), _dg(r[0], g, _TN)))


@jax.custom_vjp
def kdot_nt(a, b):
    return _dg(a, b, _NT)


kdot_nt.defvjp(lambda a, b: (_dg(a, b, _NT), (a, b)), lambda r, g: (_dg(g, r[1], _NN), _dg(g, r[0], _TN)))


@jax.custom_vjp
def kdot_tn(a, b):
    return _dg(a, b, _TN)


kdot_tn.defvjp(lambda a, b: (_dg(a, b, _TN), (a, b)), lambda r, g: (_dg(r[1], g, _NT), _dg(r[0], g, _NN)))


def _pick(d, prefs):
    for p in prefs:
        if d >= p and d % p == 0:
            return p
    return d


def _params(sem):
    return pltpu.CompilerParams(dimension_semantics=sem, vmem_limit_bytes=VMEM_LIMIT)


def _sds(shape, dtype=F32):
    return jax.ShapeDtypeStruct(shape, dtype)


def win(arr, layer, row_off=0, col_off=0, rows=None, cols=None):
    return (arr, layer, row_off, col_off, rows or arr.shape[1] - row_off, cols or arr.shape[2] - col_off)


def mm(a, b, mode, name, add=None, out_dtype=F32, into=None, rms_gain=None, norm_bwd=None):
    if isinstance(b, tuple):
        b_arr, b_layer, b_r0, b_c0, b_rows, b_cols = b
    else:
        b_arr, b_layer, b_r0, b_c0, (b_rows, b_cols) = b, None, 0, 0, b.shape
    if mode == "nn":
        (M, K), (K2, N) = a.shape, (b_rows, b_cols)
    elif mode == "nt":
        (M, K), (N, K2) = a.shape, (b_rows, b_cols)
    else:
        (K, M), (K2, N) = a.shape, (b_rows, b_cols)
    assert K == K2, (name, a.shape, b_rows, b_cols)
    tn = N if N <= 1024 else _pick(N, (1024, 1408, 512, 256, 128))
    tk = K if K <= 2048 else _pick(K, (2048, 2816, 1024, 512, 256, 128))
    nk = K // tk
    a_bytes = jnp.dtype(a.dtype).itemsize
    b_bytes = jnp.dtype(b_arr.dtype).itemsize
    extra_tiles = (add is not None) + (rms_gain is not None) / 2 + (3 if norm_bwd is not None else 0)

    def vmem_bytes(tm_):
        tiles = (2 * tm_ * tk * a_bytes + MM_B_BUFFERS * tk * tn * b_bytes
                 + tm_ * tn * 4 * (2 + 2 * extra_tiles + (nk > 1)))
        return tiles

    tm = M
    if M > 1024:
        fits = [t for t in (2048, 1024, 1408, 512, 256, 128) if M % t == 0 and vmem_bytes(t) <= MM_VMEM_BUDGET]
        tm = fits[0] if fits else _pick(M, (128,))
    dims = {"nn": _NN, "nt": _NT, "tn": _TN}[mode]
    a_spec = pl.BlockSpec((tk, tm), lambda i, j, k: (k, i)) if mode == "tn" else pl.BlockSpec((tm, tk), lambda i, j, k: (i, k))
    b_blk = (tn, tk) if mode == "nt" else (tk, tn)
    assert b_r0 % b_blk[0] == 0 and b_c0 % b_blk[1] == 0, (name, b_r0, b_c0, b_blk)
    br, bc = b_r0 // b_blk[0], b_c0 // b_blk[1]
    if mode == "nt":
        b_idx = lambda i, j, k: (br + j, bc + k)
    else:
        b_idx = lambda i, j, k: (br + k, bc + j)
    if b_layer is None:
        b_spec = pl.BlockSpec(b_blk, b_idx, pipeline_mode=pl.Buffered(MM_B_BUFFERS))
    else:
        b_spec = pl.BlockSpec((None,) + b_blk, lambda i, j, k: (b_layer,) + b_idx(i, j, k),
                              pipeline_mode=pl.Buffered(MM_B_BUFFERS))
    plain = pl.BlockSpec((tm, tn), lambda i, j, k: (i, j))
    has_add, has_rms, has_nb = add is not None, rms_gain is not None, norm_bwd is not None
    assert not (has_rms or has_nb) or (tn == N and into is None and not (has_rms and has_nb)), name
    vec = pl.BlockSpec((1, tn), lambda i, j, k: (0, j))
    ins = [a, b_arr] + ([add] if has_add else []) + ([rms_gain] if has_rms else [])
    specs = [a_spec, b_spec] + ([plain] if has_add else []) + ([vec] if has_rms else [])
    n_in = len(ins)
    if has_nb:
        nb_x, nb_gain, nb_res = norm_bwd
        ins += [nb_x, nb_gain] + ([nb_res] if nb_res is not None else [])
        specs += [plain, vec] + ([plain] if nb_res is not None else [])
    aliases = {}
    if into is None:
        o_spec, out_shape = plain, _sds((M, N), out_dtype)
    else:
        buf, o_layer, o_r0, o_c0, o_rows, o_cols = into
        assert (o_rows, o_cols) == (M, N) and o_r0 % tm == 0 and o_c0 % tn == 0, (name, into[1:], M, N, tm, tn)
        orow, ocol = o_r0 // tm, o_c0 // tn
        o_spec = pl.BlockSpec((None, tm, tn), lambda i, j, k: (o_layer, orow + i, ocol + j))
        out_shape = _sds(buf.shape, buf.dtype)
        aliases = {len(ins): 0}
        ins.append(buf)
        specs.append(pl.BlockSpec(memory_space=pl.ANY))

    n_all_in = len(ins)

    def kern(*refs):
        a_ref, b_ref = refs[0], refs[1]
        add_ref = refs[2] if has_add else None
        o_ref = refs[n_all_in]

        def finish(r):
            if has_add:
                r = r + add_ref[...].astype(F32)
            if has_nb:
                _, vjp = jax.vjp(lambda xv, gv: _rms(xv, gv, N), refs[n_in][...], refs[n_in + 1][...])
                dx, dgain = vjp(r)
                if nb_res is None:
                    o_ref[...] = dx.astype(o_ref.dtype)
                else:
                    dx = dx + refs[n_in + 2][...]
                    o_ref[...] = dx
                    refs[n_all_in + 2][...] = dx.astype(_MXU)
                _store(refs[n_all_in + 1], dgain, pl.program_id(0) == 0)
                return
            o_ref[...] = r.astype(o_ref.dtype)
            if has_rms:
                refs[n_all_in + 1][...] = _rms(r, refs[n_in - 1][...], N).astype(_MXU)

        if nk == 1:
            finish(_dg(a_ref[...], b_ref[...], dims))
            return
        acc = refs[-1]
        k = pl.program_id(2)

        @pl.when(k == 0)
        def _():
            acc[...] = jnp.zeros_like(acc)

        acc[...] += _dg(a_ref[...], b_ref[...], dims)

        @pl.when(k == nk - 1)
        def _():
            finish(acc[...])

    if has_rms:
        o_spec, out_shape = [o_spec, plain], [out_shape, _sds((M, N), _MXU)]
    if has_nb:
        if nb_res is None:
            o_spec, out_shape = [o_spec, vec], [_sds((M, N), _MXU), _sds((1, N))]
        else:
            o_spec, out_shape = [o_spec, vec, plain], [out_shape, _sds((1, N)), _sds((M, N), _MXU)]
    return pl.pallas_call(
        kern, name=name, grid=(M // tm, N // tn, nk), in_specs=specs, out_specs=o_spec, out_shape=out_shape,
        scratch_shapes=[pltpu.VMEM((tm, tn), F32)] if nk > 1 else [], input_output_aliases=aliases,
        compiler_params=_params(("arbitrary" if has_nb else "parallel", "parallel", "arbitrary")))(*ins)


def _store(ref, val, first):
    if first is None:
        ref[...] = val.astype(ref.dtype)
        return

    @pl.when(first)
    def _():
        ref[...] = val.astype(ref.dtype)

    @pl.when(jnp.logical_not(first))
    def _():
        ref[...] += val.astype(ref.dtype)


def tilecall(body, name, grid, ins, outs, sem, prefetch=(), aliases=None):
    n_pre, n_in = len(prefetch), len(ins)

    def kern(*refs):
        vals = body(*refs[:n_pre + n_in])
        for ref, val, (_, _, first) in zip(refs[n_pre + n_in:], vals, outs):
            _store(ref, val, None if first is None else first())

    in_specs, out_specs = [s for _, s in ins], [s for _, s, _ in outs]
    kwargs = dict(name=name, out_shape=[sh for sh, _, _ in outs], compiler_params=_params(sem),
                  input_output_aliases={n_pre + k: v for k, v in (aliases or {}).items()})
    if n_pre:
        kwargs["grid_spec"] = pltpu.PrefetchScalarGridSpec(num_scalar_prefetch=n_pre, grid=grid, in_specs=in_specs,
                                                           out_specs=out_specs)
    else:
        kwargs.update(grid=grid, in_specs=in_specs, out_specs=out_specs)
    return pl.pallas_call(kern, **kwargs)(*prefetch, *[a for a, _ in ins])


def _rms(x, g, n):
    ms = jnp.sum(x * x, axis=-1, keepdims=True) / n
    return x * lax.rsqrt(ms + RMS_EPS) * g


def _row_tile(S, w):
    return min(S, 512 if w <= 1024 else 256)


def rms_fwd(x, g, name, col=0, w=None):
    S = x.shape[0]
    w = w or x.shape[1]
    ts = _row_tile(S, w)
    return tilecall(
        lambda x_ref, g_ref: (_rms(x_ref[...], g_ref[...], w),), name, (S // ts,),
        [(x, pl.BlockSpec((ts, w), lambda i: (i, col))), (g, pl.BlockSpec((1, w), lambda i: (0, 0)))],
        [(_sds((S, w), _MXU), pl.BlockSpec((ts, w), lambda i: (i, 0)), None)], ("parallel",))[0]


FF_ROWS = 512
HALO_ROWS = 16


def _shifted(u, halo_ref, is_first):
    rid = lax.broadcasted_iota(jnp.int32, (SUBLANES, 1), 0)
    halo = halo_ref[...].astype(F32)
    hrow = lax.broadcasted_iota(jnp.int32, (HALO_ROWS, 1), 0)

    def halo_row(r):
        return jnp.where(is_first, 0.0, jnp.sum(jnp.where(hrow == r, halo, 0.0), axis=0, keepdims=True))

    h7, h6 = halo_row(HALO_ROWS - 1), halo_row(HALO_ROWS - 2)
    r1, r2 = pltpu.roll(u, 1, 0), pltpu.roll(u, 2, 0)
    top1 = jnp.where(rid == 0, h7, r1[:SUBLANES])
    top2 = jnp.where(rid == 0, h6, jnp.where(rid == 1, h7, r2[:SUBLANES]))
    return jnp.concatenate([top1, r1[SUBLANES:]], axis=0), jnp.concatenate([top2, r2[SUBLANES:]], axis=0)


def _conv(u, u1, u2, cw_ref, cb_ref):
    return ((cb_ref[...] + u2 * cw_ref[0:1, :]) + u1 * cw_ref[1:2, :]) + u * cw_ref[2:3, :]


def _ffn_specs(S, ts, jmap):
    hb = ts // HALO_ROWS
    return (pl.BlockSpec((ts, FF_BLOCK), lambda j, i: (i, jmap(j))),
            pl.BlockSpec((HALO_ROWS, FF_BLOCK), lambda j, i: (jnp.maximum(i * hb - 1, 0), jmap(j))),
            pl.BlockSpec((SUBLANES, FF_BLOCK), lambda j, i: (0, jmap(j))),
            pl.BlockSpec((1, FF_BLOCK), lambda j, i: (0, jmap(j))))


def ffn_act_fwd(u, cw8, cb, name):
    S = u.shape[0]
    ts = min(S, FF_ROWS)
    nb = D_FF // FF_BLOCK

    def body(ug, hg, cwg, cbg, uu, hu, cwu, cbu):
        first = pl.program_id(1) == 0
        g = ug[...].astype(F32)
        g1, g2 = _shifted(g, hg, first)
        yg = _conv(g, g1, g2, cwg, cbg)
        v = uu[...].astype(F32)
        v1, v2 = _shifted(v, hu, first)
        yu = _conv(v, v1, v2, cwu, cbu)
        return (yg * jax.nn.sigmoid(yg) * yu,)

    sg = _ffn_specs(S, ts, lambda j: j)
    su = _ffn_specs(S, ts, lambda j: j + nb)
    ins = [(u, sg[0]), (u, sg[1]), (cw8, sg[2]), (cb, sg[3]), (u, su[0]), (u, su[1]), (cw8, su[2]), (cb, su[3])]
    return tilecall(body, name, (nb, S // ts), ins,
                    [(_sds((S, D_FF), _MXU), pl.BlockSpec((ts, FF_BLOCK), lambda j, i: (i, j)), None)],
                    ("parallel", "parallel"))[0]


def ffn_act_bwd(u, cw8, cb, da, name):
    S = u.shape[0]
    ts = min(S, FF_ROWS)
    nb = D_FF // FF_BLOCK

    def taps(dy, x, x1, x2):
        return jnp.concatenate(
            [jnp.sum(dy * x2, axis=0, keepdims=True), jnp.sum(dy * x1, axis=0, keepdims=True),
             jnp.sum(dy * x, axis=0, keepdims=True), jnp.zeros((SUBLANES - 3, dy.shape[1]), F32)], axis=0)

    def body(ug, hg, cwg, cbg, uu, hu, cwu, cbu, da_ref):
        first = pl.program_id(1) == 0
        g = ug[...].astype(F32)
        g1, g2 = _shifted(g, hg, first)
        yg = _conv(g, g1, g2, cwg, cbg)
        v = uu[...].astype(F32)
        v1, v2 = _shifted(v, hu, first)
        yu = _conv(v, v1, v2, cwu, cbu)
        d = da_ref[...].astype(F32)
        sg = jax.nn.sigmoid(yg)
        dyg = d * yu * (sg * (1.0 + yg * (1.0 - sg)))
        dyu = d * (yg * sg)
        return (dyg, dyu, taps(dyg, g, g1, g2), taps(dyu, v, v1, v2),
                jnp.sum(dyg, axis=0, keepdims=True), jnp.sum(dyu, axis=0, keepdims=True))

    sg_ = _ffn_specs(S, ts, lambda j: j)
    su_ = _ffn_specs(S, ts, lambda j: j + nb)
    row = pl.BlockSpec((ts, FF_BLOCK), lambda j, i: (i, j))
    ins = [(u, sg_[0]), (u, sg_[1]), (cw8, sg_[2]), (cb, sg_[3]), (u, su_[0]), (u, su_[1]), (cw8, su_[2]), (cb, su_[3]), (da, row)]
    first_row = lambda: pl.program_id(1) == 0
    dy, dcw, dcb = (_sds((S, D_FF)), row, None), (_sds((SUBLANES, D_FF)), sg_[2], first_row), (_sds((1, D_FF)), sg_[3], first_row)
    return tilecall(body, name, (nb, S // ts), ins, [dy, dy, dcw, dcw, dcb, dcb], ("parallel", "arbitrary"))


def ffn_conv_bwd(dyg, dyu, cw8, name):
    S = dyg.shape[0]
    ts = min(S, FF_ROWS)
    hb = ts // SUBLANES
    nrow = S // ts
    nb = D_FF // FF_BLOCK

    def back(dy_ref, halo_ref, cw_ref):
        last = pl.program_id(1) == nrow - 1
        d = dy_ref[...]
        rid = lax.broadcasted_iota(jnp.int32, (SUBLANES, 1), 0)
        n0 = jnp.where(last, 0.0, halo_ref[0:1, :])
        n1 = jnp.where(last, 0.0, halo_ref[1:2, :])
        r1, r2 = pltpu.roll(d, ts - 1, 0), pltpu.roll(d, ts - 2, 0)
        end1 = jnp.where(rid == SUBLANES - 1, n0, r1[ts - SUBLANES:])
        end2 = jnp.where(rid == SUBLANES - 1, n1, jnp.where(rid == SUBLANES - 2, n0, r2[ts - SUBLANES:]))
        d1 = jnp.concatenate([r1[:ts - SUBLANES], end1], axis=0)
        d2 = jnp.concatenate([r2[:ts - SUBLANES], end2], axis=0)
        return d * cw_ref[2:3, :] + d1 * cw_ref[1:2, :] + d2 * cw_ref[0:1, :]

    row = pl.BlockSpec((ts, FF_BLOCK), lambda j, i: (i, j))
    halo = pl.BlockSpec((SUBLANES, FF_BLOCK), lambda j, i: (jnp.minimum((i + 1) * hb, S // SUBLANES - 1), j))
    ins = [(dyg, row), (dyg, halo), (cw8, pl.BlockSpec((SUBLANES, FF_BLOCK), lambda j, i: (0, j))),
           (dyu, row), (dyu, halo), (cw8, pl.BlockSpec((SUBLANES, FF_BLOCK), lambda j, i: (0, j + nb)))]
    out = (_sds((S, D_FF), _MXU), row, None)
    return tilecall(lambda a, b, c, d, e, f: (back(a, b, c), back(d, e, f)), name, (nb, nrow), ins, [out, out],
                    ("parallel", "parallel"))


def _hgrn_levels(C):
    out, m = [], C // 2
    while m >= 1:
        out.append(m)
        m //= 2
    return out


def _hgrn_sum_matrix(C):
    t = np.arange(C)[:, None]
    u = np.arange(C)[None, :]
    blocks = [u <= t, u > t]
    for m in _hgrn_levels(C):
        r = (t // (2 * m)) * (2 * m) + m
        right = (t % (2 * m)) >= m
        blocks.append((right & (u > r) & (u <= t)) | ((~right) & (u > t) & (u <= r)))
    return np.concatenate(blocks, axis=0).astype(np.float32)


def _make_partial_sums(nb, C):
    @jax.custom_vjp
    def sums(mall, lf):
        hi = lf.astype(_MXU)
        mid = (lf - hi.astype(F32)).astype(_MXU)
        e2 = _dg(mall, jnp.concatenate([hi, mid], axis=1), _NN)
        e = e2[:, :HEAD_DIM] + e2[:, HEAD_DIM:]
        return tuple(e[b * C:(b + 1) * C] for b in range(nb))

    def fwd(mall, lf):
        return sums(mall, lf), mall

    def bwd(mall, gs):
        return jnp.zeros_like(mall), _dg(mall, jnp.concatenate(gs, axis=0), _TN)

    sums.defvjp(fwd, bwd)
    return sums


def _hgrn_chunk(zq, zf, v, lb, st, mall, C):
    levels = _hgrn_levels(C)
    qs = zq * jax.nn.sigmoid(zq)
    fg = lb + (1.0 - lb) * jax.nn.sigmoid(zf)
    k = 1.0 - fg
    e = _make_partial_sums(2 + len(levels), C)(mall, jnp.log(fg))
    g_incl, g_after = e[0], e[1]
    rid = lax.broadcasted_iota(jnp.int32, (C, 1), 0)
    tt = lax.broadcasted_iota(jnp.int32, (C, C), 0)
    ss = lax.broadcasted_iota(jnp.int32, (C, C), 1)
    o = kdot_nt(qs * jnp.exp(g_incl), st)
    o = o + jnp.sum(qs * k, axis=-1, keepdims=True) * v
    scores = jnp.zeros((C, C), F32)
    for li, m in enumerate(levels):
        sh = int(np.log2(m))
        right = ((rid >> sh) & 1) == 1
        both = jnp.where(right, qs, k) * jnp.exp(e[2 + li])
        pair = ((tt >> (sh + 1)) == (ss >> (sh + 1))) & (((tt >> sh) & 1) == 1) & (((ss >> sh) & 1) == 0)
        scores = scores + jnp.where(pair, kdot_nt(both, both), 0.0)
    o = o + kdot(scores, v)
    g_last = jnp.sum(jnp.where(rid == C - 1, g_incl, 0.0), axis=0, keepdims=True)
    st_new = st * jnp.exp(g_last) + kdot_tn(v, k * jnp.exp(g_after))
    return o, st_new


HGRN_HEADS_PER_STEP = 8
_HGRN_LANES = HGRN_HEADS_PER_STEP * HEAD_DIM


def _hgrn_in_specs(C, nc, rev):
    cm = (lambda c: nc - 1 - c) if rev else (lambda c: c)
    blk = lambda: pl.BlockSpec((C, _HGRN_LANES), lambda h, c: (cm(c), h))
    return cm, [blk(), blk(), blk(), pl.BlockSpec((1, _HGRN_LANES), lambda h, c: (0, h))]


def _hgrn_state_spec(cm):
    return pl.BlockSpec((HGRN_HEADS_PER_STEP, None, HEAD_DIM, HEAD_DIM), lambda h, c: (h, cm(c), 0, 0))


def _hgrn_out(o, g, gain):
    return _rms(o, gain, HEAD_DIM) * (g * jax.nn.sigmoid(g))


def hgrn_fwd(zq, zf, zi, zg, lb, out_gain, mall, name):
    S = zq.shape[0]
    C = min(HGRN_CHUNK, S)
    nc = S // C

    def kern(zq_ref, zf_ref, zi_ref, lb_ref, zg_ref, gain_ref, mall_ref, o_ref, on_ref, st_ref, st):
        @pl.when(pl.program_id(1) == 0)
        def _():
            st[...] = jnp.zeros_like(st)

        mall_v = mall_ref[...]
        for g in range(HGRN_HEADS_PER_STEP):
            lanes = slice(g * HEAD_DIM, (g + 1) * HEAD_DIM)
            s_in = st[g]
            st_ref[g] = s_in
            o, s_new = _hgrn_chunk(zq_ref[:, lanes], zf_ref[:, lanes], zi_ref[:, lanes], lb_ref[:, lanes], s_in, mall_v, C)
            o_ref[:, lanes] = o
            on_ref[:, lanes] = _hgrn_out(o, zg_ref[:, lanes], gain_ref[...]).astype(on_ref.dtype)
            st[g] = s_new

    cm, specs = _hgrn_in_specs(C, nc, False)
    row = pl.BlockSpec((C, _HGRN_LANES), lambda h, c: (c, h))
    return pl.pallas_call(
        kern, name=name, grid=(HEADS // HGRN_HEADS_PER_STEP, nc),
        in_specs=specs + [row, pl.BlockSpec((1, HEAD_DIM), lambda h, c: (0, 0)), pl.BlockSpec(mall.shape, lambda h, c: (0, 0))],
        out_specs=[row, row, _hgrn_state_spec(cm)],
        out_shape=[_sds((S, D_MODEL)), _sds((S, D_MODEL), _MXU), _sds((HEADS, nc, HEAD_DIM, HEAD_DIM))],
        scratch_shapes=[pltpu.VMEM((HGRN_HEADS_PER_STEP, HEAD_DIM, HEAD_DIM), F32)],
        compiler_params=_params(("parallel", "arbitrary")))(zq, zf, zi, lb, zg, out_gain, mall)


def hgrn_bwd(zq, zf, zi, zg, lb, out_gain, mall, states, o, don, name):
    S = zq.shape[0]
    C = min(HGRN_CHUNK, S)
    nc = S // C

    def kern(zq_ref, zf_ref, zi_ref, lb_ref, zg_ref, gain_ref, mall_ref, st_ref, o_ref, don_ref,
             dq_ref, df_ref, di_ref, dg_ref, dlb_ref, dgain_ref, dst):
        first = pl.program_id(1) == 0

        @pl.when(first)
        def _():
            dst[...] = jnp.zeros_like(dst)

        mall_v = mall_ref[...]
        gls, dgain = [], None
        for g in range(HGRN_HEADS_PER_STEP):
            lanes = slice(g * HEAD_DIM, (g + 1) * HEAD_DIM)
            _, out_vjp = jax.vjp(_hgrn_out, o_ref[:, lanes], zg_ref[:, lanes], gain_ref[...])
            do, dzg, dgn = out_vjp(don_ref[:, lanes])
            dg_ref[:, lanes] = dzg.astype(dg_ref.dtype)
            dgain = dgn if dgain is None else dgain + dgn
            _, vjp = jax.vjp(lambda a, b, c, d, e: _hgrn_chunk(a, b, c, d, e, mall_v, C),
                             zq_ref[:, lanes], zf_ref[:, lanes], zi_ref[:, lanes], lb_ref[:, lanes], st_ref[g])
            ga, gb, gv, gl, gs = vjp((do, dst[g]))
            dq_ref[:, lanes] = ga.astype(dq_ref.dtype)
            df_ref[:, lanes] = gb.astype(df_ref.dtype)
            di_ref[:, lanes] = gv.astype(di_ref.dtype)
            gls.append(gl)
            dst[g] = gs
        _store(dlb_ref, jnp.concatenate(gls, axis=1), first)
        _store(dgain_ref, dgain, first & (pl.program_id(0) == 0))

    cm, specs = _hgrn_in_specs(C, nc, True)
    row = lambda: pl.BlockSpec((C, _HGRN_LANES), lambda h, c: (cm(c), h))
    vec = pl.BlockSpec((1, HEAD_DIM), lambda h, c: (0, 0))
    wide = _sds((S, D_MODEL), _MXU)
    return pl.pallas_call(
        kern, name=name, grid=(HEADS // HGRN_HEADS_PER_STEP, nc),
        in_specs=specs + [row(), vec, pl.BlockSpec(mall.shape, lambda h, c: (0, 0)), _hgrn_state_spec(cm), row(), row()],
        out_specs=[row(), row(), row(), row(), pl.BlockSpec((1, _HGRN_LANES), lambda h, c: (0, h)), vec],
        out_shape=[wide, wide, wide, wide, _sds((1, D_MODEL)), _sds((1, HEAD_DIM))],
        scratch_shapes=[pltpu.VMEM((HGRN_HEADS_PER_STEP, HEAD_DIM, HEAD_DIM), F32)],
        compiler_params=_params(("arbitrary", "arbitrary")))(zq, zf, zi, lb, zg, out_gain, mall, states, o, don)


def _lb_soft(p0, p1):
    mx = jnp.maximum(p0, p1)
    e0, e1 = jnp.exp(p0 - mx), jnp.exp(p1 - mx)
    s0, s1 = e0 / (e0 + e1), e1 / (e0 + e1)
    return (s0 + s1) - s0


def lower_bound_fwd(p, name):
    assert p.shape[0] == 2

    def body(p_ref):
        s = _lb_soft(p_ref[0:1, :], p_ref[1:2, :])
        return (jnp.concatenate([jnp.zeros_like(s), s] + [jnp.zeros_like(s)] * (SUBLANES - 2), axis=0),)

    spec8 = pl.BlockSpec((SUBLANES, p.shape[1]), lambda: (0, 0))
    return tilecall(body, name, (), [(p, pl.BlockSpec(p.shape, lambda: (0, 0)))], [(_sds((SUBLANES, p.shape[1])), spec8, None)], ())[0]


def lower_bound_bwd(p, dlb1, name):
    def body(p_ref, d_ref):
        _, vjp = jax.vjp(_lb_soft, p_ref[0:1, :], p_ref[1:2, :])
        g0, g1 = vjp(d_ref[...])
        return (jnp.concatenate([g0, g1] + [jnp.zeros_like(g0)] * (SUBLANES - 2), axis=0),)

    spec8 = pl.BlockSpec((SUBLANES, p.shape[1]), lambda: (0, 0))
    return tilecall(body, name, (), [(p, pl.BlockSpec(p.shape, lambda: (0, 0))), (dlb1, pl.BlockSpec(dlb1.shape, lambda: (0, 0)))],
                    [(_sds((SUBLANES, p.shape[1])), spec8, None)], ())[0]


@jax.custom_vjp
def _swap_rope_halves(x):
    lane = lax.broadcasted_iota(jnp.int32, x.shape, 1)
    lo = (lane >= MLA_NOPE) & (lane < MLA_NOPE + MLA_ROPE // 2)
    hi = (lane >= MLA_NOPE + MLA_ROPE // 2) & (lane < MLA_QK)
    return jnp.where(lo, pltpu.roll(x, MLA_SLOT - MLA_ROPE // 2, 1), jnp.where(hi, pltpu.roll(x, MLA_ROPE // 2, 1), 0.0))


_swap_rope_halves.defvjp(lambda x: (_swap_rope_halves(x), None), lambda _, g: (_swap_rope_halves(g),))


def _norm_rope(x, gain, cos_t, sin_t):
    y = _rms(x, gain, MLA_QK)
    return y * cos_t + _swap_rope_halves(y) * sin_t


_ATTN_SCALE = MLA_QK ** -0.5


def _qk_heads(qs, kn, kr, qn, kn_gain, cos_t, sin_t):
    q = _norm_rope(qs, qn, cos_t, sin_t) * _ATTN_SCALE
    k = _norm_rope(jnp.concatenate([kn, kr], axis=1), kn_gain, cos_t, sin_t)
    return q, k


QK_ROWS = 512
QK_HEADS_PER_STEP = 4


def _qk_specs(ts):
    G = QK_HEADS_PER_STEP
    slot = pl.BlockSpec((ts, G * MLA_SLOT), lambda i, h: (i, h))
    shared = pl.BlockSpec((ts, HEAD_DIM), lambda i, h: (i, 0))
    gain = pl.BlockSpec((1, MLA_SLOT), lambda i, h: (0, 0))
    table = pl.BlockSpec((ts, MLA_SLOT), lambda i, h: (i, 0))
    return slot, shared, gain, table


def _head_slot(g):
    return slice(g * MLA_SLOT, (g + 1) * MLA_SLOT)


def _head_nope(g):
    return slice(g * MLA_SLOT, g * MLA_SLOT + HEAD_DIM)


def qk_fwd(qslots, kv, krope, qn, kn, cos_t, sin_t, name):
    S = qslots.shape[0]
    ts = min(S, QK_ROWS)
    slot, shared, gain, table = _qk_specs(ts)

    def body(q_ref, kv_ref, kr_ref, qn_ref, kg_ref, c_ref, s_ref):
        kr, qg, kg, c, s = kr_ref[...], qn_ref[...], kg_ref[...], c_ref[...], s_ref[...]
        heads = [_qk_heads(q_ref[:, _head_slot(g)], kv_ref[:, _head_nope(g)], kr, qg, kg, c, s)
                 for g in range(QK_HEADS_PER_STEP)]
        return jnp.concatenate([h[0] for h in heads], axis=1), jnp.concatenate([h[1] for h in heads], axis=1)

    out = _sds((S, HEADS * MLA_SLOT), _MXU)
    return tilecall(body, name, (S // ts, HEADS // QK_HEADS_PER_STEP),
                    [(qslots, slot), (kv, slot), (krope, shared), (qn, gain), (kn, gain), (cos_t, table), (sin_t, table)],
                    [(out, slot, None), (out, slot, None)], ("parallel", "parallel"))


def qk_bwd(qslots, kv, krope, qn, kn, cos_t, sin_t, dq, dk, dv, name):
    S = qslots.shape[0]
    ts = min(S, QK_ROWS)
    slot, shared, gain, table = _qk_specs(ts)
    vblk = pl.BlockSpec((ts, QK_HEADS_PER_STEP * HEAD_DIM), lambda i, h: (i, h))

    def body(q_ref, kv_ref, kr_ref, qn_ref, kg_ref, c_ref, s_ref, dq_ref, dk_ref, dv_ref):
        kr, qg, kg, c, s = kr_ref[...], qn_ref[...], kg_ref[...], c_ref[...], s_ref[...]
        dqs, dkvs, gr, g1, g2 = [], [], None, None, None
        for g in range(QK_HEADS_PER_STEP):
            _, vjp = jax.vjp(lambda a, b, r, w1, w2: _qk_heads(a, b, r, w1, w2, c, s),
                             q_ref[:, _head_slot(g)], kv_ref[:, _head_nope(g)], kr, qg, kg)
            ga, gb, r_, a_, b_ = vjp((dq_ref[:, _head_slot(g)], dk_ref[:, _head_slot(g)]))
            dqs.append(ga)
            dkvs += [gb, dv_ref[:, g * HEAD_DIM:(g + 1) * HEAD_DIM]]
            gr, g1, g2 = (r_, a_, b_) if gr is None else (gr + r_, g1 + a_, g2 + b_)
        return jnp.concatenate(dqs, axis=1), jnp.concatenate(dkvs, axis=1), gr, g1, g2

    first_head = lambda: pl.program_id(1) == 0
    first = lambda: (pl.program_id(0) == 0) & (pl.program_id(1) == 0)
    wide = _sds((S, HEADS * MLA_SLOT), _MXU)
    return tilecall(body, name, (S // ts, HEADS // QK_HEADS_PER_STEP),
                    [(qslots, slot), (kv, slot), (krope, shared), (qn, gain), (kn, gain), (cos_t, table), (sin_t, table),
                     (dq, slot), (dk, slot), (dv, vblk)],
                    [(wide, slot, None), (wide, slot, None), (_sds((S, HEAD_DIM)), shared, first_head),
                     (_sds((1, MLA_SLOT)), gain, first), (_sds((1, MLA_SLOT)), gain, first)], ("arbitrary", "arbitrary"))


ATTN_TILE_FWD = 1024
ATTN_TILE = 1024
ATTN_HEADS_PER_STEP = 4
ATTN_HEADS_PER_STEP_BWD = 2


def _causal_pairs(nq, by_row):
    pairs = [(i, j) for i in range(nq) for j in range(i + 1)] if by_row else [(i, j) for j in range(nq) for i in range(j, nq)]
    return jnp.asarray([p[0] for p in pairs], jnp.int32), jnp.asarray([p[1] for p in pairs], jnp.int32)


def _diag_mask(s, tq):
    rows = lax.broadcasted_iota(jnp.int32, (tq, tq), 0)
    cols = lax.broadcasted_iota(jnp.int32, (tq, tq), 1)
    return jnp.where(rows >= cols, s, -jnp.inf)


def attn_fwd(qr, kr, kv, name):
    S = qr.shape[0]
    tq = min(S, ATTN_TILE_FWD)
    nq = S // tq
    i_tab, j_tab = _causal_pairs(nq, True)

    G = ATTN_HEADS_PER_STEP
    reps = tq // LANES

    def kern(it, jt, q_ref, k_ref, kv_ref, o_ref, lse_ref, m_s, l_s, acc):
        n = pl.program_id(1)
        i, j = it[n], jt[n]

        @pl.when(j == 0)
        def _():
            m_s[...] = jnp.full_like(m_s, -jnp.inf)
            l_s[...] = jnp.zeros_like(l_s)
            acc[...] = jnp.zeros_like(acc)

        def step(diagonal):
            for g in range(G):
                slot = slice(g * MLA_SLOT, (g + 1) * MLA_SLOT)
                s = _dg(q_ref[:, slot], k_ref[:, slot], _NT)
                if diagonal:
                    s = _diag_mask(s, tq)
                m_prev = m_s[g]
                m_new = jnp.maximum(m_prev, jnp.max(s, axis=-1, keepdims=True))
                alpha = jnp.exp(m_prev - m_new)
                p = jnp.exp(s - jnp.tile(m_new, (1, reps)))
                l_s[g] = alpha * l_s[g] + jnp.sum(p, axis=-1, keepdims=True)
                acc[g] = alpha * acc[g] + _dg(p, kv_ref[:, g * MLA_SLOT + HEAD_DIM:(g + 1) * MLA_SLOT], _NN)
                m_s[g] = m_new

        @pl.when(j < i)
        def _():
            step(False)

        @pl.when(j == i)
        def _():
            step(True)
            for g in range(G):
                l = l_s[g]
                lanes = slice(g * HEAD_DIM, (g + 1) * HEAD_DIM)
                o_ref[:, lanes] = acc[g] / l
                lse_ref[:, lanes] = m_s[g] + jnp.log(l)

    out = _sds((S, HEADS * HEAD_DIM))
    oblk = pl.BlockSpec((tq, G * HEAD_DIM), lambda h, n, it, jt: (it[n], h))
    stat = pltpu.VMEM((G, tq, HEAD_DIM), F32)
    return pl.pallas_call(
        kern, name=name,
        grid_spec=pltpu.PrefetchScalarGridSpec(
            num_scalar_prefetch=2, grid=(HEADS // G, i_tab.shape[0]),
            in_specs=[pl.BlockSpec((tq, G * MLA_SLOT), lambda h, n, it, jt: (it[n], h)),
                      pl.BlockSpec((tq, G * MLA_SLOT), lambda h, n, it, jt: (jt[n], h)),
                      pl.BlockSpec((tq, G * MLA_SLOT), lambda h, n, it, jt: (jt[n], h))],
            out_specs=[oblk, oblk], scratch_shapes=[stat, stat, stat]),
        out_shape=[out, out], compiler_params=_params(("parallel", "arbitrary")))(i_tab, j_tab, qr, kr, kv)


def attn_bwd(qr, kr, kv, o, lse, do, name):
    S = qr.shape[0]
    tq = min(S, ATTN_TILE)
    nq = S // tq
    i_tab, j_tab = _causal_pairs(nq, False)

    G = ATTN_HEADS_PER_STEP_BWD

    def kern(it, jt, q_ref, k_ref, kv_ref, o_ref, lse_ref, do_ref, dq_ref, dk_ref, dv_ref, dk_acc, dv_acc):
        n = pl.program_id(1)
        i, j = it[n], jt[n]

        @pl.when(n == 0)
        def _():
            dq_ref[...] = jnp.zeros_like(dq_ref)

        @pl.when(i == j)
        def _():
            dk_acc[...] = jnp.zeros_like(dk_acc)
            dv_acc[...] = jnp.zeros_like(dv_acc)

        def step(diagonal):
            rows = pl.ds(pl.multiple_of(i * tq, tq), tq)
            for g in range(G):
                slot = slice(g * MLA_SLOT, (g + 1) * MLA_SLOT)
                lanes = slice(g * HEAD_DIM, (g + 1) * HEAD_DIM)
                q, k = q_ref[:, slot], k_ref[:, slot]
                s = _dg(q, k, _NT) - jnp.tile(lse_ref[:, lanes], (1, tq // LANES))
                if diagonal:
                    s = _diag_mask(s, tq)
                p = jnp.exp(s)
                d = do_ref[:, lanes]
                delta = jnp.sum(d * o_ref[:, lanes], axis=-1, keepdims=True)
                dv_acc[:, lanes] += _dg(p, d, _TN)
                ds = p * (_dg(d, kv_ref[:, g * MLA_SLOT + HEAD_DIM:(g + 1) * MLA_SLOT], _NT) - delta)
                dk_acc[:, slot] += _dg(ds, q, _TN)
                dq_ref[rows, slot] += _dg(ds, k, _NN)

        @pl.when(i > j)
        def _():
            step(False)

        @pl.when(i == j)
        def _():
            step(True)

        @pl.when(i == nq - 1)
        def _():
            dk_ref[...] = dk_acc[...]
            dv_ref[...] = dv_acc[...]

    qblk = pl.BlockSpec((tq, G * MLA_SLOT), lambda h, n, it, jt: (it[n], h))
    oblk = pl.BlockSpec((tq, G * HEAD_DIM), lambda h, n, it, jt: (it[n], h))
    kblk = pl.BlockSpec((tq, G * MLA_SLOT), lambda h, n, it, jt: (jt[n], h))
    return pl.pallas_call(
        kern, name=name,
        grid_spec=pltpu.PrefetchScalarGridSpec(
            num_scalar_prefetch=2, grid=(HEADS // G, i_tab.shape[0]),
            in_specs=[qblk, kblk, kblk, oblk, oblk, oblk],
            out_specs=[pl.BlockSpec((S, G * MLA_SLOT), lambda h, n, it, jt: (0, h), pipeline_mode=pl.Buffered(1)), kblk,
                       pl.BlockSpec((tq, G * HEAD_DIM), lambda h, n, it, jt: (jt[n], h))],
            scratch_shapes=[pltpu.VMEM((tq, G * MLA_SLOT), F32), pltpu.VMEM((tq, G * HEAD_DIM), F32)]),
        out_shape=[_sds((S, HEADS * MLA_SLOT)), _sds((S, HEADS * MLA_SLOT)), _sds((S, HEADS * HEAD_DIM))],
        compiler_params=_params(("parallel", "arbitrary")))(i_tab, j_tab, qr, kr, kv, o, lse, do)


def loss_head(y, target, name):
    S, Dm = y.shape
    ts = _row_tile(S, Dm)

    def body(y_ref, t_ref):
        e = y_ref[...] - t_ref[...]
        tot = jnp.sum(jnp.sum(e * e, axis=-1, keepdims=True) / Dm, axis=0, keepdims=True)
        return e / Dm, jnp.broadcast_to(0.5 * tot, (SUBLANES, LANES)), e / Dm

    row = pl.BlockSpec((ts, Dm), lambda i: (i, 0))
    return tilecall(body, name, (S // ts,), [(y, row), (target, row)],
                    [(_sds((S, Dm)), row, None),
                     (_sds((SUBLANES, LANES)), pl.BlockSpec((SUBLANES, LANES), lambda i: (0, 0)), lambda: pl.program_id(0) == 0),
                     (_sds((S, Dm), _MXU), row, None)],
                    ("arbitrary",))


_PEER_FLIPS = {
    "chips": ((1, 0, 0), (0, 1, 0), (1, 1, 0)),
    "sibling": ((0, 0, 1),),
    "all": tuple((a, b, c) for a in (0, 1) for b in (0, 1) for c in (0, 1))[1:],
}
_SLOT_WEIGHTS = {"chips": (2, 1, 0), "sibling": (0, 0, 1), "all": (4, 2, 1)}
_HBM = pl.BlockSpec(memory_space=pltpu.HBM)


def _me():
    return lax.axis_index("x"), lax.axis_index("y"), lax.axis_index("c")


def _remote(src, dst, send_sem, recv_sem, peer):
    return pltpu.make_async_remote_copy(src_ref=src, dst_ref=dst, send_sem=send_sem, recv_sem=recv_sem,
                                        device_id=peer, device_id_type=pl.DeviceIdType.MESH)


def exchange(arrs, group, slab_weights, name, keep_own=True):
    flips = _PEER_FLIPS[group]
    wx, wy, wc = _SLOT_WEIGHTS[group]
    n_slots = len(flips) + (1 if keep_own else 0)
    n = len(arrs)

    def slab(ref, a, pos):
        w = slab_weights[a]
        return ref if w is None else ref.at[w[0] * pos[0] + w[1] * pos[1] + w[2] * pos[2]]

    def kern(*refs):
        srcs, outs = refs[:n], refs[n:2 * n]
        send_sems, recv_sems = refs[2 * n:2 * n + 2]
        me = _me()
        my_slot = wx * me[0] + wy * me[1] + wc * me[2]
        copies = []
        if keep_own:
            local_sems = refs[2 * n + 2]
            for a in range(n):
                cp = pltpu.make_async_copy(slab(srcs[a], a, me), outs[a].at[my_slot], local_sems.at[a])
                cp.start()
                copies.append(cp)
        for f, flip in enumerate(flips):
            peer = tuple(m ^ b if b else m for m, b in zip(me, flip))
            for a in range(n):
                cp = _remote(slab(srcs[a], a, peer), outs[a].at[my_slot if keep_own else f],
                             send_sems.at[f, a], recv_sems.at[f, a], peer)
                cp.start()
                copies.append(cp)
        for cp in copies:
            cp.wait()

    out_shape = [_sds((n_slots,) + (a.shape if slab_weights[k] is None else a.shape[1:]), a.dtype) for k, a in enumerate(arrs)]
    sems = [pltpu.SemaphoreType.DMA((len(flips), n)), pltpu.SemaphoreType.DMA((len(flips), n))]
    return pl.pallas_call(
        kern, name=name, in_specs=[_HBM] * n, out_specs=[_HBM] * n, out_shape=out_shape,
        scratch_shapes=sems + ([pltpu.SemaphoreType.DMA((n,))] if keep_own else []))(*arrs)


def _chip_window(ref, kind, size, chip, layers):
    if kind == "rows":
        return ref.at[layers, pl.ds(chip * size, size), :]
    return ref.at[layers, :, pl.ds(pl.multiple_of(chip * size, LANES), size)]


def gather_big(shards, kinds, name):
    n = len(shards)
    fulls = []
    for s, kind in zip(shards, kinds):
        L, r, c = s.shape
        fulls.append(_sds((L, N_CHIPS * r, c) if kind == "rows" else (L, r, N_CHIPS * c), s.dtype))

    def kern(*refs):
        srcs, outs = refs[:n], refs[n:2 * n]
        ici_s, ici_r, relay_s, relay_r, d2d_s, d2d_r = refs[2 * n:]
        x, y, c = _me()
        sibling = (x, y, 1 - c)
        nbrs = ((x ^ 1, y), (x, y ^ 1))
        relay_from = (x ^ (1 - c), y ^ c)
        relay_to = (x ^ c, y ^ (1 - c), c)
        diagonal = (x ^ 1, y ^ 1)

        def window(a, chip):
            L, r, cc = shards[a].shape
            size = r if kinds[a] == "rows" else cc
            return _chip_window(outs[a], kinds[a], size, 2 * chip[0] + chip[1], pl.ds(c * (L // 2), L // 2))

        def direct(a, f):
            L = shards[a].shape[0]
            return _remote(srcs[a].at[pl.ds(c * (L // 2), L // 2)], window(a, (x, y)), ici_s.at[a, f], ici_r.at[a, f],
                           (nbrs[f][0], nbrs[f][1], c))

        def to_sibling(a, k, chip):
            return _remote(window(a, chip), window(a, chip), d2d_s.at[a, k], d2d_r.at[a, k], sibling)

        sends, relays, passed = [], [], []
        for a in range(n):
            for f in range(2):
                cp = direct(a, f)
                cp.start()
                sends.append(cp)
        for a in range(n):
            _remote(window(a, relay_from), window(a, relay_from), ici_s.at[a, c], ici_r.at[a, c], relay_to).wait_recv()
            cp = _remote(window(a, relay_from), window(a, relay_from), relay_s.at[a], relay_r.at[a], relay_to)
            cp.start()
            relays.append(cp)
            passed.append(to_sibling(a, 0, relay_from))
            passed[-1].start()
            _remote(window(a, relay_from), window(a, relay_from), ici_s.at[a, 1 - c], ici_r.at[a, 1 - c], relay_to).wait_recv()
            passed.append(to_sibling(a, 1, (relay_to[0], relay_to[1])))
            passed[-1].start()
        for a in range(n):
            relays[a].wait_recv()
            passed.append(to_sibling(a, 2, diagonal))
            passed[-1].start()
        for cp in sends + relays:
            cp.wait_send()
        for cp in passed:
            cp.wait()

    pair = pltpu.SemaphoreType.DMA((n, 2))
    one = pltpu.SemaphoreType.DMA((n,))
    three = pltpu.SemaphoreType.DMA((n, 3))
    return pl.pallas_call(kern, name=name, in_specs=[_HBM] * n, out_specs=[_HBM] * n, out_shape=fulls,
                          scratch_shapes=[pair, pair, one, one, three, three])(*shards)


def send_other_half(arrs, name):
    n = len(arrs)

    def kern(*refs):
        srcs, outs = refs[:n], refs[n:2 * n]
        send_sems, recv_sems = refs[2 * n:]
        x, y, c = _me()
        copies = []
        for a in range(n):
            hl = arrs[a].shape[0] // 2
            cp = _remote(srcs[a].at[pl.ds((1 - c) * hl, hl)], outs[a], send_sems.at[a], recv_sems.at[a], (x, y, 1 - c))
            cp.start()
            copies.append(cp)
        for cp in copies:
            cp.wait()

    return pl.pallas_call(
        kern, name=name, in_specs=[_HBM] * n, out_specs=[_HBM] * n,
        out_shape=[_sds((a.shape[0] // 2,) + a.shape[1:], a.dtype) for a in arrs],
        scratch_shapes=[pltpu.SemaphoreType.DMA((n,)), pltpu.SemaphoreType.DMA((n,))])(*arrs)


def _axis_neighbours():
    x, y, c = _me()
    return (x, y, c), (x ^ (1 - c), y ^ c), (x ^ c, y ^ (1 - c)), (x ^ 1, y ^ 1)


def reduce_first_axis(arrs, kinds, name):
    n = len(arrs)
    shapes = []
    for a, kind in zip(arrs, kinds):
        l, R, C = a.shape
        shapes.append((l, R // N_CHIPS, C) if kind == "rows" else (l, R, C // N_CHIPS))

    def kern(*refs):
        srcs, outs = refs[:n], refs[n:2 * n]
        send_sems, recv_sems = refs[2 * n:]
        (x, y, c), first, second, diagonal = _axis_neighbours()
        copies = []
        for a in range(n):
            size = shapes[a][1] if kinds[a] == "rows" else shapes[a][2]
            for k, chip in enumerate((first, diagonal)):
                window = _chip_window(srcs[a], kinds[a], size, 2 * chip[0] + chip[1], slice(None))
                cp = _remote(window, outs[a].at[k], send_sems.at[a, k], recv_sems.at[a, k], (first[0], first[1], c))
                cp.start()
                copies.append(cp)
        for cp in copies:
            cp.wait()

    sem = pltpu.SemaphoreType.DMA((n, 2))
    return pl.pallas_call(
        kern, name=name, in_specs=[_HBM] * n, out_specs=[_HBM] * n,
        out_shape=[_sds((2,) + s, a.dtype) for s, a in zip(shapes, arrs)], scratch_shapes=[sem, sem])(*arrs)


def reduce_second_axis(arrs, name):
    n = len(arrs)

    def kern(*refs):
        srcs, outs = refs[:n], refs[n:2 * n]
        send_sems, recv_sems = refs[2 * n:]
        (x, y, c), first, second, diagonal = _axis_neighbours()
        copies = []
        for a in range(n):
            cp = _remote(srcs[a], outs[a], send_sems.at[a], recv_sems.at[a], (second[0], second[1], c))
            cp.start()
            copies.append(cp)
        for cp in copies:
            cp.wait()

    return pl.pallas_call(
        kern, name=name, in_specs=[_HBM] * n, out_specs=[_HBM] * n, out_shape=[_sds(a.shape, a.dtype) for a in arrs],
        scratch_shapes=[pltpu.SemaphoreType.DMA((n,)), pltpu.SemaphoreType.DMA((n,))])(*arrs)


def _stack_tile(r, c):
    for t in (1024, 704, 512, 352, 256, 128, 64, 32, 16):
        if r % t == 0 and t * c * 4 <= 3 * 512 * 1024:
            return t
    return r


def _window_map(kind, r, tr):
    nrt = r // tr
    if kind == "rows":
        return lambda l, i, chip: (l, chip[0] * nrt + i, 0)
    return lambda l, i, chip: (l, i, chip[0])


def place(full, shard, kind, chip, name):
    L, r, c = shard.shape
    tr = _stack_tile(r, c)
    wmap = _window_map(kind, r, tr)
    return tilecall(lambda chip_ref, s_ref, f_ref: (s_ref[...],), name, (L, r // tr),
                    [(shard, pl.BlockSpec((None, tr, c), lambda l, i, chip: (l, i, 0))), (full, pl.BlockSpec(memory_space=pl.ANY))],
                    [(_sds(full.shape, full.dtype), pl.BlockSpec((None, tr, c), lambda l, i, chip: wmap(l, i, chip)), None)],
                    ("parallel", "parallel"), prefetch=(chip,), aliases={1: 0})[0]


def add_cores(g, other, core, name):
    L, R, C = g.shape
    hl = L // 2
    tr = _stack_tile(R, C)
    blk = (None, tr, C)
    return tilecall(lambda core_ref, a_ref, b_ref: (a_ref[...] + b_ref[...],), name, (hl, R // tr),
                    [(g, pl.BlockSpec(blk, lambda l, i, core: (core[0] * hl + l, i, 0))),
                     (other, pl.BlockSpec(blk, lambda l, i, core: (l, i, 0)))],
                    [(_sds((hl, R, C), _MXU), pl.BlockSpec(blk, lambda l, i, core: (l, i, 0)), None)],
                    ("parallel", "parallel"), prefetch=(core,))[0]


def add_first_axis(own, got, kind, chips, name):
    _, l, r, c = got.shape
    tr = _stack_tile(r, c)
    nrt = r // tr

    def window(k):
        if kind == "rows":
            return pl.BlockSpec((None, tr, c), lambda ll, i, *ch: (ll, ch[k][0] * nrt + i, 0))
        return pl.BlockSpec((None, tr, c), lambda ll, i, *ch: (ll, i, ch[k][0]))

    def body(chip0_ref, chip1_ref, own0, own1, got0, got1):
        return own0[...].astype(F32) + got0[...].astype(F32), own1[...].astype(F32) + got1[...].astype(F32)

    plain = pl.BlockSpec((None, tr, c), lambda ll, i, *ch: (ll, i, 0))
    return tilecall(body, name, (l, r // tr),
                    [(own, window(0)), (own, window(1))] +
                    [(got, pl.BlockSpec((None, None, tr, c), lambda ll, i, *ch, k=k: (k, ll, i, 0))) for k in range(2)],
                    [(_sds((l, r, c)), plain, None), (_sds((l, r, c), _MXU), plain, None)],
                    ("parallel", "parallel"), prefetch=tuple(chips))


def add_second_axis(mine, got, name):
    l, r, c = mine.shape
    tr = _stack_tile(r, c)
    blk = pl.BlockSpec((None, tr, c), lambda ll, i: (ll, i, 0))
    return tilecall(lambda a_ref, b_ref: (a_ref[...] + b_ref[...].astype(F32),), name, (l, r // tr),
                    [(mine, blk), (got, blk)], [(_sds((l, r, c)), blk, None)], ("parallel", "parallel"))[0]


def _adam_update(g, w, m, v):
    m_new = ADAM_B1 * m + (1.0 - ADAM_B1) * g
    v_new = ADAM_B2 * v + (1.0 - ADAM_B2) * jnp.square(g)
    m_hat = m_new / (1.0 - ADAM_B1 ** ADAM_STEP)
    v_hat = v_new / (1.0 - ADAM_B2 ** ADAM_STEP)
    delta = -ADAM_LR * (m_hat / (jnp.sqrt(v_hat) + ADAM_EPS) + ADAM_WD * w)
    return g, delta, m_new, v_new


def adamw_stacked(mine, theirs, w, m, v, core, name):
    L, r, c = w.shape
    hl = L // 2
    tr = _stack_tile(r, c)

    def body(core_ref, a_ref, b_ref, w_ref, m_ref, v_ref):
        is_mine = (pl.program_id(0) // hl) == core_ref[0]
        g = jnp.where(is_mine, a_ref[...], b_ref[...])
        return _adam_update(g, w_ref[...], m_ref[...], v_ref[...])

    full = pl.BlockSpec((None, tr, c), lambda l, i, core: (l, i, 0))
    out = (_sds((L, r, c)), full, None)
    return tilecall(body, name, (L, r // tr),
                    [(mine, pl.BlockSpec((None, tr, c), lambda l, i, core: (l % hl, i, 0))),
                     (theirs, pl.BlockSpec((None, None, tr, c), lambda l, i, core: (0, l % hl, i, 0))),
                     (w, full), (m, full), (v, full)],
                    [out, out, out, out], ("parallel", "parallel"), prefetch=(core,))


def _pack(arrs, dtype, row_multiple):
    flat = jnp.concatenate([a.reshape(-1).astype(dtype) for a in arrs])
    rows = -(-flat.shape[0] // LANES)
    rows = -(-rows // row_multiple) * row_multiple
    return jnp.pad(flat, (0, rows * LANES - flat.shape[0])).reshape(rows, LANES)


def _unpack(buf, shapes):
    flat = buf.reshape(-1)
    out, off = [], 0
    for s in shapes:
        n = int(np.prod(s))
        out.append(flat[off:off + n].reshape(s))
        off += n
    return out


def adamw_packed(gparts, w, m, v, name):
    P, R, _ = gparts.shape

    def body(g_ref, w_ref, m_ref, v_ref):
        g = g_ref[0]
        for p in range(1, P):
            g = g + g_ref[p]
        return _adam_update(g, w_ref[...], m_ref[...], v_ref[...])

    whole = pl.BlockSpec((R, LANES), lambda: (0, 0))
    out = (_sds((R, LANES)), whole, None)
    return tilecall(body, name, (), [(gparts, pl.BlockSpec((P, R, LANES), lambda: (0, 0, 0))), (w, whole), (m, whole), (v, whole)],
                    [out, out, out, out], ())


def _residual_out(a, w, x, next_gain):
    if next_gain is None:
        return mm(a, w, "nn", "mm_nn_add", add=x), None
    return mm(a, w, "nn", "mm_nn_add_rms", add=x, rms_gain=next_gain)


def _input_grad(pieces, x, gain, res):
    dh = None
    for d, w in pieces[:-1]:
        dh = mm(d, w, "nt", "mm_nt" if dh is None else "mm_nt_add", add=dh)
    d, w = pieces[-1]
    return mm(d, w, "nt", "mm_nt_norm_bwd", add=dh, norm_bwd=(x, gain, res))


def _ffn_fwd(x, h, next_gain, layer, gain, wts, cw8, cb):
    u = mm(h, win(wts["ffn_w_up"], layer), "nn", "mm_nn", out_dtype=_MXU)
    a = ffn_act_fwd(u, cw8, cb, "ffn_act_fwd")
    y, h_next = _residual_out(a, win(wts["ffn_w_down"], layer), x, next_gain)
    return y, h_next, (x, h, u, a)


def _ffn_bwd(dy, saved, layer, gain, wts, cw8, cb, grads):
    x, h, u, a = saved
    dy, dy16 = dy
    grads["ffn_w_down"] = mm(a, dy16, "tn", "mm_tn_into", into=win(grads["ffn_w_down"], layer))
    da = mm(dy16, win(wts["ffn_w_down"], layer), "nt", "mm_nt", out_dtype=_MXU)
    dyg, dyu, dcw_g, dcw_u, dcb_g, dcb_u = ffn_act_bwd(u, cw8, cb, da, "ffn_act_bwd")
    pieces = []
    for half, du in enumerate(ffn_conv_bwd(dyg, dyu, cw8, "ffn_conv_bwd")):
        cols = dict(col_off=half * D_FF, cols=D_FF)
        grads["ffn_w_up"] = mm(h, du, "tn", "mm_tn_into", into=win(grads["ffn_w_up"], layer, **cols))
        pieces.append((du, win(wts["ffn_w_up"], layer, **cols)))
    dx, d_gain, dx16 = _input_grad(pieces, x, gain, dy)
    return (dx, dx16), dict(gain=d_gain, conv_w=jnp.concatenate([dcw_g[0:3], dcw_u[0:3]], axis=1),
                    conv_b=jnp.concatenate([dcb_g, dcb_u], axis=1))


def _hgrn_w_in(wts, j, k):
    return win(wts["hgrn_w_in"], j, row_off=k * D_MODEL, rows=D_MODEL)


def _hgrn_layer_fwd(x, h, next_gain, j, gain, wts, lb, out_gain, mall):
    z = [mm(h, _hgrn_w_in(wts, j, k), "nn", "mm_nn") for k in range(4)]
    o, on, states = hgrn_fwd(z[0], z[1], z[2], z[3], lb, out_gain, mall, "hgrn_fwd")
    y, h_next = _residual_out(on, win(wts["hgrn_w_out"], j), x, next_gain)
    return y, h_next, (x, h, z, o, states, on)


def _hgrn_layer_bwd(dy, saved, j, gain, wts, lb, out_gain, mall, grads):
    x, h, z, o, states, on = saved
    dy, dy16 = dy
    grads["hgrn_w_out"] = mm(on, dy16, "tn", "mm_tn_into", into=win(grads["hgrn_w_out"], j))
    don = mm(dy16, win(wts["hgrn_w_out"], j), "nt", "mm_nt")
    dzq, dzf, dzi, dzg, dlb, d_out_gain = hgrn_bwd(z[0], z[1], z[2], z[3], lb, out_gain, mall, states, o, don, "hgrn_bwd")
    dz = [dzq, dzf, dzi, dzg]
    for k, d in enumerate(dz):
        grads["hgrn_w_in"] = mm(h, d, "tn", "mm_tn_into", into=win(grads["hgrn_w_in"], j, row_off=k * D_MODEL, rows=D_MODEL))
    dx, d_gain, dx16 = _input_grad([(d, _hgrn_w_in(wts, j, k)) for k, d in enumerate(dz)], x, gain, dy)
    return (dx, dx16), dict(gain=d_gain, lb=dlb, out_gain=d_out_gain)


_MLA_IN_WINDOWS = ((0, MLA_LORA), (MLA_LORA, MLA_LORA), (2 * MLA_LORA, HEAD_DIM))


def _mla_layer_fwd(x, h, next_gain, j, gain, wts, qa_gain, kva_gain, qn, kn, cos_t, sin_t):
    (c0, n), (c1, n1), (c2, n2) = _MLA_IN_WINDOWS
    cq, cqn = mm(h, win(wts["mla_w_in"], j, col_off=c0, cols=n), "nn", "mm_nn_rms", rms_gain=qa_gain)
    ckv, ckvn = mm(h, win(wts["mla_w_in"], j, col_off=c1, cols=n1), "nn", "mm_nn_rms", rms_gain=kva_gain)
    kr = mm(h, win(wts["mla_w_in"], j, col_off=c2, cols=n2), "nn", "mm_nn")
    qslots = mm(cqn, win(wts["mla_w_q_up"], j), "nn", "mm_nn")
    kv = mm(ckvn, win(wts["mla_w_kv_up"], j), "nn", "mm_nn")
    qr, krot = qk_fwd(qslots, kv, kr, qn, kn, cos_t, sin_t, "qk_fwd")
    o, lse = attn_fwd(qr, krot, kv, "attn_fwd")
    y, h_next = _residual_out(o, win(wts["mla_w_out"], j), x, next_gain)
    return y, h_next, (x, h, cq, ckv, kr, cqn, ckvn, qslots, kv, qr, krot, o, lse)


def _mla_layer_bwd(dy, saved, j, gain, wts, qa_gain, kva_gain, qn, kn, cos_t, sin_t, grads):
    x, h, cq, ckv, kr, cqn, ckvn, qslots, kv, qr, krot, o, lse = saved
    dy, dy16 = dy
    grads["mla_w_out"] = mm(o, dy16, "tn", "mm_tn_into", into=win(grads["mla_w_out"], j))
    do = mm(dy16, win(wts["mla_w_out"], j), "nt", "mm_nt")
    dq, dk, dv = attn_bwd(qr, krot, kv, o, lse, do, "attn_bwd")
    dqslots, dkv, dkr, d_qn, d_kn = qk_bwd(qslots, kv, kr, qn, kn, cos_t, sin_t, dq, dk, dv, "qk_bwd")
    grads["mla_w_q_up"] = mm(cqn, dqslots, "tn", "mm_tn_into", into=win(grads["mla_w_q_up"], j))
    dcq, d_qa = _input_grad([(dqslots, win(wts["mla_w_q_up"], j))], cq, qa_gain, None)
    grads["mla_w_kv_up"] = mm(ckvn, dkv, "tn", "mm_tn_into", into=win(grads["mla_w_kv_up"], j))
    dckv, d_kva = _input_grad([(dkv, win(wts["mla_w_kv_up"], j))], ckv, kva_gain, None)
    pieces = []
    for d, (c0, n) in zip((dcq, dckv, dkr), _MLA_IN_WINDOWS):
        grads["mla_w_in"] = mm(h, d, "tn", "mm_tn_into", into=win(grads["mla_w_in"], j, col_off=c0, cols=n))
        pieces.append((d, win(wts["mla_w_in"], j, col_off=c0, cols=n)))
    dx, d_gain, dx16 = _input_grad(pieces, x, gain, dy)
    return (dx, dx16), dict(gain=d_gain, qa=d_qa, kva=d_kva, qn=d_qn, kn=d_kn)


BIG = (("hgrn_w_in", "rows"), ("hgrn_w_out", "rows"), ("mla_w_in", "rows"), ("mla_w_q_up", "cols"),
       ("mla_w_kv_up", "cols"), ("mla_w_out", "rows"), ("ffn_w_up", "cols"), ("ffn_w_down", "rows"))
SMALL_SHARDED = (("mla_q_a_norm", 1), ("mla_kv_a_norm", 1), ("ffn_conv_w", 2))
REPLICATED = ("norm_mix", "norm_ffn", "hgrn_lower_bounds", "hgrn_out_norm", "mla_q_norm", "mla_k_norm", "ffn_conv_b")
WEIGHTS = ("norm_mix", "norm_ffn", "hgrn_w_in", "hgrn_lower_bounds", "hgrn_out_norm", "hgrn_w_out", "mla_w_in",
           "mla_q_a_norm", "mla_w_q_up", "mla_kv_a_norm", "mla_w_kv_up", "mla_q_norm", "mla_k_norm", "mla_w_out",
           "ffn_w_up", "ffn_conv_w", "ffn_conv_b", "ffn_w_down")


def _pad_cols(a, width):
    return jnp.pad(a, [(0, 0)] * (a.ndim - 1) + [(0, width - a.shape[-1])])


def _head_slots(w):
    lead, n = w.shape[:-1], w.shape[-1] // MLA_QK
    return _pad_cols(w.reshape(lead + (n, MLA_QK)), MLA_SLOT).reshape(lead + (n * MLA_SLOT,))


def _head_unslots(w):
    lead, n = w.shape[:-1], w.shape[-1] // MLA_SLOT
    return w.reshape(lead + (n, MLA_SLOT))[..., :MLA_QK].reshape(lead + (n * MLA_QK,))


def _to_stack_layout(name, a):
    if name == "hgrn_w_in":
        return a
    if name == "mla_w_in":
        return _pad_cols(a, MLA_IN_COLS)
    if name == "mla_w_q_up":
        return _head_slots(a)
    return a


def _from_stack_layout(name, a):
    if name == "mla_w_in":
        return a[..., :2 * MLA_LORA + MLA_ROPE]
    if name == "mla_w_q_up":
        return _head_unslots(a)
    return a


def _rope_tables(positions):
    inv_freq = ROPE_THETA ** (-jnp.arange(0, MLA_ROPE, 2, dtype=F32) / MLA_ROPE)
    ang = positions.astype(F32)[:, None] * inv_freq
    cos, sin = jnp.cos(ang), jnp.sin(ang)
    S = positions.shape[0]
    ones, zeros = jnp.ones((S, MLA_NOPE), F32), jnp.zeros((S, MLA_SLOT - MLA_QK), F32)
    return (jnp.concatenate([ones, cos, cos, zeros], axis=1),
            jnp.concatenate([jnp.zeros((S, MLA_NOPE), F32), -sin, sin, zeros], axis=1))


def kernel(x, positions, norm_mix, norm_ffn, hgrn_w_in, hgrn_lower_bounds, hgrn_out_norm, hgrn_w_out, mla_w_in, mla_q_a_norm, mla_w_q_up, mla_kv_a_norm, mla_w_kv_up, mla_q_norm, mla_k_norm, mla_w_out, ffn_w_up, ffn_conv_w, ffn_conv_b, ffn_w_down, loss_target, m_norm_mix, m_norm_ffn, m_hgrn_w_in, m_hgrn_lower_bounds, m_hgrn_out_norm, m_hgrn_w_out, m_mla_w_in, m_mla_q_a_norm, m_mla_w_q_up, m_mla_kv_a_norm, m_mla_w_kv_up, m_mla_q_norm, m_mla_k_norm, m_mla_w_out, m_ffn_w_up, m_ffn_conv_w, m_ffn_conv_b, m_ffn_w_down, v_norm_mix, v_norm_ffn, v_hgrn_w_in, v_hgrn_lower_bounds, v_hgrn_out_norm, v_hgrn_w_out, v_mla_w_in, v_mla_q_a_norm, v_mla_w_q_up, v_mla_kv_a_norm, v_mla_w_kv_up, v_mla_q_norm, v_mla_k_norm, v_mla_w_out, v_ffn_w_up, v_ffn_conv_w, v_ffn_conv_b, v_ffn_w_down):
    args = dict(locals())
    w = {n: args[n] for n in WEIGHTS}
    m = {n: args["m_" + n] for n in WEIGHTS}
    v = {n: args["v_" + n] for n in WEIGHTS}
    depth = norm_mix.shape[0]
    x0 = x[0]
    S = x0.shape[0]
    chip = (2 * lax.axis_index("x") + lax.axis_index("y")).astype(jnp.int32).reshape(1)
    core = lax.axis_index("c").astype(jnp.int32).reshape(1)
    big_names = [n for n, _ in BIG]
    kinds = [k for _, k in BIG]
    small_names = [n for n, _ in SMALL_SHARDED]
    small_axis = dict(SMALL_SHARDED)

    local = {n: _to_stack_layout(n, w[n]) for n in big_names}
    gathered = gather_big([local[n].astype(_MXU) for n in big_names], kinds, "gather_weights")
    wts = {n: place(g, local[n], k, chip, "place") for n, k, g in zip(big_names, kinds, gathered)}
    (got_small,) = exchange([_pack([w[n] for n in small_names], F32, SUBLANES)], "chips", [None], "gather_small")
    per_chip = [_unpack(got_small[p], [w[n].shape for n in small_names]) for p in range(N_CHIPS)]
    small = {n: jnp.concatenate([per_chip[p][k] for p in range(N_CHIPS)], axis=small_axis[n]) for k, n in enumerate(small_names)}

    cos_t, sin_t = _rope_tables(positions[0])
    lbs = lower_bound_fwd(hgrn_lower_bounds, "lower_bound_fwd")
    mall = jnp.asarray(_hgrn_sum_matrix(min(HGRN_CHUNK, S)), _MXU)
    qn = _pad_cols(mla_q_norm, MLA_SLOT)
    kn = _pad_cols(mla_k_norm, MLA_SLOT)
    cw8 = jnp.pad(small["ffn_conv_w"], ((0, 0), (0, SUBLANES - 3), (0, 0)))

    def mixer_args(layer):
        j = layer // 2
        if layer % 2 == 0:
            return (j, norm_mix[layer:layer + 1], wts, lbs[j:j + 1], hgrn_out_norm[j:j + 1], mall)
        return (j, norm_mix[layer:layer + 1], wts, small["mla_q_a_norm"][j:j + 1], small["mla_kv_a_norm"][j:j + 1],
                qn[j:j + 1], kn[j:j + 1], cos_t, sin_t)

    def ffn_args(layer):
        return (layer, norm_ffn[layer:layer + 1], wts, cw8[layer], ffn_conv_b[layer:layer + 1])

    xc, h = x0, rms_fwd(x0, norm_mix[0:1], "rms_fwd")
    saved = []
    for layer in range(depth):
        fwd = _hgrn_layer_fwd if layer % 2 == 0 else _mla_layer_fwd
        xc, h, s_mix = fwd(xc, h, norm_ffn[layer:layer + 1], *mixer_args(layer))
        xc, h, s_ffn = _ffn_fwd(xc, h, norm_mix[layer + 1:layer + 2] if layer + 1 < depth else None, *ffn_args(layer))
        saved.append((s_mix, s_ffn))

    dh32, loss_blk, dh16 = loss_head(xc, loss_target[0], "loss_head")
    dh = (dh32, dh16)
    loss = lax.psum(loss_blk[0, 0], MESH_AXES)

    grads = {n: lax.empty(g.shape, F32) for n, g in zip(big_names, gathered)}
    g_mix, g_ffn = [None] * depth, [None] * depth
    for layer in reversed(range(depth)):
        s_mix, s_ffn = saved[layer]
        dh, g_ffn[layer] = _ffn_bwd(dh, s_ffn, *ffn_args(layer), grads)
        bwd = _hgrn_layer_bwd if layer % 2 == 0 else _mla_layer_bwd
        dh, g_mix[layer] = bwd(dh, s_mix, *mixer_args(layer), grads)
    hg = [g_mix[l] for l in range(0, depth, 2)]
    mg = [g_mix[l] for l in range(1, depth, 2)]
    d_p = lower_bound_bwd(hgrn_lower_bounds, hg[1]["lb"], "lower_bound_bwd")

    g_list = [grads[n] for n in big_names]
    from_core = send_other_half(g_list, "reduce_cores_in")
    chip_sums = [add_cores(g, o, core, "add_cores") for g, o in zip(g_list, from_core)]
    cx, cy, cc = lax.axis_index("x"), lax.axis_index("y"), lax.axis_index("c")
    second = 2 * (cx + cc - 2 * cx * cc) + (cy + (1 - cc) - 2 * cy * (1 - cc))
    chips = (chip, second.astype(jnp.int32).reshape(1))
    from_first = reduce_first_axis(chip_sums, kinds, "reduce_axis_1")
    partial = [add_first_axis(own, got, k, chips, "add_axis_1") for own, got, k in zip(chip_sums, from_first, kinds)]
    from_second = reduce_second_axis([p[1] for p in partial], "reduce_axis_2")
    reduced = [add_second_axis(p[0], got, "add_axis_2") for p, got in zip(partial, from_second)]
    other_half = exchange(reduced, "sibling", [None] * len(reduced), "reduce_cores_out", keep_own=False)
    big_out = {}
    for n, mine, theirs in zip(big_names, reduced, other_half):
        outs = adamw_stacked(mine, theirs, local[n], _to_stack_layout(n, m[n]), _to_stack_layout(n, v[n]), core, "adamw")
        big_out[n] = [_from_stack_layout(n, o) for o in outs]

    small_grads = {
        "norm_mix": jnp.concatenate([g["gain"] for g in g_mix], axis=0),
        "norm_ffn": jnp.concatenate([g["gain"] for g in g_ffn], axis=0),
        "hgrn_lower_bounds": d_p[0:2],
        "hgrn_out_norm": jnp.concatenate([g["out_gain"] for g in hg], axis=0),
        "mla_q_a_norm": jnp.concatenate([g["qa"] for g in mg], axis=0),
        "mla_kv_a_norm": jnp.concatenate([g["kva"] for g in mg], axis=0),
        "mla_q_norm": jnp.concatenate([g["qn"][:, :MLA_QK] for g in mg], axis=0),
        "mla_k_norm": jnp.concatenate([g["kn"][:, :MLA_QK] for g in mg], axis=0),
        "ffn_conv_w": jnp.stack([g["conv_w"] for g in g_ffn]),
        "ffn_conv_b": jnp.concatenate([g["conv_b"] for g in g_ffn], axis=0),
    }

    def chip_part(n, p):
        size = w[n].shape[small_axis[n]]
        return lax.slice_in_dim(small_grads[n], p * size, (p + 1) * size, axis=small_axis[n])

    to_chips = jnp.stack([_pack([chip_part(n, p) for n in small_names], F32, SUBLANES) for p in range(N_CHIPS)])
    rep_g, shard_g = exchange([_pack([small_grads[n] for n in REPLICATED], F32, SUBLANES), to_chips], "all",
                              [None, (2, 1, 0)], "reduce_small")
    small_out = {}
    for names, gparts in ((REPLICATED, rep_g), (small_names, shard_g)):
        packed = adamw_packed(gparts, *[_pack([t[n] for n in names], F32, SUBLANES) for t in (w, m, v)], "adamw_small")
        unpacked = [_unpack(buf, [w[n].shape for n in names]) for buf in packed]
        for k, n in enumerate(names):
            small_out[n] = [u[k] for u in unpacked]

    result = [loss, dh[0][None]]
    for k in range(4):
        result += [(big_out[n] if n in big_out else small_out[n])[k] for n in WEIGHTS]
    return tuple(result)
```
